```python
import math
import jax, jax.numpy as jnp
from jax import lax
import numpy as np

D_MODEL = 1024
BATCH = 8
SEQ = 4096
DEPTH = 4

N_MIXERS = 2
N_ATTN_LAYERS = (DEPTH + 1) // 2
N_POOL_LAYERS = DEPTH // 2
HEAD_DIM = 64
N_HEADS = D_MODEL // HEAD_DIM
N_KV_HEADS = 4
GROUP = N_HEADS // N_KV_HEADS
Q_WIDTH = N_HEADS * HEAD_DIM
KV_WIDTH = N_KV_HEADS * HEAD_DIM
ATTN_GATE_WIDTH = Q_WIDTH
ATTN_IN_WIDTH = Q_WIDTH + 2 * KV_WIDTH + ATTN_GATE_WIDTH
WINDOW = 128
BLOCK = 128
ROT_DIM = HEAD_DIM // 4
ROPE_THETA = 500000.0
POOL_WINDOWS = (2, 4, 8, 16)
N_POOL_GROUPS = len(POOL_WINDOWS)
POOL_WIDTH = D_MODEL
POOL_GROUP_DIM = POOL_WIDTH // N_POOL_GROUPS
POOL_IN_WIDTH = 2 * POOL_WIDTH
NORM_EPS = 1e-6

kernel_name = "hybrid_swa_sink_multiscale_pool_adaln"


def rms_norm(x, g):
    xf = x.astype(jnp.float32)
    y = xf * lax.rsqrt(jnp.mean(xf * xf, axis=-1, keepdims=True) + NORM_EPS)
    return (y * g.astype(jnp.float32)).astype(x.dtype)


def adaln_params(c, w, b):
    mod = jax.nn.silu(c) @ w + b
    shift, scale, gate = jnp.split(mod, 3, axis=-1)
    return shift[:, None, :], scale[:, None, :], gate[:, None, :]


def partial_rope(x, positions):
    half = ROT_DIM // 2
    inv_freq = ROPE_THETA ** (-jnp.arange(half, dtype=jnp.float32) * 2.0 / ROT_DIM)
    ang = positions.astype(jnp.float32)[..., None] * inv_freq
    cos = jnp.cos(ang)[:, :, None, :]
    sin = jnp.sin(ang)[:, :, None, :]
    xf = x.astype(jnp.float32)
    x1, x2 = xf[..., :half], xf[..., half:ROT_DIM]
    rot = jnp.concatenate([x1 * cos - x2 * sin, x2 * cos + x1 * sin], axis=-1)
    return jnp.concatenate([rot.astype(x.dtype), x[..., ROT_DIM:]], axis=-1)


def banded_sink_attention(q, k, v, sinks):
    B, S = q.shape[0], q.shape[1]
    nb = S // BLOCK
    qb = q.reshape(B, nb, BLOCK, N_KV_HEADS, GROUP, HEAD_DIM)

    def band(t):
        tp = jnp.pad(t, ((0, 0), (BLOCK, 0), (0, 0), (0, 0)))
        tb = tp.reshape(B, nb + 1, BLOCK, N_KV_HEADS, HEAD_DIM)
        return jnp.concatenate([tb[:, :-1], tb[:, 1:]], axis=2)

    kb, vb = band(k), band(v)
    scores = jnp.einsum('bnqkgd,bnskd->bnkgqs', qb, kb).astype(jnp.float32) * (HEAD_DIM ** -0.5)
    qi = jnp.arange(BLOCK)[:, None] + BLOCK
    ki = jnp.arange(2 * BLOCK)[None, :]
    diff = qi - ki
    band_ok = (diff >= 0) & (diff < WINDOW)
    key_abs = jnp.arange(nb)[:, None, None] * BLOCK + ki[None] - BLOCK
    mask = band_ok[None] & (key_abs >= 0)
    scores = jnp.where(mask[None, :, None, None], scores, -jnp.inf)
    sink = sinks.astype(jnp.float32).reshape(N_KV_HEADS, GROUP)[None, None, :, :, None, None]
    m = jnp.maximum(jnp.max(scores, axis=-1, keepdims=True), sink)
    p = jnp.exp(scores - m)
    denom = jnp.sum(p, axis=-1, keepdims=True) + jnp.exp(sink - m)
    probs = (p / denom).astype(v.dtype)
    o = jnp.einsum('bnkgqs,bnskd->bnqkgd', probs, vb)
    return o.reshape(B, S, N_HEADS * HEAD_DIM)


def attention_branch(h, positions, w_in, q_norm_g, k_norm_g, sinks, w_out):
    B, S = h.shape[0], h.shape[1]
    proj = h @ w_in
    q, k, v, g = jnp.split(proj, [Q_WIDTH, Q_WIDTH + KV_WIDTH, Q_WIDTH + 2 * KV_WIDTH], axis=-1)
    q = q.reshape(B, S, N_HEADS, HEAD_DIM)
    k = k.reshape(B, S, N_KV_HEADS, HEAD_DIM)
    v = v.reshape(B, S, N_KV_HEADS, HEAD_DIM)
    q = partial_rope(rms_norm(q, q_norm_g), positions)
    k = partial_rope(rms_norm(k, k_norm_g), positions)
    o = banded_sink_attention(q, k, v, sinks)
    return (o * jax.nn.silu(g)) @ w_out


def multiscale_pool(v):
    B, S = v.shape[0], v.shape[1]
    vf = v.astype(jnp.float32).reshape(B, S, N_POOL_GROUPS, POOL_GROUP_DIM)
    cs = jnp.concatenate([jnp.zeros((B, 1, N_POOL_GROUPS, POOL_GROUP_DIM), jnp.float32),
                          jnp.cumsum(vf, axis=1)], axis=1)
    t = jnp.arange(S)
    outs = []
    for gi, w in enumerate(POOL_WINDOWS):
        csg = cs[:, :, gi]
        upper = csg[:, 1:]
        lower = jnp.pad(csg[:, :S - w + 1], ((0, 0), (w - 1, 0), (0, 0)))
        count = jnp.minimum(t + 1, w).astype(jnp.float32)[None, :, None]
        outs.append((upper - lower) / count - vf[:, :, gi])
    return jnp.stack(outs, axis=2).astype(v.dtype)


def pool_branch(h, w_in, w_group, scale, w_out):
    B, S = h.shape[0], h.shape[1]
    proj = h @ w_in
    v, g = jnp.split(proj, [POOL_WIDTH], axis=-1)
    pooled = multiscale_pool(v)
    mixed = jnp.einsum('bsgc,gcd->bsgd', pooled, w_group).reshape(B, S, POOL_WIDTH)
    mixed = mixed * scale
    return (mixed * jax.nn.silu(g)) @ w_out


def _fwd_setup_inputs(seed: int = 0) -> dict:
    key = jax.random.key(seed)
    ks = jax.random.split(key, 16)
    f32 = jnp.float32
    x = jax.random.normal(ks[0], (BATCH, SEQ, D_MODEL), f32)
    c = jax.random.normal(ks[1], (BATCH, D_MODEL), f32)
    offset = jax.random.randint(ks[2], (BATCH, 1), 0, 1024, dtype=jnp.int32)
    positions = offset + jnp.arange(SEQ, dtype=jnp.int32)[None, :]
    ada_w = jax.random.normal(ks[3], (DEPTH, D_MODEL, 3 * D_MODEL), f32) * (0.5 * D_MODEL ** -0.5)
    ada_b = jax.random.normal(ks[4], (DEPTH, 3 * D_MODEL), f32) * 0.02
    norm_g = 1.0 + 0.02 * jax.random.normal(ks[5], (DEPTH, D_MODEL), f32)
    attn_w_in = jax.random.normal(ks[6], (N_ATTN_LAYERS, D_MODEL, ATTN_IN_WIDTH), f32) * D_MODEL ** -0.5
    attn_q_norm = 1.0 + 0.02 * jax.random.normal(ks[7], (N_ATTN_LAYERS, HEAD_DIM), f32)
    attn_k_norm = 1.0 + 0.02 * jax.random.normal(ks[8], (N_ATTN_LAYERS, HEAD_DIM), f32)
    attn_sinks = 0.5 * jax.random.normal(ks[9], (N_ATTN_LAYERS, N_HEADS), f32)
    attn_w_out = jax.random.normal(ks[10], (N_ATTN_LAYERS, Q_WIDTH, D_MODEL), f32) * Q_WIDTH ** -0.5
    pool_w_in = jax.random.normal(ks[11], (N_POOL_LAYERS, D_MODEL, POOL_IN_WIDTH), f32) * D_MODEL ** -0.5
    pool_w_group = jax.random.normal(ks[12], (N_POOL_LAYERS, N_POOL_GROUPS, POOL_GROUP_DIM, POOL_GROUP_DIM), f32) * POOL_GROUP_DIM ** -0.5
    pool_scale = 1.0 + 0.1 * jax.random.normal(ks[13], (N_POOL_LAYERS, POOL_WIDTH), f32)
    pool_w_out = jax.random.normal(ks[14], (N_POOL_LAYERS, POOL_WIDTH, D_MODEL), f32) * POOL_WIDTH ** -0.5
    return {"x": x, "c": c, "positions": positions, "ada_w": ada_w, "ada_b": ada_b,
            "norm_g": norm_g, "attn_w_in": attn_w_in, "attn_q_norm": attn_q_norm,
            "attn_k_norm": attn_k_norm, "attn_sinks": attn_sinks, "attn_w_out": attn_w_out,
            "pool_w_in": pool_w_in, "pool_w_group": pool_w_group, "pool_scale": pool_scale,
            "pool_w_out": pool_w_out}


def _fwd_reference(x, c, positions, ada_w, ada_b, norm_g, attn_w_in, attn_q_norm, attn_k_norm,
              attn_sinks, attn_w_out, pool_w_in, pool_w_group, pool_scale, pool_w_out):
    for i in range(DEPTH):
        shift, scale, gate = adaln_params(c, ada_w[i], ada_b[i])
        h = rms_norm(x, norm_g[i]) * (1.0 + scale) + shift
        j = i // N_MIXERS
        if i % N_MIXERS == 0:
            branch = attention_branch(h, positions, attn_w_in[j], attn_q_norm[j], attn_k_norm[j],
                                      attn_sinks[j], attn_w_out[j])
        else:
            branch = pool_branch(h, pool_w_in[j], pool_w_group[j], pool_scale[j], pool_w_out[j])
        x = x + gate * branch
    return x


import jax as _jax
import jax.numpy as _jnp

TWIN_FORMAT = 'train_step'
FWD_PARAMS = ['x', 'c', 'positions', 'ada_w', 'ada_b', 'norm_g', 'attn_w_in', 'attn_q_norm', 'attn_k_norm', 'attn_sinks', 'attn_w_out', 'pool_w_in', 'pool_w_group', 'pool_scale', 'pool_w_out']
TWIN_WEIGHTS = ['ada_w', 'ada_b', 'norm_g', 'attn_w_in', 'attn_q_norm', 'attn_k_norm', 'attn_sinks', 'attn_w_out', 'pool_w_in', 'pool_w_group', 'pool_scale', 'pool_w_out']
TWIN_DIFF_INPUT = 'x'
TWIN_INPUTS = ['x', 'c', 'positions', 'ada_w', 'ada_b', 'norm_g', 'attn_w_in', 'attn_q_norm', 'attn_k_norm', 'attn_sinks', 'attn_w_out', 'pool_w_in', 'pool_w_group', 'pool_scale', 'pool_w_out', 'loss_target', 'm_ada_w', 'm_ada_b', 'm_norm_g', 'm_attn_w_in', 'm_attn_q_norm', 'm_attn_k_norm', 'm_attn_sinks', 'm_attn_w_out', 'm_pool_w_in', 'm_pool_w_group', 'm_pool_scale', 'm_pool_w_out', 'v_ada_w', 'v_ada_b', 'v_norm_g', 'v_attn_w_in', 'v_attn_q_norm', 'v_attn_k_norm', 'v_attn_sinks', 'v_attn_w_out', 'v_pool_w_in', 'v_pool_w_group', 'v_pool_scale', 'v_pool_w_out']
TWIN_OUTPUTS = ['loss', 'grad_x', 'grad_ada_w', 'grad_ada_b', 'grad_norm_g', 'grad_attn_w_in', 'grad_attn_q_norm', 'grad_attn_k_norm', 'grad_attn_sinks', 'grad_attn_w_out', 'grad_pool_w_in', 'grad_pool_w_group', 'grad_pool_scale', 'grad_pool_w_out', 'delta_ada_w', 'delta_ada_b', 'delta_norm_g', 'delta_attn_w_in', 'delta_attn_q_norm', 'delta_attn_k_norm', 'delta_attn_sinks', 'delta_attn_w_out', 'delta_pool_w_in', 'delta_pool_w_group', 'delta_pool_scale', 'delta_pool_w_out', 'new_m_ada_w', 'new_m_ada_b', 'new_m_norm_g', 'new_m_attn_w_in', 'new_m_attn_q_norm', 'new_m_attn_k_norm', 'new_m_attn_sinks', 'new_m_attn_w_out', 'new_m_pool_w_in', 'new_m_pool_w_group', 'new_m_pool_scale', 'new_m_pool_w_out', 'new_v_ada_w', 'new_v_ada_b', 'new_v_norm_g', 'new_v_attn_w_in', 'new_v_attn_q_norm', 'new_v_attn_k_norm', 'new_v_attn_sinks', 'new_v_attn_w_out', 'new_v_pool_w_in', 'new_v_pool_w_group', 'new_v_pool_scale', 'new_v_pool_w_out']
TWIN_LEAF_KINDS = {'loss': 'loss', 'grad_x': 'grad_x', 'grad_ada_w': 'grad_w', 'grad_ada_b': 'grad_w', 'grad_norm_g': 'grad_w', 'grad_attn_w_in': 'grad_w', 'grad_attn_q_norm': 'grad_w', 'grad_attn_k_norm': 'grad_w', 'grad_attn_sinks': 'grad_w', 'grad_attn_w_out': 'grad_w', 'grad_pool_w_in': 'grad_w', 'grad_pool_w_group': 'grad_w', 'grad_pool_scale': 'grad_w', 'grad_pool_w_out': 'grad_w', 'delta_ada_w': 'delta_w', 'delta_ada_b': 'delta_w', 'delta_norm_g': 'delta_w', 'delta_attn_w_in': 'delta_w', 'delta_attn_q_norm': 'delta_w', 'delta_attn_k_norm': 'delta_w', 'delta_attn_sinks': 'delta_w', 'delta_attn_w_out': 'delta_w', 'delta_pool_w_in': 'delta_w', 'delta_pool_w_group': 'delta_w', 'delta_pool_scale': 'delta_w', 'delta_pool_w_out': 'delta_w', 'new_m_ada_w': 'new_m', 'new_m_ada_b': 'new_m', 'new_m_norm_g': 'new_m', 'new_m_attn_w_in': 'new_m', 'new_m_attn_q_norm': 'new_m', 'new_m_attn_k_norm': 'new_m', 'new_m_attn_sinks': 'new_m', 'new_m_attn_w_out': 'new_m', 'new_m_pool_w_in': 'new_m', 'new_m_pool_w_group': 'new_m', 'new_m_pool_scale': 'new_m', 'new_m_pool_w_out': 'new_m', 'new_v_ada_w': 'new_v', 'new_v_ada_b': 'new_v', 'new_v_norm_g': 'new_v', 'new_v_attn_w_in': 'new_v', 'new_v_attn_q_norm': 'new_v', 'new_v_attn_k_norm': 'new_v', 'new_v_attn_sinks': 'new_v', 'new_v_attn_w_out': 'new_v', 'new_v_pool_w_in': 'new_v', 'new_v_pool_w_group': 'new_v', 'new_v_pool_scale': 'new_v', 'new_v_pool_w_out': 'new_v'}


def _forward(args):
    return _fwd_reference(*[args[k] for k in FWD_PARAMS])


def _output_shape():
    def fwd():
        inp = _fwd_setup_inputs(0)
        return _fwd_reference(*[inp[k] for k in FWD_PARAMS])
    out = _jax.eval_shape(fwd)
    return out.shape, out.dtype

N_MICROBATCH = 1
ADAM_LR = 0.001
ADAM_B1 = 0.9
ADAM_B2 = 0.999
ADAM_EPS = 1e-08
ADAM_WD = 0.01
ADAM_STEP = 10
PER_EXAMPLE_BATCH_AXIS = {'x': 0, 'c': 0, 'positions': 0, 'loss_target': 0}
SHARED_INPUTS = []
_WEIGHT_DTYPES = {'ada_w': _jnp.float32, 'ada_b': _jnp.float32, 'norm_g': _jnp.float32, 'attn_w_in': _jnp.float32, 'attn_q_norm': _jnp.float32, 'attn_k_norm': _jnp.float32, 'attn_sinks': _jnp.float32, 'attn_w_out': _jnp.float32, 'pool_w_in': _jnp.float32, 'pool_w_group': _jnp.float32, 'pool_scale': _jnp.float32, 'pool_w_out': _jnp.float32}
MOMENT_SCALE = {'ada_w': 4.334439e-01, 'ada_b': 1.195270e+00, 'norm_g': 1.695707e+00, 'attn_w_in': 4.968121e-02, 'attn_q_norm': 2.592405e-01, 'attn_k_norm': 2.600357e-01, 'attn_sinks': 7.488072e-02, 'attn_w_out': 2.536629e-02, 'pool_w_in': 9.775014e-02, 'pool_w_group': 8.399343e-02, 'pool_scale': 1.155608e+00, 'pool_w_out': 6.203969e-02}


def _to_microbatches(a, axis):
    t = _jnp.moveaxis(a, axis, 0)
    t = t.reshape((N_MICROBATCH, t.shape[0] // N_MICROBATCH) + t.shape[1:])
    return _jnp.moveaxis(t, 1, axis + 1)


def setup_inputs(seed: int = 0) -> dict:
    inp = _fwd_setup_inputs(seed)
    key = _jax.random.fold_in(_jax.random.key(seed), 7919)
    shape, _ = _output_shape()
    out = dict(inp)
    out["loss_target"] = _jax.random.normal(_jax.random.fold_in(key, 0), shape, _jnp.float32)
    for i, name in enumerate(TWIN_WEIGHTS):
        w = inp[name].astype(_jnp.float32)
        if MOMENT_SCALE is None:
            s = _jnp.sqrt(_jnp.mean(_jnp.square(w)) + 1e-30)
        else:
            s = MOMENT_SCALE[name]
        km, kv = _jax.random.split(_jax.random.fold_in(key, i + 1))
        out[name] = w
        out["m_" + name] = s * _jax.random.normal(km, w.shape, _jnp.float32)
        out["v_" + name] = (s * s) * _jax.random.uniform(kv, w.shape, _jnp.float32, 0.5, 1.5)
    if N_MICROBATCH > 1:
        for name, axis in PER_EXAMPLE_BATCH_AXIS.items():
            out[name] = _to_microbatches(out[name], axis)
    return {'x': out['x'], 'c': out['c'], 'positions': out['positions'], 'ada_w': out['ada_w'], 'ada_b': out['ada_b'], 'norm_g': out['norm_g'], 'attn_w_in': out['attn_w_in'], 'attn_q_norm': out['attn_q_norm'], 'attn_k_norm': out['attn_k_norm'], 'attn_sinks': out['attn_sinks'], 'attn_w_out': out['attn_w_out'], 'pool_w_in': out['pool_w_in'], 'pool_w_group': out['pool_w_group'], 'pool_scale': out['pool_scale'], 'pool_w_out': out['pool_w_out'], 'loss_target': out['loss_target'], 'm_ada_w': out['m_ada_w'], 'm_ada_b': out['m_ada_b'], 'm_norm_g': out['m_norm_g'], 'm_attn_w_in': out['m_attn_w_in'], 'm_attn_q_norm': out['m_attn_q_norm'], 'm_attn_k_norm': out['m_attn_k_norm'], 'm_attn_sinks': out['m_attn_sinks'], 'm_attn_w_out': out['m_attn_w_out'], 'm_pool_w_in': out['m_pool_w_in'], 'm_pool_w_group': out['m_pool_w_group'], 'm_pool_scale': out['m_pool_scale'], 'm_pool_w_out': out['m_pool_w_out'], 'v_ada_w': out['v_ada_w'], 'v_ada_b': out['v_ada_b'], 'v_norm_g': out['v_norm_g'], 'v_attn_w_in': out['v_attn_w_in'], 'v_attn_q_norm': out['v_attn_q_norm'], 'v_attn_k_norm': out['v_attn_k_norm'], 'v_attn_sinks': out['v_attn_sinks'], 'v_attn_w_out': out['v_attn_w_out'], 'v_pool_w_in': out['v_pool_w_in'], 'v_pool_w_group': out['v_pool_w_group'], 'v_pool_scale': out['v_pool_scale'], 'v_pool_w_out': out['v_pool_w_out']}


def _loss(weights, diff, rest, loss_target):
    with _jax.named_scope("forward"):
        args = {**rest, TWIN_DIFF_INPUT: diff, **{k: w.astype(_WEIGHT_DTYPES[k]) for k, w in weights.items()}}
        y = _forward(args)
    with _jax.named_scope("loss_head"):
        err = _jnp.square(y.astype(_jnp.float32) - loss_target)
        return 0.5 * _jnp.sum(_jnp.mean(err, axis=-1)) if err.ndim else 0.5 * err


def _adamw(w, g, m, v):
    m = ADAM_B1 * m + (1.0 - ADAM_B1) * g
    v = ADAM_B2 * v + (1.0 - ADAM_B2) * _jnp.square(g)
    m_hat = m / (1.0 - ADAM_B1 ** ADAM_STEP)
    v_hat = v / (1.0 - ADAM_B2 ** ADAM_STEP)
    delta = -ADAM_LR * (m_hat / (_jnp.sqrt(v_hat) + ADAM_EPS) + ADAM_WD * w)
    return delta, m, v


def reference(x, c, positions, ada_w, ada_b, norm_g, attn_w_in, attn_q_norm, attn_k_norm, attn_sinks, attn_w_out, pool_w_in, pool_w_group, pool_scale, pool_w_out, loss_target, m_ada_w, m_ada_b, m_norm_g, m_attn_w_in, m_attn_q_norm, m_attn_k_norm, m_attn_sinks, m_attn_w_out, m_pool_w_in, m_pool_w_group, m_pool_scale, m_pool_w_out, v_ada_w, v_ada_b, v_norm_g, v_attn_w_in, v_attn_q_norm, v_attn_k_norm, v_attn_sinks, v_attn_w_out, v_pool_w_in, v_pool_w_group, v_pool_scale, v_pool_w_out):
    given = dict(x=x, c=c, positions=positions, ada_w=ada_w, ada_b=ada_b, norm_g=norm_g, attn_w_in=attn_w_in, attn_q_norm=attn_q_norm, attn_k_norm=attn_k_norm, attn_sinks=attn_sinks, attn_w_out=attn_w_out, pool_w_in=pool_w_in, pool_w_group=pool_w_group, pool_scale=pool_scale, pool_w_out=pool_w_out, loss_target=loss_target, m_ada_w=m_ada_w, m_ada_b=m_ada_b, m_norm_g=m_norm_g, m_attn_w_in=m_attn_w_in, m_attn_q_norm=m_attn_q_norm, m_attn_k_norm=m_attn_k_norm, m_attn_sinks=m_attn_sinks, m_attn_w_out=m_attn_w_out, m_pool_w_in=m_pool_w_in, m_pool_w_group=m_pool_w_group, m_pool_scale=m_pool_scale, m_pool_w_out=m_pool_w_out, v_ada_w=v_ada_w, v_ada_b=v_ada_b, v_norm_g=v_norm_g, v_attn_w_in=v_attn_w_in, v_attn_q_norm=v_attn_q_norm, v_attn_k_norm=v_attn_k_norm, v_attn_sinks=v_attn_sinks, v_attn_w_out=v_attn_w_out, v_pool_w_in=v_pool_w_in, v_pool_w_group=v_pool_w_group, v_pool_scale=v_pool_scale, v_pool_w_out=v_pool_w_out)
    weights = {n: given[n] for n in TWIN_WEIGHTS}
    shared = {n: given[n] for n in SHARED_INPUTS}
    per_example = {n: given[n] for n in ['x', 'c', 'positions']}
    grad_fn = _jax.value_and_grad(_loss, argnums=(0, 1))

    def one_microbatch(ex, loss_target):
        ex = dict(ex)
        diff = ex.pop(TWIN_DIFF_INPUT)
        return grad_fn(weights, diff, {**shared, **ex}, loss_target)

    if N_MICROBATCH == 1:
        loss, (grad_w, grad_x) = one_microbatch(per_example, given["loss_target"])
    else:
        def body(carry, xs):
            loss_sum, grad_sum = carry
            l_k, (gw_k, gx_k) = one_microbatch(xs[0], xs[1])
            with _jax.named_scope("update"):
                return (loss_sum + l_k, _jax.tree.map(_jnp.add, grad_sum, gw_k)), gx_k

        init = (_jnp.zeros((), _jnp.float32), _jax.tree.map(_jnp.zeros_like, weights))
        (loss, grad_w), grad_x = _jax.lax.scan(body, init, (per_example, given["loss_target"]))
    with _jax.named_scope("update"):
        delta_w, new_m, new_v = {}, {}, {}
        for n in TWIN_WEIGHTS:
            delta_w[n], new_m[n], new_v[n] = _adamw(weights[n], grad_w[n], given["m_" + n], given["v_" + n])
    return (loss, grad_x, *[grad_w[n] for n in TWIN_WEIGHTS], *[delta_w[n] for n in TWIN_WEIGHTS],
            *[new_m[n] for n in TWIN_WEIGHTS], *[new_v[n] for n in TWIN_WEIGHTS])
```

```python
import functools

import numpy as np
import jax
import jax.numpy as jnp
from jax import lax
from jax.experimental import pallas as pl
from jax.experimental.pallas import tpu as pltpu

F32 = jnp.float32
BF16 = jnp.bfloat16
MESH = pl.DeviceIdType.MESH

N_DEV = 8
D = 1024
DEPTH = 4
HEAD_DIM = 64
N_HEADS = 16
N_KV = 4
QK_W = 1280
ATTN_IN = 2560
POOL_IN = 2048
QBLK = 128
POOL_WINDOWS = (2, 4, 8, 16)
HALO = 16
ROPE_THETA = 500000.0
ROT_DIM = 16
NORM_EPS = 1e-6
ADAM_LR = 0.001
ADAM_B1 = 0.9
ADAM_B2 = 0.999
ADAM_EPS = 1e-08
ADAM_WD = 0.01
ADAM_STEP = 10

LANES = 128
VMEM_LIMIT = 56 * 2**20
VEC_ROWS = 32


def _cparams(n_grid=0, **kw):
    if n_grid:
        kw["dimension_semantics"] = ("arbitrary",) * n_grid
    return pltpu.CompilerParams(vmem_limit_bytes=VMEM_LIMIT, **kw)


def _call(body, **kw):
    return pl.pallas_call(body, **kw)


def _const_spec(shape):
    nd = len(shape)
    return pl.BlockSpec(shape, lambda *_: (0,) * nd, pipeline_mode=pl.Buffered(1))


def _layer_spec(shape, j):
    nd = len(shape)
    return pl.BlockSpec((None,) + tuple(shape), lambda *_: (j,) + (0,) * nd, pipeline_mode=pl.Buffered(1))


def _dot(a, b):
    return jnp.dot(a, b, preferred_element_type=F32)


def _dot_nt(a, b):
    return lax.dot_general(a, b, (((1,), (1,)), ((), ())), preferred_element_type=F32)


def _dot_tn(a, b):
    return lax.dot_general(a, b, (((0,), (0,)), ((), ())), preferred_element_type=F32)


def _dot_split(x, m):
    hi = x.astype(BF16)
    lo = (x - hi.astype(F32)).astype(BF16)
    return _dot(hi, m) + _dot(lo, m)


def _sigmoid(g):
    return 1.0 / (1.0 + jnp.exp(-g))


def _norm_mod(x, ng, sc, sh):
    r = lax.rsqrt(jnp.mean(x * x, axis=-1, keepdims=True) + NORM_EPS)
    xh = x * r
    h = (xh * ng) * (1.0 + sc) + sh
    return xh, r, h


def _rope_tables(pos_col, invf_row, rows):
    ang = pos_col.astype(F32) * invf_row
    l64 = lax.broadcasted_iota(jnp.int32, (rows, LANES), 1) & (HEAD_DIM - 1)
    cs, sn = jnp.cos(ang), jnp.sin(ang)
    cos_t = jnp.where(l64 < ROT_DIM, cs, 1.0)
    sin_a = jnp.where(l64 < ROT_DIM // 2, -sn, 0.0)
    sin_b = jnp.where((l64 >= ROT_DIM // 2) & (l64 < ROT_DIM), sn, 0.0)
    return cos_t, sin_a, sin_b


def _rope(y, tabs):
    cos_t, sin_a, sin_b = tabs
    return y * cos_t + pltpu.roll(y, LANES - ROT_DIM // 2, 1) * sin_a + pltpu.roll(y, ROT_DIM // 2, 1) * sin_b


def _rope_bwd(dy, tabs):
    cos_t, sin_a, sin_b = tabs
    return dy * cos_t + pltpu.roll(dy * sin_a, ROT_DIM // 2, 1) + pltpu.roll(dy * sin_b, LANES - ROT_DIM // 2, 1)


def _low_half(rows):
    return lax.broadcasted_iota(jnp.int32, (rows, LANES), 1) < HEAD_DIM


def _adamw(w, g, m, v):
    m = ADAM_B1 * m + (1.0 - ADAM_B1) * g
    v = ADAM_B2 * v + (1.0 - ADAM_B2) * (g * g)
    m_hat = m / (1.0 - ADAM_B1 ** ADAM_STEP)
    v_hat = v / (1.0 - ADAM_B2 ** ADAM_STEP)
    delta = -ADAM_LR * (m_hat / (jnp.sqrt(v_hat) + ADAM_EPS) + ADAM_WD * w)
    return delta, m, v


def _my_position():
    x, y, c = lax.axis_index("x"), lax.axis_index("y"), lax.axis_index("c")
    return x, y, c, 4 * x + 2 * y + c


def _peers(x, y, c):
    out = []
    for k in range(1, N_DEV):
        px = 1 - x if k & 4 else x
        py = 1 - y if k & 2 else y
        pc = 1 - c if k & 1 else c
        out.append(((px, py, pc), 4 * px + 2 * py + pc))
    return out


def _allgather_small(v, name):
    rows, cols = v.shape

    def body(v_ref, out_ref, send_sems, recv_sems, local_sem):
        x, y, c, me = _my_position()
        local = pltpu.make_async_copy(v_ref, out_ref.at[me], local_sem)
        local.start()
        sends = []
        for k, (peer, _) in enumerate(_peers(x, y, c)):
            cp = pltpu.make_async_remote_copy(v_ref, out_ref.at[me], send_sems.at[k], recv_sems.at[k],
                                              device_id=peer, device_id_type=MESH)
            cp.start()
            sends.append(cp)
        for k, (peer, idx) in enumerate(_peers(x, y, c)):
            pltpu.make_async_remote_copy(v_ref, out_ref.at[idx], send_sems.at[k], recv_sems.at[k],
                                         device_id=peer, device_id_type=MESH).wait_recv()
        for cp in sends:
            cp.wait_send()
        local.wait()

    return _call(
        body, name=name,
        out_shape=jax.ShapeDtypeStruct((N_DEV, rows, cols), F32),
        in_specs=[pl.BlockSpec(memory_space=pltpu.VMEM)],
        out_specs=pl.BlockSpec(memory_space=pltpu.VMEM),
        scratch_shapes=[pltpu.SemaphoreType.DMA((N_DEV - 1,)), pltpu.SemaphoreType.DMA((N_DEV - 1,)),
                        pltpu.SemaphoreType.DMA(())],
        compiler_params=_cparams(),
    )(v)


def _shard_rows(ref, idx, rows, axis):
    sl = [slice(None)] * len(ref.shape)
    sl[axis] = pl.ds(idx * rows, rows)
    return ref.at[tuple(sl)]


def _allgather_weights(shards, axes):
    n = len(shards)
    fulls = []
    for s, ax in zip(shards, axes):
        shp = list(s.shape)
        shp[ax] *= N_DEV
        fulls.append(jax.ShapeDtypeStruct(tuple(shp), s.dtype))

    def body(*refs):
        srcs, outs = refs[:n], refs[n:2 * n]
        send_sems, recv_sems, local_sems = refs[2 * n:]
        x, y, c, me = _my_position()
        sibling = (x, y, 1 - c)
        chips = [(1 - x, y), (x, 1 - y), (1 - x, 1 - y)]

        def idx_of(dev):
            return 4 * dev[0] + 2 * dev[1] + dev[2]

        def copy(k, a, block, to, from_shard=False):
            dst = _shard_rows(outs[a], idx_of(block), srcs[a].shape[axes[a]], axes[a])
            return pltpu.make_async_remote_copy(srcs[a] if from_shard else dst, dst, send_sems.at[k, a], recv_sems.at[k, a],
                                                device_id=to, device_id_type=MESH)

        mine = (x, y, c)
        locals_ = []
        for a in range(n):
            lc = pltpu.make_async_copy(srcs[a], _shard_rows(outs[a], me, srcs[a].shape[axes[a]], axes[a]), local_sems.at[a])
            lc.start()
            locals_.append(lc)
        first = []
        for a in range(n):
            first.append(copy(0, a, mine, sibling, from_shard=True))
            for j, chip in enumerate(chips):
                first.append(copy(1 + j, a, mine, (*chip, c), from_shard=True))
        for cp in first:
            cp.start()
        passed = []
        for j, chip in enumerate(chips):
            for a in range(n):
                copy(1 + j, a, (*chip, c), mine).wait_recv()
                cp = copy(4 + j, a, (*chip, c), sibling)
                cp.start()
                passed.append(cp)
        for a in range(n):
            copy(0, a, sibling, mine).wait_recv()
        for j, chip in enumerate(chips):
            for a in range(n):
                copy(4 + j, a, (*chip, 1 - c), mine).wait_recv()
        for cp in first + passed:
            cp.wait_send()
        for lc in locals_:
            lc.wait()

    return _call(
        body, name="allgather_weights",
        out_shape=tuple(fulls),
        in_specs=[pl.BlockSpec(memory_space=pl.ANY)] * n,
        out_specs=tuple(pl.BlockSpec(memory_space=pl.ANY) for _ in range(n)),
        scratch_shapes=[pltpu.SemaphoreType.DMA((N_DEV - 1, n)), pltpu.SemaphoreType.DMA((N_DEV - 1, n)),
                        pltpu.SemaphoreType.DMA((n,))],
        compiler_params=_cparams(),
    )(*shards)


def _exchange_grads(fulls, axes):
    n = len(fulls)
    outs_shape = []
    for f, ax in zip(fulls, axes):
        shp = list(f.shape)
        shp[ax] //= N_DEV
        outs_shape.append(jax.ShapeDtypeStruct((N_DEV,) + tuple(shp), f.dtype))

    def body(*refs):
        srcs, outs = refs[:n], refs[n:2 * n]
        send_sems, recv_sems, local_sems = refs[2 * n:]
        x, y, c, me = _my_position()
        peers = _peers(x, y, c)

        def rows_of(a, idx):
            return _shard_rows(srcs[a], idx, outs[a].shape[1 + axes[a]], axes[a])

        locals_ = []
        for a in range(n):
            lc = pltpu.make_async_copy(rows_of(a, me), outs[a].at[me], local_sems.at[a])
            lc.start()
            locals_.append(lc)
        sends = []
        for k, (peer, idx) in enumerate(peers):
            for a in range(n):
                cp = pltpu.make_async_remote_copy(rows_of(a, idx), outs[a].at[me], send_sems.at[k, a], recv_sems.at[k, a],
                                                  device_id=peer, device_id_type=MESH)
                cp.start()
                sends.append(cp)
        for k, (peer, idx) in enumerate(peers):
            for a in range(n):
                pltpu.make_async_remote_copy(rows_of(a, me), outs[a].at[idx], send_sems.at[k, a], recv_sems.at[k, a],
                                             device_id=peer, device_id_type=MESH).wait_recv()
        for cp in sends:
            cp.wait_send()
        for lc in locals_:
            lc.wait()

    return _call(
        body, name="exchange_grads",
        out_shape=tuple(outs_shape),
        in_specs=[pl.BlockSpec(memory_space=pl.ANY)] * n,
        out_specs=tuple(pl.BlockSpec(memory_space=pl.ANY) for _ in range(n)),
        scratch_shapes=[pltpu.SemaphoreType.DMA((N_DEV - 1, n)), pltpu.SemaphoreType.DMA((N_DEV - 1, n)),
                        pltpu.SemaphoreType.DMA((n,))],
        compiler_params=_cparams(),
    )(*fulls)


def _prep_weights(attn_w_in, attn_w_out, pool_w_in, pool_w_out, pool_w_group):
    nl = attn_w_in.shape[0]

    def body(awi, awo, pwi, pwo, pwg, o_awi, o_awo, o_pwi, o_pwo, o_pwg):
        o_awi[...] = awi[...].T.astype(BF16)
        o_pwi[...] = pwi[...].T.astype(BF16)
        o_awo[...] = awo[...].astype(BF16)
        o_pwo[...] = pwo[...].astype(BF16)
        o_pwg[...] = pwg[...].astype(BF16)

    def spec(shape):
        nd = len(shape)
        return pl.BlockSpec((None,) + tuple(shape), lambda j: (j,) + (0,) * nd)

    ins = (attn_w_in, attn_w_out, pool_w_in, pool_w_out, pool_w_group)
    out_shapes = [(nl, attn_w_in.shape[2], D), attn_w_out.shape, (nl, pool_w_in.shape[2], D), pool_w_out.shape,
                  pool_w_group.shape]
    return _call(
        body, name="prep_weights", grid=(nl,),
        out_shape=tuple(jax.ShapeDtypeStruct(s, BF16) for s in out_shapes),
        in_specs=[spec(a.shape[1:]) for a in ins],
        out_specs=tuple(spec(s[1:]) for s in out_shapes),
        compiler_params=_cparams(1),
    )(*ins)


def _ada_forward(c_all, ada_w):
    cols = ada_w.shape[2]

    def body(c_ref, w_ref, o_ref):
        cv = c_ref[...]
        sc = (cv * _sigmoid(cv)).astype(BF16)
        o_ref[...] = _dot(sc, w_ref[...].astype(BF16))

    return _call(
        body, name="ada_forward", grid=(DEPTH,),
        out_shape=jax.ShapeDtypeStruct((DEPTH, N_DEV, cols), F32),
        in_specs=[pl.BlockSpec((N_DEV, D), lambda i: (0, 0)), pl.BlockSpec((None, D, cols), lambda i: (i, 0, 0))],
        out_specs=pl.BlockSpec((None, N_DEV, cols), lambda i: (i, 0, 0)),
        compiler_params=_cparams(1),
    )(c_all, ada_w)


def _ada_backward_adamw(c_pad, dmod_pad, w, m, v):
    cols = w.shape[2]

    def body(c_ref, dm_ref, w_ref, m_ref, v_ref, g_out, d_out, m_out, v_out):
        cv = c_ref[...]
        sc = (cv * _sigmoid(cv)).astype(BF16)
        g = _dot_tn(sc, dm_ref[...].astype(BF16))
        g_out[...] = g
        d_out[...], m_out[...], v_out[...] = _adamw(w_ref[...], g, m_ref[...], v_ref[...])

    wspec = pl.BlockSpec((None, D, cols), lambda i: (i, 0, 0))
    return _call(
        body, name="ada_backward_adamw", grid=(DEPTH,),
        out_shape=tuple(jax.ShapeDtypeStruct(w.shape, F32) for _ in range(4)),
        in_specs=[pl.BlockSpec((2 * N_DEV, D), lambda i: (0, 0)), pl.BlockSpec((None, 2 * N_DEV, cols), lambda i: (i, 0, 0)),
                  wspec, wspec, wspec],
        out_specs=(wspec, wspec, wspec, wspec),
        compiler_params=_cparams(1),
    )(c_pad, dmod_pad, w, m, v)


def _attn_in_proj(x, pos_col, ng, mod, w_t, j, gain, invf, bd, tile):
    seq = x.shape[0]

    def body(x_ref, pos_ref, ng_ref, mod_ref, w_ref, gain_ref, invf_ref, bd_ref, qk_ref, qs_ref, kd_ref, vd_ref, g_ref):
        _, _, h = _norm_mod(x_ref[...], ng_ref[...], mod_ref[1:2, :], mod_ref[0:1, :])
        proj = _dot_nt(h.astype(BF16), w_ref[...])
        qk_ref[...] = proj[:, :QK_W]
        g_ref[...] = proj[:, QK_W + N_KV * HEAD_DIM:]
        tabs = _rope_tables(pos_ref[...], invf_ref[...], tile)
        low = _low_half(tile)
        bdm = bd_ref[...]
        for b in range(QK_W // LANES):
            blk = proj[:, LANES * b:LANES * (b + 1)]
            ms = _dot_split(blk * blk, bdm) * (1.0 / HEAD_DIM)
            y = (blk * lax.rsqrt(ms + NORM_EPS)) * gain_ref[:, LANES * b:LANES * (b + 1)]
            rp = _rope(y, tabs)
            if b < D // LANES:
                rp = rp * (HEAD_DIM ** -0.5)
                qs_ref[:, 2 * LANES * b:2 * LANES * b + LANES] = jnp.where(low, rp, 0.0).astype(BF16)
                qs_ref[:, 2 * LANES * b + LANES:2 * LANES * (b + 1)] = jnp.where(low, 0.0, rp).astype(BF16)
            else:
                kv = 2 * (b - D // LANES)
                sw = pltpu.roll(rp, HEAD_DIM, 1)
                kd_ref[:, LANES * kv:LANES * (kv + 1)] = jnp.where(low, rp, sw).astype(BF16)
                kd_ref[:, LANES * (kv + 1):LANES * (kv + 2)] = jnp.where(low, sw, rp).astype(BF16)
        for b in range(2):
            blk = proj[:, QK_W + LANES * b:QK_W + LANES * (b + 1)]
            sw = pltpu.roll(blk, HEAD_DIM, 1)
            vd_ref[:, LANES * 2 * b:LANES * (2 * b + 1)] = jnp.where(low, blk, sw).astype(BF16)
            vd_ref[:, LANES * (2 * b + 1):LANES * (2 * b + 2)] = jnp.where(low, sw, blk).astype(BF16)

    row = lambda w: pl.BlockSpec((tile, w), lambda i: (i, 0))
    return _call(
        body, name=f"attn_in_proj_{j}", grid=(seq // tile,),
        out_shape=(jax.ShapeDtypeStruct((seq, QK_W), F32), jax.ShapeDtypeStruct((seq, N_HEADS * LANES), BF16),
                   jax.ShapeDtypeStruct((seq, N_KV * LANES), BF16), jax.ShapeDtypeStruct((seq, N_KV * LANES), BF16),
                   jax.ShapeDtypeStruct((seq, D), F32)),
        in_specs=[row(D), row(1), _const_spec((1, D)), _const_spec((8, D)), _layer_spec((ATTN_IN, D), j),
                  _const_spec((1, QK_W)), _const_spec((1, LANES)), _const_spec((LANES, LANES))],
        out_specs=(row(QK_W), row(N_HEADS * LANES), row(N_KV * LANES), row(N_KV * LANES), row(D)),
        compiler_params=_cparams(1),
    )(x, pos_col, ng, mod, w_t, gain, invf, bd)


def _band_mask(n):
    rows = lax.broadcasted_iota(jnp.int32, (4 * QBLK, 2 * QBLK), 0) & (QBLK - 1)
    cols = lax.broadcasted_iota(jnp.int32, (4 * QBLK, 2 * QBLK), 1)
    diff = QBLK + rows - cols
    first_key = jnp.where(n > 0, 0, QBLK)
    return (diff >= 0) & (diff < QBLK) & (cols >= first_key)


def _stack_heads(ref, kv):
    return jnp.concatenate([ref[:, LANES * h:LANES * (h + 1)] for h in range(4 * kv, 4 * kv + 4)], axis=0)


def _pair_up(st, low):
    return jnp.concatenate([jnp.where(low, st[0:QBLK], st[QBLK:2 * QBLK]),
                            jnp.where(low, st[2 * QBLK:3 * QBLK], st[3 * QBLK:4 * QBLK])], axis=1)


def _head_column(tile, h):
    lane = lax.broadcasted_iota(jnp.int32, tile.shape, 1)
    return jnp.sum(jnp.where(lane == h, tile, 0.0), axis=1, keepdims=True)


def _attn_forward(sinks, qs, kd, vd, j):
    seq = qs.shape[0]
    nb = seq // QBLK

    def body(sink_ref, q_ref, kp_ref, kc_ref, vp_ref, vc_ref, o_ref, lse_ref):
        n = pl.program_id(0)
        ok = _band_mask(n)
        low = _low_half(QBLK)
        lane = lax.broadcasted_iota(jnp.int32, (QBLK, LANES), 1)
        rowi = lax.broadcasted_iota(jnp.int32, (4 * QBLK, 1), 0)
        lse_tile = jnp.zeros((QBLK, LANES), F32)
        for kv in range(N_KV):
            q = _stack_heads(q_ref, kv)
            kk = jnp.concatenate([kp_ref[:, LANES * kv:LANES * (kv + 1)], kc_ref[:, LANES * kv:LANES * (kv + 1)]], axis=0)
            vv = jnp.concatenate([vp_ref[:, LANES * kv:LANES * (kv + 1)], vc_ref[:, LANES * kv:LANES * (kv + 1)]], axis=0)
            s = jnp.where(ok, _dot_nt(q, kk), -1e30)
            sink = jnp.where(rowi < QBLK, sink_ref[4 * kv],
                             jnp.where(rowi < 2 * QBLK, sink_ref[4 * kv + 1],
                                       jnp.where(rowi < 3 * QBLK, sink_ref[4 * kv + 2], sink_ref[4 * kv + 3])))
            m = jnp.maximum(jnp.max(s, axis=1, keepdims=True), sink)
            p = jnp.exp(s - m)
            den = jnp.sum(p, axis=1, keepdims=True) + jnp.exp(sink - m)
            o_st = _dot((p / den).astype(BF16), vv)
            o_ref[:, 2 * LANES * kv:2 * LANES * (kv + 1)] = _pair_up(o_st, low)
            lse = m + jnp.log(den)
            for r in range(4):
                lse_tile = jnp.where(lane == 4 * kv + r, lse[QBLK * r:QBLK * (r + 1)], lse_tile)
        lse_ref[...] = lse_tile

    blk = lambda w: pl.BlockSpec((QBLK, w), lambda n: (n, 0))
    prev = lambda w: pl.BlockSpec((QBLK, w), lambda n: (jnp.maximum(n - 1, 0), 0))
    return _call(
        body, name=f"attn_forward_{j}", grid=(nb,),
        out_shape=(jax.ShapeDtypeStruct((seq, D), F32), jax.ShapeDtypeStruct((seq, LANES), F32)),
        in_specs=[pl.BlockSpec(memory_space=pltpu.SMEM), blk(N_HEADS * LANES), prev(N_KV * LANES), blk(N_KV * LANES),
                  prev(N_KV * LANES), blk(N_KV * LANES)],
        out_specs=(blk(D), blk(LANES)),
        compiler_params=_cparams(1),
    )(sinks, qs, kd, kd, vd, vd)


def _attn_out_proj(x, o, g, w, j, mod, tile):
    seq = x.shape[0]

    def body(x_ref, o_ref, g_ref, w_ref, mod_ref, xo_ref, br_ref):
        gv = g_ref[...]
        u = (o_ref[...] * (gv * _sigmoid(gv))).astype(BF16)
        br = _dot(u, w_ref[...])
        br_ref[...] = br
        xo_ref[...] = x_ref[...] + mod_ref[2:3, :] * br

    row = pl.BlockSpec((tile, D), lambda i: (i, 0))
    return _call(
        body, name=f"attn_out_proj_{j}", grid=(seq // tile,),
        out_shape=(jax.ShapeDtypeStruct((seq, D), F32), jax.ShapeDtypeStruct((seq, D), F32)),
        in_specs=[row, row, row, _layer_spec((D, D), j), _const_spec((8, D))],
        out_specs=(row, row),
        compiler_params=_cparams(1),
    )(x, o, g, w, mod)


def _attn_out_proj_bwd(dxn, br, o, g, w, j, mod, fold, tile):
    seq = dxn.shape[0]
    steps = seq // tile

    def body(dxn_ref, br_ref, o_ref, g_ref, w_ref, mod_ref, fold_ref, do_ref, dg_ref, delta_ref, dw_ref, dgate_ref, dw_acc):
        i = pl.program_id(0)

        @pl.when(i == 0)
        def _():
            dw_acc[...] = jnp.zeros_like(dw_acc)
            dgate_ref[...] = jnp.zeros_like(dgate_ref)

        dxn_v, ov, gv = dxn_ref[...], o_ref[...], g_ref[...]
        dgate_ref[...] += jnp.sum(dxn_v * br_ref[...], axis=0, keepdims=True)
        dbr = (dxn_v * mod_ref[2:3, :]).astype(BF16)
        du = _dot_nt(dbr, w_ref[...])
        sg = _sigmoid(gv)
        sl = gv * sg
        dw_acc[...] += _dot_tn((ov * sl).astype(BF16), dbr)
        do = du * sl
        dg_ref[...] = (du * ov * (sg * (1.0 + gv * (1.0 - sg)))).astype(BF16)
        delta_ref[...] = _dot_split(do * ov, fold_ref[...])
        low = _low_half(tile)
        for b in range(D // LANES):
            blk = do[:, LANES * b:LANES * (b + 1)]
            do_ref[:, 2 * LANES * b:2 * LANES * b + LANES] = jnp.where(low, blk, 0.0).astype(BF16)
            do_ref[:, 2 * LANES * b + LANES:2 * LANES * (b + 1)] = jnp.where(low, 0.0, blk).astype(BF16)

        @pl.when(i == steps - 1)
        def _():
            dw_ref[...] = dw_acc[...].astype(BF16)

    row = lambda w_: pl.BlockSpec((tile, w_), lambda i: (i, 0))
    return _call(
        body, name=f"attn_out_proj_bwd_{j}", grid=(steps,),
        out_shape=(jax.ShapeDtypeStruct((seq, N_HEADS * LANES), BF16), jax.ShapeDtypeStruct((seq, D), BF16),
                   jax.ShapeDtypeStruct((seq, LANES), F32), jax.ShapeDtypeStruct((D, D), BF16),
                   jax.ShapeDtypeStruct((1, D), F32)),
        in_specs=[row(D), row(D), row(D), row(D), _layer_spec((D, D), j), _const_spec((8, D)), _const_spec((D, LANES))],
        out_specs=(row(N_HEADS * LANES), row(D), row(LANES), pl.BlockSpec((D, D), lambda i: (0, 0)),
                   pl.BlockSpec((1, D), lambda i: (0, 0))),
        scratch_shapes=[pltpu.VMEM((D, D), F32)],
        compiler_params=_cparams(1),
    )(dxn, br, o, g, w, mod, fold)


def _attn_backward(sink_row, qs, dos, kd, vd, lse, delta, j):
    seq = qs.shape[0]
    nb = seq // QBLK

    def body(sink_ref, q_ref, do_ref, kp_ref, kc_ref, vp_ref, vc_ref, lse_ref, delta_ref,
             dq_ref, dk_ref, dv_ref, dsink_ref, carry_k, carry_v):
        n = pl.program_id(0)

        @pl.when(n == 0)
        def _():
            carry_k[...] = jnp.zeros_like(carry_k)
            carry_v[...] = jnp.zeros_like(carry_v)
            dsink_ref[...] = jnp.zeros_like(dsink_ref)

        @pl.when(n < nb)
        def _():
            ok = _band_mask(n)
            low = _low_half(QBLK)
            lse_t, delta_t = lse_ref[...], delta_ref[...]
            dsink_ref[...] -= jnp.sum(jnp.exp(sink_ref[...] - lse_t) * delta_t, axis=0, keepdims=True)
            dk_parts, dv_parts = [], []
            for kv in range(N_KV):
                q = _stack_heads(q_ref, kv)
                do = _stack_heads(do_ref, kv)
                kk = jnp.concatenate([kp_ref[:, LANES * kv:LANES * (kv + 1)], kc_ref[:, LANES * kv:LANES * (kv + 1)]], axis=0)
                vv = jnp.concatenate([vp_ref[:, LANES * kv:LANES * (kv + 1)], vc_ref[:, LANES * kv:LANES * (kv + 1)]], axis=0)
                lse_c = jnp.concatenate([_head_column(lse_t, 4 * kv + r) for r in range(4)], axis=0)
                dlt_c = jnp.concatenate([_head_column(delta_t, 4 * kv + r) for r in range(4)], axis=0)
                s = jnp.where(ok, _dot_nt(q, kk), -1e30)
                p = jnp.exp(s - lse_c)
                ds = (p * (_dot_nt(do, vv) - dlt_c)).astype(BF16)
                dq_ref[:, 2 * LANES * kv:2 * LANES * (kv + 1)] = _pair_up(_dot(ds, kk), low)
                dkd = _dot_tn(ds, q)
                dvd = _dot_tn(p.astype(BF16), do)
                dk_parts.append(dkd + pltpu.roll(dkd, HEAD_DIM, 1))
                dv_parts.append(dvd + pltpu.roll(dvd, HEAD_DIM, 1))

            def order(parts, lo, hi):
                return jnp.concatenate([jnp.where(low, parts[0][lo:hi], parts[1][lo:hi]),
                                        jnp.where(low, parts[2][lo:hi], parts[3][lo:hi])], axis=1)

            dk_ref[...] = carry_k[...] + order(dk_parts, 0, QBLK)
            dv_ref[...] = (carry_v[...] + order(dv_parts, 0, QBLK)).astype(BF16)
            carry_k[...] = order(dk_parts, QBLK, 2 * QBLK)
            carry_v[...] = order(dv_parts, QBLK, 2 * QBLK)

        @pl.when(n == nb)
        def _():
            dk_ref[...] = carry_k[...]
            dv_ref[...] = carry_v[...].astype(BF16)

    cur = lambda w: pl.BlockSpec((QBLK, w), lambda n: (jnp.minimum(n, nb - 1), 0))
    prev = lambda w: pl.BlockSpec((QBLK, w), lambda n: (jnp.maximum(n - 1, 0), 0))
    kcur = lambda w: pl.BlockSpec((QBLK, w), lambda n: (jnp.minimum(n, nb - 1), 0))
    return _call(
        body, name=f"attn_backward_{j}", grid=(nb + 1,),
        out_shape=(jax.ShapeDtypeStruct((seq, D), F32), jax.ShapeDtypeStruct((seq, N_KV * HEAD_DIM), F32),
                   jax.ShapeDtypeStruct((seq, N_KV * HEAD_DIM), BF16), jax.ShapeDtypeStruct((1, LANES), F32)),
        in_specs=[_const_spec((1, LANES)), cur(N_HEADS * LANES), cur(N_HEADS * LANES), prev(N_KV * LANES), kcur(N_KV * LANES),
                  prev(N_KV * LANES), kcur(N_KV * LANES), cur(LANES), cur(LANES)],
        out_specs=(cur(D), prev(N_KV * HEAD_DIM), prev(N_KV * HEAD_DIM), pl.BlockSpec((1, LANES), lambda n: (0, 0))),
        scratch_shapes=[pltpu.VMEM((QBLK, N_KV * HEAD_DIM), F32), pltpu.VMEM((QBLK, N_KV * HEAD_DIM), F32)],
        compiler_params=_cparams(1),
    )(sink_row, qs, dos, kd, kd, vd, vd, lse, delta)


def _in_proj_tail(x_ref, dxn_ref, ng_ref, mod_ref, w_ref, dproj, dx_ref, dw_acc, vec_acc):
    ng, sc, sh = ng_ref[...], mod_ref[1:2, :], mod_ref[0:1, :]
    xh, r, h = _norm_mod(x_ref[...], ng, sc, sh)
    dh = _dot(dproj, w_ref[...])
    dw_acc[...] += _dot_tn(dproj, h.astype(BF16))
    vec_acc[0:1, :] += jnp.sum(dh, axis=0, keepdims=True)
    vec_acc[1:2, :] += jnp.sum(dh * xh, axis=0, keepdims=True)
    dxh = dh * (ng * (1.0 + sc))
    dx_ref[...] = dxn_ref[...] + r * (dxh - xh * jnp.mean(dxh * xh, axis=-1, keepdims=True))


def _tail_finish(ng_ref, mod_ref, dw_ref, vec_ref, dw_acc, vec_acc):
    dw_ref[...] = dw_acc[...].astype(BF16)
    a = vec_acc[1:2, :]
    vec_ref[...] = jnp.zeros_like(vec_ref)
    vec_ref[0:1, :] = vec_acc[0:1, :]
    vec_ref[1:2, :] = a * ng_ref[...]
    vec_ref[3:4, :] = a * (1.0 + mod_ref[1:2, :])


def _attn_in_proj_bwd(x, dxn, pos_col, qk_raw, dq, dk, dv, dg, ng, mod, w_t, j, gain, invf, bd, tile):
    seq = x.shape[0]
    steps = seq // tile

    def body(x_ref, dxn_ref, pos_ref, qk_ref, dq_ref, dk_ref, dv_ref, dg_ref, ng_ref, mod_ref, w_ref, gain_ref, invf_ref,
             bd_ref, dx_ref, dw_ref, vec_ref, dgain_ref, dproj, dw_acc, vec_acc):
        i = pl.program_id(0)

        @pl.when(i == 0)
        def _():
            dw_acc[...] = jnp.zeros_like(dw_acc)
            vec_acc[...] = jnp.zeros_like(vec_acc)
            dgain_ref[...] = jnp.zeros_like(dgain_ref)

        tabs = _rope_tables(pos_ref[...], invf_ref[...], tile)
        bdm = bd_ref[...]
        for b in range(QK_W // LANES):
            cols = slice(LANES * b, LANES * (b + 1))
            raw = qk_ref[:, cols]
            if b < D // LANES:
                dy = dq_ref[:, cols] * (HEAD_DIM ** -0.5)
            else:
                dy = dk_ref[:, LANES * (b - D // LANES):LANES * (b + 1 - D // LANES)]
            dy = _rope_bwd(dy, tabs)
            rr = lax.rsqrt(_dot_split(raw * raw, bdm) * (1.0 / HEAD_DIM) + NORM_EPS)
            xh = raw * rr
            dgain_ref[:, cols] += jnp.sum(dy * xh, axis=0, keepdims=True)
            dxh = dy * gain_ref[:, cols]
            draw = rr * (dxh - xh * (_dot_split(dxh * xh, bdm) * (1.0 / HEAD_DIM)))
            dproj[:, cols] = draw.astype(BF16)
        dproj[:, QK_W:QK_W + N_KV * HEAD_DIM] = dv_ref[...]
        dproj[:, QK_W + N_KV * HEAD_DIM:] = dg_ref[...]
        _in_proj_tail(x_ref, dxn_ref, ng_ref, mod_ref, w_ref, dproj[...], dx_ref, dw_acc, vec_acc)

        @pl.when(i == steps - 1)
        def _():
            _tail_finish(ng_ref, mod_ref, dw_ref, vec_ref, dw_acc, vec_acc)

    row = lambda w, dt=None: pl.BlockSpec((tile, w), lambda i: (i, 0))
    fixed = lambda shape: pl.BlockSpec(shape, lambda i: (0,) * len(shape))
    return _call(
        body, name=f"attn_in_proj_bwd_{j}", grid=(steps,),
        out_shape=(jax.ShapeDtypeStruct((seq, D), F32), jax.ShapeDtypeStruct((ATTN_IN, D), BF16),
                   jax.ShapeDtypeStruct((8, D), F32), jax.ShapeDtypeStruct((1, QK_W), F32)),
        in_specs=[row(D), row(D), row(1), row(QK_W), row(D), row(N_KV * HEAD_DIM), row(N_KV * HEAD_DIM), row(D),
                  _const_spec((1, D)), _const_spec((8, D)), _layer_spec((ATTN_IN, D), j), _const_spec((1, QK_W)),
                  _const_spec((1, LANES)), _const_spec((LANES, LANES))],
        out_specs=(row(D), fixed((ATTN_IN, D)), fixed((8, D)), fixed((1, QK_W))),
        scratch_shapes=[pltpu.VMEM((tile, ATTN_IN), BF16), pltpu.VMEM((ATTN_IN, D), F32), pltpu.VMEM((8, D), F32)],
        compiler_params=_cparams(1),
    )(x, dxn, pos_col, qk_raw, dq, dk, dv, dg, ng, mod, w_t, gain, invf, bd)


def _pool_in_proj(x, ng, mod, w_t, j, tile):
    seq = x.shape[0]

    def body(x_ref, ng_ref, mod_ref, w_ref, v_ref, g_ref):
        _, _, h = _norm_mod(x_ref[...], ng_ref[...], mod_ref[1:2, :], mod_ref[0:1, :])
        proj = _dot_nt(h.astype(BF16), w_ref[...])
        v_ref[...] = proj[:, :D]
        g_ref[...] = proj[:, D:]

    row = pl.BlockSpec((tile, D), lambda i: (i, 0))
    return _call(
        body, name=f"pool_in_proj_{j}", grid=(seq // tile,),
        out_shape=(jax.ShapeDtypeStruct((seq, D), F32), jax.ShapeDtypeStruct((seq, D), F32)),
        in_specs=[row, _const_spec((1, D)), _const_spec((8, D)), _layer_spec((POOL_IN, D), j)],
        out_specs=(row, row),
        compiler_params=_cparams(1),
    )(x, ng, mod, w_t)


def _pooled(ext, first, tile):
    t_abs = first + lax.broadcasted_iota(jnp.int32, (tile, 1), 0)
    outs = []
    gw = D // len(POOL_WINDOWS)
    for gi, w in enumerate(POOL_WINDOWS):
        cols = slice(gw * gi, gw * (gi + 1))
        own = ext[HALO:HALO + tile, cols]
        acc = own
        for k in range(1, w):
            acc = acc + ext[HALO - k:HALO - k + tile, cols]
        cnt = jnp.minimum(t_abs + 1, w).astype(F32)
        outs.append(acc / cnt - own)
    return jnp.concatenate(outs, axis=1)


def _fill_ext(ext, halo_ref, v_ref, i, tile):
    ext[0:HALO, :] = jnp.where(i == 0, 0.0, halo_ref[...])
    ext[HALO:HALO + tile, :] = v_ref[...]


def _group_mix(pb, wg_ref):
    gw = D // len(POOL_WINDOWS)
    return jnp.concatenate([_dot(pb[:, gw * gi:gw * (gi + 1)], wg_ref[gi]) for gi in range(len(POOL_WINDOWS))], axis=1)


def _pool_mix_out(x, v, g, wg, w_out, j, scale, mod, tile):
    seq = x.shape[0]

    def body(x_ref, v_ref, halo_ref, g_ref, wg_ref, w_ref, scale_ref, mod_ref, xo_ref, br_ref, ext):
        i = pl.program_id(0)
        _fill_ext(ext, halo_ref, v_ref, i, tile)
        pb = _pooled(ext, i * tile, tile).astype(BF16)
        ms = _group_mix(pb, wg_ref) * scale_ref[...]
        gv = g_ref[...]
        u = (ms * (gv * _sigmoid(gv))).astype(BF16)
        br = _dot(u, w_ref[...])
        br_ref[...] = br
        xo_ref[...] = x_ref[...] + mod_ref[2:3, :] * br

    row = pl.BlockSpec((tile, D), lambda i: (i, 0))
    halo = pl.BlockSpec((HALO, D), lambda i: (jnp.maximum(i * (tile // HALO) - 1, 0), 0))
    return _call(
        body, name=f"pool_mix_out_{j}", grid=(seq // tile,),
        out_shape=(jax.ShapeDtypeStruct((seq, D), F32), jax.ShapeDtypeStruct((seq, D), F32)),
        in_specs=[row, row, halo, row, _layer_spec(wg.shape[1:], j), _layer_spec((D, D), j), _const_spec((1, D)),
                  _const_spec((8, D))],
        out_specs=(row, row),
        scratch_shapes=[pltpu.VMEM((tile + HALO, D), F32)],
        compiler_params=_cparams(1),
    )(x, v, v, g, wg, w_out, scale, mod)


def _pool_mix_out_bwd(dxn, br, v, g, wg, w_out, j, scale, mod, tile):
    seq = dxn.shape[0]
    steps = seq // tile
    ng_ = len(POOL_WINDOWS)
    gw = D // ng_

    def body(dxn_ref, br_ref, v_ref, halo_ref, g_ref, wg_ref, w_ref, scale_ref, mod_ref,
             dpool_ref, dg_ref, dw_ref, dwg_ref, vec_ref, ext, dw_acc, dwg_acc):
        i = pl.program_id(0)

        @pl.when(i == 0)
        def _():
            dw_acc[...] = jnp.zeros_like(dw_acc)
            dwg_acc[...] = jnp.zeros_like(dwg_acc)
            vec_ref[...] = jnp.zeros_like(vec_ref)

        _fill_ext(ext, halo_ref, v_ref, i, tile)
        pb = _pooled(ext, i * tile, tile).astype(BF16)
        mixed = _group_mix(pb, wg_ref)
        scale = scale_ref[...]
        ms = mixed * scale
        gv, dxn_v = g_ref[...], dxn_ref[...]
        sg = _sigmoid(gv)
        sl = gv * sg
        vec_ref[0:1, :] += jnp.sum(dxn_v * br_ref[...], axis=0, keepdims=True)
        dbr = (dxn_v * mod_ref[2:3, :]).astype(BF16)
        du = _dot_nt(dbr, w_ref[...])
        dw_acc[...] += _dot_tn((ms * sl).astype(BF16), dbr)
        dms = du * sl
        dg_ref[...] = (du * ms * (sg * (1.0 + gv * (1.0 - sg)))).astype(BF16)
        vec_ref[1:2, :] += jnp.sum(dms * mixed, axis=0, keepdims=True)
        dmx = (dms * scale).astype(BF16)
        for gi in range(ng_):
            cols = slice(gw * gi, gw * (gi + 1))
            dpool_ref[:, cols] = _dot_nt(dmx[:, cols], wg_ref[gi])
            dwg_acc[gi] += _dot_tn(pb[:, cols], dmx[:, cols])

        @pl.when(i == steps - 1)
        def _():
            dw_ref[...] = dw_acc[...].astype(BF16)
            dwg_ref[...] = dwg_acc[...].astype(BF16)

    row = pl.BlockSpec((tile, D), lambda i: (i, 0))
    halo = pl.BlockSpec((HALO, D), lambda i: (jnp.maximum(i * (tile // HALO) - 1, 0), 0))
    fixed = lambda shape: pl.BlockSpec(shape, lambda i: (0,) * len(shape))
    return _call(
        body, name=f"pool_mix_out_bwd_{j}", grid=(steps,),
        out_shape=(jax.ShapeDtypeStruct((seq, D), F32), jax.ShapeDtypeStruct((seq, D), BF16),
                   jax.ShapeDtypeStruct((D, D), BF16), jax.ShapeDtypeStruct((ng_, gw, gw), BF16),
                   jax.ShapeDtypeStruct((8, D), F32)),
        in_specs=[row, row, row, halo, row, _layer_spec(wg.shape[1:], j), _layer_spec((D, D), j), _const_spec((1, D)),
                  _const_spec((8, D))],
        out_specs=(row, row, fixed((D, D)), fixed((ng_, gw, gw)), fixed((8, D))),
        scratch_shapes=[pltpu.VMEM((tile + HALO, D), F32), pltpu.VMEM((D, D), F32), pltpu.VMEM((ng_, gw, gw), F32)],
        compiler_params=_cparams(1),
    )(dxn, br, v, v, g, wg, w_out, scale, mod)


def _pool_in_proj_bwd(x, dxn, dpool, dg, ng, mod, w_t, j, tile):
    seq = x.shape[0]
    steps = seq // tile
    gw = D // len(POOL_WINDOWS)

    def body(x_ref, dxn_ref, dp_ref, halo_ref, dg_ref, ng_ref, mod_ref, w_ref, dx_ref, dw_ref, vec_ref,
             ext, dproj, dw_acc, vec_acc):
        i = pl.program_id(0)

        @pl.when(i == 0)
        def _():
            dw_acc[...] = jnp.zeros_like(dw_acc)
            vec_acc[...] = jnp.zeros_like(vec_acc)

        t_abs = i * tile + lax.broadcasted_iota(jnp.int32, (tile, 1), 0)
        last = i == steps - 1
        for gi, w in enumerate(POOL_WINDOWS):
            cols = slice(gw * gi, gw * (gi + 1))
            cnt = jnp.minimum(t_abs + 1, w).astype(F32)
            ext[0:tile, cols] = dp_ref[:, cols] / cnt
            ext[tile:tile + HALO, cols] = jnp.where(last, 0.0, halo_ref[:, cols] * (1.0 / w))
        for gi, w in enumerate(POOL_WINDOWS):
            cols = slice(gw * gi, gw * (gi + 1))
            acc = ext[0:tile, cols]
            for k in range(1, w):
                acc = acc + ext[k:k + tile, cols]
            dproj[:, cols] = (acc - dp_ref[:, cols]).astype(BF16)
        dproj[:, D:] = dg_ref[...]
        _in_proj_tail(x_ref, dxn_ref, ng_ref, mod_ref, w_ref, dproj[...], dx_ref, dw_acc, vec_acc)

        @pl.when(last)
        def _():
            _tail_finish(ng_ref, mod_ref, dw_ref, vec_ref, dw_acc, vec_acc)

    row = pl.BlockSpec((tile, D), lambda i: (i, 0))
    halo = pl.BlockSpec((HALO, D), lambda i: (jnp.minimum((i + 1) * (tile // HALO), seq // HALO - 1), 0))
    fixed = lambda shape: pl.BlockSpec(shape, lambda i: (0,) * len(shape))
    return _call(
        body, name=f"pool_in_proj_bwd_{j}", grid=(steps,),
        out_shape=(jax.ShapeDtypeStruct((seq, D), F32), jax.ShapeDtypeStruct((POOL_IN, D), BF16),
                   jax.ShapeDtypeStruct((8, D), F32)),
        in_specs=[row, row, row, halo, row, _const_spec((1, D)), _const_spec((8, D)), _layer_spec((POOL_IN, D), j)],
        out_specs=(row, fixed((POOL_IN, D)), fixed((8, D))),
        scratch_shapes=[pltpu.VMEM((tile + HALO, D), F32), pltpu.VMEM((tile, POOL_IN), BF16), pltpu.VMEM((POOL_IN, D), F32),
                        pltpu.VMEM((8, D), F32)],
        compiler_params=_cparams(1),
    )(x, dxn, dpool, dpool, dg, ng, mod, w_t)


def _loss_head(y, target, tile):
    seq = y.shape[0]

    def body(y_ref, t_ref, dy_ref, loss_ref):
        @pl.when(pl.program_id(0) == 0)
        def _():
            loss_ref[...] = jnp.zeros_like(loss_ref)

        e = y_ref[...] - t_ref[...]
        dy_ref[...] = e * (1.0 / D)
        loss_ref[...] += 0.5 * jnp.sum(jnp.mean(e * e, axis=-1, keepdims=True), axis=0, keepdims=True)

    row = pl.BlockSpec((tile, D), lambda i: (i, 0))
    return _call(
        body, name="loss_head", grid=(seq // tile,),
        out_shape=(jax.ShapeDtypeStruct((seq, D), F32), jax.ShapeDtypeStruct((1, LANES), F32)),
        in_specs=[row, row],
        out_specs=(row, pl.BlockSpec((1, LANES), lambda i: (0, 0))),
        compiler_params=_cparams(1),
    )(y, target)


def _build_vec(vecs, gates, pool_vecs, gains, dsinks, loss_part):
    def body(v0, v1, v2, v3, g0, g2, p0, p1, n0, n1, s0, s1, loss_ref, out):
        out[...] = jnp.zeros_like(out)
        for i, v in enumerate((v0, v1, v2, v3)):
            out[3 * i:3 * i + 2, :] = v[0:2, :]
            out[12 + i:13 + i, :] = v[3:4, :]
        out[2:3, :] = g0[...]
        out[8:9, :] = g2[...]
        for j, (p, n, s) in enumerate(((p0, n0, s0), (p1, n1, s1))):
            out[3 * (2 * j + 1) + 2:3 * (2 * j + 1) + 3, :] = p[0:1, :]
            out[22 + j:23 + j, :] = p[1:2, :]
            out[16 + j:17 + j, :] = n[:, 0:D]
            out[18 + j:19 + j, 0:QK_W - D] = n[:, D:QK_W]
            out[20 + j:21 + j, 0:LANES] = s[...]
        out[24:25, 0:LANES] = loss_ref[...]

    vm = pl.BlockSpec(memory_space=pltpu.VMEM)
    args = (*vecs, gates[0], gates[2], *pool_vecs, *gains, *dsinks, loss_part)
    return _call(
        body, name="build_vec",
        out_shape=jax.ShapeDtypeStruct((VEC_ROWS, D), F32),
        in_specs=[vm] * len(args), out_specs=vm,
        compiler_params=_cparams(),
    )(*args)


def _sum_devices(g):
    rows = g.shape[1]

    def body(g_ref, tot_ref, fold_ref):
        tot = g_ref[0]
        for p in range(1, N_DEV):
            tot = tot + g_ref[p]
        tot_ref[...] = tot
        f = tot[16:24, 0:LANES]
        for b in range(1, D // LANES):
            f = f + tot[16:24, LANES * b:LANES * (b + 1)]
        fold_ref[...] = f + pltpu.roll(f, HEAD_DIM, 1)

    return _call(
        body, name="sum_devices",
        out_shape=(jax.ShapeDtypeStruct((rows, D), F32), jax.ShapeDtypeStruct((8, LANES), F32)),
        in_specs=[pl.BlockSpec(memory_space=pltpu.VMEM)],
        out_specs=(pl.BlockSpec(memory_space=pltpu.VMEM), pl.BlockSpec(memory_space=pltpu.VMEM)),
        compiler_params=_cparams(),
    )(g)


def _adamw_small(name, w, g, m, v):
    def body(w_ref, g_ref, m_ref, v_ref, d_out, m_out, v_out):
        d_out[...], m_out[...], v_out[...] = _adamw(w_ref[...], g_ref[...], m_ref[...], v_ref[...])

    vm = pl.BlockSpec(memory_space=pltpu.VMEM)
    return _call(
        body, name=name,
        out_shape=tuple(jax.ShapeDtypeStruct(w.shape, F32) for _ in range(3)),
        in_specs=[vm] * 4, out_specs=(vm, vm, vm),
        compiler_params=_cparams(),
    )(w, g, m, v)


def _adamw_shards(name, recv, w, m, v, transpose):
    nl = w.shape[0]

    def body(r_ref, w_ref, m_ref, v_ref, g_out, d_out, m_out, v_out):
        g = r_ref[0].astype(F32)
        for p in range(1, N_DEV):
            g = g + r_ref[p].astype(F32)
        if transpose:
            g = g.T
        g_out[...] = g
        d_out[...], m_out[...], v_out[...] = _adamw(w_ref[...], g, m_ref[...], v_ref[...])

    rshape = recv.shape[2:]
    rspec = pl.BlockSpec((N_DEV, None) + tuple(rshape), lambda l: (0, l) + (0,) * len(rshape))
    wshape = w.shape[1:]
    wspec = pl.BlockSpec((None,) + tuple(wshape), lambda l: (l,) + (0,) * len(wshape))
    return _call(
        body, name=name, grid=(nl,),
        out_shape=tuple(jax.ShapeDtypeStruct(w.shape, F32) for _ in range(4)),
        in_specs=[rspec, wspec, wspec, wspec],
        out_specs=(wspec, wspec, wspec, wspec),
        compiler_params=_cparams(1),
    )(recv, w, m, v)


def _constants():
    lane = np.arange(LANES)
    bd = (lane[:, None] // HEAD_DIM == lane[None, :] // HEAD_DIM).astype(np.float32)
    fold = (np.arange(D)[:, None] // HEAD_DIM == lane[None, :]).astype(np.float32)
    half = ROT_DIM // 2
    inv_freq = ROPE_THETA ** (-jnp.arange(half, dtype=F32) * 2.0 / ROT_DIM)
    invf = jnp.tile(inv_freq, LANES // half).reshape(1, LANES)
    return jnp.asarray(bd, BF16), jnp.asarray(fold, BF16), invf


def kernel(x, c, positions, ada_w, ada_b, norm_g, attn_w_in, attn_q_norm, attn_k_norm, attn_sinks, attn_w_out, pool_w_in, pool_w_group, pool_scale, pool_w_out, loss_target, m_ada_w, m_ada_b, m_norm_g, m_attn_w_in, m_attn_q_norm, m_attn_k_norm, m_attn_sinks, m_attn_w_out, m_pool_w_in, m_pool_w_group, m_pool_scale, m_pool_w_out, v_ada_w, v_ada_b, v_norm_g, v_attn_w_in, v_attn_q_norm, v_attn_k_norm, v_attn_sinks, v_attn_w_out, v_pool_w_in, v_pool_w_group, v_pool_scale, v_pool_w_out):
    seq = x.shape[1]
    me = 4 * lax.axis_index("x") + 2 * lax.axis_index("y") + lax.axis_index("c")
    bd, fold, invf = _constants()
    pos_col = positions.reshape(seq, 1)
    t_mm = min(512, seq)
    t_bw = min(256, seq)
    shard = pool_scale.shape[1]
    cols = ada_w.shape[2]

    first = jnp.concatenate([c, jnp.pad(pool_scale, ((0, 0), (0, D - shard))), jnp.zeros((5, D), F32)], axis=0)
    first = _allgather_small(first, "allgather_c")
    c_all = first[:, 0, :]
    scale_full = jnp.transpose(first[:, 1:3, :shard], (1, 0, 2)).reshape(2, D)
    mod_part = _ada_forward(c_all, ada_w)
    mod_all = _allgather_small(mod_part.reshape(DEPTH * N_DEV, cols), "allgather_mod")
    mod_all = mod_all.reshape(N_DEV, DEPTH, N_DEV, cols)
    mine = lax.dynamic_index_in_dim(mod_all, me, axis=2, keepdims=False)
    mod = jnp.transpose(mine, (1, 0, 2)).reshape(DEPTH, 3 * D) + ada_b
    mod = jnp.pad(mod.reshape(DEPTH, 3, D), ((0, 0), (0, 5), (0, 0)))

    wire = _prep_weights(attn_w_in, attn_w_out, pool_w_in, pool_w_out, pool_w_group)
    axes = (1, 1, 1, 1, 2)
    w_in_t, w_out, p_in_t, p_out, p_grp = _allgather_weights(wire, axes)

    saved = []
    h = x[0]
    for i in range(DEPTH):
        j = i // 2
        s = dict(x=h, ng=norm_g[i:i + 1], md=mod[i])
        if i % 2 == 0:
            s["gain"] = jnp.concatenate([jnp.tile(attn_q_norm[j], N_HEADS), jnp.tile(attn_k_norm[j], N_KV)]).reshape(1, QK_W)
            s["qk_raw"], s["qs"], s["kd"], s["vd"], s["g"] = _attn_in_proj(
                h, pos_col, s["ng"], s["md"], w_in_t, j, s["gain"], invf, bd, t_bw)
            s["o"], s["lse"] = _attn_forward(attn_sinks[j], s["qs"], s["kd"], s["vd"], j)
            h, s["br"] = _attn_out_proj(h, s["o"], s["g"], w_out, j, s["md"], t_mm)
        else:
            s["scale"] = scale_full[j:j + 1]
            s["v"], s["g"] = _pool_in_proj(h, s["ng"], s["md"], p_in_t, j, t_mm)
            h, s["br"] = _pool_mix_out(h, s["v"], s["g"], p_grp, p_out, j, s["scale"], s["md"], t_mm)
        saved.append(s)
    dx, loss_part = _loss_head(h, loss_target[0], t_mm)

    vecs, gates, gains, dsinks, pool_vecs = [None] * DEPTH, [None] * DEPTH, [None] * 2, [None] * 2, [None] * 2
    d_w_in_t, d_w_out, d_p_in_t, d_p_out, d_p_grp = [None] * 2, [None] * 2, [None] * 2, [None] * 2, [None] * 2
    for i in reversed(range(DEPTH)):
        j = i // 2
        s = saved[i]
        if i % 2 == 0:
            dos, dg, delta, d_w_out[j], gates[i] = _attn_out_proj_bwd(dx, s["br"], s["o"], s["g"], w_out, j, s["md"], fold, t_mm)
            sink_row = jnp.pad(attn_sinks[j], (0, LANES - N_HEADS)).reshape(1, LANES)
            dq, dk, dv, dsinks[j] = _attn_backward(sink_row, s["qs"], dos, s["kd"], s["vd"], s["lse"], delta, j)
            dx, d_w_in_t[j], vecs[i], gains[j] = _attn_in_proj_bwd(
                s["x"], dx, pos_col, s["qk_raw"], dq, dk, dv, dg, s["ng"], s["md"], w_in_t, j, s["gain"], invf, bd, t_bw)
        else:
            dpool, dg, d_p_out[j], d_p_grp[j], pool_vecs[j] = _pool_mix_out_bwd(
                dx, s["br"], s["v"], s["g"], p_grp, p_out, j, s["scale"], s["md"], t_mm)
            dx, d_p_in_t[j], vecs[i] = _pool_in_proj_bwd(s["x"], dx, dpool, dg, s["ng"], s["md"], p_in_t, j, t_bw)

    vec = _build_vec(vecs, gates, pool_vecs, gains, dsinks, loss_part)
    vec_all = _allgather_small(vec, "allgather_vec")
    tot, folded = _sum_devices(vec_all)
    loss = tot[24, 0]
    small = dict(
        ada_b=(ada_b, tot[0:12].reshape(DEPTH, 3 * D), m_ada_b, v_ada_b),
        norm_g=(norm_g, tot[12:16], m_norm_g, v_norm_g),
        q_norm=(attn_q_norm, folded[0:2, :HEAD_DIM], m_attn_q_norm, v_attn_q_norm),
        k_norm=(attn_k_norm, folded[2:4, :HEAD_DIM], m_attn_k_norm, v_attn_k_norm),
        sinks=(attn_sinks, tot[20:22, :N_HEADS], m_attn_sinks, v_attn_sinks),
        pool_scale=(pool_scale, lax.dynamic_slice(tot, (22, me * shard), (2, shard)), m_pool_scale, v_pool_scale),
    )
    res = {k: (a[1],) + tuple(_adamw_small("adamw_" + k, *a)) for k, a in small.items()}

    dmod_all = vec_all[:, 0:12, :].reshape(N_DEV, DEPTH, 3 * D)
    dmod_mine = lax.dynamic_slice_in_dim(dmod_all, me * cols, cols, axis=2)
    dmod_mine = jnp.pad(jnp.transpose(dmod_mine, (1, 0, 2)), ((0, 0), (0, N_DEV), (0, 0)))
    res["ada_w"] = _ada_backward_adamw(jnp.pad(c_all, ((0, N_DEV), (0, 0))), dmod_mine, ada_w, m_ada_w, v_ada_w)

    fulls = tuple(jnp.stack(p) for p in (d_w_in_t, d_w_out, d_p_in_t, d_p_out, d_p_grp))
    r_in, r_out, rp_in, rp_out, rp_grp = _exchange_grads(fulls, axes)
    res["attn_w_in"] = _adamw_shards("adamw_attn_w_in", r_in, attn_w_in, m_attn_w_in, v_attn_w_in, True)
    res["attn_w_out"] = _adamw_shards("adamw_attn_w_out", r_out, attn_w_out, m_attn_w_out, v_attn_w_out, False)
    res["pool_w_in"] = _adamw_shards("adamw_pool_w_in", rp_in, pool_w_in, m_pool_w_in, v_pool_w_in, True)
    res["pool_w_out"] = _adamw_shards("adamw_pool_w_out", rp_out, pool_w_out, m_pool_w_out, v_pool_w_out, False)
    res["pool_w_group"] = _adamw_shards("adamw_pool_w_group", rp_grp, pool_w_group, m_pool_w_group, v_pool_w_group, False)

    order = ("ada_w", "ada_b", "norm_g", "attn_w_in", "q_norm", "k_norm", "sinks", "attn_w_out", "pool_w_in",
             "pool_w_group", "pool_scale", "pool_w_out")
    return (loss, dx[None], *[res[k][0] for k in order], *[res[k][1] for k in order], *[res[k][2] for k in order],
            *[res[k][3] for k in order])
```

```python
import functools

import numpy as np
import jax
import jax.numpy as jnp
from jax import lax
from jax.experimental import pallas as pl
from jax.experimental.pallas import tpu as pltpu

F32 = jnp.float32
BF16 = jnp.bfloat16
MESH = pl.DeviceIdType.MESH

N_DEV = 8
D = 1024
DEPTH = 4
HEAD_DIM = 64
N_HEADS = 16
N_KV = 4
QK_W = 1280
ATTN_IN = 2560
POOL_IN = 2048
QBLK = 128
POOL_WINDOWS = (2, 4, 8, 16)
HALO = 16
ROPE_THETA = 500000.0
ROT_DIM = 16
NORM_EPS = 1e-6
ADAM_LR = 0.001
ADAM_B1 = 0.9
ADAM_B2 = 0.999
ADAM_EPS = 1e-08
ADAM_WD = 0.01
ADAM_STEP = 10

LANES = 128
VMEM_LIMIT = 56 * 2**20
VEC_ROWS = 32


def _cparams(n_grid=0, **kw):
    if n_grid:
        kw["dimension_semantics"] = ("arbitrary",) * n_grid
    return pltpu.CompilerParams(vmem_limit_bytes=VMEM_LIMIT, **kw)


def _call(body, **kw):
    return pl.pallas_call(body, **kw)


def _const_spec(shape):
    nd = len(shape)
    return pl.BlockSpec(shape, lambda *_: (0,) * nd, pipeline_mode=pl.Buffered(1))


def _dot(a, b):
    return jnp.dot(a, b, preferred_element_type=F32)


def _dot_nt(a, b):
    return lax.dot_general(a, b, (((1,), (1,)), ((), ())), preferred_element_type=F32)


def _dot_tn(a, b):
    return lax.dot_general(a, b, (((0,), (0,)), ((), ())), preferred_element_type=F32)


def _dot_split(x, m):
    hi = x.astype(BF16)
    lo = (x - hi.astype(F32)).astype(BF16)
    return _dot(hi, m) + _dot(lo, m)


def _sigmoid(g):
    return 1.0 / (1.0 + jnp.exp(-g))


def _norm_mod(x, ng, sc, sh):
    r = lax.rsqrt(jnp.mean(x * x, axis=-1, keepdims=True) + NORM_EPS)
    xh = x * r
    h = (xh * ng) * (1.0 + sc) + sh
    return xh, r, h


def _rope_tables(pos_col, invf_row, rows):
    ang = pos_col.astype(F32) * invf_row
    l64 = lax.broadcasted_iota(jnp.int32, (rows, LANES), 1) & (HEAD_DIM - 1)
    cs, sn = jnp.cos(ang), jnp.sin(ang)
    cos_t = jnp.where(l64 < ROT_DIM, cs, 1.0)
    sin_a = jnp.where(l64 < ROT_DIM // 2, -sn, 0.0)
    sin_b = jnp.where((l64 >= ROT_DIM // 2) & (l64 < ROT_DIM), sn, 0.0)
    return cos_t, sin_a, sin_b


def _rope(y, tabs):
    cos_t, sin_a, sin_b = tabs
    return y * cos_t + pltpu.roll(y, LANES - ROT_DIM // 2, 1) * sin_a + pltpu.roll(y, ROT_DIM // 2, 1) * sin_b


def _rope_bwd(dy, tabs):
    cos_t, sin_a, sin_b = tabs
    return dy * cos_t + pltpu.roll(dy * sin_a, ROT_DIM // 2, 1) + pltpu.roll(dy * sin_b, LANES - ROT_DIM // 2, 1)


def _low_half(rows):
    return lax.broadcasted_iota(jnp.int32, (rows, LANES), 1) < HEAD_DIM


def _adamw(w, g, m, v):
    m = ADAM_B1 * m + (1.0 - ADAM_B1) * g
    v = ADAM_B2 * v + (1.0 - ADAM_B2) * (g * g)
    m_hat = m / (1.0 - ADAM_B1 ** ADAM_STEP)
    v_hat = v / (1.0 - ADAM_B2 ** ADAM_STEP)
    delta = -ADAM_LR * (m_hat / (jnp.sqrt(v_hat) + ADAM_EPS) + ADAM_WD * w)
    return delta, m, v


def _my_position():
    x, y, c = lax.axis_index("x"), lax.axis_index("y"), lax.axis_index("c")
    return x, y, c, 4 * x + 2 * y + c


def _peers(x, y, c):
    out = []
    for k in range(1, N_DEV):
        px = 1 - x if k & 4 else x
        py = 1 - y if k & 2 else y
        pc = 1 - c if k & 1 else c
        out.append(((px, py, pc), 4 * px + 2 * py + pc))
    return out


def _allgather_small(v, name):
    rows, cols = v.shape

    def body(v_ref, out_ref, send_sems, recv_sems, local_sem):
        x, y, c, me = _my_position()
        local = pltpu.make_async_copy(v_ref, out_ref.at[me], local_sem)
        local.start()
        sends = []
        for k, (peer, _) in enumerate(_peers(x, y, c)):
            cp = pltpu.make_async_remote_copy(v_ref, out_ref.at[me], send_sems.at[k], recv_sems.at[k],
                                              device_id=peer, device_id_type=MESH)
            cp.start()
            sends.append(cp)
        for k, (peer, idx) in enumerate(_peers(x, y, c)):
            pltpu.make_async_remote_copy(v_ref, out_ref.at[idx], send_sems.at[k], recv_sems.at[k],
                                         device_id=peer, device_id_type=MESH).wait_recv()
        for cp in sends:
            cp.wait_send()
        local.wait()

    return _call(
        body, name=name,
        out_shape=jax.ShapeDtypeStruct((N_DEV, rows, cols), F32),
        in_specs=[pl.BlockSpec(memory_space=pltpu.VMEM)],
        out_specs=pl.BlockSpec(memory_space=pltpu.VMEM),
        scratch_shapes=[pltpu.SemaphoreType.DMA((N_DEV - 1,)), pltpu.SemaphoreType.DMA((N_DEV - 1,)),
                        pltpu.SemaphoreType.DMA(())],
        compiler_params=_cparams(),
    )(v)


def _shard_rows(ref, idx, rows, axis):
    sl = [slice(None)] * len(ref.shape)
    sl[axis] = pl.ds(idx * rows, rows)
    return ref.at[tuple(sl)]


def _own_and_peer_rows(ref, me, idx, axis):
    rows = ref.shape[axis] // N_DEV
    return _shard_rows(ref, me, rows, axis), _shard_rows(ref, idx, rows, axis)


HBM_SPEC = pl.BlockSpec(memory_space=pltpu.HBM)
SEM_SPEC = pl.BlockSpec(memory_space=pltpu.SEMAPHORE)
ANY_SPEC = pl.BlockSpec(memory_space=pl.ANY)
DATAFLOW = pltpu.SideEffectType.DATAFLOW_SIDE_EFFECTING


def _hbm(a):
    return pltpu.with_memory_space_constraint(a, pltpu.HBM)


def _gather_start(layers, axes):
    flat = [a for arrs in layers for a in arrs]
    flat_axes = [ax for axs in axes for ax in axs]
    n, nl = len(flat), len(layers)

    def body(*refs):
        ins, sems, token = refs[:n], refs[n:n + 2 * nl], refs[-1]
        x, y, c, me = _my_position()
        a0 = 0
        for li, arrs in enumerate(layers):
            for k, (peer, _) in enumerate(_peers(x, y, c)):
                for a in range(len(arrs)):
                    rows, _ = _own_and_peer_rows(ins[a0 + a], me, me, flat_axes[a0 + a])
                    pltpu.make_async_remote_copy(rows, rows, sems[2 * li].at[k * len(arrs) + a],
                                                 sems[2 * li + 1].at[k * len(arrs) + a],
                                                 device_id=peer, device_id_type=MESH).start()
            a0 += len(arrs)
        token[...] = jnp.zeros_like(token)

    sem_shapes = []
    for arrs in layers:
        sem_shapes += [pltpu.SemaphoreType.DMA(((N_DEV - 1) * len(arrs),))] * 2
    out = _call(
        body, name="gather_start",
        out_shape=(*sem_shapes, *[pltpu.HBM(a.shape, a.dtype) for a in flat], jax.ShapeDtypeStruct((8, LANES), F32)),
        in_specs=[HBM_SPEC] * n,
        out_specs=(*[SEM_SPEC] * (2 * nl), *[HBM_SPEC] * n, pl.BlockSpec(memory_space=pltpu.VMEM)),
        input_output_aliases={a: 2 * nl + a for a in range(n)},
        compiler_params=_cparams(has_side_effects=DATAFLOW),
    )(*[_hbm(a) for a in flat])
    per_layer, a0 = [], 0
    for li, arrs in enumerate(layers):
        per_layer.append((out[2 * li], out[2 * li + 1], list(out[2 * nl + a0:2 * nl + a0 + len(arrs)])))
        a0 += len(arrs)
    return per_layer, out[-1]


def _gather_wait(started, axes, after, name):
    send_sems, recv_sems, arrs = started
    n = len(arrs)

    def body(*refs):
        ins, send_ref, recv_ref = refs[:n], refs[n], refs[n + 1]
        x, y, c, me = _my_position()
        for k, (peer, idx) in enumerate(_peers(x, y, c)):
            for a in range(n):
                own, theirs = _own_and_peer_rows(ins[a], me, idx, axes[a])
                cp = pltpu.make_async_remote_copy(own, theirs, send_ref.at[k * n + a], recv_ref.at[k * n + a],
                                                  device_id=peer, device_id_type=MESH)
                cp.wait_send()
                cp.wait_recv()

    return _call(
        body, name=name,
        out_shape=tuple(pltpu.HBM(a.shape, a.dtype) for a in arrs),
        in_specs=[HBM_SPEC] * n + [SEM_SPEC, SEM_SPEC, ANY_SPEC],
        out_specs=tuple([HBM_SPEC] * n),
        input_output_aliases={a: a for a in range(n)},
        compiler_params=_cparams(has_side_effects=DATAFLOW),
    )(*arrs, send_sems, recv_sems, after)


def _scatter_start(fulls, axes, name):
    n = len(fulls)
    lands = []
    for f, ax in zip(fulls, axes):
        shp = list(f.shape)
        shp[ax] //= N_DEV
        lands.append(_hbm(lax.empty((N_DEV - 1,) + tuple(shp), f.dtype)))

    def body(*refs):
        srcs, dsts, send_ref, recv_ref, token = refs[:n], refs[n:2 * n], refs[2 * n], refs[2 * n + 1], refs[-1]
        x, y, c, me = _my_position()
        for k, (peer, idx) in enumerate(_peers(x, y, c)):
            for a in range(n):
                _, theirs = _own_and_peer_rows(srcs[a], me, idx, axes[a])
                pltpu.make_async_remote_copy(theirs, dsts[a].at[k], send_ref.at[k * n + a], recv_ref.at[k * n + a],
                                             device_id=peer, device_id_type=MESH).start()
        token[...] = jnp.zeros_like(token)

    sem = pltpu.SemaphoreType.DMA(((N_DEV - 1) * n,))
    out = _call(
        body, name=name,
        out_shape=(sem, sem, *[pltpu.HBM(a.shape, a.dtype) for a in fulls], *[pltpu.HBM(a.shape, a.dtype) for a in lands],
                   jax.ShapeDtypeStruct((8, LANES), F32)),
        in_specs=[HBM_SPEC] * (2 * n),
        out_specs=(SEM_SPEC, SEM_SPEC, *[HBM_SPEC] * (2 * n), pl.BlockSpec(memory_space=pltpu.VMEM)),
        input_output_aliases={a: 2 + a for a in range(2 * n)},
        compiler_params=_cparams(has_side_effects=DATAFLOW),
    )(*[_hbm(a) for a in fulls], *lands)
    return (out[0], out[1], list(out[2:2 + n]), list(out[2 + n:2 + 2 * n])), out[-1]


def _scatter_wait(started, axes, after, name):
    send_sems, recv_sems, fulls, lands = started
    n = len(fulls)

    def body(*refs):
        srcs, dsts, send_ref, recv_ref = refs[:n], refs[n:2 * n], refs[2 * n], refs[2 * n + 1]
        x, y, c, me = _my_position()
        for k, (peer, idx) in enumerate(_peers(x, y, c)):
            for a in range(n):
                _, theirs = _own_and_peer_rows(srcs[a], me, idx, axes[a])
                cp = pltpu.make_async_remote_copy(theirs, dsts[a].at[k], send_ref.at[k * n + a], recv_ref.at[k * n + a],
                                                  device_id=peer, device_id_type=MESH)
                cp.wait_send()
                cp.wait_recv()

    out = _call(
        body, name=name,
        out_shape=tuple(pltpu.HBM(a.shape, a.dtype) for a in (*fulls, *lands)),
        in_specs=[HBM_SPEC] * (2 * n) + [SEM_SPEC, SEM_SPEC, ANY_SPEC],
        out_specs=tuple([HBM_SPEC] * (2 * n)),
        input_output_aliases={a: a for a in range(2 * n)},
        compiler_params=_cparams(has_side_effects=DATAFLOW),
    )(*fulls, *lands, send_sems, recv_sems, after)
    return list(out[:n]), list(out[n:])


def _prep_weights(me, attn_w_in, attn_w_out, pool_w_in, pool_w_out, pool_w_group):
    nl = attn_w_in.shape[0]

    def body(me_ref, *refs):
        ins, outs = refs[:5 * nl], refs[5 * nl:]
        for j in range(nl):
            awi, awo, pwi, pwo, pwg = ins[5 * j:5 * j + 5]
            o_awi, o_awo, o_pwi, o_pwo, o_pwg = outs[5 * j:5 * j + 5]
            o_awi[...] = awi[...].T.astype(BF16)
            o_awo[...] = awo[...].astype(BF16)
            o_pwi[...] = pwi[...].T.astype(BF16)
            o_pwo[...] = pwo[...].astype(BF16)
            o_pwg[...] = pwg[...].astype(BF16)

    def in_spec(shape, j):
        nd = len(shape)
        return pl.BlockSpec((None,) + tuple(shape), lambda i, me_ref: (j,) + (0,) * nd)

    srcs = (attn_w_in, attn_w_out, pool_w_in, pool_w_out, pool_w_group)
    rows_spec = lambda r: pl.BlockSpec((r, D), lambda i, me_ref: (me_ref[0], 0))
    grp = pool_w_group.shape[1:]
    grp_spec = pl.BlockSpec(grp, lambda i, me_ref: (0, me_ref[0], 0))
    ins, in_specs, out_shapes, out_specs = [], [], [], []
    for j in range(nl):
        ins += list(srcs)
        in_specs += [in_spec(a.shape[1:], j) for a in srcs]
        out_shapes += [(N_DEV * attn_w_in.shape[2], D), (N_DEV * attn_w_out.shape[1], D), (N_DEV * pool_w_in.shape[2], D),
                       (N_DEV * pool_w_out.shape[1], D), (grp[0], N_DEV * grp[1], grp[2])]
        out_specs += [rows_spec(attn_w_in.shape[2]), rows_spec(attn_w_out.shape[1]), rows_spec(pool_w_in.shape[2]),
                      rows_spec(pool_w_out.shape[1]), grp_spec]
    out = _call(
        body, name="prep_weights",
        grid_spec=pltpu.PrefetchScalarGridSpec(num_scalar_prefetch=1, grid=(1,), in_specs=in_specs, out_specs=tuple(out_specs)),
        out_shape=tuple(jax.ShapeDtypeStruct(s, BF16) for s in out_shapes),
        compiler_params=_cparams(1),
    )(me.reshape(1), *ins)
    return [list(out[5 * j:5 * j + 5]) for j in range(nl)]


def _ada_forward(c_all, ada_w):
    cols = ada_w.shape[2]

    def body(c_ref, w_ref, o_ref):
        cv = c_ref[...]
        sc = (cv * _sigmoid(cv)).astype(BF16)
        o_ref[...] = _dot(sc, w_ref[...].astype(BF16))

    return _call(
        body, name="ada_forward", grid=(DEPTH,),
        out_shape=jax.ShapeDtypeStruct((DEPTH, N_DEV, cols), F32),
        in_specs=[pl.BlockSpec((N_DEV, D), lambda i: (0, 0)), pl.BlockSpec((None, D, cols), lambda i: (i, 0, 0))],
        out_specs=pl.BlockSpec((None, N_DEV, cols), lambda i: (i, 0, 0)),
        compiler_params=_cparams(1),
    )(c_all, ada_w)


def _ada_backward_adamw(c_pad, dmod_pad, w, m, v):
    cols = w.shape[2]

    def body(c_ref, dm_ref, w_ref, m_ref, v_ref, g_out, d_out, m_out, v_out):
        cv = c_ref[...]
        sc = (cv * _sigmoid(cv)).astype(BF16)
        g = _dot_tn(sc, dm_ref[...].astype(BF16))
        g_out[...] = g
        d_out[...], m_out[...], v_out[...] = _adamw(w_ref[...], g, m_ref[...], v_ref[...])

    wspec = pl.BlockSpec((None, D, cols), lambda i: (i, 0, 0))
    return _call(
        body, name="ada_backward_adamw", grid=(DEPTH,),
        out_shape=tuple(jax.ShapeDtypeStruct(w.shape, F32) for _ in range(4)),
        in_specs=[pl.BlockSpec((2 * N_DEV, D), lambda i: (0, 0)), pl.BlockSpec((None, 2 * N_DEV, cols), lambda i: (i, 0, 0)),
                  wspec, wspec, wspec],
        out_specs=(wspec, wspec, wspec, wspec),
        compiler_params=_cparams(1),
    )(c_pad, dmod_pad, w, m, v)


def _attn_in_proj(x, pos_col, ng, mod, w_t, j, gain, invf, bd, tile):
    seq = x.shape[0]

    def body(x_ref, pos_ref, ng_ref, mod_ref, w_ref, gain_ref, invf_ref, bd_ref, qk_ref, qs_ref, kd_ref, vd_ref, g_ref):
        _, _, h = _norm_mod(x_ref[...], ng_ref[...], mod_ref[1:2, :], mod_ref[0:1, :])
        proj = _dot_nt(h.astype(BF16), w_ref[...])
        qk_ref[...] = proj[:, :QK_W]
        g_ref[...] = proj[:, QK_W + N_KV * HEAD_DIM:]
        tabs = _rope_tables(pos_ref[...], invf_ref[...], tile)
        low = _low_half(tile)
        bdm = bd_ref[...]
        for b in range(QK_W // LANES):
            blk = proj[:, LANES * b:LANES * (b + 1)]
            ms = _dot_split(blk * blk, bdm) * (1.0 / HEAD_DIM)
            y = (blk * lax.rsqrt(ms + NORM_EPS)) * gain_ref[:, LANES * b:LANES * (b + 1)]
            rp = _rope(y, tabs)
            if b < D // LANES:
                rp = rp * (HEAD_DIM ** -0.5)
                qs_ref[:, 2 * LANES * b:2 * LANES * b + LANES] = jnp.where(low, rp, 0.0).astype(BF16)
                qs_ref[:, 2 * LANES * b + LANES:2 * LANES * (b + 1)] = jnp.where(low, 0.0, rp).astype(BF16)
            else:
                kv = 2 * (b - D // LANES)
                sw = pltpu.roll(rp, HEAD_DIM, 1)
                kd_ref[:, LANES * kv:LANES * (kv + 1)] = jnp.where(low, rp, sw).astype(BF16)
                kd_ref[:, LANES * (kv + 1):LANES * (kv + 2)] = jnp.where(low, sw, rp).astype(BF16)
        for b in range(2):
            blk = proj[:, QK_W + LANES * b:QK_W + LANES * (b + 1)]
            sw = pltpu.roll(blk, HEAD_DIM, 1)
            vd_ref[:, LANES * 2 * b:LANES * (2 * b + 1)] = jnp.where(low, blk, sw).astype(BF16)
            vd_ref[:, LANES * (2 * b + 1):LANES * (2 * b + 2)] = jnp.where(low, sw, blk).astype(BF16)

    row = lambda w: pl.BlockSpec((tile, w), lambda i: (i, 0))
    return _call(
        body, name=f"attn_in_proj_{j}", grid=(seq // tile,),
        out_shape=(jax.ShapeDtypeStruct((seq, QK_W), F32), jax.ShapeDtypeStruct((seq, N_HEADS * LANES), BF16),
                   jax.ShapeDtypeStruct((seq, N_KV * LANES), BF16), jax.ShapeDtypeStruct((seq, N_KV * LANES), BF16),
                   jax.ShapeDtypeStruct((seq, D), F32)),
        in_specs=[row(D), row(1), _const_spec((1, D)), _const_spec((8, D)), _const_spec((ATTN_IN, D)),
                  _const_spec((1, QK_W)), _const_spec((1, LANES)), _const_spec((LANES, LANES))],
        out_specs=(row(QK_W), row(N_HEADS * LANES), row(N_KV * LANES), row(N_KV * LANES), row(D)),
        compiler_params=_cparams(1),
    )(x, pos_col, ng, mod, w_t, gain, invf, bd)


def _band_mask(n):
    rows = lax.broadcasted_iota(jnp.int32, (4 * QBLK, 2 * QBLK), 0) & (QBLK - 1)
    cols = lax.broadcasted_iota(jnp.int32, (4 * QBLK, 2 * QBLK), 1)
    diff = QBLK + rows - cols
    first_key = jnp.where(n > 0, 0, QBLK)
    return (diff >= 0) & (diff < QBLK) & (cols >= first_key)


def _stack_heads(ref, kv):
    return jnp.concatenate([ref[:, LANES * h:LANES * (h + 1)] for h in range(4 * kv, 4 * kv + 4)], axis=0)


def _pair_up(st, low):
    return jnp.concatenate([jnp.where(low, st[0:QBLK], st[QBLK:2 * QBLK]),
                            jnp.where(low, st[2 * QBLK:3 * QBLK], st[3 * QBLK:4 * QBLK])], axis=1)


def _head_column(tile, h):
    lane = lax.broadcasted_iota(jnp.int32, tile.shape, 1)
    return jnp.sum(jnp.where(lane == h, tile, 0.0), axis=1, keepdims=True)


def _attn_forward(sinks, qs, kd, vd, j):
    seq = qs.shape[0]
    nb = seq // QBLK

    def body(sink_ref, q_ref, kp_ref, kc_ref, vp_ref, vc_ref, o_ref, lse_ref):
        n = pl.program_id(0)
        ok = _band_mask(n)
        low = _low_half(QBLK)
        lane = lax.broadcasted_iota(jnp.int32, (QBLK, LANES), 1)
        rowi = lax.broadcasted_iota(jnp.int32, (4 * QBLK, 1), 0)
        lse_tile = jnp.zeros((QBLK, LANES), F32)
        for kv in range(N_KV):
            q = _stack_heads(q_ref, kv)
            kk = jnp.concatenate([kp_ref[:, LANES * kv:LANES * (kv + 1)], kc_ref[:, LANES * kv:LANES * (kv + 1)]], axis=0)
            vv = jnp.concatenate([vp_ref[:, LANES * kv:LANES * (kv + 1)], vc_ref[:, LANES * kv:LANES * (kv + 1)]], axis=0)
            s = jnp.where(ok, _dot_nt(q, kk), -1e30)
            sink = jnp.where(rowi < QBLK, sink_ref[4 * kv],
                             jnp.where(rowi < 2 * QBLK, sink_ref[4 * kv + 1],
                                       jnp.where(rowi < 3 * QBLK, sink_ref[4 * kv + 2], sink_ref[4 * kv + 3])))
            m = jnp.maximum(jnp.max(s, axis=1, keepdims=True), sink)
            p = jnp.exp(s - m)
            den = jnp.sum(p, axis=1, keepdims=True) + jnp.exp(sink - m)
            o_st = _dot((p / den).astype(BF16), vv)
            o_ref[:, 2 * LANES * kv:2 * LANES * (kv + 1)] = _pair_up(o_st, low)
            lse = m + jnp.log(den)
            for r in range(4):
                lse_tile = jnp.where(lane == 4 * kv + r, lse[QBLK * r:QBLK * (r + 1)], lse_tile)
        lse_ref[...] = lse_tile

    blk = lambda w: pl.BlockSpec((QBLK, w), lambda n: (n, 0))
    prev = lambda w: pl.BlockSpec((QBLK, w), lambda n: (jnp.maximum(n - 1, 0), 0))
    return _call(
        body, name=f"attn_forward_{j}", grid=(nb,),
        out_shape=(jax.ShapeDtypeStruct((seq, D), F32), jax.ShapeDtypeStruct((seq, LANES), F32)),
        in_specs=[pl.BlockSpec(memory_space=pltpu.SMEM), blk(N_HEADS * LANES), prev(N_KV * LANES), blk(N_KV * LANES),
                  prev(N_KV * LANES), blk(N_KV * LANES)],
        out_specs=(blk(D), blk(LANES)),
        compiler_params=_cparams(1),
    )(sinks, qs, kd, kd, vd, vd)


def _attn_out_proj(x, o, g, w, j, mod, tile):
    seq = x.shape[0]

    def body(x_ref, o_ref, g_ref, w_ref, mod_ref, xo_ref, br_ref):
        gv = g_ref[...]
        u = (o_ref[...] * (gv * _sigmoid(gv))).astype(BF16)
        br = _dot(u, w_ref[...])
        br_ref[...] = br
        xo_ref[...] = x_ref[...] + mod_ref[2:3, :] * br

    row = pl.BlockSpec((tile, D), lambda i: (i, 0))
    return _call(
        body, name=f"attn_out_proj_{j}", grid=(seq // tile,),
        out_shape=(jax.ShapeDtypeStruct((seq, D), F32), jax.ShapeDtypeStruct((seq, D), F32)),
        in_specs=[row, row, row, _const_spec((D, D)), _const_spec((8, D))],
        out_specs=(row, row),
        compiler_params=_cparams(1),
    )(x, o, g, w, mod)


def _attn_out_proj_bwd(dxn, br, o, g, w, j, mod, fold, tile):
    seq = dxn.shape[0]
    steps = seq // tile

    def body(dxn_ref, br_ref, o_ref, g_ref, w_ref, mod_ref, fold_ref, do_ref, dg_ref, delta_ref, dw_ref, dgate_ref, dw_acc):
        i = pl.program_id(0)

        @pl.when(i == 0)
        def _():
            dw_acc[...] = jnp.zeros_like(dw_acc)
            dgate_ref[...] = jnp.zeros_like(dgate_ref)

        dxn_v, ov, gv = dxn_ref[...], o_ref[...], g_ref[...]
        dgate_ref[...] += jnp.sum(dxn_v * br_ref[...], axis=0, keepdims=True)
        dbr = (dxn_v * mod_ref[2:3, :]).astype(BF16)
        du = _dot_nt(dbr, w_ref[...])
        sg = _sigmoid(gv)
        sl = gv * sg
        dw_acc[...] += _dot_tn((ov * sl).astype(BF16), dbr)
        do = du * sl
        dg_ref[...] = (du * ov * (sg * (1.0 + gv * (1.0 - sg)))).astype(BF16)
        delta_ref[...] = _dot_split(do * ov, fold_ref[...])
        low = _low_half(tile)
        for b in range(D // LANES):
            blk = do[:, LANES * b:LANES * (b + 1)]
            do_ref[:, 2 * LANES * b:2 * LANES * b + LANES] = jnp.where(low, blk, 0.0).astype(BF16)
            do_ref[:, 2 * LANES * b + LANES:2 * LANES * (b + 1)] = jnp.where(low, 0.0, blk).astype(BF16)

        @pl.when(i == steps - 1)
        def _():
            dw_ref[...] = dw_acc[...].astype(BF16)

    row = lambda w_: pl.BlockSpec((tile, w_), lambda i: (i, 0))
    return _call(
        body, name=f"attn_out_proj_bwd_{j}", grid=(steps,),
        out_shape=(jax.ShapeDtypeStruct((seq, N_HEADS * LANES), BF16), jax.ShapeDtypeStruct((seq, D), BF16),
                   jax.ShapeDtypeStruct((seq, LANES), F32), jax.ShapeDtypeStruct((D, D), BF16),
                   jax.ShapeDtypeStruct((1, D), F32)),
        in_specs=[row(D), row(D), row(D), row(D), _const_spec((D, D)), _const_spec((8, D)), _const_spec((D, LANES))],
        out_specs=(row(N_HEADS * LANES), row(D), row(LANES), pl.BlockSpec((D, D), lambda i: (0, 0)),
                   pl.BlockSpec((1, D), lambda i: (0, 0))),
        scratch_shapes=[pltpu.VMEM((D, D), F32)],
        compiler_params=_cparams(1),
    )(dxn, br, o, g, w, mod, fold)


def _attn_backward(sink_row, qs, dos, kd, vd, lse, delta, j):
    seq = qs.shape[0]
    nb = seq // QBLK

    def body(sink_ref, q_ref, do_ref, kp_ref, kc_ref, vp_ref, vc_ref, lse_ref, delta_ref,
             dq_ref, dk_ref, dv_ref, dsink_ref, carry_k, carry_v):
        n = pl.program_id(0)

        @pl.when(n == 0)
        def _():
            carry_k[...] = jnp.zeros_like(carry_k)
            carry_v[...] = jnp.zeros_like(carry_v)
            dsink_ref[...] = jnp.zeros_like(dsink_ref)

        @pl.when(n < nb)
        def _():
            ok = _band_mask(n)
            low = _low_half(QBLK)
            lse_t, delta_t = lse_ref[...], delta_ref[...]
            dsink_ref[...] -= jnp.sum(jnp.exp(sink_ref[...] - lse_t) * delta_t, axis=0, keepdims=True)
            dk_parts, dv_parts = [], []
            for kv in range(N_KV):
                q = _stack_heads(q_ref, kv)
                do = _stack_heads(do_ref, kv)
                kk = jnp.concatenate([kp_ref[:, LANES * kv:LANES * (kv + 1)], kc_ref[:, LANES * kv:LANES * (kv + 1)]], axis=0)
                vv = jnp.concatenate([vp_ref[:, LANES * kv:LANES * (kv + 1)], vc_ref[:, LANES * kv:LANES * (kv + 1)]], axis=0)
                lse_c = jnp.concatenate([_head_column(lse_t, 4 * kv + r) for r in range(4)], axis=0)
                dlt_c = jnp.concatenate([_head_column(delta_t, 4 * kv + r) for r in range(4)], axis=0)
                s = jnp.where(ok, _dot_nt(q, kk), -1e30)
                p = jnp.exp(s - lse_c)
                ds = (p * (_dot_nt(do, vv) - dlt_c)).astype(BF16)
                dq_ref[:, 2 * LANES * kv:2 * LANES * (kv + 1)] = _pair_up(_dot(ds, kk), low)
                dkd = _dot_tn(ds, q)
                dvd = _dot_tn(p.astype(BF16), do)
                dk_parts.append(dkd + pltpu.roll(dkd, HEAD_DIM, 1))
                dv_parts.append(dvd + pltpu.roll(dvd, HEAD_DIM, 1))

            def order(parts, lo, hi):
                return jnp.concatenate([jnp.where(low, parts[0][lo:hi], parts[1][lo:hi]),
                                        jnp.where(low, parts[2][lo:hi], parts[3][lo:hi])], axis=1)

            dk_ref[...] = carry_k[...] + order(dk_parts, 0, QBLK)
            dv_ref[...] = (carry_v[...] + order(dv_parts, 0, QBLK)).astype(BF16)
            carry_k[...] = order(dk_parts, QBLK, 2 * QBLK)
            carry_v[...] = order(dv_parts, QBLK, 2 * QBLK)

        @pl.when(n == nb)
        def _():
            dk_ref[...] = carry_k[...]
            dv_ref[...] = carry_v[...].astype(BF16)

    cur = lambda w: pl.BlockSpec((QBLK, w), lambda n: (jnp.minimum(n, nb - 1), 0))
    prev = lambda w: pl.BlockSpec((QBLK, w), lambda n: (jnp.maximum(n - 1, 0), 0))
    kcur = lambda w: pl.BlockSpec((QBLK, w), lambda n: (jnp.minimum(n, nb - 1), 0))
    return _call(
        body, name=f"attn_backward_{j}", grid=(nb + 1,),
        out_shape=(jax.ShapeDtypeStruct((seq, D), F32), jax.ShapeDtypeStruct((seq, N_KV * HEAD_DIM), F32),
                   jax.ShapeDtypeStruct((seq, N_KV * HEAD_DIM), BF16), jax.ShapeDtypeStruct((1, LANES), F32)),
        in_specs=[_const_spec((1, LANES)), cur(N_HEADS * LANES), cur(N_HEADS * LANES), prev(N_KV * LANES), kcur(N_KV * LANES),
                  prev(N_KV * LANES), kcur(N_KV * LANES), cur(LANES), cur(LANES)],
        out_specs=(cur(D), prev(N_KV * HEAD_DIM), prev(N_KV * HEAD_DIM), pl.BlockSpec((1, LANES), lambda n: (0, 0))),
        scratch_shapes=[pltpu.VMEM((QBLK, N_KV * HEAD_DIM), F32), pltpu.VMEM((QBLK, N_KV * HEAD_DIM), F32)],
        compiler_params=_cparams(1),
    )(sink_row, qs, dos, kd, kd, vd, vd, lse, delta)


def _in_proj_tail(x_ref, dxn_ref, ng_ref, mod_ref, w_ref, dproj, dx_ref, dw_acc, vec_acc):
    ng, sc, sh = ng_ref[...], mod_ref[1:2, :], mod_ref[0:1, :]
    xh, r, h = _norm_mod(x_ref[...], ng, sc, sh)
    dh = _dot(dproj, w_ref[...])
    dw_acc[...] += _dot_tn(dproj, h.astype(BF16))
    vec_acc[0:1, :] += jnp.sum(dh, axis=0, keepdims=True)
    vec_acc[1:2, :] += jnp.sum(dh * xh, axis=0, keepdims=True)
    dxh = dh * (ng * (1.0 + sc))
    dx_ref[...] = dxn_ref[...] + r * (dxh - xh * jnp.mean(dxh * xh, axis=-1, keepdims=True))


def _tail_finish(ng_ref, mod_ref, dw_ref, vec_ref, dw_acc, vec_acc):
    dw_ref[...] = dw_acc[...].astype(BF16)
    a = vec_acc[1:2, :]
    vec_ref[...] = jnp.zeros_like(vec_ref)
    vec_ref[0:1, :] = vec_acc[0:1, :]
    vec_ref[1:2, :] = a * ng_ref[...]
    vec_ref[3:4, :] = a * (1.0 + mod_ref[1:2, :])


def _attn_in_proj_bwd(x, dxn, pos_col, qk_raw, dq, dk, dv, dg, ng, mod, w_t, j, gain, invf, bd, tile):
    seq = x.shape[0]
    steps = seq // tile

    def body(x_ref, dxn_ref, pos_ref, qk_ref, dq_ref, dk_ref, dv_ref, dg_ref, ng_ref, mod_ref, w_ref, gain_ref, invf_ref,
             bd_ref, dx_ref, dw_ref, vec_ref, dgain_ref, dproj, dw_acc, vec_acc):
        i = pl.program_id(0)

        @pl.when(i == 0)
        def _():
            dw_acc[...] = jnp.zeros_like(dw_acc)
            vec_acc[...] = jnp.zeros_like(vec_acc)
            dgain_ref[...] = jnp.zeros_like(dgain_ref)

        tabs = _rope_tables(pos_ref[...], invf_ref[...], tile)
        bdm = bd_ref[...]
        for b in range(QK_W // LANES):
            cols = slice(LANES * b, LANES * (b + 1))
            raw = qk_ref[:, cols]
            if b < D // LANES:
                dy = dq_ref[:, cols] * (HEAD_DIM ** -0.5)
            else:
                dy = dk_ref[:, LANES * (b - D // LANES):LANES * (b + 1 - D // LANES)]
            dy = _rope_bwd(dy, tabs)
            rr = lax.rsqrt(_dot_split(raw * raw, bdm) * (1.0 / HEAD_DIM) + NORM_EPS)
            xh = raw * rr
            dgain_ref[:, cols] += jnp.sum(dy * xh, axis=0, keepdims=True)
            dxh = dy * gain_ref[:, cols]
            draw = rr * (dxh - xh * (_dot_split(dxh * xh, bdm) * (1.0 / HEAD_DIM)))
            dproj[:, cols] = draw.astype(BF16)
        dproj[:, QK_W:QK_W + N_KV * HEAD_DIM] = dv_ref[...]
        dproj[:, QK_W + N_KV * HEAD_DIM:] = dg_ref[...]
        _in_proj_tail(x_ref, dxn_ref, ng_ref, mod_ref, w_ref, dproj[...], dx_ref, dw_acc, vec_acc)

        @pl.when(i == steps - 1)
        def _():
            _tail_finish(ng_ref, mod_ref, dw_ref, vec_ref, dw_acc, vec_acc)

    row = lambda w, dt=None: pl.BlockSpec((tile, w), lambda i: (i, 0))
    fixed = lambda shape: pl.BlockSpec(shape, lambda i: (0,) * len(shape))
    return _call(
        body, name=f"attn_in_proj_bwd_{j}", grid=(steps,),
        out_shape=(jax.ShapeDtypeStruct((seq, D), F32), jax.ShapeDtypeStruct((ATTN_IN, D), BF16),
                   jax.ShapeDtypeStruct((8, D), F32), jax.ShapeDtypeStruct((1, QK_W), F32)),
        in_specs=[row(D), row(D), row(1), row(QK_W), row(D), row(N_KV * HEAD_DIM), row(N_KV * HEAD_DIM), row(D),
                  _const_spec((1, D)), _const_spec((8, D)), _const_spec((ATTN_IN, D)), _const_spec((1, QK_W)),
                  _const_spec((1, LANES)), _const_spec((LANES, LANES))],
        out_specs=(row(D), fixed((ATTN_IN, D)), fixed((8, D)), fixed((1, QK_W))),
        scratch_shapes=[pltpu.VMEM((tile, ATTN_IN), BF16), pltpu.VMEM((ATTN_IN, D), F32), pltpu.VMEM((8, D), F32)],
        compiler_params=_cparams(1),
    )(x, dxn, pos_col, qk_raw, dq, dk, dv, dg, ng, mod, w_t, gain, invf, bd)


def _pool_in_proj(x, ng, mod, w_t, j, tile):
    seq = x.shape[0]

    def body(x_ref, ng_ref, mod_ref, w_ref, v_ref, g_ref):
        _, _, h = _norm_mod(x_ref[...], ng_ref[...], mod_ref[1:2, :], mod_ref[0:1, :])
        proj = _dot_nt(h.astype(BF16), w_ref[...])
        v_ref[...] = proj[:, :D]
        g_ref[...] = proj[:, D:]

    row = pl.BlockSpec((tile, D), lambda i: (i, 0))
    return _call(
        body, name=f"pool_in_proj_{j}", grid=(seq // tile,),
        out_shape=(jax.ShapeDtypeStruct((seq, D), F32), jax.ShapeDtypeStruct((seq, D), F32)),
        in_specs=[row, _const_spec((1, D)), _const_spec((8, D)), _const_spec((POOL_IN, D))],
        out_specs=(row, row),
        compiler_params=_cparams(1),
    )(x, ng, mod, w_t)


def _pooled(ext, first, tile):
    t_abs = first + lax.broadcasted_iota(jnp.int32, (tile, 1), 0)
    outs = []
    gw = D // len(POOL_WINDOWS)
    for gi, w in enumerate(POOL_WINDOWS):
        cols = slice(gw * gi, gw * (gi + 1))
        own = ext[HALO:HALO + tile, cols]
        acc = own
        for k in range(1, w):
            acc = acc + ext[HALO - k:HALO - k + tile, cols]
        cnt = jnp.minimum(t_abs + 1, w).astype(F32)
        outs.append(acc / cnt - own)
    return jnp.concatenate(outs, axis=1)


def _fill_ext(ext, halo_ref, v_ref, i, tile):
    ext[0:HALO, :] = jnp.where(i == 0, 0.0, halo_ref[...])
    ext[HALO:HALO + tile, :] = v_ref[...]


def _group_mix(pb, wg_ref):
    gw = D // len(POOL_WINDOWS)
    return jnp.concatenate([_dot(pb[:, gw * gi:gw * (gi + 1)], wg_ref[gi]) for gi in range(len(POOL_WINDOWS))], axis=1)


def _pool_mix_out(x, v, g, wg, w_out, j, scale, mod, tile):
    seq = x.shape[0]

    def body(x_ref, v_ref, halo_ref, g_ref, wg_ref, w_ref, scale_ref, mod_ref, xo_ref, br_ref, ext):
        i = pl.program_id(0)
        _fill_ext(ext, halo_ref, v_ref, i, tile)
        pb = _pooled(ext, i * tile, tile).astype(BF16)
        ms = _group_mix(pb, wg_ref) * scale_ref[...]
        gv = g_ref[...]
        u = (ms * (gv * _sigmoid(gv))).astype(BF16)
        br = _dot(u, w_ref[...])
        br_ref[...] = br
        xo_ref[...] = x_ref[...] + mod_ref[2:3, :] * br

    row = pl.BlockSpec((tile, D), lambda i: (i, 0))
    halo = pl.BlockSpec((HALO, D), lambda i: (jnp.maximum(i * (tile // HALO) - 1, 0), 0))
    return _call(
        body, name=f"pool_mix_out_{j}", grid=(seq // tile,),
        out_shape=(jax.ShapeDtypeStruct((seq, D), F32), jax.ShapeDtypeStruct((seq, D), F32)),
        in_specs=[row, row, halo, row, _const_spec(wg.shape), _const_spec((D, D)), _const_spec((1, D)),
                  _const_spec((8, D))],
        out_specs=(row, row),
        scratch_shapes=[pltpu.VMEM((tile + HALO, D), F32)],
        compiler_params=_cparams(1),
    )(x, v, v, g, wg, w_out, scale, mod)


def _pool_mix_out_bwd(dxn, br, v, g, wg, w_out, j, scale, mod, tile):
    seq = dxn.shape[0]
    steps = seq // tile
    ng_ = len(POOL_WINDOWS)
    gw = D // ng_

    def body(dxn_ref, br_ref, v_ref, halo_ref, g_ref, wg_ref, w_ref, scale_ref, mod_ref,
             dpool_ref, dg_ref, dw_ref, dwg_ref, vec_ref, ext, dw_acc, dwg_acc):
        i = pl.program_id(0)

        @pl.when(i == 0)
        def _():
            dw_acc[...] = jnp.zeros_like(dw_acc)
            dwg_acc[...] = jnp.zeros_like(dwg_acc)
            vec_ref[...] = jnp.zeros_like(vec_ref)

        _fill_ext(ext, halo_ref, v_ref, i, tile)
        pb = _pooled(ext, i * tile, tile).astype(BF16)
        mixed = _group_mix(pb, wg_ref)
        scale = scale_ref[...]
        ms = mixed * scale
        gv, dxn_v = g_ref[...], dxn_ref[...]
        sg = _sigmoid(gv)
        sl = gv * sg
        vec_ref[0:1, :] += jnp.sum(dxn_v * br_ref[...], axis=0, keepdims=True)
        dbr = (dxn_v * mod_ref[2:3, :]).astype(BF16)
        du = _dot_nt(dbr, w_ref[...])
        dw_acc[...] += _dot_tn((ms * sl).astype(BF16), dbr)
        dms = du * sl
        dg_ref[...] = (du * ms * (sg * (1.0 + gv * (1.0 - sg)))).astype(BF16)
        vec_ref[1:2, :] += jnp.sum(dms * mixed, axis=0, keepdims=True)
        dmx = (dms * scale).astype(BF16)
        for gi in range(ng_):
            cols = slice(gw * gi, gw * (gi + 1))
            dpool_ref[:, cols] = _dot_nt(dmx[:, cols], wg_ref[gi])
            dwg_acc[gi] += _dot_tn(pb[:, cols], dmx[:, cols])

        @pl.when(i == steps - 1)
        def _():
            dw_ref[...] = dw_acc[...].astype(BF16)
            dwg_ref[...] = dwg_acc[...].astype(BF16)

    row = pl.BlockSpec((tile, D), lambda i: (i, 0))
    halo = pl.BlockSpec((HALO, D), lambda i: (jnp.maximum(i * (tile // HALO) - 1, 0), 0))
    fixed = lambda shape: pl.BlockSpec(shape, lambda i: (0,) * len(shape))
    return _call(
        body, name=f"pool_mix_out_bwd_{j}", grid=(steps,),
        out_shape=(jax.ShapeDtypeStruct((seq, D), F32), jax.ShapeDtypeStruct((seq, D), BF16),
                   jax.ShapeDtypeStruct((D, D), BF16), jax.ShapeDtypeStruct((ng_, gw, gw), BF16),
                   jax.ShapeDtypeStruct((8, D), F32)),
        in_specs=[row, row, row, halo, row, _const_spec(wg.shape), _const_spec((D, D)), _const_spec((1, D)),
                  _const_spec((8, D))],
        out_specs=(row, row, fixed((D, D)), fixed((ng_, gw, gw)), fixed((8, D))),
        scratch_shapes=[pltpu.VMEM((tile + HALO, D), F32), pltpu.VMEM((D, D), F32), pltpu.VMEM((ng_, gw, gw), F32)],
        compiler_params=_cparams(1),
    )(dxn, br, v, v, g, wg, w_out, scale, mod)


def _pool_in_proj_bwd(x, dxn, dpool, dg, ng, mod, w_t, j, tile):
    seq = x.shape[0]
    steps = seq // tile
    gw = D // len(POOL_WINDOWS)

    def body(x_ref, dxn_ref, dp_ref, halo_ref, dg_ref, ng_ref, mod_ref, w_ref, dx_ref, dw_ref, vec_ref,
             ext, dproj, dw_acc, vec_acc):
        i = pl.program_id(0)

        @pl.when(i == 0)
        def _():
            dw_acc[...] = jnp.zeros_like(dw_acc)
            vec_acc[...] = jnp.zeros_like(vec_acc)

        t_abs = i * tile + lax.broadcasted_iota(jnp.int32, (tile, 1), 0)
        last = i == steps - 1
        for gi, w in enumerate(POOL_WINDOWS):
            cols = slice(gw * gi, gw * (gi + 1))
            cnt = jnp.minimum(t_abs + 1, w).astype(F32)
            ext[0:tile, cols] = dp_ref[:, cols] / cnt
            ext[tile:tile + HALO, cols] = jnp.where(last, 0.0, halo_ref[:, cols] * (1.0 / w))
        for gi, w in enumerate(POOL_WINDOWS):
            cols = slice(gw * gi, gw * (gi + 1))
            acc = ext[0:tile, cols]
            for k in range(1, w):
                acc = acc + ext[k:k + tile, cols]
            dproj[:, cols] = (acc - dp_ref[:, cols]).astype(BF16)
        dproj[:, D:] = dg_ref[...]
        _in_proj_tail(x_ref, dxn_ref, ng_ref, mod_ref, w_ref, dproj[...], dx_ref, dw_acc, vec_acc)

        @pl.when(last)
        def _():
            _tail_finish(ng_ref, mod_ref, dw_ref, vec_ref, dw_acc, vec_acc)

    row = pl.BlockSpec((tile, D), lambda i: (i, 0))
    halo = pl.BlockSpec((HALO, D), lambda i: (jnp.minimum((i + 1) * (tile // HALO), seq // HALO - 1), 0))
    fixed = lambda shape: pl.BlockSpec(shape, lambda i: (0,) * len(shape))
    return _call(
        body, name=f"pool_in_proj_bwd_{j}", grid=(steps,),
        out_shape=(jax.ShapeDtypeStruct((seq, D), F32), jax.ShapeDtypeStruct((POOL_IN, D), BF16),
                   jax.ShapeDtypeStruct((8, D), F32)),
        in_specs=[row, row, row, halo, row, _const_spec((1, D)), _const_spec((8, D)), _const_spec((POOL_IN, D))],
        out_specs=(row, fixed((POOL_IN, D)), fixed((8, D))),
        scratch_shapes=[pltpu.VMEM((tile + HALO, D), F32), pltpu.VMEM((tile, POOL_IN), BF16), pltpu.VMEM((POOL_IN, D), F32),
                        pltpu.VMEM((8, D), F32)],
        compiler_params=_cparams(1),
    )(x, dxn, dpool, dpool, dg, ng, mod, w_t)


def _loss_head(y, target, tile):
    seq = y.shape[0]

    def body(y_ref, t_ref, dy_ref, loss_ref):
        @pl.when(pl.program_id(0) == 0)
        def _():
            loss_ref[...] = jnp.zeros_like(loss_ref)

        e = y_ref[...] - t_ref[...]
        dy_ref[...] = e * (1.0 / D)
        loss_ref[...] += 0.5 * jnp.sum(jnp.mean(e * e, axis=-1, keepdims=True), axis=0, keepdims=True)

    row = pl.BlockSpec((tile, D), lambda i: (i, 0))
    return _call(
        body, name="loss_head", grid=(seq // tile,),
        out_shape=(jax.ShapeDtypeStruct((seq, D), F32), jax.ShapeDtypeStruct((1, LANES), F32)),
        in_specs=[row, row],
        out_specs=(row, pl.BlockSpec((1, LANES), lambda i: (0, 0))),
        compiler_params=_cparams(1),
    )(y, target)


def _build_vec(vecs, gates, pool_vecs, gains, dsinks, loss_part):
    def body(v0, v1, v2, v3, g0, g2, p0, p1, n0, n1, s0, s1, loss_ref, out):
        out[...] = jnp.zeros_like(out)
        for i, v in enumerate((v0, v1, v2, v3)):
            out[3 * i:3 * i + 2, :] = v[0:2, :]
            out[12 + i:13 + i, :] = v[3:4, :]
        out[2:3, :] = g0[...]
        out[8:9, :] = g2[...]
        for j, (p, n, s) in enumerate(((p0, n0, s0), (p1, n1, s1))):
            out[3 * (2 * j + 1) + 2:3 * (2 * j + 1) + 3, :] = p[0:1, :]
            out[22 + j:23 + j, :] = p[1:2, :]
            out[16 + j:17 + j, :] = n[:, 0:D]
            out[18 + j:19 + j, 0:QK_W - D] = n[:, D:QK_W]
            out[20 + j:21 + j, 0:LANES] = s[...]
        out[24:25, 0:LANES] = loss_ref[...]

    vm = pl.BlockSpec(memory_space=pltpu.VMEM)
    args = (*vecs, gates[0], gates[2], *pool_vecs, *gains, *dsinks, loss_part)
    return _call(
        body, name="build_vec",
        out_shape=jax.ShapeDtypeStruct((VEC_ROWS, D), F32),
        in_specs=[vm] * len(args), out_specs=vm,
        compiler_params=_cparams(),
    )(*args)


def _sum_devices(g):
    rows = g.shape[1]

    def body(g_ref, tot_ref, fold_ref):
        tot = g_ref[0]
        for p in range(1, N_DEV):
            tot = tot + g_ref[p]
        tot_ref[...] = tot
        f = tot[16:24, 0:LANES]
        for b in range(1, D // LANES):
            f = f + tot[16:24, LANES * b:LANES * (b + 1)]
        fold_ref[...] = f + pltpu.roll(f, HEAD_DIM, 1)

    return _call(
        body, name="sum_devices",
        out_shape=(jax.ShapeDtypeStruct((rows, D), F32), jax.ShapeDtypeStruct((8, LANES), F32)),
        in_specs=[pl.BlockSpec(memory_space=pltpu.VMEM)],
        out_specs=(pl.BlockSpec(memory_space=pltpu.VMEM), pl.BlockSpec(memory_space=pltpu.VMEM)),
        compiler_params=_cparams(),
    )(g)


def _adamw_small(name, w, g, m, v):
    def body(w_ref, g_ref, m_ref, v_ref, d_out, m_out, v_out):
        d_out[...], m_out[...], v_out[...] = _adamw(w_ref[...], g_ref[...], m_ref[...], v_ref[...])

    vm = pl.BlockSpec(memory_space=pltpu.VMEM)
    return _call(
        body, name=name,
        out_shape=tuple(jax.ShapeDtypeStruct(w.shape, F32) for _ in range(3)),
        in_specs=[vm] * 4, out_specs=(vm, vm, vm),
        compiler_params=_cparams(),
    )(w, g, m, v)


def _adamw_shards(name, me, fulls, lands, w, m, v, transpose, axis=0):
    nl = w.shape[0]
    wshape = w.shape[1:]
    own_shape = lands[0].shape[1:]

    def body(me_ref, *refs):
        own_refs, land_refs = refs[:nl], refs[nl:2 * nl]
        w_ref, m_ref, v_ref, g_out, d_out, m_out, v_out = refs[2 * nl:]
        layer = pl.program_id(0)
        for l in range(nl):
            @pl.when(layer == l)
            def _(l=l):
                g = own_refs[l][...].astype(F32)
                for k in range(N_DEV - 1):
                    g = g + land_refs[l][k].astype(F32)
                if transpose:
                    g = g.T
                g_out[...] = g
                d_out[...], m_out[...], v_out[...] = _adamw(w_ref[...], g, m_ref[...], v_ref[...])

    def own_index(l_, me_ref):
        idx = [0] * len(own_shape)
        idx[axis] = me_ref[0]
        return tuple(idx)

    own_spec = pl.BlockSpec(tuple(own_shape), own_index)
    land_spec = pl.BlockSpec((N_DEV - 1,) + tuple(own_shape), lambda l_, me_ref: (0,) * (1 + len(own_shape)))
    wspec = pl.BlockSpec((None,) + tuple(wshape), lambda l_, me_ref: (l_,) + (0,) * len(wshape))
    return _call(
        body, name=name,
        grid_spec=pltpu.PrefetchScalarGridSpec(num_scalar_prefetch=1, grid=(nl,),
                                               in_specs=[own_spec] * nl + [land_spec] * nl + [wspec] * 3,
                                               out_specs=(wspec,) * 4),
        out_shape=tuple(jax.ShapeDtypeStruct(w.shape, F32) for _ in range(4)),
        compiler_params=_cparams(1),
    )(me.reshape(1), *fulls, *lands, w, m, v)


def _constants():
    lane = np.arange(LANES)
    bd = (lane[:, None] // HEAD_DIM == lane[None, :] // HEAD_DIM).astype(np.float32)
    fold = (np.arange(D)[:, None] // HEAD_DIM == lane[None, :]).astype(np.float32)
    half = ROT_DIM // 2
    inv_freq = ROPE_THETA ** (-jnp.arange(half, dtype=F32) * 2.0 / ROT_DIM)
    invf = jnp.tile(inv_freq, LANES // half).reshape(1, LANES)
    return jnp.asarray(bd, BF16), jnp.asarray(fold, BF16), invf


def kernel(x, c, positions, ada_w, ada_b, norm_g, attn_w_in, attn_q_norm, attn_k_norm, attn_sinks, attn_w_out, pool_w_in, pool_w_group, pool_scale, pool_w_out, loss_target, m_ada_w, m_ada_b, m_norm_g, m_attn_w_in, m_attn_q_norm, m_attn_k_norm, m_attn_sinks, m_attn_w_out, m_pool_w_in, m_pool_w_group, m_pool_scale, m_pool_w_out, v_ada_w, v_ada_b, v_norm_g, v_attn_w_in, v_attn_q_norm, v_attn_k_norm, v_attn_sinks, v_attn_w_out, v_pool_w_in, v_pool_w_group, v_pool_scale, v_pool_w_out):
    seq = x.shape[1]
    me = 4 * lax.axis_index("x") + 2 * lax.axis_index("y") + lax.axis_index("c")
    bd, fold, invf = _constants()
    pos_col = positions.reshape(seq, 1)
    t_mm = min(512, seq)
    t_bw = min(256, seq)
    shard = pool_scale.shape[1]
    cols = ada_w.shape[2]

    layers = _prep_weights(me, attn_w_in, attn_w_out, pool_w_in, pool_w_out, pool_w_group)
    per_layer = [layers[i // 2][0:2] if i % 2 == 0 else layers[i // 2][2:5] for i in range(DEPTH)]
    ax_of = lambda i: (0, 0) if i % 2 == 0 else (0, 0, 1)
    started, token = _gather_start(per_layer, [ax_of(i) for i in range(DEPTH)])

    first = jnp.concatenate([c, jnp.pad(pool_scale, ((0, 0), (0, D - shard))), jnp.zeros((5, D), F32)], axis=0)
    first = _allgather_small(first + token[0, 0], "allgather_c")
    c_all = first[:, 0, :]
    scale_full = jnp.transpose(first[:, 1:3, :shard], (1, 0, 2)).reshape(2, D)
    mod_part = _ada_forward(c_all, ada_w)
    mod_all = _allgather_small(mod_part.reshape(DEPTH * N_DEV, cols), "allgather_mod")
    mod_all = mod_all.reshape(N_DEV, DEPTH, N_DEV, cols)
    mine = lax.dynamic_index_in_dim(mod_all, me, axis=2, keepdims=False)
    mod = jnp.transpose(mine, (1, 0, 2)).reshape(DEPTH, 3 * D) + ada_b
    mod = jnp.pad(mod.reshape(DEPTH, 3, D), ((0, 0), (0, 5), (0, 0)))

    saved, weights = [], []
    h = x[0]
    for i in range(DEPTH):
        j = i // 2
        s = dict(x=h, ng=norm_g[i:i + 1], md=mod[i])
        weights.append(_gather_wait(started[i], ax_of(i), mod if i == 0 else h, f"gather_wait_{i}"))
        if i % 2 == 0:
            w_in_t, w_out = weights[i]
            s["gain"] = jnp.concatenate([jnp.tile(attn_q_norm[j], N_HEADS), jnp.tile(attn_k_norm[j], N_KV)]).reshape(1, QK_W)
            s["qk_raw"], s["qs"], s["kd"], s["vd"], s["g"] = _attn_in_proj(
                h, pos_col, s["ng"], s["md"], w_in_t, j, s["gain"], invf, bd, t_bw)
            s["o"], s["lse"] = _attn_forward(attn_sinks[j], s["qs"], s["kd"], s["vd"], j)
            h, s["br"] = _attn_out_proj(h, s["o"], s["g"], w_out, j, s["md"], t_mm)
        else:
            p_in_t, p_out, p_grp = weights[i]
            s["scale"] = scale_full[j:j + 1]
            s["v"], s["g"] = _pool_in_proj(h, s["ng"], s["md"], p_in_t, j, t_mm)
            h, s["br"] = _pool_mix_out(h, s["v"], s["g"], p_grp, p_out, j, s["scale"], s["md"], t_mm)
        saved.append(s)
    dx, loss_part = _loss_head(h, loss_target[0], t_mm)

    vecs, gates, gains, dsinks, pool_vecs = [None] * DEPTH, [None] * DEPTH, [None] * 2, [None] * 2, [None] * 2
    sent = [None] * DEPTH
    token = None
    for i in reversed(range(DEPTH)):
        j = i // 2
        s = saved[i]
        md = s["md"] if token is None else s["md"] + token[0, 0]
        if i % 2 == 0:
            w_in_t, w_out = weights[i]
            dos, dg, delta, d_w_out, gates[i] = _attn_out_proj_bwd(dx, s["br"], s["o"], s["g"], w_out, j, md, fold, t_mm)
            sink_row = jnp.pad(attn_sinks[j], (0, LANES - N_HEADS)).reshape(1, LANES)
            dq, dk, dv, dsinks[j] = _attn_backward(sink_row, s["qs"], dos, s["kd"], s["vd"], s["lse"], delta, j)
            dx, d_w_in_t, vecs[i], gains[j] = _attn_in_proj_bwd(
                s["x"], dx, pos_col, s["qk_raw"], dq, dk, dv, dg, s["ng"], md, w_in_t, j, s["gain"], invf, bd, t_bw)
            grads = [d_w_in_t, d_w_out]
        else:
            p_in_t, p_out, p_grp = weights[i]
            dpool, dg, d_p_out, d_p_grp, pool_vecs[j] = _pool_mix_out_bwd(
                dx, s["br"], s["v"], s["g"], p_grp, p_out, j, s["scale"], md, t_mm)
            dx, d_p_in_t, vecs[i] = _pool_in_proj_bwd(s["x"], dx, dpool, dg, s["ng"], md, p_in_t, j, t_bw)
            grads = [d_p_in_t, d_p_out, d_p_grp]
        sent[i], token = _scatter_start(grads, ax_of(i), f"scatter_start_{i}")

    vec = _build_vec(vecs, gates, pool_vecs, gains, dsinks, loss_part + token[0:1, :])
    vec_all = _allgather_small(vec, "allgather_vec")
    tot, folded = _sum_devices(vec_all)
    loss = tot[24, 0]
    small = dict(
        ada_b=(ada_b, tot[0:12].reshape(DEPTH, 3 * D), m_ada_b, v_ada_b),
        norm_g=(norm_g, tot[12:16], m_norm_g, v_norm_g),
        q_norm=(attn_q_norm, folded[0:2, :HEAD_DIM], m_attn_q_norm, v_attn_q_norm),
        k_norm=(attn_k_norm, folded[2:4, :HEAD_DIM], m_attn_k_norm, v_attn_k_norm),
        sinks=(attn_sinks, tot[20:22, :N_HEADS], m_attn_sinks, v_attn_sinks),
        pool_scale=(pool_scale, lax.dynamic_slice(tot, (22, me * shard), (2, shard)), m_pool_scale, v_pool_scale),
    )
    res = {k: (a[1],) + tuple(_adamw_small("adamw_" + k, *a)) for k, a in small.items()}

    dmod_all = vec_all[:, 0:12, :].reshape(N_DEV, DEPTH, 3 * D)
    dmod_mine = lax.dynamic_slice_in_dim(dmod_all, me * cols, cols, axis=2)
    dmod_mine = jnp.pad(jnp.transpose(dmod_mine, (1, 0, 2)), ((0, 0), (0, N_DEV), (0, 0)))
    res["ada_w"] = _ada_backward_adamw(jnp.pad(c_all, ((0, N_DEV), (0, 0))), dmod_mine, ada_w, m_ada_w, v_ada_w)

    got = [None] * DEPTH
    for i in (3, 1):
        got[i] = _scatter_wait(sent[i], ax_of(i), res["ada_w"][0], f"scatter_wait_{i}")
    pick = lambda layers_, a: ([got[i][0][a] for i in layers_], [got[i][1][a] for i in layers_])
    res["pool_w_in"] = _adamw_shards("adamw_pool_w_in", me, *pick((1, 3), 0), pool_w_in, m_pool_w_in, v_pool_w_in, True)
    res["pool_w_out"] = _adamw_shards("adamw_pool_w_out", me, *pick((1, 3), 1), pool_w_out, m_pool_w_out, v_pool_w_out, False)
    res["pool_w_group"] = _adamw_shards("adamw_pool_w_group", me, *pick((1, 3), 2), pool_w_group, m_pool_w_group,
                                        v_pool_w_group, False, axis=1)
    for i in (2, 0):
        got[i] = _scatter_wait(sent[i], ax_of(i), res["pool_w_group"][0], f"scatter_wait_{i}")
    res["attn_w_in"] = _adamw_shards("adamw_attn_w_in", me, *pick((0, 2), 0), attn_w_in, m_attn_w_in, v_attn_w_in, True)
    res["attn_w_out"] = _adamw_shards("adamw_attn_w_out", me, *pick((0, 2), 1), attn_w_out, m_attn_w_out, v_attn_w_out, False)

    order = ("ada_w", "ada_b", "norm_g", "attn_w_in", "q_norm", "k_norm", "sinks", "attn_w_out", "pool_w_in",
             "pool_w_group", "pool_scale", "pool_w_out")
    return (loss, dx[None], *[res[k][0] for k in order], *[res[k][1] for k in order], *[res[k][2] for k in order],
            *[res[k][3] for k in order])
```

```python
import functools

import numpy as np
import jax
import jax.numpy as jnp
from jax import lax
from jax.experimental import pallas as pl
from jax.experimental.pallas import tpu as pltpu

F32 = jnp.float32
BF16 = jnp.bfloat16
MESH = pl.DeviceIdType.MESH

N_DEV = 8
D = 1024
DEPTH = 4
HEAD_DIM = 64
N_HEADS = 16
N_KV = 4
QK_W = 1280
ATTN_IN = 2560
POOL_IN = 2048
QBLK = 128
POOL_WINDOWS = (2, 4, 8, 16)
HALO = 16
ROPE_THETA = 500000.0
ROT_DIM = 16
NORM_EPS = 1e-6
ADAM_LR = 0.001
ADAM_B1 = 0.9
ADAM_B2 = 0.999
ADAM_EPS = 1e-08
ADAM_WD = 0.01
ADAM_STEP = 10

LANES = 128
VMEM_LIMIT = 56 * 2**20
VEC_ROWS = 32


def _cparams(n_grid=0, **kw):
    if n_grid:
        kw["dimension_semantics"] = ("arbitrary",) * n_grid
    return pltpu.CompilerParams(vmem_limit_bytes=VMEM_LIMIT, **kw)


def _call(body, **kw):
    return pl.pallas_call(body, **kw)


def _const_spec(shape):
    nd = len(shape)
    return pl.BlockSpec(shape, lambda *_: (0,) * nd, pipeline_mode=pl.Buffered(1))


def _dot(a, b):
    return jnp.dot(a, b, preferred_element_type=F32)


def _dot_nt(a, b):
    return lax.dot_general(a, b, (((1,), (1,)), ((), ())), preferred_element_type=F32)


def _dot_tn(a, b):
    return lax.dot_general(a, b, (((0,), (0,)), ((), ())), preferred_element_type=F32)


def _dot_split(x, m):
    hi = x.astype(BF16)
    lo = (x - hi.astype(F32)).astype(BF16)
    return _dot(hi, m) + _dot(lo, m)


def _sigmoid(g):
    return 1.0 / (1.0 + jnp.exp(-g))


def _norm_mod(x, ng, sc, sh):
    r = lax.rsqrt(jnp.mean(x * x, axis=-1, keepdims=True) + NORM_EPS)
    xh = x * r
    h = (xh * ng) * (1.0 + sc) + sh
    return xh, r, h


def _rope_tables(pos_col, invf_row, rows):
    ang = pos_col.astype(F32) * invf_row
    l64 = lax.broadcasted_iota(jnp.int32, (rows, LANES), 1) & (HEAD_DIM - 1)
    cs, sn = jnp.cos(ang), jnp.sin(ang)
    cos_t = jnp.where(l64 < ROT_DIM, cs, 1.0)
    sin_a = jnp.where(l64 < ROT_DIM // 2, -sn, 0.0)
    sin_b = jnp.where((l64 >= ROT_DIM // 2) & (l64 < ROT_DIM), sn, 0.0)
    return cos_t, sin_a, sin_b


def _rope(y, tabs):
    cos_t, sin_a, sin_b = tabs
    return y * cos_t + pltpu.roll(y, LANES - ROT_DIM // 2, 1) * sin_a + pltpu.roll(y, ROT_DIM // 2, 1) * sin_b


def _rope_bwd(dy, tabs):
    cos_t, sin_a, sin_b = tabs
    return dy * cos_t + pltpu.roll(dy * sin_a, ROT_DIM // 2, 1) + pltpu.roll(dy * sin_b, LANES - ROT_DIM // 2, 1)


def _low_half(rows):
    return lax.broadcasted_iota(jnp.int32, (rows, LANES), 1) < HEAD_DIM


def _adamw(w, g, m, v):
    m = ADAM_B1 * m + (1.0 - ADAM_B1) * g
    v = ADAM_B2 * v + (1.0 - ADAM_B2) * (g * g)
    m_hat = m / (1.0 - ADAM_B1 ** ADAM_STEP)
    v_hat = v / (1.0 - ADAM_B2 ** ADAM_STEP)
    delta = -ADAM_LR * (m_hat / (jnp.sqrt(v_hat) + ADAM_EPS) + ADAM_WD * w)
    return delta, m, v


def _my_position():
    x, y, c = lax.axis_index("x"), lax.axis_index("y"), lax.axis_index("c")
    return x, y, c, 4 * x + 2 * y + c


def _peers(x, y, c):
    out = []
    for k in range(1, N_DEV):
        px = 1 - x if k & 4 else x
        py = 1 - y if k & 2 else y
        pc = 1 - c if k & 1 else c
        out.append(((px, py, pc), 4 * px + 2 * py + pc))
    return out


def _allgather_small(v, name):
    rows, cols = v.shape

    def body(v_ref, out_ref, send_sems, recv_sems, local_sem):
        x, y, c, me = _my_position()
        local = pltpu.make_async_copy(v_ref, out_ref.at[me], local_sem)
        local.start()
        sends = []
        for k, (peer, _) in enumerate(_peers(x, y, c)):
            cp = pltpu.make_async_remote_copy(v_ref, out_ref.at[me], send_sems.at[k], recv_sems.at[k],
                                              device_id=peer, device_id_type=MESH)
            cp.start()
            sends.append(cp)
        for k, (peer, idx) in enumerate(_peers(x, y, c)):
            pltpu.make_async_remote_copy(v_ref, out_ref.at[idx], send_sems.at[k], recv_sems.at[k],
                                         device_id=peer, device_id_type=MESH).wait_recv()
        for cp in sends:
            cp.wait_send()
        local.wait()

    return _call(
        body, name=name,
        out_shape=jax.ShapeDtypeStruct((N_DEV, rows, cols), F32),
        in_specs=[pl.BlockSpec(memory_space=pltpu.VMEM)],
        out_specs=pl.BlockSpec(memory_space=pltpu.VMEM),
        scratch_shapes=[pltpu.SemaphoreType.DMA((N_DEV - 1,)), pltpu.SemaphoreType.DMA((N_DEV - 1,)),
                        pltpu.SemaphoreType.DMA(())],
        compiler_params=_cparams(),
    )(v)


def _shard_rows(ref, idx, rows, axis):
    sl = [slice(None)] * len(ref.shape)
    sl[axis] = pl.ds(idx * rows, rows)
    return ref.at[tuple(sl)]


def _own_and_peer_rows(ref, me, idx, axis):
    rows = ref.shape[axis] // N_DEV
    return _shard_rows(ref, me, rows, axis), _shard_rows(ref, idx, rows, axis)


HBM_SPEC = pl.BlockSpec(memory_space=pltpu.HBM)
SEM_SPEC = pl.BlockSpec(memory_space=pltpu.SEMAPHORE)
ANY_SPEC = pl.BlockSpec(memory_space=pl.ANY)
DATAFLOW = pltpu.SideEffectType.DATAFLOW_SIDE_EFFECTING


def _hbm(a):
    return pltpu.with_memory_space_constraint(a, pltpu.HBM)


def _gather_start(layers, axes, after):
    flat = [a for arrs in layers for a in arrs]
    flat_axes = [ax for axs in axes for ax in axs]
    n, nl = len(flat), len(layers)

    def body(*refs):
        ins, sems, token = refs[:n], refs[n + 1:n + 1 + 2 * nl], refs[-1]
        x, y, c, me = _my_position()
        a0 = 0
        for li, arrs in enumerate(layers):
            for k, (peer, _) in enumerate(_peers(x, y, c)):
                for a in range(len(arrs)):
                    rows, _ = _own_and_peer_rows(ins[a0 + a], me, me, flat_axes[a0 + a])
                    pltpu.make_async_remote_copy(rows, rows, sems[2 * li].at[k * len(arrs) + a],
                                                 sems[2 * li + 1].at[k * len(arrs) + a],
                                                 device_id=peer, device_id_type=MESH).start()
            a0 += len(arrs)
        token[...] = jnp.zeros_like(token)

    sem_shapes = []
    for arrs in layers:
        sem_shapes += [pltpu.SemaphoreType.DMA(((N_DEV - 1) * len(arrs),))] * 2
    out = _call(
        body, name="gather_start",
        out_shape=(*sem_shapes, *[pltpu.HBM(a.shape, a.dtype) for a in flat], jax.ShapeDtypeStruct((8, LANES), F32)),
        in_specs=[HBM_SPEC] * n + [ANY_SPEC],
        out_specs=(*[SEM_SPEC] * (2 * nl), *[HBM_SPEC] * n, pl.BlockSpec(memory_space=pltpu.VMEM)),
        input_output_aliases={a: 2 * nl + a for a in range(n)},
        compiler_params=_cparams(has_side_effects=DATAFLOW),
    )(*[_hbm(a) for a in flat], after)
    per_layer, a0 = [], 0
    for li, arrs in enumerate(layers):
        per_layer.append((out[2 * li], out[2 * li + 1], list(out[2 * nl + a0:2 * nl + a0 + len(arrs)])))
        a0 += len(arrs)
    return per_layer, out[-1]


def _gather_wait(started, axes, after, name):
    send_sems, recv_sems, arrs = started
    n = len(arrs)

    def body(*refs):
        ins, send_ref, recv_ref = refs[:n], refs[n], refs[n + 1]
        x, y, c, me = _my_position()
        for k, (peer, idx) in enumerate(_peers(x, y, c)):
            for a in range(n):
                own, theirs = _own_and_peer_rows(ins[a], me, idx, axes[a])
                cp = pltpu.make_async_remote_copy(own, theirs, send_ref.at[k * n + a], recv_ref.at[k * n + a],
                                                  device_id=peer, device_id_type=MESH)
                cp.wait_send()
                cp.wait_recv()

    return _call(
        body, name=name,
        out_shape=tuple(pltpu.HBM(a.shape, a.dtype) for a in arrs),
        in_specs=[HBM_SPEC] * n + [SEM_SPEC, SEM_SPEC, ANY_SPEC],
        out_specs=tuple([HBM_SPEC] * n),
        input_output_aliases={a: a for a in range(n)},
        compiler_params=_cparams(has_side_effects=DATAFLOW),
    )(*arrs, send_sems, recv_sems, after)


def _scatter_start(fulls, axes, name, after):
    n = len(fulls)
    lands = []
    for f, ax in zip(fulls, axes):
        shp = list(f.shape)
        shp[ax] //= N_DEV
        lands.append(_hbm(lax.empty((N_DEV - 1,) + tuple(shp), f.dtype)))

    def body(*refs):
        srcs, dsts, send_ref, recv_ref, token = refs[:n], refs[n:2 * n], refs[2 * n + 1], refs[2 * n + 2], refs[-1]
        x, y, c, me = _my_position()
        for k, (peer, idx) in enumerate(_peers(x, y, c)):
            for a in range(n):
                _, theirs = _own_and_peer_rows(srcs[a], me, idx, axes[a])
                pltpu.make_async_remote_copy(theirs, dsts[a].at[k], send_ref.at[k * n + a], recv_ref.at[k * n + a],
                                             device_id=peer, device_id_type=MESH).start()
        token[...] = jnp.zeros_like(token)

    sem = pltpu.SemaphoreType.DMA(((N_DEV - 1) * n,))
    out = _call(
        body, name=name,
        out_shape=(sem, sem, *[pltpu.HBM(a.shape, a.dtype) for a in fulls], *[pltpu.HBM(a.shape, a.dtype) for a in lands],
                   jax.ShapeDtypeStruct((8, LANES), F32)),
        in_specs=[HBM_SPEC] * (2 * n) + [ANY_SPEC],
        out_specs=(SEM_SPEC, SEM_SPEC, *[HBM_SPEC] * (2 * n), pl.BlockSpec(memory_space=pltpu.VMEM)),
        input_output_aliases={a: 2 + a for a in range(2 * n)},
        compiler_params=_cparams(has_side_effects=DATAFLOW),
    )(*[_hbm(a) for a in fulls], *lands, after)
    return (out[0], out[1], list(out[2:2 + n]), list(out[2 + n:2 + 2 * n])), out[-1]


def _scatter_wait(started, axes, after, name):
    send_sems, recv_sems, fulls, lands = started
    n = len(fulls)

    def body(*refs):
        srcs, dsts, send_ref, recv_ref = refs[:n], refs[n:2 * n], refs[2 * n], refs[2 * n + 1]
        x, y, c, me = _my_position()
        for k, (peer, idx) in enumerate(_peers(x, y, c)):
            for a in range(n):
                _, theirs = _own_and_peer_rows(srcs[a], me, idx, axes[a])
                cp = pltpu.make_async_remote_copy(theirs, dsts[a].at[k], send_ref.at[k * n + a], recv_ref.at[k * n + a],
                                                  device_id=peer, device_id_type=MESH)
                cp.wait_send()
                cp.wait_recv()

    out = _call(
        body, name=name,
        out_shape=tuple(pltpu.HBM(a.shape, a.dtype) for a in (*fulls, *lands)),
        in_specs=[HBM_SPEC] * (2 * n) + [SEM_SPEC, SEM_SPEC, ANY_SPEC],
        out_specs=tuple([HBM_SPEC] * (2 * n)),
        input_output_aliases={a: a for a in range(2 * n)},
        compiler_params=_cparams(has_side_effects=DATAFLOW),
    )(*fulls, *lands, send_sems, recv_sems, after)
    return list(out[:n]), list(out[n:])


def _prep_weights(me, attn_w_in, attn_w_out, pool_w_in, pool_w_out, pool_w_group):
    nl = attn_w_in.shape[0]

    def body(me_ref, *refs):
        ins, outs = refs[:5 * nl], refs[5 * nl:]
        for j in range(nl):
            awi, awo, pwi, pwo, pwg = ins[5 * j:5 * j + 5]
            o_awi, o_awo, o_pwi, o_pwo, o_pwg = outs[5 * j:5 * j + 5]
            o_awi[...] = awi[...].T.astype(BF16)
            o_awo[...] = awo[...].astype(BF16)
            o_pwi[...] = pwi[...].T.astype(BF16)
            o_pwo[...] = pwo[...].astype(BF16)
            o_pwg[...] = pwg[...].astype(BF16)

    def in_spec(shape, j):
        nd = len(shape)
        return pl.BlockSpec((None,) + tuple(shape), lambda i, me_ref: (j,) + (0,) * nd)

    srcs = (attn_w_in, attn_w_out, pool_w_in, pool_w_out, pool_w_group)
    rows_spec = lambda r: pl.BlockSpec((r, D), lambda i, me_ref: (me_ref[0], 0))
    grp = pool_w_group.shape[1:]
    grp_spec = pl.BlockSpec(grp, lambda i, me_ref: (0, me_ref[0], 0))
    ins, in_specs, out_shapes, out_specs = [], [], [], []
    for j in range(nl):
        ins += list(srcs)
        in_specs += [in_spec(a.shape[1:], j) for a in srcs]
        out_shapes += [(N_DEV * attn_w_in.shape[2], D), (N_DEV * attn_w_out.shape[1], D), (N_DEV * pool_w_in.shape[2], D),
                       (N_DEV * pool_w_out.shape[1], D), (grp[0], N_DEV * grp[1], grp[2])]
        out_specs += [rows_spec(attn_w_in.shape[2]), rows_spec(attn_w_out.shape[1]), rows_spec(pool_w_in.shape[2]),
                      rows_spec(pool_w_out.shape[1]), grp_spec]
    out = _call(
        body, name="prep_weights",
        grid_spec=pltpu.PrefetchScalarGridSpec(num_scalar_prefetch=1, grid=(1,), in_specs=in_specs, out_specs=tuple(out_specs)),
        out_shape=tuple(jax.ShapeDtypeStruct(s, BF16) for s in out_shapes),
        compiler_params=_cparams(1),
    )(me.reshape(1), *ins)
    return [list(out[5 * j:5 * j + 5]) for j in range(nl)]


def _ada_forward(c_all, ada_w):
    cols = ada_w.shape[2]

    def body(c_ref, w_ref, o_ref):
        cv = c_ref[...]
        sc = (cv * _sigmoid(cv)).astype(BF16)
        o_ref[...] = _dot(sc, w_ref[...].astype(BF16))

    return _call(
        body, name="ada_forward", grid=(DEPTH,),
        out_shape=jax.ShapeDtypeStruct((DEPTH, N_DEV, cols), F32),
        in_specs=[pl.BlockSpec((N_DEV, D), lambda i: (0, 0)), pl.BlockSpec((None, D, cols), lambda i: (i, 0, 0))],
        out_specs=pl.BlockSpec((None, N_DEV, cols), lambda i: (i, 0, 0)),
        compiler_params=_cparams(1),
    )(c_all, ada_w)


def _ada_backward_adamw(c_pad, dmod_pad, w, m, v):
    cols = w.shape[2]

    def body(c_ref, dm_ref, w_ref, m_ref, v_ref, g_out, d_out, m_out, v_out):
        cv = c_ref[...]
        sc = (cv * _sigmoid(cv)).astype(BF16)
        g = _dot_tn(sc, dm_ref[...].astype(BF16))
        g_out[...] = g
        d_out[...], m_out[...], v_out[...] = _adamw(w_ref[...], g, m_ref[...], v_ref[...])

    wspec = pl.BlockSpec((None, D, cols), lambda i: (i, 0, 0))
    return _call(
        body, name="ada_backward_adamw", grid=(DEPTH,),
        out_shape=tuple(jax.ShapeDtypeStruct(w.shape, F32) for _ in range(4)),
        in_specs=[pl.BlockSpec((2 * N_DEV, D), lambda i: (0, 0)), pl.BlockSpec((None, 2 * N_DEV, cols), lambda i: (i, 0, 0)),
                  wspec, wspec, wspec],
        out_specs=(wspec, wspec, wspec, wspec),
        compiler_params=_cparams(1),
    )(c_pad, dmod_pad, w, m, v)


def _attn_in_proj(x, pos_col, ng, mod, w_t, j, gain, invf, bd, tile):
    seq = x.shape[0]

    def body(x_ref, pos_ref, ng_ref, mod_ref, w_ref, gain_ref, invf_ref, bd_ref, qk_ref, qs_ref, kd_ref, vd_ref, g_ref):
        _, _, h = _norm_mod(x_ref[...], ng_ref[...], mod_ref[1:2, :], mod_ref[0:1, :])
        proj = _dot_nt(h.astype(BF16), w_ref[...])
        qk_ref[...] = proj[:, :QK_W]
        g_ref[...] = proj[:, QK_W + N_KV * HEAD_DIM:]
        tabs = _rope_tables(pos_ref[...], invf_ref[...], tile)
        low = _low_half(tile)
        bdm = bd_ref[...]
        for b in range(QK_W // LANES):
            blk = proj[:, LANES * b:LANES * (b + 1)]
            ms = _dot_split(blk * blk, bdm) * (1.0 / HEAD_DIM)
            y = (blk * lax.rsqrt(ms + NORM_EPS)) * gain_ref[:, LANES * b:LANES * (b + 1)]
            rp = _rope(y, tabs)
            if b < D // LANES:
                rp = rp * (HEAD_DIM ** -0.5)
                qs_ref[:, 2 * LANES * b:2 * LANES * b + LANES] = jnp.where(low, rp, 0.0).astype(BF16)
                qs_ref[:, 2 * LANES * b + LANES:2 * LANES * (b + 1)] = jnp.where(low, 0.0, rp).astype(BF16)
            else:
                kv = 2 * (b - D // LANES)
                sw = pltpu.roll(rp, HEAD_DIM, 1)
                kd_ref[:, LANES * kv:LANES * (kv + 1)] = jnp.where(low, rp, sw).astype(BF16)
                kd_ref[:, LANES * (kv + 1):LANES * (kv + 2)] = jnp.where(low, sw, rp).astype(BF16)
        for b in range(2):
            blk = proj[:, QK_W + LANES * b:QK_W + LANES * (b + 1)]
            sw = pltpu.roll(blk, HEAD_DIM, 1)
            vd_ref[:, LANES * 2 * b:LANES * (2 * b + 1)] = jnp.where(low, blk, sw).astype(BF16)
            vd_ref[:, LANES * (2 * b + 1):LANES * (2 * b + 2)] = jnp.where(low, sw, blk).astype(BF16)

    row = lambda w: pl.BlockSpec((tile, w), lambda i: (i, 0))
    return _call(
        body, name=f"attn_in_proj_{j}", grid=(seq // tile,),
        out_shape=(jax.ShapeDtypeStruct((seq, QK_W), F32), jax.ShapeDtypeStruct((seq, N_HEADS * LANES), BF16),
                   jax.ShapeDtypeStruct((seq, N_KV * LANES), BF16), jax.ShapeDtypeStruct((seq, N_KV * LANES), BF16),
                   jax.ShapeDtypeStruct((seq, D), F32)),
        in_specs=[row(D), row(1), _const_spec((1, D)), _const_spec((8, D)), _const_spec((ATTN_IN, D)),
                  _const_spec((1, QK_W)), _const_spec((1, LANES)), _const_spec((LANES, LANES))],
        out_specs=(row(QK_W), row(N_HEADS * LANES), row(N_KV * LANES), row(N_KV * LANES), row(D)),
        compiler_params=_cparams(1),
    )(x, pos_col, ng, mod, w_t, gain, invf, bd)


def _band_mask(n):
    rows = lax.broadcasted_iota(jnp.int32, (4 * QBLK, 2 * QBLK), 0) & (QBLK - 1)
    cols = lax.broadcasted_iota(jnp.int32, (4 * QBLK, 2 * QBLK), 1)
    diff = QBLK + rows - cols
    first_key = jnp.where(n > 0, 0, QBLK)
    return (diff >= 0) & (diff < QBLK) & (cols >= first_key)


def _stack_heads(ref, kv):
    return jnp.concatenate([ref[:, LANES * h:LANES * (h + 1)] for h in range(4 * kv, 4 * kv + 4)], axis=0)


def _pair_up(st, low):
    return jnp.concatenate([jnp.where(low, st[0:QBLK], st[QBLK:2 * QBLK]),
                            jnp.where(low, st[2 * QBLK:3 * QBLK], st[3 * QBLK:4 * QBLK])], axis=1)


def _head_column(tile, h):
    lane = lax.broadcasted_iota(jnp.int32, tile.shape, 1)
    return jnp.sum(jnp.where(lane == h, tile, 0.0), axis=1, keepdims=True)


def _attn_forward(sinks, qs, kd, vd, j):
    seq = qs.shape[0]
    nb = seq // QBLK

    def body(sink_ref, q_ref, kp_ref, kc_ref, vp_ref, vc_ref, o_ref, lse_ref):
        n = pl.program_id(0)
        ok = _band_mask(n)
        low = _low_half(QBLK)
        lane = lax.broadcasted_iota(jnp.int32, (QBLK, LANES), 1)
        rowi = lax.broadcasted_iota(jnp.int32, (4 * QBLK, 1), 0)
        lse_tile = jnp.zeros((QBLK, LANES), F32)
        for kv in range(N_KV):
            q = _stack_heads(q_ref, kv)
            kk = jnp.concatenate([kp_ref[:, LANES * kv:LANES * (kv + 1)], kc_ref[:, LANES * kv:LANES * (kv + 1)]], axis=0)
            vv = jnp.concatenate([vp_ref[:, LANES * kv:LANES * (kv + 1)], vc_ref[:, LANES * kv:LANES * (kv + 1)]], axis=0)
            s = jnp.where(ok, _dot_nt(q, kk), -1e30)
            sink = jnp.where(rowi < QBLK, sink_ref[4 * kv],
                             jnp.where(rowi < 2 * QBLK, sink_ref[4 * kv + 1],
                                       jnp.where(rowi < 3 * QBLK, sink_ref[4 * kv + 2], sink_ref[4 * kv + 3])))
            m = jnp.maximum(jnp.max(s, axis=1, keepdims=True), sink)
            p = jnp.exp(s - m)
            den = jnp.sum(p, axis=1, keepdims=True) + jnp.exp(sink - m)
            o_st = _dot((p / den).astype(BF16), vv)
            o_ref[:, 2 * LANES * kv:2 * LANES * (kv + 1)] = _pair_up(o_st, low)
            lse = m + jnp.log(den)
            for r in range(4):
                lse_tile = jnp.where(lane == 4 * kv + r, lse[QBLK * r:QBLK * (r + 1)], lse_tile)
        lse_ref[...] = lse_tile

    blk = lambda w: pl.BlockSpec((QBLK, w), lambda n: (n, 0))
    prev = lambda w: pl.BlockSpec((QBLK, w), lambda n: (jnp.maximum(n - 1, 0), 0))
    return _call(
        body, name=f"attn_forward_{j}", grid=(nb,),
        out_shape=(jax.ShapeDtypeStruct((seq, D), F32), jax.ShapeDtypeStruct((seq, LANES), F32)),
        in_specs=[pl.BlockSpec(memory_space=pltpu.SMEM), blk(N_HEADS * LANES), prev(N_KV * LANES), blk(N_KV * LANES),
                  prev(N_KV * LANES), blk(N_KV * LANES)],
        out_specs=(blk(D), blk(LANES)),
        compiler_params=_cparams(1),
    )(sinks, qs, kd, kd, vd, vd)


def _attn_out_proj(x, o, g, w, j, mod, tile):
    seq = x.shape[0]

    def body(x_ref, o_ref, g_ref, w_ref, mod_ref, xo_ref, br_ref):
        gv = g_ref[...]
        u = (o_ref[...] * (gv * _sigmoid(gv))).astype(BF16)
        br = _dot(u, w_ref[...])
        br_ref[...] = br
        xo_ref[...] = x_ref[...] + mod_ref[2:3, :] * br

    row = pl.BlockSpec((tile, D), lambda i: (i, 0))
    return _call(
        body, name=f"attn_out_proj_{j}", grid=(seq // tile,),
        out_shape=(jax.ShapeDtypeStruct((seq, D), F32), jax.ShapeDtypeStruct((seq, D), F32)),
        in_specs=[row, row, row, _const_spec((D, D)), _const_spec((8, D))],
        out_specs=(row, row),
        compiler_params=_cparams(1),
    )(x, o, g, w, mod)


def _attn_out_proj_bwd(dxn, br, o, g, w, j, mod, fold, tile):
    seq = dxn.shape[0]
    steps = seq // tile

    def body(dxn_ref, br_ref, o_ref, g_ref, w_ref, mod_ref, fold_ref, do_ref, dg_ref, delta_ref, dw_ref, dgate_ref, dw_acc):
        i = pl.program_id(0)

        @pl.when(i == 0)
        def _():
            dw_acc[...] = jnp.zeros_like(dw_acc)
            dgate_ref[...] = jnp.zeros_like(dgate_ref)

        dxn_v, ov, gv = dxn_ref[...], o_ref[...], g_ref[...]
        dgate_ref[...] += jnp.sum(dxn_v * br_ref[...], axis=0, keepdims=True)
        dbr = (dxn_v * mod_ref[2:3, :]).astype(BF16)
        du = _dot_nt(dbr, w_ref[...])
        sg = _sigmoid(gv)
        sl = gv * sg
        dw_acc[...] += _dot_tn((ov * sl).astype(BF16), dbr)
        do = du * sl
        dg_ref[...] = (du * ov * (sg * (1.0 + gv * (1.0 - sg)))).astype(BF16)
        delta_ref[...] = _dot_split(do * ov, fold_ref[...])
        low = _low_half(tile)
        for b in range(D // LANES):
            blk = do[:, LANES * b:LANES * (b + 1)]
            do_ref[:, 2 * LANES * b:2 * LANES * b + LANES] = jnp.where(low, blk, 0.0).astype(BF16)
            do_ref[:, 2 * LANES * b + LANES:2 * LANES * (b + 1)] = jnp.where(low, 0.0, blk).astype(BF16)

        @pl.when(i == steps - 1)
        def _():
            dw_ref[...] = dw_acc[...].astype(BF16)

    row = lambda w_: pl.BlockSpec((tile, w_), lambda i: (i, 0))
    return _call(
        body, name=f"attn_out_proj_bwd_{j}", grid=(steps,),
        out_shape=(jax.ShapeDtypeStruct((seq, N_HEADS * LANES), BF16), jax.ShapeDtypeStruct((seq, D), BF16),
                   jax.ShapeDtypeStruct((seq, LANES), F32), jax.ShapeDtypeStruct((D, D), BF16),
                   jax.ShapeDtypeStruct((1, D), F32)),
        in_specs=[row(D), row(D), row(D), row(D), _const_spec((D, D)), _const_spec((8, D)), _const_spec((D, LANES))],
        out_specs=(row(N_HEADS * LANES), row(D), row(LANES), pl.BlockSpec((D, D), lambda i: (0, 0)),
                   pl.BlockSpec((1, D), lambda i: (0, 0))),
        scratch_shapes=[pltpu.VMEM((D, D), F32)],
        compiler_params=_cparams(1),
    )(dxn, br, o, g, w, mod, fold)


def _attn_backward(sink_row, qs, dos, kd, vd, lse, delta, j):
    seq = qs.shape[0]
    nb = seq // QBLK

    def body(sink_ref, q_ref, do_ref, kp_ref, kc_ref, vp_ref, vc_ref, lse_ref, delta_ref,
             dq_ref, dk_ref, dv_ref, dsink_ref, carry_k, carry_v):
        n = pl.program_id(0)

        @pl.when(n == 0)
        def _():
            carry_k[...] = jnp.zeros_like(carry_k)
            carry_v[...] = jnp.zeros_like(carry_v)
            dsink_ref[...] = jnp.zeros_like(dsink_ref)

        @pl.when(n < nb)
        def _():
            ok = _band_mask(n)
            low = _low_half(QBLK)
            lse_t, delta_t = lse_ref[...], delta_ref[...]
            dsink_ref[...] -= jnp.sum(jnp.exp(sink_ref[...] - lse_t) * delta_t, axis=0, keepdims=True)
            dk_parts, dv_parts = [], []
            for kv in range(N_KV):
                q = _stack_heads(q_ref, kv)
                do = _stack_heads(do_ref, kv)
                kk = jnp.concatenate([kp_ref[:, LANES * kv:LANES * (kv + 1)], kc_ref[:, LANES * kv:LANES * (kv + 1)]], axis=0)
                vv = jnp.concatenate([vp_ref[:, LANES * kv:LANES * (kv + 1)], vc_ref[:, LANES * kv:LANES * (kv + 1)]], axis=0)
                lse_c = jnp.concatenate([_head_column(lse_t, 4 * kv + r) for r in range(4)], axis=0)
                dlt_c = jnp.concatenate([_head_column(delta_t, 4 * kv + r) for r in range(4)], axis=0)
                s = jnp.where(ok, _dot_nt(q, kk), -1e30)
                p = jnp.exp(s - lse_c)
                ds = (p * (_dot_nt(do, vv) - dlt_c)).astype(BF16)
                dq_ref[:, 2 * LANES * kv:2 * LANES * (kv + 1)] = _pair_up(_dot(ds, kk), low)
                dkd = _dot_tn(ds, q)
                dvd = _dot_tn(p.astype(BF16), do)
                dk_parts.append(dkd + pltpu.roll(dkd, HEAD_DIM, 1))
                dv_parts.append(dvd + pltpu.roll(dvd, HEAD_DIM, 1))

            def order(parts, lo, hi):
                return jnp.concatenate([jnp.where(low, parts[0][lo:hi], parts[1][lo:hi]),
                                        jnp.where(low, parts[2][lo:hi], parts[3][lo:hi])], axis=1)

            dk_ref[...] = carry_k[...] + order(dk_parts, 0, QBLK)
            dv_ref[...] = (carry_v[...] + order(dv_parts, 0, QBLK)).astype(BF16)
            carry_k[...] = order(dk_parts, QBLK, 2 * QBLK)
            carry_v[...] = order(dv_parts, QBLK, 2 * QBLK)

        @pl.when(n == nb)
        def _():
            dk_ref[...] = carry_k[...]
            dv_ref[...] = carry_v[...].astype(BF16)

    cur = lambda w: pl.BlockSpec((QBLK, w), lambda n: (jnp.minimum(n, nb - 1), 0))
    prev = lambda w: pl.BlockSpec((QBLK, w), lambda n: (jnp.maximum(n - 1, 0), 0))
    kcur = lambda w: pl.BlockSpec((QBLK, w), lambda n: (jnp.minimum(n, nb - 1), 0))
    return _call(
        body, name=f"attn_backward_{j}", grid=(nb + 1,),
        out_shape=(jax.ShapeDtypeStruct((seq, D), F32), jax.ShapeDtypeStruct((seq, N_KV * HEAD_DIM), F32),
                   jax.ShapeDtypeStruct((seq, N_KV * HEAD_DIM), BF16), jax.ShapeDtypeStruct((1, LANES), F32)),
        in_specs=[_const_spec((1, LANES)), cur(N_HEADS * LANES), cur(N_HEADS * LANES), prev(N_KV * LANES), kcur(N_KV * LANES),
                  prev(N_KV * LANES), kcur(N_KV * LANES), cur(LANES), cur(LANES)],
        out_specs=(cur(D), prev(N_KV * HEAD_DIM), prev(N_KV * HEAD_DIM), pl.BlockSpec((1, LANES), lambda n: (0, 0))),
        scratch_shapes=[pltpu.VMEM((QBLK, N_KV * HEAD_DIM), F32), pltpu.VMEM((QBLK, N_KV * HEAD_DIM), F32)],
        compiler_params=_cparams(1),
    )(sink_row, qs, dos, kd, kd, vd, vd, lse, delta)


def _in_proj_tail(x_ref, dxn_ref, ng_ref, mod_ref, w_ref, dproj, dx_ref, dw_acc, vec_acc):
    ng, sc, sh = ng_ref[...], mod_ref[1:2, :], mod_ref[0:1, :]
    xh, r, h = _norm_mod(x_ref[...], ng, sc, sh)
    dh = _dot(dproj, w_ref[...])
    dw_acc[...] += _dot_tn(dproj, h.astype(BF16))
    vec_acc[0:1, :] += jnp.sum(dh, axis=0, keepdims=True)
    vec_acc[1:2, :] += jnp.sum(dh * xh, axis=0, keepdims=True)
    dxh = dh * (ng * (1.0 + sc))
    dx_ref[...] = dxn_ref[...] + r * (dxh - xh * jnp.mean(dxh * xh, axis=-1, keepdims=True))


def _tail_finish(ng_ref, mod_ref, dw_ref, vec_ref, dw_acc, vec_acc):
    dw_ref[...] = dw_acc[...].astype(BF16)
    a = vec_acc[1:2, :]
    vec_ref[...] = jnp.zeros_like(vec_ref)
    vec_ref[0:1, :] = vec_acc[0:1, :]
    vec_ref[1:2, :] = a * ng_ref[...]
    vec_ref[3:4, :] = a * (1.0 + mod_ref[1:2, :])


def _attn_in_proj_bwd(x, dxn, pos_col, qk_raw, dq, dk, dv, dg, ng, mod, w_t, j, gain, invf, bd, tile):
    seq = x.shape[0]
    steps = seq // tile

    def body(x_ref, dxn_ref, pos_ref, qk_ref, dq_ref, dk_ref, dv_ref, dg_ref, ng_ref, mod_ref, w_ref, gain_ref, invf_ref,
             bd_ref, dx_ref, dw_ref, vec_ref, dgain_ref, dproj, dw_acc, vec_acc):
        i = pl.program_id(0)

        @pl.when(i == 0)
        def _():
            dw_acc[...] = jnp.zeros_like(dw_acc)
            vec_acc[...] = jnp.zeros_like(vec_acc)
            dgain_ref[...] = jnp.zeros_like(dgain_ref)

        tabs = _rope_tables(pos_ref[...], invf_ref[...], tile)
        bdm = bd_ref[...]
        for b in range(QK_W // LANES):
            cols = slice(LANES * b, LANES * (b + 1))
            raw = qk_ref[:, cols]
            if b < D // LANES:
                dy = dq_ref[:, cols] * (HEAD_DIM ** -0.5)
            else:
                dy = dk_ref[:, LANES * (b - D // LANES):LANES * (b + 1 - D // LANES)]
            dy = _rope_bwd(dy, tabs)
            rr = lax.rsqrt(_dot_split(raw * raw, bdm) * (1.0 / HEAD_DIM) + NORM_EPS)
            xh = raw * rr
            dgain_ref[:, cols] += jnp.sum(dy * xh, axis=0, keepdims=True)
            dxh = dy * gain_ref[:, cols]
            draw = rr * (dxh - xh * (_dot_split(dxh * xh, bdm) * (1.0 / HEAD_DIM)))
            dproj[:, cols] = draw.astype(BF16)
        dproj[:, QK_W:QK_W + N_KV * HEAD_DIM] = dv_ref[...]
        dproj[:, QK_W + N_KV * HEAD_DIM:] = dg_ref[...]
        _in_proj_tail(x_ref, dxn_ref, ng_ref, mod_ref, w_ref, dproj[...], dx_ref, dw_acc, vec_acc)

        @pl.when(i == steps - 1)
        def _():
            _tail_finish(ng_ref, mod_ref, dw_ref, vec_ref, dw_acc, vec_acc)

    row = lambda w, dt=None: pl.BlockSpec((tile, w), lambda i: (i, 0))
    fixed = lambda shape: pl.BlockSpec(shape, lambda i: (0,) * len(shape))
    return _call(
        body, name=f"attn_in_proj_bwd_{j}", grid=(steps,),
        out_shape=(jax.ShapeDtypeStruct((seq, D), F32), jax.ShapeDtypeStruct((ATTN_IN, D), BF16),
                   jax.ShapeDtypeStruct((8, D), F32), jax.ShapeDtypeStruct((1, QK_W), F32)),
        in_specs=[row(D), row(D), row(1), row(QK_W), row(D), row(N_KV * HEAD_DIM), row(N_KV * HEAD_DIM), row(D),
                  _const_spec((1, D)), _const_spec((8, D)), _const_spec((ATTN_IN, D)), _const_spec((1, QK_W)),
                  _const_spec((1, LANES)), _const_spec((LANES, LANES))],
        out_specs=(row(D), fixed((ATTN_IN, D)), fixed((8, D)), fixed((1, QK_W))),
        scratch_shapes=[pltpu.VMEM((tile, ATTN_IN), BF16), pltpu.VMEM((ATTN_IN, D), F32), pltpu.VMEM((8, D), F32)],
        compiler_params=_cparams(1),
    )(x, dxn, pos_col, qk_raw, dq, dk, dv, dg, ng, mod, w_t, gain, invf, bd)


def _pool_in_proj(x, ng, mod, w_t, j, tile):
    seq = x.shape[0]

    def body(x_ref, ng_ref, mod_ref, w_ref, v_ref, g_ref):
        _, _, h = _norm_mod(x_ref[...], ng_ref[...], mod_ref[1:2, :], mod_ref[0:1, :])
        proj = _dot_nt(h.astype(BF16), w_ref[...])
        v_ref[...] = proj[:, :D]
        g_ref[...] = proj[:, D:]

    row = pl.BlockSpec((tile, D), lambda i: (i, 0))
    return _call(
        body, name=f"pool_in_proj_{j}", grid=(seq // tile,),
        out_shape=(jax.ShapeDtypeStruct((seq, D), F32), jax.ShapeDtypeStruct((seq, D), F32)),
        in_specs=[row, _const_spec((1, D)), _const_spec((8, D)), _const_spec((POOL_IN, D))],
        out_specs=(row, row),
        compiler_params=_cparams(1),
    )(x, ng, mod, w_t)


def _pooled(ext, first, tile):
    t_abs = first + lax.broadcasted_iota(jnp.int32, (tile, 1), 0)
    outs = []
    gw = D // len(POOL_WINDOWS)
    for gi, w in enumerate(POOL_WINDOWS):
        cols = slice(gw * gi, gw * (gi + 1))
        own = ext[HALO:HALO + tile, cols]
        acc = own
        for k in range(1, w):
            acc = acc + ext[HALO - k:HALO - k + tile, cols]
        cnt = jnp.minimum(t_abs + 1, w).astype(F32)
        outs.append(acc / cnt - own)
    return jnp.concatenate(outs, axis=1)


def _fill_ext(ext, halo_ref, v_ref, i, tile):
    ext[0:HALO, :] = jnp.where(i == 0, 0.0, halo_ref[...])
    ext[HALO:HALO + tile, :] = v_ref[...]


def _group_mix(pb, wg_ref):
    gw = D // len(POOL_WINDOWS)
    return jnp.concatenate([_dot(pb[:, gw * gi:gw * (gi + 1)], wg_ref[gi]) for gi in range(len(POOL_WINDOWS))], axis=1)


def _pool_mix_out(x, v, g, wg, w_out, j, scale, mod, tile):
    seq = x.shape[0]

    def body(x_ref, v_ref, halo_ref, g_ref, wg_ref, w_ref, scale_ref, mod_ref, xo_ref, br_ref, ext):
        i = pl.program_id(0)
        _fill_ext(ext, halo_ref, v_ref, i, tile)
        pb = _pooled(ext, i * tile, tile).astype(BF16)
        ms = _group_mix(pb, wg_ref) * scale_ref[...]
        gv = g_ref[...]
        u = (ms * (gv * _sigmoid(gv))).astype(BF16)
        br = _dot(u, w_ref[...])
        br_ref[...] = br
        xo_ref[...] = x_ref[...] + mod_ref[2:3, :] * br

    row = pl.BlockSpec((tile, D), lambda i: (i, 0))
    halo = pl.BlockSpec((HALO, D), lambda i: (jnp.maximum(i * (tile // HALO) - 1, 0), 0))
    return _call(
        body, name=f"pool_mix_out_{j}", grid=(seq // tile,),
        out_shape=(jax.ShapeDtypeStruct((seq, D), F32), jax.ShapeDtypeStruct((seq, D), F32)),
        in_specs=[row, row, halo, row, _const_spec(wg.shape), _const_spec((D, D)), _const_spec((1, D)),
                  _const_spec((8, D))],
        out_specs=(row, row),
        scratch_shapes=[pltpu.VMEM((tile + HALO, D), F32)],
        compiler_params=_cparams(1),
    )(x, v, v, g, wg, w_out, scale, mod)


def _pool_mix_out_bwd(dxn, br, v, g, wg, w_out, j, scale, mod, tile):
    seq = dxn.shape[0]
    steps = seq // tile
    ng_ = len(POOL_WINDOWS)
    gw = D // ng_

    def body(dxn_ref, br_ref, v_ref, halo_ref, g_ref, wg_ref, w_ref, scale_ref, mod_ref,
             dpool_ref, dg_ref, dw_ref, dwg_ref, vec_ref, ext, dw_acc, dwg_acc):
        i = pl.program_id(0)

        @pl.when(i == 0)
        def _():
            dw_acc[...] = jnp.zeros_like(dw_acc)
            dwg_acc[...] = jnp.zeros_like(dwg_acc)
            vec_ref[...] = jnp.zeros_like(vec_ref)

        _fill_ext(ext, halo_ref, v_ref, i, tile)
        pb = _pooled(ext, i * tile, tile).astype(BF16)
        mixed = _group_mix(pb, wg_ref)
        scale = scale_ref[...]
        ms = mixed * scale
        gv, dxn_v = g_ref[...], dxn_ref[...]
        sg = _sigmoid(gv)
        sl = gv * sg
        vec_ref[0:1, :] += jnp.sum(dxn_v * br_ref[...], axis=0, keepdims=True)
        dbr = (dxn_v * mod_ref[2:3, :]).astype(BF16)
        du = _dot_nt(dbr, w_ref[...])
        dw_acc[...] += _dot_tn((ms * sl).astype(BF16), dbr)
        dms = du * sl
        dg_ref[...] = (du * ms * (sg * (1.0 + gv * (1.0 - sg)))).astype(BF16)
        vec_ref[1:2, :] += jnp.sum(dms * mixed, axis=0, keepdims=True)
        dmx = (dms * scale).astype(BF16)
        for gi in range(ng_):
            cols = slice(gw * gi, gw * (gi + 1))
            dpool_ref[:, cols] = _dot_nt(dmx[:, cols], wg_ref[gi])
            dwg_acc[gi] += _dot_tn(pb[:, cols], dmx[:, cols])

        @pl.when(i == steps - 1)
        def _():
            dw_ref[...] = dw_acc[...].astype(BF16)
            dwg_ref[...] = dwg_acc[...].astype(BF16)

    row = pl.BlockSpec((tile, D), lambda i: (i, 0))
    halo = pl.BlockSpec((HALO, D), lambda i: (jnp.maximum(i * (tile // HALO) - 1, 0), 0))
    fixed = lambda shape: pl.BlockSpec(shape, lambda i: (0,) * len(shape))
    return _call(
        body, name=f"pool_mix_out_bwd_{j}", grid=(steps,),
        out_shape=(jax.ShapeDtypeStruct((seq, D), F32), jax.ShapeDtypeStruct((seq, D), BF16),
                   jax.ShapeDtypeStruct((D, D), BF16), jax.ShapeDtypeStruct((ng_, gw, gw), BF16),
                   jax.ShapeDtypeStruct((8, D), F32)),
        in_specs=[row, row, row, halo, row, _const_spec(wg.shape), _const_spec((D, D)), _const_spec((1, D)),
                  _const_spec((8, D))],
        out_specs=(row, row, fixed((D, D)), fixed((ng_, gw, gw)), fixed((8, D))),
        scratch_shapes=[pltpu.VMEM((tile + HALO, D), F32), pltpu.VMEM((D, D), F32), pltpu.VMEM((ng_, gw, gw), F32)],
        compiler_params=_cparams(1),
    )(dxn, br, v, v, g, wg, w_out, scale, mod)


def _pool_in_proj_bwd(x, dxn, dpool, dg, ng, mod, w_t, j, tile):
    seq = x.shape[0]
    steps = seq // tile
    gw = D // len(POOL_WINDOWS)

    def body(x_ref, dxn_ref, dp_ref, halo_ref, dg_ref, ng_ref, mod_ref, w_ref, dx_ref, dw_ref, vec_ref,
             ext, dproj, dw_acc, vec_acc):
        i = pl.program_id(0)

        @pl.when(i == 0)
        def _():
            dw_acc[...] = jnp.zeros_like(dw_acc)
            vec_acc[...] = jnp.zeros_like(vec_acc)

        t_abs = i * tile + lax.broadcasted_iota(jnp.int32, (tile, 1), 0)
        last = i == steps - 1
        for gi, w in enumerate(POOL_WINDOWS):
            cols = slice(gw * gi, gw * (gi + 1))
            cnt = jnp.minimum(t_abs + 1, w).astype(F32)
            ext[0:tile, cols] = dp_ref[:, cols] / cnt
            ext[tile:tile + HALO, cols] = jnp.where(last, 0.0, halo_ref[:, cols] * (1.0 / w))
        for gi, w in enumerate(POOL_WINDOWS):
            cols = slice(gw * gi, gw * (gi + 1))
            acc = ext[0:tile, cols]
            for k in range(1, w):
                acc = acc + ext[k:k + tile, cols]
            dproj[:, cols] = (acc - dp_ref[:, cols]).astype(BF16)
        dproj[:, D:] = dg_ref[...]
        _in_proj_tail(x_ref, dxn_ref, ng_ref, mod_ref, w_ref, dproj[...], dx_ref, dw_acc, vec_acc)

        @pl.when(last)
        def _():
            _tail_finish(ng_ref, mod_ref, dw_ref, vec_ref, dw_acc, vec_acc)

    row = pl.BlockSpec((tile, D), lambda i: (i, 0))
    halo = pl.BlockSpec((HALO, D), lambda i: (jnp.minimum((i + 1) * (tile // HALO), seq // HALO - 1), 0))
    fixed = lambda shape: pl.BlockSpec(shape, lambda i: (0,) * len(shape))
    return _call(
        body, name=f"pool_in_proj_bwd_{j}", grid=(steps,),
        out_shape=(jax.ShapeDtypeStruct((seq, D), F32), jax.ShapeDtypeStruct((POOL_IN, D), BF16),
                   jax.ShapeDtypeStruct((8, D), F32)),
        in_specs=[row, row, row, halo, row, _const_spec((1, D)), _const_spec((8, D)), _const_spec((POOL_IN, D))],
        out_specs=(row, fixed((POOL_IN, D)), fixed((8, D))),
        scratch_shapes=[pltpu.VMEM((tile + HALO, D), F32), pltpu.VMEM((tile, POOL_IN), BF16), pltpu.VMEM((POOL_IN, D), F32),
                        pltpu.VMEM((8, D), F32)],
        compiler_params=_cparams(1),
    )(x, dxn, dpool, dpool, dg, ng, mod, w_t)


def _loss_head(y, target, tile):
    seq = y.shape[0]

    def body(y_ref, t_ref, dy_ref, loss_ref):
        @pl.when(pl.program_id(0) == 0)
        def _():
            loss_ref[...] = jnp.zeros_like(loss_ref)

        e = y_ref[...] - t_ref[...]
        dy_ref[...] = e * (1.0 / D)
        loss_ref[...] += 0.5 * jnp.sum(jnp.mean(e * e, axis=-1, keepdims=True), axis=0, keepdims=True)

    row = pl.BlockSpec((tile, D), lambda i: (i, 0))
    return _call(
        body, name="loss_head", grid=(seq // tile,),
        out_shape=(jax.ShapeDtypeStruct((seq, D), F32), jax.ShapeDtypeStruct((1, LANES), F32)),
        in_specs=[row, row],
        out_specs=(row, pl.BlockSpec((1, LANES), lambda i: (0, 0))),
        compiler_params=_cparams(1),
    )(y, target)


def _build_vec(vecs, gates, pool_vecs, gains, dsinks, loss_part):
    def body(v0, v1, v2, v3, g0, g2, p0, p1, n0, n1, s0, s1, loss_ref, out):
        out[...] = jnp.zeros_like(out)
        for i, v in enumerate((v0, v1, v2, v3)):
            out[3 * i:3 * i + 2, :] = v[0:2, :]
            out[12 + i:13 + i, :] = v[3:4, :]
        out[2:3, :] = g0[...]
        out[8:9, :] = g2[...]
        for j, (p, n, s) in enumerate(((p0, n0, s0), (p1, n1, s1))):
            out[3 * (2 * j + 1) + 2:3 * (2 * j + 1) + 3, :] = p[0:1, :]
            out[22 + j:23 + j, :] = p[1:2, :]
            out[16 + j:17 + j, :] = n[:, 0:D]
            out[18 + j:19 + j, 0:QK_W - D] = n[:, D:QK_W]
            out[20 + j:21 + j, 0:LANES] = s[...]
        out[24:25, 0:LANES] = loss_ref[...]

    vm = pl.BlockSpec(memory_space=pltpu.VMEM)
    args = (*vecs, gates[0], gates[2], *pool_vecs, *gains, *dsinks, loss_part)
    return _call(
        body, name="build_vec",
        out_shape=jax.ShapeDtypeStruct((VEC_ROWS, D), F32),
        in_specs=[vm] * len(args), out_specs=vm,
        compiler_params=_cparams(),
    )(*args)


def _sum_devices(g, after):
    rows = g.shape[1]

    def body(g_ref, after_ref, tot_ref, fold_ref):
        tot = g_ref[0]
        for p in range(1, N_DEV):
            tot = tot + g_ref[p]
        tot_ref[...] = tot
        f = tot[16:24, 0:LANES]
        for b in range(1, D // LANES):
            f = f + tot[16:24, LANES * b:LANES * (b + 1)]
        fold_ref[...] = f + pltpu.roll(f, HEAD_DIM, 1)

    return _call(
        body, name="sum_devices",
        out_shape=(jax.ShapeDtypeStruct((rows, D), F32), jax.ShapeDtypeStruct((8, LANES), F32)),
        in_specs=[pl.BlockSpec(memory_space=pltpu.VMEM), ANY_SPEC],
        out_specs=(pl.BlockSpec(memory_space=pltpu.VMEM), pl.BlockSpec(memory_space=pltpu.VMEM)),
        compiler_params=_cparams(),
    )(g, after)


def _adamw_small(name, w, g, m, v):
    def body(w_ref, g_ref, m_ref, v_ref, d_out, m_out, v_out):
        d_out[...], m_out[...], v_out[...] = _adamw(w_ref[...], g_ref[...], m_ref[...], v_ref[...])

    vm = pl.BlockSpec(memory_space=pltpu.VMEM)
    return _call(
        body, name=name,
        out_shape=tuple(jax.ShapeDtypeStruct(w.shape, F32) for _ in range(3)),
        in_specs=[vm] * 4, out_specs=(vm, vm, vm),
        compiler_params=_cparams(),
    )(w, g, m, v)


def _adamw_shards(name, me, fulls, lands, w, m, v, transpose, axis=0):
    nl = w.shape[0]
    wshape = w.shape[1:]
    own_shape = lands[0].shape[1:]

    def body(me_ref, *refs):
        own_refs, land_refs = refs[:nl], refs[nl:2 * nl]
        w_ref, m_ref, v_ref, g_out, d_out, m_out, v_out = refs[2 * nl:]
        layer = pl.program_id(0)
        for l in range(nl):
            @pl.when(layer == l)
            def _(l=l):
                g = own_refs[l][...].astype(F32)
                for k in range(N_DEV - 1):
                    g = g + land_refs[l][k].astype(F32)
                if transpose:
                    g = g.T
                g_out[...] = g
                d_out[...], m_out[...], v_out[...] = _adamw(w_ref[...], g, m_ref[...], v_ref[...])

    def own_index(l_, me_ref):
        idx = [0] * len(own_shape)
        idx[axis] = me_ref[0]
        return tuple(idx)

    own_spec = pl.BlockSpec(tuple(own_shape), own_index)
    land_spec = pl.BlockSpec((N_DEV - 1,) + tuple(own_shape), lambda l_, me_ref: (0,) * (1 + len(own_shape)))
    wspec = pl.BlockSpec((None,) + tuple(wshape), lambda l_, me_ref: (l_,) + (0,) * len(wshape))
    return _call(
        body, name=name,
        grid_spec=pltpu.PrefetchScalarGridSpec(num_scalar_prefetch=1, grid=(nl,),
                                               in_specs=[own_spec] * nl + [land_spec] * nl + [wspec] * 3,
                                               out_specs=(wspec,) * 4),
        out_shape=tuple(jax.ShapeDtypeStruct(w.shape, F32) for _ in range(4)),
        compiler_params=_cparams(1),
    )(me.reshape(1), *fulls, *lands, w, m, v)


def _constants():
    lane = np.arange(LANES)
    bd = (lane[:, None] // HEAD_DIM == lane[None, :] // HEAD_DIM).astype(np.float32)
    fold = (np.arange(D)[:, None] // HEAD_DIM == lane[None, :]).astype(np.float32)
    half = ROT_DIM // 2
    inv_freq = ROPE_THETA ** (-jnp.arange(half, dtype=F32) * 2.0 / ROT_DIM)
    invf = jnp.tile(inv_freq, LANES // half).reshape(1, LANES)
    return jnp.asarray(bd, BF16), jnp.asarray(fold, BF16), invf


def kernel(x, c, positions, ada_w, ada_b, norm_g, attn_w_in, attn_q_norm, attn_k_norm, attn_sinks, attn_w_out, pool_w_in, pool_w_group, pool_scale, pool_w_out, loss_target, m_ada_w, m_ada_b, m_norm_g, m_attn_w_in, m_attn_q_norm, m_attn_k_norm, m_attn_sinks, m_attn_w_out, m_pool_w_in, m_pool_w_group, m_pool_scale, m_pool_w_out, v_ada_w, v_ada_b, v_norm_g, v_attn_w_in, v_attn_q_norm, v_attn_k_norm, v_attn_sinks, v_attn_w_out, v_pool_w_in, v_pool_w_group, v_pool_scale, v_pool_w_out):
    seq = x.shape[1]
    me = 4 * lax.axis_index("x") + 2 * lax.axis_index("y") + lax.axis_index("c")
    bd, fold, invf = _constants()
    pos_col = positions.reshape(seq, 1)
    t_mm = min(512, seq)
    t_bw = min(256, seq)
    shard = pool_scale.shape[1]
    cols = ada_w.shape[2]

    layers = _prep_weights(me, attn_w_in, attn_w_out, pool_w_in, pool_w_out, pool_w_group)

    first = jnp.concatenate([c, jnp.pad(pool_scale, ((0, 0), (0, D - shard))), jnp.zeros((5, D), F32)], axis=0)
    first = _allgather_small(first, "allgather_c")
    c_all = first[:, 0, :]
    scale_full = jnp.transpose(first[:, 1:3, :shard], (1, 0, 2)).reshape(2, D)
    mod_part = _ada_forward(c_all, ada_w)
    mod_all = _allgather_small(mod_part.reshape(DEPTH * N_DEV, cols), "allgather_mod")
    mod_all = mod_all.reshape(N_DEV, DEPTH, N_DEV, cols)
    mine = lax.dynamic_index_in_dim(mod_all, me, axis=2, keepdims=False)
    mod = jnp.transpose(mine, (1, 0, 2)).reshape(DEPTH, 3 * D) + ada_b
    mod = jnp.pad(mod.reshape(DEPTH, 3, D), ((0, 0), (0, 5), (0, 0)))

    groups = [[layers[0][0]], [layers[0][1]], layers[0][2:5], layers[1][0:2], layers[1][2:5]]
    gaxes = [(0,), (0,), (0, 0, 1), (0, 0), (0, 0, 1)]
    started, token = _gather_start(groups, gaxes, mod)

    saved, weights = [], []
    h = x[0]
    for i in range(DEPTH):
        j = i // 2
        s = dict(x=h, ng=norm_g[i:i + 1], md=mod[i])
        if i == 0:
            w_in_t, = _gather_wait(started[0], gaxes[0], token, "gather_wait_0_in")
        else:
            wts = _gather_wait(started[i + 1], gaxes[i + 1], h, f"gather_wait_{i}")
        if i % 2 == 0:
            if i > 0:
                w_in_t, w_out = wts
            s["gain"] = jnp.concatenate([jnp.tile(attn_q_norm[j], N_HEADS), jnp.tile(attn_k_norm[j], N_KV)]).reshape(1, QK_W)
            s["qk_raw"], s["qs"], s["kd"], s["vd"], s["g"] = _attn_in_proj(
                h, pos_col, s["ng"], s["md"], w_in_t, j, s["gain"], invf, bd, t_bw)
            s["o"], s["lse"] = _attn_forward(attn_sinks[j], s["qs"], s["kd"], s["vd"], j)
            if i == 0:
                w_out, = _gather_wait(started[1], gaxes[1], s["o"], "gather_wait_0_out")
            h, s["br"] = _attn_out_proj(h, s["o"], s["g"], w_out, j, s["md"], t_mm)
            weights.append((w_in_t, w_out))
        else:
            p_in_t, p_out, p_grp = wts
            s["scale"] = scale_full[j:j + 1]
            s["v"], s["g"] = _pool_in_proj(h, s["ng"], s["md"], p_in_t, j, t_mm)
            h, s["br"] = _pool_mix_out(h, s["v"], s["g"], p_grp, p_out, j, s["scale"], s["md"], t_mm)
            weights.append(wts)
        saved.append(s)
    dx, loss_part = _loss_head(h, loss_target[0], t_mm)

    vecs, gates, gains, dsinks, pool_vecs = [None] * DEPTH, [None] * DEPTH, [None] * 2, [None] * 2, [None] * 2
    sent_in, sent_out = [None] * DEPTH, [None] * DEPTH
    token = jnp.zeros((8, LANES), F32)
    for i in reversed(range(DEPTH)):
        j = i // 2
        s = saved[i]
        md = s["md"] + token[0, 0]
        if i % 2 == 0:
            w_in_t, w_out = weights[i]
            dos, dg, delta, d_w_out, gates[i] = _attn_out_proj_bwd(dx, s["br"], s["o"], s["g"], w_out, j, md, fold, t_mm)
            sent_out[i], token = _scatter_start([d_w_out], (0,), f"scatter_start_{i}_out", token)
            sink_row = jnp.pad(attn_sinks[j], (0, LANES - N_HEADS)).reshape(1, LANES) + token[0:1, :]
            dq, dk, dv, dsinks[j] = _attn_backward(sink_row, s["qs"], dos, s["kd"], s["vd"], s["lse"], delta, j)
            dx, d_in_t, vecs[i], gains[j] = _attn_in_proj_bwd(
                s["x"], dx, pos_col, s["qk_raw"], dq, dk, dv, dg, s["ng"], md, w_in_t, j, s["gain"], invf, bd, t_bw)
        else:
            p_in_t, p_out, p_grp = weights[i]
            dpool, dg, d_p_out, d_p_grp, pool_vecs[j] = _pool_mix_out_bwd(
                dx, s["br"], s["v"], s["g"], p_grp, p_out, j, s["scale"], md, t_mm)
            sent_out[i], token = _scatter_start([d_p_out, d_p_grp], (0, 1), f"scatter_start_{i}_out", token)
            dx, d_in_t, vecs[i] = _pool_in_proj_bwd(s["x"], dx, dpool, dg, s["ng"], s["md"] + token[0, 0], p_in_t, j, t_bw)
        if i > 0:
            sent_in[i], token = _scatter_start([d_in_t], (0,), f"scatter_start_{i}_in", token)

    vec = _build_vec(vecs, gates, pool_vecs, gains, dsinks, loss_part)
    vec_all = _allgather_small(vec, "allgather_vec")
    sent_in[0], token = _scatter_start([d_in_t], (0,), "scatter_start_0_in", vec_all)
    tot, folded = _sum_devices(vec_all, token)
    loss = tot[24, 0]
    small = dict(
        ada_b=(ada_b, tot[0:12].reshape(DEPTH, 3 * D), m_ada_b, v_ada_b),
        norm_g=(norm_g, tot[12:16], m_norm_g, v_norm_g),
        q_norm=(attn_q_norm, folded[0:2, :HEAD_DIM], m_attn_q_norm, v_attn_q_norm),
        k_norm=(attn_k_norm, folded[2:4, :HEAD_DIM], m_attn_k_norm, v_attn_k_norm),
        sinks=(attn_sinks, tot[20:22, :N_HEADS], m_attn_sinks, v_attn_sinks),
        pool_scale=(pool_scale, lax.dynamic_slice(tot, (22, me * shard), (2, shard)), m_pool_scale, v_pool_scale),
    )
    res = {k: (a[1],) + tuple(_adamw_small("adamw_" + k, *a)) for k, a in small.items()}

    dmod_all = vec_all[:, 0:12, :].reshape(N_DEV, DEPTH, 3 * D)
    dmod_mine = lax.dynamic_slice_in_dim(dmod_all, me * cols, cols, axis=2)
    dmod_mine = jnp.pad(jnp.transpose(dmod_mine, (1, 0, 2)), ((0, 0), (0, N_DEV), (0, 0)))
    res["ada_w"] = _ada_backward_adamw(jnp.pad(c_all, ((0, N_DEV), (0, 0))), dmod_mine, ada_w, m_ada_w, v_ada_w)

    got_in, got_out = [None] * DEPTH, [None] * DEPTH
    for i in (3, 1):
        got_out[i] = _scatter_wait(sent_out[i], (0, 1), res["ada_w"][0], f"scatter_wait_{i}_out")
        got_in[i] = _scatter_wait(sent_in[i], (0,), res["ada_w"][0], f"scatter_wait_{i}_in")
    pick = lambda got, ls, a: ([got[i][0][a] for i in ls], [got[i][1][a] for i in ls])
    res["pool_w_in"] = _adamw_shards("adamw_pool_w_in", me, *pick(got_in, (1, 3), 0), pool_w_in, m_pool_w_in, v_pool_w_in, True)
    res["pool_w_out"] = _adamw_shards("adamw_pool_w_out", me, *pick(got_out, (1, 3), 0), pool_w_out, m_pool_w_out,
                                      v_pool_w_out, False)
    res["pool_w_group"] = _adamw_shards("adamw_pool_w_group", me, *pick(got_out, (1, 3), 1), pool_w_group, m_pool_w_group,
                                        v_pool_w_group, False, axis=1)
    for i in (2, 0):
        got_out[i] = _scatter_wait(sent_out[i], (0,), res["pool_w_group"][0], f"scatter_wait_{i}_out")
        got_in[i] = _scatter_wait(sent_in[i], (0,), res["pool_w_group"][0], f"scatter_wait_{i}_in")
    res["attn_w_out"] = _adamw_shards("adamw_attn_w_out", me, *pick(got_out, (0, 2), 0), attn_w_out, m_attn_w_out,
                                      v_attn_w_out, False)
    res["attn_w_in"] = _adamw_shards("adamw_attn_w_in", me, *pick(got_in, (0, 2), 0), attn_w_in, m_attn_w_in, v_attn_w_in, True)

    order = ("ada_w", "ada_b", "norm_g", "attn_w_in", "q_norm", "k_norm", "sinks", "attn_w_out", "pool_w_in",
             "pool_w_group", "pool_scale", "pool_w_out")
    return (loss, dx[None], *[res[k][0] for k in order], *[res[k][1] for k in order], *[res[k][2] for k in order],
            *[res[k][3] for k in order])
```

```python
import functools

import numpy as np
import jax
import jax.numpy as jnp
from jax import lax
from jax.experimental import pallas as pl
from jax.experimental.pallas import tpu as pltpu

F32 = jnp.float32
BF16 = jnp.bfloat16
MESH = pl.DeviceIdType.MESH

N_DEV = 8
D = 1024
DEPTH = 4
HEAD_DIM = 64
N_HEADS = 16
N_KV = 4
QK_W = 1280
ATTN_IN = 2560
POOL_IN = 2048
QBLK = 128
KX_W = N_KV * 128
POOL_WINDOWS = (2, 4, 8, 16)
HALO = 16
ROPE_THETA = 500000.0
ROT_DIM = 16
NORM_EPS = 1e-6
ADAM_LR = 0.001
ADAM_B1 = 0.9
ADAM_B2 = 0.999
ADAM_EPS = 1e-08
ADAM_WD = 0.01
ADAM_STEP = 10

LANES = 128
VMEM_LIMIT = 56 * 2**20
VEC_ROWS = 32


def _cparams(n_grid=0, **kw):
    if n_grid:
        kw["dimension_semantics"] = ("arbitrary",) * n_grid
    return pltpu.CompilerParams(vmem_limit_bytes=VMEM_LIMIT, **kw)


def _call(body, **kw):
    return pl.pallas_call(body, **kw)


def _const_spec(shape):
    nd = len(shape)
    return pl.BlockSpec(shape, lambda *_: (0,) * nd, pipeline_mode=pl.Buffered(1))


def _dot(a, b):
    return jnp.dot(a, b, preferred_element_type=F32)


def _dot_nt(a, b):
    return lax.dot_general(a, b, (((1,), (1,)), ((), ())), preferred_element_type=F32)


def _dot_tn(a, b):
    return lax.dot_general(a, b, (((0,), (0,)), ((), ())), preferred_element_type=F32)


def _dot_split(x, m):
    hi = x.astype(BF16)
    lo = (x - hi.astype(F32)).astype(BF16)
    return _dot(hi, m) + _dot(lo, m)


def _sigmoid(g):
    return 1.0 / (1.0 + jnp.exp(-g))


def _norm_mod(x, ng, sc, sh):
    r = lax.rsqrt(jnp.mean(x * x, axis=-1, keepdims=True) + NORM_EPS)
    xh = x * r
    h = (xh * ng) * (1.0 + sc) + sh
    return xh, r, h


def _rope_tables(pos_col, invf_row, rows):
    ang = pos_col.astype(F32) * invf_row
    l64 = lax.broadcasted_iota(jnp.int32, (rows, LANES), 1) & (HEAD_DIM - 1)
    cs, sn = jnp.cos(ang), jnp.sin(ang)
    cos_t = jnp.where(l64 < ROT_DIM, cs, 1.0)
    sin_a = jnp.where(l64 < ROT_DIM // 2, -sn, 0.0)
    sin_b = jnp.where((l64 >= ROT_DIM // 2) & (l64 < ROT_DIM), sn, 0.0)
    return cos_t, sin_a, sin_b


def _rope(y, tabs):
    cos_t, sin_a, sin_b = tabs
    return y * cos_t + pltpu.roll(y, LANES - ROT_DIM // 2, 1) * sin_a + pltpu.roll(y, ROT_DIM // 2, 1) * sin_b


def _rope_bwd(dy, tabs):
    cos_t, sin_a, sin_b = tabs
    return dy * cos_t + pltpu.roll(dy * sin_a, ROT_DIM // 2, 1) + pltpu.roll(dy * sin_b, LANES - ROT_DIM // 2, 1)


def _low_half(rows):
    return lax.broadcasted_iota(jnp.int32, (rows, LANES), 1) < HEAD_DIM


def _adamw(w, g, m, v):
    m = ADAM_B1 * m + (1.0 - ADAM_B1) * g
    v = ADAM_B2 * v + (1.0 - ADAM_B2) * (g * g)
    m_hat = m / (1.0 - ADAM_B1 ** ADAM_STEP)
    v_hat = v / (1.0 - ADAM_B2 ** ADAM_STEP)
    delta = -ADAM_LR * (m_hat / (jnp.sqrt(v_hat) + ADAM_EPS) + ADAM_WD * w)
    return delta, m, v


def _my_position():
    x, y, c = lax.axis_index("x"), lax.axis_index("y"), lax.axis_index("c")
    return x, y, c, 4 * x + 2 * y + c


def _peers(x, y, c):
    out = []
    for k in range(1, N_DEV):
        px = 1 - x if k & 4 else x
        py = 1 - y if k & 2 else y
        pc = 1 - c if k & 1 else c
        out.append(((px, py, pc), 4 * px + 2 * py + pc))
    return out


def _allgather_small(v, name):
    rows, cols = v.shape

    def body(v_ref, out_ref, send_sems, recv_sems, local_sem):
        x, y, c, me = _my_position()
        local = pltpu.make_async_copy(v_ref, out_ref.at[me], local_sem)
        local.start()
        sends = []
        for k, (peer, _) in enumerate(_peers(x, y, c)):
            cp = pltpu.make_async_remote_copy(v_ref, out_ref.at[me], send_sems.at[k], recv_sems.at[k],
                                              device_id=peer, device_id_type=MESH)
            cp.start()
            sends.append(cp)
        for k, (peer, idx) in enumerate(_peers(x, y, c)):
            pltpu.make_async_remote_copy(v_ref, out_ref.at[idx], send_sems.at[k], recv_sems.at[k],
                                         device_id=peer, device_id_type=MESH).wait_recv()
        for cp in sends:
            cp.wait_send()
        local.wait()

    return _call(
        body, name=name,
        out_shape=jax.ShapeDtypeStruct((N_DEV, rows, cols), F32),
        in_specs=[pl.BlockSpec(memory_space=pltpu.VMEM)],
        out_specs=pl.BlockSpec(memory_space=pltpu.VMEM),
        scratch_shapes=[pltpu.SemaphoreType.DMA((N_DEV - 1,)), pltpu.SemaphoreType.DMA((N_DEV - 1,)),
                        pltpu.SemaphoreType.DMA(())],
        compiler_params=_cparams(),
    )(v)


def _shard_rows(ref, idx, rows, axis):
    sl = [slice(None)] * len(ref.shape)
    sl[axis] = pl.ds(idx * rows, rows)
    return ref.at[tuple(sl)]


def _own_and_peer_rows(ref, me, idx, axis):
    rows = ref.shape[axis] // N_DEV
    return _shard_rows(ref, me, rows, axis), _shard_rows(ref, idx, rows, axis)


HBM_SPEC = pl.BlockSpec(memory_space=pltpu.HBM)
SEM_SPEC = pl.BlockSpec(memory_space=pltpu.SEMAPHORE)
ANY_SPEC = pl.BlockSpec(memory_space=pl.ANY)
DATAFLOW = pltpu.SideEffectType.DATAFLOW_SIDE_EFFECTING


def _hbm(a):
    return pltpu.with_memory_space_constraint(a, pltpu.HBM)


def _gather_start(layers, axes, after):
    flat = [a for arrs in layers for a in arrs]
    flat_axes = [ax for axs in axes for ax in axs]
    n, nl = len(flat), len(layers)

    def body(*refs):
        ins, sems, token = refs[:n], refs[n + 1:n + 1 + 2 * nl], refs[-1]
        x, y, c, me = _my_position()
        a0 = 0
        for li, arrs in enumerate(layers):
            for k, (peer, _) in enumerate(_peers(x, y, c)):
                for a in range(len(arrs)):
                    rows, _ = _own_and_peer_rows(ins[a0 + a], me, me, flat_axes[a0 + a])
                    pltpu.make_async_remote_copy(rows, rows, sems[2 * li].at[k * len(arrs) + a],
                                                 sems[2 * li + 1].at[k * len(arrs) + a],
                                                 device_id=peer, device_id_type=MESH).start()
            a0 += len(arrs)
        token[...] = jnp.zeros_like(token)

    sem_shapes = []
    for arrs in layers:
        sem_shapes += [pltpu.SemaphoreType.DMA(((N_DEV - 1) * len(arrs),))] * 2
    out = _call(
        body, name="gather_start",
        out_shape=(*sem_shapes, *[pltpu.HBM(a.shape, a.dtype) for a in flat], jax.ShapeDtypeStruct((8, LANES), F32)),
        in_specs=[HBM_SPEC] * n + [ANY_SPEC],
        out_specs=(*[SEM_SPEC] * (2 * nl), *[HBM_SPEC] * n, pl.BlockSpec(memory_space=pltpu.VMEM)),
        input_output_aliases={a: 2 * nl + a for a in range(n)},
        compiler_params=_cparams(has_side_effects=DATAFLOW),
    )(*[_hbm(a) for a in flat], after)
    per_layer, a0 = [], 0
    for li, arrs in enumerate(layers):
        per_layer.append((out[2 * li], out[2 * li + 1], list(out[2 * nl + a0:2 * nl + a0 + len(arrs)])))
        a0 += len(arrs)
    return per_layer, out[-1]


def _gather_wait(started, axes, after, name):
    send_sems, recv_sems, arrs = started
    n = len(arrs)

    def body(*refs):
        ins, send_ref, recv_ref = refs[:n], refs[n], refs[n + 1]
        x, y, c, me = _my_position()
        for k, (peer, idx) in enumerate(_peers(x, y, c)):
            for a in range(n):
                own, theirs = _own_and_peer_rows(ins[a], me, idx, axes[a])
                cp = pltpu.make_async_remote_copy(own, theirs, send_ref.at[k * n + a], recv_ref.at[k * n + a],
                                                  device_id=peer, device_id_type=MESH)
                cp.wait_send()
                cp.wait_recv()

    return _call(
        body, name=name,
        out_shape=tuple(pltpu.HBM(a.shape, a.dtype) for a in arrs),
        in_specs=[HBM_SPEC] * n + [SEM_SPEC, SEM_SPEC, ANY_SPEC],
        out_specs=tuple([HBM_SPEC] * n),
        input_output_aliases={a: a for a in range(n)},
        compiler_params=_cparams(has_side_effects=DATAFLOW),
    )(*arrs, send_sems, recv_sems, after)


def _scatter_start(fulls, axes, name, after):
    n = len(fulls)
    lands = []
    for f, ax in zip(fulls, axes):
        shp = list(f.shape)
        shp[ax] //= N_DEV
        lands.append(_hbm(lax.empty((N_DEV - 1,) + tuple(shp), f.dtype)))

    def body(*refs):
        srcs, dsts, send_ref, recv_ref, token = refs[:n], refs[n:2 * n], refs[2 * n + 1], refs[2 * n + 2], refs[-1]
        x, y, c, me = _my_position()
        for k, (peer, idx) in enumerate(_peers(x, y, c)):
            for a in range(n):
                _, theirs = _own_and_peer_rows(srcs[a], me, idx, axes[a])
                pltpu.make_async_remote_copy(theirs, dsts[a].at[k], send_ref.at[k * n + a], recv_ref.at[k * n + a],
                                             device_id=peer, device_id_type=MESH).start()
        token[...] = jnp.zeros_like(token)

    sem = pltpu.SemaphoreType.DMA(((N_DEV - 1) * n,))
    out = _call(
        body, name=name,
        out_shape=(sem, sem, *[pltpu.HBM(a.shape, a.dtype) for a in fulls], *[pltpu.HBM(a.shape, a.dtype) for a in lands],
                   jax.ShapeDtypeStruct((8, LANES), F32)),
        in_specs=[HBM_SPEC] * (2 * n) + [ANY_SPEC],
        out_specs=(SEM_SPEC, SEM_SPEC, *[HBM_SPEC] * (2 * n), pl.BlockSpec(memory_space=pltpu.VMEM)),
        input_output_aliases={a: 2 + a for a in range(2 * n)},
        compiler_params=_cparams(has_side_effects=DATAFLOW),
    )(*[_hbm(a) for a in fulls], *lands, after)
    return (out[0], out[1], list(out[2:2 + n]), list(out[2 + n:2 + 2 * n])), out[-1]


def _scatter_wait(started, axes, after, name):
    send_sems, recv_sems, fulls, lands = started
    n = len(fulls)

    def body(*refs):
        srcs, dsts, send_ref, recv_ref = refs[:n], refs[n:2 * n], refs[2 * n], refs[2 * n + 1]
        x, y, c, me = _my_position()
        for k, (peer, idx) in enumerate(_peers(x, y, c)):
            for a in range(n):
                _, theirs = _own_and_peer_rows(srcs[a], me, idx, axes[a])
                cp = pltpu.make_async_remote_copy(theirs, dsts[a].at[k], send_ref.at[k * n + a], recv_ref.at[k * n + a],
                                                  device_id=peer, device_id_type=MESH)
                cp.wait_send()
                cp.wait_recv()

    out = _call(
        body, name=name,
        out_shape=tuple(pltpu.HBM(a.shape, a.dtype) for a in (*fulls, *lands)),
        in_specs=[HBM_SPEC] * (2 * n) + [SEM_SPEC, SEM_SPEC, ANY_SPEC],
        out_specs=tuple([HBM_SPEC] * (2 * n)),
        input_output_aliases={a: a for a in range(2 * n)},
        compiler_params=_cparams(has_side_effects=DATAFLOW),
    )(*fulls, *lands, send_sems, recv_sems, after)
    return list(out[:n]), list(out[n:])


def _prep_weights(me, attn_w_in, attn_w_out, pool_w_in, pool_w_out, pool_w_group):
    nl = attn_w_in.shape[0]

    def body(me_ref, *refs):
        ins, outs = refs[:5 * nl], refs[5 * nl:]
        for j in range(nl):
            awi, awo, pwi, pwo, pwg = ins[5 * j:5 * j + 5]
            o_awi, o_awo, o_pwi, o_pwo, o_pwg = outs[5 * j:5 * j + 5]
            o_awi[...] = awi[...].T.astype(BF16)
            o_awo[...] = awo[...].astype(BF16)
            o_pwi[...] = pwi[...].T.astype(BF16)
            o_pwo[...] = pwo[...].astype(BF16)
            o_pwg[...] = pwg[...].astype(BF16)

    def in_spec(shape, j):
        nd = len(shape)
        return pl.BlockSpec((None,) + tuple(shape), lambda i, me_ref: (j,) + (0,) * nd)

    srcs = (attn_w_in, attn_w_out, pool_w_in, pool_w_out, pool_w_group)
    rows_spec = lambda r: pl.BlockSpec((r, D), lambda i, me_ref: (me_ref[0], 0))
    grp = pool_w_group.shape[1:]
    grp_spec = pl.BlockSpec(grp, lambda i, me_ref: (0, me_ref[0], 0))
    ins, in_specs, out_shapes, out_specs = [], [], [], []
    for j in range(nl):
        ins += list(srcs)
        in_specs += [in_spec(a.shape[1:], j) for a in srcs]
        out_shapes += [(N_DEV * attn_w_in.shape[2], D), (N_DEV * attn_w_out.shape[1], D), (N_DEV * pool_w_in.shape[2], D),
                       (N_DEV * pool_w_out.shape[1], D), (grp[0], N_DEV * grp[1], grp[2])]
        out_specs += [rows_spec(attn_w_in.shape[2]), rows_spec(attn_w_out.shape[1]), rows_spec(pool_w_in.shape[2]),
                      rows_spec(pool_w_out.shape[1]), grp_spec]
    out = _call(
        body, name="prep_weights",
        grid_spec=pltpu.PrefetchScalarGridSpec(num_scalar_prefetch=1, grid=(1,), in_specs=in_specs, out_specs=tuple(out_specs)),
        out_shape=tuple(jax.ShapeDtypeStruct(s, BF16) for s in out_shapes),
        compiler_params=_cparams(1),
    )(me.reshape(1), *ins)
    return [list(out[5 * j:5 * j + 5]) for j in range(nl)]


def _ada_forward(c_all, ada_w):
    cols = ada_w.shape[2]

    def body(c_ref, w_ref, o_ref):
        cv = c_ref[...]
        sc = (cv * _sigmoid(cv)).astype(BF16)
        o_ref[...] = _dot(sc, w_ref[...].astype(BF16))

    return _call(
        body, name="ada_forward", grid=(DEPTH,),
        out_shape=jax.ShapeDtypeStruct((DEPTH, N_DEV, cols), F32),
        in_specs=[pl.BlockSpec((N_DEV, D), lambda i: (0, 0)), pl.BlockSpec((None, D, cols), lambda i: (i, 0, 0))],
        out_specs=pl.BlockSpec((None, N_DEV, cols), lambda i: (i, 0, 0)),
        compiler_params=_cparams(1),
    )(c_all, ada_w)


def _ada_backward_adamw(c_pad, dmod_pad, w, m, v):
    cols = w.shape[2]

    def body(c_ref, dm_ref, w_ref, m_ref, v_ref, g_out, d_out, m_out, v_out):
        cv = c_ref[...]
        sc = (cv * _sigmoid(cv)).astype(BF16)
        g = _dot_tn(sc, dm_ref[...].astype(BF16))
        g_out[...] = g
        d_out[...], m_out[...], v_out[...] = _adamw(w_ref[...], g, m_ref[...], v_ref[...])

    wspec = pl.BlockSpec((None, D, cols), lambda i: (i, 0, 0))
    return _call(
        body, name="ada_backward_adamw", grid=(DEPTH,),
        out_shape=tuple(jax.ShapeDtypeStruct(w.shape, F32) for _ in range(4)),
        in_specs=[pl.BlockSpec((2 * N_DEV, D), lambda i: (0, 0)), pl.BlockSpec((None, 2 * N_DEV, cols), lambda i: (i, 0, 0)),
                  wspec, wspec, wspec],
        out_specs=(wspec, wspec, wspec, wspec),
        compiler_params=_cparams(1),
    )(c_pad, dmod_pad, w, m, v)


def _attn_in_proj(x, pos_col, ng, mod, w_t, j, gain, invf, bd, tile):
    seq = x.shape[0]

    def body(x_ref, pos_ref, ng_ref, mod_ref, w_ref, gain_ref, invf_ref, bd_ref, qk_ref, qs_ref, kd_ref, vd_ref, g_ref):
        _, _, h = _norm_mod(x_ref[...], ng_ref[...], mod_ref[1:2, :], mod_ref[0:1, :])
        proj = _dot_nt(h.astype(BF16), w_ref[...])
        qk_ref[...] = proj[:, :QK_W]
        g_ref[...] = proj[:, QK_W + N_KV * HEAD_DIM:]
        tabs = _rope_tables(pos_ref[...], invf_ref[...], tile)
        low = _low_half(tile)
        bdm = bd_ref[...]

        def put_kv(ref, blk, first_kv):
            sw = pltpu.roll(blk, HEAD_DIM, 1)
            ref[:, LANES * first_kv:LANES * (first_kv + 1)] = jnp.where(low, blk, sw).astype(BF16)
            ref[:, LANES * (first_kv + 1):LANES * (first_kv + 2)] = jnp.where(low, sw, blk).astype(BF16)

        for b in range(QK_W // LANES):
            blk = proj[:, LANES * b:LANES * (b + 1)]
            ms = _dot_split(blk * blk, bdm) * (1.0 / HEAD_DIM)
            y = (blk * lax.rsqrt(ms + NORM_EPS)) * gain_ref[:, LANES * b:LANES * (b + 1)]
            rp = _rope(y, tabs)
            if b < D // LANES:
                rp = rp * (HEAD_DIM ** -0.5)
                qs_ref[:, 2 * LANES * b:2 * LANES * b + LANES] = jnp.where(low, rp, 0.0).astype(BF16)
                qs_ref[:, 2 * LANES * b + LANES:2 * LANES * (b + 1)] = jnp.where(low, 0.0, rp).astype(BF16)
            else:
                put_kv(kd_ref, rp, 2 * (b - D // LANES))
        for b in range(2):
            put_kv(vd_ref, proj[:, QK_W + LANES * b:QK_W + LANES * (b + 1)], 2 * b)

    row = lambda w: pl.BlockSpec((tile, w), lambda i: (i, 0))
    return _call(
        body, name=f"attn_in_proj_{j}", grid=(seq // tile,),
        out_shape=(jax.ShapeDtypeStruct((seq, QK_W), F32), jax.ShapeDtypeStruct((seq, N_HEADS * LANES), BF16),
                   jax.ShapeDtypeStruct((seq, KX_W), BF16), jax.ShapeDtypeStruct((seq, KX_W), BF16),
                   jax.ShapeDtypeStruct((seq, D), F32)),
        in_specs=[row(D), row(1), _const_spec((1, D)), _const_spec((8, D)), _const_spec((ATTN_IN, D)),
                  _const_spec((1, QK_W)), _const_spec((1, LANES)), _const_spec((LANES, LANES))],
        out_specs=(row(QK_W), row(N_HEADS * LANES), row(KX_W), row(KX_W), row(D)),
        compiler_params=_cparams(1),
    )(x, pos_col, ng, mod, w_t, gain, invf, bd)


def _band_mask(n, rows, keys_on_rows):
    shape = (2 * QBLK, rows) if keys_on_rows else (rows, 2 * QBLK)
    qi = lax.broadcasted_iota(jnp.int32, shape, 1 if keys_on_rows else 0) & (QBLK - 1)
    kj = lax.broadcasted_iota(jnp.int32, shape, 0 if keys_on_rows else 1)
    diff = QBLK + qi - kj
    first_key = jnp.where(n > 0, 0, QBLK)
    return (diff >= 0) & (diff < QBLK) & (kj >= first_key)


def _stack_heads(ref, heads):
    return jnp.concatenate([ref[:, LANES * h:LANES * (h + 1)] for h in heads], axis=0)


def _kv_block(prev_ref, cur_ref, kv):
    cols = slice(LANES * kv, LANES * (kv + 1))
    return jnp.concatenate([prev_ref[:, cols], cur_ref[:, cols]], axis=0)


def _pair_up(st, low):
    return jnp.concatenate([jnp.where(low, st[0:QBLK], st[QBLK:2 * QBLK]),
                            jnp.where(low, st[2 * QBLK:3 * QBLK], st[3 * QBLK:4 * QBLK])], axis=1)


def _attn_forward(sinks, qs, kd, vd, j):
    seq = qs.shape[0]
    nb = seq // QBLK

    def body(sink_ref, q_ref, kp_ref, kc_ref, vp_ref, vc_ref, o_ref):
        n = pl.program_id(0)
        ok = _band_mask(n, 4 * QBLK, False)
        low = _low_half(QBLK)
        rowi = lax.broadcasted_iota(jnp.int32, (4 * QBLK, 1), 0)

        def scores(kv):
            return _dot_nt(_stack_heads(q_ref, range(4 * kv, 4 * kv + 4)), _kv_block(kp_ref, kc_ref, kv))

        nxt = scores(0)
        for kv in range(N_KV):
            s = jnp.where(ok, nxt, -1e30)
            if kv + 1 < N_KV:
                nxt = scores(kv + 1)
            sink = jnp.where(rowi < QBLK, sink_ref[4 * kv],
                             jnp.where(rowi < 2 * QBLK, sink_ref[4 * kv + 1],
                                       jnp.where(rowi < 3 * QBLK, sink_ref[4 * kv + 2], sink_ref[4 * kv + 3])))
            m = jnp.maximum(jnp.max(s, axis=1, keepdims=True), sink)
            p = jnp.exp(s - m)
            den = jnp.sum(p, axis=1, keepdims=True) + jnp.exp(sink - m)
            o_st = _dot((p / den).astype(BF16), _kv_block(vp_ref, vc_ref, kv))
            o_ref[:, 2 * LANES * kv:2 * LANES * (kv + 1)] = _pair_up(o_st, low)

    blk = lambda w: pl.BlockSpec((QBLK, w), lambda n: (n, 0))
    prev = lambda w: pl.BlockSpec((QBLK, w), lambda n: (jnp.maximum(n - 1, 0), 0))
    return _call(
        body, name=f"attn_forward_{j}", grid=(nb,),
        out_shape=jax.ShapeDtypeStruct((seq, D), F32),
        in_specs=[pl.BlockSpec(memory_space=pltpu.SMEM), blk(N_HEADS * LANES), prev(KX_W), blk(KX_W), prev(KX_W), blk(KX_W)],
        out_specs=blk(D),
        compiler_params=_cparams(1),
    )(sinks, qs, kd, kd, vd, vd)


def _attn_out_proj(x, o, g, w, j, mod, tile):
    seq = x.shape[0]

    def body(x_ref, o_ref, g_ref, w_ref, mod_ref, xo_ref, br_ref):
        gv = g_ref[...]
        u = (o_ref[...] * (gv * _sigmoid(gv))).astype(BF16)
        br = _dot(u, w_ref[...])
        br_ref[...] = br
        xo_ref[...] = x_ref[...] + mod_ref[2:3, :] * br

    row = pl.BlockSpec((tile, D), lambda i: (i, 0))
    return _call(
        body, name=f"attn_out_proj_{j}", grid=(seq // tile,),
        out_shape=(jax.ShapeDtypeStruct((seq, D), F32), jax.ShapeDtypeStruct((seq, D), F32)),
        in_specs=[row, row, row, _const_spec((D, D)), _const_spec((8, D))],
        out_specs=(row, row),
        compiler_params=_cparams(1),
    )(x, o, g, w, mod)


def _attn_out_proj_bwd(dxn, br, o, g, w, j, mod, tile):
    seq = dxn.shape[0]
    steps = seq // tile

    def body(dxn_ref, br_ref, o_ref, g_ref, w_ref, mod_ref, do_ref, dg_ref, dw_ref, dgate_ref, dw_acc):
        i = pl.program_id(0)

        @pl.when(i == 0)
        def _():
            dw_acc[...] = jnp.zeros_like(dw_acc)
            dgate_ref[...] = jnp.zeros_like(dgate_ref)

        dxn_v, ov, gv = dxn_ref[...], o_ref[...], g_ref[...]
        dgate_ref[...] += jnp.sum(dxn_v * br_ref[...], axis=0, keepdims=True)
        dbr = (dxn_v * mod_ref[2:3, :]).astype(BF16)
        du = _dot_nt(dbr, w_ref[...])
        sg = _sigmoid(gv)
        sl = gv * sg
        dw_acc[...] += _dot_tn((ov * sl).astype(BF16), dbr)
        do = du * sl
        dg_ref[...] = (du * ov * (sg * (1.0 + gv * (1.0 - sg)))).astype(BF16)
        low = _low_half(tile)
        for b in range(D // LANES):
            blk = do[:, LANES * b:LANES * (b + 1)]
            do_ref[:, 2 * LANES * b:2 * LANES * b + LANES] = jnp.where(low, blk, 0.0).astype(BF16)
            do_ref[:, 2 * LANES * b + LANES:2 * LANES * (b + 1)] = jnp.where(low, 0.0, blk).astype(BF16)

        @pl.when(i == steps - 1)
        def _():
            dw_ref[...] = dw_acc[...].astype(BF16)

    row = lambda w_: pl.BlockSpec((tile, w_), lambda i: (i, 0))
    return _call(
        body, name=f"attn_out_proj_bwd_{j}", grid=(steps,),
        out_shape=(jax.ShapeDtypeStruct((seq, N_HEADS * LANES), BF16), jax.ShapeDtypeStruct((seq, D), BF16),
                   jax.ShapeDtypeStruct((D, D), BF16), jax.ShapeDtypeStruct((1, D), F32)),
        in_specs=[row(D), row(D), row(D), row(D), _const_spec((D, D)), _const_spec((8, D))],
        out_specs=(row(N_HEADS * LANES), row(D), pl.BlockSpec((D, D), lambda i: (0, 0)),
                   pl.BlockSpec((1, D), lambda i: (0, 0))),
        scratch_shapes=[pltpu.VMEM((D, D), F32)],
        compiler_params=_cparams(1),
    )(dxn, br, o, g, w, mod)


def _attn_backward(sinks, qs, dos, kd, vd, j):
    seq = qs.shape[0]
    nb = seq // QBLK

    def body(sink_ref, q_ref, do_ref, kp_ref, kc_ref, vp_ref, vc_ref, dq_ref, dk_ref, dv_ref, dsink_ref,
             carry_k, carry_v, sink_acc):
        n = pl.program_id(0)

        @pl.when(n == 0)
        def _():
            carry_k[...] = jnp.zeros_like(carry_k)
            carry_v[...] = jnp.zeros_like(carry_v)
            sink_acc[...] = jnp.zeros_like(sink_acc)

        @pl.when(n < nb)
        def _():
            ok = _band_mask(n, 2 * QBLK, True)
            low = _low_half(QBLK)
            lane_q = lax.broadcasted_iota(jnp.int32, (1, 2 * QBLK), 1)
            dk_parts, dv_parts = [], []

            def first_products(g):
                kv, half = divmod(g, 2)
                heads = (4 * kv + half, 4 * kv + 2 + half)
                q = _stack_heads(q_ref, heads)
                do = _stack_heads(do_ref, heads)
                kk = _kv_block(kp_ref, kc_ref, kv)
                return heads, q, do, kk, _dot_nt(kk, q), _dot_nt(_kv_block(vp_ref, vc_ref, kv), do)

            nxt = first_products(0)
            dq_h, dk_kv, dv_kv = [], None, None
            for g in range(2 * N_KV):
                heads, q, do, kk, s_raw, dp_raw = nxt
                if g + 1 < 2 * N_KV:
                    nxt = first_products(g + 1)
                st = jnp.where(ok, s_raw, -1e30)
                sink = jnp.where(lane_q < QBLK, sink_ref[heads[0]], sink_ref[heads[1]])
                m = jnp.maximum(jnp.max(st, axis=0, keepdims=True), sink)
                e = jnp.exp(st - m)
                e_sink = jnp.exp(sink - m)
                inv = 1.0 / (jnp.sum(e, axis=0, keepdims=True) + e_sink)
                p = e * inv
                pdp = p * dp_raw
                delta = jnp.sum(pdp, axis=0, keepdims=True)
                ds = (pdp - p * delta).astype(BF16)
                sink_acc[g:g + 1, :] -= e_sink * inv * delta
                dk_g, dv_g = _dot(ds, q), _dot(p.astype(BF16), do)
                dk_kv = dk_g if dk_kv is None else dk_kv + dk_g
                dv_kv = dv_g if dv_kv is None else dv_kv + dv_g
                dq_h.append(_dot_tn(ds, kk))
                if g % 2 == 1:
                    kv = g // 2
                    for t in range(2):
                        dq_ref[:, LANES * (2 * kv + t):LANES * (2 * kv + t + 1)] = jnp.where(
                            low, dq_h[0][QBLK * t:QBLK * (t + 1)], dq_h[1][QBLK * t:QBLK * (t + 1)])
                    dk_parts.append(dk_kv + pltpu.roll(dk_kv, HEAD_DIM, 1))
                    dv_parts.append(dv_kv + pltpu.roll(dv_kv, HEAD_DIM, 1))
                    dq_h, dk_kv, dv_kv = [], None, None

            def order(parts, lo, hi):
                return jnp.concatenate([jnp.where(low, parts[0][lo:hi], parts[1][lo:hi]),
                                        jnp.where(low, parts[2][lo:hi], parts[3][lo:hi])], axis=1)

            dk_ref[...] = carry_k[...] + order(dk_parts, 0, QBLK)
            dv_ref[...] = (carry_v[...] + order(dv_parts, 0, QBLK)).astype(BF16)
            carry_k[...] = order(dk_parts, QBLK, 2 * QBLK)
            carry_v[...] = order(dv_parts, QBLK, 2 * QBLK)

        @pl.when(n == nb)
        def _():
            dk_ref[...] = carry_k[...]
            dv_ref[...] = carry_v[...].astype(BF16)
            lane = lax.broadcasted_iota(jnp.int32, (1, LANES), 1)
            out = jnp.zeros((1, LANES), F32)
            for g in range(2 * N_KV):
                for t in range(2):
                    tot = jnp.sum(sink_acc[g:g + 1, QBLK * t:QBLK * (t + 1)], axis=1, keepdims=True)
                    out = jnp.where(lane == 4 * (g // 2) + 2 * t + g % 2, tot, out)
            dsink_ref[...] = out

    cur = lambda w: pl.BlockSpec((QBLK, w), lambda n: (jnp.minimum(n, nb - 1), 0))
    prev = lambda w: pl.BlockSpec((QBLK, w), lambda n: (jnp.maximum(n - 1, 0), 0))
    return _call(
        body, name=f"attn_backward_{j}", grid=(nb + 1,),
        out_shape=(jax.ShapeDtypeStruct((seq, D), F32), jax.ShapeDtypeStruct((seq, N_KV * HEAD_DIM), F32),
                   jax.ShapeDtypeStruct((seq, N_KV * HEAD_DIM), BF16), jax.ShapeDtypeStruct((1, LANES), F32)),
        in_specs=[pl.BlockSpec(memory_space=pltpu.SMEM), cur(N_HEADS * LANES), cur(N_HEADS * LANES), prev(KX_W), cur(KX_W),
                  prev(KX_W), cur(KX_W)],
        out_specs=(cur(D), prev(N_KV * HEAD_DIM), prev(N_KV * HEAD_DIM), pl.BlockSpec((1, LANES), lambda n: (0, 0))),
        scratch_shapes=[pltpu.VMEM((QBLK, N_KV * HEAD_DIM), F32), pltpu.VMEM((QBLK, N_KV * HEAD_DIM), F32),
                        pltpu.VMEM((2 * N_KV, 2 * QBLK), F32)],
        compiler_params=_cparams(1),
    )(sinks, qs, dos, kd, kd, vd, vd)


def _in_proj_tail(x_ref, dxn_ref, ng_ref, mod_ref, w_ref, dproj, dx_ref, dw_acc, vec_acc):
    ng, sc, sh = ng_ref[...], mod_ref[1:2, :], mod_ref[0:1, :]
    xh, r, h = _norm_mod(x_ref[...], ng, sc, sh)
    dh = _dot(dproj, w_ref[...])
    dw_acc[...] += _dot_tn(dproj, h.astype(BF16))
    vec_acc[0:1, :] += jnp.sum(dh, axis=0, keepdims=True)
    vec_acc[1:2, :] += jnp.sum(dh * xh, axis=0, keepdims=True)
    dxh = dh * (ng * (1.0 + sc))
    dx_ref[...] = dxn_ref[...] + r * (dxh - xh * jnp.mean(dxh * xh, axis=-1, keepdims=True))


def _tail_finish(ng_ref, mod_ref, dw_ref, vec_ref, dw_acc, vec_acc):
    dw_ref[...] = dw_acc[...].astype(BF16)
    a = vec_acc[1:2, :]
    vec_ref[...] = jnp.zeros_like(vec_ref)
    vec_ref[0:1, :] = vec_acc[0:1, :]
    vec_ref[1:2, :] = a * ng_ref[...]
    vec_ref[3:4, :] = a * (1.0 + mod_ref[1:2, :])


def _attn_in_proj_bwd(x, dxn, pos_col, qk_raw, dq, dk, dv, dg, ng, mod, w_t, j, gain, invf, bd, tile):
    seq = x.shape[0]
    steps = seq // tile

    def body(x_ref, dxn_ref, pos_ref, qk_ref, dq_ref, dk_ref, dv_ref, dg_ref, ng_ref, mod_ref, w_ref, gain_ref, invf_ref,
             bd_ref, dx_ref, dw_ref, vec_ref, dgain_ref, dproj, dw_acc, vec_acc):
        i = pl.program_id(0)

        @pl.when(i == 0)
        def _():
            dw_acc[...] = jnp.zeros_like(dw_acc)
            vec_acc[...] = jnp.zeros_like(vec_acc)
            dgain_ref[...] = jnp.zeros_like(dgain_ref)

        tabs = _rope_tables(pos_ref[...], invf_ref[...], tile)
        bdm = bd_ref[...]
        for b in range(QK_W // LANES):
            cols = slice(LANES * b, LANES * (b + 1))
            raw = qk_ref[:, cols]
            if b < D // LANES:
                dy = dq_ref[:, cols] * (HEAD_DIM ** -0.5)
            else:
                dy = dk_ref[:, LANES * (b - D // LANES):LANES * (b + 1 - D // LANES)]
            dy = _rope_bwd(dy, tabs)
            rr = lax.rsqrt(_dot_split(raw * raw, bdm) * (1.0 / HEAD_DIM) + NORM_EPS)
            xh = raw * rr
            dgain_ref[:, cols] += jnp.sum(dy * xh, axis=0, keepdims=True)
            dxh = dy * gain_ref[:, cols]
            dproj[:, cols] = (rr * (dxh - xh * (_dot_split(dxh * xh, bdm) * (1.0 / HEAD_DIM)))).astype(BF16)
        dproj[:, QK_W:QK_W + N_KV * HEAD_DIM] = dv_ref[...]
        dproj[:, QK_W + N_KV * HEAD_DIM:] = dg_ref[...]
        _in_proj_tail(x_ref, dxn_ref, ng_ref, mod_ref, w_ref, dproj[...], dx_ref, dw_acc, vec_acc)

        @pl.when(i == steps - 1)
        def _():
            _tail_finish(ng_ref, mod_ref, dw_ref, vec_ref, dw_acc, vec_acc)

    row = lambda w, dt=None: pl.BlockSpec((tile, w), lambda i: (i, 0))
    fixed = lambda shape: pl.BlockSpec(shape, lambda i: (0,) * len(shape))
    return _call(
        body, name=f"attn_in_proj_bwd_{j}", grid=(steps,),
        out_shape=(jax.ShapeDtypeStruct((seq, D), F32), jax.ShapeDtypeStruct((ATTN_IN, D), BF16),
                   jax.ShapeDtypeStruct((8, D), F32), jax.ShapeDtypeStruct((1, QK_W), F32)),
        in_specs=[row(D), row(D), row(1), row(QK_W), row(D), row(N_KV * HEAD_DIM), row(N_KV * HEAD_DIM), row(D),
                  _const_spec((1, D)), _const_spec((8, D)), _const_spec((ATTN_IN, D)), _const_spec((1, QK_W)),
                  _const_spec((1, LANES)), _const_spec((LANES, LANES))],
        out_specs=(row(D), fixed((ATTN_IN, D)), fixed((8, D)), fixed((1, QK_W))),
        scratch_shapes=[pltpu.VMEM((tile, ATTN_IN), BF16), pltpu.VMEM((ATTN_IN, D), F32), pltpu.VMEM((8, D), F32)],
        compiler_params=_cparams(1),
    )(x, dxn, pos_col, qk_raw, dq, dk, dv, dg, ng, mod, w_t, gain, invf, bd)


def _pool_in_proj(x, ng, mod, w_t, j, tile):
    seq = x.shape[0]

    def body(x_ref, ng_ref, mod_ref, w_ref, v_ref, g_ref):
        _, _, h = _norm_mod(x_ref[...], ng_ref[...], mod_ref[1:2, :], mod_ref[0:1, :])
        proj = _dot_nt(h.astype(BF16), w_ref[...])
        v_ref[...] = proj[:, :D]
        g_ref[...] = proj[:, D:]

    row = pl.BlockSpec((tile, D), lambda i: (i, 0))
    return _call(
        body, name=f"pool_in_proj_{j}", grid=(seq // tile,),
        out_shape=(jax.ShapeDtypeStruct((seq, D), F32), jax.ShapeDtypeStruct((seq, D), F32)),
        in_specs=[row, _const_spec((1, D)), _const_spec((8, D)), _const_spec((POOL_IN, D))],
        out_specs=(row, row),
        compiler_params=_cparams(1),
    )(x, ng, mod, w_t)


def _pooled(ext, first, tile):
    t_abs = first + lax.broadcasted_iota(jnp.int32, (tile, 1), 0)
    outs = []
    gw = D // len(POOL_WINDOWS)
    for gi, w in enumerate(POOL_WINDOWS):
        cols = slice(gw * gi, gw * (gi + 1))
        own = ext[HALO:HALO + tile, cols]
        acc = own
        for k in range(1, w):
            acc = acc + ext[HALO - k:HALO - k + tile, cols]
        cnt = jnp.minimum(t_abs + 1, w).astype(F32)
        outs.append(acc / cnt - own)
    return jnp.concatenate(outs, axis=1)


def _fill_ext(ext, halo_ref, v_ref, i, tile):
    ext[0:HALO, :] = jnp.where(i == 0, 0.0, halo_ref[...])
    ext[HALO:HALO + tile, :] = v_ref[...]


def _group_mix(pb, wg_ref):
    gw = D // len(POOL_WINDOWS)
    return jnp.concatenate([_dot(pb[:, gw * gi:gw * (gi + 1)], wg_ref[gi]) for gi in range(len(POOL_WINDOWS))], axis=1)


def _pool_mix_out(x, v, g, wg, w_out, j, scale, mod, tile):
    seq = x.shape[0]

    def body(x_ref, v_ref, halo_ref, g_ref, wg_ref, w_ref, scale_ref, mod_ref, xo_ref, br_ref, ext):
        i = pl.program_id(0)
        _fill_ext(ext, halo_ref, v_ref, i, tile)
        pb = _pooled(ext, i * tile, tile).astype(BF16)
        ms = _group_mix(pb, wg_ref) * scale_ref[...]
        gv = g_ref[...]
        u = (ms * (gv * _sigmoid(gv))).astype(BF16)
        br = _dot(u, w_ref[...])
        br_ref[...] = br
        xo_ref[...] = x_ref[...] + mod_ref[2:3, :] * br

    row = pl.BlockSpec((tile, D), lambda i: (i, 0))
    halo = pl.BlockSpec((HALO, D), lambda i: (jnp.maximum(i * (tile // HALO) - 1, 0), 0))
    return _call(
        body, name=f"pool_mix_out_{j}", grid=(seq // tile,),
        out_shape=(jax.ShapeDtypeStruct((seq, D), F32), jax.ShapeDtypeStruct((seq, D), F32)),
        in_specs=[row, row, halo, row, _const_spec(wg.shape), _const_spec((D, D)), _const_spec((1, D)),
                  _const_spec((8, D))],
        out_specs=(row, row),
        scratch_shapes=[pltpu.VMEM((tile + HALO, D), F32)],
        compiler_params=_cparams(1),
    )(x, v, v, g, wg, w_out, scale, mod)


def _pool_mix_out_bwd(dxn, br, v, g, wg, w_out, j, scale, mod, tile):
    seq = dxn.shape[0]
    steps = seq // tile
    ng_ = len(POOL_WINDOWS)
    gw = D // ng_

    def body(dxn_ref, br_ref, v_ref, halo_ref, g_ref, wg_ref, w_ref, scale_ref, mod_ref,
             dpool_ref, dg_ref, dw_ref, dwg_ref, vec_ref, ext, dw_acc, dwg_acc):
        i = pl.program_id(0)

        @pl.when(i == 0)
        def _():
            dw_acc[...] = jnp.zeros_like(dw_acc)
            dwg_acc[...] = jnp.zeros_like(dwg_acc)
            vec_ref[...] = jnp.zeros_like(vec_ref)

        _fill_ext(ext, halo_ref, v_ref, i, tile)
        pb = _pooled(ext, i * tile, tile).astype(BF16)
        mixed = _group_mix(pb, wg_ref)
        scale = scale_ref[...]
        ms = mixed * scale
        gv, dxn_v = g_ref[...], dxn_ref[...]
        sg = _sigmoid(gv)
        sl = gv * sg
        vec_ref[0:1, :] += jnp.sum(dxn_v * br_ref[...], axis=0, keepdims=True)
        dbr = (dxn_v * mod_ref[2:3, :]).astype(BF16)
        du = _dot_nt(dbr, w_ref[...])
        dw_acc[...] += _dot_tn((ms * sl).astype(BF16), dbr)
        dms = du * sl
        dg_ref[...] = (du * ms * (sg * (1.0 + gv * (1.0 - sg)))).astype(BF16)
        vec_ref[1:2, :] += jnp.sum(dms * mixed, axis=0, keepdims=True)
        dmx = (dms * scale).astype(BF16)
        for gi in range(ng_):
            cols = slice(gw * gi, gw * (gi + 1))
            dpool_ref[:, cols] = _dot_nt(dmx[:, cols], wg_ref[gi])
            dwg_acc[gi] += _dot_tn(pb[:, cols], dmx[:, cols])

        @pl.when(i == steps - 1)
        def _():
            dw_ref[...] = dw_acc[...].astype(BF16)
            dwg_ref[...] = dwg_acc[...].astype(BF16)

    row = pl.BlockSpec((tile, D), lambda i: (i, 0))
    halo = pl.BlockSpec((HALO, D), lambda i: (jnp.maximum(i * (tile // HALO) - 1, 0), 0))
    fixed = lambda shape: pl.BlockSpec(shape, lambda i: (0,) * len(shape))
    return _call(
        body, name=f"pool_mix_out_bwd_{j}", grid=(steps,),
        out_shape=(jax.ShapeDtypeStruct((seq, D), F32), jax.ShapeDtypeStruct((seq, D), BF16),
                   jax.ShapeDtypeStruct((D, D), BF16), jax.ShapeDtypeStruct((ng_, gw, gw), BF16),
                   jax.ShapeDtypeStruct((8, D), F32)),
        in_specs=[row, row, row, halo, row, _const_spec(wg.shape), _const_spec((D, D)), _const_spec((1, D)),
                  _const_spec((8, D))],
        out_specs=(row, row, fixed((D, D)), fixed((ng_, gw, gw)), fixed((8, D))),
        scratch_shapes=[pltpu.VMEM((tile + HALO, D), F32), pltpu.VMEM((D, D), F32), pltpu.VMEM((ng_, gw, gw), F32)],
        compiler_params=_cparams(1),
    )(dxn, br, v, v, g, wg, w_out, scale, mod)


def _pool_in_proj_bwd(x, dxn, dpool, dg, ng, mod, w_t, j, tile):
    seq = x.shape[0]
    steps = seq // tile
    gw = D // len(POOL_WINDOWS)

    def body(x_ref, dxn_ref, dp_ref, halo_ref, dg_ref, ng_ref, mod_ref, w_ref, dx_ref, dw_ref, vec_ref,
             ext, dproj, dw_acc, vec_acc):
        i = pl.program_id(0)

        @pl.when(i == 0)
        def _():
            dw_acc[...] = jnp.zeros_like(dw_acc)
            vec_acc[...] = jnp.zeros_like(vec_acc)

        t_abs = i * tile + lax.broadcasted_iota(jnp.int32, (tile, 1), 0)
        last = i == steps - 1
        for gi, w in enumerate(POOL_WINDOWS):
            cols = slice(gw * gi, gw * (gi + 1))
            cnt = jnp.minimum(t_abs + 1, w).astype(F32)
            ext[0:tile, cols] = dp_ref[:, cols] / cnt
            ext[tile:tile + HALO, cols] = jnp.where(last, 0.0, halo_ref[:, cols] * (1.0 / w))
            acc = ext[0:tile, cols]
            for k in range(1, w):
                acc = acc + ext[k:k + tile, cols]
            dproj[:, cols] = (acc - dp_ref[:, cols]).astype(BF16)
        dproj[:, D:] = dg_ref[...]
        _in_proj_tail(x_ref, dxn_ref, ng_ref, mod_ref, w_ref, dproj[...], dx_ref, dw_acc, vec_acc)

        @pl.when(last)
        def _():
            _tail_finish(ng_ref, mod_ref, dw_ref, vec_ref, dw_acc, vec_acc)

    row = pl.BlockSpec((tile, D), lambda i: (i, 0))
    halo = pl.BlockSpec((HALO, D), lambda i: (jnp.minimum((i + 1) * (tile // HALO), seq // HALO - 1), 0))
    fixed = lambda shape: pl.BlockSpec(shape, lambda i: (0,) * len(shape))
    return _call(
        body, name=f"pool_in_proj_bwd_{j}", grid=(steps,),
        out_shape=(jax.ShapeDtypeStruct((seq, D), F32), jax.ShapeDtypeStruct((POOL_IN, D), BF16),
                   jax.ShapeDtypeStruct((8, D), F32)),
        in_specs=[row, row, row, halo, row, _const_spec((1, D)), _const_spec((8, D)), _const_spec((POOL_IN, D))],
        out_specs=(row, fixed((POOL_IN, D)), fixed((8, D))),
        scratch_shapes=[pltpu.VMEM((tile + HALO, D), F32), pltpu.VMEM((tile, POOL_IN), BF16), pltpu.VMEM((POOL_IN, D), F32),
                        pltpu.VMEM((8, D), F32)],
        compiler_params=_cparams(1),
    )(x, dxn, dpool, dpool, dg, ng, mod, w_t)


def _loss_head(y, target, tile):
    seq = y.shape[0]

    def body(y_ref, t_ref, dy_ref, loss_ref):
        @pl.when(pl.program_id(0) == 0)
        def _():
            loss_ref[...] = jnp.zeros_like(loss_ref)

        e = y_ref[...] - t_ref[...]
        dy_ref[...] = e * (1.0 / D)
        loss_ref[...] += 0.5 * jnp.sum(jnp.mean(e * e, axis=-1, keepdims=True), axis=0, keepdims=True)

    row = pl.BlockSpec((tile, D), lambda i: (i, 0))
    return _call(
        body, name="loss_head", grid=(seq // tile,),
        out_shape=(jax.ShapeDtypeStruct((seq, D), F32), jax.ShapeDtypeStruct((1, LANES), F32)),
        in_specs=[row, row],
        out_specs=(row, pl.BlockSpec((1, LANES), lambda i: (0, 0))),
        compiler_params=_cparams(1),
    )(y, target)


def _build_vec(vecs, gates, pool_vecs, gains, dsinks, loss_part):
    def body(v0, v1, v2, v3, g0, g2, p0, p1, n0, n1, s0, s1, loss_ref, out):
        out[...] = jnp.zeros_like(out)
        for i, v in enumerate((v0, v1, v2, v3)):
            out[3 * i:3 * i + 2, :] = v[0:2, :]
            out[12 + i:13 + i, :] = v[3:4, :]
        out[2:3, :] = g0[...]
        out[8:9, :] = g2[...]
        for j, (p, n, s) in enumerate(((p0, n0, s0), (p1, n1, s1))):
            out[3 * (2 * j + 1) + 2:3 * (2 * j + 1) + 3, :] = p[0:1, :]
            out[22 + j:23 + j, :] = p[1:2, :]
            out[16 + j:17 + j, :] = n[:, 0:D]
            out[18 + j:19 + j, 0:QK_W - D] = n[:, D:QK_W]
            out[20 + j:21 + j, 0:LANES] = s[...]
        out[24:25, 0:LANES] = loss_ref[...]

    vm = pl.BlockSpec(memory_space=pltpu.VMEM)
    args = (*vecs, gates[0], gates[2], *pool_vecs, *gains, *dsinks, loss_part)
    return _call(
        body, name="build_vec",
        out_shape=jax.ShapeDtypeStruct((VEC_ROWS, D), F32),
        in_specs=[vm] * len(args), out_specs=vm,
        compiler_params=_cparams(),
    )(*args)


def _sum_devices(g, after):
    rows = g.shape[1]

    def body(g_ref, after_ref, tot_ref, fold_ref):
        tot = g_ref[0]
        for p in range(1, N_DEV):
            tot = tot + g_ref[p]
        tot_ref[...] = tot
        f = tot[16:24, 0:LANES]
        for b in range(1, D // LANES):
            f = f + tot[16:24, LANES * b:LANES * (b + 1)]
        fold_ref[...] = f + pltpu.roll(f, HEAD_DIM, 1)

    return _call(
        body, name="sum_devices",
        out_shape=(jax.ShapeDtypeStruct((rows, D), F32), jax.ShapeDtypeStruct((8, LANES), F32)),
        in_specs=[pl.BlockSpec(memory_space=pltpu.VMEM), ANY_SPEC],
        out_specs=(pl.BlockSpec(memory_space=pltpu.VMEM), pl.BlockSpec(memory_space=pltpu.VMEM)),
        compiler_params=_cparams(),
    )(g, after)


def _adamw_small(name, w, g, m, v):
    def body(w_ref, g_ref, m_ref, v_ref, d_out, m_out, v_out):
        d_out[...], m_out[...], v_out[...] = _adamw(w_ref[...], g_ref[...], m_ref[...], v_ref[...])

    vm = pl.BlockSpec(memory_space=pltpu.VMEM)
    return _call(
        body, name=name,
        out_shape=tuple(jax.ShapeDtypeStruct(w.shape, F32) for _ in range(3)),
        in_specs=[vm] * 4, out_specs=(vm, vm, vm),
        compiler_params=_cparams(),
    )(w, g, m, v)


def _adamw_shards(name, me, fulls, lands, w, m, v, transpose, axis=0):
    nl = w.shape[0]
    wshape = w.shape[1:]
    own_shape = lands[0].shape[1:]

    def body(me_ref, *refs):
        own_refs, land_refs = refs[:nl], refs[nl:2 * nl]
        w_ref, m_ref, v_ref, g_out, d_out, m_out, v_out = refs[2 * nl:]
        layer = pl.program_id(0)
        for l in range(nl):
            @pl.when(layer == l)
            def _(l=l):
                g = own_refs[l][...].astype(F32)
                for k in range(N_DEV - 1):
                    g = g + land_refs[l][k].astype(F32)
                if transpose:
                    g = g.T
                g_out[...] = g
                d_out[...], m_out[...], v_out[...] = _adamw(w_ref[...], g, m_ref[...], v_ref[...])

    def own_index(l_, me_ref):
        idx = [0] * len(own_shape)
        idx[axis] = me_ref[0]
        return tuple(idx)

    own_spec = pl.BlockSpec(tuple(own_shape), own_index)
    land_spec = pl.BlockSpec((N_DEV - 1,) + tuple(own_shape), lambda l_, me_ref: (0,) * (1 + len(own_shape)))
    wspec = pl.BlockSpec((None,) + tuple(wshape), lambda l_, me_ref: (l_,) + (0,) * len(wshape))
    return _call(
        body, name=name,
        grid_spec=pltpu.PrefetchScalarGridSpec(num_scalar_prefetch=1, grid=(nl,),
                                               in_specs=[own_spec] * nl + [land_spec] * nl + [wspec] * 3,
                                               out_specs=(wspec,) * 4),
        out_shape=tuple(jax.ShapeDtypeStruct(w.shape, F32) for _ in range(4)),
        compiler_params=_cparams(1),
    )(me.reshape(1), *fulls, *lands, w, m, v)


def _constants():
    lane = np.arange(LANES)
    bd = (lane[:, None] // HEAD_DIM == lane[None, :] // HEAD_DIM).astype(np.float32)
    half = ROT_DIM // 2
    inv_freq = ROPE_THETA ** (-jnp.arange(half, dtype=F32) * 2.0 / ROT_DIM)
    invf = jnp.tile(inv_freq, LANES // half).reshape(1, LANES)
    return jnp.asarray(bd, BF16), invf


def kernel(x, c, positions, ada_w, ada_b, norm_g, attn_w_in, attn_q_norm, attn_k_norm, attn_sinks, attn_w_out, pool_w_in, pool_w_group, pool_scale, pool_w_out, loss_target, m_ada_w, m_ada_b, m_norm_g, m_attn_w_in, m_attn_q_norm, m_attn_k_norm, m_attn_sinks, m_attn_w_out, m_pool_w_in, m_pool_w_group, m_pool_scale, m_pool_w_out, v_ada_w, v_ada_b, v_norm_g, v_attn_w_in, v_attn_q_norm, v_attn_k_norm, v_attn_sinks, v_attn_w_out, v_pool_w_in, v_pool_w_group, v_pool_scale, v_pool_w_out):
    seq = x.shape[1]
    me = 4 * lax.axis_index("x") + 2 * lax.axis_index("y") + lax.axis_index("c")
    bd, invf = _constants()
    pos_col = positions.reshape(seq, 1)
    t_mm = min(512, seq)
    t_bw = min(256, seq)
    shard = pool_scale.shape[1]
    cols = ada_w.shape[2]

    layers = _prep_weights(me, attn_w_in, attn_w_out, pool_w_in, pool_w_out, pool_w_group)

    first = jnp.concatenate([c, jnp.pad(pool_scale, ((0, 0), (0, D - shard))), jnp.zeros((5, D), F32)], axis=0)
    first = _allgather_small(first, "allgather_c")
    c_all = first[:, 0, :]
    scale_full = jnp.transpose(first[:, 1:3, :shard], (1, 0, 2)).reshape(2, D)
    mod_part = _ada_forward(c_all, ada_w)
    mod_all = _allgather_small(mod_part.reshape(DEPTH * N_DEV, cols), "allgather_mod")
    mod_all = mod_all.reshape(N_DEV, DEPTH, N_DEV, cols)
    mine = lax.dynamic_index_in_dim(mod_all, me, axis=2, keepdims=False)
    mod = jnp.transpose(mine, (1, 0, 2)).reshape(DEPTH, 3 * D) + ada_b
    mod = jnp.pad(mod.reshape(DEPTH, 3, D), ((0, 0), (0, 5), (0, 0)))

    groups = [[layers[0][0]], [layers[0][1]], layers[0][2:5], layers[1][0:2], layers[1][2:5]]
    gaxes = [(0,), (0,), (0, 0, 1), (0, 0), (0, 0, 1)]
    started, token = _gather_start(groups, gaxes, mod)

    saved, weights = [], []
    h = x[0]
    for i in range(DEPTH):
        j = i // 2
        s = dict(x=h, ng=norm_g[i:i + 1], md=mod[i])
        if i == 0:
            w_in_t, = _gather_wait(started[0], gaxes[0], token, "gather_wait_0_in")
        else:
            wts = _gather_wait(started[i + 1], gaxes[i + 1], h, f"gather_wait_{i}")
        if i % 2 == 0:
            if i > 0:
                w_in_t, w_out = wts
            s["gain"] = jnp.concatenate([jnp.tile(attn_q_norm[j], N_HEADS), jnp.tile(attn_k_norm[j], N_KV)]).reshape(1, QK_W)
            s["qk_raw"], s["qs"], s["kd"], s["vd"], s["g"] = _attn_in_proj(
                h, pos_col, s["ng"], s["md"], w_in_t, j, s["gain"], invf, bd, t_bw)
            s["o"] = _attn_forward(attn_sinks[j], s["qs"], s["kd"], s["vd"], j)
            if i == 0:
                w_out, = _gather_wait(started[1], gaxes[1], s["o"], "gather_wait_0_out")
            h, s["br"] = _attn_out_proj(h, s["o"], s["g"], w_out, j, s["md"], t_mm)
            weights.append((w_in_t, w_out))
        else:
            p_in_t, p_out, p_grp = wts
            s["scale"] = scale_full[j:j + 1]
            s["v"], s["g"] = _pool_in_proj(h, s["ng"], s["md"], p_in_t, j, t_mm)
            h, s["br"] = _pool_mix_out(h, s["v"], s["g"], p_grp, p_out, j, s["scale"], s["md"], t_mm)
            weights.append(wts)
        saved.append(s)
    dx, loss_part = _loss_head(h, loss_target[0], t_mm)

    vecs, gates, gains, dsinks, pool_vecs = [None] * DEPTH, [None] * DEPTH, [None] * 2, [None] * 2, [None] * 2
    sent_in, sent_out = [None] * DEPTH, [None] * DEPTH
    token = jnp.zeros((8, LANES), F32)
    for i in reversed(range(DEPTH)):
        j = i // 2
        s = saved[i]
        md = s["md"] + token[0, 0]
        if i % 2 == 0:
            w_in_t, w_out = weights[i]
            dos, dg, d_w_out, gates[i] = _attn_out_proj_bwd(dx, s["br"], s["o"], s["g"], w_out, j, md, t_mm)
            sent_out[i], token = _scatter_start([d_w_out], (0,), f"scatter_start_{i}_out", token)
            dq, dk, dv, dsinks[j] = _attn_backward(attn_sinks[j] + token[0, 0], s["qs"], dos, s["kd"], s["vd"], j)
            dx, d_in_t, vecs[i], gains[j] = _attn_in_proj_bwd(
                s["x"], dx, pos_col, s["qk_raw"], dq, dk, dv, dg, s["ng"], md, w_in_t, j, s["gain"], invf, bd, t_bw)
        else:
            p_in_t, p_out, p_grp = weights[i]
            dpool, dg, d_p_out, d_p_grp, pool_vecs[j] = _pool_mix_out_bwd(
                dx, s["br"], s["v"], s["g"], p_grp, p_out, j, s["scale"], md, t_mm)
            sent_out[i], token = _scatter_start([d_p_out, d_p_grp], (0, 1), f"scatter_start_{i}_out", token)
            dx, d_in_t, vecs[i] = _pool_in_proj_bwd(s["x"], dx, dpool, dg, s["ng"], s["md"] + token[0, 0], p_in_t, j, t_bw)
        if i > 0:
            sent_in[i], token = _scatter_start([d_in_t], (0,), f"scatter_start_{i}_in", token)

    vec = _build_vec(vecs, gates, pool_vecs, gains, dsinks, loss_part)
    vec_all = _allgather_small(vec, "allgather_vec")
    sent_in[0], token = _scatter_start([d_in_t], (0,), "scatter_start_0_in", vec_all)
    tot, folded = _sum_devices(vec_all, token)
    loss = tot[24, 0]
    small = dict(
        ada_b=(ada_b, tot[0:12].reshape(DEPTH, 3 * D), m_ada_b, v_ada_b),
        norm_g=(norm_g, tot[12:16], m_norm_g, v_norm_g),
        q_norm=(attn_q_norm, folded[0:2, :HEAD_DIM], m_attn_q_norm, v_attn_q_norm),
        k_norm=(attn_k_norm, folded[2:4, :HEAD_DIM], m_attn_k_norm, v_attn_k_norm),
        sinks=(attn_sinks, tot[20:22, :N_HEADS], m_attn_sinks, v_attn_sinks),
        pool_scale=(pool_scale, lax.dynamic_slice(tot, (22, me * shard), (2, shard)), m_pool_scale, v_pool_scale),
    )
    res = {k: (a[1],) + tuple(_adamw_small("adamw_" + k, *a)) for k, a in small.items()}

    dmod_all = vec_all[:, 0:12, :].reshape(N_DEV, DEPTH, 3 * D)
    dmod_mine = lax.dynamic_slice_in_dim(dmod_all, me * cols, cols, axis=2)
    dmod_mine = jnp.pad(jnp.transpose(dmod_mine, (1, 0, 2)), ((0, 0), (0, N_DEV), (0, 0)))
    res["ada_w"] = _ada_backward_adamw(jnp.pad(c_all, ((0, N_DEV), (0, 0))), dmod_mine, ada_w, m_ada_w, v_ada_w)

    got_in, got_out = [None] * DEPTH, [None] * DEPTH
    for i in (3, 1):
        got_out[i] = _scatter_wait(sent_out[i], (0, 1), res["ada_w"][0], f"scatter_wait_{i}_out")
        got_in[i] = _scatter_wait(sent_in[i], (0,), res["ada_w"][0], f"scatter_wait_{i}_in")
    pick = lambda got, ls, a: ([got[i][0][a] for i in ls], [got[i][1][a] for i in ls])
    res["pool_w_in"] = _adamw_shards("adamw_pool_w_in", me, *pick(got_in, (1, 3), 0), pool_w_in, m_pool_w_in, v_pool_w_in, True)
    res["pool_w_out"] = _adamw_shards("adamw_pool_w_out", me, *pick(got_out, (1, 3), 0), pool_w_out, m_pool_w_out,
                                      v_pool_w_out, False)
    res["pool_w_group"] = _adamw_shards("adamw_pool_w_group", me, *pick(got_out, (1, 3), 1), pool_w_group, m_pool_w_group,
                                        v_pool_w_group, False, axis=1)
    for i in (2, 0):
        got_out[i] = _scatter_wait(sent_out[i], (0,), res["pool_w_group"][0], f"scatter_wait_{i}_out")
        got_in[i] = _scatter_wait(sent_in[i], (0,), res["pool_w_group"][0], f"scatter_wait_{i}_in")
    res["attn_w_out"] = _adamw_shards("adamw_attn_w_out", me, *pick(got_out, (0, 2), 0), attn_w_out, m_attn_w_out,
                                      v_attn_w_out, False)
    res["attn_w_in"] = _adamw_shards("adamw_attn_w_in", me, *pick(got_in, (0, 2), 0), attn_w_in, m_attn_w_in, v_attn_w_in, True)

    order = ("ada_w", "ada_b", "norm_g", "attn_w_in", "q_norm", "k_norm", "sinks", "attn_w_out", "pool_w_in",
             "pool_w_group", "pool_scale", "pool_w_out")
    return (loss, dx[None], *[res[k][0] for k in order], *[res[k][1] for k in order], *[res[k][2] for k in order],
            *[res[k][3] for k in order])
```

```python
import functools

import numpy as np
import jax
import jax.numpy as jnp
from jax import lax
from jax.experimental import pallas as pl
from jax.experimental.pallas import tpu as pltpu

F32 = jnp.float32
BF16 = jnp.bfloat16
MESH = pl.DeviceIdType.MESH

N_DEV = 8
D = 1024
DEPTH = 4
HEAD_DIM = 64
N_HEADS = 16
N_KV = 4
QK_W = 1280
ATTN_IN = 2560
POOL_IN = 2048
QBLK = 128
KX_W = N_KV * 128
CHUNK = 256
POOL_WINDOWS = (2, 4, 8, 16)
HALO = 16
ROPE_THETA = 500000.0
ROT_DIM = 16
NORM_EPS = 1e-6
ADAM_LR = 0.001
ADAM_B1 = 0.9
ADAM_B2 = 0.999
ADAM_EPS = 1e-08
ADAM_WD = 0.01
ADAM_STEP = 10

LANES = 128
VMEM_LIMIT = 56 * 2**20
VEC_ROWS = 32


def _cparams(n_grid=0, **kw):
    if n_grid:
        kw["dimension_semantics"] = ("arbitrary",) * n_grid
    return pltpu.CompilerParams(vmem_limit_bytes=VMEM_LIMIT, **kw)


def _call(body, **kw):
    return pl.pallas_call(body, **kw)


def _const_spec(shape):
    nd = len(shape)
    return pl.BlockSpec(shape, lambda *_: (0,) * nd, pipeline_mode=pl.Buffered(1))


def _dot(a, b):
    return jnp.dot(a, b, preferred_element_type=F32)


def _dot_nt(a, b):
    return lax.dot_general(a, b, (((1,), (1,)), ((), ())), preferred_element_type=F32)


def _dot_tn(a, b):
    return lax.dot_general(a, b, (((0,), (0,)), ((), ())), preferred_element_type=F32)


def _dot_split(x, m):
    hi = x.astype(BF16)
    lo = (x - hi.astype(F32)).astype(BF16)
    return _dot(hi, m) + _dot(lo, m)


def _sigmoid(g):
    return 1.0 / (1.0 + jnp.exp(-g))


def _norm_mod(x, ng, sc, sh):
    r = lax.rsqrt(jnp.mean(x * x, axis=-1, keepdims=True) + NORM_EPS)
    xh = x * r
    h = (xh * ng) * (1.0 + sc) + sh
    return xh, r, h


def _rope_table(pos_col, invf_row, tile):
    seq = pos_col.shape[0]

    def body(pos_ref, invf_ref, out_ref):
        ang = pos_ref[...].astype(F32) * invf_ref[...]
        l64 = lax.broadcasted_iota(jnp.int32, (tile, LANES), 1) & (HEAD_DIM - 1)
        cs, sn = jnp.cos(ang), jnp.sin(ang)
        out_ref[:, 0:LANES] = jnp.where(l64 < ROT_DIM, cs, 1.0)
        out_ref[:, LANES:2 * LANES] = jnp.where(l64 < ROT_DIM // 2, -sn, 0.0)
        out_ref[:, 2 * LANES:3 * LANES] = jnp.where((l64 >= ROT_DIM // 2) & (l64 < ROT_DIM), sn, 0.0)

    return _call(
        body, name="rope_table", grid=(seq // tile,),
        out_shape=jax.ShapeDtypeStruct((seq, 3 * LANES), F32),
        in_specs=[pl.BlockSpec((tile, 1), lambda i: (i, 0)), _const_spec((1, LANES))],
        out_specs=pl.BlockSpec((tile, 3 * LANES), lambda i: (i, 0)),
        compiler_params=_cparams(1),
    )(pos_col, invf_row)


def _rope_tabs(rope_ref):
    return rope_ref[:, 0:LANES], rope_ref[:, LANES:2 * LANES], rope_ref[:, 2 * LANES:3 * LANES]


def _rope(y, tabs):
    cos_t, sin_a, sin_b = tabs
    return y * cos_t + pltpu.roll(y, LANES - ROT_DIM // 2, 1) * sin_a + pltpu.roll(y, ROT_DIM // 2, 1) * sin_b


def _rope_bwd(dy, tabs):
    cos_t, sin_a, sin_b = tabs
    return dy * cos_t + pltpu.roll(dy * sin_a, ROT_DIM // 2, 1) + pltpu.roll(dy * sin_b, LANES - ROT_DIM // 2, 1)


def _low_half(rows):
    return lax.broadcasted_iota(jnp.int32, (rows, LANES), 1) < HEAD_DIM


def _adamw(w, g, m, v):
    m = ADAM_B1 * m + (1.0 - ADAM_B1) * g
    v = ADAM_B2 * v + (1.0 - ADAM_B2) * (g * g)
    m_hat = m / (1.0 - ADAM_B1 ** ADAM_STEP)
    v_hat = v / (1.0 - ADAM_B2 ** ADAM_STEP)
    delta = -ADAM_LR * (m_hat / (jnp.sqrt(v_hat) + ADAM_EPS) + ADAM_WD * w)
    return delta, m, v


def _my_position():
    x, y, c = lax.axis_index("x"), lax.axis_index("y"), lax.axis_index("c")
    return x, y, c, 4 * x + 2 * y + c


def _peers(x, y, c):
    out = []
    for k in range(1, N_DEV):
        px = 1 - x if k & 4 else x
        py = 1 - y if k & 2 else y
        pc = 1 - c if k & 1 else c
        out.append(((px, py, pc), 4 * px + 2 * py + pc))
    return out


def _allgather_small(v, name):
    rows, cols = v.shape

    def body(v_ref, out_ref, send_sems, recv_sems, local_sem):
        x, y, c, me = _my_position()
        local = pltpu.make_async_copy(v_ref, out_ref.at[me], local_sem)
        local.start()
        sends = []
        for k, (peer, _) in enumerate(_peers(x, y, c)):
            cp = pltpu.make_async_remote_copy(v_ref, out_ref.at[me], send_sems.at[k], recv_sems.at[k],
                                              device_id=peer, device_id_type=MESH)
            cp.start()
            sends.append(cp)
        for k, (peer, idx) in enumerate(_peers(x, y, c)):
            pltpu.make_async_remote_copy(v_ref, out_ref.at[idx], send_sems.at[k], recv_sems.at[k],
                                         device_id=peer, device_id_type=MESH).wait_recv()
        for cp in sends:
            cp.wait_send()
        local.wait()

    return _call(
        body, name=name,
        out_shape=jax.ShapeDtypeStruct((N_DEV, rows, cols), F32),
        in_specs=[pl.BlockSpec(memory_space=pltpu.VMEM)],
        out_specs=pl.BlockSpec(memory_space=pltpu.VMEM),
        scratch_shapes=[pltpu.SemaphoreType.DMA((N_DEV - 1,)), pltpu.SemaphoreType.DMA((N_DEV - 1,)),
                        pltpu.SemaphoreType.DMA(())],
        compiler_params=_cparams(),
    )(v)


def _shard_rows(ref, idx, rows, axis):
    sl = [slice(None)] * len(ref.shape)
    sl[axis] = pl.ds(idx * rows, rows)
    return ref.at[tuple(sl)]


def _own_and_peer_rows(ref, me, idx, axis):
    rows = ref.shape[axis] // N_DEV
    return _shard_rows(ref, me, rows, axis), _shard_rows(ref, idx, rows, axis)


HBM_SPEC = pl.BlockSpec(memory_space=pltpu.HBM)
SEM_SPEC = pl.BlockSpec(memory_space=pltpu.SEMAPHORE)
ANY_SPEC = pl.BlockSpec(memory_space=pl.ANY)
DATAFLOW = pltpu.SideEffectType.DATAFLOW_SIDE_EFFECTING


def _hbm(a):
    return pltpu.with_memory_space_constraint(a, pltpu.HBM)


def _gather_start(layers, axes, after, name):
    flat = [a for arrs in layers for a in arrs]
    flat_axes = [ax for axs in axes for ax in axs]
    n, nl = len(flat), len(layers)

    def body(*refs):
        ins, sems, token = refs[:n], refs[n + 1:n + 1 + 2 * nl], refs[-1]
        x, y, c, me = _my_position()
        a0 = 0
        for li, arrs in enumerate(layers):
            for k, (peer, _) in enumerate(_peers(x, y, c)):
                for a in range(len(arrs)):
                    rows, _ = _own_and_peer_rows(ins[a0 + a], me, me, flat_axes[a0 + a])
                    pltpu.make_async_remote_copy(rows, rows, sems[2 * li].at[k * len(arrs) + a],
                                                 sems[2 * li + 1].at[k * len(arrs) + a],
                                                 device_id=peer, device_id_type=MESH).start()
            a0 += len(arrs)
        token[...] = jnp.zeros_like(token)

    sem_shapes = []
    for arrs in layers:
        sem_shapes += [pltpu.SemaphoreType.DMA(((N_DEV - 1) * len(arrs),))] * 2
    out = _call(
        body, name=name,
        out_shape=(*sem_shapes, *[pltpu.HBM(a.shape, a.dtype) for a in flat], jax.ShapeDtypeStruct((8, LANES), F32)),
        in_specs=[HBM_SPEC] * n + [ANY_SPEC],
        out_specs=(*[SEM_SPEC] * (2 * nl), *[HBM_SPEC] * n, pl.BlockSpec(memory_space=pltpu.VMEM)),
        input_output_aliases={a: 2 * nl + a for a in range(n)},
        compiler_params=_cparams(has_side_effects=DATAFLOW),
    )(*[_hbm(a) for a in flat], after)
    per_layer, a0 = [], 0
    for li, arrs in enumerate(layers):
        per_layer.append((out[2 * li], out[2 * li + 1], list(out[2 * nl + a0:2 * nl + a0 + len(arrs)])))
        a0 += len(arrs)
    return per_layer, out[-1]


def _gather_wait(started, axes, after, name):
    send_sems, recv_sems, arrs = started
    n = len(arrs)

    def body(*refs):
        ins, send_ref, recv_ref = refs[:n], refs[n], refs[n + 1]
        x, y, c, me = _my_position()
        for k, (peer, idx) in enumerate(_peers(x, y, c)):
            for a in range(n):
                own, theirs = _own_and_peer_rows(ins[a], me, idx, axes[a])
                cp = pltpu.make_async_remote_copy(own, theirs, send_ref.at[k * n + a], recv_ref.at[k * n + a],
                                                  device_id=peer, device_id_type=MESH)
                cp.wait_send()
                cp.wait_recv()

    return _call(
        body, name=name,
        out_shape=tuple(pltpu.HBM(a.shape, a.dtype) for a in arrs),
        in_specs=[HBM_SPEC] * n + [SEM_SPEC, SEM_SPEC, ANY_SPEC],
        out_specs=tuple([HBM_SPEC] * n),
        input_output_aliases={a: a for a in range(n)},
        compiler_params=_cparams(has_side_effects=DATAFLOW),
    )(*arrs, send_sems, recv_sems, after)


def _scatter_start(fulls, axes, name, after):
    n = len(fulls)
    lands = []
    for f, ax in zip(fulls, axes):
        shp = list(f.shape)
        shp[ax] //= N_DEV
        lands.append(_hbm(lax.empty((N_DEV - 1,) + tuple(shp), f.dtype)))

    def body(*refs):
        srcs, dsts, send_ref, recv_ref, token = refs[:n], refs[n:2 * n], refs[2 * n + 1], refs[2 * n + 2], refs[-1]
        x, y, c, me = _my_position()
        for k, (peer, idx) in enumerate(_peers(x, y, c)):
            for a in range(n):
                _, theirs = _own_and_peer_rows(srcs[a], me, idx, axes[a])
                pltpu.make_async_remote_copy(theirs, dsts[a].at[k], send_ref.at[k * n + a], recv_ref.at[k * n + a],
                                             device_id=peer, device_id_type=MESH).start()
        token[...] = jnp.zeros_like(token)

    sem = pltpu.SemaphoreType.DMA(((N_DEV - 1) * n,))
    out = _call(
        body, name=name,
        out_shape=(sem, sem, *[pltpu.HBM(a.shape, a.dtype) for a in fulls], *[pltpu.HBM(a.shape, a.dtype) for a in lands],
                   jax.ShapeDtypeStruct((8, LANES), F32)),
        in_specs=[HBM_SPEC] * (2 * n) + [ANY_SPEC],
        out_specs=(SEM_SPEC, SEM_SPEC, *[HBM_SPEC] * (2 * n), pl.BlockSpec(memory_space=pltpu.VMEM)),
        input_output_aliases={a: 2 + a for a in range(2 * n)},
        compiler_params=_cparams(has_side_effects=DATAFLOW),
    )(*[_hbm(a) for a in fulls], *lands, after)
    return (out[0], out[1], list(out[2:2 + n]), list(out[2 + n:2 + 2 * n])), out[-1]


def _scatter_wait(started, axes, after, name):
    send_sems, recv_sems, fulls, lands = started
    n = len(fulls)

    def body(*refs):
        srcs, dsts, send_ref, recv_ref = refs[:n], refs[n:2 * n], refs[2 * n], refs[2 * n + 1]
        x, y, c, me = _my_position()
        for k, (peer, idx) in enumerate(_peers(x, y, c)):
            for a in range(n):
                _, theirs = _own_and_peer_rows(srcs[a], me, idx, axes[a])
                cp = pltpu.make_async_remote_copy(theirs, dsts[a].at[k], send_ref.at[k * n + a], recv_ref.at[k * n + a],
                                                  device_id=peer, device_id_type=MESH)
                cp.wait_send()
                cp.wait_recv()

    out = _call(
        body, name=name,
        out_shape=tuple(pltpu.HBM(a.shape, a.dtype) for a in (*fulls, *lands)),
        in_specs=[HBM_SPEC] * (2 * n) + [SEM_SPEC, SEM_SPEC, ANY_SPEC],
        out_specs=tuple([HBM_SPEC] * (2 * n)),
        input_output_aliases={a: a for a in range(2 * n)},
        compiler_params=_cparams(has_side_effects=DATAFLOW),
    )(*fulls, *lands, send_sems, recv_sems, after)
    return list(out[:n]), list(out[n:])


def _prep_weights(me, attn_w_in, attn_w_out, pool_w_in, pool_w_out, pool_w_group):
    nl = attn_w_in.shape[0]

    def body(me_ref, *refs):
        ins, outs = refs[:5 * nl], refs[5 * nl:]
        for j in range(nl):
            awi, awo, pwi, pwo, pwg = ins[5 * j:5 * j + 5]
            o_awi, o_awo, o_pwi, o_pwo, o_pwg = outs[5 * j:5 * j + 5]
            o_awi[...] = awi[...].T.astype(BF16)
            o_awo[...] = awo[...].astype(BF16)
            o_pwi[...] = pwi[...].T.astype(BF16)
            o_pwo[...] = pwo[...].astype(BF16)
            o_pwg[...] = pwg[...].astype(BF16)

    def in_spec(shape, j):
        nd = len(shape)
        return pl.BlockSpec((None,) + tuple(shape), lambda i, me_ref: (j,) + (0,) * nd)

    srcs = (attn_w_in, attn_w_out, pool_w_in, pool_w_out, pool_w_group)
    rows_spec = lambda r: pl.BlockSpec((r, D), lambda i, me_ref: (me_ref[0], 0))
    grp = pool_w_group.shape[1:]
    grp_spec = pl.BlockSpec(grp, lambda i, me_ref: (0, me_ref[0], 0))
    ins, in_specs, out_shapes, out_specs = [], [], [], []
    for j in range(nl):
        ins += list(srcs)
        in_specs += [in_spec(a.shape[1:], j) for a in srcs]
        out_shapes += [(N_DEV * attn_w_in.shape[2], D), (N_DEV * attn_w_out.shape[1], D), (N_DEV * pool_w_in.shape[2], D),
                       (N_DEV * pool_w_out.shape[1], D), (grp[0], N_DEV * grp[1], grp[2])]
        out_specs += [rows_spec(attn_w_in.shape[2]), rows_spec(attn_w_out.shape[1]), rows_spec(pool_w_in.shape[2]),
                      rows_spec(pool_w_out.shape[1]), grp_spec]
    out = _call(
        body, name="prep_weights",
        grid_spec=pltpu.PrefetchScalarGridSpec(num_scalar_prefetch=1, grid=(1,), in_specs=in_specs, out_specs=tuple(out_specs)),
        out_shape=tuple(jax.ShapeDtypeStruct(s, BF16) for s in out_shapes),
        compiler_params=_cparams(1),
    )(me.reshape(1), *ins)
    return [list(out[5 * j:5 * j + 5]) for j in range(nl)]


def _ada_forward(c_all, ada_w):
    cols = ada_w.shape[2]

    def body(c_ref, w_ref, o_ref):
        cv = c_ref[...]
        sc = (cv * _sigmoid(cv)).astype(BF16)
        o_ref[...] = _dot(sc, w_ref[...].astype(BF16))

    return _call(
        body, name="ada_forward", grid=(DEPTH,),
        out_shape=jax.ShapeDtypeStruct((DEPTH, N_DEV, cols), F32),
        in_specs=[pl.BlockSpec((N_DEV, D), lambda i: (0, 0)), pl.BlockSpec((None, D, cols), lambda i: (i, 0, 0))],
        out_specs=pl.BlockSpec((None, N_DEV, cols), lambda i: (i, 0, 0)),
        compiler_params=_cparams(1),
    )(c_all, ada_w)


def _ada_backward_adamw(c_pad, dmod_pad, w, m, v):
    cols = w.shape[2]

    def body(c_ref, dm_ref, w_ref, m_ref, v_ref, g_out, d_out, m_out, v_out):
        cv = c_ref[...]
        sc = (cv * _sigmoid(cv)).astype(BF16)
        g = _dot_tn(sc, dm_ref[...].astype(BF16))
        g_out[...] = g
        d_out[...], m_out[...], v_out[...] = _adamw(w_ref[...], g, m_ref[...], v_ref[...])

    wspec = pl.BlockSpec((None, D, cols), lambda i: (i, 0, 0))
    return _call(
        body, name="ada_backward_adamw", grid=(DEPTH,),
        out_shape=tuple(jax.ShapeDtypeStruct(w.shape, F32) for _ in range(4)),
        in_specs=[pl.BlockSpec((2 * N_DEV, D), lambda i: (0, 0)), pl.BlockSpec((None, 2 * N_DEV, cols), lambda i: (i, 0, 0)),
                  wspec, wspec, wspec],
        out_specs=(wspec, wspec, wspec, wspec),
        compiler_params=_cparams(1),
    )(c_pad, dmod_pad, w, m, v)


def _attn_in_proj(x, rope, ng, mod, w_t, j, gain, bd, tile):
    seq = x.shape[0]

    def body(x_ref, rope_ref, ng_ref, mod_ref, w_ref, gain_ref, bd_ref, qk_ref, qs_ref, kd_ref, vd_ref, g_ref):
        _, _, h = _norm_mod(x_ref[...], ng_ref[...], mod_ref[1:2, :], mod_ref[0:1, :])
        hb = h.astype(BF16)
        tabs = _rope_tabs(rope_ref)
        low = _low_half(tile)
        bdm = bd_ref[...]

        def put_kv(ref, blk, first_kv):
            sw = pltpu.roll(blk, HEAD_DIM, 1)
            ref[:, LANES * first_kv:LANES * (first_kv + 1)] = jnp.where(low, blk, sw).astype(BF16)
            ref[:, LANES * (first_kv + 1):LANES * (first_kv + 2)] = jnp.where(low, sw, blk).astype(BF16)

        def project(c):
            return _dot_nt(hb, w_ref[CHUNK * c:CHUNK * (c + 1), :])

        n_chunks = ATTN_IN // CHUNK
        per = CHUNK // LANES
        nxt = project(0)
        for c in range(n_chunks):
            cur = nxt
            if c + 1 < n_chunks:
                nxt = project(c + 1)
            col = CHUNK * c
            if col >= QK_W + N_KV * HEAD_DIM:
                g_ref[:, col - QK_W - N_KV * HEAD_DIM:col - QK_W - N_KV * HEAD_DIM + CHUNK] = cur
            elif col >= QK_W:
                for t in range(per):
                    put_kv(vd_ref, cur[:, LANES * t:LANES * (t + 1)], (col - QK_W) // HEAD_DIM + 2 * t)
            else:
                qk_ref[:, col:col + CHUNK] = cur
                for t in range(per):
                    b = per * c + t
                    blk = cur[:, LANES * t:LANES * (t + 1)]
                    ms = _dot_split(blk * blk, bdm) * (1.0 / HEAD_DIM)
                    y = (blk * lax.rsqrt(ms + NORM_EPS)) * gain_ref[:, LANES * b:LANES * (b + 1)]
                    rp = _rope(y, tabs)
                    if b < D // LANES:
                        rp = rp * (HEAD_DIM ** -0.5)
                        qs_ref[:, 2 * LANES * b:2 * LANES * b + LANES] = jnp.where(low, rp, 0.0).astype(BF16)
                        qs_ref[:, 2 * LANES * b + LANES:2 * LANES * (b + 1)] = jnp.where(low, 0.0, rp).astype(BF16)
                    else:
                        put_kv(kd_ref, rp, 2 * (b - D // LANES))

    row = lambda w: pl.BlockSpec((tile, w), lambda i: (i, 0))
    return _call(
        body, name=f"attn_in_proj_{j}", grid=(seq // tile,),
        out_shape=(jax.ShapeDtypeStruct((seq, QK_W), F32), jax.ShapeDtypeStruct((seq, N_HEADS * LANES), BF16),
                   jax.ShapeDtypeStruct((seq, KX_W), BF16), jax.ShapeDtypeStruct((seq, KX_W), BF16),
                   jax.ShapeDtypeStruct((seq, D), F32)),
        in_specs=[row(D), row(3 * LANES), _const_spec((1, D)), _const_spec((8, D)), _const_spec((ATTN_IN, D)),
                  _const_spec((1, QK_W)), _const_spec((LANES, LANES))],
        out_specs=(row(QK_W), row(N_HEADS * LANES), row(KX_W), row(KX_W), row(D)),
        compiler_params=_cparams(1),
    )(x, rope, ng, mod, w_t, gain, bd)


def _band_mask(n, rows, keys_on_rows):
    shape = (2 * QBLK, rows) if keys_on_rows else (rows, 2 * QBLK)
    qi = lax.broadcasted_iota(jnp.int32, shape, 1 if keys_on_rows else 0) & (QBLK - 1)
    kj = lax.broadcasted_iota(jnp.int32, shape, 0 if keys_on_rows else 1)
    diff = QBLK + qi - kj
    first_key = jnp.where(n > 0, 0, QBLK)
    return (diff >= 0) & (diff < QBLK) & (kj >= first_key)


def _stack_heads(ref, heads):
    return jnp.concatenate([ref[:, LANES * h:LANES * (h + 1)] for h in heads], axis=0)


def _kv_block(prev_ref, cur_ref, kv):
    cols = slice(LANES * kv, LANES * (kv + 1))
    return jnp.concatenate([prev_ref[:, cols], cur_ref[:, cols]], axis=0)


def _pair_up(st, low):
    return jnp.concatenate([jnp.where(low, st[0:QBLK], st[QBLK:2 * QBLK]),
                            jnp.where(low, st[2 * QBLK:3 * QBLK], st[3 * QBLK:4 * QBLK])], axis=1)


def _attn_forward(sinks, qs, kd, vd, j):
    seq = qs.shape[0]
    nb = seq // QBLK

    def body(sink_ref, q_ref, kp_ref, kc_ref, vp_ref, vc_ref, o_ref):
        n = pl.program_id(0)
        ok = _band_mask(n, 4 * QBLK, False)
        low = _low_half(QBLK)
        rowi = lax.broadcasted_iota(jnp.int32, (4 * QBLK, 1), 0)

        def scores(kv):
            return _dot_nt(_stack_heads(q_ref, range(4 * kv, 4 * kv + 4)), _kv_block(kp_ref, kc_ref, kv))

        nxt = scores(0)
        for kv in range(N_KV):
            s = jnp.where(ok, nxt, -1e30)
            if kv + 1 < N_KV:
                nxt = scores(kv + 1)
            sink = jnp.where(rowi < QBLK, sink_ref[4 * kv],
                             jnp.where(rowi < 2 * QBLK, sink_ref[4 * kv + 1],
                                       jnp.where(rowi < 3 * QBLK, sink_ref[4 * kv + 2], sink_ref[4 * kv + 3])))
            m = jnp.maximum(jnp.max(s, axis=1, keepdims=True), sink)
            p = jnp.exp(s - m)
            den = jnp.sum(p, axis=1, keepdims=True) + jnp.exp(sink - m)
            o_st = _dot((p / den).astype(BF16), _kv_block(vp_ref, vc_ref, kv))
            o_ref[:, 2 * LANES * kv:2 * LANES * (kv + 1)] = _pair_up(o_st, low)

    blk = lambda w: pl.BlockSpec((QBLK, w), lambda n: (n, 0))
    prev = lambda w: pl.BlockSpec((QBLK, w), lambda n: (jnp.maximum(n - 1, 0), 0))
    return _call(
        body, name=f"attn_forward_{j}", grid=(nb,),
        out_shape=jax.ShapeDtypeStruct((seq, D), F32),
        in_specs=[pl.BlockSpec(memory_space=pltpu.SMEM), blk(N_HEADS * LANES), prev(KX_W), blk(KX_W), prev(KX_W), blk(KX_W)],
        out_specs=blk(D),
        compiler_params=_cparams(1),
    )(sinks, qs, kd, kd, vd, vd)


def _attn_out_proj(x, o, g, w, j, mod, tile):
    seq = x.shape[0]

    def body(x_ref, o_ref, g_ref, w_ref, mod_ref, xo_ref, br_ref):
        gv = g_ref[...]
        u = (o_ref[...] * (gv * _sigmoid(gv))).astype(BF16)
        br = _dot(u, w_ref[...])
        br_ref[...] = br
        xo_ref[...] = x_ref[...] + mod_ref[2:3, :] * br

    row = pl.BlockSpec((tile, D), lambda i: (i, 0))
    return _call(
        body, name=f"attn_out_proj_{j}", grid=(seq // tile,),
        out_shape=(jax.ShapeDtypeStruct((seq, D), F32), jax.ShapeDtypeStruct((seq, D), F32)),
        in_specs=[row, row, row, _const_spec((D, D)), _const_spec((8, D))],
        out_specs=(row, row),
        compiler_params=_cparams(1),
    )(x, o, g, w, mod)


def _attn_out_proj_bwd(dxn, br, o, g, w, j, mod, tile):
    seq = dxn.shape[0]
    steps = seq // tile

    def body(dxn_ref, br_ref, o_ref, g_ref, w_ref, mod_ref, do_ref, dg_ref, dw_ref, dgate_ref, dw_acc):
        i = pl.program_id(0)

        @pl.when(i == 0)
        def _():
            dw_acc[...] = jnp.zeros_like(dw_acc)
            dgate_ref[...] = jnp.zeros_like(dgate_ref)

        dxn_v, ov, gv = dxn_ref[...], o_ref[...], g_ref[...]
        dgate_ref[...] += jnp.sum(dxn_v * br_ref[...], axis=0, keepdims=True)
        dbr = (dxn_v * mod_ref[2:3, :]).astype(BF16)
        du = _dot_nt(dbr, w_ref[...])
        sg = _sigmoid(gv)
        sl = gv * sg
        dw_acc[...] += _dot_tn((ov * sl).astype(BF16), dbr)
        do = du * sl
        dg_ref[...] = (du * ov * (sg * (1.0 + gv * (1.0 - sg)))).astype(BF16)
        low = _low_half(tile)
        for b in range(D // LANES):
            blk = do[:, LANES * b:LANES * (b + 1)]
            do_ref[:, 2 * LANES * b:2 * LANES * b + LANES] = jnp.where(low, blk, 0.0).astype(BF16)
            do_ref[:, 2 * LANES * b + LANES:2 * LANES * (b + 1)] = jnp.where(low, 0.0, blk).astype(BF16)

        @pl.when(i == steps - 1)
        def _():
            dw_ref[...] = dw_acc[...].astype(BF16)

    row = lambda w_: pl.BlockSpec((tile, w_), lambda i: (i, 0))
    return _call(
        body, name=f"attn_out_proj_bwd_{j}", grid=(steps,),
        out_shape=(jax.ShapeDtypeStruct((seq, N_HEADS * LANES), BF16), jax.ShapeDtypeStruct((seq, D), BF16),
                   jax.ShapeDtypeStruct((D, D), BF16), jax.ShapeDtypeStruct((1, D), F32)),
        in_specs=[row(D), row(D), row(D), row(D), _const_spec((D, D)), _const_spec((8, D))],
        out_specs=(row(N_HEADS * LANES), row(D), pl.BlockSpec((D, D), lambda i: (0, 0)),
                   pl.BlockSpec((1, D), lambda i: (0, 0))),
        scratch_shapes=[pltpu.VMEM((D, D), F32)],
        compiler_params=_cparams(1),
    )(dxn, br, o, g, w, mod)


def _attn_backward(sinks, qs, dos, kd, vd, j):
    seq = qs.shape[0]
    nb = seq // QBLK

    def body(sink_ref, q_ref, do_ref, kp_ref, kc_ref, vp_ref, vc_ref, dq_ref, dk_ref, dv_ref, dsink_ref,
             carry_k, carry_v, sink_acc):
        n = pl.program_id(0)

        @pl.when(n == 0)
        def _():
            carry_k[...] = jnp.zeros_like(carry_k)
            carry_v[...] = jnp.zeros_like(carry_v)
            sink_acc[...] = jnp.zeros_like(sink_acc)

        @pl.when(n < nb)
        def _():
            ok = _band_mask(n, 2 * QBLK, True)
            low = _low_half(QBLK)
            lane_q = lax.broadcasted_iota(jnp.int32, (1, 2 * QBLK), 1)
            dk_parts, dv_parts = [], []

            def first_products(g):
                kv, half = divmod(g, 2)
                heads = (4 * kv + half, 4 * kv + 2 + half)
                q = _stack_heads(q_ref, heads)
                do = _stack_heads(do_ref, heads)
                kk = _kv_block(kp_ref, kc_ref, kv)
                return heads, q, do, kk, _dot_nt(kk, q), _dot_nt(_kv_block(vp_ref, vc_ref, kv), do)

            nxt = first_products(0)
            dq_h, dk_kv, dv_kv = [], None, None
            for g in range(2 * N_KV):
                heads, q, do, kk, s_raw, dp_raw = nxt
                if g + 1 < 2 * N_KV:
                    nxt = first_products(g + 1)
                st = jnp.where(ok, s_raw, -1e30)
                sink = jnp.where(lane_q < QBLK, sink_ref[heads[0]], sink_ref[heads[1]])
                m = jnp.maximum(jnp.max(st, axis=0, keepdims=True), sink)
                e = jnp.exp(st - m)
                e_sink = jnp.exp(sink - m)
                inv = 1.0 / (jnp.sum(e, axis=0, keepdims=True) + e_sink)
                p = e * inv
                pdp = p * dp_raw
                delta = jnp.sum(pdp, axis=0, keepdims=True)
                ds = (pdp - p * delta).astype(BF16)
                sink_acc[g:g + 1, :] -= e_sink * inv * delta
                dk_g, dv_g = _dot(ds, q), _dot(p.astype(BF16), do)
                dk_kv = dk_g if dk_kv is None else dk_kv + dk_g
                dv_kv = dv_g if dv_kv is None else dv_kv + dv_g
                dq_h.append(_dot_tn(ds, kk))
                if g % 2 == 1:
                    kv = g // 2
                    for t in range(2):
                        dq_ref[:, LANES * (2 * kv + t):LANES * (2 * kv + t + 1)] = jnp.where(
                            low, dq_h[0][QBLK * t:QBLK * (t + 1)], dq_h[1][QBLK * t:QBLK * (t + 1)])
                    dk_parts.append(dk_kv + pltpu.roll(dk_kv, HEAD_DIM, 1))
                    dv_parts.append(dv_kv + pltpu.roll(dv_kv, HEAD_DIM, 1))
                    dq_h, dk_kv, dv_kv = [], None, None

            def order(parts, lo, hi):
                return jnp.concatenate([jnp.where(low, parts[0][lo:hi], parts[1][lo:hi]),
                                        jnp.where(low, parts[2][lo:hi], parts[3][lo:hi])], axis=1)

            dk_ref[...] = carry_k[...] + order(dk_parts, 0, QBLK)
            dv_ref[...] = (carry_v[...] + order(dv_parts, 0, QBLK)).astype(BF16)
            carry_k[...] = order(dk_parts, QBLK, 2 * QBLK)
            carry_v[...] = order(dv_parts, QBLK, 2 * QBLK)

        @pl.when(n == nb)
        def _():
            dk_ref[...] = carry_k[...]
            dv_ref[...] = carry_v[...].astype(BF16)
            lane = lax.broadcasted_iota(jnp.int32, (1, LANES), 1)
            out = jnp.zeros((1, LANES), F32)
            for g in range(2 * N_KV):
                for t in range(2):
                    tot = jnp.sum(sink_acc[g:g + 1, QBLK * t:QBLK * (t + 1)], axis=1, keepdims=True)
                    out = jnp.where(lane == 4 * (g // 2) + 2 * t + g % 2, tot, out)
            dsink_ref[...] = out

    cur = lambda w: pl.BlockSpec((QBLK, w), lambda n: (jnp.minimum(n, nb - 1), 0))
    prev = lambda w: pl.BlockSpec((QBLK, w), lambda n: (jnp.maximum(n - 1, 0), 0))
    return _call(
        body, name=f"attn_backward_{j}", grid=(nb + 1,),
        out_shape=(jax.ShapeDtypeStruct((seq, D), F32), jax.ShapeDtypeStruct((seq, N_KV * HEAD_DIM), F32),
                   jax.ShapeDtypeStruct((seq, N_KV * HEAD_DIM), BF16), jax.ShapeDtypeStruct((1, LANES), F32)),
        in_specs=[pl.BlockSpec(memory_space=pltpu.SMEM), cur(N_HEADS * LANES), cur(N_HEADS * LANES), prev(KX_W), cur(KX_W),
                  prev(KX_W), cur(KX_W)],
        out_specs=(cur(D), prev(N_KV * HEAD_DIM), prev(N_KV * HEAD_DIM), pl.BlockSpec((1, LANES), lambda n: (0, 0))),
        scratch_shapes=[pltpu.VMEM((QBLK, N_KV * HEAD_DIM), F32), pltpu.VMEM((QBLK, N_KV * HEAD_DIM), F32),
                        pltpu.VMEM((2 * N_KV, 2 * QBLK), F32)],
        compiler_params=_cparams(1),
    )(sinks, qs, dos, kd, kd, vd, vd)


def _in_proj_tail(x_ref, dxn_ref, ng_ref, mod_ref, w_ref, dproj, dx_ref, dw_acc, vec_acc):
    ng, sc, sh = ng_ref[...], mod_ref[1:2, :], mod_ref[0:1, :]
    xh, r, h = _norm_mod(x_ref[...], ng, sc, sh)
    dh = _dot(dproj, w_ref[...])
    dw_acc[...] += _dot_tn(dproj, h.astype(BF16))
    vec_acc[0:1, :] += jnp.sum(dh, axis=0, keepdims=True)
    vec_acc[1:2, :] += jnp.sum(dh * xh, axis=0, keepdims=True)
    dxh = dh * (ng * (1.0 + sc))
    dx_ref[...] = dxn_ref[...] + r * (dxh - xh * jnp.mean(dxh * xh, axis=-1, keepdims=True))


def _tail_finish(ng_ref, mod_ref, dw_ref, vec_ref, dw_acc, vec_acc):
    dw_ref[...] = dw_acc[...].astype(BF16)
    a = vec_acc[1:2, :]
    vec_ref[...] = jnp.zeros_like(vec_ref)
    vec_ref[0:1, :] = vec_acc[0:1, :]
    vec_ref[1:2, :] = a * ng_ref[...]
    vec_ref[3:4, :] = a * (1.0 + mod_ref[1:2, :])


def _attn_in_proj_bwd(x, dxn, rope, qk_raw, dq, dk, dv, dg, ng, mod, w_t, j, gain, bd, tile):
    seq = x.shape[0]
    steps = seq // tile

    def body(x_ref, dxn_ref, rope_ref, qk_ref, dq_ref, dk_ref, dv_ref, dg_ref, ng_ref, mod_ref, w_ref, gain_ref,
             bd_ref, dx_ref, dw_ref, vec_ref, dgain_ref, dproj, dw_acc, vec_acc):
        i = pl.program_id(0)

        @pl.when(i == 0)
        def _():
            dw_acc[...] = jnp.zeros_like(dw_acc)
            vec_acc[...] = jnp.zeros_like(vec_acc)
            dgain_ref[...] = jnp.zeros_like(dgain_ref)

        tabs = _rope_tabs(rope_ref)
        bdm = bd_ref[...]
        for b in range(QK_W // LANES):
            cols = slice(LANES * b, LANES * (b + 1))
            raw = qk_ref[:, cols]
            if b < D // LANES:
                dy = dq_ref[:, cols] * (HEAD_DIM ** -0.5)
            else:
                dy = dk_ref[:, LANES * (b - D // LANES):LANES * (b + 1 - D // LANES)]
            dy = _rope_bwd(dy, tabs)
            rr = lax.rsqrt(_dot_split(raw * raw, bdm) * (1.0 / HEAD_DIM) + NORM_EPS)
            xh = raw * rr
            dgain_ref[:, cols] += jnp.sum(dy * xh, axis=0, keepdims=True)
            dxh = dy * gain_ref[:, cols]
            dproj[:, cols] = (rr * (dxh - xh * (_dot_split(dxh * xh, bdm) * (1.0 / HEAD_DIM)))).astype(BF16)
        dproj[:, QK_W:QK_W + N_KV * HEAD_DIM] = dv_ref[...]
        dproj[:, QK_W + N_KV * HEAD_DIM:] = dg_ref[...]
        _in_proj_tail(x_ref, dxn_ref, ng_ref, mod_ref, w_ref, dproj[...], dx_ref, dw_acc, vec_acc)

        @pl.when(i == steps - 1)
        def _():
            _tail_finish(ng_ref, mod_ref, dw_ref, vec_ref, dw_acc, vec_acc)

    row = lambda w, dt=None: pl.BlockSpec((tile, w), lambda i: (i, 0))
    fixed = lambda shape: pl.BlockSpec(shape, lambda i: (0,) * len(shape))
    return _call(
        body, name=f"attn_in_proj_bwd_{j}", grid=(steps,),
        out_shape=(jax.ShapeDtypeStruct((seq, D), F32), jax.ShapeDtypeStruct((ATTN_IN, D), BF16),
                   jax.ShapeDtypeStruct((8, D), F32), jax.ShapeDtypeStruct((1, QK_W), F32)),
        in_specs=[row(D), row(D), row(3 * LANES), row(QK_W), row(D), row(N_KV * HEAD_DIM), row(N_KV * HEAD_DIM), row(D),
                  _const_spec((1, D)), _const_spec((8, D)), _const_spec((ATTN_IN, D)), _const_spec((1, QK_W)),
                  _const_spec((LANES, LANES))],
        out_specs=(row(D), fixed((ATTN_IN, D)), fixed((8, D)), fixed((1, QK_W))),
        scratch_shapes=[pltpu.VMEM((tile, ATTN_IN), BF16), pltpu.VMEM((ATTN_IN, D), F32), pltpu.VMEM((8, D), F32)],
        compiler_params=_cparams(1),
    )(x, dxn, rope, qk_raw, dq, dk, dv, dg, ng, mod, w_t, gain, bd)


def _pool_in_proj(x, ng, mod, w_t, j, tile):
    seq = x.shape[0]

    def body(x_ref, ng_ref, mod_ref, w_ref, v_ref, g_ref):
        _, _, h = _norm_mod(x_ref[...], ng_ref[...], mod_ref[1:2, :], mod_ref[0:1, :])
        proj = _dot_nt(h.astype(BF16), w_ref[...])
        v_ref[...] = proj[:, :D]
        g_ref[...] = proj[:, D:]

    row = pl.BlockSpec((tile, D), lambda i: (i, 0))
    return _call(
        body, name=f"pool_in_proj_{j}", grid=(seq // tile,),
        out_shape=(jax.ShapeDtypeStruct((seq, D), F32), jax.ShapeDtypeStruct((seq, D), F32)),
        in_specs=[row, _const_spec((1, D)), _const_spec((8, D)), _const_spec((POOL_IN, D))],
        out_specs=(row, row),
        compiler_params=_cparams(1),
    )(x, ng, mod, w_t)


def _pooled(ext, first, tile):
    t_abs = first + lax.broadcasted_iota(jnp.int32, (tile, 1), 0)
    outs = []
    gw = D // len(POOL_WINDOWS)
    for gi, w in enumerate(POOL_WINDOWS):
        cols = slice(gw * gi, gw * (gi + 1))
        own = ext[HALO:HALO + tile, cols]
        acc = own
        for k in range(1, w):
            acc = acc + ext[HALO - k:HALO - k + tile, cols]
        cnt = jnp.minimum(t_abs + 1, w).astype(F32)
        outs.append(acc / cnt - own)
    return jnp.concatenate(outs, axis=1)


def _fill_ext(ext, halo_ref, v_ref, i, tile):
    ext[0:HALO, :] = jnp.where(i == 0, 0.0, halo_ref[...])
    ext[HALO:HALO + tile, :] = v_ref[...]


def _group_mix(pb, wg_ref):
    gw = D // len(POOL_WINDOWS)
    return jnp.concatenate([_dot(pb[:, gw * gi:gw * (gi + 1)], wg_ref[gi]) for gi in range(len(POOL_WINDOWS))], axis=1)


def _pool_mix_out(x, v, g, wg, w_out, j, scale, mod, tile):
    seq = x.shape[0]

    def body(x_ref, v_ref, halo_ref, g_ref, wg_ref, w_ref, scale_ref, mod_ref, xo_ref, br_ref, ext):
        i = pl.program_id(0)
        _fill_ext(ext, halo_ref, v_ref, i, tile)
        pb = _pooled(ext, i * tile, tile).astype(BF16)
        ms = _group_mix(pb, wg_ref) * scale_ref[...]
        gv = g_ref[...]
        u = (ms * (gv * _sigmoid(gv))).astype(BF16)
        br = _dot(u, w_ref[...])
        br_ref[...] = br
        xo_ref[...] = x_ref[...] + mod_ref[2:3, :] * br

    row = pl.BlockSpec((tile, D), lambda i: (i, 0))
    halo = pl.BlockSpec((HALO, D), lambda i: (jnp.maximum(i * (tile // HALO) - 1, 0), 0))
    return _call(
        body, name=f"pool_mix_out_{j}", grid=(seq // tile,),
        out_shape=(jax.ShapeDtypeStruct((seq, D), F32), jax.ShapeDtypeStruct((seq, D), F32)),
        in_specs=[row, row, halo, row, _const_spec(wg.shape), _const_spec((D, D)), _const_spec((1, D)),
                  _const_spec((8, D))],
        out_specs=(row, row),
        scratch_shapes=[pltpu.VMEM((tile + HALO, D), F32)],
        compiler_params=_cparams(1),
    )(x, v, v, g, wg, w_out, scale, mod)


def _pool_mix_out_bwd(dxn, br, v, g, wg, w_out, j, scale, mod, tile):
    seq = dxn.shape[0]
    steps = seq // tile
    ng_ = len(POOL_WINDOWS)
    gw = D // ng_

    def body(dxn_ref, br_ref, v_ref, halo_ref, g_ref, wg_ref, w_ref, scale_ref, mod_ref,
             dpool_ref, dg_ref, dw_ref, dwg_ref, vec_ref, ext, dw_acc, dwg_acc):
        i = pl.program_id(0)

        @pl.when(i == 0)
        def _():
            dw_acc[...] = jnp.zeros_like(dw_acc)
            dwg_acc[...] = jnp.zeros_like(dwg_acc)
            vec_ref[...] = jnp.zeros_like(vec_ref)

        _fill_ext(ext, halo_ref, v_ref, i, tile)
        pb = _pooled(ext, i * tile, tile).astype(BF16)
        mixed = _group_mix(pb, wg_ref)
        scale = scale_ref[...]
        ms = mixed * scale
        gv, dxn_v = g_ref[...], dxn_ref[...]
        sg = _sigmoid(gv)
        sl = gv * sg
        vec_ref[0:1, :] += jnp.sum(dxn_v * br_ref[...], axis=0, keepdims=True)
        dbr = (dxn_v * mod_ref[2:3, :]).astype(BF16)
        du = _dot_nt(dbr, w_ref[...])
        dw_acc[...] += _dot_tn((ms * sl).astype(BF16), dbr)
        dms = du * sl
        dg_ref[...] = (du * ms * (sg * (1.0 + gv * (1.0 - sg)))).astype(BF16)
        vec_ref[1:2, :] += jnp.sum(dms * mixed, axis=0, keepdims=True)
        dmx = (dms * scale).astype(BF16)
        for gi in range(ng_):
            cols = slice(gw * gi, gw * (gi + 1))
            dpool_ref[:, cols] = _dot_nt(dmx[:, cols], wg_ref[gi])
            dwg_acc[gi] += _dot_tn(pb[:, cols], dmx[:, cols])

        @pl.when(i == steps - 1)
        def _():
            dw_ref[...] = dw_acc[...].astype(BF16)
            dwg_ref[...] = dwg_acc[...].astype(BF16)

    row = pl.BlockSpec((tile, D), lambda i: (i, 0))
    halo = pl.BlockSpec((HALO, D), lambda i: (jnp.maximum(i * (tile // HALO) - 1, 0), 0))
    fixed = lambda shape: pl.BlockSpec(shape, lambda i: (0,) * len(shape))
    return _call(
        body, name=f"pool_mix_out_bwd_{j}", grid=(steps,),
        out_shape=(jax.ShapeDtypeStruct((seq, D), F32), jax.ShapeDtypeStruct((seq, D), BF16),
                   jax.ShapeDtypeStruct((D, D), BF16), jax.ShapeDtypeStruct((ng_, gw, gw), BF16),
                   jax.ShapeDtypeStruct((8, D), F32)),
        in_specs=[row, row, row, halo, row, _const_spec(wg.shape), _const_spec((D, D)), _const_spec((1, D)),
                  _const_spec((8, D))],
        out_specs=(row, row, fixed((D, D)), fixed((ng_, gw, gw)), fixed((8, D))),
        scratch_shapes=[pltpu.VMEM((tile + HALO, D), F32), pltpu.VMEM((D, D), F32), pltpu.VMEM((ng_, gw, gw), F32)],
        compiler_params=_cparams(1),
    )(dxn, br, v, v, g, wg, w_out, scale, mod)


def _pool_in_proj_bwd(x, dxn, dpool, dg, ng, mod, w_t, j, tile):
    seq = x.shape[0]
    steps = seq // tile
    gw = D // len(POOL_WINDOWS)

    def body(x_ref, dxn_ref, dp_ref, halo_ref, dg_ref, ng_ref, mod_ref, w_ref, dx_ref, dw_ref, vec_ref,
             ext, dproj, dw_acc, vec_acc):
        i = pl.program_id(0)

        @pl.when(i == 0)
        def _():
            dw_acc[...] = jnp.zeros_like(dw_acc)
            vec_acc[...] = jnp.zeros_like(vec_acc)

        t_abs = i * tile + lax.broadcasted_iota(jnp.int32, (tile, 1), 0)
        last = i == steps - 1
        for gi, w in enumerate(POOL_WINDOWS):
            cols = slice(gw * gi, gw * (gi + 1))
            cnt = jnp.minimum(t_abs + 1, w).astype(F32)
            ext[0:tile, cols] = dp_ref[:, cols] / cnt
            ext[tile:tile + HALO, cols] = jnp.where(last, 0.0, halo_ref[:, cols] * (1.0 / w))
            acc = ext[0:tile, cols]
            for k in range(1, w):
                acc = acc + ext[k:k + tile, cols]
            dproj[:, cols] = (acc - dp_ref[:, cols]).astype(BF16)
        dproj[:, D:] = dg_ref[...]
        _in_proj_tail(x_ref, dxn_ref, ng_ref, mod_ref, w_ref, dproj[...], dx_ref, dw_acc, vec_acc)

        @pl.when(last)
        def _():
            _tail_finish(ng_ref, mod_ref, dw_ref, vec_ref, dw_acc, vec_acc)

    row = pl.BlockSpec((tile, D), lambda i: (i, 0))
    halo = pl.BlockSpec((HALO, D), lambda i: (jnp.minimum((i + 1) * (tile // HALO), seq // HALO - 1), 0))
    fixed = lambda shape: pl.BlockSpec(shape, lambda i: (0,) * len(shape))
    return _call(
        body, name=f"pool_in_proj_bwd_{j}", grid=(steps,),
        out_shape=(jax.ShapeDtypeStruct((seq, D), F32), jax.ShapeDtypeStruct((POOL_IN, D), BF16),
                   jax.ShapeDtypeStruct((8, D), F32)),
        in_specs=[row, row, row, halo, row, _const_spec((1, D)), _const_spec((8, D)), _const_spec((POOL_IN, D))],
        out_specs=(row, fixed((POOL_IN, D)), fixed((8, D))),
        scratch_shapes=[pltpu.VMEM((tile + HALO, D), F32), pltpu.VMEM((tile, POOL_IN), BF16), pltpu.VMEM((POOL_IN, D), F32),
                        pltpu.VMEM((8, D), F32)],
        compiler_params=_cparams(1),
    )(x, dxn, dpool, dpool, dg, ng, mod, w_t)


def _loss_head(y, target, tile):
    seq = y.shape[0]

    def body(y_ref, t_ref, dy_ref, loss_ref):
        @pl.when(pl.program_id(0) == 0)
        def _():
            loss_ref[...] = jnp.zeros_like(loss_ref)

        e = y_ref[...] - t_ref[...]
        dy_ref[...] = e * (1.0 / D)
        loss_ref[...] += 0.5 * jnp.sum(jnp.mean(e * e, axis=-1, keepdims=True), axis=0, keepdims=True)

    row = pl.BlockSpec((tile, D), lambda i: (i, 0))
    return _call(
        body, name="loss_head", grid=(seq // tile,),
        out_shape=(jax.ShapeDtypeStruct((seq, D), F32), jax.ShapeDtypeStruct((1, LANES), F32)),
        in_specs=[row, row],
        out_specs=(row, pl.BlockSpec((1, LANES), lambda i: (0, 0))),
        compiler_params=_cparams(1),
    )(y, target)


def _build_vec(vecs, gates, pool_vecs, gains, dsinks, loss_part):
    def body(v0, v1, v2, v3, g0, g2, p0, p1, n0, n1, s0, s1, loss_ref, out):
        out[...] = jnp.zeros_like(out)
        for i, v in enumerate((v0, v1, v2, v3)):
            out[3 * i:3 * i + 2, :] = v[0:2, :]
            out[12 + i:13 + i, :] = v[3:4, :]
        out[2:3, :] = g0[...]
        out[8:9, :] = g2[...]
        for j, (p, n, s) in enumerate(((p0, n0, s0), (p1, n1, s1))):
            out[3 * (2 * j + 1) + 2:3 * (2 * j + 1) + 3, :] = p[0:1, :]
            out[22 + j:23 + j, :] = p[1:2, :]
            out[16 + j:17 + j, :] = n[:, 0:D]
            out[18 + j:19 + j, 0:QK_W - D] = n[:, D:QK_W]
            out[20 + j:21 + j, 0:LANES] = s[...]
        out[24:25, 0:LANES] = loss_ref[...]

    vm = pl.BlockSpec(memory_space=pltpu.VMEM)
    args = (*vecs, gates[0], gates[2], *pool_vecs, *gains, *dsinks, loss_part)
    return _call(
        body, name="build_vec",
        out_shape=jax.ShapeDtypeStruct((VEC_ROWS, D), F32),
        in_specs=[vm] * len(args), out_specs=vm,
        compiler_params=_cparams(),
    )(*args)


def _sum_devices(g, after):
    rows = g.shape[1]

    def body(g_ref, after_ref, tot_ref, fold_ref):
        tot = g_ref[0]
        for p in range(1, N_DEV):
            tot = tot + g_ref[p]
        tot_ref[...] = tot
        f = tot[16:24, 0:LANES]
        for b in range(1, D // LANES):
            f = f + tot[16:24, LANES * b:LANES * (b + 1)]
        fold_ref[...] = f + pltpu.roll(f, HEAD_DIM, 1)

    return _call(
        body, name="sum_devices",
        out_shape=(jax.ShapeDtypeStruct((rows, D), F32), jax.ShapeDtypeStruct((8, LANES), F32)),
        in_specs=[pl.BlockSpec(memory_space=pltpu.VMEM), ANY_SPEC],
        out_specs=(pl.BlockSpec(memory_space=pltpu.VMEM), pl.BlockSpec(memory_space=pltpu.VMEM)),
        compiler_params=_cparams(),
    )(g, after)


def _adamw_small(name, w, g, m, v):
    def body(w_ref, g_ref, m_ref, v_ref, d_out, m_out, v_out):
        d_out[...], m_out[...], v_out[...] = _adamw(w_ref[...], g_ref[...], m_ref[...], v_ref[...])

    vm = pl.BlockSpec(memory_space=pltpu.VMEM)
    return _call(
        body, name=name,
        out_shape=tuple(jax.ShapeDtypeStruct(w.shape, F32) for _ in range(3)),
        in_specs=[vm] * 4, out_specs=(vm, vm, vm),
        compiler_params=_cparams(),
    )(w, g, m, v)


def _adamw_shards(name, me, fulls, lands, w, m, v, transpose, axis=0):
    nl = w.shape[0]
    wshape = w.shape[1:]
    own_shape = lands[0].shape[1:]

    def body(me_ref, *refs):
        own_refs, land_refs = refs[:nl], refs[nl:2 * nl]
        w_ref, m_ref, v_ref, g_out, d_out, m_out, v_out = refs[2 * nl:]
        layer = pl.program_id(0)
        for l in range(nl):
            @pl.when(layer == l)
            def _(l=l):
                g = own_refs[l][...].astype(F32)
                for k in range(N_DEV - 1):
                    g = g + land_refs[l][k].astype(F32)
                if transpose:
                    g = g.T
                g_out[...] = g
                d_out[...], m_out[...], v_out[...] = _adamw(w_ref[...], g, m_ref[...], v_ref[...])

    def own_index(l_, me_ref):
        idx = [0] * len(own_shape)
        idx[axis] = me_ref[0]
        return tuple(idx)

    own_spec = pl.BlockSpec(tuple(own_shape), own_index)
    land_spec = pl.BlockSpec((N_DEV - 1,) + tuple(own_shape), lambda l_, me_ref: (0,) * (1 + len(own_shape)))
    wspec = pl.BlockSpec((None,) + tuple(wshape), lambda l_, me_ref: (l_,) + (0,) * len(wshape))
    return _call(
        body, name=name,
        grid_spec=pltpu.PrefetchScalarGridSpec(num_scalar_prefetch=1, grid=(nl,),
                                               in_specs=[own_spec] * nl + [land_spec] * nl + [wspec] * 3,
                                               out_specs=(wspec,) * 4),
        out_shape=tuple(jax.ShapeDtypeStruct(w.shape, F32) for _ in range(4)),
        compiler_params=_cparams(1),
    )(me.reshape(1), *fulls, *lands, w, m, v)


def _constants():
    lane = np.arange(LANES)
    bd = (lane[:, None] // HEAD_DIM == lane[None, :] // HEAD_DIM).astype(np.float32)
    half = ROT_DIM // 2
    inv_freq = ROPE_THETA ** (-jnp.arange(half, dtype=F32) * 2.0 / ROT_DIM)
    invf = jnp.tile(inv_freq, LANES // half).reshape(1, LANES)
    return jnp.asarray(bd, BF16), invf


def kernel(x, c, positions, ada_w, ada_b, norm_g, attn_w_in, attn_q_norm, attn_k_norm, attn_sinks, attn_w_out, pool_w_in, pool_w_group, pool_scale, pool_w_out, loss_target, m_ada_w, m_ada_b, m_norm_g, m_attn_w_in, m_attn_q_norm, m_attn_k_norm, m_attn_sinks, m_attn_w_out, m_pool_w_in, m_pool_w_group, m_pool_scale, m_pool_w_out, v_ada_w, v_ada_b, v_norm_g, v_attn_w_in, v_attn_q_norm, v_attn_k_norm, v_attn_sinks, v_attn_w_out, v_pool_w_in, v_pool_w_group, v_pool_scale, v_pool_w_out):
    seq = x.shape[1]
    me = 4 * lax.axis_index("x") + 2 * lax.axis_index("y") + lax.axis_index("c")
    bd, invf = _constants()
    t_mm = min(512, seq)
    rope = _rope_table(positions.reshape(seq, 1), invf, t_mm)
    t_bw = min(256, seq)
    shard = pool_scale.shape[1]
    cols = ada_w.shape[2]

    layers = _prep_weights(me, attn_w_in, attn_w_out, pool_w_in, pool_w_out, pool_w_group)
    started, token = _gather_start([[layers[0][0]]], [(0,)], c, "gather_start_first")

    first = jnp.concatenate([c, jnp.pad(pool_scale, ((0, 0), (0, D - shard))), jnp.zeros((5, D), F32)], axis=0)
    first = _allgather_small(first + token[0, 0], "allgather_c")
    c_all = first[:, 0, :]
    scale_full = jnp.transpose(first[:, 1:3, :shard], (1, 0, 2)).reshape(2, D)
    mod_part = _ada_forward(c_all, ada_w)
    mod_all = _allgather_small(mod_part.reshape(DEPTH * N_DEV, cols), "allgather_mod")
    mod_all = mod_all.reshape(N_DEV, DEPTH, N_DEV, cols)
    mine = lax.dynamic_index_in_dim(mod_all, me, axis=2, keepdims=False)
    mod = jnp.transpose(mine, (1, 0, 2)).reshape(DEPTH, 3 * D) + ada_b
    mod = jnp.pad(mod.reshape(DEPTH, 3, D), ((0, 0), (0, 5), (0, 0)))

    groups = [[layers[0][1]], layers[0][2:5], layers[1][0:2], layers[1][2:5]]
    gaxes = [(0,), (0,), (0, 0, 1), (0, 0), (0, 0, 1)]
    rest, token = _gather_start(groups, gaxes[1:], mod, "gather_start_rest")
    started = started + rest

    saved, weights = [], []
    h = x[0]
    for i in range(DEPTH):
        j = i // 2
        s = dict(x=h, ng=norm_g[i:i + 1], md=mod[i])
        if i == 0:
            w_in_t, = _gather_wait(started[0], gaxes[0], token, "gather_wait_0_in")
        else:
            wts = _gather_wait(started[i + 1], gaxes[i + 1], h, f"gather_wait_{i}")
        if i % 2 == 0:
            if i > 0:
                w_in_t, w_out = wts
            s["gain"] = jnp.concatenate([jnp.tile(attn_q_norm[j], N_HEADS), jnp.tile(attn_k_norm[j], N_KV)]).reshape(1, QK_W)
            s["qk_raw"], s["qs"], s["kd"], s["vd"], s["g"] = _attn_in_proj(
                h, rope, s["ng"], s["md"], w_in_t, j, s["gain"], bd, t_bw)
            s["o"] = _attn_forward(attn_sinks[j], s["qs"], s["kd"], s["vd"], j)
            if i == 0:
                w_out, = _gather_wait(started[1], gaxes[1], s["o"], "gather_wait_0_out")
            h, s["br"] = _attn_out_proj(h, s["o"], s["g"], w_out, j, s["md"], t_mm)
            weights.append((w_in_t, w_out))
        else:
            p_in_t, p_out, p_grp = wts
            s["scale"] = scale_full[j:j + 1]
            s["v"], s["g"] = _pool_in_proj(h, s["ng"], s["md"], p_in_t, j, t_mm)
            h, s["br"] = _pool_mix_out(h, s["v"], s["g"], p_grp, p_out, j, s["scale"], s["md"], t_mm)
            weights.append(wts)
        saved.append(s)
    dx, loss_part = _loss_head(h, loss_target[0], t_mm)

    vecs, gates, gains, dsinks, pool_vecs = [None] * DEPTH, [None] * DEPTH, [None] * 2, [None] * 2, [None] * 2
    sent_in, sent_out = [None] * DEPTH, [None] * DEPTH
    token = jnp.zeros((8, LANES), F32)
    for i in reversed(range(DEPTH)):
        j = i // 2
        s = saved[i]
        md = s["md"] + token[0, 0]
        if i % 2 == 0:
            w_in_t, w_out = weights[i]
            dos, dg, d_w_out, gates[i] = _attn_out_proj_bwd(dx, s["br"], s["o"], s["g"], w_out, j, md, t_mm)
            sent_out[i], token = _scatter_start([d_w_out], (0,), f"scatter_start_{i}_out", token)
            dq, dk, dv, dsinks[j] = _attn_backward(attn_sinks[j] + token[0, 0], s["qs"], dos, s["kd"], s["vd"], j)
            dx, d_in_t, vecs[i], gains[j] = _attn_in_proj_bwd(
                s["x"], dx, rope, s["qk_raw"], dq, dk, dv, dg, s["ng"], md, w_in_t, j, s["gain"], bd, t_bw)
        else:
            p_in_t, p_out, p_grp = weights[i]
            dpool, dg, d_p_out, d_p_grp, pool_vecs[j] = _pool_mix_out_bwd(
                dx, s["br"], s["v"], s["g"], p_grp, p_out, j, s["scale"], md, t_mm)
            sent_out[i], token = _scatter_start([d_p_out, d_p_grp], (0, 1), f"scatter_start_{i}_out", token)
            dx, d_in_t, vecs[i] = _pool_in_proj_bwd(s["x"], dx, dpool, dg, s["ng"], s["md"] + token[0, 0], p_in_t, j, t_bw)
        if i > 0:
            sent_in[i], token = _scatter_start([d_in_t], (0,), f"scatter_start_{i}_in", token)

    vec = _build_vec(vecs, gates, pool_vecs, gains, dsinks, loss_part)
    vec_all = _allgather_small(vec, "allgather_vec")
    sent_in[0], token = _scatter_start([d_in_t], (0,), "scatter_start_0_in", vec_all)
    tot, folded = _sum_devices(vec_all, token)
    loss = tot[24, 0]
    small = dict(
        ada_b=(ada_b, tot[0:12].reshape(DEPTH, 3 * D), m_ada_b, v_ada_b),
        norm_g=(norm_g, tot[12:16], m_norm_g, v_norm_g),
        q_norm=(attn_q_norm, folded[0:2, :HEAD_DIM], m_attn_q_norm, v_attn_q_norm),
        k_norm=(attn_k_norm, folded[2:4, :HEAD_DIM], m_attn_k_norm, v_attn_k_norm),
        sinks=(attn_sinks, tot[20:22, :N_HEADS], m_attn_sinks, v_attn_sinks),
        pool_scale=(pool_scale, lax.dynamic_slice(tot, (22, me * shard), (2, shard)), m_pool_scale, v_pool_scale),
    )
    res = {k: (a[1],) + tuple(_adamw_small("adamw_" + k, *a)) for k, a in small.items()}

    dmod_all = vec_all[:, 0:12, :].reshape(N_DEV, DEPTH, 3 * D)
    dmod_mine = lax.dynamic_slice_in_dim(dmod_all, me * cols, cols, axis=2)
    dmod_mine = jnp.pad(jnp.transpose(dmod_mine, (1, 0, 2)), ((0, 0), (0, N_DEV), (0, 0))) + token[0, 0]
    res["ada_w"] = _ada_backward_adamw(jnp.pad(c_all, ((0, N_DEV), (0, 0))), dmod_mine, ada_w, m_ada_w, v_ada_w)

    got_in, got_out = [None] * DEPTH, [None] * DEPTH
    for i in (3, 1):
        got_out[i] = _scatter_wait(sent_out[i], (0, 1), res["ada_w"][0], f"scatter_wait_{i}_out")
        got_in[i] = _scatter_wait(sent_in[i], (0,), res["ada_w"][0], f"scatter_wait_{i}_in")
    pick = lambda got, ls, a: ([got[i][0][a] for i in ls], [got[i][1][a] for i in ls])
    res["pool_w_in"] = _adamw_shards("adamw_pool_w_in", me, *pick(got_in, (1, 3), 0), pool_w_in, m_pool_w_in, v_pool_w_in, True)
    res["pool_w_out"] = _adamw_shards("adamw_pool_w_out", me, *pick(got_out, (1, 3), 0), pool_w_out, m_pool_w_out,
                                      v_pool_w_out, False)
    res["pool_w_group"] = _adamw_shards("adamw_pool_w_group", me, *pick(got_out, (1, 3), 1), pool_w_group, m_pool_w_group,
                                        v_pool_w_group, False, axis=1)
    for i in (2, 0):
        got_out[i] = _scatter_wait(sent_out[i], (0,), res["pool_w_group"][0], f"scatter_wait_{i}_out")
        got_in[i] = _scatter_wait(sent_in[i], (0,), res["pool_w_group"][0], f"scatter_wait_{i}_in")
    res["attn_w_out"] = _adamw_shards("adamw_attn_w_out", me, *pick(got_out, (0, 2), 0), attn_w_out, m_attn_w_out,
                                      v_attn_w_out, False)
    res["attn_w_in"] = _adamw_shards("adamw_attn_w_in", me, *pick(got_in, (0, 2), 0), attn_w_in, m_attn_w_in, v_attn_w_in, True)

    order = ("ada_w", "ada_b", "norm_g", "attn_w_in", "q_norm", "k_norm", "sinks", "attn_w_out", "pool_w_in",
             "pool_w_group", "pool_scale", "pool_w_out")
    return (loss, dx[None], *[res[k][0] for k in order], *[res[k][1] for k in order], *[res[k][2] for k in order],
            *[res[k][3] for k in order])
```

```python
import functools

import numpy as np
import jax
import jax.numpy as jnp
from jax import lax
from jax.experimental import pallas as pl
from jax.experimental.pallas import tpu as pltpu

F32 = jnp.float32
BF16 = jnp.bfloat16
MESH = pl.DeviceIdType.MESH

N_DEV = 8
D = 1024
DEPTH = 4
HEAD_DIM = 64
N_HEADS = 16
N_KV = 4
QK_W = 1280
ATTN_IN = 2560
POOL_IN = 2048
QBLK = 128
KX_W = N_KV * 128
CHUNK = 256
POOL_WINDOWS = (2, 4, 8, 16)
HALO = 16
ROPE_THETA = 500000.0
ROT_DIM = 16
NORM_EPS = 1e-6
ADAM_LR = 0.001
ADAM_B1 = 0.9
ADAM_B2 = 0.999
ADAM_EPS = 1e-08
ADAM_WD = 0.01
ADAM_STEP = 10

LANES = 128
VMEM_LIMIT = 56 * 2**20
VEC_ROWS = 32


def _cparams(n_grid=0, **kw):
    if n_grid:
        kw["dimension_semantics"] = ("arbitrary",) * n_grid
    return pltpu.CompilerParams(vmem_limit_bytes=VMEM_LIMIT, **kw)


def _call(body, **kw):
    return pl.pallas_call(body, **kw)


def _const_spec(shape):
    nd = len(shape)
    return pl.BlockSpec(shape, lambda *_: (0,) * nd, pipeline_mode=pl.Buffered(1))


def _dot(a, b):
    return jnp.dot(a, b, preferred_element_type=F32)


def _dot_nt(a, b):
    return lax.dot_general(a, b, (((1,), (1,)), ((), ())), preferred_element_type=F32)


def _dot_tn(a, b):
    return lax.dot_general(a, b, (((0,), (0,)), ((), ())), preferred_element_type=F32)


def _dot_split(x, m):
    hi = x.astype(BF16)
    lo = (x - hi.astype(F32)).astype(BF16)
    return _dot(hi, m) + _dot(lo, m)


def _sigmoid(g):
    return 1.0 / (1.0 + jnp.exp(-g))


def _norm_mod(x, ng, sc, sh):
    r = lax.rsqrt(jnp.mean(x * x, axis=-1, keepdims=True) + NORM_EPS)
    xh = x * r
    h = (xh * ng) * (1.0 + sc) + sh
    return xh, r, h


def _rope_table(pos_col, invf_row, tile):
    seq = pos_col.shape[0]

    def body(pos_ref, invf_ref, out_ref):
        ang = pos_ref[...].astype(F32) * invf_ref[...]
        l64 = lax.broadcasted_iota(jnp.int32, (tile, LANES), 1) & (HEAD_DIM - 1)
        cs, sn = jnp.cos(ang), jnp.sin(ang)
        out_ref[:, 0:LANES] = jnp.where(l64 < ROT_DIM, cs, 1.0)
        out_ref[:, LANES:2 * LANES] = jnp.where(l64 < ROT_DIM // 2, -sn, 0.0)
        out_ref[:, 2 * LANES:3 * LANES] = jnp.where((l64 >= ROT_DIM // 2) & (l64 < ROT_DIM), sn, 0.0)

    return _call(
        body, name="rope_table", grid=(seq // tile,),
        out_shape=jax.ShapeDtypeStruct((seq, 3 * LANES), F32),
        in_specs=[pl.BlockSpec((tile, 1), lambda i: (i, 0)), _const_spec((1, LANES))],
        out_specs=pl.BlockSpec((tile, 3 * LANES), lambda i: (i, 0)),
        compiler_params=_cparams(1),
    )(pos_col, invf_row)


def _rope_tabs(rope_ref):
    return rope_ref[:, 0:LANES], rope_ref[:, LANES:2 * LANES], rope_ref[:, 2 * LANES:3 * LANES]


def _rope(y, tabs):
    cos_t, sin_a, sin_b = tabs
    return y * cos_t + pltpu.roll(y, LANES - ROT_DIM // 2, 1) * sin_a + pltpu.roll(y, ROT_DIM // 2, 1) * sin_b


def _rope_bwd(dy, tabs):
    cos_t, sin_a, sin_b = tabs
    return dy * cos_t + pltpu.roll(dy * sin_a, ROT_DIM // 2, 1) + pltpu.roll(dy * sin_b, LANES - ROT_DIM // 2, 1)


def _low_half(rows):
    return lax.broadcasted_iota(jnp.int32, (rows, LANES), 1) < HEAD_DIM


def _adamw(w, g, m, v):
    m = ADAM_B1 * m + (1.0 - ADAM_B1) * g
    v = ADAM_B2 * v + (1.0 - ADAM_B2) * (g * g)
    m_hat = m / (1.0 - ADAM_B1 ** ADAM_STEP)
    v_hat = v / (1.0 - ADAM_B2 ** ADAM_STEP)
    delta = -ADAM_LR * (m_hat / (jnp.sqrt(v_hat) + ADAM_EPS) + ADAM_WD * w)
    return delta, m, v


def _my_position():
    x, y, c = lax.axis_index("x"), lax.axis_index("y"), lax.axis_index("c")
    return x, y, c, 4 * x + 2 * y + c


def _peers(x, y, c):
    out = []
    for k in range(1, N_DEV):
        px = 1 - x if k & 4 else x
        py = 1 - y if k & 2 else y
        pc = 1 - c if k & 1 else c
        out.append(((px, py, pc), 4 * px + 2 * py + pc))
    return out


def _allgather_small(v, name, after):
    rows, cols = v.shape

    def body(v_ref, after_ref, out_ref, send_sems, recv_sems, local_sem):
        x, y, c, me = _my_position()
        local = pltpu.make_async_copy(v_ref, out_ref.at[me], local_sem)
        local.start()
        sends = []
        for k, (peer, _) in enumerate(_peers(x, y, c)):
            cp = pltpu.make_async_remote_copy(v_ref, out_ref.at[me], send_sems.at[k], recv_sems.at[k],
                                              device_id=peer, device_id_type=MESH)
            cp.start()
            sends.append(cp)
        for k, (peer, idx) in enumerate(_peers(x, y, c)):
            pltpu.make_async_remote_copy(v_ref, out_ref.at[idx], send_sems.at[k], recv_sems.at[k],
                                         device_id=peer, device_id_type=MESH).wait_recv()
        for cp in sends:
            cp.wait_send()
        local.wait()

    return _call(
        body, name=name,
        out_shape=jax.ShapeDtypeStruct((N_DEV, rows, cols), F32),
        in_specs=[pl.BlockSpec(memory_space=pltpu.VMEM), pl.BlockSpec(memory_space=pl.ANY)],
        out_specs=pl.BlockSpec(memory_space=pltpu.VMEM),
        scratch_shapes=[pltpu.SemaphoreType.DMA((N_DEV - 1,)), pltpu.SemaphoreType.DMA((N_DEV - 1,)),
                        pltpu.SemaphoreType.DMA(())],
        compiler_params=_cparams(),
    )(v, after)


def _shard_rows(ref, idx, rows, axis):
    sl = [slice(None)] * len(ref.shape)
    sl[axis] = pl.ds(idx * rows, rows)
    return ref.at[tuple(sl)]


def _own_and_peer_rows(ref, me, idx, axis):
    rows = ref.shape[axis] // N_DEV
    return _shard_rows(ref, me, rows, axis), _shard_rows(ref, idx, rows, axis)


HBM_SPEC = pl.BlockSpec(memory_space=pltpu.HBM)
SEM_SPEC = pl.BlockSpec(memory_space=pltpu.SEMAPHORE)
ANY_SPEC = pl.BlockSpec(memory_space=pl.ANY)
DATAFLOW = pltpu.SideEffectType.DATAFLOW_SIDE_EFFECTING


def _hbm(a):
    return pltpu.with_memory_space_constraint(a, pltpu.HBM)


def _gather_start(layers, axes, after, name):
    flat = [a for arrs in layers for a in arrs]
    flat_axes = [ax for axs in axes for ax in axs]
    n, nl = len(flat), len(layers)

    def body(*refs):
        ins, sems, token = refs[:n], refs[n + 1:n + 1 + 2 * nl], refs[-1]
        x, y, c, me = _my_position()
        a0 = 0
        for li, arrs in enumerate(layers):
            for k, (peer, _) in enumerate(_peers(x, y, c)):
                for a in range(len(arrs)):
                    rows, _ = _own_and_peer_rows(ins[a0 + a], me, me, flat_axes[a0 + a])
                    pltpu.make_async_remote_copy(rows, rows, sems[2 * li].at[k * len(arrs) + a],
                                                 sems[2 * li + 1].at[k * len(arrs) + a],
                                                 device_id=peer, device_id_type=MESH).start()
            a0 += len(arrs)
        token[...] = jnp.zeros_like(token)

    sem_shapes = []
    for arrs in layers:
        sem_shapes += [pltpu.SemaphoreType.DMA(((N_DEV - 1) * len(arrs),))] * 2
    out = _call(
        body, name=name,
        out_shape=(*sem_shapes, *[pltpu.HBM(a.shape, a.dtype) for a in flat], jax.ShapeDtypeStruct((8, LANES), F32)),
        in_specs=[HBM_SPEC] * n + [ANY_SPEC],
        out_specs=(*[SEM_SPEC] * (2 * nl), *[HBM_SPEC] * n, pl.BlockSpec(memory_space=pltpu.VMEM)),
        input_output_aliases={a: 2 * nl + a for a in range(n)},
        compiler_params=_cparams(has_side_effects=DATAFLOW),
    )(*[_hbm(a) for a in flat], after)
    per_layer, a0 = [], 0
    for li, arrs in enumerate(layers):
        per_layer.append((out[2 * li], out[2 * li + 1], list(out[2 * nl + a0:2 * nl + a0 + len(arrs)])))
        a0 += len(arrs)
    return per_layer, out[-1]


def _gather_wait(started, axes, after, name):
    send_sems, recv_sems, arrs = started
    n = len(arrs)

    def body(*refs):
        ins, send_ref, recv_ref = refs[:n], refs[n], refs[n + 1]
        x, y, c, me = _my_position()
        for k, (peer, idx) in enumerate(_peers(x, y, c)):
            for a in range(n):
                own, theirs = _own_and_peer_rows(ins[a], me, idx, axes[a])
                cp = pltpu.make_async_remote_copy(own, theirs, send_ref.at[k * n + a], recv_ref.at[k * n + a],
                                                  device_id=peer, device_id_type=MESH)
                cp.wait_send()
                cp.wait_recv()

    return _call(
        body, name=name,
        out_shape=tuple(pltpu.HBM(a.shape, a.dtype) for a in arrs),
        in_specs=[HBM_SPEC] * n + [SEM_SPEC, SEM_SPEC, ANY_SPEC],
        out_specs=tuple([HBM_SPEC] * n),
        input_output_aliases={a: a for a in range(n)},
        compiler_params=_cparams(has_side_effects=DATAFLOW),
    )(*arrs, send_sems, recv_sems, after)


def _first_relations(x, y, c):
    return [(x, y, 1 - c), (1 - x, y, c), (x, 1 - y, c), (1 - x, 1 - y, c)]


def _gather_first_start(arr, after):
    n_rel = 4

    def body(a_ref, after_ref, send_ref, recv_ref, thru, token):
        x, y, c, me = _my_position()
        rows, _ = _own_and_peer_rows(a_ref, me, me, 0)
        for k, peer in enumerate(_first_relations(x, y, c)):
            pltpu.make_async_remote_copy(rows, rows, send_ref.at[k], recv_ref.at[k], device_id=peer, device_id_type=MESH).start()
        token[...] = jnp.zeros_like(token)

    sem = pltpu.SemaphoreType.DMA((n_rel,))
    out = _call(
        body, name="gather_first_start",
        out_shape=(sem, sem, pltpu.HBM(arr.shape, arr.dtype), jax.ShapeDtypeStruct((8, LANES), F32)),
        in_specs=[HBM_SPEC, ANY_SPEC],
        out_specs=(SEM_SPEC, SEM_SPEC, HBM_SPEC, pl.BlockSpec(memory_space=pltpu.VMEM)),
        input_output_aliases={0: 2},
        compiler_params=_cparams(has_side_effects=DATAFLOW),
    )(_hbm(arr), after)
    return out[:3], out[3]


def _gather_first_forward(started, after):
    send_a, recv_a, arr = started

    def body(a_ref, send_a_ref, recv_a_ref, after_ref, send_b_ref, recv_b_ref, thru, token):
        x, y, c, me = _my_position()
        sibling = (x, y, 1 - c)
        for k, peer in enumerate(_first_relations(x, y, c)):
            own, theirs = _own_and_peer_rows(a_ref, me, 4 * peer[0] + 2 * peer[1] + peer[2], 0)
            cp = pltpu.make_async_remote_copy(own, theirs, send_a_ref.at[k], recv_a_ref.at[k], device_id=peer, device_id_type=MESH)
            cp.wait_send()
            cp.wait_recv()
            if k > 0:
                pltpu.make_async_remote_copy(theirs, theirs, send_b_ref.at[k - 1], recv_b_ref.at[k - 1],
                                             device_id=sibling, device_id_type=MESH).start()
        token[...] = jnp.zeros_like(token)

    sem = pltpu.SemaphoreType.DMA((3,))
    out = _call(
        body, name="gather_first_forward",
        out_shape=(sem, sem, pltpu.HBM(arr.shape, arr.dtype), jax.ShapeDtypeStruct((8, LANES), F32)),
        in_specs=[HBM_SPEC, SEM_SPEC, SEM_SPEC, ANY_SPEC],
        out_specs=(SEM_SPEC, SEM_SPEC, HBM_SPEC, pl.BlockSpec(memory_space=pltpu.VMEM)),
        input_output_aliases={0: 2},
        compiler_params=_cparams(has_side_effects=DATAFLOW),
    )(arr, send_a, recv_a, after)
    return out[:3], out[3]


def _gather_first_wait(forwarded, after):
    send_b, recv_b, arr = forwarded

    def body(a_ref, send_b_ref, recv_b_ref, after_ref, thru):
        x, y, c, me = _my_position()
        sibling = (x, y, 1 - c)
        for k, peer in enumerate(_first_relations(x, y, c)[1:]):
            _, sent = _own_and_peer_rows(a_ref, me, 4 * peer[0] + 2 * peer[1] + peer[2], 0)
            _, got = _own_and_peer_rows(a_ref, me, 4 * peer[0] + 2 * peer[1] + (1 - peer[2]), 0)
            cp = pltpu.make_async_remote_copy(sent, got, send_b_ref.at[k], recv_b_ref.at[k], device_id=sibling, device_id_type=MESH)
            cp.wait_send()
            cp.wait_recv()

    return _call(
        body, name="gather_first_wait",
        out_shape=pltpu.HBM(arr.shape, arr.dtype),
        in_specs=[HBM_SPEC, SEM_SPEC, SEM_SPEC, ANY_SPEC],
        out_specs=HBM_SPEC,
        input_output_aliases={0: 0},
        compiler_params=_cparams(has_side_effects=DATAFLOW),
    )(arr, send_b, recv_b, after)


def _scatter_start(fulls, axes, name, after):
    n = len(fulls)
    lands = []
    for f, ax in zip(fulls, axes):
        shp = list(f.shape)
        shp[ax] //= N_DEV
        lands.append(_hbm(lax.empty((N_DEV - 1,) + tuple(shp), f.dtype)))

    def body(*refs):
        srcs, dsts, send_ref, recv_ref, token = refs[:n], refs[n:2 * n], refs[2 * n + 1], refs[2 * n + 2], refs[-1]
        x, y, c, me = _my_position()
        for k, (peer, idx) in enumerate(_peers(x, y, c)):
            for a in range(n):
                _, theirs = _own_and_peer_rows(srcs[a], me, idx, axes[a])
                pltpu.make_async_remote_copy(theirs, dsts[a].at[k], send_ref.at[k * n + a], recv_ref.at[k * n + a],
                                             device_id=peer, device_id_type=MESH).start()
        token[...] = jnp.zeros_like(token)

    sem = pltpu.SemaphoreType.DMA(((N_DEV - 1) * n,))
    out = _call(
        body, name=name,
        out_shape=(sem, sem, *[pltpu.HBM(a.shape, a.dtype) for a in fulls], *[pltpu.HBM(a.shape, a.dtype) for a in lands],
                   jax.ShapeDtypeStruct((8, LANES), F32)),
        in_specs=[HBM_SPEC] * (2 * n) + [ANY_SPEC],
        out_specs=(SEM_SPEC, SEM_SPEC, *[HBM_SPEC] * (2 * n), pl.BlockSpec(memory_space=pltpu.VMEM)),
        input_output_aliases={a: 2 + a for a in range(2 * n)},
        compiler_params=_cparams(has_side_effects=DATAFLOW),
    )(*[_hbm(a) for a in fulls], *lands, after)
    return (out[0], out[1], list(out[2:2 + n]), list(out[2 + n:2 + 2 * n])), out[-1]


def _scatter_wait(started, axes, after, name):
    send_sems, recv_sems, fulls, lands = started
    n = len(fulls)

    def body(*refs):
        srcs, dsts, send_ref, recv_ref = refs[:n], refs[n:2 * n], refs[2 * n], refs[2 * n + 1]
        x, y, c, me = _my_position()
        for k, (peer, idx) in enumerate(_peers(x, y, c)):
            for a in range(n):
                _, theirs = _own_and_peer_rows(srcs[a], me, idx, axes[a])
                cp = pltpu.make_async_remote_copy(theirs, dsts[a].at[k], send_ref.at[k * n + a], recv_ref.at[k * n + a],
                                                  device_id=peer, device_id_type=MESH)
                cp.wait_send()
                cp.wait_recv()

    out = _call(
        body, name=name,
        out_shape=tuple(pltpu.HBM(a.shape, a.dtype) for a in (*fulls, *lands)),
        in_specs=[HBM_SPEC] * (2 * n) + [SEM_SPEC, SEM_SPEC, ANY_SPEC],
        out_specs=tuple([HBM_SPEC] * (2 * n)),
        input_output_aliases={a: a for a in range(2 * n)},
        compiler_params=_cparams(has_side_effects=DATAFLOW),
    )(*fulls, *lands, send_sems, recv_sems, after)
    return list(out[:n]), list(out[n:])


def _prep_weights(me, attn_w_in, attn_w_out, pool_w_in, pool_w_out, pool_w_group):
    nl = attn_w_in.shape[0]

    def body(me_ref, *refs):
        ins, outs = refs[:5 * nl], refs[5 * nl:]
        for j in range(nl):
            awi, awo, pwi, pwo, pwg = ins[5 * j:5 * j + 5]
            o_awi, o_awo, o_pwi, o_pwo, o_pwg = outs[5 * j:5 * j + 5]
            o_awi[...] = awi[...].T.astype(BF16)
            o_awo[...] = awo[...].astype(BF16)
            o_pwi[...] = pwi[...].T.astype(BF16)
            o_pwo[...] = pwo[...].astype(BF16)
            o_pwg[...] = pwg[...].astype(BF16)

    def in_spec(shape, j):
        nd = len(shape)
        return pl.BlockSpec((None,) + tuple(shape), lambda i, me_ref: (j,) + (0,) * nd)

    srcs = (attn_w_in, attn_w_out, pool_w_in, pool_w_out, pool_w_group)
    rows_spec = lambda r: pl.BlockSpec((r, D), lambda i, me_ref: (me_ref[0], 0))
    grp = pool_w_group.shape[1:]
    grp_spec = pl.BlockSpec(grp, lambda i, me_ref: (0, me_ref[0], 0))
    ins, in_specs, out_shapes, out_specs = [], [], [], []
    for j in range(nl):
        ins += list(srcs)
        in_specs += [in_spec(a.shape[1:], j) for a in srcs]
        out_shapes += [(N_DEV * attn_w_in.shape[2], D), (N_DEV * attn_w_out.shape[1], D), (N_DEV * pool_w_in.shape[2], D),
                       (N_DEV * pool_w_out.shape[1], D), (grp[0], N_DEV * grp[1], grp[2])]
        out_specs += [rows_spec(attn_w_in.shape[2]), rows_spec(attn_w_out.shape[1]), rows_spec(pool_w_in.shape[2]),
                      rows_spec(pool_w_out.shape[1]), grp_spec]
    out = _call(
        body, name="prep_weights",
        grid_spec=pltpu.PrefetchScalarGridSpec(num_scalar_prefetch=1, grid=(1,), in_specs=in_specs, out_specs=tuple(out_specs)),
        out_shape=tuple(jax.ShapeDtypeStruct(s, BF16) for s in out_shapes),
        compiler_params=_cparams(1),
    )(me.reshape(1), *ins)
    return [list(out[5 * j:5 * j + 5]) for j in range(nl)]


def _ada_forward(c_all, ada_w):
    cols = ada_w.shape[2]

    def body(c_ref, w_ref, o_ref):
        cv = c_ref[...]
        sc = (cv * _sigmoid(cv)).astype(BF16)
        o_ref[...] = _dot(sc, w_ref[...].astype(BF16))

    return _call(
        body, name="ada_forward", grid=(DEPTH,),
        out_shape=jax.ShapeDtypeStruct((DEPTH, N_DEV, cols), F32),
        in_specs=[pl.BlockSpec((N_DEV, D), lambda i: (0, 0)), pl.BlockSpec((None, D, cols), lambda i: (i, 0, 0))],
        out_specs=pl.BlockSpec((None, N_DEV, cols), lambda i: (i, 0, 0)),
        compiler_params=_cparams(1),
    )(c_all, ada_w)


def _ada_backward_adamw(c_pad, dmod_pad, w, m, v):
    cols = w.shape[2]

    def body(c_ref, dm_ref, w_ref, m_ref, v_ref, g_out, d_out, m_out, v_out):
        cv = c_ref[...]
        sc = (cv * _sigmoid(cv)).astype(BF16)
        g = _dot_tn(sc, dm_ref[...].astype(BF16))
        g_out[...] = g
        d_out[...], m_out[...], v_out[...] = _adamw(w_ref[...], g, m_ref[...], v_ref[...])

    wspec = pl.BlockSpec((None, D, cols), lambda i: (i, 0, 0))
    return _call(
        body, name="ada_backward_adamw", grid=(DEPTH,),
        out_shape=tuple(jax.ShapeDtypeStruct(w.shape, F32) for _ in range(4)),
        in_specs=[pl.BlockSpec((2 * N_DEV, D), lambda i: (0, 0)), pl.BlockSpec((None, 2 * N_DEV, cols), lambda i: (i, 0, 0)),
                  wspec, wspec, wspec],
        out_specs=(wspec, wspec, wspec, wspec),
        compiler_params=_cparams(1),
    )(c_pad, dmod_pad, w, m, v)


def _attn_in_proj(x, rope, ng, mod, w_t, j, gain, bd, tile):
    seq = x.shape[0]

    def body(x_ref, rope_ref, ng_ref, mod_ref, w_ref, gain_ref, bd_ref, qk_ref, qs_ref, kd_ref, vd_ref, g_ref):
        _, _, h = _norm_mod(x_ref[...], ng_ref[...], mod_ref[1:2, :], mod_ref[0:1, :])
        hb = h.astype(BF16)
        tabs = _rope_tabs(rope_ref)
        low = _low_half(tile)
        bdm = bd_ref[...]

        def put_kv(ref, blk, first_kv):
            sw = pltpu.roll(blk, HEAD_DIM, 1)
            ref[:, LANES * first_kv:LANES * (first_kv + 1)] = jnp.where(low, blk, sw).astype(BF16)
            ref[:, LANES * (first_kv + 1):LANES * (first_kv + 2)] = jnp.where(low, sw, blk).astype(BF16)

        def project(c):
            return _dot_nt(hb, w_ref[CHUNK * c:CHUNK * (c + 1), :])

        n_chunks = ATTN_IN // CHUNK
        per = CHUNK // LANES
        nxt = project(0)
        for c in range(n_chunks):
            cur = nxt
            if c + 1 < n_chunks:
                nxt = project(c + 1)
            col = CHUNK * c
            if col >= QK_W + N_KV * HEAD_DIM:
                g_ref[:, col - QK_W - N_KV * HEAD_DIM:col - QK_W - N_KV * HEAD_DIM + CHUNK] = cur
            elif col >= QK_W:
                for t in range(per):
                    put_kv(vd_ref, cur[:, LANES * t:LANES * (t + 1)], (col - QK_W) // HEAD_DIM + 2 * t)
            else:
                qk_ref[:, col:col + CHUNK] = cur
                for t in range(per):
                    b = per * c + t
                    blk = cur[:, LANES * t:LANES * (t + 1)]
                    ms = _dot_split(blk * blk, bdm) * (1.0 / HEAD_DIM)
                    y = (blk * lax.rsqrt(ms + NORM_EPS)) * gain_ref[:, LANES * b:LANES * (b + 1)]
                    rp = _rope(y, tabs)
                    if b < D // LANES:
                        rp = rp * (HEAD_DIM ** -0.5)
                        qs_ref[:, 2 * LANES * b:2 * LANES * b + LANES] = jnp.where(low, rp, 0.0).astype(BF16)
                        qs_ref[:, 2 * LANES * b + LANES:2 * LANES * (b + 1)] = jnp.where(low, 0.0, rp).astype(BF16)
                    else:
                        put_kv(kd_ref, rp, 2 * (b - D // LANES))

    row = lambda w: pl.BlockSpec((tile, w), lambda i: (i, 0))
    return _call(
        body, name=f"attn_in_proj_{j}", grid=(seq // tile,),
        out_shape=(jax.ShapeDtypeStruct((seq, QK_W), F32), jax.ShapeDtypeStruct((seq, N_HEADS * LANES), BF16),
                   jax.ShapeDtypeStruct((seq, KX_W), BF16), jax.ShapeDtypeStruct((seq, KX_W), BF16),
                   jax.ShapeDtypeStruct((seq, D), F32)),
        in_specs=[row(D), row(3 * LANES), _const_spec((1, D)), _const_spec((8, D)), _const_spec((ATTN_IN, D)),
                  _const_spec((1, QK_W)), _const_spec((LANES, LANES))],
        out_specs=(row(QK_W), row(N_HEADS * LANES), row(KX_W), row(KX_W), row(D)),
        compiler_params=_cparams(1),
    )(x, rope, ng, mod, w_t, gain, bd)


def _band_mask(n, rows, keys_on_rows):
    shape = (2 * QBLK, rows) if keys_on_rows else (rows, 2 * QBLK)
    qi = lax.broadcasted_iota(jnp.int32, shape, 1 if keys_on_rows else 0) & (QBLK - 1)
    kj = lax.broadcasted_iota(jnp.int32, shape, 0 if keys_on_rows else 1)
    diff = QBLK + qi - kj
    first_key = jnp.where(n > 0, 0, QBLK)
    return (diff >= 0) & (diff < QBLK) & (kj >= first_key)


def _stack_heads(ref, heads):
    return jnp.concatenate([ref[:, LANES * h:LANES * (h + 1)] for h in heads], axis=0)


def _kv_block(prev_ref, cur_ref, kv):
    cols = slice(LANES * kv, LANES * (kv + 1))
    return jnp.concatenate([prev_ref[:, cols], cur_ref[:, cols]], axis=0)


def _pair_up(st, low):
    return jnp.concatenate([jnp.where(low, st[0:QBLK], st[QBLK:2 * QBLK]),
                            jnp.where(low, st[2 * QBLK:3 * QBLK], st[3 * QBLK:4 * QBLK])], axis=1)


def _attn_forward(sinks, qs, kd, vd, j):
    seq = qs.shape[0]
    nb = seq // QBLK

    def body(sink_ref, q_ref, kp_ref, kc_ref, vp_ref, vc_ref, o_ref):
        n = pl.program_id(0)
        ok = _band_mask(n, 4 * QBLK, False)
        low = _low_half(QBLK)
        rowi = lax.broadcasted_iota(jnp.int32, (4 * QBLK, 1), 0)

        def scores(kv):
            return _dot_nt(_stack_heads(q_ref, range(4 * kv, 4 * kv + 4)), _kv_block(kp_ref, kc_ref, kv))

        nxt = scores(0)
        for kv in range(N_KV):
            s = jnp.where(ok, nxt, -1e30)
            if kv + 1 < N_KV:
                nxt = scores(kv + 1)
            sink = jnp.where(rowi < QBLK, sink_ref[4 * kv],
                             jnp.where(rowi < 2 * QBLK, sink_ref[4 * kv + 1],
                                       jnp.where(rowi < 3 * QBLK, sink_ref[4 * kv + 2], sink_ref[4 * kv + 3])))
            m = jnp.maximum(jnp.max(s, axis=1, keepdims=True), sink)
            p = jnp.exp(s - m)
            den = jnp.sum(p, axis=1, keepdims=True) + jnp.exp(sink - m)
            o_st = _dot((p / den).astype(BF16), _kv_block(vp_ref, vc_ref, kv))
            o_ref[:, 2 * LANES * kv:2 * LANES * (kv + 1)] = _pair_up(o_st, low)

    blk = lambda w: pl.BlockSpec((QBLK, w), lambda n: (n, 0))
    prev = lambda w: pl.BlockSpec((QBLK, w), lambda n: (jnp.maximum(n - 1, 0), 0))
    return _call(
        body, name=f"attn_forward_{j}", grid=(nb,),
        out_shape=jax.ShapeDtypeStruct((seq, D), F32),
        in_specs=[pl.BlockSpec(memory_space=pltpu.SMEM), blk(N_HEADS * LANES), prev(KX_W), blk(KX_W), prev(KX_W), blk(KX_W)],
        out_specs=blk(D),
        compiler_params=_cparams(1),
    )(sinks, qs, kd, kd, vd, vd)


def _attn_out_proj(x, o, g, w, j, mod, tile):
    seq = x.shape[0]

    def body(x_ref, o_ref, g_ref, w_ref, mod_ref, xo_ref, br_ref):
        gv = g_ref[...]
        u = (o_ref[...] * (gv * _sigmoid(gv))).astype(BF16)
        br = _dot(u, w_ref[...])
        br_ref[...] = br.astype(BF16)
        xo_ref[...] = x_ref[...] + mod_ref[2:3, :] * br

    row = pl.BlockSpec((tile, D), lambda i: (i, 0))
    return _call(
        body, name=f"attn_out_proj_{j}", grid=(seq // tile,),
        out_shape=(jax.ShapeDtypeStruct((seq, D), F32), jax.ShapeDtypeStruct((seq, D), BF16)),
        in_specs=[row, row, row, _const_spec((D, D)), _const_spec((8, D))],
        out_specs=(row, row),
        compiler_params=_cparams(1),
    )(x, o, g, w, mod)


def _attn_out_proj_bwd(dxn, br, o, g, w, j, mod, tile):
    seq = dxn.shape[0]
    steps = seq // tile

    def body(dxn_ref, br_ref, o_ref, g_ref, w_ref, mod_ref, do_ref, dg_ref, dw_ref, dgate_ref, dw_acc):
        i = pl.program_id(0)

        @pl.when(i == 0)
        def _():
            dw_acc[...] = jnp.zeros_like(dw_acc)
            dgate_ref[...] = jnp.zeros_like(dgate_ref)

        dxn_v, ov, gv = dxn_ref[...], o_ref[...], g_ref[...]
        dgate_ref[...] += jnp.sum(dxn_v * br_ref[...].astype(F32), axis=0, keepdims=True)
        dbr = (dxn_v * mod_ref[2:3, :]).astype(BF16)
        du = _dot_nt(dbr, w_ref[...])
        sg = _sigmoid(gv)
        sl = gv * sg
        dw_acc[...] += _dot_tn((ov * sl).astype(BF16), dbr)
        do = du * sl
        dg_ref[...] = (du * ov * (sg * (1.0 + gv * (1.0 - sg)))).astype(BF16)
        low = _low_half(tile)
        for b in range(D // LANES):
            blk = do[:, LANES * b:LANES * (b + 1)]
            do_ref[:, 2 * LANES * b:2 * LANES * b + LANES] = jnp.where(low, blk, 0.0).astype(BF16)
            do_ref[:, 2 * LANES * b + LANES:2 * LANES * (b + 1)] = jnp.where(low, 0.0, blk).astype(BF16)

        @pl.when(i == steps - 1)
        def _():
            dw_ref[...] = dw_acc[...].astype(BF16)

    row = lambda w_: pl.BlockSpec((tile, w_), lambda i: (i, 0))
    return _call(
        body, name=f"attn_out_proj_bwd_{j}", grid=(steps,),
        out_shape=(jax.ShapeDtypeStruct((seq, N_HEADS * LANES), BF16), jax.ShapeDtypeStruct((seq, D), BF16),
                   jax.ShapeDtypeStruct((D, D), BF16), jax.ShapeDtypeStruct((1, D), F32)),
        in_specs=[row(D), row(D), row(D), row(D), _const_spec((D, D)), _const_spec((8, D))],
        out_specs=(row(N_HEADS * LANES), row(D), pl.BlockSpec((D, D), lambda i: (0, 0)),
                   pl.BlockSpec((1, D), lambda i: (0, 0))),
        scratch_shapes=[pltpu.VMEM((D, D), F32)],
        compiler_params=_cparams(1),
    )(dxn, br, o, g, w, mod)


def _attn_backward(sinks, qs, dos, kd, vd, j):
    seq = qs.shape[0]
    nb = seq // QBLK

    def body(sink_ref, q_ref, do_ref, kp_ref, kc_ref, vp_ref, vc_ref, dq_ref, dk_ref, dv_ref, dsink_ref,
             carry_k, carry_v, sink_acc):
        n = pl.program_id(0)

        @pl.when(n == 0)
        def _():
            carry_k[...] = jnp.zeros_like(carry_k)
            carry_v[...] = jnp.zeros_like(carry_v)
            sink_acc[...] = jnp.zeros_like(sink_acc)

        @pl.when(n < nb)
        def _():
            ok = _band_mask(n, 2 * QBLK, True)
            low = _low_half(QBLK)
            lane_q = lax.broadcasted_iota(jnp.int32, (1, 2 * QBLK), 1)
            dk_parts, dv_parts = [], []

            def first_products(g):
                kv, half = divmod(g, 2)
                heads = (4 * kv + half, 4 * kv + 2 + half)
                q = _stack_heads(q_ref, heads)
                do = _stack_heads(do_ref, heads)
                kk = _kv_block(kp_ref, kc_ref, kv)
                return heads, q, do, kk, _dot_nt(kk, q), _dot_nt(_kv_block(vp_ref, vc_ref, kv), do)

            nxt = first_products(0)
            dq_h, dk_kv, dv_kv = [], None, None
            for g in range(2 * N_KV):
                heads, q, do, kk, s_raw, dp_raw = nxt
                if g + 1 < 2 * N_KV:
                    nxt = first_products(g + 1)
                st = jnp.where(ok, s_raw, -1e30)
                sink = jnp.where(lane_q < QBLK, sink_ref[heads[0]], sink_ref[heads[1]])
                m = jnp.maximum(jnp.max(st, axis=0, keepdims=True), sink)
                e = jnp.exp(st - m)
                e_sink = jnp.exp(sink - m)
                inv = 1.0 / (jnp.sum(e, axis=0, keepdims=True) + e_sink)
                p = e * inv
                pdp = p * dp_raw
                delta = jnp.sum(pdp, axis=0, keepdims=True)
                ds = (pdp - p * delta).astype(BF16)
                sink_acc[g:g + 1, :] -= e_sink * inv * delta
                dk_g, dv_g = _dot(ds, q), _dot(p.astype(BF16), do)
                dk_kv = dk_g if dk_kv is None else dk_kv + dk_g
                dv_kv = dv_g if dv_kv is None else dv_kv + dv_g
                dq_h.append(_dot_tn(ds, kk))
                if g % 2 == 1:
                    kv = g // 2
                    for t in range(2):
                        dq_ref[:, LANES * (2 * kv + t):LANES * (2 * kv + t + 1)] = jnp.where(
                            low, dq_h[0][QBLK * t:QBLK * (t + 1)], dq_h[1][QBLK * t:QBLK * (t + 1)])
                    dk_parts.append(dk_kv + pltpu.roll(dk_kv, HEAD_DIM, 1))
                    dv_parts.append(dv_kv + pltpu.roll(dv_kv, HEAD_DIM, 1))
                    dq_h, dk_kv, dv_kv = [], None, None

            def order(parts, lo, hi):
                return jnp.concatenate([jnp.where(low, parts[0][lo:hi], parts[1][lo:hi]),
                                        jnp.where(low, parts[2][lo:hi], parts[3][lo:hi])], axis=1)

            dk_ref[...] = carry_k[...] + order(dk_parts, 0, QBLK)
            dv_ref[...] = (carry_v[...] + order(dv_parts, 0, QBLK)).astype(BF16)
            carry_k[...] = order(dk_parts, QBLK, 2 * QBLK)
            carry_v[...] = order(dv_parts, QBLK, 2 * QBLK)

        @pl.when(n == nb)
        def _():
            dk_ref[...] = carry_k[...]
            dv_ref[...] = carry_v[...].astype(BF16)
            lane = lax.broadcasted_iota(jnp.int32, (1, LANES), 1)
            out = jnp.zeros((1, LANES), F32)
            for g in range(2 * N_KV):
                for t in range(2):
                    tot = jnp.sum(sink_acc[g:g + 1, QBLK * t:QBLK * (t + 1)], axis=1, keepdims=True)
                    out = jnp.where(lane == 4 * (g // 2) + 2 * t + g % 2, tot, out)
            dsink_ref[...] = out

    cur = lambda w: pl.BlockSpec((QBLK, w), lambda n: (jnp.minimum(n, nb - 1), 0))
    prev = lambda w: pl.BlockSpec((QBLK, w), lambda n: (jnp.maximum(n - 1, 0), 0))
    return _call(
        body, name=f"attn_backward_{j}", grid=(nb + 1,),
        out_shape=(jax.ShapeDtypeStruct((seq, D), F32), jax.ShapeDtypeStruct((seq, N_KV * HEAD_DIM), F32),
                   jax.ShapeDtypeStruct((seq, N_KV * HEAD_DIM), BF16), jax.ShapeDtypeStruct((1, LANES), F32)),
        in_specs=[pl.BlockSpec(memory_space=pltpu.SMEM), cur(N_HEADS * LANES), cur(N_HEADS * LANES), prev(KX_W), cur(KX_W),
                  prev(KX_W), cur(KX_W)],
        out_specs=(cur(D), prev(N_KV * HEAD_DIM), prev(N_KV * HEAD_DIM), pl.BlockSpec((1, LANES), lambda n: (0, 0))),
        scratch_shapes=[pltpu.VMEM((QBLK, N_KV * HEAD_DIM), F32), pltpu.VMEM((QBLK, N_KV * HEAD_DIM), F32),
                        pltpu.VMEM((2 * N_KV, 2 * QBLK), F32)],
        compiler_params=_cparams(1),
    )(sinks, qs, dos, kd, kd, vd, vd)


def _in_proj_tail(x_ref, dxn_ref, ng_ref, mod_ref, w_ref, dproj, dx_ref, dw_acc, vec_acc):
    ng, sc, sh = ng_ref[...], mod_ref[1:2, :], mod_ref[0:1, :]
    xh, r, h = _norm_mod(x_ref[...], ng, sc, sh)
    dh = _dot(dproj, w_ref[...])
    dw_acc[...] += _dot_tn(dproj, h.astype(BF16))
    vec_acc[0:1, :] += jnp.sum(dh, axis=0, keepdims=True)
    vec_acc[1:2, :] += jnp.sum(dh * xh, axis=0, keepdims=True)
    dxh = dh * (ng * (1.0 + sc))
    dx_ref[...] = dxn_ref[...] + r * (dxh - xh * jnp.mean(dxh * xh, axis=-1, keepdims=True))


def _tail_finish(ng_ref, mod_ref, dw_ref, vec_ref, dw_acc, vec_acc):
    dw_ref[...] = dw_acc[...].astype(BF16)
    a = vec_acc[1:2, :]
    vec_ref[...] = jnp.zeros_like(vec_ref)
    vec_ref[0:1, :] = vec_acc[0:1, :]
    vec_ref[1:2, :] = a * ng_ref[...]
    vec_ref[3:4, :] = a * (1.0 + mod_ref[1:2, :])


def _attn_in_proj_bwd(x, dxn, rope, qk_raw, dq, dk, dv, dg, ng, mod, w_t, j, gain, bd, tile):
    seq = x.shape[0]
    steps = seq // tile

    def body(x_ref, dxn_ref, rope_ref, qk_ref, dq_ref, dk_ref, dv_ref, dg_ref, ng_ref, mod_ref, w_ref, gain_ref,
             bd_ref, dx_ref, dw_ref, vec_ref, dgain_ref, dproj, dw_acc, vec_acc):
        i = pl.program_id(0)

        @pl.when(i == 0)
        def _():
            dw_acc[...] = jnp.zeros_like(dw_acc)
            vec_acc[...] = jnp.zeros_like(vec_acc)
            dgain_ref[...] = jnp.zeros_like(dgain_ref)

        tabs = _rope_tabs(rope_ref)
        bdm = bd_ref[...]
        for b in range(QK_W // LANES):
            cols = slice(LANES * b, LANES * (b + 1))
            raw = qk_ref[:, cols]
            if b < D // LANES:
                dy = dq_ref[:, cols] * (HEAD_DIM ** -0.5)
            else:
                dy = dk_ref[:, LANES * (b - D // LANES):LANES * (b + 1 - D // LANES)]
            dy = _rope_bwd(dy, tabs)
            rr = lax.rsqrt(_dot_split(raw * raw, bdm) * (1.0 / HEAD_DIM) + NORM_EPS)
            xh = raw * rr
            dgain_ref[:, cols] += jnp.sum(dy * xh, axis=0, keepdims=True)
            dxh = dy * gain_ref[:, cols]
            dproj[:, cols] = (rr * (dxh - xh * (_dot_split(dxh * xh, bdm) * (1.0 / HEAD_DIM)))).astype(BF16)
        dproj[:, QK_W:QK_W + N_KV * HEAD_DIM] = dv_ref[...]
        dproj[:, QK_W + N_KV * HEAD_DIM:] = dg_ref[...]
        _in_proj_tail(x_ref, dxn_ref, ng_ref, mod_ref, w_ref, dproj[...], dx_ref, dw_acc, vec_acc)

        @pl.when(i == steps - 1)
        def _():
            _tail_finish(ng_ref, mod_ref, dw_ref, vec_ref, dw_acc, vec_acc)

    row = lambda w, dt=None: pl.BlockSpec((tile, w), lambda i: (i, 0))
    fixed = lambda shape: pl.BlockSpec(shape, lambda i: (0,) * len(shape))
    return _call(
        body, name=f"attn_in_proj_bwd_{j}", grid=(steps,),
        out_shape=(jax.ShapeDtypeStruct((seq, D), F32), jax.ShapeDtypeStruct((ATTN_IN, D), BF16),
                   jax.ShapeDtypeStruct((8, D), F32), jax.ShapeDtypeStruct((1, QK_W), F32)),
        in_specs=[row(D), row(D), row(3 * LANES), row(QK_W), row(D), row(N_KV * HEAD_DIM), row(N_KV * HEAD_DIM), row(D),
                  _const_spec((1, D)), _const_spec((8, D)), _const_spec((ATTN_IN, D)), _const_spec((1, QK_W)),
                  _const_spec((LANES, LANES))],
        out_specs=(row(D), fixed((ATTN_IN, D)), fixed((8, D)), fixed((1, QK_W))),
        scratch_shapes=[pltpu.VMEM((tile, ATTN_IN), BF16), pltpu.VMEM((ATTN_IN, D), F32), pltpu.VMEM((8, D), F32)],
        compiler_params=_cparams(1),
    )(x, dxn, rope, qk_raw, dq, dk, dv, dg, ng, mod, w_t, gain, bd)


def _pool_in_proj(x, ng, mod, w_t, j, tile):
    seq = x.shape[0]

    def body(x_ref, ng_ref, mod_ref, w_ref, v_ref, g_ref):
        _, _, h = _norm_mod(x_ref[...], ng_ref[...], mod_ref[1:2, :], mod_ref[0:1, :])
        proj = _dot_nt(h.astype(BF16), w_ref[...])
        v_ref[...] = proj[:, :D]
        g_ref[...] = proj[:, D:]

    row = pl.BlockSpec((tile, D), lambda i: (i, 0))
    return _call(
        body, name=f"pool_in_proj_{j}", grid=(seq // tile,),
        out_shape=(jax.ShapeDtypeStruct((seq, D), F32), jax.ShapeDtypeStruct((seq, D), F32)),
        in_specs=[row, _const_spec((1, D)), _const_spec((8, D)), _const_spec((POOL_IN, D))],
        out_specs=(row, row),
        compiler_params=_cparams(1),
    )(x, ng, mod, w_t)


def _pooled(ext, first, tile):
    t_abs = first + lax.broadcasted_iota(jnp.int32, (tile, 1), 0)
    outs = []
    gw = D // len(POOL_WINDOWS)
    for gi, w in enumerate(POOL_WINDOWS):
        cols = slice(gw * gi, gw * (gi + 1))
        own = ext[HALO:HALO + tile, cols]
        acc = own
        for k in range(1, w):
            acc = acc + ext[HALO - k:HALO - k + tile, cols]
        cnt = jnp.minimum(t_abs + 1, w).astype(F32)
        outs.append(acc / cnt - own)
    return jnp.concatenate(outs, axis=1)


def _fill_ext(ext, halo_ref, v_ref, i, tile):
    ext[0:HALO, :] = jnp.where(i == 0, 0.0, halo_ref[...])
    ext[HALO:HALO + tile, :] = v_ref[...]


def _group_mix(pb, wg_ref):
    gw = D // len(POOL_WINDOWS)
    return jnp.concatenate([_dot(pb[:, gw * gi:gw * (gi + 1)], wg_ref[gi]) for gi in range(len(POOL_WINDOWS))], axis=1)


def _pool_mix_out(x, v, g, wg, w_out, j, scale, mod, tile, target=None):
    seq = x.shape[0]

    def body(*refs):
        if target is None:
            x_ref, v_ref, halo_ref, g_ref, wg_ref, w_ref, scale_ref, mod_ref, xo_ref, br_ref, ext = refs
        else:
            x_ref, v_ref, halo_ref, g_ref, wg_ref, w_ref, scale_ref, mod_ref, t_ref, xo_ref, br_ref, loss_ref, ext = refs
        i = pl.program_id(0)
        _fill_ext(ext, halo_ref, v_ref, i, tile)
        pb = _pooled(ext, i * tile, tile).astype(BF16)
        ms = _group_mix(pb, wg_ref) * scale_ref[...]
        gv = g_ref[...]
        u = (ms * (gv * _sigmoid(gv))).astype(BF16)
        br = _dot(u, w_ref[...])
        br_ref[...] = br.astype(BF16)
        y = x_ref[...] + mod_ref[2:3, :] * br
        if target is None:
            xo_ref[...] = y
        else:
            @pl.when(i == 0)
            def _():
                loss_ref[...] = jnp.zeros_like(loss_ref)

            e = y - t_ref[...]
            xo_ref[...] = e * (1.0 / D)
            loss_ref[...] += 0.5 * jnp.sum(jnp.mean(e * e, axis=-1, keepdims=True), axis=0, keepdims=True)

    row = pl.BlockSpec((tile, D), lambda i: (i, 0))
    halo = pl.BlockSpec((HALO, D), lambda i: (jnp.maximum(i * (tile // HALO) - 1, 0), 0))
    extra_in, extra_out, extra_shape = ([], (), ()) if target is None else (
        [row], (pl.BlockSpec((1, LANES), lambda i: (0, 0)),), (jax.ShapeDtypeStruct((1, LANES), F32),))
    return _call(
        body, name=f"pool_mix_out_{j}", grid=(seq // tile,),
        out_shape=(jax.ShapeDtypeStruct((seq, D), F32), jax.ShapeDtypeStruct((seq, D), BF16)) + extra_shape,
        in_specs=[row, row, halo, row, _const_spec(wg.shape), _const_spec((D, D)), _const_spec((1, D)),
                  _const_spec((8, D))] + extra_in,
        out_specs=(row, row) + extra_out,
        scratch_shapes=[pltpu.VMEM((tile + HALO, D), F32)],
        compiler_params=_cparams(1),
    )(x, v, v, g, wg, w_out, scale, mod, *(() if target is None else (target,)))


def _pool_mix_out_bwd(dxn, br, v, g, wg, w_out, j, scale, mod, tile):
    seq = dxn.shape[0]
    steps = seq // tile
    ng_ = len(POOL_WINDOWS)
    gw = D // ng_

    def body(dxn_ref, br_ref, v_ref, halo_ref, g_ref, wg_ref, w_ref, scale_ref, mod_ref,
             dpool_ref, dg_ref, dw_ref, dwg_ref, vec_ref, ext, dw_acc, dwg_acc):
        i = pl.program_id(0)

        @pl.when(i == 0)
        def _():
            dw_acc[...] = jnp.zeros_like(dw_acc)
            dwg_acc[...] = jnp.zeros_like(dwg_acc)
            vec_ref[...] = jnp.zeros_like(vec_ref)

        _fill_ext(ext, halo_ref, v_ref, i, tile)
        pb = _pooled(ext, i * tile, tile).astype(BF16)
        mixed = _group_mix(pb, wg_ref)
        scale = scale_ref[...]
        ms = mixed * scale
        gv, dxn_v = g_ref[...], dxn_ref[...]
        sg = _sigmoid(gv)
        sl = gv * sg
        vec_ref[0:1, :] += jnp.sum(dxn_v * br_ref[...].astype(F32), axis=0, keepdims=True)
        dbr = (dxn_v * mod_ref[2:3, :]).astype(BF16)
        du = _dot_nt(dbr, w_ref[...])
        dw_acc[...] += _dot_tn((ms * sl).astype(BF16), dbr)
        dms = du * sl
        dg_ref[...] = (du * ms * (sg * (1.0 + gv * (1.0 - sg)))).astype(BF16)
        vec_ref[1:2, :] += jnp.sum(dms * mixed, axis=0, keepdims=True)
        dmx = (dms * scale).astype(BF16)
        for gi in range(ng_):
            cols = slice(gw * gi, gw * (gi + 1))
            dpool_ref[:, cols] = _dot_nt(dmx[:, cols], wg_ref[gi])
            dwg_acc[gi] += _dot_tn(pb[:, cols], dmx[:, cols])

        @pl.when(i == steps - 1)
        def _():
            dw_ref[...] = dw_acc[...].astype(BF16)
            dwg_ref[...] = dwg_acc[...].astype(BF16)

    row = pl.BlockSpec((tile, D), lambda i: (i, 0))
    halo = pl.BlockSpec((HALO, D), lambda i: (jnp.maximum(i * (tile // HALO) - 1, 0), 0))
    fixed = lambda shape: pl.BlockSpec(shape, lambda i: (0,) * len(shape))
    return _call(
        body, name=f"pool_mix_out_bwd_{j}", grid=(steps,),
        out_shape=(jax.ShapeDtypeStruct((seq, D), F32), jax.ShapeDtypeStruct((seq, D), BF16),
                   jax.ShapeDtypeStruct((D, D), BF16), jax.ShapeDtypeStruct((ng_, gw, gw), BF16),
                   jax.ShapeDtypeStruct((8, D), F32)),
        in_specs=[row, row, row, halo, row, _const_spec(wg.shape), _const_spec((D, D)), _const_spec((1, D)),
                  _const_spec((8, D))],
        out_specs=(row, row, fixed((D, D)), fixed((ng_, gw, gw)), fixed((8, D))),
        scratch_shapes=[pltpu.VMEM((tile + HALO, D), F32), pltpu.VMEM((D, D), F32), pltpu.VMEM((ng_, gw, gw), F32)],
        compiler_params=_cparams(1),
    )(dxn, br, v, v, g, wg, w_out, scale, mod)


def _pool_in_proj_bwd(x, dxn, dpool, dg, ng, mod, w_t, j, tile):
    seq = x.shape[0]
    steps = seq // tile
    gw = D // len(POOL_WINDOWS)

    def body(x_ref, dxn_ref, dp_ref, halo_ref, dg_ref, ng_ref, mod_ref, w_ref, dx_ref, dw_ref, vec_ref,
             ext, dproj, dw_acc, vec_acc):
        i = pl.program_id(0)

        @pl.when(i == 0)
        def _():
            dw_acc[...] = jnp.zeros_like(dw_acc)
            vec_acc[...] = jnp.zeros_like(vec_acc)

        t_abs = i * tile + lax.broadcasted_iota(jnp.int32, (tile, 1), 0)
        last = i == steps - 1
        for gi, w in enumerate(POOL_WINDOWS):
            cols = slice(gw * gi, gw * (gi + 1))
            cnt = jnp.minimum(t_abs + 1, w).astype(F32)
            ext[0:tile, cols] = dp_ref[:, cols] / cnt
            ext[tile:tile + HALO, cols] = jnp.where(last, 0.0, halo_ref[:, cols] * (1.0 / w))
            acc = ext[0:tile, cols]
            for k in range(1, w):
                acc = acc + ext[k:k + tile, cols]
            dproj[:, cols] = (acc - dp_ref[:, cols]).astype(BF16)
        dproj[:, D:] = dg_ref[...]
        _in_proj_tail(x_ref, dxn_ref, ng_ref, mod_ref, w_ref, dproj[...], dx_ref, dw_acc, vec_acc)

        @pl.when(last)
        def _():
            _tail_finish(ng_ref, mod_ref, dw_ref, vec_ref, dw_acc, vec_acc)

    row = pl.BlockSpec((tile, D), lambda i: (i, 0))
    halo = pl.BlockSpec((HALO, D), lambda i: (jnp.minimum((i + 1) * (tile // HALO), seq // HALO - 1), 0))
    fixed = lambda shape: pl.BlockSpec(shape, lambda i: (0,) * len(shape))
    return _call(
        body, name=f"pool_in_proj_bwd_{j}", grid=(steps,),
        out_shape=(jax.ShapeDtypeStruct((seq, D), F32), jax.ShapeDtypeStruct((POOL_IN, D), BF16),
                   jax.ShapeDtypeStruct((8, D), F32)),
        in_specs=[row, row, row, halo, row, _const_spec((1, D)), _const_spec((8, D)), _const_spec((POOL_IN, D))],
        out_specs=(row, fixed((POOL_IN, D)), fixed((8, D))),
        scratch_shapes=[pltpu.VMEM((tile + HALO, D), F32), pltpu.VMEM((tile, POOL_IN), BF16), pltpu.VMEM((POOL_IN, D), F32),
                        pltpu.VMEM((8, D), F32)],
        compiler_params=_cparams(1),
    )(x, dxn, dpool, dpool, dg, ng, mod, w_t)


def _build_vec(vecs, gates, pool_vecs, gains, dsinks, loss_part):
    def body(v0, v1, v2, v3, g0, g2, p0, p1, n0, n1, s0, s1, loss_ref, out):
        out[...] = jnp.zeros_like(out)
        for i, v in enumerate((v0, v1, v2, v3)):
            out[3 * i:3 * i + 2, :] = v[0:2, :]
            out[12 + i:13 + i, :] = v[3:4, :]
        out[2:3, :] = g0[...]
        out[8:9, :] = g2[...]
        for j, (p, n, s) in enumerate(((p0, n0, s0), (p1, n1, s1))):
            out[3 * (2 * j + 1) + 2:3 * (2 * j + 1) + 3, :] = p[0:1, :]
            out[22 + j:23 + j, :] = p[1:2, :]
            out[16 + j:17 + j, :] = n[:, 0:D]
            out[18 + j:19 + j, 0:QK_W - D] = n[:, D:QK_W]
            out[20 + j:21 + j, 0:LANES] = s[...]
        out[24:25, 0:LANES] = loss_ref[...]

    vm = pl.BlockSpec(memory_space=pltpu.VMEM)
    args = (*vecs, gates[0], gates[2], *pool_vecs, *gains, *dsinks, loss_part)
    return _call(
        body, name="build_vec",
        out_shape=jax.ShapeDtypeStruct((VEC_ROWS, D), F32),
        in_specs=[vm] * len(args), out_specs=vm,
        compiler_params=_cparams(),
    )(*args)


def _sum_devices(g, after):
    rows = g.shape[1]

    def body(g_ref, after_ref, tot_ref, fold_ref):
        tot = g_ref[0]
        for p in range(1, N_DEV):
            tot = tot + g_ref[p]
        tot_ref[...] = tot
        f = tot[16:24, 0:LANES]
        for b in range(1, D // LANES):
            f = f + tot[16:24, LANES * b:LANES * (b + 1)]
        fold_ref[...] = f + pltpu.roll(f, HEAD_DIM, 1)

    return _call(
        body, name="sum_devices",
        out_shape=(jax.ShapeDtypeStruct((rows, D), F32), jax.ShapeDtypeStruct((8, LANES), F32)),
        in_specs=[pl.BlockSpec(memory_space=pltpu.VMEM), ANY_SPEC],
        out_specs=(pl.BlockSpec(memory_space=pltpu.VMEM), pl.BlockSpec(memory_space=pltpu.VMEM)),
        compiler_params=_cparams(),
    )(g, after)


def _adamw_small(params):
    n = len(params)

    def body(*refs):
        ins, outs = refs[:4 * n], refs[4 * n:]
        for p in range(n):
            w_ref, g_ref, m_ref, v_ref = ins[4 * p:4 * p + 4]
            outs[3 * p][...], outs[3 * p + 1][...], outs[3 * p + 2][...] = _adamw(w_ref[...], g_ref[...], m_ref[...], v_ref[...])

    vm = pl.BlockSpec(memory_space=pltpu.VMEM)
    out = _call(
        body, name="adamw_small",
        out_shape=tuple(jax.ShapeDtypeStruct(w.shape, F32) for (w, _, _, _) in params for _ in range(3)),
        in_specs=[vm] * (4 * n), out_specs=tuple([vm] * (3 * n)),
        compiler_params=_cparams(),
    )(*[a for p in params for a in p])
    return [tuple(out[3 * p:3 * p + 3]) for p in range(n)]


def _adamw_shards(name, me, fulls, lands, w, m, v, transpose, axis=0):
    nl = w.shape[0]
    wshape = w.shape[1:]
    own_shape = lands[0].shape[1:]

    def body(me_ref, *refs):
        own_refs, land_refs = refs[:nl], refs[nl:2 * nl]
        w_ref, m_ref, v_ref, g_out, d_out, m_out, v_out = refs[2 * nl:]
        layer = pl.program_id(0)
        for l in range(nl):
            @pl.when(layer == l)
            def _(l=l):
                g = own_refs[l][...].astype(F32)
                for k in range(N_DEV - 1):
                    g = g + land_refs[l][k].astype(F32)
                if transpose:
                    g = g.T
                g_out[...] = g
                d_out[...], m_out[...], v_out[...] = _adamw(w_ref[...], g, m_ref[...], v_ref[...])

    def own_index(l_, me_ref):
        idx = [0] * len(own_shape)
        idx[axis] = me_ref[0]
        return tuple(idx)

    own_spec = pl.BlockSpec(tuple(own_shape), own_index)
    land_spec = pl.BlockSpec((N_DEV - 1,) + tuple(own_shape), lambda l_, me_ref: (0,) * (1 + len(own_shape)))
    wspec = pl.BlockSpec((None,) + tuple(wshape), lambda l_, me_ref: (l_,) + (0,) * len(wshape))
    return _call(
        body, name=name,
        grid_spec=pltpu.PrefetchScalarGridSpec(num_scalar_prefetch=1, grid=(nl,),
                                               in_specs=[own_spec] * nl + [land_spec] * nl + [wspec] * 3,
                                               out_specs=(wspec,) * 4),
        out_shape=tuple(jax.ShapeDtypeStruct(w.shape, F32) for _ in range(4)),
        compiler_params=_cparams(1),
    )(me.reshape(1), *fulls, *lands, w, m, v)


def _constants():
    lane = np.arange(LANES)
    bd = (lane[:, None] // HEAD_DIM == lane[None, :] // HEAD_DIM).astype(np.float32)
    half = ROT_DIM // 2
    inv_freq = ROPE_THETA ** (-jnp.arange(half, dtype=F32) * 2.0 / ROT_DIM)
    invf = jnp.tile(inv_freq, LANES // half).reshape(1, LANES)
    return jnp.asarray(bd, BF16), invf


def kernel(x, c, positions, ada_w, ada_b, norm_g, attn_w_in, attn_q_norm, attn_k_norm, attn_sinks, attn_w_out, pool_w_in, pool_w_group, pool_scale, pool_w_out, loss_target, m_ada_w, m_ada_b, m_norm_g, m_attn_w_in, m_attn_q_norm, m_attn_k_norm, m_attn_sinks, m_attn_w_out, m_pool_w_in, m_pool_w_group, m_pool_scale, m_pool_w_out, v_ada_w, v_ada_b, v_norm_g, v_attn_w_in, v_attn_q_norm, v_attn_k_norm, v_attn_sinks, v_attn_w_out, v_pool_w_in, v_pool_w_group, v_pool_scale, v_pool_w_out):
    seq = x.shape[1]
    me = 4 * lax.axis_index("x") + 2 * lax.axis_index("y") + lax.axis_index("c")
    bd, invf = _constants()
    t_mm = min(512, seq)
    rope = _rope_table(positions.reshape(seq, 1), invf, t_mm)
    t_bw = min(256, seq)
    shard = pool_scale.shape[1]
    cols = ada_w.shape[2]

    layers = _prep_weights(me, attn_w_in, attn_w_out, pool_w_in, pool_w_out, pool_w_group)
    first_w, token = _gather_first_start(layers[0][0], c)

    first = jnp.concatenate([c, jnp.pad(pool_scale, ((0, 0), (0, D - shard))), jnp.zeros((5, D), F32)], axis=0)
    first = _allgather_small(first + token[0, 0], "allgather_c", rope)
    c_all = first[:, 0, :]
    scale_full = jnp.transpose(first[:, 1:3, :shard], (1, 0, 2)).reshape(2, D)
    mod_part = _ada_forward(c_all, ada_w)
    mod_all = _allgather_small(mod_part.reshape(DEPTH * N_DEV, cols), "allgather_mod", c_all)
    mod_all = mod_all.reshape(N_DEV, DEPTH, N_DEV, cols)
    mine = lax.dynamic_index_in_dim(mod_all, me, axis=2, keepdims=False)
    mod = jnp.transpose(mine, (1, 0, 2)).reshape(DEPTH, 3 * D) + ada_b
    mod = jnp.pad(mod.reshape(DEPTH, 3, D), ((0, 0), (0, 5), (0, 0)))

    groups = [[layers[0][1]], layers[0][2:5], layers[1][0:2], layers[1][2:5]]
    gaxes = [(0,), (0,), (0, 0, 1), (0, 0), (0, 0, 1)]
    first_w, token = _gather_first_forward(first_w, mod)
    rest, token = _gather_start(groups, gaxes[1:], token, "gather_start_rest")
    started = [None] + rest

    saved, weights = [], []
    h = x[0]
    for i in range(DEPTH):
        j = i // 2
        s = dict(x=h, ng=norm_g[i:i + 1], md=mod[i])
        if i == 0:
            w_in_t = _gather_first_wait(first_w, token)
        else:
            wts = _gather_wait(started[i + 1], gaxes[i + 1], h, f"gather_wait_{i}")
        if i % 2 == 0:
            if i > 0:
                w_in_t, w_out = wts
            s["gain"] = jnp.concatenate([jnp.tile(attn_q_norm[j], N_HEADS), jnp.tile(attn_k_norm[j], N_KV)]).reshape(1, QK_W)
            s["qk_raw"], s["qs"], s["kd"], s["vd"], s["g"] = _attn_in_proj(
                h, rope, s["ng"], s["md"], w_in_t, j, s["gain"], bd, t_bw)
            s["o"] = _attn_forward(attn_sinks[j], s["qs"], s["kd"], s["vd"], j)
            if i == 0:
                w_out, = _gather_wait(started[1], gaxes[1], s["o"], "gather_wait_0_out")
            h, s["br"] = _attn_out_proj(h, s["o"], s["g"], w_out, j, s["md"], t_mm)
            weights.append((w_in_t, w_out))
        else:
            p_in_t, p_out, p_grp = wts
            s["scale"] = scale_full[j:j + 1]
            s["v"], s["g"] = _pool_in_proj(h, s["ng"], s["md"], p_in_t, j, t_mm)
            if i < DEPTH - 1:
                h, s["br"] = _pool_mix_out(h, s["v"], s["g"], p_grp, p_out, j, s["scale"], s["md"], t_mm)
            else:
                dx, s["br"], loss_part = _pool_mix_out(h, s["v"], s["g"], p_grp, p_out, j, s["scale"], s["md"], t_mm,
                                                       loss_target[0])
            weights.append(wts)
        saved.append(s)

    vecs, gates, gains, dsinks, pool_vecs = [None] * DEPTH, [None] * DEPTH, [None] * 2, [None] * 2, [None] * 2
    sent_in, sent_out = [None] * DEPTH, [None] * DEPTH
    token = jnp.zeros((8, LANES), F32)
    for i in reversed(range(DEPTH)):
        j = i // 2
        s = saved[i]
        md = s["md"] + token[0, 0]
        if i % 2 == 0:
            w_in_t, w_out = weights[i]
            dos, dg, d_w_out, gates[i] = _attn_out_proj_bwd(dx, s["br"], s["o"], s["g"], w_out, j, md, t_mm)
            sent_out[i], token = _scatter_start([d_w_out], (0,), f"scatter_start_{i}_out", token)
            dq, dk, dv, dsinks[j] = _attn_backward(attn_sinks[j] + token[0, 0], s["qs"], dos, s["kd"], s["vd"], j)
            dx, d_in_t, vecs[i], gains[j] = _attn_in_proj_bwd(
                s["x"], dx, rope, s["qk_raw"], dq, dk, dv, dg, s["ng"], md, w_in_t, j, s["gain"], bd, t_bw)
        else:
            p_in_t, p_out, p_grp = weights[i]
            dpool, dg, d_p_out, d_p_grp, pool_vecs[j] = _pool_mix_out_bwd(
                dx, s["br"], s["v"], s["g"], p_grp, p_out, j, s["scale"], md, t_mm)
            sent_out[i], token = _scatter_start([d_p_out, d_p_grp], (0, 1), f"scatter_start_{i}_out", token)
            dx, d_in_t, vecs[i] = _pool_in_proj_bwd(s["x"], dx, dpool, dg, s["ng"], s["md"] + token[0, 0], p_in_t, j, t_bw)
        if i > 0:
            sent_in[i], token = _scatter_start([d_in_t], (0,), f"scatter_start_{i}_in", token)

    vec = _build_vec(vecs, gates, pool_vecs, gains, dsinks, loss_part)
    vec_all = _allgather_small(vec, "allgather_vec", loss_part)
    sent_in[0], token = _scatter_start([d_in_t], (0,), "scatter_start_0_in", vec_all)
    tot, folded = _sum_devices(vec_all, token)
    loss = tot[24, 0]
    small = dict(
        ada_b=(ada_b, tot[0:12].reshape(DEPTH, 3 * D), m_ada_b, v_ada_b),
        norm_g=(norm_g, tot[12:16], m_norm_g, v_norm_g),
        q_norm=(attn_q_norm, folded[0:2, :HEAD_DIM], m_attn_q_norm, v_attn_q_norm),
        k_norm=(attn_k_norm, folded[2:4, :HEAD_DIM], m_attn_k_norm, v_attn_k_norm),
        sinks=(attn_sinks, tot[20:22, :N_HEADS], m_attn_sinks, v_attn_sinks),
        pool_scale=(pool_scale, lax.dynamic_slice(tot, (22, me * shard), (2, shard)), m_pool_scale, v_pool_scale),
    )
    res = {k: (a[1],) + upd for (k, a), upd in zip(small.items(), _adamw_small(list(small.values())))}

    dmod_all = vec_all[:, 0:12, :].reshape(N_DEV, DEPTH, 3 * D)
    dmod_mine = lax.dynamic_slice_in_dim(dmod_all, me * cols, cols, axis=2)
    dmod_mine = jnp.pad(jnp.transpose(dmod_mine, (1, 0, 2)), ((0, 0), (0, N_DEV), (0, 0))) + token[0, 0]
    res["ada_w"] = _ada_backward_adamw(jnp.pad(c_all, ((0, N_DEV), (0, 0))), dmod_mine, ada_w, m_ada_w, v_ada_w)

    got_in, got_out = [None] * DEPTH, [None] * DEPTH
    for i in (3, 1):
        got_out[i] = _scatter_wait(sent_out[i], (0, 1), res["ada_w"][0], f"scatter_wait_{i}_out")
        got_in[i] = _scatter_wait(sent_in[i], (0,), res["ada_w"][0], f"scatter_wait_{i}_in")
    pick = lambda got, ls, a: ([got[i][0][a] for i in ls], [got[i][1][a] for i in ls])
    res["pool_w_in"] = _adamw_shards("adamw_pool_w_in", me, *pick(got_in, (1, 3), 0), pool_w_in, m_pool_w_in, v_pool_w_in, True)
    res["pool_w_out"] = _adamw_shards("adamw_pool_w_out", me, *pick(got_out, (1, 3), 0), pool_w_out, m_pool_w_out,
                                      v_pool_w_out, False)
    res["pool_w_group"] = _adamw_shards("adamw_pool_w_group", me, *pick(got_out, (1, 3), 1), pool_w_group, m_pool_w_group,
                                        v_pool_w_group, False, axis=1)
    for i in (2, 0):
        got_out[i] = _scatter_wait(sent_out[i], (0,), res["pool_w_group"][0], f"scatter_wait_{i}_out")
        got_in[i] = _scatter_wait(sent_in[i], (0,), res["pool_w_group"][0], f"scatter_wait_{i}_in")
    res["attn_w_out"] = _adamw_shards("adamw_attn_w_out", me, *pick(got_out, (0, 2), 0), attn_w_out, m_attn_w_out,
                                      v_attn_w_out, False)
    res["attn_w_in"] = _adamw_shards("adamw_attn_w_in", me, *pick(got_in, (0, 2), 0), attn_w_in, m_attn_w_in, v_attn_w_in, True)

    order = ("ada_w", "ada_b", "norm_g", "attn_w_in", "q_norm", "k_norm", "sinks", "attn_w_out", "pool_w_in",
             "pool_w_group", "pool_scale", "pool_w_out")
    return (loss, dx[None], *[res[k][0] for k in order], *[res[k][1] for k in order], *[res[k][2] for k in order],
            *[res[k][3] for k in order])
```

```python
import functools

import numpy as np
import jax
import jax.numpy as jnp
from jax import lax
from jax.experimental import pallas as pl
from jax.experimental.pallas import tpu as pltpu

F32 = jnp.float32
BF16 = jnp.bfloat16
MESH = pl.DeviceIdType.MESH

N_DEV = 8
D = 1024
DEPTH = 4
HEAD_DIM = 64
N_HEADS = 16
N_KV = 4
QK_W = 1280
ATTN_IN = 2560
POOL_IN = 2048
QBLK = 128
KX_W = N_KV * 128
CHUNK = 256
POOL_WINDOWS = (2, 4, 8, 16)
HALO = 16
ROPE_THETA = 500000.0
ROT_DIM = 16
NORM_EPS = 1e-6
ADAM_LR = 0.001
ADAM_B1 = 0.9
ADAM_B2 = 0.999
ADAM_EPS = 1e-08
ADAM_WD = 0.01
ADAM_STEP = 10

LANES = 128
VMEM_LIMIT = 56 * 2**20
VEC_ROWS = 32


def _cparams(n_grid=0, **kw):
    if n_grid:
        kw["dimension_semantics"] = ("arbitrary",) * n_grid
    return pltpu.CompilerParams(vmem_limit_bytes=VMEM_LIMIT, **kw)


def _call(body, **kw):
    return pl.pallas_call(body, **kw)


def _const_spec(shape):
    nd = len(shape)
    return pl.BlockSpec(shape, lambda *_: (0,) * nd, pipeline_mode=pl.Buffered(1))


def _dot(a, b):
    return jnp.dot(a, b, preferred_element_type=F32)


def _dot_nt(a, b):
    return lax.dot_general(a, b, (((1,), (1,)), ((), ())), preferred_element_type=F32)


def _dot_tn(a, b):
    return lax.dot_general(a, b, (((0,), (0,)), ((), ())), preferred_element_type=F32)


def _dot_split(x, m):
    hi = x.astype(BF16)
    lo = (x - hi.astype(F32)).astype(BF16)
    return _dot(hi, m) + _dot(lo, m)


def _sigmoid(g):
    return 1.0 / (1.0 + jnp.exp(-g))


def _norm_mod(x, ng, sc, sh):
    r = lax.rsqrt(jnp.mean(x * x, axis=-1, keepdims=True) + NORM_EPS)
    xh = x * r
    h = (xh * ng) * (1.0 + sc) + sh
    return xh, r, h


def _rope_table(pos_col, invf_row, tile):
    seq = pos_col.shape[0]

    def body(pos_ref, invf_ref, out_ref):
        ang = pos_ref[...].astype(F32) * invf_ref[...]
        l64 = lax.broadcasted_iota(jnp.int32, (tile, LANES), 1) & (HEAD_DIM - 1)
        cs, sn = jnp.cos(ang), jnp.sin(ang)
        out_ref[:, 0:LANES] = jnp.where(l64 < ROT_DIM, cs, 1.0)
        out_ref[:, LANES:2 * LANES] = jnp.where(l64 < ROT_DIM // 2, -sn, 0.0)
        out_ref[:, 2 * LANES:3 * LANES] = jnp.where((l64 >= ROT_DIM // 2) & (l64 < ROT_DIM), sn, 0.0)

    return _call(
        body, name="rope_table", grid=(seq // tile,),
        out_shape=jax.ShapeDtypeStruct((seq, 3 * LANES), F32),
        in_specs=[pl.BlockSpec((tile, 1), lambda i: (i, 0)), _const_spec((1, LANES))],
        out_specs=pl.BlockSpec((tile, 3 * LANES), lambda i: (i, 0)),
        compiler_params=_cparams(1),
    )(pos_col, invf_row)


def _rope_tabs(rope_ref):
    return rope_ref[:, 0:LANES], rope_ref[:, LANES:2 * LANES], rope_ref[:, 2 * LANES:3 * LANES]


def _rope(y, tabs):
    cos_t, sin_a, sin_b = tabs
    return y * cos_t + pltpu.roll(y, LANES - ROT_DIM // 2, 1) * sin_a + pltpu.roll(y, ROT_DIM // 2, 1) * sin_b


def _rope_bwd(dy, tabs):
    cos_t, sin_a, sin_b = tabs
    return dy * cos_t + pltpu.roll(dy * sin_a, ROT_DIM // 2, 1) + pltpu.roll(dy * sin_b, LANES - ROT_DIM // 2, 1)


def _low_half(rows):
    return lax.broadcasted_iota(jnp.int32, (rows, LANES), 1) < HEAD_DIM


def _adamw(w, g, m, v):
    m = ADAM_B1 * m + (1.0 - ADAM_B1) * g
    v = ADAM_B2 * v + (1.0 - ADAM_B2) * (g * g)
    m_hat = m / (1.0 - ADAM_B1 ** ADAM_STEP)
    v_hat = v / (1.0 - ADAM_B2 ** ADAM_STEP)
    delta = -ADAM_LR * (m_hat / (jnp.sqrt(v_hat) + ADAM_EPS) + ADAM_WD * w)
    return delta, m, v


def _my_position():
    x, y, c = lax.axis_index("x"), lax.axis_index("y"), lax.axis_index("c")
    return x, y, c, 4 * x + 2 * y + c


def _peers(x, y, c):
    out = []
    for k in range(1, N_DEV):
        px = 1 - x if k & 4 else x
        py = 1 - y if k & 2 else y
        pc = 1 - c if k & 1 else c
        out.append(((px, py, pc), 4 * px + 2 * py + pc))
    return out


def _allgather_small(v, name, after):
    rows, cols = v.shape

    def body(v_ref, after_ref, out_ref, send_sems, recv_sems, local_sem):
        x, y, c, me = _my_position()
        local = pltpu.make_async_copy(v_ref, out_ref.at[me], local_sem)
        local.start()
        sends = []
        for k, (peer, _) in enumerate(_peers(x, y, c)):
            cp = pltpu.make_async_remote_copy(v_ref, out_ref.at[me], send_sems.at[k], recv_sems.at[k],
                                              device_id=peer, device_id_type=MESH)
            cp.start()
            sends.append(cp)
        for k, (peer, idx) in enumerate(_peers(x, y, c)):
            pltpu.make_async_remote_copy(v_ref, out_ref.at[idx], send_sems.at[k], recv_sems.at[k],
                                         device_id=peer, device_id_type=MESH).wait_recv()
        for cp in sends:
            cp.wait_send()
        local.wait()

    return _call(
        body, name=name,
        out_shape=jax.ShapeDtypeStruct((N_DEV, rows, cols), F32),
        in_specs=[pl.BlockSpec(memory_space=pltpu.VMEM), pl.BlockSpec(memory_space=pl.ANY)],
        out_specs=pl.BlockSpec(memory_space=pltpu.VMEM),
        scratch_shapes=[pltpu.SemaphoreType.DMA((N_DEV - 1,)), pltpu.SemaphoreType.DMA((N_DEV - 1,)),
                        pltpu.SemaphoreType.DMA(())],
        compiler_params=_cparams(),
    )(v, after)


def _shard_rows(ref, idx, rows, axis):
    sl = [slice(None)] * len(ref.shape)
    sl[axis] = pl.ds(idx * rows, rows)
    return ref.at[tuple(sl)]


def _own_and_peer_rows(ref, me, idx, axis):
    rows = ref.shape[axis] // N_DEV
    return _shard_rows(ref, me, rows, axis), _shard_rows(ref, idx, rows, axis)


HBM_SPEC = pl.BlockSpec(memory_space=pltpu.HBM)
SEM_SPEC = pl.BlockSpec(memory_space=pltpu.SEMAPHORE)
ANY_SPEC = pl.BlockSpec(memory_space=pl.ANY)
DATAFLOW = pltpu.SideEffectType.DATAFLOW_SIDE_EFFECTING


def _hbm(a):
    return pltpu.with_memory_space_constraint(a, pltpu.HBM)


def _gather_start(layers, axes, after, name):
    flat = [a for arrs in layers for a in arrs]
    flat_axes = [ax for axs in axes for ax in axs]
    n, nl = len(flat), len(layers)

    def body(*refs):
        ins, sems, token = refs[:n], refs[n + 1:n + 1 + 2 * nl], refs[-1]
        x, y, c, me = _my_position()
        a0 = 0
        for li, arrs in enumerate(layers):
            for k, (peer, _) in enumerate(_peers(x, y, c)):
                for a in range(len(arrs)):
                    rows, _ = _own_and_peer_rows(ins[a0 + a], me, me, flat_axes[a0 + a])
                    pltpu.make_async_remote_copy(rows, rows, sems[2 * li].at[k * len(arrs) + a],
                                                 sems[2 * li + 1].at[k * len(arrs) + a],
                                                 device_id=peer, device_id_type=MESH).start()
            a0 += len(arrs)
        token[...] = jnp.zeros_like(token)

    sem_shapes = []
    for arrs in layers:
        sem_shapes += [pltpu.SemaphoreType.DMA(((N_DEV - 1) * len(arrs),))] * 2
    out = _call(
        body, name=name,
        out_shape=(*sem_shapes, *[pltpu.HBM(a.shape, a.dtype) for a in flat], jax.ShapeDtypeStruct((8, LANES), F32)),
        in_specs=[HBM_SPEC] * n + [ANY_SPEC],
        out_specs=(*[SEM_SPEC] * (2 * nl), *[HBM_SPEC] * n, pl.BlockSpec(memory_space=pltpu.VMEM)),
        input_output_aliases={a: 2 * nl + a for a in range(n)},
        compiler_params=_cparams(has_side_effects=DATAFLOW),
    )(*[_hbm(a) for a in flat], after)
    per_layer, a0 = [], 0
    for li, arrs in enumerate(layers):
        per_layer.append((out[2 * li], out[2 * li + 1], list(out[2 * nl + a0:2 * nl + a0 + len(arrs)])))
        a0 += len(arrs)
    return per_layer, out[-1]


def _gather_wait(started, axes, after, name):
    send_sems, recv_sems, arrs = started
    n = len(arrs)

    def body(*refs):
        ins, send_ref, recv_ref = refs[:n], refs[n], refs[n + 1]
        x, y, c, me = _my_position()
        for k, (peer, idx) in enumerate(_peers(x, y, c)):
            for a in range(n):
                own, theirs = _own_and_peer_rows(ins[a], me, idx, axes[a])
                cp = pltpu.make_async_remote_copy(own, theirs, send_ref.at[k * n + a], recv_ref.at[k * n + a],
                                                  device_id=peer, device_id_type=MESH)
                cp.wait_send()
                cp.wait_recv()

    return _call(
        body, name=name,
        out_shape=tuple(pltpu.HBM(a.shape, a.dtype) for a in arrs),
        in_specs=[HBM_SPEC] * n + [SEM_SPEC, SEM_SPEC, ANY_SPEC],
        out_specs=tuple([HBM_SPEC] * n),
        input_output_aliases={a: a for a in range(n)},
        compiler_params=_cparams(has_side_effects=DATAFLOW),
    )(*arrs, send_sems, recv_sems, after)


def _first_relations(x, y, c):
    return [(x, y, 1 - c), (1 - x, y, c), (x, 1 - y, c), (1 - x, 1 - y, c)]


def _gather_first_start(arr, after):
    n_rel = 4

    def body(a_ref, after_ref, send_ref, recv_ref, thru, token):
        x, y, c, me = _my_position()
        rows, _ = _own_and_peer_rows(a_ref, me, me, 0)
        for k, peer in enumerate(_first_relations(x, y, c)):
            pltpu.make_async_remote_copy(rows, rows, send_ref.at[k], recv_ref.at[k], device_id=peer, device_id_type=MESH).start()
        token[...] = jnp.zeros_like(token)

    sem = pltpu.SemaphoreType.DMA((n_rel,))
    out = _call(
        body, name="gather_first_start",
        out_shape=(sem, sem, pltpu.HBM(arr.shape, arr.dtype), jax.ShapeDtypeStruct((8, LANES), F32)),
        in_specs=[HBM_SPEC, ANY_SPEC],
        out_specs=(SEM_SPEC, SEM_SPEC, HBM_SPEC, pl.BlockSpec(memory_space=pltpu.VMEM)),
        input_output_aliases={0: 2},
        compiler_params=_cparams(has_side_effects=DATAFLOW),
    )(_hbm(arr), after)
    return out[:3], out[3]


def _gather_first_forward(started, after):
    send_a, recv_a, arr = started

    def body(a_ref, send_a_ref, recv_a_ref, after_ref, send_b_ref, recv_b_ref, thru, token):
        x, y, c, me = _my_position()
        sibling = (x, y, 1 - c)
        for k, peer in enumerate(_first_relations(x, y, c)):
            own, theirs = _own_and_peer_rows(a_ref, me, 4 * peer[0] + 2 * peer[1] + peer[2], 0)
            cp = pltpu.make_async_remote_copy(own, theirs, send_a_ref.at[k], recv_a_ref.at[k], device_id=peer, device_id_type=MESH)
            cp.wait_send()
            cp.wait_recv()
            if k > 0:
                pltpu.make_async_remote_copy(theirs, theirs, send_b_ref.at[k - 1], recv_b_ref.at[k - 1],
                                             device_id=sibling, device_id_type=MESH).start()
        token[...] = jnp.zeros_like(token)

    sem = pltpu.SemaphoreType.DMA((3,))
    out = _call(
        body, name="gather_first_forward",
        out_shape=(sem, sem, pltpu.HBM(arr.shape, arr.dtype), jax.ShapeDtypeStruct((8, LANES), F32)),
        in_specs=[HBM_SPEC, SEM_SPEC, SEM_SPEC, ANY_SPEC],
        out_specs=(SEM_SPEC, SEM_SPEC, HBM_SPEC, pl.BlockSpec(memory_space=pltpu.VMEM)),
        input_output_aliases={0: 2},
        compiler_params=_cparams(has_side_effects=DATAFLOW),
    )(arr, send_a, recv_a, after)
    return out[:3], out[3]


def _gather_first_wait(forwarded, after):
    send_b, recv_b, arr = forwarded

    def body(a_ref, send_b_ref, recv_b_ref, after_ref, thru):
        x, y, c, me = _my_position()
        sibling = (x, y, 1 - c)
        for k, peer in enumerate(_first_relations(x, y, c)[1:]):
            _, sent = _own_and_peer_rows(a_ref, me, 4 * peer[0] + 2 * peer[1] + peer[2], 0)
            _, got = _own_and_peer_rows(a_ref, me, 4 * peer[0] + 2 * peer[1] + (1 - peer[2]), 0)
            cp = pltpu.make_async_remote_copy(sent, got, send_b_ref.at[k], recv_b_ref.at[k], device_id=sibling, device_id_type=MESH)
            cp.wait_send()
            cp.wait_recv()

    return _call(
        body, name="gather_first_wait",
        out_shape=pltpu.HBM(arr.shape, arr.dtype),
        in_specs=[HBM_SPEC, SEM_SPEC, SEM_SPEC, ANY_SPEC],
        out_specs=HBM_SPEC,
        input_output_aliases={0: 0},
        compiler_params=_cparams(has_side_effects=DATAFLOW),
    )(arr, send_b, recv_b, after)


def _scatter_start(fulls, axes, name, after):
    n = len(fulls)
    lands = []
    for f, ax in zip(fulls, axes):
        shp = list(f.shape)
        shp[ax] //= N_DEV
        lands.append(_hbm(lax.empty((N_DEV - 1,) + tuple(shp), f.dtype)))

    def body(*refs):
        srcs, dsts, send_ref, recv_ref, token = refs[:n], refs[n:2 * n], refs[2 * n + 1], refs[2 * n + 2], refs[-1]
        x, y, c, me = _my_position()
        for k, (peer, idx) in enumerate(_peers(x, y, c)):
            for a in range(n):
                _, theirs = _own_and_peer_rows(srcs[a], me, idx, axes[a])
                pltpu.make_async_remote_copy(theirs, dsts[a].at[k], send_ref.at[k * n + a], recv_ref.at[k * n + a],
                                             device_id=peer, device_id_type=MESH).start()
        token[...] = jnp.zeros_like(token)

    sem = pltpu.SemaphoreType.DMA(((N_DEV - 1) * n,))
    out = _call(
        body, name=name,
        out_shape=(sem, sem, *[pltpu.HBM(a.shape, a.dtype) for a in fulls], *[pltpu.HBM(a.shape, a.dtype) for a in lands],
                   jax.ShapeDtypeStruct((8, LANES), F32)),
        in_specs=[HBM_SPEC] * (2 * n) + [ANY_SPEC],
        out_specs=(SEM_SPEC, SEM_SPEC, *[HBM_SPEC] * (2 * n), pl.BlockSpec(memory_space=pltpu.VMEM)),
        input_output_aliases={a: 2 + a for a in range(2 * n)},
        compiler_params=_cparams(has_side_effects=DATAFLOW),
    )(*[_hbm(a) for a in fulls], *lands, after)
    return (out[0], out[1], list(out[2:2 + n]), list(out[2 + n:2 + 2 * n])), out[-1]


def _scatter_wait(started, axes, after, name):
    send_sems, recv_sems, fulls, lands = started
    n = len(fulls)

    def body(*refs):
        srcs, dsts, send_ref, recv_ref = refs[:n], refs[n:2 * n], refs[2 * n], refs[2 * n + 1]
        x, y, c, me = _my_position()
        for k, (peer, idx) in enumerate(_peers(x, y, c)):
            for a in range(n):
                _, theirs = _own_and_peer_rows(srcs[a], me, idx, axes[a])
                cp = pltpu.make_async_remote_copy(theirs, dsts[a].at[k], send_ref.at[k * n + a], recv_ref.at[k * n + a],
                                                  device_id=peer, device_id_type=MESH)
                cp.wait_send()
                cp.wait_recv()

    out = _call(
        body, name=name,
        out_shape=tuple(pltpu.HBM(a.shape, a.dtype) for a in (*fulls, *lands)),
        in_specs=[HBM_SPEC] * (2 * n) + [SEM_SPEC, SEM_SPEC, ANY_SPEC],
        out_specs=tuple([HBM_SPEC] * (2 * n)),
        input_output_aliases={a: a for a in range(2 * n)},
        compiler_params=_cparams(has_side_effects=DATAFLOW),
    )(*fulls, *lands, send_sems, recv_sems, after)
    return list(out[:n]), list(out[n:])


def _prep_weights(me, attn_w_in, attn_w_out, pool_w_in, pool_w_out, pool_w_group):
    nl = attn_w_in.shape[0]

    def body(me_ref, *refs):
        ins, outs = refs[:5 * nl], refs[5 * nl:]
        for j in range(nl):
            awi, awo, pwi, pwo, pwg = ins[5 * j:5 * j + 5]
            o_awi, o_awo, o_pwi, o_pwo, o_pwg = outs[5 * j:5 * j + 5]
            o_awi[...] = awi[...].T.astype(BF16)
            o_awo[...] = awo[...].astype(BF16)
            o_pwi[...] = pwi[...].T.astype(BF16)
            o_pwo[...] = pwo[...].astype(BF16)
            o_pwg[...] = pwg[...].astype(BF16)

    def in_spec(shape, j):
        nd = len(shape)
        return pl.BlockSpec((None,) + tuple(shape), lambda i, me_ref: (j,) + (0,) * nd)

    srcs = (attn_w_in, attn_w_out, pool_w_in, pool_w_out, pool_w_group)
    rows_spec = lambda r: pl.BlockSpec((r, D), lambda i, me_ref: (me_ref[0], 0))
    grp = pool_w_group.shape[1:]
    grp_spec = pl.BlockSpec(grp, lambda i, me_ref: (0, me_ref[0], 0))
    ins, in_specs, out_shapes, out_specs = [], [], [], []
    for j in range(nl):
        ins += list(srcs)
        in_specs += [in_spec(a.shape[1:], j) for a in srcs]
        out_shapes += [(N_DEV * attn_w_in.shape[2], D), (N_DEV * attn_w_out.shape[1], D), (N_DEV * pool_w_in.shape[2], D),
                       (N_DEV * pool_w_out.shape[1], D), (grp[0], N_DEV * grp[1], grp[2])]
        out_specs += [rows_spec(attn_w_in.shape[2]), rows_spec(attn_w_out.shape[1]), rows_spec(pool_w_in.shape[2]),
                      rows_spec(pool_w_out.shape[1]), grp_spec]
    out = _call(
        body, name="prep_weights",
        grid_spec=pltpu.PrefetchScalarGridSpec(num_scalar_prefetch=1, grid=(1,), in_specs=in_specs, out_specs=tuple(out_specs)),
        out_shape=tuple(jax.ShapeDtypeStruct(s, BF16) for s in out_shapes),
        compiler_params=_cparams(1),
    )(me.reshape(1), *ins)
    return [list(out[5 * j:5 * j + 5]) for j in range(nl)]


def _ada_forward(c_all, ada_w):
    cols = ada_w.shape[2]

    def body(c_ref, w_ref, o_ref):
        cv = c_ref[...]
        sc = (cv * _sigmoid(cv)).astype(BF16)
        o_ref[...] = _dot(sc, w_ref[...].astype(BF16))

    return _call(
        body, name="ada_forward", grid=(DEPTH,),
        out_shape=jax.ShapeDtypeStruct((DEPTH, N_DEV, cols), F32),
        in_specs=[pl.BlockSpec((N_DEV, D), lambda i: (0, 0)), pl.BlockSpec((None, D, cols), lambda i: (i, 0, 0))],
        out_specs=pl.BlockSpec((None, N_DEV, cols), lambda i: (i, 0, 0)),
        compiler_params=_cparams(1),
    )(c_all, ada_w)


def _ada_backward_adamw(c_pad, dmod_pad, w, m, v):
    cols = w.shape[2]

    def body(c_ref, dm_ref, w_ref, m_ref, v_ref, g_out, d_out, m_out, v_out):
        cv = c_ref[...]
        sc = (cv * _sigmoid(cv)).astype(BF16)
        g = _dot_tn(sc, dm_ref[...].astype(BF16))
        g_out[...] = g
        d_out[...], m_out[...], v_out[...] = _adamw(w_ref[...], g, m_ref[...], v_ref[...])

    wspec = pl.BlockSpec((None, D, cols), lambda i: (i, 0, 0))
    return _call(
        body, name="ada_backward_adamw", grid=(DEPTH,),
        out_shape=tuple(jax.ShapeDtypeStruct(w.shape, F32) for _ in range(4)),
        in_specs=[pl.BlockSpec((2 * N_DEV, D), lambda i: (0, 0)), pl.BlockSpec((None, 2 * N_DEV, cols), lambda i: (i, 0, 0)),
                  wspec, wspec, wspec],
        out_specs=(wspec, wspec, wspec, wspec),
        compiler_params=_cparams(1),
    )(c_pad, dmod_pad, w, m, v)


def _attn_in_proj(x, rope, ng, mod, w_t, j, gain, bd, tile):
    seq = x.shape[0]

    def body(x_ref, rope_ref, ng_ref, mod_ref, w_ref, gain_ref, bd_ref, qk_ref, qs_ref, kd_ref, vd_ref, g_ref):
        _, _, h = _norm_mod(x_ref[...], ng_ref[...], mod_ref[1:2, :], mod_ref[0:1, :])
        hb = h.astype(BF16)
        tabs = _rope_tabs(rope_ref)
        low = _low_half(tile)
        bdm = bd_ref[...]

        def put_kv(ref, blk, first_kv):
            sw = pltpu.roll(blk, HEAD_DIM, 1)
            ref[:, LANES * first_kv:LANES * (first_kv + 1)] = jnp.where(low, blk, sw).astype(BF16)
            ref[:, LANES * (first_kv + 1):LANES * (first_kv + 2)] = jnp.where(low, sw, blk).astype(BF16)

        def project(c):
            return _dot_nt(hb, w_ref[CHUNK * c:CHUNK * (c + 1), :])

        n_chunks = ATTN_IN // CHUNK
        per = CHUNK // LANES
        nxt = project(0)
        for c in range(n_chunks):
            cur = nxt
            if c + 1 < n_chunks:
                nxt = project(c + 1)
            col = CHUNK * c
            if col >= QK_W + N_KV * HEAD_DIM:
                g_ref[:, col - QK_W - N_KV * HEAD_DIM:col - QK_W - N_KV * HEAD_DIM + CHUNK] = cur
            elif col >= QK_W:
                for t in range(per):
                    put_kv(vd_ref, cur[:, LANES * t:LANES * (t + 1)], (col - QK_W) // HEAD_DIM + 2 * t)
            else:
                qk_ref[:, col:col + CHUNK] = cur
                for t in range(per):
                    b = per * c + t
                    blk = cur[:, LANES * t:LANES * (t + 1)]
                    ms = _dot_split(blk * blk, bdm) * (1.0 / HEAD_DIM)
                    y = (blk * lax.rsqrt(ms + NORM_EPS)) * gain_ref[:, LANES * b:LANES * (b + 1)]
                    rp = _rope(y, tabs)
                    if b < D // LANES:
                        rp = rp * (HEAD_DIM ** -0.5)
                        qs_ref[:, 2 * LANES * b:2 * LANES * b + LANES] = jnp.where(low, rp, 0.0).astype(BF16)
                        qs_ref[:, 2 * LANES * b + LANES:2 * LANES * (b + 1)] = jnp.where(low, 0.0, rp).astype(BF16)
                    else:
                        put_kv(kd_ref, rp, 2 * (b - D // LANES))

    row = lambda w: pl.BlockSpec((tile, w), lambda i: (i, 0))
    return _call(
        body, name=f"attn_in_proj_{j}", grid=(seq // tile,),
        out_shape=(jax.ShapeDtypeStruct((seq, QK_W), F32), jax.ShapeDtypeStruct((seq, N_HEADS * LANES), BF16),
                   jax.ShapeDtypeStruct((seq, KX_W), BF16), jax.ShapeDtypeStruct((seq, KX_W), BF16),
                   jax.ShapeDtypeStruct((seq, D), F32)),
        in_specs=[row(D), row(3 * LANES), _const_spec((1, D)), _const_spec((8, D)), _const_spec((ATTN_IN, D)),
                  _const_spec((1, QK_W)), _const_spec((LANES, LANES))],
        out_specs=(row(QK_W), row(N_HEADS * LANES), row(KX_W), row(KX_W), row(D)),
        compiler_params=_cparams(1),
    )(x, rope, ng, mod, w_t, gain, bd)


def _band_mask(n, rows, keys_on_rows):
    shape = (2 * QBLK, rows) if keys_on_rows else (rows, 2 * QBLK)
    qi = lax.broadcasted_iota(jnp.int32, shape, 1 if keys_on_rows else 0) & (QBLK - 1)
    kj = lax.broadcasted_iota(jnp.int32, shape, 0 if keys_on_rows else 1)
    diff = QBLK + qi - kj
    first_key = jnp.where(n > 0, 0, QBLK)
    return (diff >= 0) & (diff < QBLK) & (kj >= first_key)


def _stack_heads(ref, heads):
    return jnp.concatenate([ref[:, LANES * h:LANES * (h + 1)] for h in heads], axis=0)


def _kv_block(prev_ref, cur_ref, kv):
    cols = slice(LANES * kv, LANES * (kv + 1))
    return jnp.concatenate([prev_ref[:, cols], cur_ref[:, cols]], axis=0)


def _pair_up(st, low):
    return jnp.concatenate([jnp.where(low, st[0:QBLK], st[QBLK:2 * QBLK]),
                            jnp.where(low, st[2 * QBLK:3 * QBLK], st[3 * QBLK:4 * QBLK])], axis=1)


def _attn_forward(sinks, qs, kd, vd, j):
    seq = qs.shape[0]
    nb = seq // QBLK

    def body(sink_ref, q_ref, kp_ref, kc_ref, vp_ref, vc_ref, o_ref):
        n = pl.program_id(0)
        ok = _band_mask(n, 4 * QBLK, False)
        low = _low_half(QBLK)
        rowi = lax.broadcasted_iota(jnp.int32, (4 * QBLK, 1), 0)

        def scores(kv):
            return _dot_nt(_stack_heads(q_ref, range(4 * kv, 4 * kv + 4)), _kv_block(kp_ref, kc_ref, kv))

        nxt = scores(0)
        for kv in range(N_KV):
            s = jnp.where(ok, nxt, -1e30)
            if kv + 1 < N_KV:
                nxt = scores(kv + 1)
            sink = jnp.where(rowi < QBLK, sink_ref[4 * kv],
                             jnp.where(rowi < 2 * QBLK, sink_ref[4 * kv + 1],
                                       jnp.where(rowi < 3 * QBLK, sink_ref[4 * kv + 2], sink_ref[4 * kv + 3])))
            m = jnp.maximum(jnp.max(s, axis=1, keepdims=True), sink)
            p = jnp.exp(s - m)
            den = jnp.sum(p, axis=1, keepdims=True) + jnp.exp(sink - m)
            o_st = _dot((p / den).astype(BF16), _kv_block(vp_ref, vc_ref, kv))
            o_ref[:, 2 * LANES * kv:2 * LANES * (kv + 1)] = _pair_up(o_st, low)

    blk = lambda w: pl.BlockSpec((QBLK, w), lambda n: (n, 0))
    prev = lambda w: pl.BlockSpec((QBLK, w), lambda n: (jnp.maximum(n - 1, 0), 0))
    return _call(
        body, name=f"attn_forward_{j}", grid=(nb,),
        out_shape=jax.ShapeDtypeStruct((seq, D), F32),
        in_specs=[pl.BlockSpec(memory_space=pltpu.SMEM), blk(N_HEADS * LANES), prev(KX_W), blk(KX_W), prev(KX_W), blk(KX_W)],
        out_specs=blk(D),
        compiler_params=_cparams(1),
    )(sinks, qs, kd, kd, vd, vd)


def _attn_out_proj(x, o, g, w, j, mod, tile):
    seq = x.shape[0]

    def body(x_ref, o_ref, g_ref, w_ref, mod_ref, xo_ref, br_ref):
        gv = g_ref[...]
        u = (o_ref[...] * (gv * _sigmoid(gv))).astype(BF16)
        br = _dot(u, w_ref[...])
        br_ref[...] = br.astype(BF16)
        xo_ref[...] = x_ref[...] + mod_ref[2:3, :] * br

    row = pl.BlockSpec((tile, D), lambda i: (i, 0))
    return _call(
        body, name=f"attn_out_proj_{j}", grid=(seq // tile,),
        out_shape=(jax.ShapeDtypeStruct((seq, D), F32), jax.ShapeDtypeStruct((seq, D), BF16)),
        in_specs=[row, row, row, _const_spec((D, D)), _const_spec((8, D))],
        out_specs=(row, row),
        compiler_params=_cparams(1),
    )(x, o, g, w, mod)


def _attn_out_proj_bwd(dxn, br, o, g, w, j, mod, tile):
    seq = dxn.shape[0]
    steps = seq // tile

    def body(dxn_ref, br_ref, o_ref, g_ref, w_ref, mod_ref, do_ref, dg_ref, dw_ref, dgate_ref, dw_acc):
        i = pl.program_id(0)

        @pl.when(i == 0)
        def _():
            dw_acc[...] = jnp.zeros_like(dw_acc)
            dgate_ref[...] = jnp.zeros_like(dgate_ref)

        dxn_v, ov, gv = dxn_ref[...], o_ref[...], g_ref[...]
        dgate_ref[...] += jnp.sum(dxn_v * br_ref[...].astype(F32), axis=0, keepdims=True)
        dbr = (dxn_v * mod_ref[2:3, :]).astype(BF16)
        du = _dot_nt(dbr, w_ref[...])
        sg = _sigmoid(gv)
        sl = gv * sg
        dw_acc[...] += _dot_tn((ov * sl).astype(BF16), dbr)
        do = du * sl
        dg_ref[...] = (du * ov * (sg * (1.0 + gv * (1.0 - sg)))).astype(BF16)
        low = _low_half(tile)
        for b in range(D // LANES):
            blk = do[:, LANES * b:LANES * (b + 1)]
            do_ref[:, 2 * LANES * b:2 * LANES * b + LANES] = jnp.where(low, blk, 0.0).astype(BF16)
            do_ref[:, 2 * LANES * b + LANES:2 * LANES * (b + 1)] = jnp.where(low, 0.0, blk).astype(BF16)

        @pl.when(i == steps - 1)
        def _():
            dw_ref[...] = dw_acc[...].astype(BF16)

    row = lambda w_: pl.BlockSpec((tile, w_), lambda i: (i, 0))
    return _call(
        body, name=f"attn_out_proj_bwd_{j}", grid=(steps,),
        out_shape=(jax.ShapeDtypeStruct((seq, N_HEADS * LANES), BF16), jax.ShapeDtypeStruct((seq, D), BF16),
                   jax.ShapeDtypeStruct((D, D), BF16), jax.ShapeDtypeStruct((1, D), F32)),
        in_specs=[row(D), row(D), row(D), row(D), _const_spec((D, D)), _const_spec((8, D))],
        out_specs=(row(N_HEADS * LANES), row(D), pl.BlockSpec((D, D), lambda i: (0, 0)),
                   pl.BlockSpec((1, D), lambda i: (0, 0))),
        scratch_shapes=[pltpu.VMEM((D, D), F32)],
        compiler_params=_cparams(1),
    )(dxn, br, o, g, w, mod)


def _attn_backward(sinks, qs, dos, kd, vd, j):
    seq = qs.shape[0]
    nb = seq // QBLK

    def body(sink_ref, q_ref, do_ref, kp_ref, kc_ref, vp_ref, vc_ref, dq_ref, dk_ref, dv_ref, dsink_ref,
             carry_k, carry_v, sink_acc):
        n = pl.program_id(0)

        @pl.when(n == 0)
        def _():
            carry_k[...] = jnp.zeros_like(carry_k)
            carry_v[...] = jnp.zeros_like(carry_v)
            sink_acc[...] = jnp.zeros_like(sink_acc)

        @pl.when(n < nb)
        def _():
            ok = _band_mask(n, 2 * QBLK, True)
            low = _low_half(QBLK)
            lane_q = lax.broadcasted_iota(jnp.int32, (1, 2 * QBLK), 1)
            dk_parts, dv_parts = [], []

            def first_products(g):
                kv, half = divmod(g, 2)
                heads = (4 * kv + half, 4 * kv + 2 + half)
                q = _stack_heads(q_ref, heads)
                do = _stack_heads(do_ref, heads)
                kk = _kv_block(kp_ref, kc_ref, kv)
                return heads, q, do, kk, _dot_nt(kk, q), _dot_nt(_kv_block(vp_ref, vc_ref, kv), do)

            nxt = first_products(0)
            dq_h, dk_kv, dv_kv = [], None, None
            for g in range(2 * N_KV):
                heads, q, do, kk, s_raw, dp_raw = nxt
                if g + 1 < 2 * N_KV:
                    nxt = first_products(g + 1)
                st = jnp.where(ok, s_raw, -1e30)
                sink = jnp.where(lane_q < QBLK, sink_ref[heads[0]], sink_ref[heads[1]])
                m = jnp.maximum(jnp.max(st, axis=0, keepdims=True), sink)
                e = jnp.exp(st - m)
                e_sink = jnp.exp(sink - m)
                inv = 1.0 / (jnp.sum(e, axis=0, keepdims=True) + e_sink)
                p = e * inv
                pdp = p * dp_raw
                delta = jnp.sum(pdp, axis=0, keepdims=True)
                ds = (pdp - p * delta).astype(BF16)
                sink_acc[g:g + 1, :] -= e_sink * inv * delta
                dk_g, dv_g = _dot(ds, q), _dot(p.astype(BF16), do)
                dk_kv = dk_g if dk_kv is None else dk_kv + dk_g
                dv_kv = dv_g if dv_kv is None else dv_kv + dv_g
                dq_h.append(_dot_tn(ds, kk))
                if g % 2 == 1:
                    kv = g // 2
                    for t in range(2):
                        dq_ref[:, LANES * (2 * kv + t):LANES * (2 * kv + t + 1)] = jnp.where(
                            low, dq_h[0][QBLK * t:QBLK * (t + 1)], dq_h[1][QBLK * t:QBLK * (t + 1)])
                    dk_parts.append(dk_kv + pltpu.roll(dk_kv, HEAD_DIM, 1))
                    dv_parts.append(dv_kv + pltpu.roll(dv_kv, HEAD_DIM, 1))
                    dq_h, dk_kv, dv_kv = [], None, None

            def order(parts, lo, hi):
                return jnp.concatenate([jnp.where(low, parts[0][lo:hi], parts[1][lo:hi]),
                                        jnp.where(low, parts[2][lo:hi], parts[3][lo:hi])], axis=1)

            dk_ref[...] = carry_k[...] + order(dk_parts, 0, QBLK)
            dv_ref[...] = (carry_v[...] + order(dv_parts, 0, QBLK)).astype(BF16)
            carry_k[...] = order(dk_parts, QBLK, 2 * QBLK)
            carry_v[...] = order(dv_parts, QBLK, 2 * QBLK)

        @pl.when(n == nb)
        def _():
            dk_ref[...] = carry_k[...]
            dv_ref[...] = carry_v[...].astype(BF16)
            lane = lax.broadcasted_iota(jnp.int32, (1, LANES), 1)
            out = jnp.zeros((1, LANES), F32)
            for g in range(2 * N_KV):
                for t in range(2):
                    tot = jnp.sum(sink_acc[g:g + 1, QBLK * t:QBLK * (t + 1)], axis=1, keepdims=True)
                    out = jnp.where(lane == 4 * (g // 2) + 2 * t + g % 2, tot, out)
            dsink_ref[...] = out

    cur = lambda w: pl.BlockSpec((QBLK, w), lambda n: (jnp.minimum(n, nb - 1), 0))
    prev = lambda w: pl.BlockSpec((QBLK, w), lambda n: (jnp.maximum(n - 1, 0), 0))
    return _call(
        body, name=f"attn_backward_{j}", grid=(nb + 1,),
        out_shape=(jax.ShapeDtypeStruct((seq, D), F32), jax.ShapeDtypeStruct((seq, N_KV * HEAD_DIM), F32),
                   jax.ShapeDtypeStruct((seq, N_KV * HEAD_DIM), BF16), jax.ShapeDtypeStruct((1, LANES), F32)),
        in_specs=[pl.BlockSpec(memory_space=pltpu.SMEM), cur(N_HEADS * LANES), cur(N_HEADS * LANES), prev(KX_W), cur(KX_W),
                  prev(KX_W), cur(KX_W)],
        out_specs=(cur(D), prev(N_KV * HEAD_DIM), prev(N_KV * HEAD_DIM), pl.BlockSpec((1, LANES), lambda n: (0, 0))),
        scratch_shapes=[pltpu.VMEM((QBLK, N_KV * HEAD_DIM), F32), pltpu.VMEM((QBLK, N_KV * HEAD_DIM), F32),
                        pltpu.VMEM((2 * N_KV, 2 * QBLK), F32)],
        compiler_params=_cparams(1),
    )(sinks, qs, dos, kd, kd, vd, vd)


def _in_proj_tail(x_ref, dxn_ref, ng_ref, mod_ref, w_ref, dproj, dx_ref, dw_acc, vec_acc):
    ng, sc, sh = ng_ref[...], mod_ref[1:2, :], mod_ref[0:1, :]
    xh, r, h = _norm_mod(x_ref[...], ng, sc, sh)
    dh = _dot(dproj, w_ref[...])
    dw_acc[...] += _dot_tn(dproj, h.astype(BF16))
    vec_acc[0:1, :] += jnp.sum(dh, axis=0, keepdims=True)
    vec_acc[1:2, :] += jnp.sum(dh * xh, axis=0, keepdims=True)
    dxh = dh * (ng * (1.0 + sc))
    dx_ref[...] = dxn_ref[...] + r * (dxh - xh * jnp.mean(dxh * xh, axis=-1, keepdims=True))


def _tail_finish(ng_ref, mod_ref, dw_ref, vec_ref, dw_acc, vec_acc):
    dw_ref[...] = dw_acc[...].astype(BF16)
    a = vec_acc[1:2, :]
    vec_ref[...] = jnp.zeros_like(vec_ref)
    vec_ref[0:1, :] = vec_acc[0:1, :]
    vec_ref[1:2, :] = a * ng_ref[...]
    vec_ref[3:4, :] = a * (1.0 + mod_ref[1:2, :])


def _attn_in_proj_bwd(x, dxn, rope, qk_raw, dq, dk, dv, dg, ng, mod, w_t, j, gain, bd, tile):
    seq = x.shape[0]
    steps = seq // tile

    def body(x_ref, dxn_ref, rope_ref, qk_ref, dq_ref, dk_ref, dv_ref, dg_ref, ng_ref, mod_ref, w_ref, gain_ref,
             bd_ref, dx_ref, dw_ref, vec_ref, dgain_ref, dproj, dw_acc, vec_acc):
        i = pl.program_id(0)

        @pl.when(i == 0)
        def _():
            dw_acc[...] = jnp.zeros_like(dw_acc)
            vec_acc[...] = jnp.zeros_like(vec_acc)
            dgain_ref[...] = jnp.zeros_like(dgain_ref)

        tabs = _rope_tabs(rope_ref)
        bdm = bd_ref[...]
        for b in range(QK_W // LANES):
            cols = slice(LANES * b, LANES * (b + 1))
            raw = qk_ref[:, cols]
            if b < D // LANES:
                dy = dq_ref[:, cols] * (HEAD_DIM ** -0.5)
            else:
                dy = dk_ref[:, LANES * (b - D // LANES):LANES * (b + 1 - D // LANES)]
            dy = _rope_bwd(dy, tabs)
            rr = lax.rsqrt(_dot_split(raw * raw, bdm) * (1.0 / HEAD_DIM) + NORM_EPS)
            xh = raw * rr
            dgain_ref[:, cols] += jnp.sum(dy * xh, axis=0, keepdims=True)
            dxh = dy * gain_ref[:, cols]
            dproj[:, cols] = (rr * (dxh - xh * (_dot_split(dxh * xh, bdm) * (1.0 / HEAD_DIM)))).astype(BF16)
        dproj[:, QK_W:QK_W + N_KV * HEAD_DIM] = dv_ref[...]
        dproj[:, QK_W + N_KV * HEAD_DIM:] = dg_ref[...]
        _in_proj_tail(x_ref, dxn_ref, ng_ref, mod_ref, w_ref, dproj[...], dx_ref, dw_acc, vec_acc)

        @pl.when(i == steps - 1)
        def _():
            _tail_finish(ng_ref, mod_ref, dw_ref, vec_ref, dw_acc, vec_acc)

    row = lambda w, dt=None: pl.BlockSpec((tile, w), lambda i: (i, 0))
    fixed = lambda shape: pl.BlockSpec(shape, lambda i: (0,) * len(shape))
    return _call(
        body, name=f"attn_in_proj_bwd_{j}", grid=(steps,),
        out_shape=(jax.ShapeDtypeStruct((seq, D), F32), jax.ShapeDtypeStruct((ATTN_IN, D), BF16),
                   jax.ShapeDtypeStruct((8, D), F32), jax.ShapeDtypeStruct((1, QK_W), F32)),
        in_specs=[row(D), row(D), row(3 * LANES), row(QK_W), row(D), row(N_KV * HEAD_DIM), row(N_KV * HEAD_DIM), row(D),
                  _const_spec((1, D)), _const_spec((8, D)), _const_spec((ATTN_IN, D)), _const_spec((1, QK_W)),
                  _const_spec((LANES, LANES))],
        out_specs=(row(D), fixed((ATTN_IN, D)), fixed((8, D)), fixed((1, QK_W))),
        scratch_shapes=[pltpu.VMEM((tile, ATTN_IN), BF16), pltpu.VMEM((ATTN_IN, D), F32), pltpu.VMEM((8, D), F32)],
        compiler_params=_cparams(1),
    )(x, dxn, rope, qk_raw, dq, dk, dv, dg, ng, mod, w_t, gain, bd)


def _pool_in_proj(x, ng, mod, w_t, j, tile):
    seq = x.shape[0]

    def body(x_ref, ng_ref, mod_ref, w_ref, v_ref, g_ref):
        _, _, h = _norm_mod(x_ref[...], ng_ref[...], mod_ref[1:2, :], mod_ref[0:1, :])
        proj = _dot_nt(h.astype(BF16), w_ref[...])
        v_ref[...] = proj[:, :D]
        g_ref[...] = proj[:, D:]

    row = pl.BlockSpec((tile, D), lambda i: (i, 0))
    return _call(
        body, name=f"pool_in_proj_{j}", grid=(seq // tile,),
        out_shape=(jax.ShapeDtypeStruct((seq, D), F32), jax.ShapeDtypeStruct((seq, D), F32)),
        in_specs=[row, _const_spec((1, D)), _const_spec((8, D)), _const_spec((POOL_IN, D))],
        out_specs=(row, row),
        compiler_params=_cparams(1),
    )(x, ng, mod, w_t)


PAD = 8


def _window_sums(ext, lo, hi, forward):
    gw = D // len(POOL_WINDOWS)
    planes = []
    for gi, w in enumerate(POOL_WINDOWS):
        cols = slice(gw * gi, gw * (gi + 1))
        src, k = 0, 1
        while k < w:
            d = k if forward else -k
            ext[1 - src, lo:hi, cols] = ext[src, lo:hi, cols] + ext[src, lo + d:hi + d, cols]
            src, k = 1 - src, 2 * k
        planes.append(src)
    return planes


def _pooled(ext, v_ref, first, tile):
    t_abs = first + lax.broadcasted_iota(jnp.int32, (tile, 1), 0)
    top = PAD + HALO
    planes = _window_sums(ext, PAD, top + tile, False)
    outs = []
    gw = D // len(POOL_WINDOWS)
    for gi, w in enumerate(POOL_WINDOWS):
        cols = slice(gw * gi, gw * (gi + 1))
        cnt = jnp.minimum(t_abs + 1, w).astype(F32)
        outs.append(ext[planes[gi], top:top + tile, cols] / cnt - v_ref[:, cols])
    return jnp.concatenate(outs, axis=1)


def _fill_ext(ext, halo_ref, v_ref, i, tile):
    ext[0, 0:PAD, :] = jnp.zeros((PAD, D), F32)
    ext[1, 0:PAD, :] = jnp.zeros((PAD, D), F32)
    ext[0, PAD:PAD + HALO, :] = jnp.where(i == 0, 0.0, halo_ref[...])
    ext[0, PAD + HALO:PAD + HALO + tile, :] = v_ref[...]


def _group_mix(pb, wg_ref):
    gw = D // len(POOL_WINDOWS)
    return jnp.concatenate([_dot(pb[:, gw * gi:gw * (gi + 1)], wg_ref[gi]) for gi in range(len(POOL_WINDOWS))], axis=1)


def _pool_mix_out(x, v, g, wg, w_out, j, scale, mod, tile, target=None):
    seq = x.shape[0]

    def body(*refs):
        if target is None:
            x_ref, v_ref, halo_ref, g_ref, wg_ref, w_ref, scale_ref, mod_ref, xo_ref, br_ref, ext = refs
        else:
            x_ref, v_ref, halo_ref, g_ref, wg_ref, w_ref, scale_ref, mod_ref, t_ref, xo_ref, br_ref, loss_ref, ext = refs
        i = pl.program_id(0)
        _fill_ext(ext, halo_ref, v_ref, i, tile)
        pb = _pooled(ext, v_ref, i * tile, tile).astype(BF16)
        ms = _group_mix(pb, wg_ref) * scale_ref[...]
        gv = g_ref[...]
        u = (ms * (gv * _sigmoid(gv))).astype(BF16)
        br = _dot(u, w_ref[...])
        br_ref[...] = br.astype(BF16)
        y = x_ref[...] + mod_ref[2:3, :] * br
        if target is None:
            xo_ref[...] = y
        else:
            @pl.when(i == 0)
            def _():
                loss_ref[...] = jnp.zeros_like(loss_ref)

            e = y - t_ref[...]
            xo_ref[...] = e * (1.0 / D)
            loss_ref[...] += 0.5 * jnp.sum(jnp.mean(e * e, axis=-1, keepdims=True), axis=0, keepdims=True)

    row = pl.BlockSpec((tile, D), lambda i: (i, 0))
    halo = pl.BlockSpec((HALO, D), lambda i: (jnp.maximum(i * (tile // HALO) - 1, 0), 0))
    extra_in, extra_out, extra_shape = ([], (), ()) if target is None else (
        [row], (pl.BlockSpec((1, LANES), lambda i: (0, 0)),), (jax.ShapeDtypeStruct((1, LANES), F32),))
    return _call(
        body, name=f"pool_mix_out_{j}", grid=(seq // tile,),
        out_shape=(jax.ShapeDtypeStruct((seq, D), F32), jax.ShapeDtypeStruct((seq, D), BF16)) + extra_shape,
        in_specs=[row, row, halo, row, _const_spec(wg.shape), _const_spec((D, D)), _const_spec((1, D)),
                  _const_spec((8, D))] + extra_in,
        out_specs=(row, row) + extra_out,
        scratch_shapes=[pltpu.VMEM((2, tile + HALO + PAD, D), F32)],
        compiler_params=_cparams(1),
    )(x, v, v, g, wg, w_out, scale, mod, *(() if target is None else (target,)))


def _pool_mix_out_bwd(dxn, br, v, g, wg, w_out, j, scale, mod, tile):
    seq = dxn.shape[0]
    steps = seq // tile
    ng_ = len(POOL_WINDOWS)
    gw = D // ng_

    def body(dxn_ref, br_ref, v_ref, halo_ref, g_ref, wg_ref, w_ref, scale_ref, mod_ref,
             dpool_ref, dg_ref, dw_ref, dwg_ref, vec_ref, ext, dw_acc, dwg_acc):
        i = pl.program_id(0)

        @pl.when(i == 0)
        def _():
            dw_acc[...] = jnp.zeros_like(dw_acc)
            dwg_acc[...] = jnp.zeros_like(dwg_acc)
            vec_ref[...] = jnp.zeros_like(vec_ref)

        _fill_ext(ext, halo_ref, v_ref, i, tile)
        pb = _pooled(ext, v_ref, i * tile, tile).astype(BF16)
        mixed = _group_mix(pb, wg_ref)
        scale = scale_ref[...]
        ms = mixed * scale
        gv, dxn_v = g_ref[...], dxn_ref[...]
        sg = _sigmoid(gv)
        sl = gv * sg
        vec_ref[0:1, :] += jnp.sum(dxn_v * br_ref[...].astype(F32), axis=0, keepdims=True)
        dbr = (dxn_v * mod_ref[2:3, :]).astype(BF16)
        du = _dot_nt(dbr, w_ref[...])
        dw_acc[...] += _dot_tn((ms * sl).astype(BF16), dbr)
        dms = du * sl
        dg_ref[...] = (du * ms * (sg * (1.0 + gv * (1.0 - sg)))).astype(BF16)
        vec_ref[1:2, :] += jnp.sum(dms * mixed, axis=0, keepdims=True)
        dmx = (dms * scale).astype(BF16)
        for gi in range(ng_):
            cols = slice(gw * gi, gw * (gi + 1))
            dpool_ref[:, cols] = _dot_nt(dmx[:, cols], wg_ref[gi])
            dwg_acc[gi] += _dot_tn(pb[:, cols], dmx[:, cols])

        @pl.when(i == steps - 1)
        def _():
            dw_ref[...] = dw_acc[...].astype(BF16)
            dwg_ref[...] = dwg_acc[...].astype(BF16)

    row = pl.BlockSpec((tile, D), lambda i: (i, 0))
    halo = pl.BlockSpec((HALO, D), lambda i: (jnp.maximum(i * (tile // HALO) - 1, 0), 0))
    fixed = lambda shape: pl.BlockSpec(shape, lambda i: (0,) * len(shape))
    return _call(
        body, name=f"pool_mix_out_bwd_{j}", grid=(steps,),
        out_shape=(jax.ShapeDtypeStruct((seq, D), F32), jax.ShapeDtypeStruct((seq, D), BF16),
                   jax.ShapeDtypeStruct((D, D), BF16), jax.ShapeDtypeStruct((ng_, gw, gw), BF16),
                   jax.ShapeDtypeStruct((8, D), F32)),
        in_specs=[row, row, row, halo, row, _const_spec(wg.shape), _const_spec((D, D)), _const_spec((1, D)),
                  _const_spec((8, D))],
        out_specs=(row, row, fixed((D, D)), fixed((ng_, gw, gw)), fixed((8, D))),
        scratch_shapes=[pltpu.VMEM((2, tile + HALO + PAD, D), F32), pltpu.VMEM((D, D), F32), pltpu.VMEM((ng_, gw, gw), F32)],
        compiler_params=_cparams(1),
    )(dxn, br, v, v, g, wg, w_out, scale, mod)


def _pool_in_proj_bwd(x, dxn, dpool, dg, ng, mod, w_t, j, tile):
    seq = x.shape[0]
    steps = seq // tile
    gw = D // len(POOL_WINDOWS)

    def body(x_ref, dxn_ref, dp_ref, halo_ref, dg_ref, ng_ref, mod_ref, w_ref, dx_ref, dw_ref, vec_ref,
             ext, dproj, dw_acc, vec_acc):
        i = pl.program_id(0)

        @pl.when(i == 0)
        def _():
            dw_acc[...] = jnp.zeros_like(dw_acc)
            vec_acc[...] = jnp.zeros_like(vec_acc)

        t_abs = i * tile + lax.broadcasted_iota(jnp.int32, (tile, 1), 0)
        last = i == steps - 1
        ext[0, tile + HALO:tile + HALO + PAD, :] = jnp.zeros((PAD, D), F32)
        ext[1, tile + HALO:tile + HALO + PAD, :] = jnp.zeros((PAD, D), F32)
        for gi, w in enumerate(POOL_WINDOWS):
            cols = slice(gw * gi, gw * (gi + 1))
            cnt = jnp.minimum(t_abs + 1, w).astype(F32)
            ext[0, 0:tile, cols] = dp_ref[:, cols] / cnt
            ext[0, tile:tile + HALO, cols] = jnp.where(last, 0.0, halo_ref[:, cols] * (1.0 / w))
        planes = _window_sums(ext, 0, tile + HALO, True)
        for gi, w in enumerate(POOL_WINDOWS):
            cols = slice(gw * gi, gw * (gi + 1))
            dproj[:, cols] = (ext[planes[gi], 0:tile, cols] - dp_ref[:, cols]).astype(BF16)
        dproj[:, D:] = dg_ref[...]
        _in_proj_tail(x_ref, dxn_ref, ng_ref, mod_ref, w_ref, dproj[...], dx_ref, dw_acc, vec_acc)

        @pl.when(last)
        def _():
            _tail_finish(ng_ref, mod_ref, dw_ref, vec_ref, dw_acc, vec_acc)

    row = pl.BlockSpec((tile, D), lambda i: (i, 0))
    halo = pl.BlockSpec((HALO, D), lambda i: (jnp.minimum((i + 1) * (tile // HALO), seq // HALO - 1), 0))
    fixed = lambda shape: pl.BlockSpec(shape, lambda i: (0,) * len(shape))
    return _call(
        body, name=f"pool_in_proj_bwd_{j}", grid=(steps,),
        out_shape=(jax.ShapeDtypeStruct((seq, D), F32), jax.ShapeDtypeStruct((POOL_IN, D), BF16),
                   jax.ShapeDtypeStruct((8, D), F32)),
        in_specs=[row, row, row, halo, row, _const_spec((1, D)), _const_spec((8, D)), _const_spec((POOL_IN, D))],
        out_specs=(row, fixed((POOL_IN, D)), fixed((8, D))),
        scratch_shapes=[pltpu.VMEM((2, tile + HALO + PAD, D), F32), pltpu.VMEM((tile, POOL_IN), BF16), pltpu.VMEM((POOL_IN, D), F32),
                        pltpu.VMEM((8, D), F32)],
        compiler_params=_cparams(1),
    )(x, dxn, dpool, dpool, dg, ng, mod, w_t)


def _build_vec(vecs, gates, pool_vecs, gains, dsinks, loss_part):
    def body(v0, v1, v2, v3, g0, g2, p0, p1, n0, n1, s0, s1, loss_ref, out):
        out[...] = jnp.zeros_like(out)
        for i, v in enumerate((v0, v1, v2, v3)):
            out[3 * i:3 * i + 2, :] = v[0:2, :]
            out[12 + i:13 + i, :] = v[3:4, :]
        out[2:3, :] = g0[...]
        out[8:9, :] = g2[...]
        for j, (p, n, s) in enumerate(((p0, n0, s0), (p1, n1, s1))):
            out[3 * (2 * j + 1) + 2:3 * (2 * j + 1) + 3, :] = p[0:1, :]
            out[22 + j:23 + j, :] = p[1:2, :]
            out[16 + j:17 + j, :] = n[:, 0:D]
            out[18 + j:19 + j, 0:QK_W - D] = n[:, D:QK_W]
            out[20 + j:21 + j, 0:LANES] = s[...]
        out[24:25, 0:LANES] = loss_ref[...]

    vm = pl.BlockSpec(memory_space=pltpu.VMEM)
    args = (*vecs, gates[0], gates[2], *pool_vecs, *gains, *dsinks, loss_part)
    return _call(
        body, name="build_vec",
        out_shape=jax.ShapeDtypeStruct((VEC_ROWS, D), F32),
        in_specs=[vm] * len(args), out_specs=vm,
        compiler_params=_cparams(),
    )(*args)


def _sum_devices(g, after):
    rows = g.shape[1]

    def body(g_ref, after_ref, tot_ref, fold_ref):
        tot = g_ref[0]
        for p in range(1, N_DEV):
            tot = tot + g_ref[p]
        tot_ref[...] = tot
        f = tot[16:24, 0:LANES]
        for b in range(1, D // LANES):
            f = f + tot[16:24, LANES * b:LANES * (b + 1)]
        fold_ref[...] = f + pltpu.roll(f, HEAD_DIM, 1)

    return _call(
        body, name="sum_devices",
        out_shape=(jax.ShapeDtypeStruct((rows, D), F32), jax.ShapeDtypeStruct((8, LANES), F32)),
        in_specs=[pl.BlockSpec(memory_space=pltpu.VMEM), ANY_SPEC],
        out_specs=(pl.BlockSpec(memory_space=pltpu.VMEM), pl.BlockSpec(memory_space=pltpu.VMEM)),
        compiler_params=_cparams(),
    )(g, after)


def _adamw_small(params):
    n = len(params)

    def body(*refs):
        ins, outs = refs[:4 * n], refs[4 * n:]
        for p in range(n):
            w_ref, g_ref, m_ref, v_ref = ins[4 * p:4 * p + 4]
            outs[3 * p][...], outs[3 * p + 1][...], outs[3 * p + 2][...] = _adamw(w_ref[...], g_ref[...], m_ref[...], v_ref[...])

    vm = pl.BlockSpec(memory_space=pltpu.VMEM)
    out = _call(
        body, name="adamw_small",
        out_shape=tuple(jax.ShapeDtypeStruct(w.shape, F32) for (w, _, _, _) in params for _ in range(3)),
        in_specs=[vm] * (4 * n), out_specs=tuple([vm] * (3 * n)),
        compiler_params=_cparams(),
    )(*[a for p in params for a in p])
    return [tuple(out[3 * p:3 * p + 3]) for p in range(n)]


def _adamw_shards(name, me, fulls, lands, w, m, v, transpose, axis=0):
    nl = w.shape[0]
    wshape = w.shape[1:]
    own_shape = lands[0].shape[1:]

    def body(me_ref, *refs):
        own_refs, land_refs = refs[:nl], refs[nl:2 * nl]
        w_ref, m_ref, v_ref, g_out, d_out, m_out, v_out = refs[2 * nl:]
        layer = pl.program_id(0)
        for l in range(nl):
            @pl.when(layer == l)
            def _(l=l):
                g = own_refs[l][...].astype(F32)
                for k in range(N_DEV - 1):
                    g = g + land_refs[l][k].astype(F32)
                if transpose:
                    g = g.T
                g_out[...] = g
                d_out[...], m_out[...], v_out[...] = _adamw(w_ref[...], g, m_ref[...], v_ref[...])

    def own_index(l_, me_ref):
        idx = [0] * len(own_shape)
        idx[axis] = me_ref[0]
        return tuple(idx)

    own_spec = pl.BlockSpec(tuple(own_shape), own_index)
    land_spec = pl.BlockSpec((N_DEV - 1,) + tuple(own_shape), lambda l_, me_ref: (0,) * (1 + len(own_shape)))
    wspec = pl.BlockSpec((None,) + tuple(wshape), lambda l_, me_ref: (l_,) + (0,) * len(wshape))
    return _call(
        body, name=name,
        grid_spec=pltpu.PrefetchScalarGridSpec(num_scalar_prefetch=1, grid=(nl,),
                                               in_specs=[own_spec] * nl + [land_spec] * nl + [wspec] * 3,
                                               out_specs=(wspec,) * 4),
        out_shape=tuple(jax.ShapeDtypeStruct(w.shape, F32) for _ in range(4)),
        compiler_params=_cparams(1),
    )(me.reshape(1), *fulls, *lands, w, m, v)


def _constants():
    lane = np.arange(LANES)
    bd = (lane[:, None] // HEAD_DIM == lane[None, :] // HEAD_DIM).astype(np.float32)
    half = ROT_DIM // 2
    inv_freq = ROPE_THETA ** (-jnp.arange(half, dtype=F32) * 2.0 / ROT_DIM)
    invf = jnp.tile(inv_freq, LANES // half).reshape(1, LANES)
    return jnp.asarray(bd, BF16), invf


def kernel(x, c, positions, ada_w, ada_b, norm_g, attn_w_in, attn_q_norm, attn_k_norm, attn_sinks, attn_w_out, pool_w_in, pool_w_group, pool_scale, pool_w_out, loss_target, m_ada_w, m_ada_b, m_norm_g, m_attn_w_in, m_attn_q_norm, m_attn_k_norm, m_attn_sinks, m_attn_w_out, m_pool_w_in, m_pool_w_group, m_pool_scale, m_pool_w_out, v_ada_w, v_ada_b, v_norm_g, v_attn_w_in, v_attn_q_norm, v_attn_k_norm, v_attn_sinks, v_attn_w_out, v_pool_w_in, v_pool_w_group, v_pool_scale, v_pool_w_out):
    seq = x.shape[1]
    me = 4 * lax.axis_index("x") + 2 * lax.axis_index("y") + lax.axis_index("c")
    bd, invf = _constants()
    t_mm = min(512, seq)
    rope = _rope_table(positions.reshape(seq, 1), invf, t_mm)
    t_bw = min(256, seq)
    shard = pool_scale.shape[1]
    cols = ada_w.shape[2]

    layers = _prep_weights(me, attn_w_in, attn_w_out, pool_w_in, pool_w_out, pool_w_group)
    first_w, token = _gather_first_start(layers[0][0], c)

    first = jnp.concatenate([c, jnp.pad(pool_scale, ((0, 0), (0, D - shard))), jnp.zeros((5, D), F32)], axis=0)
    first = _allgather_small(first + token[0, 0], "allgather_c", rope)
    c_all = first[:, 0, :]
    scale_full = jnp.transpose(first[:, 1:3, :shard], (1, 0, 2)).reshape(2, D)
    mod_part = _ada_forward(c_all, ada_w)
    mod_all = _allgather_small(mod_part.reshape(DEPTH * N_DEV, cols), "allgather_mod", c_all)
    mod_all = mod_all.reshape(N_DEV, DEPTH, N_DEV, cols)
    mine = lax.dynamic_index_in_dim(mod_all, me, axis=2, keepdims=False)
    mod = jnp.transpose(mine, (1, 0, 2)).reshape(DEPTH, 3 * D) + ada_b
    mod = jnp.pad(mod.reshape(DEPTH, 3, D), ((0, 0), (0, 5), (0, 0)))

    groups = [[layers[0][1]], layers[0][2:5], layers[1][0:2], layers[1][2:5]]
    gaxes = [(0,), (0,), (0, 0, 1), (0, 0), (0, 0, 1)]
    first_w, token = _gather_first_forward(first_w, mod)
    rest, token = _gather_start(groups, gaxes[1:], token, "gather_start_rest")
    started = [None] + rest

    saved, weights = [], []
    h = x[0]
    for i in range(DEPTH):
        j = i // 2
        s = dict(x=h, ng=norm_g[i:i + 1], md=mod[i])
        if i == 0:
            w_in_t = _gather_first_wait(first_w, token)
        else:
            wts = _gather_wait(started[i + 1], gaxes[i + 1], h, f"gather_wait_{i}")
        if i % 2 == 0:
            if i > 0:
                w_in_t, w_out = wts
            s["gain"] = jnp.concatenate([jnp.tile(attn_q_norm[j], N_HEADS), jnp.tile(attn_k_norm[j], N_KV)]).reshape(1, QK_W)
            s["qk_raw"], s["qs"], s["kd"], s["vd"], s["g"] = _attn_in_proj(
                h, rope, s["ng"], s["md"], w_in_t, j, s["gain"], bd, t_bw)
            s["o"] = _attn_forward(attn_sinks[j], s["qs"], s["kd"], s["vd"], j)
            if i == 0:
                w_out, = _gather_wait(started[1], gaxes[1], s["o"], "gather_wait_0_out")
            h, s["br"] = _attn_out_proj(h, s["o"], s["g"], w_out, j, s["md"], t_mm)
            weights.append((w_in_t, w_out))
        else:
            p_in_t, p_out, p_grp = wts
            s["scale"] = scale_full[j:j + 1]
            s["v"], s["g"] = _pool_in_proj(h, s["ng"], s["md"], p_in_t, j, t_mm)
            if i < DEPTH - 1:
                h, s["br"] = _pool_mix_out(h, s["v"], s["g"], p_grp, p_out, j, s["scale"], s["md"], t_mm)
            else:
                dx, s["br"], loss_part = _pool_mix_out(h, s["v"], s["g"], p_grp, p_out, j, s["scale"], s["md"], t_mm,
                                                       loss_target[0])
            weights.append(wts)
        saved.append(s)

    vecs, gates, gains, dsinks, pool_vecs = [None] * DEPTH, [None] * DEPTH, [None] * 2, [None] * 2, [None] * 2
    sent_in, sent_out = [None] * DEPTH, [None] * DEPTH
    token = jnp.zeros((8, LANES), F32)
    for i in reversed(range(DEPTH)):
        j = i // 2
        s = saved[i]
        md = s["md"] + token[0, 0]
        if i % 2 == 0:
            w_in_t, w_out = weights[i]
            dos, dg, d_w_out, gates[i] = _attn_out_proj_bwd(dx, s["br"], s["o"], s["g"], w_out, j, md, t_mm)
            sent_out[i], token = _scatter_start([d_w_out], (0,), f"scatter_start_{i}_out", token)
            dq, dk, dv, dsinks[j] = _attn_backward(attn_sinks[j] + token[0, 0], s["qs"], dos, s["kd"], s["vd"], j)
            dx, d_in_t, vecs[i], gains[j] = _attn_in_proj_bwd(
                s["x"], dx, rope, s["qk_raw"], dq, dk, dv, dg, s["ng"], md, w_in_t, j, s["gain"], bd, t_bw)
        else:
            p_in_t, p_out, p_grp = weights[i]
            dpool, dg, d_p_out, d_p_grp, pool_vecs[j] = _pool_mix_out_bwd(
                dx, s["br"], s["v"], s["g"], p_grp, p_out, j, s["scale"], md, t_mm)
            sent_out[i], token = _scatter_start([d_p_out, d_p_grp], (0, 1), f"scatter_start_{i}_out", token)
            dx, d_in_t, vecs[i] = _pool_in_proj_bwd(s["x"], dx, dpool, dg, s["ng"], s["md"] + token[0, 0], p_in_t, j, t_bw)
        if i > 0:
            sent_in[i], token = _scatter_start([d_in_t], (0,), f"scatter_start_{i}_in", token)

    vec = _build_vec(vecs, gates, pool_vecs, gains, dsinks, loss_part)
    vec_all = _allgather_small(vec, "allgather_vec", loss_part)
    sent_in[0], token = _scatter_start([d_in_t], (0,), "scatter_start_0_in", vec_all)
    tot, folded = _sum_devices(vec_all, token)
    loss = tot[24, 0]
    small = dict(
        ada_b=(ada_b, tot[0:12].reshape(DEPTH, 3 * D), m_ada_b, v_ada_b),
        norm_g=(norm_g, tot[12:16], m_norm_g, v_norm_g),
        q_norm=(attn_q_norm, folded[0:2, :HEAD_DIM], m_attn_q_norm, v_attn_q_norm),
        k_norm=(attn_k_norm, folded[2:4, :HEAD_DIM], m_attn_k_norm, v_attn_k_norm),
        sinks=(attn_sinks, tot[20:22, :N_HEADS], m_attn_sinks, v_attn_sinks),
        pool_scale=(pool_scale, lax.dynamic_slice(tot, (22, me * shard), (2, shard)), m_pool_scale, v_pool_scale),
    )
    res = {k: (a[1],) + upd for (k, a), upd in zip(small.items(), _adamw_small(list(small.values())))}

    dmod_all = vec_all[:, 0:12, :].reshape(N_DEV, DEPTH, 3 * D)
    dmod_mine = lax.dynamic_slice_in_dim(dmod_all, me * cols, cols, axis=2)
    dmod_mine = jnp.pad(jnp.transpose(dmod_mine, (1, 0, 2)), ((0, 0), (0, N_DEV), (0, 0))) + token[0, 0]
    res["ada_w"] = _ada_backward_adamw(jnp.pad(c_all, ((0, N_DEV), (0, 0))), dmod_mine, ada_w, m_ada_w, v_ada_w)

    got_in, got_out = [None] * DEPTH, [None] * DEPTH
    for i in (3, 1):
        got_out[i] = _scatter_wait(sent_out[i], (0, 1), res["ada_w"][0], f"scatter_wait_{i}_out")
        got_in[i] = _scatter_wait(sent_in[i], (0,), res["ada_w"][0], f"scatter_wait_{i}_in")
    pick = lambda got, ls, a: ([got[i][0][a] for i in ls], [got[i][1][a] for i in ls])
    res["pool_w_in"] = _adamw_shards("adamw_pool_w_in", me, *pick(got_in, (1, 3), 0), pool_w_in, m_pool_w_in, v_pool_w_in, True)
    res["pool_w_out"] = _adamw_shards("adamw_pool_w_out", me, *pick(got_out, (1, 3), 0), pool_w_out, m_pool_w_out,
                                      v_pool_w_out, False)
    res["pool_w_group"] = _adamw_shards("adamw_pool_w_group", me, *pick(got_out, (1, 3), 1), pool_w_group, m_pool_w_group,
                                        v_pool_w_group, False, axis=1)
    for i in (2, 0):
        got_out[i] = _scatter_wait(sent_out[i], (0,), res["pool_w_group"][0], f"scatter_wait_{i}_out")
        got_in[i] = _scatter_wait(sent_in[i], (0,), res["pool_w_group"][0], f"scatter_wait_{i}_in")
    res["attn_w_out"] = _adamw_shards("adamw_attn_w_out", me, *pick(got_out, (0, 2), 0), attn_w_out, m_attn_w_out,
                                      v_attn_w_out, False)
    res["attn_w_in"] = _adamw_shards("adamw_attn_w_in", me, *pick(got_in, (0, 2), 0), attn_w_in, m_attn_w_in, v_attn_w_in, True)

    order = ("ada_w", "ada_b", "norm_g", "attn_w_in", "q_norm", "k_norm", "sinks", "attn_w_out", "pool_w_in",
             "pool_w_group", "pool_scale", "pool_w_out")
    return (loss, dx[None], *[res[k][0] for k in order], *[res[k][1] for k in order], *[res[k][2] for k in order],
            *[res[k][3] for k in order])
```

```python
import functools

import numpy as np
import jax
import jax.numpy as jnp
from jax import lax
from jax.experimental import pallas as pl
from jax.experimental.pallas import tpu as pltpu

F32 = jnp.float32
BF16 = jnp.bfloat16
MESH = pl.DeviceIdType.MESH

N_DEV = 8
D = 1024
DEPTH = 4
HEAD_DIM = 64
N_HEADS = 16
N_KV = 4
QK_W = 1280
ATTN_IN = 2560
POOL_IN = 2048
QBLK = 128
KX_W = N_KV * 128
CHUNK = 256
POOL_WINDOWS = (2, 4, 8, 16)
HALO = 16
ROPE_THETA = 500000.0
ROT_DIM = 16
NORM_EPS = 1e-6
ADAM_LR = 0.001
ADAM_B1 = 0.9
ADAM_B2 = 0.999
ADAM_EPS = 1e-08
ADAM_WD = 0.01
ADAM_STEP = 10

LANES = 128
VMEM_LIMIT = 56 * 2**20
VEC_ROWS = 32


def _cparams(n_grid=0, **kw):
    if n_grid:
        kw["dimension_semantics"] = ("arbitrary",) * n_grid
    return pltpu.CompilerParams(vmem_limit_bytes=VMEM_LIMIT, **kw)


def _call(body, **kw):
    return pl.pallas_call(body, **kw)


def _const_spec(shape):
    nd = len(shape)
    return pl.BlockSpec(shape, lambda *_: (0,) * nd, pipeline_mode=pl.Buffered(1))


def _dot(a, b):
    return jnp.dot(a, b, preferred_element_type=F32)


def _dot_nt(a, b):
    return lax.dot_general(a, b, (((1,), (1,)), ((), ())), preferred_element_type=F32)


def _dot_tn(a, b):
    return lax.dot_general(a, b, (((0,), (0,)), ((), ())), preferred_element_type=F32)


def _dot_split(x, m):
    hi = x.astype(BF16)
    lo = (x - hi.astype(F32)).astype(BF16)
    return _dot(hi, m) + _dot(lo, m)


def _sigmoid(g):
    return 1.0 / (1.0 + jnp.exp(-g))


def _norm_mod(x, ng, sc, sh):
    r = lax.rsqrt(jnp.mean(x * x, axis=-1, keepdims=True) + NORM_EPS)
    xh = x * r
    h = (xh * ng) * (1.0 + sc) + sh
    return xh, r, h


def _rope_table(pos_col, invf_row, tile):
    seq = pos_col.shape[0]

    def body(pos_ref, invf_ref, out_ref):
        ang = pos_ref[...].astype(F32) * invf_ref[...]
        l64 = lax.broadcasted_iota(jnp.int32, (tile, LANES), 1) & (HEAD_DIM - 1)
        cs, sn = jnp.cos(ang), jnp.sin(ang)
        out_ref[:, 0:LANES] = jnp.where(l64 < ROT_DIM, cs, 1.0)
        out_ref[:, LANES:2 * LANES] = jnp.where(l64 < ROT_DIM // 2, -sn, 0.0)
        out_ref[:, 2 * LANES:3 * LANES] = jnp.where((l64 >= ROT_DIM // 2) & (l64 < ROT_DIM), sn, 0.0)

    return _call(
        body, name="rope_table", grid=(seq // tile,),
        out_shape=jax.ShapeDtypeStruct((seq, 3 * LANES), F32),
        in_specs=[pl.BlockSpec((tile, 1), lambda i: (i, 0)), _const_spec((1, LANES))],
        out_specs=pl.BlockSpec((tile, 3 * LANES), lambda i: (i, 0)),
        compiler_params=_cparams(1),
    )(pos_col, invf_row)


def _rope_tabs(rope_ref):
    return rope_ref[:, 0:LANES], rope_ref[:, LANES:2 * LANES], rope_ref[:, 2 * LANES:3 * LANES]


def _rope(y, tabs):
    cos_t, sin_a, sin_b = tabs
    return y * cos_t + pltpu.roll(y, LANES - ROT_DIM // 2, 1) * sin_a + pltpu.roll(y, ROT_DIM // 2, 1) * sin_b


def _rope_bwd(dy, tabs):
    cos_t, sin_a, sin_b = tabs
    return dy * cos_t + pltpu.roll(dy * sin_a, ROT_DIM // 2, 1) + pltpu.roll(dy * sin_b, LANES - ROT_DIM // 2, 1)


def _low_half(rows):
    return lax.broadcasted_iota(jnp.int32, (rows, LANES), 1) < HEAD_DIM


def _adamw(w, g, m, v):
    m = ADAM_B1 * m + (1.0 - ADAM_B1) * g
    v = ADAM_B2 * v + (1.0 - ADAM_B2) * (g * g)
    m_hat = m / (1.0 - ADAM_B1 ** ADAM_STEP)
    v_hat = v / (1.0 - ADAM_B2 ** ADAM_STEP)
    delta = -ADAM_LR * (m_hat / (jnp.sqrt(v_hat) + ADAM_EPS) + ADAM_WD * w)
    return delta, m, v


def _my_position():
    x, y, c = lax.axis_index("x"), lax.axis_index("y"), lax.axis_index("c")
    return x, y, c, 4 * x + 2 * y + c


def _peers(x, y, c):
    out = []
    for k in range(1, N_DEV):
        px = 1 - x if k & 4 else x
        py = 1 - y if k & 2 else y
        pc = 1 - c if k & 1 else c
        out.append(((px, py, pc), 4 * px + 2 * py + pc))
    return out


def _allgather_small(v, name, after):
    rows, cols = v.shape

    def body(v_ref, after_ref, out_ref, send_sems, recv_sems, local_sem):
        x, y, c, me = _my_position()
        local = pltpu.make_async_copy(v_ref, out_ref.at[me], local_sem)
        local.start()
        sends = []
        for k, (peer, _) in enumerate(_peers(x, y, c)):
            cp = pltpu.make_async_remote_copy(v_ref, out_ref.at[me], send_sems.at[k], recv_sems.at[k],
                                              device_id=peer, device_id_type=MESH)
            cp.start()
            sends.append(cp)
        for k, (peer, idx) in enumerate(_peers(x, y, c)):
            pltpu.make_async_remote_copy(v_ref, out_ref.at[idx], send_sems.at[k], recv_sems.at[k],
                                         device_id=peer, device_id_type=MESH).wait_recv()
        for cp in sends:
            cp.wait_send()
        local.wait()

    return _call(
        body, name=name,
        out_shape=jax.ShapeDtypeStruct((N_DEV, rows, cols), F32),
        in_specs=[pl.BlockSpec(memory_space=pltpu.VMEM), pl.BlockSpec(memory_space=pl.ANY)],
        out_specs=pl.BlockSpec(memory_space=pltpu.VMEM),
        scratch_shapes=[pltpu.SemaphoreType.DMA((N_DEV - 1,)), pltpu.SemaphoreType.DMA((N_DEV - 1,)),
                        pltpu.SemaphoreType.DMA(())],
        compiler_params=_cparams(),
    )(v, after)


def _shard_rows(ref, idx, rows, axis):
    sl = [slice(None)] * len(ref.shape)
    sl[axis] = pl.ds(idx * rows, rows)
    return ref.at[tuple(sl)]


def _own_and_peer_rows(ref, me, idx, axis):
    rows = ref.shape[axis] // N_DEV
    return _shard_rows(ref, me, rows, axis), _shard_rows(ref, idx, rows, axis)


HBM_SPEC = pl.BlockSpec(memory_space=pltpu.HBM)
SEM_SPEC = pl.BlockSpec(memory_space=pltpu.SEMAPHORE)
ANY_SPEC = pl.BlockSpec(memory_space=pl.ANY)
DATAFLOW = pltpu.SideEffectType.DATAFLOW_SIDE_EFFECTING


def _hbm(a):
    return pltpu.with_memory_space_constraint(a, pltpu.HBM)


def _gather_start(layers, axes, after, name):
    flat = [a for arrs in layers for a in arrs]
    flat_axes = [ax for axs in axes for ax in axs]
    n, nl = len(flat), len(layers)

    def body(*refs):
        ins, sems, token = refs[:n], refs[n + 1:n + 1 + 2 * nl], refs[-1]
        x, y, c, me = _my_position()
        a0 = 0
        for li, arrs in enumerate(layers):
            for k, (peer, _) in enumerate(_peers(x, y, c)):
                for a in range(len(arrs)):
                    rows, _ = _own_and_peer_rows(ins[a0 + a], me, me, flat_axes[a0 + a])
                    pltpu.make_async_remote_copy(rows, rows, sems[2 * li].at[k * len(arrs) + a],
                                                 sems[2 * li + 1].at[k * len(arrs) + a],
                                                 device_id=peer, device_id_type=MESH).start()
            a0 += len(arrs)
        token[...] = jnp.zeros_like(token)

    sem_shapes = []
    for arrs in layers:
        sem_shapes += [pltpu.SemaphoreType.DMA(((N_DEV - 1) * len(arrs),))] * 2
    out = _call(
        body, name=name,
        out_shape=(*sem_shapes, *[pltpu.HBM(a.shape, a.dtype) for a in flat], jax.ShapeDtypeStruct((8, LANES), F32)),
        in_specs=[HBM_SPEC] * n + [ANY_SPEC],
        out_specs=(*[SEM_SPEC] * (2 * nl), *[HBM_SPEC] * n, pl.BlockSpec(memory_space=pltpu.VMEM)),
        input_output_aliases={a: 2 * nl + a for a in range(n)},
        compiler_params=_cparams(has_side_effects=DATAFLOW),
    )(*[_hbm(a) for a in flat], after)
    per_layer, a0 = [], 0
    for li, arrs in enumerate(layers):
        per_layer.append((out[2 * li], out[2 * li + 1], list(out[2 * nl + a0:2 * nl + a0 + len(arrs)])))
        a0 += len(arrs)
    return per_layer, out[-1]


def _gather_wait(started, axes, after, name):
    send_sems, recv_sems, arrs = started
    n = len(arrs)

    def body(*refs):
        ins, send_ref, recv_ref = refs[:n], refs[n], refs[n + 1]
        x, y, c, me = _my_position()
        for k, (peer, idx) in enumerate(_peers(x, y, c)):
            for a in range(n):
                own, theirs = _own_and_peer_rows(ins[a], me, idx, axes[a])
                cp = pltpu.make_async_remote_copy(own, theirs, send_ref.at[k * n + a], recv_ref.at[k * n + a],
                                                  device_id=peer, device_id_type=MESH)
                cp.wait_send()
                cp.wait_recv()

    return _call(
        body, name=name,
        out_shape=tuple(pltpu.HBM(a.shape, a.dtype) for a in arrs),
        in_specs=[HBM_SPEC] * n + [SEM_SPEC, SEM_SPEC, ANY_SPEC],
        out_specs=tuple([HBM_SPEC] * n),
        input_output_aliases={a: a for a in range(n)},
        compiler_params=_cparams(has_side_effects=DATAFLOW),
    )(*arrs, send_sems, recv_sems, after)


def _first_relations(x, y, c):
    return [(x, y, 1 - c), (1 - x, y, c), (x, 1 - y, c), (1 - x, 1 - y, c)]


def _gather_first_start(arr, after):
    n_rel = 4

    def body(a_ref, after_ref, send_ref, recv_ref, thru, token):
        x, y, c, me = _my_position()
        rows, _ = _own_and_peer_rows(a_ref, me, me, 0)
        for k, peer in enumerate(_first_relations(x, y, c)):
            pltpu.make_async_remote_copy(rows, rows, send_ref.at[k], recv_ref.at[k], device_id=peer, device_id_type=MESH).start()
        token[...] = jnp.zeros_like(token)

    sem = pltpu.SemaphoreType.DMA((n_rel,))
    out = _call(
        body, name="gather_first_start",
        out_shape=(sem, sem, pltpu.HBM(arr.shape, arr.dtype), jax.ShapeDtypeStruct((8, LANES), F32)),
        in_specs=[HBM_SPEC, ANY_SPEC],
        out_specs=(SEM_SPEC, SEM_SPEC, HBM_SPEC, pl.BlockSpec(memory_space=pltpu.VMEM)),
        input_output_aliases={0: 2},
        compiler_params=_cparams(has_side_effects=DATAFLOW),
    )(_hbm(arr), after)
    return out[:3], out[3]


def _gather_first_forward(started, after):
    send_a, recv_a, arr = started

    def body(a_ref, send_a_ref, recv_a_ref, after_ref, send_b_ref, recv_b_ref, thru, token):
        x, y, c, me = _my_position()
        sibling = (x, y, 1 - c)
        for k, peer in enumerate(_first_relations(x, y, c)):
            own, theirs = _own_and_peer_rows(a_ref, me, 4 * peer[0] + 2 * peer[1] + peer[2], 0)
            cp = pltpu.make_async_remote_copy(own, theirs, send_a_ref.at[k], recv_a_ref.at[k], device_id=peer, device_id_type=MESH)
            cp.wait_send()
            cp.wait_recv()
            if k > 0:
                pltpu.make_async_remote_copy(theirs, theirs, send_b_ref.at[k - 1], recv_b_ref.at[k - 1],
                                             device_id=sibling, device_id_type=MESH).start()
        token[...] = jnp.zeros_like(token)

    sem = pltpu.SemaphoreType.DMA((3,))
    out = _call(
        body, name="gather_first_forward",
        out_shape=(sem, sem, pltpu.HBM(arr.shape, arr.dtype), jax.ShapeDtypeStruct((8, LANES), F32)),
        in_specs=[HBM_SPEC, SEM_SPEC, SEM_SPEC, ANY_SPEC],
        out_specs=(SEM_SPEC, SEM_SPEC, HBM_SPEC, pl.BlockSpec(memory_space=pltpu.VMEM)),
        input_output_aliases={0: 2},
        compiler_params=_cparams(has_side_effects=DATAFLOW),
    )(arr, send_a, recv_a, after)
    return out[:3], out[3]


def _gather_first_wait(forwarded, after):
    send_b, recv_b, arr = forwarded

    def body(a_ref, send_b_ref, recv_b_ref, after_ref, thru):
        x, y, c, me = _my_position()
        sibling = (x, y, 1 - c)
        for k, peer in enumerate(_first_relations(x, y, c)[1:]):
            _, sent = _own_and_peer_rows(a_ref, me, 4 * peer[0] + 2 * peer[1] + peer[2], 0)
            _, got = _own_and_peer_rows(a_ref, me, 4 * peer[0] + 2 * peer[1] + (1 - peer[2]), 0)
            cp = pltpu.make_async_remote_copy(sent, got, send_b_ref.at[k], recv_b_ref.at[k], device_id=sibling, device_id_type=MESH)
            cp.wait_send()
            cp.wait_recv()

    return _call(
        body, name="gather_first_wait",
        out_shape=pltpu.HBM(arr.shape, arr.dtype),
        in_specs=[HBM_SPEC, SEM_SPEC, SEM_SPEC, ANY_SPEC],
        out_specs=HBM_SPEC,
        input_output_aliases={0: 0},
        compiler_params=_cparams(has_side_effects=DATAFLOW),
    )(arr, send_b, recv_b, after)


def _scatter_start(fulls, axes, name, after):
    n = len(fulls)
    lands = []
    for f, ax in zip(fulls, axes):
        shp = list(f.shape)
        shp[ax] //= N_DEV
        lands.append(_hbm(lax.empty((N_DEV - 1,) + tuple(shp), f.dtype)))

    def body(*refs):
        srcs, dsts, send_ref, recv_ref, token = refs[:n], refs[n:2 * n], refs[2 * n + 1], refs[2 * n + 2], refs[-1]
        x, y, c, me = _my_position()
        for k, (peer, idx) in enumerate(_peers(x, y, c)):
            for a in range(n):
                _, theirs = _own_and_peer_rows(srcs[a], me, idx, axes[a])
                pltpu.make_async_remote_copy(theirs, dsts[a].at[k], send_ref.at[k * n + a], recv_ref.at[k * n + a],
                                             device_id=peer, device_id_type=MESH).start()
        token[...] = jnp.zeros_like(token)

    sem = pltpu.SemaphoreType.DMA(((N_DEV - 1) * n,))
    out = _call(
        body, name=name,
        out_shape=(sem, sem, *[pltpu.HBM(a.shape, a.dtype) for a in fulls], *[pltpu.HBM(a.shape, a.dtype) for a in lands],
                   jax.ShapeDtypeStruct((8, LANES), F32)),
        in_specs=[HBM_SPEC] * (2 * n) + [ANY_SPEC],
        out_specs=(SEM_SPEC, SEM_SPEC, *[HBM_SPEC] * (2 * n), pl.BlockSpec(memory_space=pltpu.VMEM)),
        input_output_aliases={a: 2 + a for a in range(2 * n)},
        compiler_params=_cparams(has_side_effects=DATAFLOW),
    )(*[_hbm(a) for a in fulls], *lands, after)
    return (out[0], out[1], list(out[2:2 + n]), list(out[2 + n:2 + 2 * n])), out[-1]


def _scatter_wait(started, axes, after, name):
    send_sems, recv_sems, fulls, lands = started
    n = len(fulls)

    def body(*refs):
        srcs, dsts, send_ref, recv_ref = refs[:n], refs[n:2 * n], refs[2 * n], refs[2 * n + 1]
        x, y, c, me = _my_position()
        for k, (peer, idx) in enumerate(_peers(x, y, c)):
            for a in range(n):
                _, theirs = _own_and_peer_rows(srcs[a], me, idx, axes[a])
                cp = pltpu.make_async_remote_copy(theirs, dsts[a].at[k], send_ref.at[k * n + a], recv_ref.at[k * n + a],
                                                  device_id=peer, device_id_type=MESH)
                cp.wait_send()
                cp.wait_recv()

    out = _call(
        body, name=name,
        out_shape=tuple(pltpu.HBM(a.shape, a.dtype) for a in (*fulls, *lands)),
        in_specs=[HBM_SPEC] * (2 * n) + [SEM_SPEC, SEM_SPEC, ANY_SPEC],
        out_specs=tuple([HBM_SPEC] * (2 * n)),
        input_output_aliases={a: a for a in range(2 * n)},
        compiler_params=_cparams(has_side_effects=DATAFLOW),
    )(*fulls, *lands, send_sems, recv_sems, after)
    return list(out[:n]), list(out[n:])


def _prep_weights(me, items, name):
    def body(me_ref, *refs):
        for (_, _, kind), src, dst in zip(items, refs[:len(items)], refs[len(items):]):
            dst[...] = (src[...].T if kind == "T" else src[...]).astype(BF16)

    ins, in_specs, out_shapes, out_specs = [], [], [], []
    for src, j, kind in items:
        shard = src.shape[1:]
        ins.append(src)
        in_specs.append(pl.BlockSpec((None,) + tuple(shard), lambda i, me_ref, j=j, nd=len(shard): (j,) + (0,) * nd))
        if kind == "G":
            out_shapes.append((shard[0], N_DEV * shard[1], shard[2]))
            out_specs.append(pl.BlockSpec(tuple(shard), lambda i, me_ref: (0, me_ref[0], 0)))
        else:
            rows = shard[1] if kind == "T" else shard[0]
            out_shapes.append((N_DEV * rows, D))
            out_specs.append(pl.BlockSpec((rows, D), lambda i, me_ref: (me_ref[0], 0)))
    out = _call(
        body, name=name,
        grid_spec=pltpu.PrefetchScalarGridSpec(num_scalar_prefetch=1, grid=(1,), in_specs=in_specs, out_specs=tuple(out_specs)),
        out_shape=tuple(jax.ShapeDtypeStruct(s, BF16) for s in out_shapes),
        compiler_params=_cparams(1),
    )(me.reshape(1), *ins)
    return list(out)


def _ada_forward(c_all, ada_w):
    cols = ada_w.shape[2]

    def body(c_ref, w_ref, o_ref):
        cv = c_ref[...]
        sc = (cv * _sigmoid(cv)).astype(BF16)
        o_ref[...] = _dot(sc, w_ref[...].astype(BF16))

    return _call(
        body, name="ada_forward", grid=(DEPTH,),
        out_shape=jax.ShapeDtypeStruct((DEPTH, N_DEV, cols), F32),
        in_specs=[pl.BlockSpec((N_DEV, D), lambda i: (0, 0)), pl.BlockSpec((None, D, cols), lambda i: (i, 0, 0))],
        out_specs=pl.BlockSpec((None, N_DEV, cols), lambda i: (i, 0, 0)),
        compiler_params=_cparams(1),
    )(c_all, ada_w)


def _ada_backward_adamw(c_pad, dmod_pad, w, m, v):
    cols = w.shape[2]

    def body(c_ref, dm_ref, w_ref, m_ref, v_ref, g_out, d_out, m_out, v_out):
        cv = c_ref[...]
        sc = (cv * _sigmoid(cv)).astype(BF16)
        g = _dot_tn(sc, dm_ref[...].astype(BF16))
        g_out[...] = g
        d_out[...], m_out[...], v_out[...] = _adamw(w_ref[...], g, m_ref[...], v_ref[...])

    wspec = pl.BlockSpec((None, D, cols), lambda i: (i, 0, 0))
    return _call(
        body, name="ada_backward_adamw", grid=(DEPTH,),
        out_shape=tuple(jax.ShapeDtypeStruct(w.shape, F32) for _ in range(4)),
        in_specs=[pl.BlockSpec((2 * N_DEV, D), lambda i: (0, 0)), pl.BlockSpec((None, 2 * N_DEV, cols), lambda i: (i, 0, 0)),
                  wspec, wspec, wspec],
        out_specs=(wspec, wspec, wspec, wspec),
        compiler_params=_cparams(1),
    )(c_pad, dmod_pad, w, m, v)


def _attn_in_proj(x, rope, ng, mod, w_t, j, gain, bd, tile):
    seq = x.shape[0]

    def body(x_ref, rope_ref, ng_ref, mod_ref, w_ref, gain_ref, bd_ref, qk_ref, qs_ref, kd_ref, vd_ref, g_ref):
        _, _, h = _norm_mod(x_ref[...], ng_ref[...], mod_ref[1:2, :], mod_ref[0:1, :])
        hb = h.astype(BF16)
        tabs = _rope_tabs(rope_ref)
        low = _low_half(tile)
        bdm = bd_ref[...]

        def put_kv(ref, blk, first_kv):
            sw = pltpu.roll(blk, HEAD_DIM, 1)
            ref[:, LANES * first_kv:LANES * (first_kv + 1)] = jnp.where(low, blk, sw).astype(BF16)
            ref[:, LANES * (first_kv + 1):LANES * (first_kv + 2)] = jnp.where(low, sw, blk).astype(BF16)

        def project(c):
            return _dot_nt(hb, w_ref[CHUNK * c:CHUNK * (c + 1), :])

        n_chunks = ATTN_IN // CHUNK
        per = CHUNK // LANES
        nxt = project(0)
        for c in range(n_chunks):
            cur = nxt
            if c + 1 < n_chunks:
                nxt = project(c + 1)
            col = CHUNK * c
            if col >= QK_W + N_KV * HEAD_DIM:
                g_ref[:, col - QK_W - N_KV * HEAD_DIM:col - QK_W - N_KV * HEAD_DIM + CHUNK] = cur
            elif col >= QK_W:
                for t in range(per):
                    put_kv(vd_ref, cur[:, LANES * t:LANES * (t + 1)], (col - QK_W) // HEAD_DIM + 2 * t)
            else:
                qk_ref[:, col:col + CHUNK] = cur
                for t in range(per):
                    b = per * c + t
                    blk = cur[:, LANES * t:LANES * (t + 1)]
                    ms = _dot_split(blk * blk, bdm) * (1.0 / HEAD_DIM)
                    y = (blk * lax.rsqrt(ms + NORM_EPS)) * gain_ref[:, LANES * b:LANES * (b + 1)]
                    rp = _rope(y, tabs)
                    if b < D // LANES:
                        rp = rp * (HEAD_DIM ** -0.5)
                        qs_ref[:, 2 * LANES * b:2 * LANES * b + LANES] = jnp.where(low, rp, 0.0).astype(BF16)
                        qs_ref[:, 2 * LANES * b + LANES:2 * LANES * (b + 1)] = jnp.where(low, 0.0, rp).astype(BF16)
                    else:
                        put_kv(kd_ref, rp, 2 * (b - D // LANES))

    row = lambda w: pl.BlockSpec((tile, w), lambda i: (i, 0))
    return _call(
        body, name=f"attn_in_proj_{j}", grid=(seq // tile,),
        out_shape=(jax.ShapeDtypeStruct((seq, QK_W), F32), jax.ShapeDtypeStruct((seq, N_HEADS * LANES), BF16),
                   jax.ShapeDtypeStruct((seq, KX_W), BF16), jax.ShapeDtypeStruct((seq, KX_W), BF16),
                   jax.ShapeDtypeStruct((seq, D), F32)),
        in_specs=[row(D), row(3 * LANES), _const_spec((1, D)), _const_spec((8, D)), _const_spec((ATTN_IN, D)),
                  _const_spec((1, QK_W)), _const_spec((LANES, LANES))],
        out_specs=(row(QK_W), row(N_HEADS * LANES), row(KX_W), row(KX_W), row(D)),
        compiler_params=_cparams(1),
    )(x, rope, ng, mod, w_t, gain, bd)


def _band_mask(n, rows, keys_on_rows):
    shape = (2 * QBLK, rows) if keys_on_rows else (rows, 2 * QBLK)
    qi = lax.broadcasted_iota(jnp.int32, shape, 1 if keys_on_rows else 0) & (QBLK - 1)
    kj = lax.broadcasted_iota(jnp.int32, shape, 0 if keys_on_rows else 1)
    diff = QBLK + qi - kj
    first_key = jnp.where(n > 0, 0, QBLK)
    return (diff >= 0) & (diff < QBLK) & (kj >= first_key)


def _stack_heads(ref, heads):
    return jnp.concatenate([ref[:, LANES * h:LANES * (h + 1)] for h in heads], axis=0)


def _kv_block(prev_ref, cur_ref, kv):
    cols = slice(LANES * kv, LANES * (kv + 1))
    return jnp.concatenate([prev_ref[:, cols], cur_ref[:, cols]], axis=0)


def _pair_up(st, low):
    return jnp.concatenate([jnp.where(low, st[0:QBLK], st[QBLK:2 * QBLK]),
                            jnp.where(low, st[2 * QBLK:3 * QBLK], st[3 * QBLK:4 * QBLK])], axis=1)


def _attn_forward(sinks, qs, kd, vd, j):
    seq = qs.shape[0]
    nb = seq // QBLK

    def body(sink_ref, q_ref, kp_ref, kc_ref, vp_ref, vc_ref, o_ref):
        n = pl.program_id(0)
        ok = _band_mask(n, 4 * QBLK, False)
        low = _low_half(QBLK)
        rowi = lax.broadcasted_iota(jnp.int32, (4 * QBLK, 1), 0)

        def scores(kv):
            return _dot_nt(_stack_heads(q_ref, range(4 * kv, 4 * kv + 4)), _kv_block(kp_ref, kc_ref, kv))

        nxt = scores(0)
        for kv in range(N_KV):
            s = jnp.where(ok, nxt, -1e30)
            if kv + 1 < N_KV:
                nxt = scores(kv + 1)
            sink = jnp.where(rowi < QBLK, sink_ref[4 * kv],
                             jnp.where(rowi < 2 * QBLK, sink_ref[4 * kv + 1],
                                       jnp.where(rowi < 3 * QBLK, sink_ref[4 * kv + 2], sink_ref[4 * kv + 3])))
            m = jnp.maximum(jnp.max(s, axis=1, keepdims=True), sink)
            p = jnp.exp(s - m)
            den = jnp.sum(p, axis=1, keepdims=True) + jnp.exp(sink - m)
            o_st = _dot((p / den).astype(BF16), _kv_block(vp_ref, vc_ref, kv))
            o_ref[:, 2 * LANES * kv:2 * LANES * (kv + 1)] = _pair_up(o_st, low)

    blk = lambda w: pl.BlockSpec((QBLK, w), lambda n: (n, 0))
    prev = lambda w: pl.BlockSpec((QBLK, w), lambda n: (jnp.maximum(n - 1, 0), 0))
    return _call(
        body, name=f"attn_forward_{j}", grid=(nb,),
        out_shape=jax.ShapeDtypeStruct((seq, D), F32),
        in_specs=[pl.BlockSpec(memory_space=pltpu.SMEM), blk(N_HEADS * LANES), prev(KX_W), blk(KX_W), prev(KX_W), blk(KX_W)],
        out_specs=blk(D),
        compiler_params=_cparams(1),
    )(sinks, qs, kd, kd, vd, vd)


def _attn_out_proj(x, o, g, w, j, mod, tile):
    seq = x.shape[0]

    def body(x_ref, o_ref, g_ref, w_ref, mod_ref, xo_ref, br_ref):
        gv = g_ref[...]
        u = (o_ref[...] * (gv * _sigmoid(gv))).astype(BF16)
        br = _dot(u, w_ref[...])
        br_ref[...] = br.astype(BF16)
        xo_ref[...] = x_ref[...] + mod_ref[2:3, :] * br

    row = pl.BlockSpec((tile, D), lambda i: (i, 0))
    return _call(
        body, name=f"attn_out_proj_{j}", grid=(seq // tile,),
        out_shape=(jax.ShapeDtypeStruct((seq, D), F32), jax.ShapeDtypeStruct((seq, D), BF16)),
        in_specs=[row, row, row, _const_spec((D, D)), _const_spec((8, D))],
        out_specs=(row, row),
        compiler_params=_cparams(1),
    )(x, o, g, w, mod)


def _attn_out_proj_bwd(dxn, br, o, g, w, j, mod, tile):
    seq = dxn.shape[0]
    steps = seq // tile

    def body(dxn_ref, br_ref, o_ref, g_ref, w_ref, mod_ref, do_ref, dg_ref, dw_ref, dgate_ref, dw_acc):
        i = pl.program_id(0)

        @pl.when(i == 0)
        def _():
            dw_acc[...] = jnp.zeros_like(dw_acc)
            dgate_ref[...] = jnp.zeros_like(dgate_ref)

        dxn_v, ov, gv = dxn_ref[...], o_ref[...], g_ref[...]
        dgate_ref[...] += jnp.sum(dxn_v * br_ref[...].astype(F32), axis=0, keepdims=True)
        dbr = (dxn_v * mod_ref[2:3, :]).astype(BF16)
        du = _dot_nt(dbr, w_ref[...])
        sg = _sigmoid(gv)
        sl = gv * sg
        dw_acc[...] += _dot_tn((ov * sl).astype(BF16), dbr)
        do = du * sl
        dg_ref[...] = (du * ov * (sg * (1.0 + gv * (1.0 - sg)))).astype(BF16)
        low = _low_half(tile)
        for b in range(D // LANES):
            blk = do[:, LANES * b:LANES * (b + 1)]
            do_ref[:, 2 * LANES * b:2 * LANES * b + LANES] = jnp.where(low, blk, 0.0).astype(BF16)
            do_ref[:, 2 * LANES * b + LANES:2 * LANES * (b + 1)] = jnp.where(low, 0.0, blk).astype(BF16)

        @pl.when(i == steps - 1)
        def _():
            dw_ref[...] = dw_acc[...].astype(BF16)

    row = lambda w_: pl.BlockSpec((tile, w_), lambda i: (i, 0))
    return _call(
        body, name=f"attn_out_proj_bwd_{j}", grid=(steps,),
        out_shape=(jax.ShapeDtypeStruct((seq, N_HEADS * LANES), BF16), jax.ShapeDtypeStruct((seq, D), BF16),
                   jax.ShapeDtypeStruct((D, D), BF16), jax.ShapeDtypeStruct((1, D), F32)),
        in_specs=[row(D), row(D), row(D), row(D), _const_spec((D, D)), _const_spec((8, D))],
        out_specs=(row(N_HEADS * LANES), row(D), pl.BlockSpec((D, D), lambda i: (0, 0)),
                   pl.BlockSpec((1, D), lambda i: (0, 0))),
        scratch_shapes=[pltpu.VMEM((D, D), F32)],
        compiler_params=_cparams(1),
    )(dxn, br, o, g, w, mod)


def _attn_backward(sinks, qs, dos, kd, vd, j):
    seq = qs.shape[0]
    nb = seq // QBLK

    def body(sink_ref, q_ref, do_ref, kp_ref, kc_ref, vp_ref, vc_ref, dq_ref, dk_ref, dv_ref, dsink_ref,
             carry_k, carry_v, sink_acc):
        n = pl.program_id(0)

        @pl.when(n == 0)
        def _():
            carry_k[...] = jnp.zeros_like(carry_k)
            carry_v[...] = jnp.zeros_like(carry_v)
            sink_acc[...] = jnp.zeros_like(sink_acc)

        @pl.when(n < nb)
        def _():
            ok = _band_mask(n, 2 * QBLK, True)
            low = _low_half(QBLK)
            lane_q = lax.broadcasted_iota(jnp.int32, (1, 2 * QBLK), 1)
            dk_parts, dv_parts = [], []

            def first_products(g):
                kv, half = divmod(g, 2)
                heads = (4 * kv + half, 4 * kv + 2 + half)
                q = _stack_heads(q_ref, heads)
                do = _stack_heads(do_ref, heads)
                kk = _kv_block(kp_ref, kc_ref, kv)
                return heads, q, do, kk, _dot_nt(kk, q), _dot_nt(_kv_block(vp_ref, vc_ref, kv), do)

            nxt = first_products(0)
            dq_h, dk_kv, dv_kv = [], None, None
            for g in range(2 * N_KV):
                heads, q, do, kk, s_raw, dp_raw = nxt
                if g + 1 < 2 * N_KV:
                    nxt = first_products(g + 1)
                st = jnp.where(ok, s_raw, -1e30)
                sink = jnp.where(lane_q < QBLK, sink_ref[heads[0]], sink_ref[heads[1]])
                m = jnp.maximum(jnp.max(st, axis=0, keepdims=True), sink)
                e = jnp.exp(st - m)
                e_sink = jnp.exp(sink - m)
                inv = 1.0 / (jnp.sum(e, axis=0, keepdims=True) + e_sink)
                p = e * inv
                pdp = p * dp_raw
                delta = jnp.sum(pdp, axis=0, keepdims=True)
                ds = (pdp - p * delta).astype(BF16)
                sink_acc[g:g + 1, :] -= e_sink * inv * delta
                dk_g, dv_g = _dot(ds, q), _dot(p.astype(BF16), do)
                dk_kv = dk_g if dk_kv is None else dk_kv + dk_g
                dv_kv = dv_g if dv_kv is None else dv_kv + dv_g
                dq_h.append(_dot_tn(ds, kk))
                if g % 2 == 1:
                    kv = g // 2
                    for t in range(2):
                        dq_ref[:, LANES * (2 * kv + t):LANES * (2 * kv + t + 1)] = jnp.where(
                            low, dq_h[0][QBLK * t:QBLK * (t + 1)], dq_h[1][QBLK * t:QBLK * (t + 1)])
                    dk_parts.append(dk_kv + pltpu.roll(dk_kv, HEAD_DIM, 1))
                    dv_parts.append(dv_kv + pltpu.roll(dv_kv, HEAD_DIM, 1))
                    dq_h, dk_kv, dv_kv = [], None, None

            def order(parts, lo, hi):
                return jnp.concatenate([jnp.where(low, parts[0][lo:hi], parts[1][lo:hi]),
                                        jnp.where(low, parts[2][lo:hi], parts[3][lo:hi])], axis=1)

            dk_ref[...] = carry_k[...] + order(dk_parts, 0, QBLK)
            dv_ref[...] = (carry_v[...] + order(dv_parts, 0, QBLK)).astype(BF16)
            carry_k[...] = order(dk_parts, QBLK, 2 * QBLK)
            carry_v[...] = order(dv_parts, QBLK, 2 * QBLK)

        @pl.when(n == nb)
        def _():
            dk_ref[...] = carry_k[...]
            dv_ref[...] = carry_v[...].astype(BF16)
            lane = lax.broadcasted_iota(jnp.int32, (1, LANES), 1)
            out = jnp.zeros((1, LANES), F32)
            for g in range(2 * N_KV):
                for t in range(2):
                    tot = jnp.sum(sink_acc[g:g + 1, QBLK * t:QBLK * (t + 1)], axis=1, keepdims=True)
                    out = jnp.where(lane == 4 * (g // 2) + 2 * t + g % 2, tot, out)
            dsink_ref[...] = out

    cur = lambda w: pl.BlockSpec((QBLK, w), lambda n: (jnp.minimum(n, nb - 1), 0))
    prev = lambda w: pl.BlockSpec((QBLK, w), lambda n: (jnp.maximum(n - 1, 0), 0))
    return _call(
        body, name=f"attn_backward_{j}", grid=(nb + 1,),
        out_shape=(jax.ShapeDtypeStruct((seq, D), F32), jax.ShapeDtypeStruct((seq, N_KV * HEAD_DIM), F32),
                   jax.ShapeDtypeStruct((seq, N_KV * HEAD_DIM), BF16), jax.ShapeDtypeStruct((1, LANES), F32)),
        in_specs=[pl.BlockSpec(memory_space=pltpu.SMEM), cur(N_HEADS * LANES), cur(N_HEADS * LANES), prev(KX_W), cur(KX_W),
                  prev(KX_W), cur(KX_W)],
        out_specs=(cur(D), prev(N_KV * HEAD_DIM), prev(N_KV * HEAD_DIM), pl.BlockSpec((1, LANES), lambda n: (0, 0))),
        scratch_shapes=[pltpu.VMEM((QBLK, N_KV * HEAD_DIM), F32), pltpu.VMEM((QBLK, N_KV * HEAD_DIM), F32),
                        pltpu.VMEM((2 * N_KV, 2 * QBLK), F32)],
        compiler_params=_cparams(1),
    )(sinks, qs, dos, kd, kd, vd, vd)


def _in_proj_tail(x_ref, dxn_ref, ng_ref, mod_ref, w_ref, dproj, dx_ref, dw_acc, vec_acc):
    ng, sc, sh = ng_ref[...], mod_ref[1:2, :], mod_ref[0:1, :]
    xh, r, h = _norm_mod(x_ref[...], ng, sc, sh)
    dh = _dot(dproj, w_ref[...])
    dw_acc[...] += _dot_tn(dproj, h.astype(BF16))
    vec_acc[0:1, :] += jnp.sum(dh, axis=0, keepdims=True)
    vec_acc[1:2, :] += jnp.sum(dh * xh, axis=0, keepdims=True)
    dxh = dh * (ng * (1.0 + sc))
    dx_ref[...] = dxn_ref[...] + r * (dxh - xh * jnp.mean(dxh * xh, axis=-1, keepdims=True))


def _tail_finish(ng_ref, mod_ref, dw_ref, vec_ref, dw_acc, vec_acc):
    dw_ref[...] = dw_acc[...].astype(BF16)
    a = vec_acc[1:2, :]
    vec_ref[...] = jnp.zeros_like(vec_ref)
    vec_ref[0:1, :] = vec_acc[0:1, :]
    vec_ref[1:2, :] = a * ng_ref[...]
    vec_ref[3:4, :] = a * (1.0 + mod_ref[1:2, :])


def _attn_in_proj_bwd(x, dxn, rope, qk_raw, dq, dk, dv, dg, ng, mod, w_t, j, gain, bd, tile):
    seq = x.shape[0]
    steps = seq // tile

    def body(x_ref, dxn_ref, rope_ref, qk_ref, dq_ref, dk_ref, dv_ref, dg_ref, ng_ref, mod_ref, w_ref, gain_ref,
             bd_ref, dx_ref, dw_ref, vec_ref, dgain_ref, dproj, dw_acc, vec_acc):
        i = pl.program_id(0)

        @pl.when(i == 0)
        def _():
            dw_acc[...] = jnp.zeros_like(dw_acc)
            vec_acc[...] = jnp.zeros_like(vec_acc)
            dgain_ref[...] = jnp.zeros_like(dgain_ref)

        tabs = _rope_tabs(rope_ref)
        bdm = bd_ref[...]
        for b in range(QK_W // LANES):
            cols = slice(LANES * b, LANES * (b + 1))
            raw = qk_ref[:, cols]
            if b < D // LANES:
                dy = dq_ref[:, cols] * (HEAD_DIM ** -0.5)
            else:
                dy = dk_ref[:, LANES * (b - D // LANES):LANES * (b + 1 - D // LANES)]
            dy = _rope_bwd(dy, tabs)
            rr = lax.rsqrt(_dot_split(raw * raw, bdm) * (1.0 / HEAD_DIM) + NORM_EPS)
            xh = raw * rr
            dgain_ref[:, cols] += jnp.sum(dy * xh, axis=0, keepdims=True)
            dxh = dy * gain_ref[:, cols]
            dproj[:, cols] = (rr * (dxh - xh * (_dot_split(dxh * xh, bdm) * (1.0 / HEAD_DIM)))).astype(BF16)
        dproj[:, QK_W:QK_W + N_KV * HEAD_DIM] = dv_ref[...]
        dproj[:, QK_W + N_KV * HEAD_DIM:] = dg_ref[...]
        _in_proj_tail(x_ref, dxn_ref, ng_ref, mod_ref, w_ref, dproj[...], dx_ref, dw_acc, vec_acc)

        @pl.when(i == steps - 1)
        def _():
            _tail_finish(ng_ref, mod_ref, dw_ref, vec_ref, dw_acc, vec_acc)

    row = lambda w, dt=None: pl.BlockSpec((tile, w), lambda i: (i, 0))
    fixed = lambda shape: pl.BlockSpec(shape, lambda i: (0,) * len(shape))
    return _call(
        body, name=f"attn_in_proj_bwd_{j}", grid=(steps,),
        out_shape=(jax.ShapeDtypeStruct((seq, D), F32), jax.ShapeDtypeStruct((ATTN_IN, D), BF16),
                   jax.ShapeDtypeStruct((8, D), F32), jax.ShapeDtypeStruct((1, QK_W), F32)),
        in_specs=[row(D), row(D), row(3 * LANES), row(QK_W), row(D), row(N_KV * HEAD_DIM), row(N_KV * HEAD_DIM), row(D),
                  _const_spec((1, D)), _const_spec((8, D)), _const_spec((ATTN_IN, D)), _const_spec((1, QK_W)),
                  _const_spec((LANES, LANES))],
        out_specs=(row(D), fixed((ATTN_IN, D)), fixed((8, D)), fixed((1, QK_W))),
        scratch_shapes=[pltpu.VMEM((tile, ATTN_IN), BF16), pltpu.VMEM((ATTN_IN, D), F32), pltpu.VMEM((8, D), F32)],
        compiler_params=_cparams(1),
    )(x, dxn, rope, qk_raw, dq, dk, dv, dg, ng, mod, w_t, gain, bd)


def _pool_in_proj(x, ng, mod, w_t, j, tile):
    seq = x.shape[0]

    def body(x_ref, ng_ref, mod_ref, w_ref, v_ref, g_ref):
        _, _, h = _norm_mod(x_ref[...], ng_ref[...], mod_ref[1:2, :], mod_ref[0:1, :])
        proj = _dot_nt(h.astype(BF16), w_ref[...])
        v_ref[...] = proj[:, :D]
        g_ref[...] = proj[:, D:]

    row = pl.BlockSpec((tile, D), lambda i: (i, 0))
    return _call(
        body, name=f"pool_in_proj_{j}", grid=(seq // tile,),
        out_shape=(jax.ShapeDtypeStruct((seq, D), F32), jax.ShapeDtypeStruct((seq, D), F32)),
        in_specs=[row, _const_spec((1, D)), _const_spec((8, D)), _const_spec((POOL_IN, D))],
        out_specs=(row, row),
        compiler_params=_cparams(1),
    )(x, ng, mod, w_t)


PAD = 8


def _window_sums(ext, lo, hi, forward):
    gw = D // len(POOL_WINDOWS)
    planes = []
    for gi, w in enumerate(POOL_WINDOWS):
        cols = slice(gw * gi, gw * (gi + 1))
        src, k = 0, 1
        while k < w:
            d = k if forward else -k
            ext[1 - src, lo:hi, cols] = ext[src, lo:hi, cols] + ext[src, lo + d:hi + d, cols]
            src, k = 1 - src, 2 * k
        planes.append(src)
    return planes


def _pooled(ext, v_ref, first, tile):
    t_abs = first + lax.broadcasted_iota(jnp.int32, (tile, 1), 0)
    top = PAD + HALO
    planes = _window_sums(ext, PAD, top + tile, False)
    outs = []
    gw = D // len(POOL_WINDOWS)
    for gi, w in enumerate(POOL_WINDOWS):
        cols = slice(gw * gi, gw * (gi + 1))
        cnt = jnp.minimum(t_abs + 1, w).astype(F32)
        outs.append(ext[planes[gi], top:top + tile, cols] / cnt - v_ref[:, cols])
    return jnp.concatenate(outs, axis=1)


def _fill_ext(ext, halo_ref, v_ref, i, tile):
    ext[0, 0:PAD, :] = jnp.zeros((PAD, D), F32)
    ext[1, 0:PAD, :] = jnp.zeros((PAD, D), F32)
    ext[0, PAD:PAD + HALO, :] = jnp.where(i == 0, 0.0, halo_ref[...])
    ext[0, PAD + HALO:PAD + HALO + tile, :] = v_ref[...]


def _group_mix(pb, wg_ref):
    gw = D // len(POOL_WINDOWS)
    return jnp.concatenate([_dot(pb[:, gw * gi:gw * (gi + 1)], wg_ref[gi]) for gi in range(len(POOL_WINDOWS))], axis=1)


def _pool_mix_out(x, v, g, wg, w_out, j, scale, mod, tile, target=None):
    seq = x.shape[0]

    def body(*refs):
        if target is None:
            x_ref, v_ref, halo_ref, g_ref, wg_ref, w_ref, scale_ref, mod_ref, xo_ref, br_ref, ext = refs
        else:
            x_ref, v_ref, halo_ref, g_ref, wg_ref, w_ref, scale_ref, mod_ref, t_ref, xo_ref, br_ref, loss_ref, ext = refs
        i = pl.program_id(0)
        _fill_ext(ext, halo_ref, v_ref, i, tile)
        pb = _pooled(ext, v_ref, i * tile, tile).astype(BF16)
        ms = _group_mix(pb, wg_ref) * scale_ref[...]
        gv = g_ref[...]
        u = (ms * (gv * _sigmoid(gv))).astype(BF16)
        br = _dot(u, w_ref[...])
        br_ref[...] = br.astype(BF16)
        y = x_ref[...] + mod_ref[2:3, :] * br
        if target is None:
            xo_ref[...] = y
        else:
            @pl.when(i == 0)
            def _():
                loss_ref[...] = jnp.zeros_like(loss_ref)

            e = y - t_ref[...]
            xo_ref[...] = e * (1.0 / D)
            loss_ref[...] += 0.5 * jnp.sum(jnp.mean(e * e, axis=-1, keepdims=True), axis=0, keepdims=True)

    row = pl.BlockSpec((tile, D), lambda i: (i, 0))
    halo = pl.BlockSpec((HALO, D), lambda i: (jnp.maximum(i * (tile // HALO) - 1, 0), 0))
    extra_in, extra_out, extra_shape = ([], (), ()) if target is None else (
        [row], (pl.BlockSpec((1, LANES), lambda i: (0, 0)),), (jax.ShapeDtypeStruct((1, LANES), F32),))
    return _call(
        body, name=f"pool_mix_out_{j}", grid=(seq // tile,),
        out_shape=(jax.ShapeDtypeStruct((seq, D), F32), jax.ShapeDtypeStruct((seq, D), BF16)) + extra_shape,
        in_specs=[row, row, halo, row, _const_spec(wg.shape), _const_spec((D, D)), _const_spec((1, D)),
                  _const_spec((8, D))] + extra_in,
        out_specs=(row, row) + extra_out,
        scratch_shapes=[pltpu.VMEM((2, tile + HALO + PAD, D), F32)],
        compiler_params=_cparams(1),
    )(x, v, v, g, wg, w_out, scale, mod, *(() if target is None else (target,)))


def _pool_mix_out_bwd(dxn, br, v, g, wg, w_out, j, scale, mod, tile):
    seq = dxn.shape[0]
    steps = seq // tile
    ng_ = len(POOL_WINDOWS)
    gw = D // ng_

    def body(dxn_ref, br_ref, v_ref, halo_ref, g_ref, wg_ref, w_ref, scale_ref, mod_ref,
             dpool_ref, dg_ref, dw_ref, dwg_ref, vec_ref, ext, dw_acc, dwg_acc):
        i = pl.program_id(0)

        @pl.when(i == 0)
        def _():
            dw_acc[...] = jnp.zeros_like(dw_acc)
            dwg_acc[...] = jnp.zeros_like(dwg_acc)
            vec_ref[...] = jnp.zeros_like(vec_ref)

        _fill_ext(ext, halo_ref, v_ref, i, tile)
        pb = _pooled(ext, v_ref, i * tile, tile).astype(BF16)
        mixed = _group_mix(pb, wg_ref)
        scale = scale_ref[...]
        ms = mixed * scale
        gv, dxn_v = g_ref[...], dxn_ref[...]
        sg = _sigmoid(gv)
        sl = gv * sg
        vec_ref[0:1, :] += jnp.sum(dxn_v * br_ref[...].astype(F32), axis=0, keepdims=True)
        dbr = (dxn_v * mod_ref[2:3, :]).astype(BF16)
        du = _dot_nt(dbr, w_ref[...])
        dw_acc[...] += _dot_tn((ms * sl).astype(BF16), dbr)
        dms = du * sl
        dg_ref[...] = (du * ms * (sg * (1.0 + gv * (1.0 - sg)))).astype(BF16)
        vec_ref[1:2, :] += jnp.sum(dms * mixed, axis=0, keepdims=True)
        dmx = (dms * scale).astype(BF16)
        for gi in range(ng_):
            cols = slice(gw * gi, gw * (gi + 1))
            dpool_ref[:, cols] = _dot_nt(dmx[:, cols], wg_ref[gi])
            dwg_acc[gi] += _dot_tn(pb[:, cols], dmx[:, cols])

        @pl.when(i == steps - 1)
        def _():
            dw_ref[...] = dw_acc[...].astype(BF16)
            dwg_ref[...] = dwg_acc[...].astype(BF16)

    row = pl.BlockSpec((tile, D), lambda i: (i, 0))
    halo = pl.BlockSpec((HALO, D), lambda i: (jnp.maximum(i * (tile // HALO) - 1, 0), 0))
    fixed = lambda shape: pl.BlockSpec(shape, lambda i: (0,) * len(shape))
    return _call(
        body, name=f"pool_mix_out_bwd_{j}", grid=(steps,),
        out_shape=(jax.ShapeDtypeStruct((seq, D), F32), jax.ShapeDtypeStruct((seq, D), BF16),
                   jax.ShapeDtypeStruct((D, D), BF16), jax.ShapeDtypeStruct((ng_, gw, gw), BF16),
                   jax.ShapeDtypeStruct((8, D), F32)),
        in_specs=[row, row, row, halo, row, _const_spec(wg.shape), _const_spec((D, D)), _const_spec((1, D)),
                  _const_spec((8, D))],
        out_specs=(row, row, fixed((D, D)), fixed((ng_, gw, gw)), fixed((8, D))),
        scratch_shapes=[pltpu.VMEM((2, tile + HALO + PAD, D), F32), pltpu.VMEM((D, D), F32), pltpu.VMEM((ng_, gw, gw), F32)],
        compiler_params=_cparams(1),
    )(dxn, br, v, v, g, wg, w_out, scale, mod)


def _pool_in_proj_bwd(x, dxn, dpool, dg, ng, mod, w_t, j, tile):
    seq = x.shape[0]
    steps = seq // tile
    gw = D // len(POOL_WINDOWS)

    def body(x_ref, dxn_ref, dp_ref, halo_ref, dg_ref, ng_ref, mod_ref, w_ref, dx_ref, dw_ref, vec_ref,
             ext, dproj, dw_acc, vec_acc):
        i = pl.program_id(0)

        @pl.when(i == 0)
        def _():
            dw_acc[...] = jnp.zeros_like(dw_acc)
            vec_acc[...] = jnp.zeros_like(vec_acc)

        t_abs = i * tile + lax.broadcasted_iota(jnp.int32, (tile, 1), 0)
        last = i == steps - 1
        ext[0, tile + HALO:tile + HALO + PAD, :] = jnp.zeros((PAD, D), F32)
        ext[1, tile + HALO:tile + HALO + PAD, :] = jnp.zeros((PAD, D), F32)
        for gi, w in enumerate(POOL_WINDOWS):
            cols = slice(gw * gi, gw * (gi + 1))
            cnt = jnp.minimum(t_abs + 1, w).astype(F32)
            ext[0, 0:tile, cols] = dp_ref[:, cols] / cnt
            ext[0, tile:tile + HALO, cols] = jnp.where(last, 0.0, halo_ref[:, cols] * (1.0 / w))
        planes = _window_sums(ext, 0, tile + HALO, True)
        for gi, w in enumerate(POOL_WINDOWS):
            cols = slice(gw * gi, gw * (gi + 1))
            dproj[:, cols] = (ext[planes[gi], 0:tile, cols] - dp_ref[:, cols]).astype(BF16)
        dproj[:, D:] = dg_ref[...]
        _in_proj_tail(x_ref, dxn_ref, ng_ref, mod_ref, w_ref, dproj[...], dx_ref, dw_acc, vec_acc)

        @pl.when(last)
        def _():
            _tail_finish(ng_ref, mod_ref, dw_ref, vec_ref, dw_acc, vec_acc)

    row = pl.BlockSpec((tile, D), lambda i: (i, 0))
    halo = pl.BlockSpec((HALO, D), lambda i: (jnp.minimum((i + 1) * (tile // HALO), seq // HALO - 1), 0))
    fixed = lambda shape: pl.BlockSpec(shape, lambda i: (0,) * len(shape))
    return _call(
        body, name=f"pool_in_proj_bwd_{j}", grid=(steps,),
        out_shape=(jax.ShapeDtypeStruct((seq, D), F32), jax.ShapeDtypeStruct((POOL_IN, D), BF16),
                   jax.ShapeDtypeStruct((8, D), F32)),
        in_specs=[row, row, row, halo, row, _const_spec((1, D)), _const_spec((8, D)), _const_spec((POOL_IN, D))],
        out_specs=(row, fixed((POOL_IN, D)), fixed((8, D))),
        scratch_shapes=[pltpu.VMEM((2, tile + HALO + PAD, D), F32), pltpu.VMEM((tile, POOL_IN), BF16), pltpu.VMEM((POOL_IN, D), F32),
                        pltpu.VMEM((8, D), F32)],
        compiler_params=_cparams(1),
    )(x, dxn, dpool, dpool, dg, ng, mod, w_t)


def _build_vec(vecs, gates, pool_vecs, gains, dsinks, loss_part):
    def body(v0, v1, v2, v3, g0, g2, p0, p1, n0, n1, s0, s1, loss_ref, out):
        out[...] = jnp.zeros_like(out)
        for i, v in enumerate((v0, v1, v2, v3)):
            out[3 * i:3 * i + 2, :] = v[0:2, :]
            out[12 + i:13 + i, :] = v[3:4, :]
        out[2:3, :] = g0[...]
        out[8:9, :] = g2[...]
        for j, (p, n, s) in enumerate(((p0, n0, s0), (p1, n1, s1))):
            out[3 * (2 * j + 1) + 2:3 * (2 * j + 1) + 3, :] = p[0:1, :]
            out[22 + j:23 + j, :] = p[1:2, :]
            out[16 + j:17 + j, :] = n[:, 0:D]
            out[18 + j:19 + j, 0:QK_W - D] = n[:, D:QK_W]
            out[20 + j:21 + j, 0:LANES] = s[...]
        out[24:25, 0:LANES] = loss_ref[...]

    vm = pl.BlockSpec(memory_space=pltpu.VMEM)
    args = (*vecs, gates[0], gates[2], *pool_vecs, *gains, *dsinks, loss_part)
    return _call(
        body, name="build_vec",
        out_shape=jax.ShapeDtypeStruct((VEC_ROWS, D), F32),
        in_specs=[vm] * len(args), out_specs=vm,
        compiler_params=_cparams(),
    )(*args)


def _sum_devices(g, after):
    rows = g.shape[1]

    def body(g_ref, after_ref, tot_ref, fold_ref):
        tot = g_ref[0]
        for p in range(1, N_DEV):
            tot = tot + g_ref[p]
        tot_ref[...] = tot
        f = tot[16:24, 0:LANES]
        for b in range(1, D // LANES):
            f = f + tot[16:24, LANES * b:LANES * (b + 1)]
        fold_ref[...] = f + pltpu.roll(f, HEAD_DIM, 1)

    return _call(
        body, name="sum_devices",
        out_shape=(jax.ShapeDtypeStruct((rows, D), F32), jax.ShapeDtypeStruct((8, LANES), F32)),
        in_specs=[pl.BlockSpec(memory_space=pltpu.VMEM), ANY_SPEC],
        out_specs=(pl.BlockSpec(memory_space=pltpu.VMEM), pl.BlockSpec(memory_space=pltpu.VMEM)),
        compiler_params=_cparams(),
    )(g, after)


def _adamw_small(params):
    n = len(params)

    def body(*refs):
        ins, outs = refs[:4 * n], refs[4 * n:]
        for p in range(n):
            w_ref, g_ref, m_ref, v_ref = ins[4 * p:4 * p + 4]
            outs[3 * p][...], outs[3 * p + 1][...], outs[3 * p + 2][...] = _adamw(w_ref[...], g_ref[...], m_ref[...], v_ref[...])

    vm = pl.BlockSpec(memory_space=pltpu.VMEM)
    out = _call(
        body, name="adamw_small",
        out_shape=tuple(jax.ShapeDtypeStruct(w.shape, F32) for (w, _, _, _) in params for _ in range(3)),
        in_specs=[vm] * (4 * n), out_specs=tuple([vm] * (3 * n)),
        compiler_params=_cparams(),
    )(*[a for p in params for a in p])
    return [tuple(out[3 * p:3 * p + 3]) for p in range(n)]


def _adamw_shards(name, me, fulls, lands, w, m, v, transpose, axis=0):
    nl = w.shape[0]
    wshape = w.shape[1:]
    own_shape = lands[0].shape[1:]

    def body(me_ref, *refs):
        own_refs, land_refs = refs[:nl], refs[nl:2 * nl]
        w_ref, m_ref, v_ref, g_out, d_out, m_out, v_out = refs[2 * nl:]
        layer = pl.program_id(0)
        for l in range(nl):
            @pl.when(layer == l)
            def _(l=l):
                g = own_refs[l][...].astype(F32)
                for k in range(N_DEV - 1):
                    g = g + land_refs[l][k].astype(F32)
                if transpose:
                    g = g.T
                g_out[...] = g
                d_out[...], m_out[...], v_out[...] = _adamw(w_ref[...], g, m_ref[...], v_ref[...])

    def own_index(l_, me_ref):
        idx = [0] * len(own_shape)
        idx[axis] = me_ref[0]
        return tuple(idx)

    own_spec = pl.BlockSpec(tuple(own_shape), own_index)
    land_spec = pl.BlockSpec((N_DEV - 1,) + tuple(own_shape), lambda l_, me_ref: (0,) * (1 + len(own_shape)))
    wspec = pl.BlockSpec((None,) + tuple(wshape), lambda l_, me_ref: (l_,) + (0,) * len(wshape))
    return _call(
        body, name=name,
        grid_spec=pltpu.PrefetchScalarGridSpec(num_scalar_prefetch=1, grid=(nl,),
                                               in_specs=[own_spec] * nl + [land_spec] * nl + [wspec] * 3,
                                               out_specs=(wspec,) * 4),
        out_shape=tuple(jax.ShapeDtypeStruct(w.shape, F32) for _ in range(4)),
        compiler_params=_cparams(1),
    )(me.reshape(1), *fulls, *lands, w, m, v)


def _constants():
    lane = np.arange(LANES)
    bd = (lane[:, None] // HEAD_DIM == lane[None, :] // HEAD_DIM).astype(np.float32)
    half = ROT_DIM // 2
    inv_freq = ROPE_THETA ** (-jnp.arange(half, dtype=F32) * 2.0 / ROT_DIM)
    invf = jnp.tile(inv_freq, LANES // half).reshape(1, LANES)
    return jnp.asarray(bd, BF16), invf


def kernel(x, c, positions, ada_w, ada_b, norm_g, attn_w_in, attn_q_norm, attn_k_norm, attn_sinks, attn_w_out, pool_w_in, pool_w_group, pool_scale, pool_w_out, loss_target, m_ada_w, m_ada_b, m_norm_g, m_attn_w_in, m_attn_q_norm, m_attn_k_norm, m_attn_sinks, m_attn_w_out, m_pool_w_in, m_pool_w_group, m_pool_scale, m_pool_w_out, v_ada_w, v_ada_b, v_norm_g, v_attn_w_in, v_attn_q_norm, v_attn_k_norm, v_attn_sinks, v_attn_w_out, v_pool_w_in, v_pool_w_group, v_pool_scale, v_pool_w_out):
    seq = x.shape[1]
    me = 4 * lax.axis_index("x") + 2 * lax.axis_index("y") + lax.axis_index("c")
    bd, invf = _constants()
    t_mm = min(512, seq)
    rope = _rope_table(positions.reshape(seq, 1), invf, t_mm)
    t_bw = min(256, seq)
    shard = pool_scale.shape[1]
    cols = ada_w.shape[2]

    w_first, = _prep_weights(me, [(attn_w_in, 0, "T")], "prep_first")
    first_w, token = _gather_first_start(w_first, c)
    prepped = _prep_weights(me, [(attn_w_out, 0, "N"), (pool_w_in, 0, "T"), (pool_w_out, 0, "N"), (pool_w_group, 0, "G"),
                                 (attn_w_in, 1, "T"), (attn_w_out, 1, "N"), (pool_w_in, 1, "T"), (pool_w_out, 1, "N"),
                                 (pool_w_group, 1, "G")], "prep_rest")

    first = jnp.concatenate([c, jnp.pad(pool_scale, ((0, 0), (0, D - shard))), jnp.zeros((5, D), F32)], axis=0)
    first = _allgather_small(first + token[0, 0], "allgather_c", rope)
    c_all = first[:, 0, :]
    scale_full = jnp.transpose(first[:, 1:3, :shard], (1, 0, 2)).reshape(2, D)
    mod_part = _ada_forward(c_all, ada_w)
    mod_all = _allgather_small(mod_part.reshape(DEPTH * N_DEV, cols), "allgather_mod", prepped[0])
    mod_all = mod_all.reshape(N_DEV, DEPTH, N_DEV, cols)
    mine = lax.dynamic_index_in_dim(mod_all, me, axis=2, keepdims=False)
    mod = jnp.transpose(mine, (1, 0, 2)).reshape(DEPTH, 3 * D) + ada_b
    mod = jnp.pad(mod.reshape(DEPTH, 3, D), ((0, 0), (0, 5), (0, 0)))

    groups = [prepped[0:1], prepped[1:4], prepped[4:6], prepped[6:9]]
    gaxes = [(0,), (0,), (0, 0, 1), (0, 0), (0, 0, 1)]
    first_w, token = _gather_first_forward(first_w, mod)
    rest, token = _gather_start(groups, gaxes[1:], token, "gather_start_rest")
    started = [None] + rest

    saved, weights = [], []
    h = x[0]
    for i in range(DEPTH):
        j = i // 2
        s = dict(x=h, ng=norm_g[i:i + 1], md=mod[i])
        if i == 0:
            w_in_t = _gather_first_wait(first_w, token)
        else:
            wts = _gather_wait(started[i + 1], gaxes[i + 1], h, f"gather_wait_{i}")
        if i % 2 == 0:
            if i > 0:
                w_in_t, w_out = wts
            s["gain"] = jnp.concatenate([jnp.tile(attn_q_norm[j], N_HEADS), jnp.tile(attn_k_norm[j], N_KV)]).reshape(1, QK_W)
            s["qk_raw"], s["qs"], s["kd"], s["vd"], s["g"] = _attn_in_proj(
                h, rope, s["ng"], s["md"], w_in_t, j, s["gain"], bd, t_bw)
            s["o"] = _attn_forward(attn_sinks[j], s["qs"], s["kd"], s["vd"], j)
            if i == 0:
                w_out, = _gather_wait(started[1], gaxes[1], s["o"], "gather_wait_0_out")
            h, s["br"] = _attn_out_proj(h, s["o"], s["g"], w_out, j, s["md"], t_mm)
            weights.append((w_in_t, w_out))
        else:
            p_in_t, p_out, p_grp = wts
            s["scale"] = scale_full[j:j + 1]
            s["v"], s["g"] = _pool_in_proj(h, s["ng"], s["md"], p_in_t, j, t_mm)
            if i < DEPTH - 1:
                h, s["br"] = _pool_mix_out(h, s["v"], s["g"], p_grp, p_out, j, s["scale"], s["md"], t_mm)
            else:
                dx, s["br"], loss_part = _pool_mix_out(h, s["v"], s["g"], p_grp, p_out, j, s["scale"], s["md"], t_mm,
                                                       loss_target[0])
            weights.append(wts)
        saved.append(s)

    vecs, gates, gains, dsinks, pool_vecs = [None] * DEPTH, [None] * DEPTH, [None] * 2, [None] * 2, [None] * 2
    sent_in, sent_out = [None] * DEPTH, [None] * DEPTH
    token = jnp.zeros((8, LANES), F32)
    for i in reversed(range(DEPTH)):
        j = i // 2
        s = saved[i]
        md = s["md"] + token[0, 0]
        if i % 2 == 0:
            w_in_t, w_out = weights[i]
            dos, dg, d_w_out, gates[i] = _attn_out_proj_bwd(dx, s["br"], s["o"], s["g"], w_out, j, md, t_mm)
            sent_out[i], token = _scatter_start([d_w_out], (0,), f"scatter_start_{i}_out", token)
            dq, dk, dv, dsinks[j] = _attn_backward(attn_sinks[j] + token[0, 0], s["qs"], dos, s["kd"], s["vd"], j)
            dx, d_in_t, vecs[i], gains[j] = _attn_in_proj_bwd(
                s["x"], dx, rope, s["qk_raw"], dq, dk, dv, dg, s["ng"], md, w_in_t, j, s["gain"], bd, t_bw)
        else:
            p_in_t, p_out, p_grp = weights[i]
            dpool, dg, d_p_out, d_p_grp, pool_vecs[j] = _pool_mix_out_bwd(
                dx, s["br"], s["v"], s["g"], p_grp, p_out, j, s["scale"], md, t_mm)
            sent_out[i], token = _scatter_start([d_p_out, d_p_grp], (0, 1), f"scatter_start_{i}_out", token)
            dx, d_in_t, vecs[i] = _pool_in_proj_bwd(s["x"], dx, dpool, dg, s["ng"], s["md"] + token[0, 0], p_in_t, j, t_bw)
        if i > 0:
            sent_in[i], token = _scatter_start([d_in_t], (0,), f"scatter_start_{i}_in", token)

    vec = _build_vec(vecs, gates, pool_vecs, gains, dsinks, loss_part)
    vec_rows = lax.dynamic_update_slice(jnp.zeros((N_DEV * VEC_ROWS, D), F32), vec, (me * VEC_ROWS, 0))
    vec_sent, token = _gather_start([[vec_rows]], [(0,)], loss_part, "vec_gather_start")
    sent_in[0], token = _scatter_start([d_in_t], (0,), "scatter_start_0_in", token)

    got_in, got_out = [None] * DEPTH, [None] * DEPTH
    for i in (3, 1):
        got_out[i] = _scatter_wait(sent_out[i], (0, 1), token, f"scatter_wait_{i}_out")
        got_in[i] = _scatter_wait(sent_in[i], (0,), token, f"scatter_wait_{i}_in")
    pick = lambda got, ls, a: ([got[i][0][a] for i in ls], [got[i][1][a] for i in ls])
    res = {}
    res["pool_w_in"] = _adamw_shards("adamw_pool_w_in", me, *pick(got_in, (1, 3), 0), pool_w_in, m_pool_w_in, v_pool_w_in, True)
    res["pool_w_out"] = _adamw_shards("adamw_pool_w_out", me, *pick(got_out, (1, 3), 0), pool_w_out, m_pool_w_out,
                                      v_pool_w_out, False)
    res["pool_w_group"] = _adamw_shards("adamw_pool_w_group", me, *pick(got_out, (1, 3), 1), pool_w_group, m_pool_w_group,
                                        v_pool_w_group, False, axis=1)

    vec_all, = _gather_wait(vec_sent[0], (0,), res["pool_w_group"][0], "vec_gather_wait")
    vec_all = vec_all.reshape(N_DEV, VEC_ROWS, D)
    tot, folded = _sum_devices(vec_all, token)
    loss = tot[24, 0]
    small = dict(
        ada_b=(ada_b, tot[0:12].reshape(DEPTH, 3 * D), m_ada_b, v_ada_b),
        norm_g=(norm_g, tot[12:16], m_norm_g, v_norm_g),
        q_norm=(attn_q_norm, folded[0:2, :HEAD_DIM], m_attn_q_norm, v_attn_q_norm),
        k_norm=(attn_k_norm, folded[2:4, :HEAD_DIM], m_attn_k_norm, v_attn_k_norm),
        sinks=(attn_sinks, tot[20:22, :N_HEADS], m_attn_sinks, v_attn_sinks),
        pool_scale=(pool_scale, lax.dynamic_slice(tot, (22, me * shard), (2, shard)), m_pool_scale, v_pool_scale),
    )
    res.update({k: (a[1],) + upd for (k, a), upd in zip(small.items(), _adamw_small(list(small.values())))})

    dmod_all = vec_all[:, 0:12, :].reshape(N_DEV, DEPTH, 3 * D)
    dmod_mine = lax.dynamic_slice_in_dim(dmod_all, me * cols, cols, axis=2)
    dmod_mine = jnp.pad(jnp.transpose(dmod_mine, (1, 0, 2)), ((0, 0), (0, N_DEV), (0, 0))) + token[0, 0]
    res["ada_w"] = _ada_backward_adamw(jnp.pad(c_all, ((0, N_DEV), (0, 0))), dmod_mine, ada_w, m_ada_w, v_ada_w)

    for i in (2, 0):
        got_out[i] = _scatter_wait(sent_out[i], (0,), res["ada_w"][0], f"scatter_wait_{i}_out")
        got_in[i] = _scatter_wait(sent_in[i], (0,), res["ada_w"][0], f"scatter_wait_{i}_in")
    res["attn_w_out"] = _adamw_shards("adamw_attn_w_out", me, *pick(got_out, (0, 2), 0), attn_w_out, m_attn_w_out,
                                      v_attn_w_out, False)
    res["attn_w_in"] = _adamw_shards("adamw_attn_w_in", me, *pick(got_in, (0, 2), 0), attn_w_in, m_attn_w_in, v_attn_w_in, True)

    order = ("ada_w", "ada_b", "norm_g", "attn_w_in", "q_norm", "k_norm", "sinks", "attn_w_out", "pool_w_in",
             "pool_w_group", "pool_scale", "pool_w_out")
    return (loss, dx[None], *[res[k][0] for k in order], *[res[k][1] for k in order], *[res[k][2] for k in order],
            *[res[k][3] for k in order])
```

```python
import functools

import numpy as np
import jax
import jax.numpy as jnp
from jax import lax
from jax.experimental import pallas as pl
from jax.experimental.pallas import tpu as pltpu

F32 = jnp.float32
BF16 = jnp.bfloat16
MESH = pl.DeviceIdType.MESH

N_DEV = 8
D = 1024
DEPTH = 4
HEAD_DIM = 64
N_HEADS = 16
N_KV = 4
QK_W = 1280
ATTN_IN = 2560
POOL_IN = 2048
QBLK = 128
KX_W = N_KV * 128
CHUNK = 256
POOL_WINDOWS = (2, 4, 8, 16)
HALO = 16
ROPE_THETA = 500000.0
ROT_DIM = 16
NORM_EPS = 1e-6
ADAM_LR = 0.001
ADAM_B1 = 0.9
ADAM_B2 = 0.999
ADAM_EPS = 1e-08
ADAM_WD = 0.01
ADAM_STEP = 10

LANES = 128
VMEM_LIMIT = 56 * 2**20
VEC_ROWS = 32


def _cparams(n_grid=0, **kw):
    if n_grid:
        kw["dimension_semantics"] = ("arbitrary",) * n_grid
    return pltpu.CompilerParams(vmem_limit_bytes=VMEM_LIMIT, **kw)


def _call(body, **kw):
    return pl.pallas_call(body, **kw)


def _const_spec(shape):
    nd = len(shape)
    return pl.BlockSpec(shape, lambda *_: (0,) * nd, pipeline_mode=pl.Buffered(1))


def _dot(a, b):
    return jnp.dot(a, b, preferred_element_type=F32)


def _dot_nt(a, b):
    return lax.dot_general(a, b, (((1,), (1,)), ((), ())), preferred_element_type=F32)


def _dot_tn(a, b):
    return lax.dot_general(a, b, (((0,), (0,)), ((), ())), preferred_element_type=F32)


def _dot_split(x, m):
    hi = x.astype(BF16)
    lo = (x - hi.astype(F32)).astype(BF16)
    return _dot(hi, m) + _dot(lo, m)


def _sigmoid(g):
    return 1.0 / (1.0 + jnp.exp(-g))


def _norm_mod(x, ng, sc, sh):
    r = lax.rsqrt(jnp.mean(x * x, axis=-1, keepdims=True) + NORM_EPS)
    xh = x * r
    h = (xh * ng) * (1.0 + sc) + sh
    return xh, r, h


def _rope_table(pos_col, invf_row, tile):
    seq = pos_col.shape[0]

    def body(pos_ref, invf_ref, out_ref):
        ang = pos_ref[...].astype(F32) * invf_ref[...]
        l64 = lax.broadcasted_iota(jnp.int32, (tile, LANES), 1) & (HEAD_DIM - 1)
        cs, sn = jnp.cos(ang), jnp.sin(ang)
        out_ref[:, 0:LANES] = jnp.where(l64 < ROT_DIM, cs, 1.0)
        out_ref[:, LANES:2 * LANES] = jnp.where(l64 < ROT_DIM // 2, -sn, 0.0)
        out_ref[:, 2 * LANES:3 * LANES] = jnp.where((l64 >= ROT_DIM // 2) & (l64 < ROT_DIM), sn, 0.0)

    return _call(
        body, name="rope_table", grid=(seq // tile,),
        out_shape=jax.ShapeDtypeStruct((seq, 3 * LANES), F32),
        in_specs=[pl.BlockSpec((tile, 1), lambda i: (i, 0)), _const_spec((1, LANES))],
        out_specs=pl.BlockSpec((tile, 3 * LANES), lambda i: (i, 0)),
        compiler_params=_cparams(1),
    )(pos_col, invf_row)


def _rope_tabs(rope_ref):
    return rope_ref[:, 0:LANES], rope_ref[:, LANES:2 * LANES], rope_ref[:, 2 * LANES:3 * LANES]


def _rope(y, tabs):
    cos_t, sin_a, sin_b = tabs
    return y * cos_t + pltpu.roll(y, LANES - ROT_DIM // 2, 1) * sin_a + pltpu.roll(y, ROT_DIM // 2, 1) * sin_b


def _rope_bwd(dy, tabs):
    cos_t, sin_a, sin_b = tabs
    return dy * cos_t + pltpu.roll(dy * sin_a, ROT_DIM // 2, 1) + pltpu.roll(dy * sin_b, LANES - ROT_DIM // 2, 1)


def _low_half(rows):
    return lax.broadcasted_iota(jnp.int32, (rows, LANES), 1) < HEAD_DIM


def _adamw(w, g, m, v):
    m = ADAM_B1 * m + (1.0 - ADAM_B1) * g
    v = ADAM_B2 * v + (1.0 - ADAM_B2) * (g * g)
    m_hat = m / (1.0 - ADAM_B1 ** ADAM_STEP)
    v_hat = v / (1.0 - ADAM_B2 ** ADAM_STEP)
    delta = -ADAM_LR * (m_hat / (jnp.sqrt(v_hat) + ADAM_EPS) + ADAM_WD * w)
    return delta, m, v


def _my_position():
    x, y, c = lax.axis_index("x"), lax.axis_index("y"), lax.axis_index("c")
    return x, y, c, 4 * x + 2 * y + c


def _peers(x, y, c):
    out = []
    for k in range(1, N_DEV):
        px = 1 - x if k & 4 else x
        py = 1 - y if k & 2 else y
        pc = 1 - c if k & 1 else c
        out.append(((px, py, pc), 4 * px + 2 * py + pc))
    return out


def _allgather_small(v, name, after):
    rows, cols = v.shape

    def body(v_ref, after_ref, out_ref, send_sems, recv_sems, local_sem):
        x, y, c, me = _my_position()
        local = pltpu.make_async_copy(v_ref, out_ref.at[me], local_sem)
        local.start()
        sends = []
        for k, (peer, _) in enumerate(_peers(x, y, c)):
            cp = pltpu.make_async_remote_copy(v_ref, out_ref.at[me], send_sems.at[k], recv_sems.at[k],
                                              device_id=peer, device_id_type=MESH)
            cp.start()
            sends.append(cp)
        for k, (peer, idx) in enumerate(_peers(x, y, c)):
            pltpu.make_async_remote_copy(v_ref, out_ref.at[idx], send_sems.at[k], recv_sems.at[k],
                                         device_id=peer, device_id_type=MESH).wait_recv()
        for cp in sends:
            cp.wait_send()
        local.wait()

    return _call(
        body, name=name,
        out_shape=jax.ShapeDtypeStruct((N_DEV, rows, cols), F32),
        in_specs=[pl.BlockSpec(memory_space=pltpu.VMEM), pl.BlockSpec(memory_space=pl.ANY)],
        out_specs=pl.BlockSpec(memory_space=pltpu.VMEM),
        scratch_shapes=[pltpu.SemaphoreType.DMA((N_DEV - 1,)), pltpu.SemaphoreType.DMA((N_DEV - 1,)),
                        pltpu.SemaphoreType.DMA(())],
        compiler_params=_cparams(),
    )(v, after)


def _shard_rows(ref, idx, rows, axis):
    sl = [slice(None)] * len(ref.shape)
    sl[axis] = pl.ds(idx * rows, rows)
    return ref.at[tuple(sl)]


def _own_and_peer_rows(ref, me, idx, axis):
    rows = ref.shape[axis] // N_DEV
    return _shard_rows(ref, me, rows, axis), _shard_rows(ref, idx, rows, axis)


HBM_SPEC = pl.BlockSpec(memory_space=pltpu.HBM)
SEM_SPEC = pl.BlockSpec(memory_space=pltpu.SEMAPHORE)
ANY_SPEC = pl.BlockSpec(memory_space=pl.ANY)
DATAFLOW = pltpu.SideEffectType.DATAFLOW_SIDE_EFFECTING


def _hbm(a):
    return pltpu.with_memory_space_constraint(a, pltpu.HBM)


def _gather_start(layers, axes, after, name):
    flat = [a for arrs in layers for a in arrs]
    flat_axes = [ax for axs in axes for ax in axs]
    n, nl = len(flat), len(layers)

    def body(*refs):
        ins, sems, token = refs[:n], refs[n + 1:n + 1 + 2 * nl], refs[-1]
        x, y, c, me = _my_position()
        a0 = 0
        for li, arrs in enumerate(layers):
            for k, (peer, _) in enumerate(_peers(x, y, c)):
                for a in range(len(arrs)):
                    rows, _ = _own_and_peer_rows(ins[a0 + a], me, me, flat_axes[a0 + a])
                    pltpu.make_async_remote_copy(rows, rows, sems[2 * li].at[k * len(arrs) + a],
                                                 sems[2 * li + 1].at[k * len(arrs) + a],
                                                 device_id=peer, device_id_type=MESH).start()
            a0 += len(arrs)
        token[...] = jnp.zeros_like(token)

    sem_shapes = []
    for arrs in layers:
        sem_shapes += [pltpu.SemaphoreType.DMA(((N_DEV - 1) * len(arrs),))] * 2
    out = _call(
        body, name=name,
        out_shape=(*sem_shapes, *[pltpu.HBM(a.shape, a.dtype) for a in flat], jax.ShapeDtypeStruct((8, LANES), F32)),
        in_specs=[HBM_SPEC] * n + [ANY_SPEC],
        out_specs=(*[SEM_SPEC] * (2 * nl), *[HBM_SPEC] * n, pl.BlockSpec(memory_space=pltpu.VMEM)),
        input_output_aliases={a: 2 * nl + a for a in range(n)},
        compiler_params=_cparams(has_side_effects=DATAFLOW),
    )(*[_hbm(a) for a in flat], after)
    per_layer, a0 = [], 0
    for li, arrs in enumerate(layers):
        per_layer.append((out[2 * li], out[2 * li + 1], list(out[2 * nl + a0:2 * nl + a0 + len(arrs)])))
        a0 += len(arrs)
    return per_layer, out[-1]


def _gather_wait(started, axes, after, name):
    send_sems, recv_sems, arrs = started
    n = len(arrs)

    def body(*refs):
        ins, send_ref, recv_ref = refs[:n], refs[n], refs[n + 1]
        x, y, c, me = _my_position()
        for k, (peer, idx) in enumerate(_peers(x, y, c)):
            for a in range(n):
                own, theirs = _own_and_peer_rows(ins[a], me, idx, axes[a])
                cp = pltpu.make_async_remote_copy(own, theirs, send_ref.at[k * n + a], recv_ref.at[k * n + a],
                                                  device_id=peer, device_id_type=MESH)
                cp.wait_send()
                cp.wait_recv()

    return _call(
        body, name=name,
        out_shape=tuple(pltpu.HBM(a.shape, a.dtype) for a in arrs),
        in_specs=[HBM_SPEC] * n + [SEM_SPEC, SEM_SPEC, ANY_SPEC],
        out_specs=tuple([HBM_SPEC] * n),
        input_output_aliases={a: a for a in range(n)},
        compiler_params=_cparams(has_side_effects=DATAFLOW),
    )(*arrs, send_sems, recv_sems, after)


def _first_relations(x, y, c):
    return [(x, y, 1 - c), (1 - x, y, c), (x, 1 - y, c), (1 - x, 1 - y, c)]


def _gather_first_start(arr, after):
    n_rel = 4

    def body(a_ref, after_ref, send_ref, recv_ref, thru, token):
        x, y, c, me = _my_position()
        rows, _ = _own_and_peer_rows(a_ref, me, me, 0)
        for k, peer in enumerate(_first_relations(x, y, c)):
            pltpu.make_async_remote_copy(rows, rows, send_ref.at[k], recv_ref.at[k], device_id=peer, device_id_type=MESH).start()
        token[...] = jnp.zeros_like(token)

    sem = pltpu.SemaphoreType.DMA((n_rel,))
    out = _call(
        body, name="gather_first_start",
        out_shape=(sem, sem, pltpu.HBM(arr.shape, arr.dtype), jax.ShapeDtypeStruct((8, LANES), F32)),
        in_specs=[HBM_SPEC, ANY_SPEC],
        out_specs=(SEM_SPEC, SEM_SPEC, HBM_SPEC, pl.BlockSpec(memory_space=pltpu.VMEM)),
        input_output_aliases={0: 2},
        compiler_params=_cparams(has_side_effects=DATAFLOW),
    )(_hbm(arr), after)
    return out[:3], out[3]


def _gather_first_forward(started, after):
    send_a, recv_a, arr = started

    def body(a_ref, send_a_ref, recv_a_ref, after_ref, send_b_ref, recv_b_ref, thru, token):
        x, y, c, me = _my_position()
        sibling = (x, y, 1 - c)
        for k, peer in enumerate(_first_relations(x, y, c)):
            own, theirs = _own_and_peer_rows(a_ref, me, 4 * peer[0] + 2 * peer[1] + peer[2], 0)
            cp = pltpu.make_async_remote_copy(own, theirs, send_a_ref.at[k], recv_a_ref.at[k], device_id=peer, device_id_type=MESH)
            cp.wait_send()
            cp.wait_recv()
            if k > 0:
                pltpu.make_async_remote_copy(theirs, theirs, send_b_ref.at[k - 1], recv_b_ref.at[k - 1],
                                             device_id=sibling, device_id_type=MESH).start()
        token[...] = jnp.zeros_like(token)

    sem = pltpu.SemaphoreType.DMA((3,))
    out = _call(
        body, name="gather_first_forward",
        out_shape=(sem, sem, pltpu.HBM(arr.shape, arr.dtype), jax.ShapeDtypeStruct((8, LANES), F32)),
        in_specs=[HBM_SPEC, SEM_SPEC, SEM_SPEC, ANY_SPEC],
        out_specs=(SEM_SPEC, SEM_SPEC, HBM_SPEC, pl.BlockSpec(memory_space=pltpu.VMEM)),
        input_output_aliases={0: 2},
        compiler_params=_cparams(has_side_effects=DATAFLOW),
    )(arr, send_a, recv_a, after)
    return out[:3], out[3]


def _gather_first_wait(forwarded, after):
    send_b, recv_b, arr = forwarded

    def body(a_ref, send_b_ref, recv_b_ref, after_ref, thru):
        x, y, c, me = _my_position()
        sibling = (x, y, 1 - c)
        for k, peer in enumerate(_first_relations(x, y, c)[1:]):
            _, sent = _own_and_peer_rows(a_ref, me, 4 * peer[0] + 2 * peer[1] + peer[2], 0)
            _, got = _own_and_peer_rows(a_ref, me, 4 * peer[0] + 2 * peer[1] + (1 - peer[2]), 0)
            cp = pltpu.make_async_remote_copy(sent, got, send_b_ref.at[k], recv_b_ref.at[k], device_id=sibling, device_id_type=MESH)
            cp.wait_send()
            cp.wait_recv()

    return _call(
        body, name="gather_first_wait",
        out_shape=pltpu.HBM(arr.shape, arr.dtype),
        in_specs=[HBM_SPEC, SEM_SPEC, SEM_SPEC, ANY_SPEC],
        out_specs=HBM_SPEC,
        input_output_aliases={0: 0},
        compiler_params=_cparams(has_side_effects=DATAFLOW),
    )(arr, send_b, recv_b, after)


def _scatter_start(fulls, axes, name, after):
    n = len(fulls)
    lands = []
    for f, ax in zip(fulls, axes):
        shp = list(f.shape)
        shp[ax] //= N_DEV
        lands.append(_hbm(lax.empty((N_DEV - 1,) + tuple(shp), f.dtype)))

    def body(*refs):
        srcs, dsts, send_ref, recv_ref, token = refs[:n], refs[n:2 * n], refs[2 * n + 1], refs[2 * n + 2], refs[-1]
        x, y, c, me = _my_position()
        for k, (peer, idx) in enumerate(_peers(x, y, c)):
            for a in range(n):
                _, theirs = _own_and_peer_rows(srcs[a], me, idx, axes[a])
                pltpu.make_async_remote_copy(theirs, dsts[a].at[k], send_ref.at[k * n + a], recv_ref.at[k * n + a],
                                             device_id=peer, device_id_type=MESH).start()
        token[...] = jnp.zeros_like(token)

    sem = pltpu.SemaphoreType.DMA(((N_DEV - 1) * n,))
    out = _call(
        body, name=name,
        out_shape=(sem, sem, *[pltpu.HBM(a.shape, a.dtype) for a in fulls], *[pltpu.HBM(a.shape, a.dtype) for a in lands],
                   jax.ShapeDtypeStruct((8, LANES), F32)),
        in_specs=[HBM_SPEC] * (2 * n) + [ANY_SPEC],
        out_specs=(SEM_SPEC, SEM_SPEC, *[HBM_SPEC] * (2 * n), pl.BlockSpec(memory_space=pltpu.VMEM)),
        input_output_aliases={a: 2 + a for a in range(2 * n)},
        compiler_params=_cparams(has_side_effects=DATAFLOW),
    )(*[_hbm(a) for a in fulls], *lands, after)
    return (out[0], out[1], list(out[2:2 + n]), list(out[2 + n:2 + 2 * n])), out[-1]


def _scatter_wait(started, axes, after, name):
    send_sems, recv_sems, fulls, lands = started
    n = len(fulls)

    def body(*refs):
        srcs, dsts, send_ref, recv_ref = refs[:n], refs[n:2 * n], refs[2 * n], refs[2 * n + 1]
        x, y, c, me = _my_position()
        for k, (peer, idx) in enumerate(_peers(x, y, c)):
            for a in range(n):
                _, theirs = _own_and_peer_rows(srcs[a], me, idx, axes[a])
                cp = pltpu.make_async_remote_copy(theirs, dsts[a].at[k], send_ref.at[k * n + a], recv_ref.at[k * n + a],
                                                  device_id=peer, device_id_type=MESH)
                cp.wait_send()
                cp.wait_recv()

    out = _call(
        body, name=name,
        out_shape=tuple(pltpu.HBM(a.shape, a.dtype) for a in (*fulls, *lands)),
        in_specs=[HBM_SPEC] * (2 * n) + [SEM_SPEC, SEM_SPEC, ANY_SPEC],
        out_specs=tuple([HBM_SPEC] * (2 * n)),
        input_output_aliases={a: a for a in range(2 * n)},
        compiler_params=_cparams(has_side_effects=DATAFLOW),
    )(*fulls, *lands, send_sems, recv_sems, after)
    return list(out[:n]), list(out[n:])


def _prep_weights(me, items, name):
    def body(me_ref, *refs):
        for (_, _, kind), src, dst in zip(items, refs[:len(items)], refs[len(items):]):
            dst[...] = (src[...].T if kind == "T" else src[...]).astype(BF16)

    ins, in_specs, out_shapes, out_specs = [], [], [], []
    for src, j, kind in items:
        shard = src.shape[1:]
        ins.append(src)
        in_specs.append(pl.BlockSpec((None,) + tuple(shard), lambda i, me_ref, j=j, nd=len(shard): (j,) + (0,) * nd))
        if kind == "G":
            out_shapes.append((shard[0], N_DEV * shard[1], shard[2]))
            out_specs.append(pl.BlockSpec(tuple(shard), lambda i, me_ref: (0, me_ref[0], 0)))
        else:
            rows = shard[1] if kind == "T" else shard[0]
            out_shapes.append((N_DEV * rows, D))
            out_specs.append(pl.BlockSpec((rows, D), lambda i, me_ref: (me_ref[0], 0)))
    out = _call(
        body, name=name,
        grid_spec=pltpu.PrefetchScalarGridSpec(num_scalar_prefetch=1, grid=(1,), in_specs=in_specs, out_specs=tuple(out_specs)),
        out_shape=tuple(jax.ShapeDtypeStruct(s, BF16) for s in out_shapes),
        compiler_params=_cparams(1),
    )(me.reshape(1), *ins)
    return list(out)


def _ada_forward(c_all, ada_w):
    cols = ada_w.shape[2]

    def body(c_ref, w_ref, o_ref):
        cv = c_ref[...]
        sc = (cv * _sigmoid(cv)).astype(BF16)
        o_ref[...] = _dot(sc, w_ref[...].astype(BF16))

    return _call(
        body, name="ada_forward", grid=(DEPTH,),
        out_shape=jax.ShapeDtypeStruct((DEPTH, N_DEV, cols), F32),
        in_specs=[pl.BlockSpec((N_DEV, D), lambda i: (0, 0)), pl.BlockSpec((None, D, cols), lambda i: (i, 0, 0))],
        out_specs=pl.BlockSpec((None, N_DEV, cols), lambda i: (i, 0, 0)),
        compiler_params=_cparams(1),
    )(c_all, ada_w)


def _ada_backward_adamw(c_pad, dmod_pad, w, m, v):
    cols = w.shape[2]

    def body(c_ref, dm_ref, w_ref, m_ref, v_ref, g_out, d_out, m_out, v_out):
        cv = c_ref[...]
        sc = (cv * _sigmoid(cv)).astype(BF16)
        g = _dot_tn(sc, dm_ref[...].astype(BF16))
        g_out[...] = g
        d_out[...], m_out[...], v_out[...] = _adamw(w_ref[...], g, m_ref[...], v_ref[...])

    wspec = pl.BlockSpec((None, D, cols), lambda i: (i, 0, 0))
    return _call(
        body, name="ada_backward_adamw", grid=(DEPTH,),
        out_shape=tuple(jax.ShapeDtypeStruct(w.shape, F32) for _ in range(4)),
        in_specs=[pl.BlockSpec((2 * N_DEV, D), lambda i: (0, 0)), pl.BlockSpec((None, 2 * N_DEV, cols), lambda i: (i, 0, 0)),
                  wspec, wspec, wspec],
        out_specs=(wspec, wspec, wspec, wspec),
        compiler_params=_cparams(1),
    )(c_pad, dmod_pad, w, m, v)


def _attn_in_proj(x, rope, ng, mod, w_t, j, gain, bd, tile):
    seq = x.shape[0]

    def body(x_ref, rope_ref, ng_ref, mod_ref, w_ref, gain_ref, bd_ref, qk_ref, qs_ref, kd_ref, vd_ref, g_ref):
        _, _, h = _norm_mod(x_ref[...], ng_ref[...], mod_ref[1:2, :], mod_ref[0:1, :])
        hb = h.astype(BF16)
        tabs = _rope_tabs(rope_ref)
        low = _low_half(tile)
        bdm = bd_ref[...]

        def put_kv(ref, blk, first_kv):
            sw = pltpu.roll(blk, HEAD_DIM, 1)
            ref[:, LANES * first_kv:LANES * (first_kv + 1)] = jnp.where(low, blk, sw).astype(BF16)
            ref[:, LANES * (first_kv + 1):LANES * (first_kv + 2)] = jnp.where(low, sw, blk).astype(BF16)

        def project(c):
            return _dot_nt(hb, w_ref[CHUNK * c:CHUNK * (c + 1), :])

        n_chunks = ATTN_IN // CHUNK
        per = CHUNK // LANES
        nxt = project(0)
        for c in range(n_chunks):
            cur = nxt
            if c + 1 < n_chunks:
                nxt = project(c + 1)
            col = CHUNK * c
            if col >= QK_W + N_KV * HEAD_DIM:
                g_ref[:, col - QK_W - N_KV * HEAD_DIM:col - QK_W - N_KV * HEAD_DIM + CHUNK] = cur.astype(BF16)
            elif col >= QK_W:
                for t in range(per):
                    put_kv(vd_ref, cur[:, LANES * t:LANES * (t + 1)], (col - QK_W) // HEAD_DIM + 2 * t)
            else:
                qk_ref[:, col:col + CHUNK] = cur
                for t in range(per):
                    b = per * c + t
                    blk = cur[:, LANES * t:LANES * (t + 1)]
                    ms = _dot_split(blk * blk, bdm) * (1.0 / HEAD_DIM)
                    y = (blk * lax.rsqrt(ms + NORM_EPS)) * gain_ref[:, LANES * b:LANES * (b + 1)]
                    rp = _rope(y, tabs)
                    if b < D // LANES:
                        rp = rp * (HEAD_DIM ** -0.5)
                        qs_ref[:, 2 * LANES * b:2 * LANES * b + LANES] = jnp.where(low, rp, 0.0).astype(BF16)
                        qs_ref[:, 2 * LANES * b + LANES:2 * LANES * (b + 1)] = jnp.where(low, 0.0, rp).astype(BF16)
                    else:
                        put_kv(kd_ref, rp, 2 * (b - D // LANES))

    row = lambda w: pl.BlockSpec((tile, w), lambda i: (i, 0))
    return _call(
        body, name=f"attn_in_proj_{j}", grid=(seq // tile,),
        out_shape=(jax.ShapeDtypeStruct((seq, QK_W), F32), jax.ShapeDtypeStruct((seq, N_HEADS * LANES), BF16),
                   jax.ShapeDtypeStruct((seq, KX_W), BF16), jax.ShapeDtypeStruct((seq, KX_W), BF16),
                   jax.ShapeDtypeStruct((seq, D), BF16)),
        in_specs=[row(D), row(3 * LANES), _const_spec((1, D)), _const_spec((8, D)), _const_spec((ATTN_IN, D)),
                  _const_spec((1, QK_W)), _const_spec((LANES, LANES))],
        out_specs=(row(QK_W), row(N_HEADS * LANES), row(KX_W), row(KX_W), row(D)),
        compiler_params=_cparams(1),
    )(x, rope, ng, mod, w_t, gain, bd)


def _band_mask(n, rows, keys_on_rows):
    shape = (2 * QBLK, rows) if keys_on_rows else (rows, 2 * QBLK)
    qi = lax.broadcasted_iota(jnp.int32, shape, 1 if keys_on_rows else 0) & (QBLK - 1)
    kj = lax.broadcasted_iota(jnp.int32, shape, 0 if keys_on_rows else 1)
    diff = QBLK + qi - kj
    first_key = jnp.where(n > 0, 0, QBLK)
    return (diff >= 0) & (diff < QBLK) & (kj >= first_key)


def _stack_heads(ref, heads):
    return jnp.concatenate([ref[:, LANES * h:LANES * (h + 1)] for h in heads], axis=0)


def _kv_block(prev_ref, cur_ref, kv):
    cols = slice(LANES * kv, LANES * (kv + 1))
    return jnp.concatenate([prev_ref[:, cols], cur_ref[:, cols]], axis=0)


def _pair_up(st, low):
    return jnp.concatenate([jnp.where(low, st[0:QBLK], st[QBLK:2 * QBLK]),
                            jnp.where(low, st[2 * QBLK:3 * QBLK], st[3 * QBLK:4 * QBLK])], axis=1)


def _attn_forward(sinks, qs, kd, vd, j):
    seq = qs.shape[0]
    nb = seq // QBLK

    def body(sink_ref, q_ref, kp_ref, kc_ref, vp_ref, vc_ref, o_ref):
        n = pl.program_id(0)
        ok = _band_mask(n, 4 * QBLK, False)
        low = _low_half(QBLK)
        rowi = lax.broadcasted_iota(jnp.int32, (4 * QBLK, 1), 0)

        def scores(kv):
            return _dot_nt(_stack_heads(q_ref, range(4 * kv, 4 * kv + 4)), _kv_block(kp_ref, kc_ref, kv))

        nxt = scores(0)
        for kv in range(N_KV):
            s = jnp.where(ok, nxt, -1e30)
            if kv + 1 < N_KV:
                nxt = scores(kv + 1)
            sink = jnp.where(rowi < QBLK, sink_ref[4 * kv],
                             jnp.where(rowi < 2 * QBLK, sink_ref[4 * kv + 1],
                                       jnp.where(rowi < 3 * QBLK, sink_ref[4 * kv + 2], sink_ref[4 * kv + 3])))
            m = jnp.maximum(jnp.max(s, axis=1, keepdims=True), sink)
            p = jnp.exp(s - m)
            den = jnp.sum(p, axis=1, keepdims=True) + jnp.exp(sink - m)
            o_st = _dot((p / den).astype(BF16), _kv_block(vp_ref, vc_ref, kv))
            o_ref[:, 2 * LANES * kv:2 * LANES * (kv + 1)] = _pair_up(o_st, low).astype(BF16)

    blk = lambda w: pl.BlockSpec((QBLK, w), lambda n: (n, 0))
    prev = lambda w: pl.BlockSpec((QBLK, w), lambda n: (jnp.maximum(n - 1, 0), 0))
    return _call(
        body, name=f"attn_forward_{j}", grid=(nb,),
        out_shape=jax.ShapeDtypeStruct((seq, D), BF16),
        in_specs=[pl.BlockSpec(memory_space=pltpu.SMEM), blk(N_HEADS * LANES), prev(KX_W), blk(KX_W), prev(KX_W), blk(KX_W)],
        out_specs=blk(D),
        compiler_params=_cparams(1),
    )(sinks, qs, kd, kd, vd, vd)


def _attn_out_proj(x, o, g, w, j, mod, tile):
    seq = x.shape[0]

    def body(x_ref, o_ref, g_ref, w_ref, mod_ref, xo_ref, br_ref):
        gv = g_ref[...].astype(F32)
        u = (o_ref[...].astype(F32) * (gv * _sigmoid(gv))).astype(BF16)
        br = _dot(u, w_ref[...])
        br_ref[...] = br.astype(BF16)
        xo_ref[...] = x_ref[...] + mod_ref[2:3, :] * br

    row = pl.BlockSpec((tile, D), lambda i: (i, 0))
    return _call(
        body, name=f"attn_out_proj_{j}", grid=(seq // tile,),
        out_shape=(jax.ShapeDtypeStruct((seq, D), F32), jax.ShapeDtypeStruct((seq, D), BF16)),
        in_specs=[row, row, row, _const_spec((D, D)), _const_spec((8, D))],
        out_specs=(row, row),
        compiler_params=_cparams(1),
    )(x, o, g, w, mod)


def _attn_out_proj_bwd(dxn, br, o, g, w, j, mod, tile):
    seq = dxn.shape[0]
    steps = seq // tile

    def body(dxn_ref, br_ref, o_ref, g_ref, w_ref, mod_ref, do_ref, dg_ref, dw_ref, dgate_ref, dw_acc):
        i = pl.program_id(0)

        @pl.when(i == 0)
        def _():
            dw_acc[...] = jnp.zeros_like(dw_acc)
            dgate_ref[...] = jnp.zeros_like(dgate_ref)

        dxn_v, ov, gv = dxn_ref[...], o_ref[...].astype(F32), g_ref[...].astype(F32)
        dgate_ref[...] += jnp.sum(dxn_v * br_ref[...].astype(F32), axis=0, keepdims=True)
        dbr = (dxn_v * mod_ref[2:3, :]).astype(BF16)
        du = _dot_nt(dbr, w_ref[...])
        sg = _sigmoid(gv)
        sl = gv * sg
        dw_acc[...] += _dot_tn((ov * sl).astype(BF16), dbr)
        do = du * sl
        dg_ref[...] = (du * ov * (sg * (1.0 + gv * (1.0 - sg)))).astype(BF16)
        low = _low_half(tile)
        for b in range(D // LANES):
            blk = do[:, LANES * b:LANES * (b + 1)]
            do_ref[:, 2 * LANES * b:2 * LANES * b + LANES] = jnp.where(low, blk, 0.0).astype(BF16)
            do_ref[:, 2 * LANES * b + LANES:2 * LANES * (b + 1)] = jnp.where(low, 0.0, blk).astype(BF16)

        @pl.when(i == steps - 1)
        def _():
            dw_ref[...] = dw_acc[...].astype(BF16)

    row = lambda w_: pl.BlockSpec((tile, w_), lambda i: (i, 0))
    return _call(
        body, name=f"attn_out_proj_bwd_{j}", grid=(steps,),
        out_shape=(jax.ShapeDtypeStruct((seq, N_HEADS * LANES), BF16), jax.ShapeDtypeStruct((seq, D), BF16),
                   jax.ShapeDtypeStruct((D, D), BF16), jax.ShapeDtypeStruct((1, D), F32)),
        in_specs=[row(D), row(D), row(D), row(D), _const_spec((D, D)), _const_spec((8, D))],
        out_specs=(row(N_HEADS * LANES), row(D), pl.BlockSpec((D, D), lambda i: (0, 0)),
                   pl.BlockSpec((1, D), lambda i: (0, 0))),
        scratch_shapes=[pltpu.VMEM((D, D), F32)],
        compiler_params=_cparams(1),
    )(dxn, br, o, g, w, mod)


def _attn_backward(sinks, qs, dos, kd, vd, j):
    seq = qs.shape[0]
    nb = seq // QBLK

    def body(sink_ref, q_ref, do_ref, kp_ref, kc_ref, vp_ref, vc_ref, dq_ref, dk_ref, dv_ref, dsink_ref,
             carry_k, carry_v, sink_acc):
        n = pl.program_id(0)

        @pl.when(n == 0)
        def _():
            carry_k[...] = jnp.zeros_like(carry_k)
            carry_v[...] = jnp.zeros_like(carry_v)
            sink_acc[...] = jnp.zeros_like(sink_acc)

        @pl.when(n < nb)
        def _():
            ok = _band_mask(n, 2 * QBLK, True)
            low = _low_half(QBLK)
            lane_q = lax.broadcasted_iota(jnp.int32, (1, 2 * QBLK), 1)
            dk_parts, dv_parts = [], []

            def first_products(g):
                kv, half = divmod(g, 2)
                heads = (4 * kv + half, 4 * kv + 2 + half)
                q = _stack_heads(q_ref, heads)
                do = _stack_heads(do_ref, heads)
                kk = _kv_block(kp_ref, kc_ref, kv)
                return heads, q, do, kk, _dot_nt(kk, q), _dot_nt(_kv_block(vp_ref, vc_ref, kv), do)

            nxt = first_products(0)
            dq_h, dk_kv, dv_kv = [], None, None
            for g in range(2 * N_KV):
                heads, q, do, kk, s_raw, dp_raw = nxt
                if g + 1 < 2 * N_KV:
                    nxt = first_products(g + 1)
                st = jnp.where(ok, s_raw, -1e30)
                sink = jnp.where(lane_q < QBLK, sink_ref[heads[0]], sink_ref[heads[1]])
                m = jnp.maximum(jnp.max(st, axis=0, keepdims=True), sink)
                e = jnp.exp(st - m)
                e_sink = jnp.exp(sink - m)
                inv = 1.0 / (jnp.sum(e, axis=0, keepdims=True) + e_sink)
                p = e * inv
                pdp = p * dp_raw
                delta = jnp.sum(pdp, axis=0, keepdims=True)
                ds = (pdp - p * delta).astype(BF16)
                sink_acc[g:g + 1, :] -= e_sink * inv * delta
                dk_g, dv_g = _dot(ds, q), _dot(p.astype(BF16), do)
                dk_kv = dk_g if dk_kv is None else dk_kv + dk_g
                dv_kv = dv_g if dv_kv is None else dv_kv + dv_g
                dq_h.append(_dot_tn(ds, kk))
                if g % 2 == 1:
                    kv = g // 2
                    for t in range(2):
                        dq_ref[:, LANES * (2 * kv + t):LANES * (2 * kv + t + 1)] = jnp.where(
                            low, dq_h[0][QBLK * t:QBLK * (t + 1)], dq_h[1][QBLK * t:QBLK * (t + 1)])
                    dk_parts.append(dk_kv + pltpu.roll(dk_kv, HEAD_DIM, 1))
                    dv_parts.append(dv_kv + pltpu.roll(dv_kv, HEAD_DIM, 1))
                    dq_h, dk_kv, dv_kv = [], None, None

            def order(parts, lo, hi):
                return jnp.concatenate([jnp.where(low, parts[0][lo:hi], parts[1][lo:hi]),
                                        jnp.where(low, parts[2][lo:hi], parts[3][lo:hi])], axis=1)

            dk_ref[...] = carry_k[...] + order(dk_parts, 0, QBLK)
            dv_ref[...] = (carry_v[...] + order(dv_parts, 0, QBLK)).astype(BF16)
            carry_k[...] = order(dk_parts, QBLK, 2 * QBLK)
            carry_v[...] = order(dv_parts, QBLK, 2 * QBLK)

        @pl.when(n == nb)
        def _():
            dk_ref[...] = carry_k[...]
            dv_ref[...] = carry_v[...].astype(BF16)
            lane = lax.broadcasted_iota(jnp.int32, (1, LANES), 1)
            out = jnp.zeros((1, LANES), F32)
            for g in range(2 * N_KV):
                for t in range(2):
                    tot = jnp.sum(sink_acc[g:g + 1, QBLK * t:QBLK * (t + 1)], axis=1, keepdims=True)
                    out = jnp.where(lane == 4 * (g // 2) + 2 * t + g % 2, tot, out)
            dsink_ref[...] = out

    cur = lambda w: pl.BlockSpec((QBLK, w), lambda n: (jnp.minimum(n, nb - 1), 0))
    prev = lambda w: pl.BlockSpec((QBLK, w), lambda n: (jnp.maximum(n - 1, 0), 0))
    return _call(
        body, name=f"attn_backward_{j}", grid=(nb + 1,),
        out_shape=(jax.ShapeDtypeStruct((seq, D), F32), jax.ShapeDtypeStruct((seq, N_KV * HEAD_DIM), F32),
                   jax.ShapeDtypeStruct((seq, N_KV * HEAD_DIM), BF16), jax.ShapeDtypeStruct((1, LANES), F32)),
        in_specs=[pl.BlockSpec(memory_space=pltpu.SMEM), cur(N_HEADS * LANES), cur(N_HEADS * LANES), prev(KX_W), cur(KX_W),
                  prev(KX_W), cur(KX_W)],
        out_specs=(cur(D), prev(N_KV * HEAD_DIM), prev(N_KV * HEAD_DIM), pl.BlockSpec((1, LANES), lambda n: (0, 0))),
        scratch_shapes=[pltpu.VMEM((QBLK, N_KV * HEAD_DIM), F32), pltpu.VMEM((QBLK, N_KV * HEAD_DIM), F32),
                        pltpu.VMEM((2 * N_KV, 2 * QBLK), F32)],
        compiler_params=_cparams(1),
    )(sinks, qs, dos, kd, kd, vd, vd)


def _in_proj_tail(x_ref, dxn_ref, ng_ref, mod_ref, w_ref, dproj, dx_ref, dw_acc, vec_acc):
    ng, sc, sh = ng_ref[...], mod_ref[1:2, :], mod_ref[0:1, :]
    xh, r, h = _norm_mod(x_ref[...], ng, sc, sh)
    dh = _dot(dproj, w_ref[...])
    dw_acc[...] += _dot_tn(dproj, h.astype(BF16))
    vec_acc[0:1, :] += jnp.sum(dh, axis=0, keepdims=True)
    vec_acc[1:2, :] += jnp.sum(dh * xh, axis=0, keepdims=True)
    dxh = dh * (ng * (1.0 + sc))
    dx_ref[...] = dxn_ref[...] + r * (dxh - xh * jnp.mean(dxh * xh, axis=-1, keepdims=True))


def _tail_finish(ng_ref, mod_ref, dw_ref, vec_ref, dw_acc, vec_acc):
    dw_ref[...] = dw_acc[...].astype(BF16)
    a = vec_acc[1:2, :]
    vec_ref[...] = jnp.zeros_like(vec_ref)
    vec_ref[0:1, :] = vec_acc[0:1, :]
    vec_ref[1:2, :] = a * ng_ref[...]
    vec_ref[3:4, :] = a * (1.0 + mod_ref[1:2, :])


def _attn_in_proj_bwd(x, dxn, rope, qk_raw, dq, dk, dv, dg, ng, mod, w_t, j, gain, bd, tile):
    seq = x.shape[0]
    steps = seq // tile

    def body(x_ref, dxn_ref, rope_ref, qk_ref, dq_ref, dk_ref, dv_ref, dg_ref, ng_ref, mod_ref, w_ref, gain_ref,
             bd_ref, dx_ref, dw_ref, vec_ref, dgain_ref, dproj, dw_acc, vec_acc):
        i = pl.program_id(0)

        @pl.when(i == 0)
        def _():
            dw_acc[...] = jnp.zeros_like(dw_acc)
            vec_acc[...] = jnp.zeros_like(vec_acc)
            dgain_ref[...] = jnp.zeros_like(dgain_ref)

        tabs = _rope_tabs(rope_ref)
        bdm = bd_ref[...]
        for b in range(QK_W // LANES):
            cols = slice(LANES * b, LANES * (b + 1))
            raw = qk_ref[:, cols]
            if b < D // LANES:
                dy = dq_ref[:, cols] * (HEAD_DIM ** -0.5)
            else:
                dy = dk_ref[:, LANES * (b - D // LANES):LANES * (b + 1 - D // LANES)]
            dy = _rope_bwd(dy, tabs)
            rr = lax.rsqrt(_dot_split(raw * raw, bdm) * (1.0 / HEAD_DIM) + NORM_EPS)
            xh = raw * rr
            dgain_ref[:, cols] += jnp.sum(dy * xh, axis=0, keepdims=True)
            dxh = dy * gain_ref[:, cols]
            dproj[:, cols] = (rr * (dxh - xh * (_dot_split(dxh * xh, bdm) * (1.0 / HEAD_DIM)))).astype(BF16)
        dproj[:, QK_W:QK_W + N_KV * HEAD_DIM] = dv_ref[...]
        dproj[:, QK_W + N_KV * HEAD_DIM:] = dg_ref[...]
        _in_proj_tail(x_ref, dxn_ref, ng_ref, mod_ref, w_ref, dproj[...], dx_ref, dw_acc, vec_acc)

        @pl.when(i == steps - 1)
        def _():
            _tail_finish(ng_ref, mod_ref, dw_ref, vec_ref, dw_acc, vec_acc)

    row = lambda w, dt=None: pl.BlockSpec((tile, w), lambda i: (i, 0))
    fixed = lambda shape: pl.BlockSpec(shape, lambda i: (0,) * len(shape))
    return _call(
        body, name=f"attn_in_proj_bwd_{j}", grid=(steps,),
        out_shape=(jax.ShapeDtypeStruct((seq, D), F32), jax.ShapeDtypeStruct((ATTN_IN, D), BF16),
                   jax.ShapeDtypeStruct((8, D), F32), jax.ShapeDtypeStruct((1, QK_W), F32)),
        in_specs=[row(D), row(D), row(3 * LANES), row(QK_W), row(D), row(N_KV * HEAD_DIM), row(N_KV * HEAD_DIM), row(D),
                  _const_spec((1, D)), _const_spec((8, D)), _const_spec((ATTN_IN, D)), _const_spec((1, QK_W)),
                  _const_spec((LANES, LANES))],
        out_specs=(row(D), fixed((ATTN_IN, D)), fixed((8, D)), fixed((1, QK_W))),
        scratch_shapes=[pltpu.VMEM((tile, ATTN_IN), BF16), pltpu.VMEM((ATTN_IN, D), F32), pltpu.VMEM((8, D), F32)],
        compiler_params=_cparams(1),
    )(x, dxn, rope, qk_raw, dq, dk, dv, dg, ng, mod, w_t, gain, bd)


def _pool_in_proj(x, ng, mod, w_t, j, tile):
    seq = x.shape[0]

    def body(x_ref, ng_ref, mod_ref, w_ref, v_ref, g_ref):
        _, _, h = _norm_mod(x_ref[...], ng_ref[...], mod_ref[1:2, :], mod_ref[0:1, :])
        proj = _dot_nt(h.astype(BF16), w_ref[...])
        v_ref[...] = proj[:, :D].astype(BF16)
        g_ref[...] = proj[:, D:].astype(BF16)

    row = pl.BlockSpec((tile, D), lambda i: (i, 0))
    return _call(
        body, name=f"pool_in_proj_{j}", grid=(seq // tile,),
        out_shape=(jax.ShapeDtypeStruct((seq, D), BF16), jax.ShapeDtypeStruct((seq, D), BF16)),
        in_specs=[row, _const_spec((1, D)), _const_spec((8, D)), _const_spec((POOL_IN, D))],
        out_specs=(row, row),
        compiler_params=_cparams(1),
    )(x, ng, mod, w_t)


PAD = 8


def _window_sums(ext, lo, hi, forward):
    gw = D // len(POOL_WINDOWS)
    planes = []
    for gi, w in enumerate(POOL_WINDOWS):
        cols = slice(gw * gi, gw * (gi + 1))
        src, k = 0, 1
        while k < w:
            d = k if forward else -k
            ext[1 - src, lo:hi, cols] = ext[src, lo:hi, cols] + ext[src, lo + d:hi + d, cols]
            src, k = 1 - src, 2 * k
        planes.append(src)
    return planes


def _pooled(ext, v_ref, first, tile):
    t_abs = first + lax.broadcasted_iota(jnp.int32, (tile, 1), 0)
    top = PAD + HALO
    planes = _window_sums(ext, PAD, top + tile, False)
    outs = []
    gw = D // len(POOL_WINDOWS)
    for gi, w in enumerate(POOL_WINDOWS):
        cols = slice(gw * gi, gw * (gi + 1))
        cnt = jnp.minimum(t_abs + 1, w).astype(F32)
        outs.append(ext[planes[gi], top:top + tile, cols] / cnt - v_ref[:, cols].astype(F32))
    return jnp.concatenate(outs, axis=1)


def _fill_ext(ext, halo_ref, v_ref, i, tile):
    ext[0, 0:PAD, :] = jnp.zeros((PAD, D), F32)
    ext[1, 0:PAD, :] = jnp.zeros((PAD, D), F32)
    ext[0, PAD:PAD + HALO, :] = jnp.where(i == 0, 0.0, halo_ref[...].astype(F32))
    ext[0, PAD + HALO:PAD + HALO + tile, :] = v_ref[...].astype(F32)


def _group_mix(pb, wg_ref):
    gw = D // len(POOL_WINDOWS)
    return jnp.concatenate([_dot(pb[:, gw * gi:gw * (gi + 1)], wg_ref[gi]) for gi in range(len(POOL_WINDOWS))], axis=1)


def _pool_mix_out(x, v, g, wg, w_out, j, scale, mod, tile, target=None):
    seq = x.shape[0]

    def body(*refs):
        if target is None:
            x_ref, v_ref, halo_ref, g_ref, wg_ref, w_ref, scale_ref, mod_ref, xo_ref, br_ref, ext = refs
        else:
            x_ref, v_ref, halo_ref, g_ref, wg_ref, w_ref, scale_ref, mod_ref, t_ref, xo_ref, br_ref, loss_ref, ext = refs
        i = pl.program_id(0)
        _fill_ext(ext, halo_ref, v_ref, i, tile)
        pb = _pooled(ext, v_ref, i * tile, tile).astype(BF16)
        ms = _group_mix(pb, wg_ref) * scale_ref[...]
        gv = g_ref[...].astype(F32)
        u = (ms * (gv * _sigmoid(gv))).astype(BF16)
        br = _dot(u, w_ref[...])
        br_ref[...] = br.astype(BF16)
        y = x_ref[...] + mod_ref[2:3, :] * br
        if target is None:
            xo_ref[...] = y
        else:
            @pl.when(i == 0)
            def _():
                loss_ref[...] = jnp.zeros_like(loss_ref)

            e = y - t_ref[...]
            xo_ref[...] = e * (1.0 / D)
            loss_ref[...] += 0.5 * jnp.sum(jnp.mean(e * e, axis=-1, keepdims=True), axis=0, keepdims=True)

    row = pl.BlockSpec((tile, D), lambda i: (i, 0))
    halo = pl.BlockSpec((HALO, D), lambda i: (jnp.maximum(i * (tile // HALO) - 1, 0), 0))
    extra_in, extra_out, extra_shape = ([], (), ()) if target is None else (
        [row], (pl.BlockSpec((1, LANES), lambda i: (0, 0)),), (jax.ShapeDtypeStruct((1, LANES), F32),))
    return _call(
        body, name=f"pool_mix_out_{j}", grid=(seq // tile,),
        out_shape=(jax.ShapeDtypeStruct((seq, D), F32), jax.ShapeDtypeStruct((seq, D), BF16)) + extra_shape,
        in_specs=[row, row, halo, row, _const_spec(wg.shape), _const_spec((D, D)), _const_spec((1, D)),
                  _const_spec((8, D))] + extra_in,
        out_specs=(row, row) + extra_out,
        scratch_shapes=[pltpu.VMEM((2, tile + HALO + PAD, D), F32)],
        compiler_params=_cparams(1),
    )(x, v, v, g, wg, w_out, scale, mod, *(() if target is None else (target,)))


def _pool_mix_out_bwd(dxn, br, v, g, wg, w_out, j, scale, mod, tile):
    seq = dxn.shape[0]
    steps = seq // tile
    ng_ = len(POOL_WINDOWS)
    gw = D // ng_

    def body(dxn_ref, br_ref, v_ref, halo_ref, g_ref, wg_ref, w_ref, scale_ref, mod_ref,
             dpool_ref, dg_ref, dw_ref, dwg_ref, vec_ref, ext, dw_acc, dwg_acc):
        i = pl.program_id(0)

        @pl.when(i == 0)
        def _():
            dw_acc[...] = jnp.zeros_like(dw_acc)
            dwg_acc[...] = jnp.zeros_like(dwg_acc)
            vec_ref[...] = jnp.zeros_like(vec_ref)

        _fill_ext(ext, halo_ref, v_ref, i, tile)
        pb = _pooled(ext, v_ref, i * tile, tile).astype(BF16)
        mixed = _group_mix(pb, wg_ref)
        scale = scale_ref[...]
        ms = mixed * scale
        gv, dxn_v = g_ref[...].astype(F32), dxn_ref[...]
        sg = _sigmoid(gv)
        sl = gv * sg
        vec_ref[0:1, :] += jnp.sum(dxn_v * br_ref[...].astype(F32), axis=0, keepdims=True)
        dbr = (dxn_v * mod_ref[2:3, :]).astype(BF16)
        du = _dot_nt(dbr, w_ref[...])
        dw_acc[...] += _dot_tn((ms * sl).astype(BF16), dbr)
        dms = du * sl
        dg_ref[...] = (du * ms * (sg * (1.0 + gv * (1.0 - sg)))).astype(BF16)
        vec_ref[1:2, :] += jnp.sum(dms * mixed, axis=0, keepdims=True)
        dmx = (dms * scale).astype(BF16)
        for gi in range(ng_):
            cols = slice(gw * gi, gw * (gi + 1))
            dpool_ref[:, cols] = _dot_nt(dmx[:, cols], wg_ref[gi])
            dwg_acc[gi] += _dot_tn(pb[:, cols], dmx[:, cols])

        @pl.when(i == steps - 1)
        def _():
            dw_ref[...] = dw_acc[...].astype(BF16)
            dwg_ref[...] = dwg_acc[...].astype(BF16)

    row = pl.BlockSpec((tile, D), lambda i: (i, 0))
    halo = pl.BlockSpec((HALO, D), lambda i: (jnp.maximum(i * (tile // HALO) - 1, 0), 0))
    fixed = lambda shape: pl.BlockSpec(shape, lambda i: (0,) * len(shape))
    return _call(
        body, name=f"pool_mix_out_bwd_{j}", grid=(steps,),
        out_shape=(jax.ShapeDtypeStruct((seq, D), F32), jax.ShapeDtypeStruct((seq, D), BF16),
                   jax.ShapeDtypeStruct((D, D), BF16), jax.ShapeDtypeStruct((ng_, gw, gw), BF16),
                   jax.ShapeDtypeStruct((8, D), F32)),
        in_specs=[row, row, row, halo, row, _const_spec(wg.shape), _const_spec((D, D)), _const_spec((1, D)),
                  _const_spec((8, D))],
        out_specs=(row, row, fixed((D, D)), fixed((ng_, gw, gw)), fixed((8, D))),
        scratch_shapes=[pltpu.VMEM((2, tile + HALO + PAD, D), F32), pltpu.VMEM((D, D), F32), pltpu.VMEM((ng_, gw, gw), F32)],
        compiler_params=_cparams(1),
    )(dxn, br, v, v, g, wg, w_out, scale, mod)


def _pool_in_proj_bwd(x, dxn, dpool, dg, ng, mod, w_t, j, tile):
    seq = x.shape[0]
    steps = seq // tile
    gw = D // len(POOL_WINDOWS)

    def body(x_ref, dxn_ref, dp_ref, halo_ref, dg_ref, ng_ref, mod_ref, w_ref, dx_ref, dw_ref, vec_ref,
             ext, dproj, dw_acc, vec_acc):
        i = pl.program_id(0)

        @pl.when(i == 0)
        def _():
            dw_acc[...] = jnp.zeros_like(dw_acc)
            vec_acc[...] = jnp.zeros_like(vec_acc)

        t_abs = i * tile + lax.broadcasted_iota(jnp.int32, (tile, 1), 0)
        last = i == steps - 1
        ext[0, tile + HALO:tile + HALO + PAD, :] = jnp.zeros((PAD, D), F32)
        ext[1, tile + HALO:tile + HALO + PAD, :] = jnp.zeros((PAD, D), F32)
        for gi, w in enumerate(POOL_WINDOWS):
            cols = slice(gw * gi, gw * (gi + 1))
            cnt = jnp.minimum(t_abs + 1, w).astype(F32)
            ext[0, 0:tile, cols] = dp_ref[:, cols] / cnt
            ext[0, tile:tile + HALO, cols] = jnp.where(last, 0.0, halo_ref[:, cols] * (1.0 / w))
        planes = _window_sums(ext, 0, tile + HALO, True)
        for gi, w in enumerate(POOL_WINDOWS):
            cols = slice(gw * gi, gw * (gi + 1))
            dproj[:, cols] = (ext[planes[gi], 0:tile, cols] - dp_ref[:, cols]).astype(BF16)
        dproj[:, D:] = dg_ref[...]
        _in_proj_tail(x_ref, dxn_ref, ng_ref, mod_ref, w_ref, dproj[...], dx_ref, dw_acc, vec_acc)

        @pl.when(last)
        def _():
            _tail_finish(ng_ref, mod_ref, dw_ref, vec_ref, dw_acc, vec_acc)

    row = pl.BlockSpec((tile, D), lambda i: (i, 0))
    halo = pl.BlockSpec((HALO, D), lambda i: (jnp.minimum((i + 1) * (tile // HALO), seq // HALO - 1), 0))
    fixed = lambda shape: pl.BlockSpec(shape, lambda i: (0,) * len(shape))
    return _call(
        body, name=f"pool_in_proj_bwd_{j}", grid=(steps,),
        out_shape=(jax.ShapeDtypeStruct((seq, D), F32), jax.ShapeDtypeStruct((POOL_IN, D), BF16),
                   jax.ShapeDtypeStruct((8, D), F32)),
        in_specs=[row, row, row, halo, row, _const_spec((1, D)), _const_spec((8, D)), _const_spec((POOL_IN, D))],
        out_specs=(row, fixed((POOL_IN, D)), fixed((8, D))),
        scratch_shapes=[pltpu.VMEM((2, tile + HALO + PAD, D), F32), pltpu.VMEM((tile, POOL_IN), BF16), pltpu.VMEM((POOL_IN, D), F32),
                        pltpu.VMEM((8, D), F32)],
        compiler_params=_cparams(1),
    )(x, dxn, dpool, dpool, dg, ng, mod, w_t)


def _build_vec(vecs, gates, pool_vecs, gains, dsinks, loss_part):
    def body(v0, v1, v2, v3, g0, g2, p0, p1, n0, n1, s0, s1, loss_ref, out):
        out[...] = jnp.zeros_like(out)
        for i, v in enumerate((v0, v1, v2, v3)):
            out[3 * i:3 * i + 2, :] = v[0:2, :]
            out[12 + i:13 + i, :] = v[3:4, :]
        out[2:3, :] = g0[...]
        out[8:9, :] = g2[...]
        for j, (p, n, s) in enumerate(((p0, n0, s0), (p1, n1, s1))):
            out[3 * (2 * j + 1) + 2:3 * (2 * j + 1) + 3, :] = p[0:1, :]
            out[22 + j:23 + j, :] = p[1:2, :]
            out[16 + j:17 + j, :] = n[:, 0:D]
            out[18 + j:19 + j, 0:QK_W - D] = n[:, D:QK_W]
            out[20 + j:21 + j, 0:LANES] = s[...]
        out[24:25, 0:LANES] = loss_ref[...]

    vm = pl.BlockSpec(memory_space=pltpu.VMEM)
    args = (*vecs, gates[0], gates[2], *pool_vecs, *gains, *dsinks, loss_part)
    return _call(
        body, name="build_vec",
        out_shape=jax.ShapeDtypeStruct((VEC_ROWS, D), F32),
        in_specs=[vm] * len(args), out_specs=vm,
        compiler_params=_cparams(),
    )(*args)


def _sum_devices(g, after):
    rows = g.shape[1]

    def body(g_ref, after_ref, tot_ref, fold_ref):
        tot = g_ref[0]
        for p in range(1, N_DEV):
            tot = tot + g_ref[p]
        tot_ref[...] = tot
        f = tot[16:24, 0:LANES]
        for b in range(1, D // LANES):
            f = f + tot[16:24, LANES * b:LANES * (b + 1)]
        fold_ref[...] = f + pltpu.roll(f, HEAD_DIM, 1)

    return _call(
        body, name="sum_devices",
        out_shape=(jax.ShapeDtypeStruct((rows, D), F32), jax.ShapeDtypeStruct((8, LANES), F32)),
        in_specs=[pl.BlockSpec(memory_space=pltpu.VMEM), ANY_SPEC],
        out_specs=(pl.BlockSpec(memory_space=pltpu.VMEM), pl.BlockSpec(memory_space=pltpu.VMEM)),
        compiler_params=_cparams(),
    )(g, after)


def _adamw_small(params):
    n = len(params)

    def body(*refs):
        ins, outs = refs[:4 * n], refs[4 * n:]
        for p in range(n):
            w_ref, g_ref, m_ref, v_ref = ins[4 * p:4 * p + 4]
            outs[3 * p][...], outs[3 * p + 1][...], outs[3 * p + 2][...] = _adamw(w_ref[...], g_ref[...], m_ref[...], v_ref[...])

    vm = pl.BlockSpec(memory_space=pltpu.VMEM)
    out = _call(
        body, name="adamw_small",
        out_shape=tuple(jax.ShapeDtypeStruct(w.shape, F32) for (w, _, _, _) in params for _ in range(3)),
        in_specs=[vm] * (4 * n), out_specs=tuple([vm] * (3 * n)),
        compiler_params=_cparams(),
    )(*[a for p in params for a in p])
    return [tuple(out[3 * p:3 * p + 3]) for p in range(n)]


def _adamw_shards(name, me, fulls, lands, w, m, v, transpose, axis=0):
    nl = w.shape[0]
    wshape = w.shape[1:]
    own_shape = lands[0].shape[1:]

    def body(me_ref, *refs):
        own_refs, land_refs = refs[:nl], refs[nl:2 * nl]
        w_ref, m_ref, v_ref, g_out, d_out, m_out, v_out = refs[2 * nl:]
        layer = pl.program_id(0)
        for l in range(nl):
            @pl.when(layer == l)
            def _(l=l):
                g = own_refs[l][...].astype(F32)
                for k in range(N_DEV - 1):
                    g = g + land_refs[l][k].astype(F32)
                if transpose:
                    g = g.T
                g_out[...] = g
                d_out[...], m_out[...], v_out[...] = _adamw(w_ref[...], g, m_ref[...], v_ref[...])

    def own_index(l_, me_ref):
        idx = [0] * len(own_shape)
        idx[axis] = me_ref[0]
        return tuple(idx)

    own_spec = pl.BlockSpec(tuple(own_shape), own_index)
    land_spec = pl.BlockSpec((N_DEV - 1,) + tuple(own_shape), lambda l_, me_ref: (0,) * (1 + len(own_shape)))
    wspec = pl.BlockSpec((None,) + tuple(wshape), lambda l_, me_ref: (l_,) + (0,) * len(wshape))
    return _call(
        body, name=name,
        grid_spec=pltpu.PrefetchScalarGridSpec(num_scalar_prefetch=1, grid=(nl,),
                                               in_specs=[own_spec] * nl + [land_spec] * nl + [wspec] * 3,
                                               out_specs=(wspec,) * 4),
        out_shape=tuple(jax.ShapeDtypeStruct(w.shape, F32) for _ in range(4)),
        compiler_params=_cparams(1),
    )(me.reshape(1), *fulls, *lands, w, m, v)


def _constants():
    lane = np.arange(LANES)
    bd = (lane[:, None] // HEAD_DIM == lane[None, :] // HEAD_DIM).astype(np.float32)
    half = ROT_DIM // 2
    inv_freq = ROPE_THETA ** (-jnp.arange(half, dtype=F32) * 2.0 / ROT_DIM)
    invf = jnp.tile(inv_freq, LANES // half).reshape(1, LANES)
    return jnp.asarray(bd, BF16), invf


def kernel(x, c, positions, ada_w, ada_b, norm_g, attn_w_in, attn_q_norm, attn_k_norm, attn_sinks, attn_w_out, pool_w_in, pool_w_group, pool_scale, pool_w_out, loss_target, m_ada_w, m_ada_b, m_norm_g, m_attn_w_in, m_attn_q_norm, m_attn_k_norm, m_attn_sinks, m_attn_w_out, m_pool_w_in, m_pool_w_group, m_pool_scale, m_pool_w_out, v_ada_w, v_ada_b, v_norm_g, v_attn_w_in, v_attn_q_norm, v_attn_k_norm, v_attn_sinks, v_attn_w_out, v_pool_w_in, v_pool_w_group, v_pool_scale, v_pool_w_out):
    seq = x.shape[1]
    me = 4 * lax.axis_index("x") + 2 * lax.axis_index("y") + lax.axis_index("c")
    bd, invf = _constants()
    t_mm = min(512, seq)
    rope = _rope_table(positions.reshape(seq, 1), invf, t_mm)
    t_bw = min(256, seq)
    shard = pool_scale.shape[1]
    cols = ada_w.shape[2]

    w_first, = _prep_weights(me, [(attn_w_in, 0, "T")], "prep_first")
    first_w, token = _gather_first_start(w_first, c)
    prepped = _prep_weights(me, [(attn_w_out, 0, "N"), (pool_w_in, 0, "T"), (pool_w_out, 0, "N"), (pool_w_group, 0, "G"),
                                 (attn_w_in, 1, "T"), (attn_w_out, 1, "N"), (pool_w_in, 1, "T"), (pool_w_out, 1, "N"),
                                 (pool_w_group, 1, "G")], "prep_rest")

    first = jnp.concatenate([c, jnp.pad(pool_scale, ((0, 0), (0, D - shard))), jnp.zeros((5, D), F32)], axis=0)
    first = _allgather_small(first + token[0, 0], "allgather_c", rope)
    c_all = first[:, 0, :]
    scale_full = jnp.transpose(first[:, 1:3, :shard], (1, 0, 2)).reshape(2, D)
    mod_part = _ada_forward(c_all, ada_w)
    mod_all = _allgather_small(mod_part.reshape(DEPTH * N_DEV, cols), "allgather_mod", prepped[0])
    mod_all = mod_all.reshape(N_DEV, DEPTH, N_DEV, cols)
    mine = lax.dynamic_index_in_dim(mod_all, me, axis=2, keepdims=False)
    mod = jnp.transpose(mine, (1, 0, 2)).reshape(DEPTH, 3 * D) + ada_b
    mod = jnp.pad(mod.reshape(DEPTH, 3, D), ((0, 0), (0, 5), (0, 0)))

    groups = [prepped[0:1], prepped[1:4], prepped[4:6], prepped[6:9]]
    gaxes = [(0,), (0,), (0, 0, 1), (0, 0), (0, 0, 1)]
    first_w, token = _gather_first_forward(first_w, mod)
    rest, token = _gather_start(groups, gaxes[1:], token, "gather_start_rest")
    started = [None] + rest

    saved, weights = [], []
    h = x[0]
    for i in range(DEPTH):
        j = i // 2
        s = dict(x=h, ng=norm_g[i:i + 1], md=mod[i])
        if i == 0:
            w_in_t = _gather_first_wait(first_w, token)
        else:
            wts = _gather_wait(started[i + 1], gaxes[i + 1], h, f"gather_wait_{i}")
        if i % 2 == 0:
            if i > 0:
                w_in_t, w_out = wts
            s["gain"] = jnp.concatenate([jnp.tile(attn_q_norm[j], N_HEADS), jnp.tile(attn_k_norm[j], N_KV)]).reshape(1, QK_W)
            s["qk_raw"], s["qs"], s["kd"], s["vd"], s["g"] = _attn_in_proj(
                h, rope, s["ng"], s["md"], w_in_t, j, s["gain"], bd, t_bw)
            s["o"] = _attn_forward(attn_sinks[j], s["qs"], s["kd"], s["vd"], j)
            if i == 0:
                w_out, = _gather_wait(started[1], gaxes[1], s["o"], "gather_wait_0_out")
            h, s["br"] = _attn_out_proj(h, s["o"], s["g"], w_out, j, s["md"], t_mm)
            weights.append((w_in_t, w_out))
        else:
            p_in_t, p_out, p_grp = wts
            s["scale"] = scale_full[j:j + 1]
            s["v"], s["g"] = _pool_in_proj(h, s["ng"], s["md"], p_in_t, j, t_mm)
            if i < DEPTH - 1:
                h, s["br"] = _pool_mix_out(h, s["v"], s["g"], p_grp, p_out, j, s["scale"], s["md"], t_mm)
            else:
                dx, s["br"], loss_part = _pool_mix_out(h, s["v"], s["g"], p_grp, p_out, j, s["scale"], s["md"], t_mm,
                                                       loss_target[0])
            weights.append(wts)
        saved.append(s)

    vecs, gates, gains, dsinks, pool_vecs = [None] * DEPTH, [None] * DEPTH, [None] * 2, [None] * 2, [None] * 2
    sent_in, sent_out = [None] * DEPTH, [None] * DEPTH
    token = jnp.zeros((8, LANES), F32)
    for i in reversed(range(DEPTH)):
        j = i // 2
        s = saved[i]
        md = s["md"] + token[0, 0]
        if i % 2 == 0:
            w_in_t, w_out = weights[i]
            dos, dg, d_w_out, gates[i] = _attn_out_proj_bwd(dx, s["br"], s["o"], s["g"], w_out, j, md, t_mm)
            sent_out[i], token = _scatter_start([d_w_out], (0,), f"scatter_start_{i}_out", token)
            dq, dk, dv, dsinks[j] = _attn_backward(attn_sinks[j] + token[0, 0], s["qs"], dos, s["kd"], s["vd"], j)
            dx, d_in_t, vecs[i], gains[j] = _attn_in_proj_bwd(
                s["x"], dx, rope, s["qk_raw"], dq, dk, dv, dg, s["ng"], md, w_in_t, j, s["gain"], bd, t_bw)
        else:
            p_in_t, p_out, p_grp = weights[i]
            dpool, dg, d_p_out, d_p_grp, pool_vecs[j] = _pool_mix_out_bwd(
                dx, s["br"], s["v"], s["g"], p_grp, p_out, j, s["scale"], md, t_mm)
            sent_out[i], token = _scatter_start([d_p_out, d_p_grp], (0, 1), f"scatter_start_{i}_out", token)
            dx, d_in_t, vecs[i] = _pool_in_proj_bwd(s["x"], dx, dpool, dg, s["ng"], s["md"] + token[0, 0], p_in_t, j, t_bw)
        if i > 0:
            sent_in[i], token = _scatter_start([d_in_t], (0,), f"scatter_start_{i}_in", token)

    vec = _build_vec(vecs, gates, pool_vecs, gains, dsinks, loss_part)
    vec_rows = lax.dynamic_update_slice(jnp.zeros((N_DEV * VEC_ROWS, D), F32), vec, (me * VEC_ROWS, 0))
    vec_sent, token = _gather_start([[vec_rows]], [(0,)], loss_part, "vec_gather_start")
    sent_in[0], token = _scatter_start([d_in_t], (0,), "scatter_start_0_in", token)

    got_in, got_out = [None] * DEPTH, [None] * DEPTH
    for i in (3, 1):
        got_out[i] = _scatter_wait(sent_out[i], (0, 1), token, f"scatter_wait_{i}_out")
        got_in[i] = _scatter_wait(sent_in[i], (0,), token, f"scatter_wait_{i}_in")
    pick = lambda got, ls, a: ([got[i][0][a] for i in ls], [got[i][1][a] for i in ls])
    res = {}
    res["pool_w_in"] = _adamw_shards("adamw_pool_w_in", me, *pick(got_in, (1, 3), 0), pool_w_in, m_pool_w_in, v_pool_w_in, True)
    res["pool_w_out"] = _adamw_shards("adamw_pool_w_out", me, *pick(got_out, (1, 3), 0), pool_w_out, m_pool_w_out,
                                      v_pool_w_out, False)
    res["pool_w_group"] = _adamw_shards("adamw_pool_w_group", me, *pick(got_out, (1, 3), 1), pool_w_group, m_pool_w_group,
                                        v_pool_w_group, False, axis=1)

    vec_all, = _gather_wait(vec_sent[0], (0,), res["pool_w_group"][0], "vec_gather_wait")
    vec_all = vec_all.reshape(N_DEV, VEC_ROWS, D)
    tot, folded = _sum_devices(vec_all, token)
    loss = tot[24, 0]
    small = dict(
        ada_b=(ada_b, tot[0:12].reshape(DEPTH, 3 * D), m_ada_b, v_ada_b),
        norm_g=(norm_g, tot[12:16], m_norm_g, v_norm_g),
        q_norm=(attn_q_norm, folded[0:2, :HEAD_DIM], m_attn_q_norm, v_attn_q_norm),
        k_norm=(attn_k_norm, folded[2:4, :HEAD_DIM], m_attn_k_norm, v_attn_k_norm),
        sinks=(attn_sinks, tot[20:22, :N_HEADS], m_attn_sinks, v_attn_sinks),
        pool_scale=(pool_scale, lax.dynamic_slice(tot, (22, me * shard), (2, shard)), m_pool_scale, v_pool_scale),
    )
    res.update({k: (a[1],) + upd for (k, a), upd in zip(small.items(), _adamw_small(list(small.values())))})

    dmod_all = vec_all[:, 0:12, :].reshape(N_DEV, DEPTH, 3 * D)
    dmod_mine = lax.dynamic_slice_in_dim(dmod_all, me * cols, cols, axis=2)
    dmod_mine = jnp.pad(jnp.transpose(dmod_mine, (1, 0, 2)), ((0, 0), (0, N_DEV), (0, 0))) + token[0, 0]
    res["ada_w"] = _ada_backward_adamw(jnp.pad(c_all, ((0, N_DEV), (0, 0))), dmod_mine, ada_w, m_ada_w, v_ada_w)

    for i in (2, 0):
        got_out[i] = _scatter_wait(sent_out[i], (0,), res["ada_w"][0], f"scatter_wait_{i}_out")
        got_in[i] = _scatter_wait(sent_in[i], (0,), res["ada_w"][0], f"scatter_wait_{i}_in")
    res["attn_w_out"] = _adamw_shards("adamw_attn_w_out", me, *pick(got_out, (0, 2), 0), attn_w_out, m_attn_w_out,
                                      v_attn_w_out, False)
    res["attn_w_in"] = _adamw_shards("adamw_attn_w_in", me, *pick(got_in, (0, 2), 0), attn_w_in, m_attn_w_in, v_attn_w_in, True)

    order = ("ada_w", "ada_b", "norm_g", "attn_w_in", "q_norm", "k_norm", "sinks", "attn_w_out", "pool_w_in",
             "pool_w_group", "pool_scale", "pool_w_out")
    return (loss, dx[None], *[res[k][0] for k in order], *[res[k][1] for k in order], *[res[k][2] for k in order],
            *[res[k][3] for k in order])
```

```python
import functools

import numpy as np
import jax
import jax.numpy as jnp
from jax import lax
from jax.experimental import pallas as pl
from jax.experimental.pallas import tpu as pltpu

F32 = jnp.float32
BF16 = jnp.bfloat16
MESH = pl.DeviceIdType.MESH

N_DEV = 8
D = 1024
DEPTH = 4
HEAD_DIM = 64
N_HEADS = 16
N_KV = 4
QK_W = 1280
ATTN_IN = 2560
POOL_IN = 2048
QBLK = 128
KX_W = N_KV * 128
CHUNK = 256
POOL_WINDOWS = (2, 4, 8, 16)
HALO = 16
ROPE_THETA = 500000.0
ROT_DIM = 16
NORM_EPS = 1e-6
ADAM_LR = 0.001
ADAM_B1 = 0.9
ADAM_B2 = 0.999
ADAM_EPS = 1e-08
ADAM_WD = 0.01
ADAM_STEP = 10

LANES = 128
VMEM_LIMIT = 56 * 2**20
VEC_ROWS = 32


def _cparams(n_grid=0, **kw):
    if n_grid:
        kw["dimension_semantics"] = ("arbitrary",) * n_grid
    return pltpu.CompilerParams(vmem_limit_bytes=VMEM_LIMIT, **kw)


def _call(body, **kw):
    return pl.pallas_call(body, **kw)


def _mod_spec(layer):
    return pl.BlockSpec((None, 8, D), lambda *_: (layer, 0, 0), pipeline_mode=pl.Buffered(1))


def _mod_row_spec(layer, row):
    return pl.BlockSpec((None, None, 1, D), lambda *_: (layer, row, 0, 0), pipeline_mode=pl.Buffered(1))


NORM_ROW, POOL_SCALE_ROW = 3, 4


def _const_spec(shape):
    nd = len(shape)
    return pl.BlockSpec(shape, lambda *_: (0,) * nd, pipeline_mode=pl.Buffered(1))


def _dot(a, b):
    return jnp.dot(a, b, preferred_element_type=F32)


def _dot_nt(a, b):
    return lax.dot_general(a, b, (((1,), (1,)), ((), ())), preferred_element_type=F32)


def _dot_tn(a, b):
    return lax.dot_general(a, b, (((0,), (0,)), ((), ())), preferred_element_type=F32)


def _dot_split(x, m):
    hi = x.astype(BF16)
    lo = (x - hi.astype(F32)).astype(BF16)
    return _dot(hi, m) + _dot(lo, m)


def _sigmoid(g):
    return 1.0 / (1.0 + jnp.exp(-g))


def _norm_mod(x, ng, sc, sh):
    r = lax.rsqrt(jnp.mean(x * x, axis=-1, keepdims=True) + NORM_EPS)
    xh = x * r
    h = (xh * ng) * (1.0 + sc) + sh
    return xh, r, h


def _rope_table(pos_col, invf_row, tile):
    seq = pos_col.shape[0]

    def body(pos_ref, invf_ref, out_ref):
        ang = pos_ref[...].astype(F32) * invf_ref[...]
        l64 = lax.broadcasted_iota(jnp.int32, (tile, LANES), 1) & (HEAD_DIM - 1)
        cs, sn = jnp.cos(ang), jnp.sin(ang)
        out_ref[:, 0:LANES] = jnp.where(l64 < ROT_DIM, cs, 1.0)
        out_ref[:, LANES:2 * LANES] = jnp.where(l64 < ROT_DIM // 2, -sn, 0.0)
        out_ref[:, 2 * LANES:3 * LANES] = jnp.where((l64 >= ROT_DIM // 2) & (l64 < ROT_DIM), sn, 0.0)

    return _call(
        body, name="rope_table", grid=(seq // tile,),
        out_shape=jax.ShapeDtypeStruct((seq, 3 * LANES), F32),
        in_specs=[pl.BlockSpec((tile, 1), lambda i: (i, 0)), _const_spec((1, LANES))],
        out_specs=pl.BlockSpec((tile, 3 * LANES), lambda i: (i, 0)),
        compiler_params=_cparams(1),
    )(pos_col, invf_row)


def _rope_tabs(rope_ref):
    return rope_ref[:, 0:LANES], rope_ref[:, LANES:2 * LANES], rope_ref[:, 2 * LANES:3 * LANES]


def _rope(y, tabs):
    cos_t, sin_a, sin_b = tabs
    return y * cos_t + pltpu.roll(y, LANES - ROT_DIM // 2, 1) * sin_a + pltpu.roll(y, ROT_DIM // 2, 1) * sin_b


def _rope_bwd(dy, tabs):
    cos_t, sin_a, sin_b = tabs
    return dy * cos_t + pltpu.roll(dy * sin_a, ROT_DIM // 2, 1) + pltpu.roll(dy * sin_b, LANES - ROT_DIM // 2, 1)


def _low_half(rows):
    return lax.broadcasted_iota(jnp.int32, (rows, LANES), 1) < HEAD_DIM


def _adamw(w, g, m, v):
    m = ADAM_B1 * m + (1.0 - ADAM_B1) * g
    v = ADAM_B2 * v + (1.0 - ADAM_B2) * (g * g)
    m_hat = m / (1.0 - ADAM_B1 ** ADAM_STEP)
    v_hat = v / (1.0 - ADAM_B2 ** ADAM_STEP)
    delta = -ADAM_LR * (m_hat / (jnp.sqrt(v_hat) + ADAM_EPS) + ADAM_WD * w)
    return delta, m, v


def _my_position():
    x, y, c = lax.axis_index("x"), lax.axis_index("y"), lax.axis_index("c")
    return x, y, c, 4 * x + 2 * y + c


def _peers(x, y, c):
    out = []
    for k in range(1, N_DEV):
        px = 1 - x if k & 4 else x
        py = 1 - y if k & 2 else y
        pc = 1 - c if k & 1 else c
        out.append(((px, py, pc), 4 * px + 2 * py + pc))
    return out


def _allgather_small(v, name, after):
    rows, cols = v.shape

    def body(v_ref, after_ref, out_ref, send_sems, recv_sems, local_sem):
        x, y, c, me = _my_position()
        local = pltpu.make_async_copy(v_ref, out_ref.at[me], local_sem)
        local.start()
        sends = []
        for k, (peer, _) in enumerate(_peers(x, y, c)):
            cp = pltpu.make_async_remote_copy(v_ref, out_ref.at[me], send_sems.at[k], recv_sems.at[k],
                                              device_id=peer, device_id_type=MESH)
            cp.start()
            sends.append(cp)
        for k, (peer, idx) in enumerate(_peers(x, y, c)):
            pltpu.make_async_remote_copy(v_ref, out_ref.at[idx], send_sems.at[k], recv_sems.at[k],
                                         device_id=peer, device_id_type=MESH).wait_recv()
        for cp in sends:
            cp.wait_send()
        local.wait()

    return _call(
        body, name=name,
        out_shape=jax.ShapeDtypeStruct((N_DEV, rows, cols), F32),
        in_specs=[pl.BlockSpec(memory_space=pltpu.VMEM), pl.BlockSpec(memory_space=pl.ANY)],
        out_specs=pl.BlockSpec(memory_space=pltpu.VMEM),
        scratch_shapes=[pltpu.SemaphoreType.DMA((N_DEV - 1,)), pltpu.SemaphoreType.DMA((N_DEV - 1,)),
                        pltpu.SemaphoreType.DMA(())],
        compiler_params=_cparams(),
    )(v, after)


def _shard_rows(ref, idx, rows, axis):
    sl = [slice(None)] * len(ref.shape)
    sl[axis] = pl.ds(idx * rows, rows)
    return ref.at[tuple(sl)]


def _own_and_peer_rows(ref, me, idx, axis):
    rows = ref.shape[axis] // N_DEV
    return _shard_rows(ref, me, rows, axis), _shard_rows(ref, idx, rows, axis)


HBM_SPEC = pl.BlockSpec(memory_space=pltpu.HBM)
SEM_SPEC = pl.BlockSpec(memory_space=pltpu.SEMAPHORE)
ANY_SPEC = pl.BlockSpec(memory_space=pl.ANY)
DATAFLOW = pltpu.SideEffectType.DATAFLOW_SIDE_EFFECTING


def _hbm(a):
    return pltpu.with_memory_space_constraint(a, pltpu.HBM)


def _gather_start(layers, axes, after, name):
    flat = [a for arrs in layers for a in arrs]
    flat_axes = [ax for axs in axes for ax in axs]
    n, nl = len(flat), len(layers)

    def body(*refs):
        ins, sems, token = refs[:n], refs[n + 1:n + 1 + 2 * nl], refs[-1]
        x, y, c, me = _my_position()
        a0 = 0
        for li, arrs in enumerate(layers):
            for k, (peer, _) in enumerate(_peers(x, y, c)):
                for a in range(len(arrs)):
                    rows, _ = _own_and_peer_rows(ins[a0 + a], me, me, flat_axes[a0 + a])
                    pltpu.make_async_remote_copy(rows, rows, sems[2 * li].at[k * len(arrs) + a],
                                                 sems[2 * li + 1].at[k * len(arrs) + a],
                                                 device_id=peer, device_id_type=MESH).start()
            a0 += len(arrs)
        token[...] = jnp.zeros_like(token)

    sem_shapes = []
    for arrs in layers:
        sem_shapes += [pltpu.SemaphoreType.DMA(((N_DEV - 1) * len(arrs),))] * 2
    out = _call(
        body, name=name,
        out_shape=(*sem_shapes, *[pltpu.HBM(a.shape, a.dtype) for a in flat], jax.ShapeDtypeStruct((8, LANES), F32)),
        in_specs=[HBM_SPEC] * n + [ANY_SPEC],
        out_specs=(*[SEM_SPEC] * (2 * nl), *[HBM_SPEC] * n, pl.BlockSpec(memory_space=pltpu.VMEM)),
        input_output_aliases={a: 2 * nl + a for a in range(n)},
        compiler_params=_cparams(has_side_effects=DATAFLOW),
    )(*[_hbm(a) for a in flat], after)
    per_layer, a0 = [], 0
    for li, arrs in enumerate(layers):
        per_layer.append((out[2 * li], out[2 * li + 1], list(out[2 * nl + a0:2 * nl + a0 + len(arrs)])))
        a0 += len(arrs)
    return per_layer, out[-1]


def _gather_wait(started, axes, after, name):
    send_sems, recv_sems, arrs = started
    n = len(arrs)

    def body(*refs):
        ins, send_ref, recv_ref = refs[:n], refs[n], refs[n + 1]
        x, y, c, me = _my_position()
        for k, (peer, idx) in enumerate(_peers(x, y, c)):
            for a in range(n):
                own, theirs = _own_and_peer_rows(ins[a], me, idx, axes[a])
                cp = pltpu.make_async_remote_copy(own, theirs, send_ref.at[k * n + a], recv_ref.at[k * n + a],
                                                  device_id=peer, device_id_type=MESH)
                cp.wait_send()
                cp.wait_recv()

    return _call(
        body, name=name,
        out_shape=tuple(pltpu.HBM(a.shape, a.dtype) for a in arrs),
        in_specs=[HBM_SPEC] * n + [SEM_SPEC, SEM_SPEC, ANY_SPEC],
        out_specs=tuple([HBM_SPEC] * n),
        input_output_aliases={a: a for a in range(n)},
        compiler_params=_cparams(has_side_effects=DATAFLOW),
    )(*arrs, send_sems, recv_sems, after)


def _first_relations(x, y, c):
    return [(x, y, 1 - c), (1 - x, y, c), (x, 1 - y, c), (1 - x, 1 - y, c)]


def _gather_first_start(arr, after):
    n_rel = 4

    def body(a_ref, after_ref, send_ref, recv_ref, thru, token):
        x, y, c, me = _my_position()
        rows, _ = _own_and_peer_rows(a_ref, me, me, 0)
        for k, peer in enumerate(_first_relations(x, y, c)):
            pltpu.make_async_remote_copy(rows, rows, send_ref.at[k], recv_ref.at[k], device_id=peer, device_id_type=MESH).start()
        token[...] = jnp.zeros_like(token)

    sem = pltpu.SemaphoreType.DMA((n_rel,))
    out = _call(
        body, name="gather_first_start",
        out_shape=(sem, sem, pltpu.HBM(arr.shape, arr.dtype), jax.ShapeDtypeStruct((8, LANES), F32)),
        in_specs=[HBM_SPEC, ANY_SPEC],
        out_specs=(SEM_SPEC, SEM_SPEC, HBM_SPEC, pl.BlockSpec(memory_space=pltpu.VMEM)),
        input_output_aliases={0: 2},
        compiler_params=_cparams(has_side_effects=DATAFLOW),
    )(_hbm(arr), after)
    return out[:3], out[3]


def _gather_first_forward(started, after):
    send_a, recv_a, arr = started

    def body(a_ref, send_a_ref, recv_a_ref, after_ref, send_b_ref, recv_b_ref, thru, token):
        x, y, c, me = _my_position()
        sibling = (x, y, 1 - c)
        for k, peer in enumerate(_first_relations(x, y, c)):
            own, theirs = _own_and_peer_rows(a_ref, me, 4 * peer[0] + 2 * peer[1] + peer[2], 0)
            cp = pltpu.make_async_remote_copy(own, theirs, send_a_ref.at[k], recv_a_ref.at[k], device_id=peer, device_id_type=MESH)
            cp.wait_send()
            cp.wait_recv()
            if k > 0:
                pltpu.make_async_remote_copy(theirs, theirs, send_b_ref.at[k - 1], recv_b_ref.at[k - 1],
                                             device_id=sibling, device_id_type=MESH).start()
        token[...] = jnp.zeros_like(token)

    sem = pltpu.SemaphoreType.DMA((3,))
    out = _call(
        body, name="gather_first_forward",
        out_shape=(sem, sem, pltpu.HBM(arr.shape, arr.dtype), jax.ShapeDtypeStruct((8, LANES), F32)),
        in_specs=[HBM_SPEC, SEM_SPEC, SEM_SPEC, ANY_SPEC],
        out_specs=(SEM_SPEC, SEM_SPEC, HBM_SPEC, pl.BlockSpec(memory_space=pltpu.VMEM)),
        input_output_aliases={0: 2},
        compiler_params=_cparams(has_side_effects=DATAFLOW),
    )(arr, send_a, recv_a, after)
    return out[:3], out[3]


def _gather_first_wait(forwarded, after):
    send_b, recv_b, arr = forwarded

    def body(a_ref, send_b_ref, recv_b_ref, after_ref, thru):
        x, y, c, me = _my_position()
        sibling = (x, y, 1 - c)
        for k, peer in enumerate(_first_relations(x, y, c)[1:]):
            _, sent = _own_and_peer_rows(a_ref, me, 4 * peer[0] + 2 * peer[1] + peer[2], 0)
            _, got = _own_and_peer_rows(a_ref, me, 4 * peer[0] + 2 * peer[1] + (1 - peer[2]), 0)
            cp = pltpu.make_async_remote_copy(sent, got, send_b_ref.at[k], recv_b_ref.at[k], device_id=sibling, device_id_type=MESH)
            cp.wait_send()
            cp.wait_recv()

    return _call(
        body, name="gather_first_wait",
        out_shape=pltpu.HBM(arr.shape, arr.dtype),
        in_specs=[HBM_SPEC, SEM_SPEC, SEM_SPEC, ANY_SPEC],
        out_specs=HBM_SPEC,
        input_output_aliases={0: 0},
        compiler_params=_cparams(has_side_effects=DATAFLOW),
    )(arr, send_b, recv_b, after)


def _scatter_start(fulls, axes, name, after):
    n = len(fulls)
    lands = []
    for f, ax in zip(fulls, axes):
        shp = list(f.shape)
        shp[ax] //= N_DEV
        lands.append(_hbm(lax.empty((N_DEV - 1,) + tuple(shp), f.dtype)))

    def body(*refs):
        srcs, dsts, send_ref, recv_ref, token = refs[:n], refs[n:2 * n], refs[2 * n + 1], refs[2 * n + 2], refs[-1]
        x, y, c, me = _my_position()
        for k, (peer, idx) in enumerate(_peers(x, y, c)):
            for a in range(n):
                _, theirs = _own_and_peer_rows(srcs[a], me, idx, axes[a])
                pltpu.make_async_remote_copy(theirs, dsts[a].at[k], send_ref.at[k * n + a], recv_ref.at[k * n + a],
                                             device_id=peer, device_id_type=MESH).start()
        token[...] = jnp.zeros_like(token)

    sem = pltpu.SemaphoreType.DMA(((N_DEV - 1) * n,))
    out = _call(
        body, name=name,
        out_shape=(sem, sem, *[pltpu.HBM(a.shape, a.dtype) for a in fulls], *[pltpu.HBM(a.shape, a.dtype) for a in lands],
                   jax.ShapeDtypeStruct((8, LANES), F32)),
        in_specs=[HBM_SPEC] * (2 * n) + [ANY_SPEC],
        out_specs=(SEM_SPEC, SEM_SPEC, *[HBM_SPEC] * (2 * n), pl.BlockSpec(memory_space=pltpu.VMEM)),
        input_output_aliases={a: 2 + a for a in range(2 * n)},
        compiler_params=_cparams(has_side_effects=DATAFLOW),
    )(*[_hbm(a) for a in fulls], *lands, after)
    return (out[0], out[1], list(out[2:2 + n]), list(out[2 + n:2 + 2 * n])), out[-1]


def _scatter_wait(started, axes, after, name):
    send_sems, recv_sems, fulls, lands = started
    n = len(fulls)

    def body(*refs):
        srcs, dsts, send_ref, recv_ref = refs[:n], refs[n:2 * n], refs[2 * n], refs[2 * n + 1]
        x, y, c, me = _my_position()
        for k, (peer, idx) in enumerate(_peers(x, y, c)):
            for a in range(n):
                _, theirs = _own_and_peer_rows(srcs[a], me, idx, axes[a])
                cp = pltpu.make_async_remote_copy(theirs, dsts[a].at[k], send_ref.at[k * n + a], recv_ref.at[k * n + a],
                                                  device_id=peer, device_id_type=MESH)
                cp.wait_send()
                cp.wait_recv()

    out = _call(
        body, name=name,
        out_shape=tuple(pltpu.HBM(a.shape, a.dtype) for a in (*fulls, *lands)),
        in_specs=[HBM_SPEC] * (2 * n) + [SEM_SPEC, SEM_SPEC, ANY_SPEC],
        out_specs=tuple([HBM_SPEC] * (2 * n)),
        input_output_aliases={a: a for a in range(2 * n)},
        compiler_params=_cparams(has_side_effects=DATAFLOW),
    )(*fulls, *lands, send_sems, recv_sems, after)
    return list(out[:n]), list(out[n:])


def _prep_weights(me, items, name):
    def body(me_ref, *refs):
        for (_, _, kind), src, dst in zip(items, refs[:len(items)], refs[len(items):]):
            dst[...] = (src[...].T if kind == "T" else src[...]).astype(BF16)

    ins, in_specs, out_shapes, out_specs = [], [], [], []
    for src, j, kind in items:
        shard = src.shape[1:]
        ins.append(src)
        in_specs.append(pl.BlockSpec((None,) + tuple(shard), lambda i, me_ref, j=j, nd=len(shard): (j,) + (0,) * nd))
        if kind == "G":
            out_shapes.append((shard[0], N_DEV * shard[1], shard[2]))
            out_specs.append(pl.BlockSpec(tuple(shard), lambda i, me_ref: (0, me_ref[0], 0)))
        else:
            rows = shard[1] if kind == "T" else shard[0]
            out_shapes.append((N_DEV * rows, D))
            out_specs.append(pl.BlockSpec((rows, D), lambda i, me_ref: (me_ref[0], 0)))
    out = _call(
        body, name=name,
        grid_spec=pltpu.PrefetchScalarGridSpec(num_scalar_prefetch=1, grid=(1,), in_specs=in_specs, out_specs=tuple(out_specs)),
        out_shape=tuple(jax.ShapeDtypeStruct(s, BF16) for s in out_shapes),
        compiler_params=_cparams(1),
    )(me.reshape(1), *ins)
    return list(out)


def _ada_forward(c_all, ada_w):
    cols = ada_w.shape[2]

    def body(c_ref, w_ref, o_ref):
        cv = c_ref[...]
        sc = (cv * _sigmoid(cv)).astype(BF16)
        o_ref[...] = _dot(sc, w_ref[...].astype(BF16))

    return _call(
        body, name="ada_forward", grid=(DEPTH,),
        out_shape=jax.ShapeDtypeStruct((DEPTH, N_DEV, cols), F32),
        in_specs=[pl.BlockSpec((N_DEV, D), lambda i: (0, 0)), pl.BlockSpec((None, D, cols), lambda i: (i, 0, 0))],
        out_specs=pl.BlockSpec((None, N_DEV, cols), lambda i: (i, 0, 0)),
        compiler_params=_cparams(1),
    )(c_all, ada_w)


def _ada_backward_adamw(c_pad, dmod_pad, w, m, v):
    cols = w.shape[2]

    def body(c_ref, dm_ref, w_ref, m_ref, v_ref, g_out, d_out, m_out, v_out):
        cv = c_ref[...]
        sc = (cv * _sigmoid(cv)).astype(BF16)
        g = _dot_tn(sc, dm_ref[...].astype(BF16))
        g_out[...] = g
        d_out[...], m_out[...], v_out[...] = _adamw(w_ref[...], g, m_ref[...], v_ref[...])

    wspec = pl.BlockSpec((None, D, cols), lambda i: (i, 0, 0))
    return _call(
        body, name="ada_backward_adamw", grid=(DEPTH,),
        out_shape=tuple(jax.ShapeDtypeStruct(w.shape, F32) for _ in range(4)),
        in_specs=[pl.BlockSpec((2 * N_DEV, D), lambda i: (0, 0)), pl.BlockSpec((None, 2 * N_DEV, cols), lambda i: (i, 0, 0)),
                  wspec, wspec, wspec],
        out_specs=(wspec, wspec, wspec, wspec),
        compiler_params=_cparams(1),
    )(c_pad, dmod_pad, w, m, v)


def _attn_in_proj(x, rope, rows, mod, layer, w_t, j, gain, bd, tile):
    seq = x.shape[0]

    def body(x_ref, rope_ref, ng_ref, mod_ref, w_ref, gain_ref, bd_ref, qk_ref, qs_ref, kd_ref, vd_ref, g_ref):
        _, _, h = _norm_mod(x_ref[...], ng_ref[...], mod_ref[1:2, :], mod_ref[0:1, :])
        hb = h.astype(BF16)
        tabs = _rope_tabs(rope_ref)
        low = _low_half(tile)
        bdm = bd_ref[...]

        def put_kv(ref, blk, first_kv):
            sw = pltpu.roll(blk, HEAD_DIM, 1)
            ref[:, LANES * first_kv:LANES * (first_kv + 1)] = jnp.where(low, blk, sw).astype(BF16)
            ref[:, LANES * (first_kv + 1):LANES * (first_kv + 2)] = jnp.where(low, sw, blk).astype(BF16)

        def project(c):
            return _dot_nt(hb, w_ref[CHUNK * c:CHUNK * (c + 1), :])

        n_chunks = ATTN_IN // CHUNK
        per = CHUNK // LANES
        nxt = project(0)
        for c in range(n_chunks):
            cur = nxt
            if c + 1 < n_chunks:
                nxt = project(c + 1)
            col = CHUNK * c
            if col >= QK_W + N_KV * HEAD_DIM:
                g_ref[:, col - QK_W - N_KV * HEAD_DIM:col - QK_W - N_KV * HEAD_DIM + CHUNK] = cur.astype(BF16)
            elif col >= QK_W:
                for t in range(per):
                    put_kv(vd_ref, cur[:, LANES * t:LANES * (t + 1)], (col - QK_W) // HEAD_DIM + 2 * t)
            else:
                qk_ref[:, col:col + CHUNK] = cur
                for t in range(per):
                    b = per * c + t
                    blk = cur[:, LANES * t:LANES * (t + 1)]
                    ms = _dot_split(blk * blk, bdm) * (1.0 / HEAD_DIM)
                    y = (blk * lax.rsqrt(ms + NORM_EPS)) * gain_ref[:, LANES * b:LANES * (b + 1)]
                    rp = _rope(y, tabs)
                    if b < D // LANES:
                        rp = rp * (HEAD_DIM ** -0.5)
                        qs_ref[:, 2 * LANES * b:2 * LANES * b + LANES] = jnp.where(low, rp, 0.0).astype(BF16)
                        qs_ref[:, 2 * LANES * b + LANES:2 * LANES * (b + 1)] = jnp.where(low, 0.0, rp).astype(BF16)
                    else:
                        put_kv(kd_ref, rp, 2 * (b - D // LANES))

    row = lambda w: pl.BlockSpec((tile, w), lambda i: (i, 0))
    return _call(
        body, name=f"attn_in_proj_{j}", grid=(seq // tile,),
        out_shape=(jax.ShapeDtypeStruct((seq, QK_W), F32), jax.ShapeDtypeStruct((seq, N_HEADS * LANES), BF16),
                   jax.ShapeDtypeStruct((seq, KX_W), BF16), jax.ShapeDtypeStruct((seq, KX_W), BF16),
                   jax.ShapeDtypeStruct((seq, D), BF16)),
        in_specs=[row(D), row(3 * LANES), _mod_row_spec(layer, NORM_ROW), _mod_spec(layer), _const_spec((ATTN_IN, D)),
                  _const_spec((1, QK_W)), _const_spec((LANES, LANES))],
        out_specs=(row(QK_W), row(N_HEADS * LANES), row(KX_W), row(KX_W), row(D)),
        compiler_params=_cparams(1),
    )(x, rope, rows, mod, w_t, gain, bd)


def _band_mask(n, rows, keys_on_rows):
    shape = (2 * QBLK, rows) if keys_on_rows else (rows, 2 * QBLK)
    qi = lax.broadcasted_iota(jnp.int32, shape, 1 if keys_on_rows else 0) & (QBLK - 1)
    kj = lax.broadcasted_iota(jnp.int32, shape, 0 if keys_on_rows else 1)
    diff = QBLK + qi - kj
    first_key = jnp.where(n > 0, 0, QBLK)
    return (diff >= 0) & (diff < QBLK) & (kj >= first_key)


def _stack_heads(ref, heads):
    return jnp.concatenate([ref[:, LANES * h:LANES * (h + 1)] for h in heads], axis=0)


def _kv_block(prev_ref, cur_ref, kv):
    cols = slice(LANES * kv, LANES * (kv + 1))
    return jnp.concatenate([prev_ref[:, cols], cur_ref[:, cols]], axis=0)


def _pair_up(st, low):
    return jnp.concatenate([jnp.where(low, st[0:QBLK], st[QBLK:2 * QBLK]),
                            jnp.where(low, st[2 * QBLK:3 * QBLK], st[3 * QBLK:4 * QBLK])], axis=1)


def _attn_forward(sinks, qs, kd, vd, j):
    seq = qs.shape[0]
    nb = seq // QBLK

    def body(sink_ref, q_ref, kp_ref, kc_ref, vp_ref, vc_ref, o_ref):
        n = pl.program_id(0)
        ok = _band_mask(n, 4 * QBLK, False)
        low = _low_half(QBLK)
        rowi = lax.broadcasted_iota(jnp.int32, (4 * QBLK, 1), 0)

        def scores(kv):
            return _dot_nt(_stack_heads(q_ref, range(4 * kv, 4 * kv + 4)), _kv_block(kp_ref, kc_ref, kv))

        nxt = scores(0)
        for kv in range(N_KV):
            s = jnp.where(ok, nxt, -1e30)
            if kv + 1 < N_KV:
                nxt = scores(kv + 1)
            sink = jnp.where(rowi < QBLK, sink_ref[j, 4 * kv],
                             jnp.where(rowi < 2 * QBLK, sink_ref[j, 4 * kv + 1],
                                       jnp.where(rowi < 3 * QBLK, sink_ref[j, 4 * kv + 2], sink_ref[j, 4 * kv + 3])))
            m = jnp.maximum(jnp.max(s, axis=1, keepdims=True), sink)
            p = jnp.exp(s - m)
            den = jnp.sum(p, axis=1, keepdims=True) + jnp.exp(sink - m)
            o_st = _dot((p / den).astype(BF16), _kv_block(vp_ref, vc_ref, kv))
            o_ref[:, 2 * LANES * kv:2 * LANES * (kv + 1)] = _pair_up(o_st, low).astype(BF16)

    blk = lambda w: pl.BlockSpec((QBLK, w), lambda n: (n, 0))
    prev = lambda w: pl.BlockSpec((QBLK, w), lambda n: (jnp.maximum(n - 1, 0), 0))
    return _call(
        body, name=f"attn_forward_{j}", grid=(nb,),
        out_shape=jax.ShapeDtypeStruct((seq, D), BF16),
        in_specs=[pl.BlockSpec(memory_space=pltpu.SMEM), blk(N_HEADS * LANES), prev(KX_W), blk(KX_W), prev(KX_W), blk(KX_W)],
        out_specs=blk(D),
        compiler_params=_cparams(1),
    )(sinks, qs, kd, kd, vd, vd)


def _attn_out_proj(x, o, g, w, j, mod, layer, tile):
    seq = x.shape[0]

    def body(x_ref, o_ref, g_ref, w_ref, mod_ref, xo_ref, br_ref):
        gv = g_ref[...].astype(F32)
        u = (o_ref[...].astype(F32) * (gv * _sigmoid(gv))).astype(BF16)
        br = _dot(u, w_ref[...])
        br_ref[...] = br.astype(BF16)
        xo_ref[...] = x_ref[...] + mod_ref[2:3, :] * br

    row = pl.BlockSpec((tile, D), lambda i: (i, 0))
    return _call(
        body, name=f"attn_out_proj_{j}", grid=(seq // tile,),
        out_shape=(jax.ShapeDtypeStruct((seq, D), F32), jax.ShapeDtypeStruct((seq, D), BF16)),
        in_specs=[row, row, row, _const_spec((D, D)), _mod_spec(layer)],
        out_specs=(row, row),
        compiler_params=_cparams(1),
    )(x, o, g, w, mod)


def _attn_out_proj_bwd(dxn, br, o, g, w, j, mod, layer, tile, after):
    seq = dxn.shape[0]
    steps = seq // tile

    def body(dxn_ref, br_ref, o_ref, g_ref, w_ref, mod_ref, after_ref, do_ref, dg_ref, dw_ref, dgate_ref, dw_acc):
        i = pl.program_id(0)

        @pl.when(i == 0)
        def _():
            dw_acc[...] = jnp.zeros_like(dw_acc)
            dgate_ref[...] = jnp.zeros_like(dgate_ref)

        dxn_v, ov, gv = dxn_ref[...], o_ref[...].astype(F32), g_ref[...].astype(F32)
        dgate_ref[...] += jnp.sum(dxn_v * br_ref[...].astype(F32), axis=0, keepdims=True)
        dbr = (dxn_v * mod_ref[2:3, :]).astype(BF16)
        du = _dot_nt(dbr, w_ref[...])
        sg = _sigmoid(gv)
        sl = gv * sg
        dw_acc[...] += _dot_tn((ov * sl).astype(BF16), dbr)
        do = du * sl
        dg_ref[...] = (du * ov * (sg * (1.0 + gv * (1.0 - sg)))).astype(BF16)
        low = _low_half(tile)
        for b in range(D // LANES):
            blk = do[:, LANES * b:LANES * (b + 1)]
            do_ref[:, 2 * LANES * b:2 * LANES * b + LANES] = jnp.where(low, blk, 0.0).astype(BF16)
            do_ref[:, 2 * LANES * b + LANES:2 * LANES * (b + 1)] = jnp.where(low, 0.0, blk).astype(BF16)

        @pl.when(i == steps - 1)
        def _():
            dw_ref[...] = dw_acc[...].astype(BF16)

    row = lambda w_: pl.BlockSpec((tile, w_), lambda i: (i, 0))
    return _call(
        body, name=f"attn_out_proj_bwd_{j}", grid=(steps,),
        out_shape=(jax.ShapeDtypeStruct((seq, N_HEADS * LANES), BF16), jax.ShapeDtypeStruct((seq, D), BF16),
                   jax.ShapeDtypeStruct((D, D), BF16), jax.ShapeDtypeStruct((1, D), F32)),
        in_specs=[row(D), row(D), row(D), row(D), _const_spec((D, D)), _mod_spec(layer), ANY_SPEC],
        out_specs=(row(N_HEADS * LANES), row(D), pl.BlockSpec((D, D), lambda i: (0, 0)),
                   pl.BlockSpec((1, D), lambda i: (0, 0))),
        scratch_shapes=[pltpu.VMEM((D, D), F32)],
        compiler_params=_cparams(1),
    )(dxn, br, o, g, w, mod, after)


def _attn_backward(sinks, qs, dos, kd, vd, j, after):
    seq = qs.shape[0]
    nb = seq // QBLK

    def body(sink_ref, q_ref, do_ref, kp_ref, kc_ref, vp_ref, vc_ref, after_ref, dq_ref, dk_ref, dv_ref, dsink_ref,
             carry_k, carry_v, sink_acc):
        n = pl.program_id(0)

        @pl.when(n == 0)
        def _():
            carry_k[...] = jnp.zeros_like(carry_k)
            carry_v[...] = jnp.zeros_like(carry_v)
            sink_acc[...] = jnp.zeros_like(sink_acc)

        @pl.when(n < nb)
        def _():
            ok = _band_mask(n, 2 * QBLK, True)
            low = _low_half(QBLK)
            lane_q = lax.broadcasted_iota(jnp.int32, (1, 2 * QBLK), 1)
            dk_parts, dv_parts = [], []

            def first_products(g):
                kv, half = divmod(g, 2)
                heads = (4 * kv + half, 4 * kv + 2 + half)
                q = _stack_heads(q_ref, heads)
                do = _stack_heads(do_ref, heads)
                kk = _kv_block(kp_ref, kc_ref, kv)
                return heads, q, do, kk, _dot_nt(kk, q), _dot_nt(_kv_block(vp_ref, vc_ref, kv), do)

            nxt = first_products(0)
            dq_h, dk_kv, dv_kv = [], None, None
            for g in range(2 * N_KV):
                heads, q, do, kk, s_raw, dp_raw = nxt
                if g + 1 < 2 * N_KV:
                    nxt = first_products(g + 1)
                st = jnp.where(ok, s_raw, -1e30)
                sink = jnp.where(lane_q < QBLK, sink_ref[j, heads[0]], sink_ref[j, heads[1]])
                m = jnp.maximum(jnp.max(st, axis=0, keepdims=True), sink)
                e = jnp.exp(st - m)
                e_sink = jnp.exp(sink - m)
                inv = 1.0 / (jnp.sum(e, axis=0, keepdims=True) + e_sink)
                p = e * inv
                pdp = p * dp_raw
                delta = jnp.sum(pdp, axis=0, keepdims=True)
                ds = (pdp - p * delta).astype(BF16)
                sink_acc[g:g + 1, :] -= e_sink * inv * delta
                dk_g, dv_g = _dot(ds, q), _dot(p.astype(BF16), do)
                dk_kv = dk_g if dk_kv is None else dk_kv + dk_g
                dv_kv = dv_g if dv_kv is None else dv_kv + dv_g
                dq_h.append(_dot_tn(ds, kk))
                if g % 2 == 1:
                    kv = g // 2
                    for t in range(2):
                        dq_ref[:, LANES * (2 * kv + t):LANES * (2 * kv + t + 1)] = jnp.where(
                            low, dq_h[0][QBLK * t:QBLK * (t + 1)], dq_h[1][QBLK * t:QBLK * (t + 1)])
                    dk_parts.append(dk_kv + pltpu.roll(dk_kv, HEAD_DIM, 1))
                    dv_parts.append(dv_kv + pltpu.roll(dv_kv, HEAD_DIM, 1))
                    dq_h, dk_kv, dv_kv = [], None, None

            def order(parts, lo, hi):
                return jnp.concatenate([jnp.where(low, parts[0][lo:hi], parts[1][lo:hi]),
                                        jnp.where(low, parts[2][lo:hi], parts[3][lo:hi])], axis=1)

            dk_ref[...] = carry_k[...] + order(dk_parts, 0, QBLK)
            dv_ref[...] = (carry_v[...] + order(dv_parts, 0, QBLK)).astype(BF16)
            carry_k[...] = order(dk_parts, QBLK, 2 * QBLK)
            carry_v[...] = order(dv_parts, QBLK, 2 * QBLK)

        @pl.when(n == nb)
        def _():
            dk_ref[...] = carry_k[...]
            dv_ref[...] = carry_v[...].astype(BF16)
            lane = lax.broadcasted_iota(jnp.int32, (1, LANES), 1)
            out = jnp.zeros((1, LANES), F32)
            for g in range(2 * N_KV):
                for t in range(2):
                    tot = jnp.sum(sink_acc[g:g + 1, QBLK * t:QBLK * (t + 1)], axis=1, keepdims=True)
                    out = jnp.where(lane == 4 * (g // 2) + 2 * t + g % 2, tot, out)
            dsink_ref[...] = out

    cur = lambda w: pl.BlockSpec((QBLK, w), lambda n: (jnp.minimum(n, nb - 1), 0))
    prev = lambda w: pl.BlockSpec((QBLK, w), lambda n: (jnp.maximum(n - 1, 0), 0))
    return _call(
        body, name=f"attn_backward_{j}", grid=(nb + 1,),
        out_shape=(jax.ShapeDtypeStruct((seq, D), F32), jax.ShapeDtypeStruct((seq, N_KV * HEAD_DIM), F32),
                   jax.ShapeDtypeStruct((seq, N_KV * HEAD_DIM), BF16), jax.ShapeDtypeStruct((1, LANES), F32)),
        in_specs=[pl.BlockSpec(memory_space=pltpu.SMEM), cur(N_HEADS * LANES), cur(N_HEADS * LANES), prev(KX_W), cur(KX_W),
                  prev(KX_W), cur(KX_W), ANY_SPEC],
        out_specs=(cur(D), prev(N_KV * HEAD_DIM), prev(N_KV * HEAD_DIM), pl.BlockSpec((1, LANES), lambda n: (0, 0))),
        scratch_shapes=[pltpu.VMEM((QBLK, N_KV * HEAD_DIM), F32), pltpu.VMEM((QBLK, N_KV * HEAD_DIM), F32),
                        pltpu.VMEM((2 * N_KV, 2 * QBLK), F32)],
        compiler_params=_cparams(1),
    )(sinks, qs, dos, kd, kd, vd, vd, after)


def _in_proj_tail(x_ref, dxn_ref, ng_ref, mod_ref, w_ref, dproj, dx_ref, dw_acc, vec_acc):
    ng, sc, sh = ng_ref[...], mod_ref[1:2, :], mod_ref[0:1, :]
    xh, r, h = _norm_mod(x_ref[...], ng, sc, sh)
    dh = _dot(dproj, w_ref[...])
    dw_acc[...] += _dot_tn(dproj, h.astype(BF16))
    vec_acc[0:1, :] += jnp.sum(dh, axis=0, keepdims=True)
    vec_acc[1:2, :] += jnp.sum(dh * xh, axis=0, keepdims=True)
    dxh = dh * (ng * (1.0 + sc))
    dx_ref[...] = dxn_ref[...] + r * (dxh - xh * jnp.mean(dxh * xh, axis=-1, keepdims=True))


def _tail_finish(ng_ref, mod_ref, dw_ref, vec_ref, dw_acc, vec_acc):
    dw_ref[...] = dw_acc[...].astype(BF16)
    a = vec_acc[1:2, :]
    vec_ref[...] = jnp.zeros_like(vec_ref)
    vec_ref[0:1, :] = vec_acc[0:1, :]
    vec_ref[1:2, :] = a * ng_ref[...]
    vec_ref[3:4, :] = a * (1.0 + mod_ref[1:2, :])


def _attn_in_proj_bwd(x, dxn, rope, qk_raw, dq, dk, dv, dg, rows, mod, layer, w_t, j, gain, bd, tile):
    seq = x.shape[0]
    steps = seq // tile

    def body(x_ref, dxn_ref, rope_ref, qk_ref, dq_ref, dk_ref, dv_ref, dg_ref, ng_ref, mod_ref, w_ref, gain_ref,
             bd_ref, dx_ref, dw_ref, vec_ref, dgain_ref, dproj, dw_acc, vec_acc):
        i = pl.program_id(0)

        @pl.when(i == 0)
        def _():
            dw_acc[...] = jnp.zeros_like(dw_acc)
            vec_acc[...] = jnp.zeros_like(vec_acc)
            dgain_ref[...] = jnp.zeros_like(dgain_ref)

        tabs = _rope_tabs(rope_ref)
        bdm = bd_ref[...]
        for b in range(QK_W // LANES):
            cols = slice(LANES * b, LANES * (b + 1))
            raw = qk_ref[:, cols]
            if b < D // LANES:
                dy = dq_ref[:, cols] * (HEAD_DIM ** -0.5)
            else:
                dy = dk_ref[:, LANES * (b - D // LANES):LANES * (b + 1 - D // LANES)]
            dy = _rope_bwd(dy, tabs)
            rr = lax.rsqrt(_dot_split(raw * raw, bdm) * (1.0 / HEAD_DIM) + NORM_EPS)
            xh = raw * rr
            dgain_ref[:, cols] += jnp.sum(dy * xh, axis=0, keepdims=True)
            dxh = dy * gain_ref[:, cols]
            dproj[:, cols] = (rr * (dxh - xh * (_dot_split(dxh * xh, bdm) * (1.0 / HEAD_DIM)))).astype(BF16)
        dproj[:, QK_W:QK_W + N_KV * HEAD_DIM] = dv_ref[...]
        dproj[:, QK_W + N_KV * HEAD_DIM:] = dg_ref[...]
        _in_proj_tail(x_ref, dxn_ref, ng_ref, mod_ref, w_ref, dproj[...], dx_ref, dw_acc, vec_acc)

        @pl.when(i == steps - 1)
        def _():
            _tail_finish(ng_ref, mod_ref, dw_ref, vec_ref, dw_acc, vec_acc)

    row = lambda w, dt=None: pl.BlockSpec((tile, w), lambda i: (i, 0))
    fixed = lambda shape: pl.BlockSpec(shape, lambda i: (0,) * len(shape))
    return _call(
        body, name=f"attn_in_proj_bwd_{j}", grid=(steps,),
        out_shape=(jax.ShapeDtypeStruct((seq, D), F32), jax.ShapeDtypeStruct((ATTN_IN, D), BF16),
                   jax.ShapeDtypeStruct((8, D), F32), jax.ShapeDtypeStruct((1, QK_W), F32)),
        in_specs=[row(D), row(D), row(3 * LANES), row(QK_W), row(D), row(N_KV * HEAD_DIM), row(N_KV * HEAD_DIM), row(D),
                  _mod_row_spec(layer, NORM_ROW), _mod_spec(layer), _const_spec((ATTN_IN, D)), _const_spec((1, QK_W)),
                  _const_spec((LANES, LANES))],
        out_specs=(row(D), fixed((ATTN_IN, D)), fixed((8, D)), fixed((1, QK_W))),
        scratch_shapes=[pltpu.VMEM((tile, ATTN_IN), BF16), pltpu.VMEM((ATTN_IN, D), F32), pltpu.VMEM((8, D), F32)],
        compiler_params=_cparams(1),
    )(x, dxn, rope, qk_raw, dq, dk, dv, dg, rows, mod, w_t, gain, bd)


def _pool_in_proj(x, rows, mod, layer, w_t, j, tile):
    seq = x.shape[0]

    def body(x_ref, ng_ref, mod_ref, w_ref, v_ref, g_ref):
        _, _, h = _norm_mod(x_ref[...], ng_ref[...], mod_ref[1:2, :], mod_ref[0:1, :])
        proj = _dot_nt(h.astype(BF16), w_ref[...])
        v_ref[...] = proj[:, :D].astype(BF16)
        g_ref[...] = proj[:, D:].astype(BF16)

    row = pl.BlockSpec((tile, D), lambda i: (i, 0))
    return _call(
        body, name=f"pool_in_proj_{j}", grid=(seq // tile,),
        out_shape=(jax.ShapeDtypeStruct((seq, D), BF16), jax.ShapeDtypeStruct((seq, D), BF16)),
        in_specs=[row, _mod_row_spec(layer, NORM_ROW), _mod_spec(layer), _const_spec((POOL_IN, D))],
        out_specs=(row, row),
        compiler_params=_cparams(1),
    )(x, rows, mod, w_t)


PAD = 8


def _window_sums(ext, lo, hi, forward):
    gw = D // len(POOL_WINDOWS)
    planes = []
    for gi, w in enumerate(POOL_WINDOWS):
        cols = slice(gw * gi, gw * (gi + 1))
        src, k = 0, 1
        while k < w:
            d = k if forward else -k
            ext[1 - src, lo:hi, cols] = ext[src, lo:hi, cols] + ext[src, lo + d:hi + d, cols]
            src, k = 1 - src, 2 * k
        planes.append(src)
    return planes


def _pooled(ext, v_ref, first, tile):
    t_abs = first + lax.broadcasted_iota(jnp.int32, (tile, 1), 0)
    top = PAD + HALO
    planes = _window_sums(ext, PAD, top + tile, False)
    outs = []
    gw = D // len(POOL_WINDOWS)
    for gi, w in enumerate(POOL_WINDOWS):
        cols = slice(gw * gi, gw * (gi + 1))
        cnt = jnp.minimum(t_abs + 1, w).astype(F32)
        outs.append(ext[planes[gi], top:top + tile, cols] / cnt - v_ref[:, cols].astype(F32))
    return jnp.concatenate(outs, axis=1)


def _fill_ext(ext, halo_ref, v_ref, i, tile):
    ext[0, 0:PAD, :] = jnp.zeros((PAD, D), F32)
    ext[1, 0:PAD, :] = jnp.zeros((PAD, D), F32)
    ext[0, PAD:PAD + HALO, :] = jnp.where(i == 0, 0.0, halo_ref[...].astype(F32))
    ext[0, PAD + HALO:PAD + HALO + tile, :] = v_ref[...].astype(F32)


def _group_mix(pb, wg_ref):
    gw = D // len(POOL_WINDOWS)
    return jnp.concatenate([_dot(pb[:, gw * gi:gw * (gi + 1)], wg_ref[gi]) for gi in range(len(POOL_WINDOWS))], axis=1)


def _pool_mix_out(x, v, g, wg, w_out, j, rows, mod, layer, tile, target=None):
    seq = x.shape[0]

    def body(*refs):
        if target is None:
            x_ref, v_ref, halo_ref, g_ref, wg_ref, w_ref, scale_ref, mod_ref, xo_ref, br_ref, ext = refs
        else:
            x_ref, v_ref, halo_ref, g_ref, wg_ref, w_ref, scale_ref, mod_ref, t_ref, xo_ref, br_ref, loss_ref, ext = refs
        i = pl.program_id(0)
        _fill_ext(ext, halo_ref, v_ref, i, tile)
        pb = _pooled(ext, v_ref, i * tile, tile).astype(BF16)
        ms = _group_mix(pb, wg_ref) * scale_ref[...]
        gv = g_ref[...].astype(F32)
        u = (ms * (gv * _sigmoid(gv))).astype(BF16)
        br = _dot(u, w_ref[...])
        br_ref[...] = br.astype(BF16)
        y = x_ref[...] + mod_ref[2:3, :] * br
        if target is None:
            xo_ref[...] = y
        else:
            @pl.when(i == 0)
            def _():
                loss_ref[...] = jnp.zeros_like(loss_ref)

            e = y - t_ref[...]
            xo_ref[...] = e * (1.0 / D)
            loss_ref[...] += 0.5 * jnp.sum(jnp.mean(e * e, axis=-1, keepdims=True), axis=0, keepdims=True)

    row = pl.BlockSpec((tile, D), lambda i: (i, 0))
    halo = pl.BlockSpec((HALO, D), lambda i: (jnp.maximum(i * (tile // HALO) - 1, 0), 0))
    extra_in, extra_out, extra_shape = ([], (), ()) if target is None else (
        [row], (pl.BlockSpec((1, LANES), lambda i: (0, 0)),), (jax.ShapeDtypeStruct((1, LANES), F32),))
    return _call(
        body, name=f"pool_mix_out_{j}", grid=(seq // tile,),
        out_shape=(jax.ShapeDtypeStruct((seq, D), F32), jax.ShapeDtypeStruct((seq, D), BF16)) + extra_shape,
        in_specs=[row, row, halo, row, _const_spec(wg.shape), _const_spec((D, D)), _mod_row_spec(layer, POOL_SCALE_ROW),
                  _mod_spec(layer)] + extra_in,
        out_specs=(row, row) + extra_out,
        scratch_shapes=[pltpu.VMEM((2, tile + HALO + PAD, D), F32)],
        compiler_params=_cparams(1),
    )(x, v, v, g, wg, w_out, rows, mod, *(() if target is None else (target,)))


def _pool_mix_out_bwd(dxn, br, v, g, wg, w_out, j, rows, mod, layer, tile, after):
    seq = dxn.shape[0]
    steps = seq // tile
    ng_ = len(POOL_WINDOWS)
    gw = D // ng_

    def body(dxn_ref, br_ref, v_ref, halo_ref, g_ref, wg_ref, w_ref, scale_ref, mod_ref, after_ref,
             dpool_ref, dg_ref, dw_ref, dwg_ref, vec_ref, ext, dw_acc, dwg_acc):
        i = pl.program_id(0)

        @pl.when(i == 0)
        def _():
            dw_acc[...] = jnp.zeros_like(dw_acc)
            dwg_acc[...] = jnp.zeros_like(dwg_acc)
            vec_ref[...] = jnp.zeros_like(vec_ref)

        _fill_ext(ext, halo_ref, v_ref, i, tile)
        pb = _pooled(ext, v_ref, i * tile, tile).astype(BF16)
        mixed = _group_mix(pb, wg_ref)
        scale = scale_ref[...]
        ms = mixed * scale
        gv, dxn_v = g_ref[...].astype(F32), dxn_ref[...]
        sg = _sigmoid(gv)
        sl = gv * sg
        vec_ref[0:1, :] += jnp.sum(dxn_v * br_ref[...].astype(F32), axis=0, keepdims=True)
        dbr = (dxn_v * mod_ref[2:3, :]).astype(BF16)
        du = _dot_nt(dbr, w_ref[...])
        dw_acc[...] += _dot_tn((ms * sl).astype(BF16), dbr)
        dms = du * sl
        dg_ref[...] = (du * ms * (sg * (1.0 + gv * (1.0 - sg)))).astype(BF16)
        vec_ref[1:2, :] += jnp.sum(dms * mixed, axis=0, keepdims=True)
        dmx = (dms * scale).astype(BF16)
        for gi in range(ng_):
            cols = slice(gw * gi, gw * (gi + 1))
            dpool_ref[:, cols] = _dot_nt(dmx[:, cols], wg_ref[gi])
            dwg_acc[gi] += _dot_tn(pb[:, cols], dmx[:, cols])

        @pl.when(i == steps - 1)
        def _():
            dw_ref[...] = dw_acc[...].astype(BF16)
            dwg_ref[...] = dwg_acc[...].astype(BF16)

    row = pl.BlockSpec((tile, D), lambda i: (i, 0))
    halo = pl.BlockSpec((HALO, D), lambda i: (jnp.maximum(i * (tile // HALO) - 1, 0), 0))
    fixed = lambda shape: pl.BlockSpec(shape, lambda i: (0,) * len(shape))
    return _call(
        body, name=f"pool_mix_out_bwd_{j}", grid=(steps,),
        out_shape=(jax.ShapeDtypeStruct((seq, D), F32), jax.ShapeDtypeStruct((seq, D), BF16),
                   jax.ShapeDtypeStruct((D, D), BF16), jax.ShapeDtypeStruct((ng_, gw, gw), BF16),
                   jax.ShapeDtypeStruct((8, D), F32)),
        in_specs=[row, row, row, halo, row, _const_spec(wg.shape), _const_spec((D, D)), _mod_row_spec(layer, POOL_SCALE_ROW),
                  _mod_spec(layer), ANY_SPEC],
        out_specs=(row, row, fixed((D, D)), fixed((ng_, gw, gw)), fixed((8, D))),
        scratch_shapes=[pltpu.VMEM((2, tile + HALO + PAD, D), F32), pltpu.VMEM((D, D), F32), pltpu.VMEM((ng_, gw, gw), F32)],
        compiler_params=_cparams(1),
    )(dxn, br, v, v, g, wg, w_out, rows, mod, after)


def _pool_in_proj_bwd(x, dxn, dpool, dg, rows, mod, layer, w_t, j, tile, after):
    seq = x.shape[0]
    steps = seq // tile
    gw = D // len(POOL_WINDOWS)

    def body(x_ref, dxn_ref, dp_ref, halo_ref, dg_ref, ng_ref, mod_ref, w_ref, after_ref, dx_ref, dw_ref, vec_ref,
             ext, dproj, dw_acc, vec_acc):
        i = pl.program_id(0)

        @pl.when(i == 0)
        def _():
            dw_acc[...] = jnp.zeros_like(dw_acc)
            vec_acc[...] = jnp.zeros_like(vec_acc)

        t_abs = i * tile + lax.broadcasted_iota(jnp.int32, (tile, 1), 0)
        last = i == steps - 1
        ext[0, tile + HALO:tile + HALO + PAD, :] = jnp.zeros((PAD, D), F32)
        ext[1, tile + HALO:tile + HALO + PAD, :] = jnp.zeros((PAD, D), F32)
        for gi, w in enumerate(POOL_WINDOWS):
            cols = slice(gw * gi, gw * (gi + 1))
            cnt = jnp.minimum(t_abs + 1, w).astype(F32)
            ext[0, 0:tile, cols] = dp_ref[:, cols] / cnt
            ext[0, tile:tile + HALO, cols] = jnp.where(last, 0.0, halo_ref[:, cols] * (1.0 / w))
        planes = _window_sums(ext, 0, tile + HALO, True)
        for gi, w in enumerate(POOL_WINDOWS):
            cols = slice(gw * gi, gw * (gi + 1))
            dproj[:, cols] = (ext[planes[gi], 0:tile, cols] - dp_ref[:, cols]).astype(BF16)
        dproj[:, D:] = dg_ref[...]
        _in_proj_tail(x_ref, dxn_ref, ng_ref, mod_ref, w_ref, dproj[...], dx_ref, dw_acc, vec_acc)

        @pl.when(last)
        def _():
            _tail_finish(ng_ref, mod_ref, dw_ref, vec_ref, dw_acc, vec_acc)

    row = pl.BlockSpec((tile, D), lambda i: (i, 0))
    halo = pl.BlockSpec((HALO, D), lambda i: (jnp.minimum((i + 1) * (tile // HALO), seq // HALO - 1), 0))
    fixed = lambda shape: pl.BlockSpec(shape, lambda i: (0,) * len(shape))
    return _call(
        body, name=f"pool_in_proj_bwd_{j}", grid=(steps,),
        out_shape=(jax.ShapeDtypeStruct((seq, D), F32), jax.ShapeDtypeStruct((POOL_IN, D), BF16),
                   jax.ShapeDtypeStruct((8, D), F32)),
        in_specs=[row, row, row, halo, row, _mod_row_spec(layer, NORM_ROW), _mod_spec(layer), _const_spec((POOL_IN, D)), ANY_SPEC],
        out_specs=(row, fixed((POOL_IN, D)), fixed((8, D))),
        scratch_shapes=[pltpu.VMEM((2, tile + HALO + PAD, D), F32), pltpu.VMEM((tile, POOL_IN), BF16), pltpu.VMEM((POOL_IN, D), F32),
                        pltpu.VMEM((8, D), F32)],
        compiler_params=_cparams(1),
    )(x, dxn, dpool, dpool, dg, rows, mod, w_t, after)


def _build_vec(vecs, gates, pool_vecs, gains, dsinks, loss_part):
    def body(v0, v1, v2, v3, g0, g2, p0, p1, n0, n1, s0, s1, loss_ref, out):
        out[...] = jnp.zeros_like(out)
        for i, v in enumerate((v0, v1, v2, v3)):
            out[3 * i:3 * i + 2, :] = v[0:2, :]
            out[12 + i:13 + i, :] = v[3:4, :]
        out[2:3, :] = g0[...]
        out[8:9, :] = g2[...]
        for j, (p, n, s) in enumerate(((p0, n0, s0), (p1, n1, s1))):
            out[3 * (2 * j + 1) + 2:3 * (2 * j + 1) + 3, :] = p[0:1, :]
            out[22 + j:23 + j, :] = p[1:2, :]
            out[16 + j:17 + j, :] = n[:, 0:D]
            out[18 + j:19 + j, 0:QK_W - D] = n[:, D:QK_W]
            out[20 + j:21 + j, 0:LANES] = s[...]
        out[24:25, 0:LANES] = loss_ref[...]

    vm = pl.BlockSpec(memory_space=pltpu.VMEM)
    args = (*vecs, gates[0], gates[2], *pool_vecs, *gains, *dsinks, loss_part)
    return _call(
        body, name="build_vec",
        out_shape=jax.ShapeDtypeStruct((VEC_ROWS, D), F32),
        in_specs=[vm] * len(args), out_specs=vm,
        compiler_params=_cparams(),
    )(*args)


def _sum_devices(g, after):
    rows = g.shape[1]

    def body(g_ref, after_ref, tot_ref, fold_ref):
        tot = g_ref[0]
        for p in range(1, N_DEV):
            tot = tot + g_ref[p]
        tot_ref[...] = tot
        f = tot[16:24, 0:LANES]
        for b in range(1, D // LANES):
            f = f + tot[16:24, LANES * b:LANES * (b + 1)]
        fold_ref[...] = f + pltpu.roll(f, HEAD_DIM, 1)

    return _call(
        body, name="sum_devices",
        out_shape=(jax.ShapeDtypeStruct((rows, D), F32), jax.ShapeDtypeStruct((8, LANES), F32)),
        in_specs=[pl.BlockSpec(memory_space=pltpu.VMEM), ANY_SPEC],
        out_specs=(pl.BlockSpec(memory_space=pltpu.VMEM), pl.BlockSpec(memory_space=pltpu.VMEM)),
        compiler_params=_cparams(),
    )(g, after)


def _adamw_small(params):
    n = len(params)

    def body(*refs):
        ins, outs = refs[:4 * n], refs[4 * n:]
        for p in range(n):
            w_ref, g_ref, m_ref, v_ref = ins[4 * p:4 * p + 4]
            outs[3 * p][...], outs[3 * p + 1][...], outs[3 * p + 2][...] = _adamw(w_ref[...], g_ref[...], m_ref[...], v_ref[...])

    vm = pl.BlockSpec(memory_space=pltpu.VMEM)
    out = _call(
        body, name="adamw_small",
        out_shape=tuple(jax.ShapeDtypeStruct(w.shape, F32) for (w, _, _, _) in params for _ in range(3)),
        in_specs=[vm] * (4 * n), out_specs=tuple([vm] * (3 * n)),
        compiler_params=_cparams(),
    )(*[a for p in params for a in p])
    return [tuple(out[3 * p:3 * p + 3]) for p in range(n)]


def _adamw_shards(name, me, fulls, lands, w, m, v, transpose, axis=0):
    nl = w.shape[0]
    wshape = w.shape[1:]
    own_shape = lands[0].shape[1:]

    def body(me_ref, *refs):
        own_refs, land_refs = refs[:nl], refs[nl:2 * nl]
        w_ref, m_ref, v_ref, g_out, d_out, m_out, v_out = refs[2 * nl:]
        layer = pl.program_id(0)
        for l in range(nl):
            @pl.when(layer == l)
            def _(l=l):
                g = own_refs[l][...].astype(F32)
                for k in range(N_DEV - 1):
                    g = g + land_refs[l][k].astype(F32)
                if transpose:
                    g = g.T
                g_out[...] = g
                d_out[...], m_out[...], v_out[...] = _adamw(w_ref[...], g, m_ref[...], v_ref[...])

    def own_index(l_, me_ref):
        idx = [0] * len(own_shape)
        idx[axis] = me_ref[0]
        return tuple(idx)

    own_spec = pl.BlockSpec(tuple(own_shape), own_index)
    land_spec = pl.BlockSpec((N_DEV - 1,) + tuple(own_shape), lambda l_, me_ref: (0,) * (1 + len(own_shape)))
    wspec = pl.BlockSpec((None,) + tuple(wshape), lambda l_, me_ref: (l_,) + (0,) * len(wshape))
    return _call(
        body, name=name,
        grid_spec=pltpu.PrefetchScalarGridSpec(num_scalar_prefetch=1, grid=(nl,),
                                               in_specs=[own_spec] * nl + [land_spec] * nl + [wspec] * 3,
                                               out_specs=(wspec,) * 4),
        out_shape=tuple(jax.ShapeDtypeStruct(w.shape, F32) for _ in range(4)),
        compiler_params=_cparams(1),
    )(me.reshape(1), *fulls, *lands, w, m, v)


def _constants():
    lane = np.arange(LANES)
    bd = (lane[:, None] // HEAD_DIM == lane[None, :] // HEAD_DIM).astype(np.float32)
    half = ROT_DIM // 2
    inv_freq = ROPE_THETA ** (-jnp.arange(half, dtype=F32) * 2.0 / ROT_DIM)
    invf = jnp.tile(inv_freq, LANES // half).reshape(1, LANES)
    return jnp.asarray(bd, BF16), invf


def kernel(x, c, positions, ada_w, ada_b, norm_g, attn_w_in, attn_q_norm, attn_k_norm, attn_sinks, attn_w_out, pool_w_in, pool_w_group, pool_scale, pool_w_out, loss_target, m_ada_w, m_ada_b, m_norm_g, m_attn_w_in, m_attn_q_norm, m_attn_k_norm, m_attn_sinks, m_attn_w_out, m_pool_w_in, m_pool_w_group, m_pool_scale, m_pool_w_out, v_ada_w, v_ada_b, v_norm_g, v_attn_w_in, v_attn_q_norm, v_attn_k_norm, v_attn_sinks, v_attn_w_out, v_pool_w_in, v_pool_w_group, v_pool_scale, v_pool_w_out):
    seq = x.shape[1]
    me = 4 * lax.axis_index("x") + 2 * lax.axis_index("y") + lax.axis_index("c")
    bd, invf = _constants()
    t_mm = min(512, seq)
    rope = _rope_table(positions.reshape(seq, 1), invf, t_mm)
    t_bw = min(256, seq)
    shard = pool_scale.shape[1]
    cols = ada_w.shape[2]

    w_first, = _prep_weights(me, [(attn_w_in, 0, "T")], "prep_first")
    first_w, token = _gather_first_start(w_first, c)
    prepped = _prep_weights(me, [(attn_w_out, 0, "N"), (pool_w_in, 0, "T"), (pool_w_out, 0, "N"), (pool_w_group, 0, "G"),
                                 (attn_w_in, 1, "T"), (attn_w_out, 1, "N"), (pool_w_in, 1, "T"), (pool_w_out, 1, "N"),
                                 (pool_w_group, 1, "G")], "prep_rest")

    first = jnp.concatenate([c, jnp.pad(pool_scale, ((0, 0), (0, D - shard))), jnp.zeros((5, D), F32)], axis=0)
    first = _allgather_small(first + token[0, 0], "allgather_c", rope)
    c_all = first[:, 0, :]
    scale_full = jnp.transpose(first[:, 1:3, :shard], (1, 0, 2)).reshape(2, D)
    mod_part = _ada_forward(c_all, ada_w)
    mod_all = _allgather_small(mod_part.reshape(DEPTH * N_DEV, cols), "allgather_mod", prepped[0])
    mod_all = mod_all.reshape(N_DEV, DEPTH, N_DEV, cols)
    mine = lax.dynamic_index_in_dim(mod_all, me, axis=2, keepdims=False)
    mod = jnp.transpose(mine, (1, 0, 2)).reshape(DEPTH, 3 * D) + ada_b
    pool_rows = jnp.stack([jnp.zeros_like(scale_full[0]), scale_full[0], jnp.zeros_like(scale_full[0]), scale_full[1]])
    mod = jnp.concatenate([mod.reshape(DEPTH, 3, D), norm_g[:, None, :], pool_rows[:, None, :],
                           jnp.zeros((DEPTH, 3, D), F32)], axis=1)
    rows = mod.reshape(DEPTH, 8, 1, D)

    groups = [prepped[0:1], prepped[1:4], prepped[4:6], prepped[6:9]]
    gaxes = [(0,), (0,), (0, 0, 1), (0, 0), (0, 0, 1)]
    first_w, token = _gather_first_forward(first_w, mod)
    rest, token = _gather_start(groups, gaxes[1:], token, "gather_start_rest")
    started = [None] + rest

    saved, weights = [], []
    h = x[0]
    for i in range(DEPTH):
        j = i // 2
        s = dict(x=h)
        if i == 0:
            w_in_t = _gather_first_wait(first_w, token)
        else:
            wts = _gather_wait(started[i + 1], gaxes[i + 1], h, f"gather_wait_{i}")
        if i % 2 == 0:
            if i > 0:
                w_in_t, w_out = wts
            s["gain"] = jnp.concatenate([jnp.tile(attn_q_norm[j], N_HEADS), jnp.tile(attn_k_norm[j], N_KV)]).reshape(1, QK_W)
            s["qk_raw"], s["qs"], s["kd"], s["vd"], s["g"] = _attn_in_proj(
                h, rope, rows, mod, i, w_in_t, j, s["gain"], bd, t_bw)
            s["o"] = _attn_forward(attn_sinks, s["qs"], s["kd"], s["vd"], j)
            if i == 0:
                w_out, = _gather_wait(started[1], gaxes[1], s["o"], "gather_wait_0_out")
            h, s["br"] = _attn_out_proj(h, s["o"], s["g"], w_out, j, mod, i, t_mm)
            weights.append((w_in_t, w_out))
        else:
            p_in_t, p_out, p_grp = wts
            s["v"], s["g"] = _pool_in_proj(h, rows, mod, i, p_in_t, j, t_mm)
            if i < DEPTH - 1:
                h, s["br"] = _pool_mix_out(h, s["v"], s["g"], p_grp, p_out, j, rows, mod, i, t_mm)
            else:
                dx, s["br"], loss_part = _pool_mix_out(h, s["v"], s["g"], p_grp, p_out, j, rows, mod, i, t_mm,
                                                       loss_target[0])
            weights.append(wts)
        saved.append(s)

    vecs, gates, gains, dsinks, pool_vecs = [None] * DEPTH, [None] * DEPTH, [None] * 2, [None] * 2, [None] * 2
    sent_in, sent_out = [None] * DEPTH, [None] * DEPTH
    token = jnp.zeros((8, LANES), F32)
    for i in reversed(range(DEPTH)):
        j = i // 2
        s = saved[i]
        if i % 2 == 0:
            w_in_t, w_out = weights[i]
            dos, dg, d_w_out, gates[i] = _attn_out_proj_bwd(dx, s["br"], s["o"], s["g"], w_out, j, mod, i, t_mm, token)
            sent_out[i], token = _scatter_start([d_w_out], (0,), f"scatter_start_{i}_out", token)
            dq, dk, dv, dsinks[j] = _attn_backward(attn_sinks, s["qs"], dos, s["kd"], s["vd"], j, token)
            dx, d_in_t, vecs[i], gains[j] = _attn_in_proj_bwd(
                s["x"], dx, rope, s["qk_raw"], dq, dk, dv, dg, rows, mod, i, w_in_t, j, s["gain"], bd, t_bw)
        else:
            p_in_t, p_out, p_grp = weights[i]
            dpool, dg, d_p_out, d_p_grp, pool_vecs[j] = _pool_mix_out_bwd(
                dx, s["br"], s["v"], s["g"], p_grp, p_out, j, rows, mod, i, t_mm, token)
            sent_out[i], token = _scatter_start([d_p_out, d_p_grp], (0, 1), f"scatter_start_{i}_out", token)
            dx, d_in_t, vecs[i] = _pool_in_proj_bwd(s["x"], dx, dpool, dg, rows, mod, i, p_in_t, j, t_bw, token)
        if i > 0:
            sent_in[i], token = _scatter_start([d_in_t], (0,), f"scatter_start_{i}_in", token)

    vec = _build_vec(vecs, gates, pool_vecs, gains, dsinks, loss_part)
    vec_rows = lax.dynamic_update_slice(jnp.zeros((N_DEV * VEC_ROWS, D), F32), vec, (me * VEC_ROWS, 0))
    vec_sent, token = _gather_start([[vec_rows]], [(0,)], loss_part, "vec_gather_start")
    sent_in[0], token = _scatter_start([d_in_t], (0,), "scatter_start_0_in", token)

    got_in, got_out = [None] * DEPTH, [None] * DEPTH
    for i in (3, 1):
        got_out[i] = _scatter_wait(sent_out[i], (0, 1), token, f"scatter_wait_{i}_out")
        got_in[i] = _scatter_wait(sent_in[i], (0,), token, f"scatter_wait_{i}_in")
    pick = lambda got, ls, a: ([got[i][0][a] for i in ls], [got[i][1][a] for i in ls])
    res = {}
    res["pool_w_in"] = _adamw_shards("adamw_pool_w_in", me, *pick(got_in, (1, 3), 0), pool_w_in, m_pool_w_in, v_pool_w_in, True)
    res["pool_w_out"] = _adamw_shards("adamw_pool_w_out", me, *pick(got_out, (1, 3), 0), pool_w_out, m_pool_w_out,
                                      v_pool_w_out, False)
    res["pool_w_group"] = _adamw_shards("adamw_pool_w_group", me, *pick(got_out, (1, 3), 1), pool_w_group, m_pool_w_group,
                                        v_pool_w_group, False, axis=1)

    vec_all, = _gather_wait(vec_sent[0], (0,), res["pool_w_group"][0], "vec_gather_wait")
    vec_all = vec_all.reshape(N_DEV, VEC_ROWS, D)
    tot, folded = _sum_devices(vec_all, token)
    loss = tot[24, 0]
    small = dict(
        ada_b=(ada_b, tot[0:12].reshape(DEPTH, 3 * D), m_ada_b, v_ada_b),
        norm_g=(norm_g, tot[12:16], m_norm_g, v_norm_g),
        q_norm=(attn_q_norm, folded[0:2, :HEAD_DIM], m_attn_q_norm, v_attn_q_norm),
        k_norm=(attn_k_norm, folded[2:4, :HEAD_DIM], m_attn_k_norm, v_attn_k_norm),
        sinks=(attn_sinks, tot[20:22, :N_HEADS], m_attn_sinks, v_attn_sinks),
        pool_scale=(pool_scale, lax.dynamic_slice(tot, (22, me * shard), (2, shard)), m_pool_scale, v_pool_scale),
    )
    res.update({k: (a[1],) + upd for (k, a), upd in zip(small.items(), _adamw_small(list(small.values())))})

    dmod_all = vec_all[:, 0:12, :].reshape(N_DEV, DEPTH, 3 * D)
    dmod_mine = lax.dynamic_slice_in_dim(dmod_all, me * cols, cols, axis=2)
    dmod_mine = jnp.pad(jnp.transpose(dmod_mine, (1, 0, 2)), ((0, 0), (0, N_DEV), (0, 0))) + token[0, 0]
    res["ada_w"] = _ada_backward_adamw(jnp.pad(c_all, ((0, N_DEV), (0, 0))), dmod_mine, ada_w, m_ada_w, v_ada_w)

    for i in (2, 0):
        got_out[i] = _scatter_wait(sent_out[i], (0,), res["ada_w"][0], f"scatter_wait_{i}_out")
        got_in[i] = _scatter_wait(sent_in[i], (0,), res["ada_w"][0], f"scatter_wait_{i}_in")
    res["attn_w_out"] = _adamw_shards("adamw_attn_w_out", me, *pick(got_out, (0, 2), 0), attn_w_out, m_attn_w_out,
                                      v_attn_w_out, False)
    res["attn_w_in"] = _adamw_shards("adamw_attn_w_in", me, *pick(got_in, (0, 2), 0), attn_w_in, m_attn_w_in, v_attn_w_in, True)

    order = ("ada_w", "ada_b", "norm_g", "attn_w_in", "q_norm", "k_norm", "sinks", "attn_w_out", "pool_w_in",
             "pool_w_group", "pool_scale", "pool_w_out")
    return (loss, dx[None], *[res[k][0] for k in order], *[res[k][1] for k in order], *[res[k][2] for k in order],
            *[res[k][3] for k in order])
```

```python
import functools

import numpy as np
import jax
import jax.numpy as jnp
from jax import lax
from jax.experimental import pallas as pl
from jax.experimental.pallas import tpu as pltpu

F32 = jnp.float32
BF16 = jnp.bfloat16
MESH = pl.DeviceIdType.MESH

N_DEV = 8
D = 1024
DEPTH = 4
HEAD_DIM = 64
N_HEADS = 16
N_KV = 4
QK_W = 1280
ATTN_IN = 2560
POOL_IN = 2048
QBLK = 128
KX_W = N_KV * 128
CHUNK = 256
POOL_WINDOWS = (2, 4, 8, 16)
HALO = 16
ROPE_THETA = 500000.0
ROT_DIM = 16
NORM_EPS = 1e-6
ADAM_LR = 0.001
ADAM_B1 = 0.9
ADAM_B2 = 0.999
ADAM_EPS = 1e-08
ADAM_WD = 0.01
ADAM_STEP = 10

LANES = 128
VMEM_LIMIT = 56 * 2**20
VEC_ROWS = 32


def _cparams(n_grid=0, **kw):
    if n_grid:
        kw["dimension_semantics"] = ("arbitrary",) * n_grid
    return pltpu.CompilerParams(vmem_limit_bytes=VMEM_LIMIT, **kw)


def _call(body, **kw):
    return pl.pallas_call(body, **kw)


def _mod_spec(layer):
    return pl.BlockSpec((None, 8, D), lambda *_: (layer, 0, 0), pipeline_mode=pl.Buffered(1))


def _mod_row_spec(layer, row):
    return pl.BlockSpec((None, None, 1, D), lambda *_: (layer, row, 0, 0), pipeline_mode=pl.Buffered(1))


NORM_ROW, POOL_SCALE_ROW = 3, 4


def _const_spec(shape):
    nd = len(shape)
    return pl.BlockSpec(shape, lambda *_: (0,) * nd, pipeline_mode=pl.Buffered(1))


def _dot(a, b):
    return jnp.dot(a, b, preferred_element_type=F32)


def _dot_nt(a, b):
    return lax.dot_general(a, b, (((1,), (1,)), ((), ())), preferred_element_type=F32)


def _dot_tn(a, b):
    return lax.dot_general(a, b, (((0,), (0,)), ((), ())), preferred_element_type=F32)


def _group_mean(x, m):
    return _dot(x.astype(BF16), m) * (1.0 / HEAD_DIM)


def _sigmoid(g):
    return 1.0 / (1.0 + jnp.exp(-g))


def _norm_mod(x, ng, sc, sh):
    r = lax.rsqrt(jnp.mean(x * x, axis=-1, keepdims=True) + NORM_EPS)
    xh = x * r
    h = (xh * ng) * (1.0 + sc) + sh
    return xh, r, h


def _rope_table(pos_col, invf_row, tile):
    seq = pos_col.shape[0]

    def body(pos_ref, invf_ref, out_ref):
        ang = pos_ref[...].astype(F32) * invf_ref[...]
        l64 = lax.broadcasted_iota(jnp.int32, (tile, LANES), 1) & (HEAD_DIM - 1)
        cs, sn = jnp.cos(ang), jnp.sin(ang)
        out_ref[:, 0:LANES] = jnp.where(l64 < ROT_DIM, cs, 1.0)
        out_ref[:, LANES:2 * LANES] = jnp.where(l64 < ROT_DIM // 2, -sn, 0.0)
        out_ref[:, 2 * LANES:3 * LANES] = jnp.where((l64 >= ROT_DIM // 2) & (l64 < ROT_DIM), sn, 0.0)

    return _call(
        body, name="rope_table", grid=(seq // tile,),
        out_shape=jax.ShapeDtypeStruct((seq, 3 * LANES), F32),
        in_specs=[pl.BlockSpec((tile, 1), lambda i: (i, 0)), _const_spec((1, LANES))],
        out_specs=pl.BlockSpec((tile, 3 * LANES), lambda i: (i, 0)),
        compiler_params=_cparams(1),
    )(pos_col, invf_row)


def _rope_tabs(rope_ref):
    return rope_ref[:, 0:LANES], rope_ref[:, LANES:2 * LANES], rope_ref[:, 2 * LANES:3 * LANES]


def _rope(y, tabs):
    cos_t, sin_a, sin_b = tabs
    return y * cos_t + pltpu.roll(y, LANES - ROT_DIM // 2, 1) * sin_a + pltpu.roll(y, ROT_DIM // 2, 1) * sin_b


def _rope_bwd(dy, tabs):
    cos_t, sin_a, sin_b = tabs
    return dy * cos_t + pltpu.roll(dy * sin_a, ROT_DIM // 2, 1) + pltpu.roll(dy * sin_b, LANES - ROT_DIM // 2, 1)


def _low_half(rows):
    return lax.broadcasted_iota(jnp.int32, (rows, LANES), 1) < HEAD_DIM


def _adamw(w, g, m, v):
    m = ADAM_B1 * m + (1.0 - ADAM_B1) * g
    v = ADAM_B2 * v + (1.0 - ADAM_B2) * (g * g)
    m_hat = m / (1.0 - ADAM_B1 ** ADAM_STEP)
    v_hat = v / (1.0 - ADAM_B2 ** ADAM_STEP)
    delta = -ADAM_LR * (m_hat / (jnp.sqrt(v_hat) + ADAM_EPS) + ADAM_WD * w)
    return delta, m, v


def _my_position():
    x, y, c = lax.axis_index("x"), lax.axis_index("y"), lax.axis_index("c")
    return x, y, c, 4 * x + 2 * y + c


def _peers(x, y, c):
    out = []
    for k in range(1, N_DEV):
        px = 1 - x if k & 4 else x
        py = 1 - y if k & 2 else y
        pc = 1 - c if k & 1 else c
        out.append(((px, py, pc), 4 * px + 2 * py + pc))
    return out


def _allgather_small(v, name, after):
    rows, cols = v.shape

    def body(v_ref, after_ref, out_ref, send_sems, recv_sems, local_sem):
        x, y, c, me = _my_position()
        local = pltpu.make_async_copy(v_ref, out_ref.at[me], local_sem)
        local.start()
        sends = []
        for k, (peer, _) in enumerate(_peers(x, y, c)):
            cp = pltpu.make_async_remote_copy(v_ref, out_ref.at[me], send_sems.at[k], recv_sems.at[k],
                                              device_id=peer, device_id_type=MESH)
            cp.start()
            sends.append(cp)
        for k, (peer, idx) in enumerate(_peers(x, y, c)):
            pltpu.make_async_remote_copy(v_ref, out_ref.at[idx], send_sems.at[k], recv_sems.at[k],
                                         device_id=peer, device_id_type=MESH).wait_recv()
        for cp in sends:
            cp.wait_send()
        local.wait()

    return _call(
        body, name=name,
        out_shape=jax.ShapeDtypeStruct((N_DEV, rows, cols), F32),
        in_specs=[pl.BlockSpec(memory_space=pltpu.VMEM), pl.BlockSpec(memory_space=pl.ANY)],
        out_specs=pl.BlockSpec(memory_space=pltpu.VMEM),
        scratch_shapes=[pltpu.SemaphoreType.DMA((N_DEV - 1,)), pltpu.SemaphoreType.DMA((N_DEV - 1,)),
                        pltpu.SemaphoreType.DMA(())],
        compiler_params=_cparams(),
    )(v, after)


def _shard_rows(ref, idx, rows, axis):
    sl = [slice(None)] * len(ref.shape)
    sl[axis] = pl.ds(idx * rows, rows)
    return ref.at[tuple(sl)]


def _own_and_peer_rows(ref, me, idx, axis):
    rows = ref.shape[axis] // N_DEV
    return _shard_rows(ref, me, rows, axis), _shard_rows(ref, idx, rows, axis)


HBM_SPEC = pl.BlockSpec(memory_space=pltpu.HBM)
SEM_SPEC = pl.BlockSpec(memory_space=pltpu.SEMAPHORE)
ANY_SPEC = pl.BlockSpec(memory_space=pl.ANY)
DATAFLOW = pltpu.SideEffectType.DATAFLOW_SIDE_EFFECTING


def _hbm(a):
    return pltpu.with_memory_space_constraint(a, pltpu.HBM)


def _gather_start(layers, axes, after, name):
    flat = [a for arrs in layers for a in arrs]
    flat_axes = [ax for axs in axes for ax in axs]
    n, nl = len(flat), len(layers)

    def body(*refs):
        ins, sems, token = refs[:n], refs[n + 1:n + 1 + 2 * nl], refs[-1]
        x, y, c, me = _my_position()
        a0 = 0
        for li, arrs in enumerate(layers):
            for k, (peer, _) in enumerate(_peers(x, y, c)):
                for a in range(len(arrs)):
                    rows, _ = _own_and_peer_rows(ins[a0 + a], me, me, flat_axes[a0 + a])
                    pltpu.make_async_remote_copy(rows, rows, sems[2 * li].at[k * len(arrs) + a],
                                                 sems[2 * li + 1].at[k * len(arrs) + a],
                                                 device_id=peer, device_id_type=MESH).start()
            a0 += len(arrs)
        token[...] = jnp.zeros_like(token)

    sem_shapes = []
    for arrs in layers:
        sem_shapes += [pltpu.SemaphoreType.DMA(((N_DEV - 1) * len(arrs),))] * 2
    out = _call(
        body, name=name,
        out_shape=(*sem_shapes, *[pltpu.HBM(a.shape, a.dtype) for a in flat], jax.ShapeDtypeStruct((8, LANES), F32)),
        in_specs=[HBM_SPEC] * n + [ANY_SPEC],
        out_specs=(*[SEM_SPEC] * (2 * nl), *[HBM_SPEC] * n, pl.BlockSpec(memory_space=pltpu.VMEM)),
        input_output_aliases={a: 2 * nl + a for a in range(n)},
        compiler_params=_cparams(has_side_effects=DATAFLOW),
    )(*[_hbm(a) for a in flat], after)
    per_layer, a0 = [], 0
    for li, arrs in enumerate(layers):
        per_layer.append((out[2 * li], out[2 * li + 1], list(out[2 * nl + a0:2 * nl + a0 + len(arrs)])))
        a0 += len(arrs)
    return per_layer, out[-1]


def _gather_wait(started, axes, after, name):
    send_sems, recv_sems, arrs = started
    n = len(arrs)

    def body(*refs):
        ins, send_ref, recv_ref = refs[:n], refs[n], refs[n + 1]
        x, y, c, me = _my_position()
        for k, (peer, idx) in enumerate(_peers(x, y, c)):
            for a in range(n):
                own, theirs = _own_and_peer_rows(ins[a], me, idx, axes[a])
                cp = pltpu.make_async_remote_copy(own, theirs, send_ref.at[k * n + a], recv_ref.at[k * n + a],
                                                  device_id=peer, device_id_type=MESH)
                cp.wait_send()
                cp.wait_recv()

    return _call(
        body, name=name,
        out_shape=tuple(pltpu.HBM(a.shape, a.dtype) for a in arrs),
        in_specs=[HBM_SPEC] * n + [SEM_SPEC, SEM_SPEC, ANY_SPEC],
        out_specs=tuple([HBM_SPEC] * n),
        input_output_aliases={a: a for a in range(n)},
        compiler_params=_cparams(has_side_effects=DATAFLOW),
    )(*arrs, send_sems, recv_sems, after)


def _first_relations(x, y, c):
    return [(x, y, 1 - c), (1 - x, y, c), (x, 1 - y, c), (1 - x, 1 - y, c)]


def _gather_first_start(arr, after):
    n_rel = 4

    def body(a_ref, after_ref, send_ref, recv_ref, thru, token):
        x, y, c, me = _my_position()
        rows, _ = _own_and_peer_rows(a_ref, me, me, 0)
        for k, peer in enumerate(_first_relations(x, y, c)):
            pltpu.make_async_remote_copy(rows, rows, send_ref.at[k], recv_ref.at[k], device_id=peer, device_id_type=MESH).start()
        token[...] = jnp.zeros_like(token)

    sem = pltpu.SemaphoreType.DMA((n_rel,))
    out = _call(
        body, name="gather_first_start",
        out_shape=(sem, sem, pltpu.HBM(arr.shape, arr.dtype), jax.ShapeDtypeStruct((8, LANES), F32)),
        in_specs=[HBM_SPEC, ANY_SPEC],
        out_specs=(SEM_SPEC, SEM_SPEC, HBM_SPEC, pl.BlockSpec(memory_space=pltpu.VMEM)),
        input_output_aliases={0: 2},
        compiler_params=_cparams(has_side_effects=DATAFLOW),
    )(_hbm(arr), after)
    return out[:3], out[3]


def _gather_first_forward(started, after):
    send_a, recv_a, arr = started

    def body(a_ref, send_a_ref, recv_a_ref, after_ref, send_b_ref, recv_b_ref, thru, token):
        x, y, c, me = _my_position()
        sibling = (x, y, 1 - c)
        for k, peer in enumerate(_first_relations(x, y, c)):
            own, theirs = _own_and_peer_rows(a_ref, me, 4 * peer[0] + 2 * peer[1] + peer[2], 0)
            cp = pltpu.make_async_remote_copy(own, theirs, send_a_ref.at[k], recv_a_ref.at[k], device_id=peer, device_id_type=MESH)
            cp.wait_send()
            cp.wait_recv()
            if k > 0:
                pltpu.make_async_remote_copy(theirs, theirs, send_b_ref.at[k - 1], recv_b_ref.at[k - 1],
                                             device_id=sibling, device_id_type=MESH).start()
        token[...] = jnp.zeros_like(token)

    sem = pltpu.SemaphoreType.DMA((3,))
    out = _call(
        body, name="gather_first_forward",
        out_shape=(sem, sem, pltpu.HBM(arr.shape, arr.dtype), jax.ShapeDtypeStruct((8, LANES), F32)),
        in_specs=[HBM_SPEC, SEM_SPEC, SEM_SPEC, ANY_SPEC],
        out_specs=(SEM_SPEC, SEM_SPEC, HBM_SPEC, pl.BlockSpec(memory_space=pltpu.VMEM)),
        input_output_aliases={0: 2},
        compiler_params=_cparams(has_side_effects=DATAFLOW),
    )(arr, send_a, recv_a, after)
    return out[:3], out[3]


def _gather_first_wait(forwarded, after):
    send_b, recv_b, arr = forwarded

    def body(a_ref, send_b_ref, recv_b_ref, after_ref, thru):
        x, y, c, me = _my_position()
        sibling = (x, y, 1 - c)
        for k, peer in enumerate(_first_relations(x, y, c)[1:]):
            _, sent = _own_and_peer_rows(a_ref, me, 4 * peer[0] + 2 * peer[1] + peer[2], 0)
            _, got = _own_and_peer_rows(a_ref, me, 4 * peer[0] + 2 * peer[1] + (1 - peer[2]), 0)
            cp = pltpu.make_async_remote_copy(sent, got, send_b_ref.at[k], recv_b_ref.at[k], device_id=sibling, device_id_type=MESH)
            cp.wait_send()
            cp.wait_recv()

    return _call(
        body, name="gather_first_wait",
        out_shape=pltpu.HBM(arr.shape, arr.dtype),
        in_specs=[HBM_SPEC, SEM_SPEC, SEM_SPEC, ANY_SPEC],
        out_specs=HBM_SPEC,
        input_output_aliases={0: 0},
        compiler_params=_cparams(has_side_effects=DATAFLOW),
    )(arr, send_b, recv_b, after)


def _scatter_start(fulls, axes, name, after):
    n = len(fulls)
    lands = []
    for f, ax in zip(fulls, axes):
        shp = list(f.shape)
        shp[ax] //= N_DEV
        lands.append(_hbm(lax.empty((N_DEV - 1,) + tuple(shp), f.dtype)))

    def body(*refs):
        srcs, dsts, send_ref, recv_ref, token = refs[:n], refs[n:2 * n], refs[2 * n + 1], refs[2 * n + 2], refs[-1]
        x, y, c, me = _my_position()
        for k, (peer, idx) in enumerate(_peers(x, y, c)):
            for a in range(n):
                _, theirs = _own_and_peer_rows(srcs[a], me, idx, axes[a])
                pltpu.make_async_remote_copy(theirs, dsts[a].at[k], send_ref.at[k * n + a], recv_ref.at[k * n + a],
                                             device_id=peer, device_id_type=MESH).start()
        token[...] = jnp.zeros_like(token)

    sem = pltpu.SemaphoreType.DMA(((N_DEV - 1) * n,))
    out = _call(
        body, name=name,
        out_shape=(sem, sem, *[pltpu.HBM(a.shape, a.dtype) for a in fulls], *[pltpu.HBM(a.shape, a.dtype) for a in lands],
                   jax.ShapeDtypeStruct((8, LANES), F32)),
        in_specs=[HBM_SPEC] * (2 * n) + [ANY_SPEC],
        out_specs=(SEM_SPEC, SEM_SPEC, *[HBM_SPEC] * (2 * n), pl.BlockSpec(memory_space=pltpu.VMEM)),
        input_output_aliases={a: 2 + a for a in range(2 * n)},
        compiler_params=_cparams(has_side_effects=DATAFLOW),
    )(*[_hbm(a) for a in fulls], *lands, after)
    return (out[0], out[1], list(out[2:2 + n]), list(out[2 + n:2 + 2 * n])), out[-1]


def _scatter_wait(started, axes, after, name):
    send_sems, recv_sems, fulls, lands = started
    n = len(fulls)

    def body(*refs):
        srcs, dsts, send_ref, recv_ref = refs[:n], refs[n:2 * n], refs[2 * n], refs[2 * n + 1]
        x, y, c, me = _my_position()
        for k, (peer, idx) in enumerate(_peers(x, y, c)):
            for a in range(n):
                _, theirs = _own_and_peer_rows(srcs[a], me, idx, axes[a])
                cp = pltpu.make_async_remote_copy(theirs, dsts[a].at[k], send_ref.at[k * n + a], recv_ref.at[k * n + a],
                                                  device_id=peer, device_id_type=MESH)
                cp.wait_send()
                cp.wait_recv()

    out = _call(
        body, name=name,
        out_shape=tuple(pltpu.HBM(a.shape, a.dtype) for a in (*fulls, *lands)),
        in_specs=[HBM_SPEC] * (2 * n) + [SEM_SPEC, SEM_SPEC, ANY_SPEC],
        out_specs=tuple([HBM_SPEC] * (2 * n)),
        input_output_aliases={a: a for a in range(2 * n)},
        compiler_params=_cparams(has_side_effects=DATAFLOW),
    )(*fulls, *lands, send_sems, recv_sems, after)
    return list(out[:n]), list(out[n:])


def _prep_weights(me, items, name):
    def body(me_ref, *refs):
        for (_, _, kind), src, dst in zip(items, refs[:len(items)], refs[len(items):]):
            dst[...] = (src[...].T if kind == "T" else src[...]).astype(BF16)

    ins, in_specs, out_shapes, out_specs = [], [], [], []
    for src, j, kind in items:
        shard = src.shape[1:]
        ins.append(src)
        in_specs.append(pl.BlockSpec((None,) + tuple(shard), lambda i, me_ref, j=j, nd=len(shard): (j,) + (0,) * nd))
        if kind == "G":
            out_shapes.append((shard[0], N_DEV * shard[1], shard[2]))
            out_specs.append(pl.BlockSpec(tuple(shard), lambda i, me_ref: (0, me_ref[0], 0)))
        else:
            rows = shard[1] if kind == "T" else shard[0]
            out_shapes.append((N_DEV * rows, D))
            out_specs.append(pl.BlockSpec((rows, D), lambda i, me_ref: (me_ref[0], 0)))
    out = _call(
        body, name=name,
        grid_spec=pltpu.PrefetchScalarGridSpec(num_scalar_prefetch=1, grid=(1,), in_specs=in_specs, out_specs=tuple(out_specs)),
        out_shape=tuple(jax.ShapeDtypeStruct(s, BF16) for s in out_shapes),
        compiler_params=_cparams(1),
    )(me.reshape(1), *ins)
    return list(out)


def _ada_forward(c_all, ada_w):
    cols = ada_w.shape[2]

    def body(c_ref, w_ref, o_ref):
        cv = c_ref[...]
        sc = (cv * _sigmoid(cv)).astype(BF16)
        o_ref[...] = _dot(sc, w_ref[...].astype(BF16))

    return _call(
        body, name="ada_forward", grid=(DEPTH,),
        out_shape=jax.ShapeDtypeStruct((DEPTH, N_DEV, cols), F32),
        in_specs=[pl.BlockSpec((N_DEV, D), lambda i: (0, 0)), pl.BlockSpec((None, D, cols), lambda i: (i, 0, 0))],
        out_specs=pl.BlockSpec((None, N_DEV, cols), lambda i: (i, 0, 0)),
        compiler_params=_cparams(1),
    )(c_all, ada_w)


def _ada_backward_adamw(c_pad, dmod_pad, w, m, v):
    cols = w.shape[2]

    def body(c_ref, dm_ref, w_ref, m_ref, v_ref, g_out, d_out, m_out, v_out):
        cv = c_ref[...]
        sc = (cv * _sigmoid(cv)).astype(BF16)
        g = _dot_tn(sc, dm_ref[...].astype(BF16))
        g_out[...] = g
        d_out[...], m_out[...], v_out[...] = _adamw(w_ref[...], g, m_ref[...], v_ref[...])

    wspec = pl.BlockSpec((None, D, cols), lambda i: (i, 0, 0))
    return _call(
        body, name="ada_backward_adamw", grid=(DEPTH,),
        out_shape=tuple(jax.ShapeDtypeStruct(w.shape, F32) for _ in range(4)),
        in_specs=[pl.BlockSpec((2 * N_DEV, D), lambda i: (0, 0)), pl.BlockSpec((None, 2 * N_DEV, cols), lambda i: (i, 0, 0)),
                  wspec, wspec, wspec],
        out_specs=(wspec, wspec, wspec, wspec),
        compiler_params=_cparams(1),
    )(c_pad, dmod_pad, w, m, v)


def _attn_in_proj(x, rope, rows, mod, layer, w_t, j, gain, bd, tile):
    seq = x.shape[0]

    def body(x_ref, rope_ref, ng_ref, mod_ref, w_ref, gain_ref, bd_ref, qk_ref, qs_ref, kd_ref, vd_ref, g_ref):
        _, _, h = _norm_mod(x_ref[...], ng_ref[...], mod_ref[1:2, :], mod_ref[0:1, :])
        hb = h.astype(BF16)
        tabs = _rope_tabs(rope_ref)
        low = _low_half(tile)
        bdm = bd_ref[...]

        def put_kv(ref, blk, first_kv):
            sw = pltpu.roll(blk, HEAD_DIM, 1)
            ref[:, LANES * first_kv:LANES * (first_kv + 1)] = jnp.where(low, blk, sw).astype(BF16)
            ref[:, LANES * (first_kv + 1):LANES * (first_kv + 2)] = jnp.where(low, sw, blk).astype(BF16)

        def project(c):
            return _dot_nt(hb, w_ref[CHUNK * c:CHUNK * (c + 1), :])

        n_chunks = ATTN_IN // CHUNK
        per = CHUNK // LANES
        nxt = project(0)
        for c in range(n_chunks):
            cur = nxt
            if c + 1 < n_chunks:
                nxt = project(c + 1)
            col = CHUNK * c
            if col >= QK_W + N_KV * HEAD_DIM:
                g_ref[:, col - QK_W - N_KV * HEAD_DIM:col - QK_W - N_KV * HEAD_DIM + CHUNK] = cur.astype(BF16)
            elif col >= QK_W:
                for t in range(per):
                    put_kv(vd_ref, cur[:, LANES * t:LANES * (t + 1)], (col - QK_W) // HEAD_DIM + 2 * t)
            else:
                qk_ref[:, col:col + CHUNK] = cur
                for t in range(per):
                    b = per * c + t
                    blk = cur[:, LANES * t:LANES * (t + 1)]
                    ms = _group_mean(blk * blk, bdm)
                    y = (blk * lax.rsqrt(ms + NORM_EPS)) * gain_ref[:, LANES * b:LANES * (b + 1)]
                    rp = _rope(y, tabs)
                    if b < D // LANES:
                        rp = rp * (HEAD_DIM ** -0.5)
                        qs_ref[:, 2 * LANES * b:2 * LANES * b + LANES] = jnp.where(low, rp, 0.0).astype(BF16)
                        qs_ref[:, 2 * LANES * b + LANES:2 * LANES * (b + 1)] = jnp.where(low, 0.0, rp).astype(BF16)
                    else:
                        put_kv(kd_ref, rp, 2 * (b - D // LANES))

    row = lambda w: pl.BlockSpec((tile, w), lambda i: (i, 0))
    return _call(
        body, name=f"attn_in_proj_{j}", grid=(seq // tile,),
        out_shape=(jax.ShapeDtypeStruct((seq, QK_W), F32), jax.ShapeDtypeStruct((seq, N_HEADS * LANES), BF16),
                   jax.ShapeDtypeStruct((seq, KX_W), BF16), jax.ShapeDtypeStruct((seq, KX_W), BF16),
                   jax.ShapeDtypeStruct((seq, D), BF16)),
        in_specs=[row(D), row(3 * LANES), _mod_row_spec(layer, NORM_ROW), _mod_spec(layer), _const_spec((ATTN_IN, D)),
                  _const_spec((1, QK_W)), _const_spec((LANES, LANES))],
        out_specs=(row(QK_W), row(N_HEADS * LANES), row(KX_W), row(KX_W), row(D)),
        compiler_params=_cparams(1),
    )(x, rope, rows, mod, w_t, gain, bd)


def _band_mask(n, rows, keys_on_rows):
    shape = (2 * QBLK, rows) if keys_on_rows else (rows, 2 * QBLK)
    qi = lax.broadcasted_iota(jnp.int32, shape, 1 if keys_on_rows else 0) & (QBLK - 1)
    kj = lax.broadcasted_iota(jnp.int32, shape, 0 if keys_on_rows else 1)
    diff = QBLK + qi - kj
    first_key = jnp.where(n > 0, 0, QBLK)
    return (diff >= 0) & (diff < QBLK) & (kj >= first_key)


def _stack_heads(ref, heads):
    return jnp.concatenate([ref[:, LANES * h:LANES * (h + 1)] for h in heads], axis=0)


def _kv_block(prev_ref, cur_ref, kv):
    cols = slice(LANES * kv, LANES * (kv + 1))
    return jnp.concatenate([prev_ref[:, cols], cur_ref[:, cols]], axis=0)


def _pair_up(st, low):
    return jnp.concatenate([jnp.where(low, st[0:QBLK], st[QBLK:2 * QBLK]),
                            jnp.where(low, st[2 * QBLK:3 * QBLK], st[3 * QBLK:4 * QBLK])], axis=1)


def _attn_forward(sinks, qs, kd, vd, j):
    seq = qs.shape[0]
    nb = seq // QBLK

    def body(sink_ref, q_ref, kp_ref, kc_ref, vp_ref, vc_ref, o_ref):
        n = pl.program_id(0)
        ok = _band_mask(n, 4 * QBLK, False)
        low = _low_half(QBLK)
        rowi = lax.broadcasted_iota(jnp.int32, (4 * QBLK, 1), 0)

        def scores(kv):
            return _dot_nt(_stack_heads(q_ref, range(4 * kv, 4 * kv + 4)), _kv_block(kp_ref, kc_ref, kv))

        nxt = scores(0)
        for kv in range(N_KV):
            s = jnp.where(ok, nxt, -1e30)
            if kv + 1 < N_KV:
                nxt = scores(kv + 1)
            sink = jnp.where(rowi < QBLK, sink_ref[j, 4 * kv],
                             jnp.where(rowi < 2 * QBLK, sink_ref[j, 4 * kv + 1],
                                       jnp.where(rowi < 3 * QBLK, sink_ref[j, 4 * kv + 2], sink_ref[j, 4 * kv + 3])))
            m = jnp.maximum(jnp.max(s, axis=1, keepdims=True), sink)
            p = jnp.exp(s - m)
            den = jnp.sum(p, axis=1, keepdims=True) + jnp.exp(sink - m)
            o_st = _dot((p / den).astype(BF16), _kv_block(vp_ref, vc_ref, kv))
            o_ref[:, 2 * LANES * kv:2 * LANES * (kv + 1)] = _pair_up(o_st, low).astype(BF16)

    blk = lambda w: pl.BlockSpec((QBLK, w), lambda n: (n, 0))
    prev = lambda w: pl.BlockSpec((QBLK, w), lambda n: (jnp.maximum(n - 1, 0), 0))
    return _call(
        body, name=f"attn_forward_{j}", grid=(nb,),
        out_shape=jax.ShapeDtypeStruct((seq, D), BF16),
        in_specs=[pl.BlockSpec(memory_space=pltpu.SMEM), blk(N_HEADS * LANES), prev(KX_W), blk(KX_W), prev(KX_W), blk(KX_W)],
        out_specs=blk(D),
        compiler_params=_cparams(1),
    )(sinks, qs, kd, kd, vd, vd)


def _attn_out_proj(x, o, g, w, j, mod, layer, tile):
    seq = x.shape[0]

    def body(x_ref, o_ref, g_ref, w_ref, mod_ref, xo_ref, br_ref):
        gv = g_ref[...].astype(F32)
        u = (o_ref[...].astype(F32) * (gv * _sigmoid(gv))).astype(BF16)
        br = _dot(u, w_ref[...])
        br_ref[...] = br.astype(BF16)
        xo_ref[...] = x_ref[...] + mod_ref[2:3, :] * br

    row = pl.BlockSpec((tile, D), lambda i: (i, 0))
    return _call(
        body, name=f"attn_out_proj_{j}", grid=(seq // tile,),
        out_shape=(jax.ShapeDtypeStruct((seq, D), F32), jax.ShapeDtypeStruct((seq, D), BF16)),
        in_specs=[row, row, row, _const_spec((D, D)), _mod_spec(layer)],
        out_specs=(row, row),
        compiler_params=_cparams(1),
    )(x, o, g, w, mod)


def _attn_out_proj_bwd(dxn, br, o, g, w, j, mod, layer, tile, after):
    seq = dxn.shape[0]
    steps = seq // tile

    def body(dxn_ref, br_ref, o_ref, g_ref, w_ref, mod_ref, after_ref, do_ref, dg_ref, dw_ref, dgate_ref, dw_acc):
        i = pl.program_id(0)

        @pl.when(i == 0)
        def _():
            dw_acc[...] = jnp.zeros_like(dw_acc)
            dgate_ref[...] = jnp.zeros_like(dgate_ref)

        dxn_v, ov, gv = dxn_ref[...], o_ref[...].astype(F32), g_ref[...].astype(F32)
        dgate_ref[...] += jnp.sum(dxn_v * br_ref[...].astype(F32), axis=0, keepdims=True)
        dbr = (dxn_v * mod_ref[2:3, :]).astype(BF16)
        du = _dot_nt(dbr, w_ref[...])
        sg = _sigmoid(gv)
        sl = gv * sg
        dw_acc[...] += _dot_tn((ov * sl).astype(BF16), dbr)
        do = du * sl
        dg_ref[...] = (du * ov * (sg * (1.0 + gv * (1.0 - sg)))).astype(BF16)
        low = _low_half(tile)
        for b in range(D // LANES):
            blk = do[:, LANES * b:LANES * (b + 1)]
            do_ref[:, 2 * LANES * b:2 * LANES * b + LANES] = jnp.where(low, blk, 0.0).astype(BF16)
            do_ref[:, 2 * LANES * b + LANES:2 * LANES * (b + 1)] = jnp.where(low, 0.0, blk).astype(BF16)

        @pl.when(i == steps - 1)
        def _():
            dw_ref[...] = dw_acc[...].astype(BF16)

    row = lambda w_: pl.BlockSpec((tile, w_), lambda i: (i, 0))
    return _call(
        body, name=f"attn_out_proj_bwd_{j}", grid=(steps,),
        out_shape=(jax.ShapeDtypeStruct((seq, N_HEADS * LANES), BF16), jax.ShapeDtypeStruct((seq, D), BF16),
                   jax.ShapeDtypeStruct((D, D), BF16), jax.ShapeDtypeStruct((1, D), F32)),
        in_specs=[row(D), row(D), row(D), row(D), _const_spec((D, D)), _mod_spec(layer), ANY_SPEC],
        out_specs=(row(N_HEADS * LANES), row(D), pl.BlockSpec((D, D), lambda i: (0, 0)),
                   pl.BlockSpec((1, D), lambda i: (0, 0))),
        scratch_shapes=[pltpu.VMEM((D, D), F32)],
        compiler_params=_cparams(1),
    )(dxn, br, o, g, w, mod, after)


def _attn_backward(sinks, qs, dos, kd, vd, j, after):
    seq = qs.shape[0]
    nb = seq // QBLK

    def body(sink_ref, q_ref, do_ref, kp_ref, kc_ref, vp_ref, vc_ref, after_ref, dq_ref, dk_ref, dv_ref, dsink_ref,
             carry_k, carry_v, sink_acc):
        n = pl.program_id(0)

        @pl.when(n == 0)
        def _():
            carry_k[...] = jnp.zeros_like(carry_k)
            carry_v[...] = jnp.zeros_like(carry_v)
            sink_acc[...] = jnp.zeros_like(sink_acc)

        @pl.when(n < nb)
        def _():
            ok = _band_mask(n, 2 * QBLK, True)
            low = _low_half(QBLK)
            lane_q = lax.broadcasted_iota(jnp.int32, (1, 2 * QBLK), 1)
            dk_parts, dv_parts = [], []

            def first_products(g):
                kv, half = divmod(g, 2)
                heads = (4 * kv + half, 4 * kv + 2 + half)
                q = _stack_heads(q_ref, heads)
                do = _stack_heads(do_ref, heads)
                kk = _kv_block(kp_ref, kc_ref, kv)
                return heads, q, do, kk, _dot_nt(kk, q), _dot_nt(_kv_block(vp_ref, vc_ref, kv), do)

            nxt = first_products(0)
            dq_h, dk_kv, dv_kv = [], None, None
            for g in range(2 * N_KV):
                heads, q, do, kk, s_raw, dp_raw = nxt
                if g + 1 < 2 * N_KV:
                    nxt = first_products(g + 1)
                st = jnp.where(ok, s_raw, -1e30)
                sink = jnp.where(lane_q < QBLK, sink_ref[j, heads[0]], sink_ref[j, heads[1]])
                m = jnp.maximum(jnp.max(st, axis=0, keepdims=True), sink)
                e = jnp.exp(st - m)
                e_sink = jnp.exp(sink - m)
                inv = 1.0 / (jnp.sum(e, axis=0, keepdims=True) + e_sink)
                p = e * inv
                pdp = p * dp_raw
                delta = jnp.sum(pdp, axis=0, keepdims=True)
                ds = (pdp - p * delta).astype(BF16)
                sink_acc[g:g + 1, :] -= e_sink * inv * delta
                dk_g, dv_g = _dot(ds, q), _dot(p.astype(BF16), do)
                dk_kv = dk_g if dk_kv is None else dk_kv + dk_g
                dv_kv = dv_g if dv_kv is None else dv_kv + dv_g
                dq_h.append(_dot_tn(ds, kk))
                if g % 2 == 1:
                    kv = g // 2
                    for t in range(2):
                        dq_ref[:, LANES * (2 * kv + t):LANES * (2 * kv + t + 1)] = jnp.where(
                            low, dq_h[0][QBLK * t:QBLK * (t + 1)], dq_h[1][QBLK * t:QBLK * (t + 1)])
                    dk_parts.append(dk_kv + pltpu.roll(dk_kv, HEAD_DIM, 1))
                    dv_parts.append(dv_kv + pltpu.roll(dv_kv, HEAD_DIM, 1))
                    dq_h, dk_kv, dv_kv = [], None, None

            def order(parts, lo, hi):
                return jnp.concatenate([jnp.where(low, parts[0][lo:hi], parts[1][lo:hi]),
                                        jnp.where(low, parts[2][lo:hi], parts[3][lo:hi])], axis=1)

            dk_ref[...] = carry_k[...] + order(dk_parts, 0, QBLK)
            dv_ref[...] = (carry_v[...] + order(dv_parts, 0, QBLK)).astype(BF16)
            carry_k[...] = order(dk_parts, QBLK, 2 * QBLK)
            carry_v[...] = order(dv_parts, QBLK, 2 * QBLK)

        @pl.when(n == nb)
        def _():
            dk_ref[...] = carry_k[...]
            dv_ref[...] = carry_v[...].astype(BF16)
            lane = lax.broadcasted_iota(jnp.int32, (1, LANES), 1)
            out = jnp.zeros((1, LANES), F32)
            for g in range(2 * N_KV):
                for t in range(2):
                    tot = jnp.sum(sink_acc[g:g + 1, QBLK * t:QBLK * (t + 1)], axis=1, keepdims=True)
                    out = jnp.where(lane == 4 * (g // 2) + 2 * t + g % 2, tot, out)
            dsink_ref[...] = out

    cur = lambda w: pl.BlockSpec((QBLK, w), lambda n: (jnp.minimum(n, nb - 1), 0))
    prev = lambda w: pl.BlockSpec((QBLK, w), lambda n: (jnp.maximum(n - 1, 0), 0))
    return _call(
        body, name=f"attn_backward_{j}", grid=(nb + 1,),
        out_shape=(jax.ShapeDtypeStruct((seq, D), F32), jax.ShapeDtypeStruct((seq, N_KV * HEAD_DIM), F32),
                   jax.ShapeDtypeStruct((seq, N_KV * HEAD_DIM), BF16), jax.ShapeDtypeStruct((1, LANES), F32)),
        in_specs=[pl.BlockSpec(memory_space=pltpu.SMEM), cur(N_HEADS * LANES), cur(N_HEADS * LANES), prev(KX_W), cur(KX_W),
                  prev(KX_W), cur(KX_W), ANY_SPEC],
        out_specs=(cur(D), prev(N_KV * HEAD_DIM), prev(N_KV * HEAD_DIM), pl.BlockSpec((1, LANES), lambda n: (0, 0))),
        scratch_shapes=[pltpu.VMEM((QBLK, N_KV * HEAD_DIM), F32), pltpu.VMEM((QBLK, N_KV * HEAD_DIM), F32),
                        pltpu.VMEM((2 * N_KV, 2 * QBLK), F32)],
        compiler_params=_cparams(1),
    )(sinks, qs, dos, kd, kd, vd, vd, after)


def _in_proj_tail(x_ref, dxn_ref, ng_ref, mod_ref, w_ref, dproj, dx_ref, dw_acc, vec_acc):
    ng, sc, sh = ng_ref[...], mod_ref[1:2, :], mod_ref[0:1, :]
    xh, r, h = _norm_mod(x_ref[...], ng, sc, sh)
    dh = _dot(dproj, w_ref[...])
    dw_acc[...] += _dot_tn(dproj, h.astype(BF16))
    vec_acc[0:1, :] += jnp.sum(dh, axis=0, keepdims=True)
    vec_acc[1:2, :] += jnp.sum(dh * xh, axis=0, keepdims=True)
    dxh = dh * (ng * (1.0 + sc))
    dx_ref[...] = dxn_ref[...] + r * (dxh - xh * jnp.mean(dxh * xh, axis=-1, keepdims=True))


def _tail_finish(ng_ref, mod_ref, dw_ref, vec_ref, dw_acc, vec_acc):
    dw_ref[...] = dw_acc[...].astype(BF16)
    a = vec_acc[1:2, :]
    vec_ref[...] = jnp.zeros_like(vec_ref)
    vec_ref[0:1, :] = vec_acc[0:1, :]
    vec_ref[1:2, :] = a * ng_ref[...]
    vec_ref[3:4, :] = a * (1.0 + mod_ref[1:2, :])


def _attn_in_proj_bwd(x, dxn, rope, qk_raw, dq, dk, dv, dg, rows, mod, layer, w_t, j, gain, bd, tile):
    seq = x.shape[0]
    steps = seq // tile

    def body(x_ref, dxn_ref, rope_ref, qk_ref, dq_ref, dk_ref, dv_ref, dg_ref, ng_ref, mod_ref, w_ref, gain_ref,
             bd_ref, dx_ref, dw_ref, vec_ref, dgain_ref, dproj, dw_acc, vec_acc):
        i = pl.program_id(0)

        @pl.when(i == 0)
        def _():
            dw_acc[...] = jnp.zeros_like(dw_acc)
            vec_acc[...] = jnp.zeros_like(vec_acc)
            dgain_ref[...] = jnp.zeros_like(dgain_ref)

        tabs = _rope_tabs(rope_ref)
        bdm = bd_ref[...]
        for b in range(QK_W // LANES):
            cols = slice(LANES * b, LANES * (b + 1))
            raw = qk_ref[:, cols]
            if b < D // LANES:
                dy = dq_ref[:, cols] * (HEAD_DIM ** -0.5)
            else:
                dy = dk_ref[:, LANES * (b - D // LANES):LANES * (b + 1 - D // LANES)]
            dy = _rope_bwd(dy, tabs)
            rr = lax.rsqrt(_group_mean(raw * raw, bdm) + NORM_EPS)
            xh = raw * rr
            dgain_ref[:, cols] += jnp.sum(dy * xh, axis=0, keepdims=True)
            dxh = dy * gain_ref[:, cols]
            dproj[:, cols] = (rr * (dxh - xh * _group_mean(dxh * xh, bdm))).astype(BF16)
        dproj[:, QK_W:QK_W + N_KV * HEAD_DIM] = dv_ref[...]
        dproj[:, QK_W + N_KV * HEAD_DIM:] = dg_ref[...]
        _in_proj_tail(x_ref, dxn_ref, ng_ref, mod_ref, w_ref, dproj[...], dx_ref, dw_acc, vec_acc)

        @pl.when(i == steps - 1)
        def _():
            _tail_finish(ng_ref, mod_ref, dw_ref, vec_ref, dw_acc, vec_acc)

    row = lambda w, dt=None: pl.BlockSpec((tile, w), lambda i: (i, 0))
    fixed = lambda shape: pl.BlockSpec(shape, lambda i: (0,) * len(shape))
    return _call(
        body, name=f"attn_in_proj_bwd_{j}", grid=(steps,),
        out_shape=(jax.ShapeDtypeStruct((seq, D), F32), jax.ShapeDtypeStruct((ATTN_IN, D), BF16),
                   jax.ShapeDtypeStruct((8, D), F32), jax.ShapeDtypeStruct((1, QK_W), F32)),
        in_specs=[row(D), row(D), row(3 * LANES), row(QK_W), row(D), row(N_KV * HEAD_DIM), row(N_KV * HEAD_DIM), row(D),
                  _mod_row_spec(layer, NORM_ROW), _mod_spec(layer), _const_spec((ATTN_IN, D)), _const_spec((1, QK_W)),
                  _const_spec((LANES, LANES))],
        out_specs=(row(D), fixed((ATTN_IN, D)), fixed((8, D)), fixed((1, QK_W))),
        scratch_shapes=[pltpu.VMEM((tile, ATTN_IN), BF16), pltpu.VMEM((ATTN_IN, D), F32), pltpu.VMEM((8, D), F32)],
        compiler_params=_cparams(1),
    )(x, dxn, rope, qk_raw, dq, dk, dv, dg, rows, mod, w_t, gain, bd)


def _pool_in_proj(x, rows, mod, layer, w_t, j, tile):
    seq = x.shape[0]

    def body(x_ref, ng_ref, mod_ref, w_ref, v_ref, g_ref):
        _, _, h = _norm_mod(x_ref[...], ng_ref[...], mod_ref[1:2, :], mod_ref[0:1, :])
        proj = _dot_nt(h.astype(BF16), w_ref[...])
        v_ref[...] = proj[:, :D].astype(BF16)
        g_ref[...] = proj[:, D:].astype(BF16)

    row = pl.BlockSpec((tile, D), lambda i: (i, 0))
    return _call(
        body, name=f"pool_in_proj_{j}", grid=(seq // tile,),
        out_shape=(jax.ShapeDtypeStruct((seq, D), BF16), jax.ShapeDtypeStruct((seq, D), BF16)),
        in_specs=[row, _mod_row_spec(layer, NORM_ROW), _mod_spec(layer), _const_spec((POOL_IN, D))],
        out_specs=(row, row),
        compiler_params=_cparams(1),
    )(x, rows, mod, w_t)


PAD = 8


def _window_sums(ext, lo, hi, forward):
    gw = D // len(POOL_WINDOWS)
    planes = []
    for gi, w in enumerate(POOL_WINDOWS):
        cols = slice(gw * gi, gw * (gi + 1))
        src, k = 0, 1
        while k < w:
            d = k if forward else -k
            ext[1 - src, lo:hi, cols] = ext[src, lo:hi, cols] + ext[src, lo + d:hi + d, cols]
            src, k = 1 - src, 2 * k
        planes.append(src)
    return planes


def _pooled(ext, v_ref, first, tile):
    t_abs = first + lax.broadcasted_iota(jnp.int32, (tile, 1), 0)
    top = PAD + HALO
    planes = _window_sums(ext, PAD, top + tile, False)
    outs = []
    gw = D // len(POOL_WINDOWS)
    for gi, w in enumerate(POOL_WINDOWS):
        cols = slice(gw * gi, gw * (gi + 1))
        cnt = jnp.minimum(t_abs + 1, w).astype(F32)
        outs.append(ext[planes[gi], top:top + tile, cols] / cnt - v_ref[:, cols].astype(F32))
    return jnp.concatenate(outs, axis=1)


def _fill_ext(ext, halo_ref, v_ref, i, tile):
    ext[0, 0:PAD, :] = jnp.zeros((PAD, D), F32)
    ext[1, 0:PAD, :] = jnp.zeros((PAD, D), F32)
    ext[0, PAD:PAD + HALO, :] = jnp.where(i == 0, 0.0, halo_ref[...].astype(F32))
    ext[0, PAD + HALO:PAD + HALO + tile, :] = v_ref[...].astype(F32)


def _group_mix(pb, wg_ref):
    gw = D // len(POOL_WINDOWS)
    return jnp.concatenate([_dot(pb[:, gw * gi:gw * (gi + 1)], wg_ref[gi]) for gi in range(len(POOL_WINDOWS))], axis=1)


def _pool_mix_out(x, v, g, wg, w_out, j, rows, mod, layer, tile, target=None):
    seq = x.shape[0]

    def body(*refs):
        if target is None:
            x_ref, v_ref, halo_ref, g_ref, wg_ref, w_ref, scale_ref, mod_ref, xo_ref, br_ref, ext = refs
        else:
            x_ref, v_ref, halo_ref, g_ref, wg_ref, w_ref, scale_ref, mod_ref, t_ref, xo_ref, br_ref, loss_ref, ext = refs
        i = pl.program_id(0)
        _fill_ext(ext, halo_ref, v_ref, i, tile)
        pb = _pooled(ext, v_ref, i * tile, tile).astype(BF16)
        ms = _group_mix(pb, wg_ref) * scale_ref[...]
        gv = g_ref[...].astype(F32)
        u = (ms * (gv * _sigmoid(gv))).astype(BF16)
        br = _dot(u, w_ref[...])
        br_ref[...] = br.astype(BF16)
        y = x_ref[...] + mod_ref[2:3, :] * br
        if target is None:
            xo_ref[...] = y
        else:
            @pl.when(i == 0)
            def _():
                loss_ref[...] = jnp.zeros_like(loss_ref)

            e = y - t_ref[...]
            xo_ref[...] = e * (1.0 / D)
            loss_ref[...] += 0.5 * jnp.sum(jnp.mean(e * e, axis=-1, keepdims=True), axis=0, keepdims=True)

    row = pl.BlockSpec((tile, D), lambda i: (i, 0))
    halo = pl.BlockSpec((HALO, D), lambda i: (jnp.maximum(i * (tile // HALO) - 1, 0), 0))
    extra_in, extra_out, extra_shape = ([], (), ()) if target is None else (
        [row], (pl.BlockSpec((1, LANES), lambda i: (0, 0)),), (jax.ShapeDtypeStruct((1, LANES), F32),))
    return _call(
        body, name=f"pool_mix_out_{j}", grid=(seq // tile,),
        out_shape=(jax.ShapeDtypeStruct((seq, D), F32), jax.ShapeDtypeStruct((seq, D), BF16)) + extra_shape,
        in_specs=[row, row, halo, row, _const_spec(wg.shape), _const_spec((D, D)), _mod_row_spec(layer, POOL_SCALE_ROW),
                  _mod_spec(layer)] + extra_in,
        out_specs=(row, row) + extra_out,
        scratch_shapes=[pltpu.VMEM((2, tile + HALO + PAD, D), F32)],
        compiler_params=_cparams(1),
    )(x, v, v, g, wg, w_out, rows, mod, *(() if target is None else (target,)))


def _pool_mix_out_bwd(dxn, br, v, g, wg, w_out, j, rows, mod, layer, tile, after):
    seq = dxn.shape[0]
    steps = seq // tile
    ng_ = len(POOL_WINDOWS)
    gw = D // ng_

    def body(dxn_ref, br_ref, v_ref, halo_ref, g_ref, wg_ref, w_ref, scale_ref, mod_ref, after_ref,
             dpool_ref, dg_ref, dw_ref, dwg_ref, vec_ref, ext, dw_acc, dwg_acc):
        i = pl.program_id(0)

        @pl.when(i == 0)
        def _():
            dw_acc[...] = jnp.zeros_like(dw_acc)
            dwg_acc[...] = jnp.zeros_like(dwg_acc)
            vec_ref[...] = jnp.zeros_like(vec_ref)

        _fill_ext(ext, halo_ref, v_ref, i, tile)
        pb = _pooled(ext, v_ref, i * tile, tile).astype(BF16)
        mixed = _group_mix(pb, wg_ref)
        scale = scale_ref[...]
        ms = mixed * scale
        gv, dxn_v = g_ref[...].astype(F32), dxn_ref[...]
        sg = _sigmoid(gv)
        sl = gv * sg
        vec_ref[0:1, :] += jnp.sum(dxn_v * br_ref[...].astype(F32), axis=0, keepdims=True)
        dbr = (dxn_v * mod_ref[2:3, :]).astype(BF16)
        du = _dot_nt(dbr, w_ref[...])
        dw_acc[...] += _dot_tn((ms * sl).astype(BF16), dbr)
        dms = du * sl
        dg_ref[...] = (du * ms * (sg * (1.0 + gv * (1.0 - sg)))).astype(BF16)
        vec_ref[1:2, :] += jnp.sum(dms * mixed, axis=0, keepdims=True)
        dmx = (dms * scale).astype(BF16)
        for gi in range(ng_):
            cols = slice(gw * gi, gw * (gi + 1))
            dpool_ref[:, cols] = _dot_nt(dmx[:, cols], wg_ref[gi])
            dwg_acc[gi] += _dot_tn(pb[:, cols], dmx[:, cols])

        @pl.when(i == steps - 1)
        def _():
            dw_ref[...] = dw_acc[...].astype(BF16)
            dwg_ref[...] = dwg_acc[...].astype(BF16)

    row = pl.BlockSpec((tile, D), lambda i: (i, 0))
    halo = pl.BlockSpec((HALO, D), lambda i: (jnp.maximum(i * (tile // HALO) - 1, 0), 0))
    fixed = lambda shape: pl.BlockSpec(shape, lambda i: (0,) * len(shape))
    return _call(
        body, name=f"pool_mix_out_bwd_{j}", grid=(steps,),
        out_shape=(jax.ShapeDtypeStruct((seq, D), F32), jax.ShapeDtypeStruct((seq, D), BF16),
                   jax.ShapeDtypeStruct((D, D), BF16), jax.ShapeDtypeStruct((ng_, gw, gw), BF16),
                   jax.ShapeDtypeStruct((8, D), F32)),
        in_specs=[row, row, row, halo, row, _const_spec(wg.shape), _const_spec((D, D)), _mod_row_spec(layer, POOL_SCALE_ROW),
                  _mod_spec(layer), ANY_SPEC],
        out_specs=(row, row, fixed((D, D)), fixed((ng_, gw, gw)), fixed((8, D))),
        scratch_shapes=[pltpu.VMEM((2, tile + HALO + PAD, D), F32), pltpu.VMEM((D, D), F32), pltpu.VMEM((ng_, gw, gw), F32)],
        compiler_params=_cparams(1),
    )(dxn, br, v, v, g, wg, w_out, rows, mod, after)


def _pool_in_proj_bwd(x, dxn, dpool, dg, rows, mod, layer, w_t, j, tile, after):
    seq = x.shape[0]
    steps = seq // tile
    gw = D // len(POOL_WINDOWS)

    def body(x_ref, dxn_ref, dp_ref, halo_ref, dg_ref, ng_ref, mod_ref, w_ref, after_ref, dx_ref, dw_ref, vec_ref,
             ext, dproj, dw_acc, vec_acc):
        i = pl.program_id(0)

        @pl.when(i == 0)
        def _():
            dw_acc[...] = jnp.zeros_like(dw_acc)
            vec_acc[...] = jnp.zeros_like(vec_acc)

        t_abs = i * tile + lax.broadcasted_iota(jnp.int32, (tile, 1), 0)
        last = i == steps - 1
        ext[0, tile + HALO:tile + HALO + PAD, :] = jnp.zeros((PAD, D), F32)
        ext[1, tile + HALO:tile + HALO + PAD, :] = jnp.zeros((PAD, D), F32)
        for gi, w in enumerate(POOL_WINDOWS):
            cols = slice(gw * gi, gw * (gi + 1))
            cnt = jnp.minimum(t_abs + 1, w).astype(F32)
            ext[0, 0:tile, cols] = dp_ref[:, cols] / cnt
            ext[0, tile:tile + HALO, cols] = jnp.where(last, 0.0, halo_ref[:, cols] * (1.0 / w))
        planes = _window_sums(ext, 0, tile + HALO, True)
        for gi, w in enumerate(POOL_WINDOWS):
            cols = slice(gw * gi, gw * (gi + 1))
            dproj[:, cols] = (ext[planes[gi], 0:tile, cols] - dp_ref[:, cols]).astype(BF16)
        dproj[:, D:] = dg_ref[...]
        _in_proj_tail(x_ref, dxn_ref, ng_ref, mod_ref, w_ref, dproj[...], dx_ref, dw_acc, vec_acc)

        @pl.when(last)
        def _():
            _tail_finish(ng_ref, mod_ref, dw_ref, vec_ref, dw_acc, vec_acc)

    row = pl.BlockSpec((tile, D), lambda i: (i, 0))
    halo = pl.BlockSpec((HALO, D), lambda i: (jnp.minimum((i + 1) * (tile // HALO), seq // HALO - 1), 0))
    fixed = lambda shape: pl.BlockSpec(shape, lambda i: (0,) * len(shape))
    return _call(
        body, name=f"pool_in_proj_bwd_{j}", grid=(steps,),
        out_shape=(jax.ShapeDtypeStruct((seq, D), F32), jax.ShapeDtypeStruct((POOL_IN, D), BF16),
                   jax.ShapeDtypeStruct((8, D), F32)),
        in_specs=[row, row, row, halo, row, _mod_row_spec(layer, NORM_ROW), _mod_spec(layer), _const_spec((POOL_IN, D)), ANY_SPEC],
        out_specs=(row, fixed((POOL_IN, D)), fixed((8, D))),
        scratch_shapes=[pltpu.VMEM((2, tile + HALO + PAD, D), F32), pltpu.VMEM((tile, POOL_IN), BF16), pltpu.VMEM((POOL_IN, D), F32),
                        pltpu.VMEM((8, D), F32)],
        compiler_params=_cparams(1),
    )(x, dxn, dpool, dpool, dg, rows, mod, w_t, after)


def _build_vec(vecs, gates, pool_vecs, gains, dsinks, loss_part):
    def body(v0, v1, v2, v3, g0, g2, p0, p1, n0, n1, s0, s1, loss_ref, out):
        out[...] = jnp.zeros_like(out)
        for i, v in enumerate((v0, v1, v2, v3)):
            out[3 * i:3 * i + 2, :] = v[0:2, :]
            out[12 + i:13 + i, :] = v[3:4, :]
        out[2:3, :] = g0[...]
        out[8:9, :] = g2[...]
        for j, (p, n, s) in enumerate(((p0, n0, s0), (p1, n1, s1))):
            out[3 * (2 * j + 1) + 2:3 * (2 * j + 1) + 3, :] = p[0:1, :]
            out[22 + j:23 + j, :] = p[1:2, :]
            out[16 + j:17 + j, :] = n[:, 0:D]
            out[18 + j:19 + j, 0:QK_W - D] = n[:, D:QK_W]
            out[20 + j:21 + j, 0:LANES] = s[...]
        out[24:25, 0:LANES] = loss_ref[...]

    vm = pl.BlockSpec(memory_space=pltpu.VMEM)
    args = (*vecs, gates[0], gates[2], *pool_vecs, *gains, *dsinks, loss_part)
    return _call(
        body, name="build_vec",
        out_shape=jax.ShapeDtypeStruct((VEC_ROWS, D), F32),
        in_specs=[vm] * len(args), out_specs=vm,
        compiler_params=_cparams(),
    )(*args)


def _sum_devices(g, after):
    rows = g.shape[1]

    def body(g_ref, after_ref, tot_ref, fold_ref):
        tot = g_ref[0]
        for p in range(1, N_DEV):
            tot = tot + g_ref[p]
        tot_ref[...] = tot
        f = tot[16:24, 0:LANES]
        for b in range(1, D // LANES):
            f = f + tot[16:24, LANES * b:LANES * (b + 1)]
        fold_ref[...] = f + pltpu.roll(f, HEAD_DIM, 1)

    return _call(
        body, name="sum_devices",
        out_shape=(jax.ShapeDtypeStruct((rows, D), F32), jax.ShapeDtypeStruct((8, LANES), F32)),
        in_specs=[pl.BlockSpec(memory_space=pltpu.VMEM), ANY_SPEC],
        out_specs=(pl.BlockSpec(memory_space=pltpu.VMEM), pl.BlockSpec(memory_space=pltpu.VMEM)),
        compiler_params=_cparams(),
    )(g, after)


def _adamw_small(params):
    n = len(params)

    def body(*refs):
        ins, outs = refs[:4 * n], refs[4 * n:]
        for p in range(n):
            w_ref, g_ref, m_ref, v_ref = ins[4 * p:4 * p + 4]
            outs[3 * p][...], outs[3 * p + 1][...], outs[3 * p + 2][...] = _adamw(w_ref[...], g_ref[...], m_ref[...], v_ref[...])

    vm = pl.BlockSpec(memory_space=pltpu.VMEM)
    out = _call(
        body, name="adamw_small",
        out_shape=tuple(jax.ShapeDtypeStruct(w.shape, F32) for (w, _, _, _) in params for _ in range(3)),
        in_specs=[vm] * (4 * n), out_specs=tuple([vm] * (3 * n)),
        compiler_params=_cparams(),
    )(*[a for p in params for a in p])
    return [tuple(out[3 * p:3 * p + 3]) for p in range(n)]


def _adamw_shards(name, me, fulls, lands, w, m, v, transpose, axis=0):
    nl = w.shape[0]
    wshape = w.shape[1:]
    own_shape = lands[0].shape[1:]

    def body(me_ref, *refs):
        own_refs, land_refs = refs[:nl], refs[nl:2 * nl]
        w_ref, m_ref, v_ref, g_out, d_out, m_out, v_out = refs[2 * nl:]
        layer = pl.program_id(0)
        for l in range(nl):
            @pl.when(layer == l)
            def _(l=l):
                g = own_refs[l][...].astype(F32)
                for k in range(N_DEV - 1):
                    g = g + land_refs[l][k].astype(F32)
                if transpose:
                    g = g.T
                g_out[...] = g
                d_out[...], m_out[...], v_out[...] = _adamw(w_ref[...], g, m_ref[...], v_ref[...])

    def own_index(l_, me_ref):
        idx = [0] * len(own_shape)
        idx[axis] = me_ref[0]
        return tuple(idx)

    own_spec = pl.BlockSpec(tuple(own_shape), own_index)
    land_spec = pl.BlockSpec((N_DEV - 1,) + tuple(own_shape), lambda l_, me_ref: (0,) * (1 + len(own_shape)))
    wspec = pl.BlockSpec((None,) + tuple(wshape), lambda l_, me_ref: (l_,) + (0,) * len(wshape))
    return _call(
        body, name=name,
        grid_spec=pltpu.PrefetchScalarGridSpec(num_scalar_prefetch=1, grid=(nl,),
                                               in_specs=[own_spec] * nl + [land_spec] * nl + [wspec] * 3,
                                               out_specs=(wspec,) * 4),
        out_shape=tuple(jax.ShapeDtypeStruct(w.shape, F32) for _ in range(4)),
        compiler_params=_cparams(1),
    )(me.reshape(1), *fulls, *lands, w, m, v)


def _constants():
    lane = np.arange(LANES)
    bd = (lane[:, None] // HEAD_DIM == lane[None, :] // HEAD_DIM).astype(np.float32)
    half = ROT_DIM // 2
    inv_freq = ROPE_THETA ** (-jnp.arange(half, dtype=F32) * 2.0 / ROT_DIM)
    invf = jnp.tile(inv_freq, LANES // half).reshape(1, LANES)
    return jnp.asarray(bd, BF16), invf


def kernel(x, c, positions, ada_w, ada_b, norm_g, attn_w_in, attn_q_norm, attn_k_norm, attn_sinks, attn_w_out, pool_w_in, pool_w_group, pool_scale, pool_w_out, loss_target, m_ada_w, m_ada_b, m_norm_g, m_attn_w_in, m_attn_q_norm, m_attn_k_norm, m_attn_sinks, m_attn_w_out, m_pool_w_in, m_pool_w_group, m_pool_scale, m_pool_w_out, v_ada_w, v_ada_b, v_norm_g, v_attn_w_in, v_attn_q_norm, v_attn_k_norm, v_attn_sinks, v_attn_w_out, v_pool_w_in, v_pool_w_group, v_pool_scale, v_pool_w_out):
    seq = x.shape[1]
    me = 4 * lax.axis_index("x") + 2 * lax.axis_index("y") + lax.axis_index("c")
    bd, invf = _constants()
    t_mm = min(512, seq)
    rope = _rope_table(positions.reshape(seq, 1), invf, t_mm)
    t_bw = min(256, seq)
    shard = pool_scale.shape[1]
    cols = ada_w.shape[2]

    w_first, = _prep_weights(me, [(attn_w_in, 0, "T")], "prep_first")
    first_w, token = _gather_first_start(w_first, c)
    prepped = _prep_weights(me, [(attn_w_out, 0, "N"), (pool_w_in, 0, "T"), (pool_w_out, 0, "N"), (pool_w_group, 0, "G"),
                                 (attn_w_in, 1, "T"), (attn_w_out, 1, "N"), (pool_w_in, 1, "T"), (pool_w_out, 1, "N"),
                                 (pool_w_group, 1, "G")], "prep_rest")

    first = jnp.concatenate([c, jnp.pad(pool_scale, ((0, 0), (0, D - shard))), jnp.zeros((5, D), F32)], axis=0)
    first = _allgather_small(first + token[0, 0], "allgather_c", rope)
    c_all = first[:, 0, :]
    scale_full = jnp.transpose(first[:, 1:3, :shard], (1, 0, 2)).reshape(2, D)
    mod_part = _ada_forward(c_all, ada_w)
    mod_all = _allgather_small(mod_part.reshape(DEPTH * N_DEV, cols), "allgather_mod", prepped[0])
    mod_all = mod_all.reshape(N_DEV, DEPTH, N_DEV, cols)
    mine = lax.dynamic_index_in_dim(mod_all, me, axis=2, keepdims=False)
    mod = jnp.transpose(mine, (1, 0, 2)).reshape(DEPTH, 3 * D) + ada_b
    pool_rows = jnp.stack([jnp.zeros_like(scale_full[0]), scale_full[0], jnp.zeros_like(scale_full[0]), scale_full[1]])
    mod = jnp.concatenate([mod.reshape(DEPTH, 3, D), norm_g[:, None, :], pool_rows[:, None, :],
                           jnp.zeros((DEPTH, 3, D), F32)], axis=1)
    rows = mod.reshape(DEPTH, 8, 1, D)

    groups = [prepped[0:1], prepped[1:4], prepped[4:6], prepped[6:9]]
    gaxes = [(0,), (0,), (0, 0, 1), (0, 0), (0, 0, 1)]
    first_w, token = _gather_first_forward(first_w, mod)
    rest, token = _gather_start(groups, gaxes[1:], token, "gather_start_rest")
    started = [None] + rest

    saved, weights = [], []
    h = x[0]
    for i in range(DEPTH):
        j = i // 2
        s = dict(x=h)
        if i == 0:
            w_in_t = _gather_first_wait(first_w, token)
        else:
            wts = _gather_wait(started[i + 1], gaxes[i + 1], h, f"gather_wait_{i}")
        if i % 2 == 0:
            if i > 0:
                w_in_t, w_out = wts
            s["gain"] = jnp.concatenate([jnp.tile(attn_q_norm[j], N_HEADS), jnp.tile(attn_k_norm[j], N_KV)]).reshape(1, QK_W)
            s["qk_raw"], s["qs"], s["kd"], s["vd"], s["g"] = _attn_in_proj(
                h, rope, rows, mod, i, w_in_t, j, s["gain"], bd, t_bw)
            s["o"] = _attn_forward(attn_sinks, s["qs"], s["kd"], s["vd"], j)
            if i == 0:
                w_out, = _gather_wait(started[1], gaxes[1], s["o"], "gather_wait_0_out")
            h, s["br"] = _attn_out_proj(h, s["o"], s["g"], w_out, j, mod, i, t_mm)
            weights.append((w_in_t, w_out))
        else:
            p_in_t, p_out, p_grp = wts
            s["v"], s["g"] = _pool_in_proj(h, rows, mod, i, p_in_t, j, t_mm)
            if i < DEPTH - 1:
                h, s["br"] = _pool_mix_out(h, s["v"], s["g"], p_grp, p_out, j, rows, mod, i, t_mm)
            else:
                dx, s["br"], loss_part = _pool_mix_out(h, s["v"], s["g"], p_grp, p_out, j, rows, mod, i, t_mm,
                                                       loss_target[0])
            weights.append(wts)
        saved.append(s)

    vecs, gates, gains, dsinks, pool_vecs = [None] * DEPTH, [None] * DEPTH, [None] * 2, [None] * 2, [None] * 2
    sent = {}
    token = jnp.zeros((8, LANES), F32)
    for i in reversed(range(DEPTH)):
        j = i // 2
        s = saved[i]
        if i % 2 == 0:
            w_in_t, w_out = weights[i]
            dos, dg, d_w_out, gates[i] = _attn_out_proj_bwd(dx, s["br"], s["o"], s["g"], w_out, j, mod, i, t_mm, token)
            if i == 0:
                sent["0_out"], token = _scatter_start([d_w_out], (0,), "scatter_start_0_out", token)
            dq, dk, dv, dsinks[j] = _attn_backward(attn_sinks, s["qs"], dos, s["kd"], s["vd"], j, token)
            dx, d_in_t, vecs[i], gains[j] = _attn_in_proj_bwd(
                s["x"], dx, rope, s["qk_raw"], dq, dk, dv, dg, rows, mod, i, w_in_t, j, s["gain"], bd, t_bw)
            if i > 0:
                sent[i], token = _scatter_start([d_in_t, d_w_out], (0, 0), f"scatter_start_{i}", token)
        else:
            p_in_t, p_out, p_grp = weights[i]
            dpool, dg, d_p_out, d_p_grp, pool_vecs[j] = _pool_mix_out_bwd(
                dx, s["br"], s["v"], s["g"], p_grp, p_out, j, rows, mod, i, t_mm, token)
            dx, d_in_t, vecs[i] = _pool_in_proj_bwd(s["x"], dx, dpool, dg, rows, mod, i, p_in_t, j, t_bw, token)
            sent[i], token = _scatter_start([d_in_t, d_p_out, d_p_grp], (0, 0, 1), f"scatter_start_{i}", token)

    vec = _build_vec(vecs, gates, pool_vecs, gains, dsinks, loss_part)
    vec_rows = lax.dynamic_update_slice(jnp.zeros((N_DEV * VEC_ROWS, D), F32), vec, (me * VEC_ROWS, 0))
    vec_sent, token = _gather_start([[vec_rows]], [(0,)], loss_part, "vec_gather_start")
    sent["0_in"], token = _scatter_start([d_in_t], (0,), "scatter_start_0_in", token)

    got = {}
    for i in (3, 1):
        fulls, lands = _scatter_wait(sent[i], (0, 0, 1), token, f"scatter_wait_{i}")
        got[i] = dict(zip(("in", "out", "grp"), zip(fulls, lands)))
    pick = lambda ls, kind: ([got[i][kind][0] for i in ls], [got[i][kind][1] for i in ls])
    res = {}
    res["pool_w_in"] = _adamw_shards("adamw_pool_w_in", me, *pick((1, 3), "in"), pool_w_in, m_pool_w_in, v_pool_w_in, True)
    res["pool_w_out"] = _adamw_shards("adamw_pool_w_out", me, *pick((1, 3), "out"), pool_w_out, m_pool_w_out,
                                      v_pool_w_out, False)
    res["pool_w_group"] = _adamw_shards("adamw_pool_w_group", me, *pick((1, 3), "grp"), pool_w_group, m_pool_w_group,
                                        v_pool_w_group, False, axis=1)

    vec_all, = _gather_wait(vec_sent[0], (0,), res["pool_w_group"][0], "vec_gather_wait")
    vec_all = vec_all.reshape(N_DEV, VEC_ROWS, D)
    tot, folded = _sum_devices(vec_all, token)
    loss = tot[24, 0]
    small = dict(
        ada_b=(ada_b, tot[0:12].reshape(DEPTH, 3 * D), m_ada_b, v_ada_b),
        norm_g=(norm_g, tot[12:16], m_norm_g, v_norm_g),
        q_norm=(attn_q_norm, folded[0:2, :HEAD_DIM], m_attn_q_norm, v_attn_q_norm),
        k_norm=(attn_k_norm, folded[2:4, :HEAD_DIM], m_attn_k_norm, v_attn_k_norm),
        sinks=(attn_sinks, tot[20:22, :N_HEADS], m_attn_sinks, v_attn_sinks),
        pool_scale=(pool_scale, lax.dynamic_slice(tot, (22, me * shard), (2, shard)), m_pool_scale, v_pool_scale),
    )
    res.update({k: (a[1],) + upd for (k, a), upd in zip(small.items(), _adamw_small(list(small.values())))})

    dmod_all = vec_all[:, 0:12, :].reshape(N_DEV, DEPTH, 3 * D)
    dmod_mine = lax.dynamic_slice_in_dim(dmod_all, me * cols, cols, axis=2)
    dmod_mine = jnp.pad(jnp.transpose(dmod_mine, (1, 0, 2)), ((0, 0), (0, N_DEV), (0, 0))) + token[0, 0]
    res["ada_w"] = _ada_backward_adamw(jnp.pad(c_all, ((0, N_DEV), (0, 0))), dmod_mine, ada_w, m_ada_w, v_ada_w)

    fulls, lands = _scatter_wait(sent[2], (0, 0), res["ada_w"][0], "scatter_wait_2")
    got[2] = dict(zip(("in", "out"), zip(fulls, lands)))
    got[0] = {}
    for kind in ("out", "in"):
        fulls, lands = _scatter_wait(sent["0_" + kind], (0,), res["ada_w"][0], "scatter_wait_0_" + kind)
        got[0][kind] = (fulls[0], lands[0])
    res["attn_w_out"] = _adamw_shards("adamw_attn_w_out", me, *pick((0, 2), "out"), attn_w_out, m_attn_w_out,
                                      v_attn_w_out, False)
    res["attn_w_in"] = _adamw_shards("adamw_attn_w_in", me, *pick((0, 2), "in"), attn_w_in, m_attn_w_in, v_attn_w_in, True)

    order = ("ada_w", "ada_b", "norm_g", "attn_w_in", "q_norm", "k_norm", "sinks", "attn_w_out", "pool_w_in",
             "pool_w_group", "pool_scale", "pool_w_out")
    return (loss, dx[None], *[res[k][0] for k in order], *[res[k][1] for k in order], *[res[k][2] for k in order],
            *[res[k][3] for k in order])
```

```python
import functools

import numpy as np
import jax
import jax.numpy as jnp
from jax import lax
from jax.experimental import pallas as pl
from jax.experimental.pallas import tpu as pltpu

F32 = jnp.float32
BF16 = jnp.bfloat16
MESH = pl.DeviceIdType.MESH

N_DEV = 8
D = 1024
DEPTH = 4
HEAD_DIM = 64
N_HEADS = 16
N_KV = 4
QK_W = 1280
ATTN_IN = 2560
POOL_IN = 2048
QBLK = 128
KX_W = N_KV * 128
CHUNK = 256
POOL_WINDOWS = (2, 4, 8, 16)
HALO = 16
ROPE_THETA = 500000.0
ROT_DIM = 16
NORM_EPS = 1e-6
ADAM_LR = 0.001
ADAM_B1 = 0.9
ADAM_B2 = 0.999
ADAM_EPS = 1e-08
ADAM_WD = 0.01
ADAM_STEP = 10

LANES = 128
VMEM_LIMIT = 56 * 2**20
VEC_ROWS = 32


def _cparams(n_grid=0, **kw):
    if n_grid:
        kw["dimension_semantics"] = ("arbitrary",) * n_grid
    return pltpu.CompilerParams(vmem_limit_bytes=VMEM_LIMIT, **kw)


def _call(body, **kw):
    return pl.pallas_call(body, **kw)


def _mod_spec(layer):
    return pl.BlockSpec((None, 8, D), lambda *_: (layer, 0, 0), pipeline_mode=pl.Buffered(1))


def _mod_row_spec(layer, row):
    return pl.BlockSpec((None, None, 1, D), lambda *_: (layer, row, 0, 0), pipeline_mode=pl.Buffered(1))


NORM_ROW, POOL_SCALE_ROW = 3, 4


def _const_spec(shape):
    nd = len(shape)
    return pl.BlockSpec(shape, lambda *_: (0,) * nd, pipeline_mode=pl.Buffered(1))


def _dot(a, b):
    return jnp.dot(a, b, preferred_element_type=F32)


def _dot_nt(a, b):
    return lax.dot_general(a, b, (((1,), (1,)), ((), ())), preferred_element_type=F32)


def _dot_tn(a, b):
    return lax.dot_general(a, b, (((0,), (0,)), ((), ())), preferred_element_type=F32)


def _group_mean(x, m):
    return _dot(x.astype(BF16), m) * (1.0 / HEAD_DIM)


def _sigmoid(g):
    return 1.0 / (1.0 + jnp.exp(-g))


def _norm_mod(x, ng, sc, sh):
    r = lax.rsqrt(jnp.mean(x * x, axis=-1, keepdims=True) + NORM_EPS)
    xh = x * r
    h = (xh * ng) * (1.0 + sc) + sh
    return xh, r, h


def _rope_table(pos_col, invf_row, tile):
    seq = pos_col.shape[0]

    def body(pos_ref, invf_ref, out_ref):
        ang = pos_ref[...].astype(F32) * invf_ref[...]
        l64 = lax.broadcasted_iota(jnp.int32, (tile, LANES), 1) & (HEAD_DIM - 1)
        cs, sn = jnp.cos(ang), jnp.sin(ang)
        out_ref[:, 0:LANES] = jnp.where(l64 < ROT_DIM, cs, 1.0)
        out_ref[:, LANES:2 * LANES] = jnp.where(l64 < ROT_DIM // 2, -sn, 0.0)
        out_ref[:, 2 * LANES:3 * LANES] = jnp.where((l64 >= ROT_DIM // 2) & (l64 < ROT_DIM), sn, 0.0)

    return _call(
        body, name="rope_table", grid=(seq // tile,),
        out_shape=jax.ShapeDtypeStruct((seq, 3 * LANES), F32),
        in_specs=[pl.BlockSpec((tile, 1), lambda i: (i, 0)), _const_spec((1, LANES))],
        out_specs=pl.BlockSpec((tile, 3 * LANES), lambda i: (i, 0)),
        compiler_params=_cparams(1),
    )(pos_col, invf_row)


def _rope_tabs(rope_ref):
    return rope_ref[:, 0:LANES], rope_ref[:, LANES:2 * LANES], rope_ref[:, 2 * LANES:3 * LANES]


def _rope(y, tabs):
    cos_t, sin_a, sin_b = tabs
    return y * cos_t + pltpu.roll(y, LANES - ROT_DIM // 2, 1) * sin_a + pltpu.roll(y, ROT_DIM // 2, 1) * sin_b


def _rope_bwd(dy, tabs):
    cos_t, sin_a, sin_b = tabs
    return dy * cos_t + pltpu.roll(dy * sin_a, ROT_DIM // 2, 1) + pltpu.roll(dy * sin_b, LANES - ROT_DIM // 2, 1)


def _low_half(rows):
    return lax.broadcasted_iota(jnp.int32, (rows, LANES), 1) < HEAD_DIM


def _adamw(w, g, m, v):
    m = ADAM_B1 * m + (1.0 - ADAM_B1) * g
    v = ADAM_B2 * v + (1.0 - ADAM_B2) * (g * g)
    m_hat = m / (1.0 - ADAM_B1 ** ADAM_STEP)
    v_hat = v / (1.0 - ADAM_B2 ** ADAM_STEP)
    delta = -ADAM_LR * (m_hat / (jnp.sqrt(v_hat) + ADAM_EPS) + ADAM_WD * w)
    return delta, m, v


def _my_position():
    x, y, c = lax.axis_index("x"), lax.axis_index("y"), lax.axis_index("c")
    return x, y, c, 4 * x + 2 * y + c


def _peers(x, y, c):
    out = []
    for k in range(1, N_DEV):
        px = 1 - x if k & 4 else x
        py = 1 - y if k & 2 else y
        pc = 1 - c if k & 1 else c
        out.append(((px, py, pc), 4 * px + 2 * py + pc))
    return out


def _allgather_small(v, name, after):
    rows, cols = v.shape

    def body(v_ref, after_ref, out_ref, send_sems, recv_sems, local_sem):
        x, y, c, me = _my_position()
        local = pltpu.make_async_copy(v_ref, out_ref.at[me], local_sem)
        local.start()
        sends = []
        for k, (peer, _) in enumerate(_peers(x, y, c)):
            cp = pltpu.make_async_remote_copy(v_ref, out_ref.at[me], send_sems.at[k], recv_sems.at[k],
                                              device_id=peer, device_id_type=MESH)
            cp.start()
            sends.append(cp)
        for k, (peer, idx) in enumerate(_peers(x, y, c)):
            pltpu.make_async_remote_copy(v_ref, out_ref.at[idx], send_sems.at[k], recv_sems.at[k],
                                         device_id=peer, device_id_type=MESH).wait_recv()
        for cp in sends:
            cp.wait_send()
        local.wait()

    return _call(
        body, name=name,
        out_shape=jax.ShapeDtypeStruct((N_DEV, rows, cols), F32),
        in_specs=[pl.BlockSpec(memory_space=pltpu.VMEM), pl.BlockSpec(memory_space=pl.ANY)],
        out_specs=pl.BlockSpec(memory_space=pltpu.VMEM),
        scratch_shapes=[pltpu.SemaphoreType.DMA((N_DEV - 1,)), pltpu.SemaphoreType.DMA((N_DEV - 1,)),
                        pltpu.SemaphoreType.DMA(())],
        compiler_params=_cparams(),
    )(v, after)


def _shard_rows(ref, idx, rows, axis):
    sl = [slice(None)] * len(ref.shape)
    sl[axis] = pl.ds(idx * rows, rows)
    return ref.at[tuple(sl)]


def _own_and_peer_rows(ref, me, idx, axis):
    rows = ref.shape[axis] // N_DEV
    return _shard_rows(ref, me, rows, axis), _shard_rows(ref, idx, rows, axis)


HBM_SPEC = pl.BlockSpec(memory_space=pltpu.HBM)
SEM_SPEC = pl.BlockSpec(memory_space=pltpu.SEMAPHORE)
ANY_SPEC = pl.BlockSpec(memory_space=pl.ANY)
DATAFLOW = pltpu.SideEffectType.DATAFLOW_SIDE_EFFECTING


def _hbm(a):
    return pltpu.with_memory_space_constraint(a, pltpu.HBM)


def _gather_start(layers, axes, after, name):
    flat = [a for arrs in layers for a in arrs]
    flat_axes = [ax for axs in axes for ax in axs]
    n, nl = len(flat), len(layers)

    def body(*refs):
        ins, sems, token = refs[:n], refs[n + 1:n + 1 + 2 * nl], refs[-1]
        x, y, c, me = _my_position()
        a0 = 0
        for li, arrs in enumerate(layers):
            for k, (peer, _) in enumerate(_peers(x, y, c)):
                for a in range(len(arrs)):
                    rows, _ = _own_and_peer_rows(ins[a0 + a], me, me, flat_axes[a0 + a])
                    pltpu.make_async_remote_copy(rows, rows, sems[2 * li].at[k * len(arrs) + a],
                                                 sems[2 * li + 1].at[k * len(arrs) + a],
                                                 device_id=peer, device_id_type=MESH).start()
            a0 += len(arrs)
        token[...] = jnp.zeros_like(token)

    sem_shapes = []
    for arrs in layers:
        sem_shapes += [pltpu.SemaphoreType.DMA(((N_DEV - 1) * len(arrs),))] * 2
    out = _call(
        body, name=name,
        out_shape=(*sem_shapes, *[pltpu.HBM(a.shape, a.dtype) for a in flat], jax.ShapeDtypeStruct((8, LANES), F32)),
        in_specs=[HBM_SPEC] * n + [ANY_SPEC],
        out_specs=(*[SEM_SPEC] * (2 * nl), *[HBM_SPEC] * n, pl.BlockSpec(memory_space=pltpu.VMEM)),
        input_output_aliases={a: 2 * nl + a for a in range(n)},
        compiler_params=_cparams(has_side_effects=DATAFLOW),
    )(*[_hbm(a) for a in flat], after)
    per_layer, a0 = [], 0
    for li, arrs in enumerate(layers):
        per_layer.append((out[2 * li], out[2 * li + 1], list(out[2 * nl + a0:2 * nl + a0 + len(arrs)])))
        a0 += len(arrs)
    return per_layer, out[-1]


def _gather_wait(started, axes, after, name):
    send_sems, recv_sems, arrs = started
    n = len(arrs)

    def body(*refs):
        ins, send_ref, recv_ref = refs[:n], refs[n], refs[n + 1]
        x, y, c, me = _my_position()
        for k, (peer, idx) in enumerate(_peers(x, y, c)):
            for a in range(n):
                own, theirs = _own_and_peer_rows(ins[a], me, idx, axes[a])
                cp = pltpu.make_async_remote_copy(own, theirs, send_ref.at[k * n + a], recv_ref.at[k * n + a],
                                                  device_id=peer, device_id_type=MESH)
                cp.wait_send()
                cp.wait_recv()

    return _call(
        body, name=name,
        out_shape=tuple(pltpu.HBM(a.shape, a.dtype) for a in arrs),
        in_specs=[HBM_SPEC] * n + [SEM_SPEC, SEM_SPEC, ANY_SPEC],
        out_specs=tuple([HBM_SPEC] * n),
        input_output_aliases={a: a for a in range(n)},
        compiler_params=_cparams(has_side_effects=DATAFLOW),
    )(*arrs, send_sems, recv_sems, after)


def _first_relations(x, y, c):
    return [(x, y, 1 - c), (1 - x, y, c), (x, 1 - y, c), (1 - x, 1 - y, c)]


def _gather_first_start(arr, after):
    n_rel = 4

    def body(a_ref, after_ref, send_ref, recv_ref, thru, token):
        x, y, c, me = _my_position()
        rows, _ = _own_and_peer_rows(a_ref, me, me, 0)
        for k, peer in enumerate(_first_relations(x, y, c)):
            pltpu.make_async_remote_copy(rows, rows, send_ref.at[k], recv_ref.at[k], device_id=peer, device_id_type=MESH).start()
        token[...] = jnp.zeros_like(token)

    sem = pltpu.SemaphoreType.DMA((n_rel,))
    out = _call(
        body, name="gather_first_start",
        out_shape=(sem, sem, pltpu.HBM(arr.shape, arr.dtype), jax.ShapeDtypeStruct((8, LANES), F32)),
        in_specs=[HBM_SPEC, ANY_SPEC],
        out_specs=(SEM_SPEC, SEM_SPEC, HBM_SPEC, pl.BlockSpec(memory_space=pltpu.VMEM)),
        input_output_aliases={0: 2},
        compiler_params=_cparams(has_side_effects=DATAFLOW),
    )(_hbm(arr), after)
    return out[:3], out[3]


def _gather_first_forward(started, after):
    send_a, recv_a, arr = started

    def body(a_ref, send_a_ref, recv_a_ref, after_ref, send_b_ref, recv_b_ref, thru, token):
        x, y, c, me = _my_position()
        sibling = (x, y, 1 - c)
        for k, peer in enumerate(_first_relations(x, y, c)):
            own, theirs = _own_and_peer_rows(a_ref, me, 4 * peer[0] + 2 * peer[1] + peer[2], 0)
            cp = pltpu.make_async_remote_copy(own, theirs, send_a_ref.at[k], recv_a_ref.at[k], device_id=peer, device_id_type=MESH)
            cp.wait_send()
            cp.wait_recv()
            if k > 0:
                pltpu.make_async_remote_copy(theirs, theirs, send_b_ref.at[k - 1], recv_b_ref.at[k - 1],
                                             device_id=sibling, device_id_type=MESH).start()
        token[...] = jnp.zeros_like(token)

    sem = pltpu.SemaphoreType.DMA((3,))
    out = _call(
        body, name="gather_first_forward",
        out_shape=(sem, sem, pltpu.HBM(arr.shape, arr.dtype), jax.ShapeDtypeStruct((8, LANES), F32)),
        in_specs=[HBM_SPEC, SEM_SPEC, SEM_SPEC, ANY_SPEC],
        out_specs=(SEM_SPEC, SEM_SPEC, HBM_SPEC, pl.BlockSpec(memory_space=pltpu.VMEM)),
        input_output_aliases={0: 2},
        compiler_params=_cparams(has_side_effects=DATAFLOW),
    )(arr, send_a, recv_a, after)
    return out[:3], out[3]


def _gather_first_wait(forwarded, after):
    send_b, recv_b, arr = forwarded

    def body(a_ref, send_b_ref, recv_b_ref, after_ref, thru):
        x, y, c, me = _my_position()
        sibling = (x, y, 1 - c)
        for k, peer in enumerate(_first_relations(x, y, c)[1:]):
            _, sent = _own_and_peer_rows(a_ref, me, 4 * peer[0] + 2 * peer[1] + peer[2], 0)
            _, got = _own_and_peer_rows(a_ref, me, 4 * peer[0] + 2 * peer[1] + (1 - peer[2]), 0)
            cp = pltpu.make_async_remote_copy(sent, got, send_b_ref.at[k], recv_b_ref.at[k], device_id=sibling, device_id_type=MESH)
            cp.wait_send()
            cp.wait_recv()

    return _call(
        body, name="gather_first_wait",
        out_shape=pltpu.HBM(arr.shape, arr.dtype),
        in_specs=[HBM_SPEC, SEM_SPEC, SEM_SPEC, ANY_SPEC],
        out_specs=HBM_SPEC,
        input_output_aliases={0: 0},
        compiler_params=_cparams(has_side_effects=DATAFLOW),
    )(arr, send_b, recv_b, after)


def _scatter_start(fulls, axes, name, after):
    n = len(fulls)
    lands = []
    for f, ax in zip(fulls, axes):
        shp = list(f.shape)
        shp[ax] //= N_DEV
        lands.append(_hbm(lax.empty((N_DEV - 1,) + tuple(shp), f.dtype)))

    def body(*refs):
        srcs, dsts, send_ref, recv_ref, token = refs[:n], refs[n:2 * n], refs[2 * n + 1], refs[2 * n + 2], refs[-1]
        x, y, c, me = _my_position()
        for k, (peer, idx) in enumerate(_peers(x, y, c)):
            for a in range(n):
                _, theirs = _own_and_peer_rows(srcs[a], me, idx, axes[a])
                pltpu.make_async_remote_copy(theirs, dsts[a].at[k], send_ref.at[k * n + a], recv_ref.at[k * n + a],
                                             device_id=peer, device_id_type=MESH).start()
        token[...] = jnp.zeros_like(token)

    sem = pltpu.SemaphoreType.DMA(((N_DEV - 1) * n,))
    out = _call(
        body, name=name,
        out_shape=(sem, sem, *[pltpu.HBM(a.shape, a.dtype) for a in fulls], *[pltpu.HBM(a.shape, a.dtype) for a in lands],
                   jax.ShapeDtypeStruct((8, LANES), F32)),
        in_specs=[HBM_SPEC] * (2 * n) + [ANY_SPEC],
        out_specs=(SEM_SPEC, SEM_SPEC, *[HBM_SPEC] * (2 * n), pl.BlockSpec(memory_space=pltpu.VMEM)),
        input_output_aliases={a: 2 + a for a in range(2 * n)},
        compiler_params=_cparams(has_side_effects=DATAFLOW),
    )(*[_hbm(a) for a in fulls], *lands, after)
    return (out[0], out[1], list(out[2:2 + n]), list(out[2 + n:2 + 2 * n])), out[-1]


def _scatter_wait(started, axes, after, name):
    send_sems, recv_sems, fulls, lands = started
    n = len(fulls)

    def body(*refs):
        srcs, dsts, send_ref, recv_ref = refs[:n], refs[n:2 * n], refs[2 * n], refs[2 * n + 1]
        x, y, c, me = _my_position()
        for k, (peer, idx) in enumerate(_peers(x, y, c)):
            for a in range(n):
                _, theirs = _own_and_peer_rows(srcs[a], me, idx, axes[a])
                cp = pltpu.make_async_remote_copy(theirs, dsts[a].at[k], send_ref.at[k * n + a], recv_ref.at[k * n + a],
                                                  device_id=peer, device_id_type=MESH)
                cp.wait_send()
                cp.wait_recv()

    out = _call(
        body, name=name,
        out_shape=tuple(pltpu.HBM(a.shape, a.dtype) for a in (*fulls, *lands)),
        in_specs=[HBM_SPEC] * (2 * n) + [SEM_SPEC, SEM_SPEC, ANY_SPEC],
        out_specs=tuple([HBM_SPEC] * (2 * n)),
        input_output_aliases={a: a for a in range(2 * n)},
        compiler_params=_cparams(has_side_effects=DATAFLOW),
    )(*fulls, *lands, send_sems, recv_sems, after)
    return list(out[:n]), list(out[n:])


def _prep_weights(me, items, name):
    def body(me_ref, *refs):
        for (_, _, kind), src, dst in zip(items, refs[:len(items)], refs[len(items):]):
            dst[...] = (src[...].T if kind == "T" else src[...]).astype(BF16)

    ins, in_specs, out_shapes, out_specs = [], [], [], []
    for src, j, kind in items:
        shard = src.shape[1:]
        ins.append(src)
        in_specs.append(pl.BlockSpec((None,) + tuple(shard), lambda i, me_ref, j=j, nd=len(shard): (j,) + (0,) * nd))
        if kind == "G":
            out_shapes.append((shard[0], N_DEV * shard[1], shard[2]))
            out_specs.append(pl.BlockSpec(tuple(shard), lambda i, me_ref: (0, me_ref[0], 0)))
        else:
            rows = shard[1] if kind == "T" else shard[0]
            out_shapes.append((N_DEV * rows, D))
            out_specs.append(pl.BlockSpec((rows, D), lambda i, me_ref: (me_ref[0], 0)))
    out = _call(
        body, name=name,
        grid_spec=pltpu.PrefetchScalarGridSpec(num_scalar_prefetch=1, grid=(1,), in_specs=in_specs, out_specs=tuple(out_specs)),
        out_shape=tuple(jax.ShapeDtypeStruct(s, BF16) for s in out_shapes),
        compiler_params=_cparams(1),
    )(me.reshape(1), *ins)
    return list(out)


def _ada_forward(c_all, ada_w):
    cols = ada_w.shape[2]

    def body(c_ref, w_ref, o_ref):
        cv = c_ref[...]
        sc = (cv * _sigmoid(cv)).astype(BF16)
        o_ref[...] = _dot(sc, w_ref[...].astype(BF16))

    return _call(
        body, name="ada_forward", grid=(DEPTH,),
        out_shape=jax.ShapeDtypeStruct((DEPTH, N_DEV, cols), F32),
        in_specs=[pl.BlockSpec((N_DEV, D), lambda i: (0, 0)), pl.BlockSpec((None, D, cols), lambda i: (i, 0, 0))],
        out_specs=pl.BlockSpec((None, N_DEV, cols), lambda i: (i, 0, 0)),
        compiler_params=_cparams(1),
    )(c_all, ada_w)


def _ada_backward_adamw(c_pad, dmod_pad, w, m, v):
    cols = w.shape[2]

    def body(c_ref, dm_ref, w_ref, m_ref, v_ref, g_out, d_out, m_out, v_out):
        cv = c_ref[...]
        sc = (cv * _sigmoid(cv)).astype(BF16)
        g = _dot_tn(sc, dm_ref[...].astype(BF16))
        g_out[...] = g
        d_out[...], m_out[...], v_out[...] = _adamw(w_ref[...], g, m_ref[...], v_ref[...])

    wspec = pl.BlockSpec((None, D, cols), lambda i: (i, 0, 0))
    return _call(
        body, name="ada_backward_adamw", grid=(DEPTH,),
        out_shape=tuple(jax.ShapeDtypeStruct(w.shape, F32) for _ in range(4)),
        in_specs=[pl.BlockSpec((2 * N_DEV, D), lambda i: (0, 0)), pl.BlockSpec((None, 2 * N_DEV, cols), lambda i: (i, 0, 0)),
                  wspec, wspec, wspec],
        out_specs=(wspec, wspec, wspec, wspec),
        compiler_params=_cparams(1),
    )(c_pad, dmod_pad, w, m, v)


def _attn_in_proj(x, rope, rows, mod, layer, w_t, j, gain, bd, tile):
    seq = x.shape[0]

    def body(x_ref, rope_ref, ng_ref, mod_ref, w_ref, gain_ref, bd_ref, qk_ref, qs_ref, kd_ref, vd_ref, g_ref):
        _, _, h = _norm_mod(x_ref[...], ng_ref[...], mod_ref[1:2, :], mod_ref[0:1, :])
        hb = h.astype(BF16)
        tabs = _rope_tabs(rope_ref)
        low = _low_half(tile)
        bdm = bd_ref[...]

        def put_kv(ref, blk, first_kv):
            sw = pltpu.roll(blk, HEAD_DIM, 1)
            ref[:, LANES * first_kv:LANES * (first_kv + 1)] = jnp.where(low, blk, sw).astype(BF16)
            ref[:, LANES * (first_kv + 1):LANES * (first_kv + 2)] = jnp.where(low, sw, blk).astype(BF16)

        def project(c):
            return _dot_nt(hb, w_ref[CHUNK * c:CHUNK * (c + 1), :])

        n_chunks = ATTN_IN // CHUNK
        per = CHUNK // LANES
        nxt = project(0)
        for c in range(n_chunks):
            cur = nxt
            if c + 1 < n_chunks:
                nxt = project(c + 1)
            col = CHUNK * c
            if col >= QK_W + N_KV * HEAD_DIM:
                g_ref[:, col - QK_W - N_KV * HEAD_DIM:col - QK_W - N_KV * HEAD_DIM + CHUNK] = cur.astype(BF16)
            elif col >= QK_W:
                for t in range(per):
                    put_kv(vd_ref, cur[:, LANES * t:LANES * (t + 1)], (col - QK_W) // HEAD_DIM + 2 * t)
            else:
                qk_ref[:, col:col + CHUNK] = cur
                for t in range(per):
                    b = per * c + t
                    blk = cur[:, LANES * t:LANES * (t + 1)]
                    ms = _group_mean(blk * blk, bdm)
                    y = (blk * lax.rsqrt(ms + NORM_EPS)) * gain_ref[:, LANES * b:LANES * (b + 1)]
                    rp = _rope(y, tabs)
                    if b < D // LANES:
                        rp = rp * (HEAD_DIM ** -0.5)
                        qs_ref[:, 2 * LANES * b:2 * LANES * b + LANES] = jnp.where(low, rp, 0.0).astype(BF16)
                        qs_ref[:, 2 * LANES * b + LANES:2 * LANES * (b + 1)] = jnp.where(low, 0.0, rp).astype(BF16)
                    else:
                        put_kv(kd_ref, rp, 2 * (b - D // LANES))

    row = lambda w: pl.BlockSpec((tile, w), lambda i: (i, 0))
    return _call(
        body, name=f"attn_in_proj_{j}", grid=(seq // tile,),
        out_shape=(jax.ShapeDtypeStruct((seq, QK_W), F32), jax.ShapeDtypeStruct((seq, N_HEADS * LANES), BF16),
                   jax.ShapeDtypeStruct((seq, KX_W), BF16), jax.ShapeDtypeStruct((seq, KX_W), BF16),
                   jax.ShapeDtypeStruct((seq, D), BF16)),
        in_specs=[row(D), row(3 * LANES), _mod_row_spec(layer, NORM_ROW), _mod_spec(layer), _const_spec((ATTN_IN, D)),
                  _const_spec((1, QK_W)), _const_spec((LANES, LANES))],
        out_specs=(row(QK_W), row(N_HEADS * LANES), row(KX_W), row(KX_W), row(D)),
        compiler_params=_cparams(1),
    )(x, rope, rows, mod, w_t, gain, bd)


def _band_mask(n, rows, keys_on_rows):
    shape = (2 * QBLK, rows) if keys_on_rows else (rows, 2 * QBLK)
    qi = lax.broadcasted_iota(jnp.int32, shape, 1 if keys_on_rows else 0) & (QBLK - 1)
    kj = lax.broadcasted_iota(jnp.int32, shape, 0 if keys_on_rows else 1)
    diff = QBLK + qi - kj
    first_key = jnp.where(n > 0, 0, QBLK)
    return (diff >= 0) & (diff < QBLK) & (kj >= first_key)


def _stack_heads(ref, heads):
    return jnp.concatenate([ref[:, LANES * h:LANES * (h + 1)] for h in heads], axis=0)


def _kv_block(prev_ref, cur_ref, kv):
    cols = slice(LANES * kv, LANES * (kv + 1))
    return jnp.concatenate([prev_ref[:, cols], cur_ref[:, cols]], axis=0)


def _pair_up(st, low):
    return jnp.concatenate([jnp.where(low, st[0:QBLK], st[QBLK:2 * QBLK]),
                            jnp.where(low, st[2 * QBLK:3 * QBLK], st[3 * QBLK:4 * QBLK])], axis=1)


def _attn_forward(sinks, qs, kd, vd, j):
    seq = qs.shape[0]
    nb = seq // QBLK

    def body(sink_ref, q_ref, kp_ref, kc_ref, vp_ref, vc_ref, o_ref):
        n = pl.program_id(0)
        ok = _band_mask(n, 4 * QBLK, False)
        low = _low_half(QBLK)
        rowi = lax.broadcasted_iota(jnp.int32, (4 * QBLK, 1), 0)

        def scores(kv):
            return _dot_nt(_stack_heads(q_ref, range(4 * kv, 4 * kv + 4)), _kv_block(kp_ref, kc_ref, kv))

        nxt = scores(0)
        for kv in range(N_KV):
            s = jnp.where(ok, nxt, -1e30)
            if kv + 1 < N_KV:
                nxt = scores(kv + 1)
            sink = jnp.where(rowi < QBLK, sink_ref[j, 4 * kv],
                             jnp.where(rowi < 2 * QBLK, sink_ref[j, 4 * kv + 1],
                                       jnp.where(rowi < 3 * QBLK, sink_ref[j, 4 * kv + 2], sink_ref[j, 4 * kv + 3])))
            m = jnp.maximum(jnp.max(s, axis=1, keepdims=True), sink)
            p = jnp.exp(s - m)
            den = jnp.sum(p, axis=1, keepdims=True) + jnp.exp(sink - m)
            o_st = _dot((p / den).astype(BF16), _kv_block(vp_ref, vc_ref, kv))
            o_ref[:, 2 * LANES * kv:2 * LANES * (kv + 1)] = _pair_up(o_st, low).astype(BF16)

    blk = lambda w: pl.BlockSpec((QBLK, w), lambda n: (n, 0))
    prev = lambda w: pl.BlockSpec((QBLK, w), lambda n: (jnp.maximum(n - 1, 0), 0))
    return _call(
        body, name=f"attn_forward_{j}", grid=(nb,),
        out_shape=jax.ShapeDtypeStruct((seq, D), BF16),
        in_specs=[pl.BlockSpec(memory_space=pltpu.SMEM), blk(N_HEADS * LANES), prev(KX_W), blk(KX_W), prev(KX_W), blk(KX_W)],
        out_specs=blk(D),
        compiler_params=_cparams(1),
    )(sinks, qs, kd, kd, vd, vd)


def _attn_out_proj(x, o, g, w, j, mod, layer, tile):
    seq = x.shape[0]

    def body(x_ref, o_ref, g_ref, w_ref, mod_ref, xo_ref, br_ref):
        gv = g_ref[...].astype(F32)
        u = (o_ref[...].astype(F32) * (gv * _sigmoid(gv))).astype(BF16)
        br = _dot(u, w_ref[...])
        br_ref[...] = br.astype(BF16)
        xo_ref[...] = x_ref[...] + mod_ref[2:3, :] * br

    row = pl.BlockSpec((tile, D), lambda i: (i, 0))
    return _call(
        body, name=f"attn_out_proj_{j}", grid=(seq // tile,),
        out_shape=(jax.ShapeDtypeStruct((seq, D), F32), jax.ShapeDtypeStruct((seq, D), BF16)),
        in_specs=[row, row, row, _const_spec((D, D)), _mod_spec(layer)],
        out_specs=(row, row),
        compiler_params=_cparams(1),
    )(x, o, g, w, mod)


def _attn_out_proj_bwd(dxn, br, o, g, w, j, mod, layer, tile, after):
    seq = dxn.shape[0]
    steps = seq // tile

    def body(dxn_ref, br_ref, o_ref, g_ref, w_ref, mod_ref, after_ref, do_ref, dg_ref, dw_ref, dgate_ref, dw_acc):
        i = pl.program_id(0)

        @pl.when(i == 0)
        def _():
            dw_acc[...] = jnp.zeros_like(dw_acc)
            dgate_ref[...] = jnp.zeros_like(dgate_ref)

        dxn_v, ov, gv = dxn_ref[...], o_ref[...].astype(F32), g_ref[...].astype(F32)
        dgate_ref[...] += jnp.sum(dxn_v * br_ref[...].astype(F32), axis=0, keepdims=True)
        dbr = (dxn_v * mod_ref[2:3, :]).astype(BF16)
        du = _dot_nt(dbr, w_ref[...])
        sg = _sigmoid(gv)
        sl = gv * sg
        dw_acc[...] += _dot_tn((ov * sl).astype(BF16), dbr)
        do = du * sl
        dg_ref[...] = (du * ov * (sg * (1.0 + gv * (1.0 - sg)))).astype(BF16)
        low = _low_half(tile)
        for b in range(D // LANES):
            blk = do[:, LANES * b:LANES * (b + 1)]
            do_ref[:, 2 * LANES * b:2 * LANES * b + LANES] = jnp.where(low, blk, 0.0).astype(BF16)
            do_ref[:, 2 * LANES * b + LANES:2 * LANES * (b + 1)] = jnp.where(low, 0.0, blk).astype(BF16)

        @pl.when(i == steps - 1)
        def _():
            dw_ref[...] = dw_acc[...].astype(BF16)

    row = lambda w_: pl.BlockSpec((tile, w_), lambda i: (i, 0))
    return _call(
        body, name=f"attn_out_proj_bwd_{j}", grid=(steps,),
        out_shape=(jax.ShapeDtypeStruct((seq, N_HEADS * LANES), BF16), jax.ShapeDtypeStruct((seq, D), BF16),
                   jax.ShapeDtypeStruct((D, D), BF16), jax.ShapeDtypeStruct((1, D), F32)),
        in_specs=[row(D), row(D), row(D), row(D), _const_spec((D, D)), _mod_spec(layer), ANY_SPEC],
        out_specs=(row(N_HEADS * LANES), row(D), pl.BlockSpec((D, D), lambda i: (0, 0)),
                   pl.BlockSpec((1, D), lambda i: (0, 0))),
        scratch_shapes=[pltpu.VMEM((D, D), F32)],
        compiler_params=_cparams(1),
    )(dxn, br, o, g, w, mod, after)


def _attn_backward(sinks, qs, dos, kd, vd, j, after):
    seq = qs.shape[0]
    nb = seq // QBLK

    def body(sink_ref, q_ref, do_ref, kp_ref, kc_ref, vp_ref, vc_ref, after_ref, dq_ref, dk_ref, dv_ref, dsink_ref,
             carry_k, carry_v, sink_acc):
        n = pl.program_id(0)

        @pl.when(n == 0)
        def _():
            carry_k[...] = jnp.zeros_like(carry_k)
            carry_v[...] = jnp.zeros_like(carry_v)
            sink_acc[...] = jnp.zeros_like(sink_acc)

        @pl.when(n < nb)
        def _():
            ok = _band_mask(n, 2 * QBLK, True)
            low = _low_half(QBLK)
            lane_q = lax.broadcasted_iota(jnp.int32, (1, 2 * QBLK), 1)
            dk_parts, dv_parts = [], []

            def first_products(g):
                kv, half = divmod(g, 2)
                heads = (4 * kv + half, 4 * kv + 2 + half)
                q = _stack_heads(q_ref, heads)
                do = _stack_heads(do_ref, heads)
                kk = _kv_block(kp_ref, kc_ref, kv)
                return heads, q, do, kk, _dot_nt(kk, q), _dot_nt(_kv_block(vp_ref, vc_ref, kv), do)

            nxt = first_products(0)
            dq_h, dk_kv, dv_kv = [], None, None
            for g in range(2 * N_KV):
                heads, q, do, kk, s_raw, dp_raw = nxt
                if g + 1 < 2 * N_KV:
                    nxt = first_products(g + 1)
                st = jnp.where(ok, s_raw, -1e30)
                sink = jnp.where(lane_q < QBLK, sink_ref[j, heads[0]], sink_ref[j, heads[1]])
                m = jnp.maximum(jnp.max(st, axis=0, keepdims=True), sink)
                e = jnp.exp(st - m)
                e_sink = jnp.exp(sink - m)
                inv = 1.0 / (jnp.sum(e, axis=0, keepdims=True) + e_sink)
                p = e * inv
                pdp = p * dp_raw
                delta = jnp.sum(pdp, axis=0, keepdims=True)
                ds = (pdp - p * delta).astype(BF16)
                sink_acc[g:g + 1, :] -= e_sink * inv * delta
                dk_g, dv_g = _dot(ds, q), _dot(p.astype(BF16), do)
                dk_kv = dk_g if dk_kv is None else dk_kv + dk_g
                dv_kv = dv_g if dv_kv is None else dv_kv + dv_g
                dq_h.append(_dot_tn(ds, kk))
                if g % 2 == 1:
                    kv = g // 2
                    for t in range(2):
                        dq_ref[:, LANES * (2 * kv + t):LANES * (2 * kv + t + 1)] = jnp.where(
                            low, dq_h[0][QBLK * t:QBLK * (t + 1)], dq_h[1][QBLK * t:QBLK * (t + 1)])
                    dk_parts.append(dk_kv + pltpu.roll(dk_kv, HEAD_DIM, 1))
                    dv_parts.append(dv_kv + pltpu.roll(dv_kv, HEAD_DIM, 1))
                    dq_h, dk_kv, dv_kv = [], None, None

            def order(parts, lo, hi):
                return jnp.concatenate([jnp.where(low, parts[0][lo:hi], parts[1][lo:hi]),
                                        jnp.where(low, parts[2][lo:hi], parts[3][lo:hi])], axis=1)

            dk_ref[...] = carry_k[...] + order(dk_parts, 0, QBLK)
            dv_ref[...] = (carry_v[...] + order(dv_parts, 0, QBLK)).astype(BF16)
            carry_k[...] = order(dk_parts, QBLK, 2 * QBLK)
            carry_v[...] = order(dv_parts, QBLK, 2 * QBLK)

        @pl.when(n == nb)
        def _():
            dk_ref[...] = carry_k[...]
            dv_ref[...] = carry_v[...].astype(BF16)
            lane = lax.broadcasted_iota(jnp.int32, (1, LANES), 1)
            out = jnp.zeros((1, LANES), F32)
            for g in range(2 * N_KV):
                for t in range(2):
                    tot = jnp.sum(sink_acc[g:g + 1, QBLK * t:QBLK * (t + 1)], axis=1, keepdims=True)
                    out = jnp.where(lane == 4 * (g // 2) + 2 * t + g % 2, tot, out)
            dsink_ref[...] = out

    cur = lambda w: pl.BlockSpec((QBLK, w), lambda n: (jnp.minimum(n, nb - 1), 0))
    prev = lambda w: pl.BlockSpec((QBLK, w), lambda n: (jnp.maximum(n - 1, 0), 0))
    return _call(
        body, name=f"attn_backward_{j}", grid=(nb + 1,),
        out_shape=(jax.ShapeDtypeStruct((seq, D), F32), jax.ShapeDtypeStruct((seq, N_KV * HEAD_DIM), F32),
                   jax.ShapeDtypeStruct((seq, N_KV * HEAD_DIM), BF16), jax.ShapeDtypeStruct((1, LANES), F32)),
        in_specs=[pl.BlockSpec(memory_space=pltpu.SMEM), cur(N_HEADS * LANES), cur(N_HEADS * LANES), prev(KX_W), cur(KX_W),
                  prev(KX_W), cur(KX_W), ANY_SPEC],
        out_specs=(cur(D), prev(N_KV * HEAD_DIM), prev(N_KV * HEAD_DIM), pl.BlockSpec((1, LANES), lambda n: (0, 0))),
        scratch_shapes=[pltpu.VMEM((QBLK, N_KV * HEAD_DIM), F32), pltpu.VMEM((QBLK, N_KV * HEAD_DIM), F32),
                        pltpu.VMEM((2 * N_KV, 2 * QBLK), F32)],
        compiler_params=_cparams(1),
    )(sinks, qs, dos, kd, kd, vd, vd, after)


def _in_proj_tail(x_ref, dxn_ref, ng_ref, mod_ref, w_ref, dproj, dx_ref, dw_acc, vec_acc):
    ng, sc, sh = ng_ref[...], mod_ref[1:2, :], mod_ref[0:1, :]
    xh, r, h = _norm_mod(x_ref[...], ng, sc, sh)
    dh = _dot(dproj, w_ref[...])
    dw_acc[...] += _dot_tn(dproj, h.astype(BF16))
    vec_acc[0:1, :] += jnp.sum(dh, axis=0, keepdims=True)
    vec_acc[1:2, :] += jnp.sum(dh * xh, axis=0, keepdims=True)
    dxh = dh * (ng * (1.0 + sc))
    dx_ref[...] = dxn_ref[...] + r * (dxh - xh * jnp.mean(dxh * xh, axis=-1, keepdims=True))


def _tail_finish(ng_ref, mod_ref, dw_ref, vec_ref, dw_acc, vec_acc):
    dw_ref[...] = dw_acc[...].astype(BF16)
    a = vec_acc[1:2, :]
    vec_ref[...] = jnp.zeros_like(vec_ref)
    vec_ref[0:1, :] = vec_acc[0:1, :]
    vec_ref[1:2, :] = a * ng_ref[...]
    vec_ref[3:4, :] = a * (1.0 + mod_ref[1:2, :])


def _attn_in_proj_bwd(x, dxn, rope, qk_raw, dq, dk, dv, dg, rows, mod, layer, w_t, j, gain, bd, tile):
    seq = x.shape[0]
    steps = seq // tile

    def body(x_ref, dxn_ref, rope_ref, qk_ref, dq_ref, dk_ref, dv_ref, dg_ref, ng_ref, mod_ref, w_ref, gain_ref,
             bd_ref, dx_ref, dw_ref, vec_ref, dgain_ref, dproj, dw_acc, vec_acc):
        i = pl.program_id(0)

        @pl.when(i == 0)
        def _():
            dw_acc[...] = jnp.zeros_like(dw_acc)
            vec_acc[...] = jnp.zeros_like(vec_acc)
            dgain_ref[...] = jnp.zeros_like(dgain_ref)

        tabs = _rope_tabs(rope_ref)
        bdm = bd_ref[...]
        for b in range(QK_W // LANES):
            cols = slice(LANES * b, LANES * (b + 1))
            raw = qk_ref[:, cols]
            if b < D // LANES:
                dy = dq_ref[:, cols] * (HEAD_DIM ** -0.5)
            else:
                dy = dk_ref[:, LANES * (b - D // LANES):LANES * (b + 1 - D // LANES)]
            dy = _rope_bwd(dy, tabs)
            rr = lax.rsqrt(_group_mean(raw * raw, bdm) + NORM_EPS)
            xh = raw * rr
            dgain_ref[:, cols] += jnp.sum(dy * xh, axis=0, keepdims=True)
            dxh = dy * gain_ref[:, cols]
            dproj[:, cols] = (rr * (dxh - xh * _group_mean(dxh * xh, bdm))).astype(BF16)
        dproj[:, QK_W:QK_W + N_KV * HEAD_DIM] = dv_ref[...]
        dproj[:, QK_W + N_KV * HEAD_DIM:] = dg_ref[...]
        _in_proj_tail(x_ref, dxn_ref, ng_ref, mod_ref, w_ref, dproj[...], dx_ref, dw_acc, vec_acc)

        @pl.when(i == steps - 1)
        def _():
            _tail_finish(ng_ref, mod_ref, dw_ref, vec_ref, dw_acc, vec_acc)

    row = lambda w, dt=None: pl.BlockSpec((tile, w), lambda i: (i, 0))
    fixed = lambda shape: pl.BlockSpec(shape, lambda i: (0,) * len(shape))
    return _call(
        body, name=f"attn_in_proj_bwd_{j}", grid=(steps,),
        out_shape=(jax.ShapeDtypeStruct((seq, D), F32), jax.ShapeDtypeStruct((ATTN_IN, D), BF16),
                   jax.ShapeDtypeStruct((8, D), F32), jax.ShapeDtypeStruct((1, QK_W), F32)),
        in_specs=[row(D), row(D), row(3 * LANES), row(QK_W), row(D), row(N_KV * HEAD_DIM), row(N_KV * HEAD_DIM), row(D),
                  _mod_row_spec(layer, NORM_ROW), _mod_spec(layer), _const_spec((ATTN_IN, D)), _const_spec((1, QK_W)),
                  _const_spec((LANES, LANES))],
        out_specs=(row(D), fixed((ATTN_IN, D)), fixed((8, D)), fixed((1, QK_W))),
        scratch_shapes=[pltpu.VMEM((tile, ATTN_IN), BF16), pltpu.VMEM((ATTN_IN, D), F32), pltpu.VMEM((8, D), F32)],
        compiler_params=_cparams(1),
    )(x, dxn, rope, qk_raw, dq, dk, dv, dg, rows, mod, w_t, gain, bd)


def _pool_in_proj(x, rows, mod, layer, w_t, j, tile):
    seq = x.shape[0]

    def body(x_ref, ng_ref, mod_ref, w_ref, v_ref, g_ref):
        _, _, h = _norm_mod(x_ref[...], ng_ref[...], mod_ref[1:2, :], mod_ref[0:1, :])
        proj = _dot_nt(h.astype(BF16), w_ref[...])
        v_ref[...] = proj[:, :D].astype(BF16)
        g_ref[...] = proj[:, D:].astype(BF16)

    row = pl.BlockSpec((tile, D), lambda i: (i, 0))
    return _call(
        body, name=f"pool_in_proj_{j}", grid=(seq // tile,),
        out_shape=(jax.ShapeDtypeStruct((seq, D), BF16), jax.ShapeDtypeStruct((seq, D), BF16)),
        in_specs=[row, _mod_row_spec(layer, NORM_ROW), _mod_spec(layer), _const_spec((POOL_IN, D))],
        out_specs=(row, row),
        compiler_params=_cparams(1),
    )(x, rows, mod, w_t)


PAD = 8


def _window_sums(ext, lo, hi, forward):
    gw = D // len(POOL_WINDOWS)
    planes = []
    for gi, w in enumerate(POOL_WINDOWS):
        cols = slice(gw * gi, gw * (gi + 1))
        src, k = 0, 1
        while k < w:
            d = k if forward else -k
            ext[1 - src, lo:hi, cols] = ext[src, lo:hi, cols] + ext[src, lo + d:hi + d, cols]
            src, k = 1 - src, 2 * k
        planes.append(src)
    return planes


def _pooled(ext, v_ref, first, tile):
    t_abs = first + lax.broadcasted_iota(jnp.int32, (tile, 1), 0)
    top = PAD + HALO
    planes = _window_sums(ext, PAD, top + tile, False)
    outs = []
    gw = D // len(POOL_WINDOWS)
    for gi, w in enumerate(POOL_WINDOWS):
        cols = slice(gw * gi, gw * (gi + 1))
        cnt = jnp.minimum(t_abs + 1, w).astype(F32)
        outs.append(ext[planes[gi], top:top + tile, cols] / cnt - v_ref[:, cols].astype(F32))
    return jnp.concatenate(outs, axis=1)


def _fill_ext(ext, halo_ref, v_ref, i, tile):
    ext[0, 0:PAD, :] = jnp.zeros((PAD, D), F32)
    ext[1, 0:PAD, :] = jnp.zeros((PAD, D), F32)
    ext[0, PAD:PAD + HALO, :] = jnp.where(i == 0, 0.0, halo_ref[...].astype(F32))
    ext[0, PAD + HALO:PAD + HALO + tile, :] = v_ref[...].astype(F32)


def _group_mix(pb, wg_ref):
    gw = D // len(POOL_WINDOWS)
    return jnp.concatenate([_dot(pb[:, gw * gi:gw * (gi + 1)], wg_ref[gi]) for gi in range(len(POOL_WINDOWS))], axis=1)


def _pool_mix_out(x, v, g, wg, w_out, j, rows, mod, layer, tile, target=None):
    seq = x.shape[0]

    def body(*refs):
        if target is None:
            x_ref, v_ref, halo_ref, g_ref, wg_ref, w_ref, scale_ref, mod_ref, xo_ref, br_ref, ext = refs
        else:
            x_ref, v_ref, halo_ref, g_ref, wg_ref, w_ref, scale_ref, mod_ref, t_ref, xo_ref, br_ref, loss_ref, ext = refs
        i = pl.program_id(0)
        _fill_ext(ext, halo_ref, v_ref, i, tile)
        pb = _pooled(ext, v_ref, i * tile, tile).astype(BF16)
        ms = _group_mix(pb, wg_ref) * scale_ref[...]
        gv = g_ref[...].astype(F32)
        u = (ms * (gv * _sigmoid(gv))).astype(BF16)
        br = _dot(u, w_ref[...])
        br_ref[...] = br.astype(BF16)
        y = x_ref[...] + mod_ref[2:3, :] * br
        if target is None:
            xo_ref[...] = y
        else:
            @pl.when(i == 0)
            def _():
                loss_ref[...] = jnp.zeros_like(loss_ref)

            e = y - t_ref[...]
            xo_ref[...] = e * (1.0 / D)
            loss_ref[...] += 0.5 * jnp.sum(jnp.mean(e * e, axis=-1, keepdims=True), axis=0, keepdims=True)

    row = pl.BlockSpec((tile, D), lambda i: (i, 0))
    halo = pl.BlockSpec((HALO, D), lambda i: (jnp.maximum(i * (tile // HALO) - 1, 0), 0))
    extra_in, extra_out, extra_shape = ([], (), ()) if target is None else (
        [row], (pl.BlockSpec((1, LANES), lambda i: (0, 0)),), (jax.ShapeDtypeStruct((1, LANES), F32),))
    return _call(
        body, name=f"pool_mix_out_{j}", grid=(seq // tile,),
        out_shape=(jax.ShapeDtypeStruct((seq, D), F32), jax.ShapeDtypeStruct((seq, D), BF16)) + extra_shape,
        in_specs=[row, row, halo, row, _const_spec(wg.shape), _const_spec((D, D)), _mod_row_spec(layer, POOL_SCALE_ROW),
                  _mod_spec(layer)] + extra_in,
        out_specs=(row, row) + extra_out,
        scratch_shapes=[pltpu.VMEM((2, tile + HALO + PAD, D), F32)],
        compiler_params=_cparams(1),
    )(x, v, v, g, wg, w_out, rows, mod, *(() if target is None else (target,)))


def _pool_mix_out_bwd(dxn, br, v, g, wg, w_out, j, rows, mod, layer, tile, after):
    seq = dxn.shape[0]
    steps = seq // tile
    ng_ = len(POOL_WINDOWS)
    gw = D // ng_

    def body(dxn_ref, br_ref, v_ref, halo_ref, g_ref, wg_ref, w_ref, scale_ref, mod_ref, after_ref,
             dpool_ref, dg_ref, dw_ref, dwg_ref, vec_ref, ext, dw_acc, dwg_acc):
        i = pl.program_id(0)

        @pl.when(i == 0)
        def _():
            dw_acc[...] = jnp.zeros_like(dw_acc)
            dwg_acc[...] = jnp.zeros_like(dwg_acc)
            vec_ref[...] = jnp.zeros_like(vec_ref)

        _fill_ext(ext, halo_ref, v_ref, i, tile)
        pb = _pooled(ext, v_ref, i * tile, tile).astype(BF16)
        mixed = _group_mix(pb, wg_ref)
        scale = scale_ref[...]
        ms = mixed * scale
        gv, dxn_v = g_ref[...].astype(F32), dxn_ref[...]
        sg = _sigmoid(gv)
        sl = gv * sg
        vec_ref[0:1, :] += jnp.sum(dxn_v * br_ref[...].astype(F32), axis=0, keepdims=True)
        dbr = (dxn_v * mod_ref[2:3, :]).astype(BF16)
        du = _dot_nt(dbr, w_ref[...])
        dw_acc[...] += _dot_tn((ms * sl).astype(BF16), dbr)
        dms = du * sl
        dg_ref[...] = (du * ms * (sg * (1.0 + gv * (1.0 - sg)))).astype(BF16)
        vec_ref[1:2, :] += jnp.sum(dms * mixed, axis=0, keepdims=True)
        dmx = (dms * scale).astype(BF16)
        for gi in range(ng_):
            cols = slice(gw * gi, gw * (gi + 1))
            dpool_ref[:, cols] = _dot_nt(dmx[:, cols], wg_ref[gi])
            dwg_acc[gi] += _dot_tn(pb[:, cols], dmx[:, cols])

        @pl.when(i == steps - 1)
        def _():
            dw_ref[...] = dw_acc[...].astype(BF16)
            dwg_ref[...] = dwg_acc[...].astype(BF16)

    row = pl.BlockSpec((tile, D), lambda i: (i, 0))
    halo = pl.BlockSpec((HALO, D), lambda i: (jnp.maximum(i * (tile // HALO) - 1, 0), 0))
    fixed = lambda shape: pl.BlockSpec(shape, lambda i: (0,) * len(shape))
    return _call(
        body, name=f"pool_mix_out_bwd_{j}", grid=(steps,),
        out_shape=(jax.ShapeDtypeStruct((seq, D), F32), jax.ShapeDtypeStruct((seq, D), BF16),
                   jax.ShapeDtypeStruct((D, D), BF16), jax.ShapeDtypeStruct((ng_, gw, gw), BF16),
                   jax.ShapeDtypeStruct((8, D), F32)),
        in_specs=[row, row, row, halo, row, _const_spec(wg.shape), _const_spec((D, D)), _mod_row_spec(layer, POOL_SCALE_ROW),
                  _mod_spec(layer), ANY_SPEC],
        out_specs=(row, row, fixed((D, D)), fixed((ng_, gw, gw)), fixed((8, D))),
        scratch_shapes=[pltpu.VMEM((2, tile + HALO + PAD, D), F32), pltpu.VMEM((D, D), F32), pltpu.VMEM((ng_, gw, gw), F32)],
        compiler_params=_cparams(1),
    )(dxn, br, v, v, g, wg, w_out, rows, mod, after)


def _pool_in_proj_bwd(x, dxn, dpool, dg, rows, mod, layer, w_t, j, tile, after):
    seq = x.shape[0]
    steps = seq // tile
    gw = D // len(POOL_WINDOWS)

    def body(x_ref, dxn_ref, dp_ref, halo_ref, dg_ref, ng_ref, mod_ref, w_ref, after_ref, dx_ref, dw_ref, vec_ref,
             ext, dproj, dw_acc, vec_acc):
        i = pl.program_id(0)

        @pl.when(i == 0)
        def _():
            dw_acc[...] = jnp.zeros_like(dw_acc)
            vec_acc[...] = jnp.zeros_like(vec_acc)

        t_abs = i * tile + lax.broadcasted_iota(jnp.int32, (tile, 1), 0)
        last = i == steps - 1
        ext[0, tile + HALO:tile + HALO + PAD, :] = jnp.zeros((PAD, D), F32)
        ext[1, tile + HALO:tile + HALO + PAD, :] = jnp.zeros((PAD, D), F32)
        for gi, w in enumerate(POOL_WINDOWS):
            cols = slice(gw * gi, gw * (gi + 1))
            cnt = jnp.minimum(t_abs + 1, w).astype(F32)
            ext[0, 0:tile, cols] = dp_ref[:, cols] / cnt
            ext[0, tile:tile + HALO, cols] = jnp.where(last, 0.0, halo_ref[:, cols] * (1.0 / w))
        planes = _window_sums(ext, 0, tile + HALO, True)
        for gi, w in enumerate(POOL_WINDOWS):
            cols = slice(gw * gi, gw * (gi + 1))
            dproj[:, cols] = (ext[planes[gi], 0:tile, cols] - dp_ref[:, cols]).astype(BF16)
        dproj[:, D:] = dg_ref[...]
        _in_proj_tail(x_ref, dxn_ref, ng_ref, mod_ref, w_ref, dproj[...], dx_ref, dw_acc, vec_acc)

        @pl.when(last)
        def _():
            _tail_finish(ng_ref, mod_ref, dw_ref, vec_ref, dw_acc, vec_acc)

    row = pl.BlockSpec((tile, D), lambda i: (i, 0))
    halo = pl.BlockSpec((HALO, D), lambda i: (jnp.minimum((i + 1) * (tile // HALO), seq // HALO - 1), 0))
    fixed = lambda shape: pl.BlockSpec(shape, lambda i: (0,) * len(shape))
    return _call(
        body, name=f"pool_in_proj_bwd_{j}", grid=(steps,),
        out_shape=(jax.ShapeDtypeStruct((seq, D), F32), jax.ShapeDtypeStruct((POOL_IN, D), BF16),
                   jax.ShapeDtypeStruct((8, D), F32)),
        in_specs=[row, row, row, halo, row, _mod_row_spec(layer, NORM_ROW), _mod_spec(layer), _const_spec((POOL_IN, D)), ANY_SPEC],
        out_specs=(row, fixed((POOL_IN, D)), fixed((8, D))),
        scratch_shapes=[pltpu.VMEM((2, tile + HALO + PAD, D), F32), pltpu.VMEM((tile, POOL_IN), BF16), pltpu.VMEM((POOL_IN, D), F32),
                        pltpu.VMEM((8, D), F32)],
        compiler_params=_cparams(1),
    )(x, dxn, dpool, dpool, dg, rows, mod, w_t, after)


def _build_vec(vecs, gates, pool_vecs, gains, dsinks, loss_part):
    def body(v0, v1, v2, v3, g0, g2, p0, p1, n0, n1, s0, s1, loss_ref, out):
        out[...] = jnp.zeros_like(out)
        for i, v in enumerate((v0, v1, v2, v3)):
            out[3 * i:3 * i + 2, :] = v[0:2, :]
            out[12 + i:13 + i, :] = v[3:4, :]
        out[2:3, :] = g0[...]
        out[8:9, :] = g2[...]
        for j, (p, n, s) in enumerate(((p0, n0, s0), (p1, n1, s1))):
            out[3 * (2 * j + 1) + 2:3 * (2 * j + 1) + 3, :] = p[0:1, :]
            out[22 + j:23 + j, :] = p[1:2, :]
            out[16 + j:17 + j, :] = n[:, 0:D]
            out[18 + j:19 + j, 0:QK_W - D] = n[:, D:QK_W]
            out[20 + j:21 + j, 0:LANES] = s[...]
        out[24:25, 0:LANES] = loss_ref[...]

    vm = pl.BlockSpec(memory_space=pltpu.VMEM)
    args = (*vecs, gates[0], gates[2], *pool_vecs, *gains, *dsinks, loss_part)
    return _call(
        body, name="build_vec",
        out_shape=jax.ShapeDtypeStruct((VEC_ROWS, D), F32),
        in_specs=[vm] * len(args), out_specs=vm,
        compiler_params=_cparams(),
    )(*args)


def _sum_devices(g, after):
    rows = g.shape[1]

    def body(g_ref, after_ref, tot_ref, fold_ref):
        tot = g_ref[0]
        for p in range(1, N_DEV):
            tot = tot + g_ref[p]
        tot_ref[...] = tot
        f = tot[16:24, 0:LANES]
        for b in range(1, D // LANES):
            f = f + tot[16:24, LANES * b:LANES * (b + 1)]
        fold_ref[...] = f + pltpu.roll(f, HEAD_DIM, 1)

    return _call(
        body, name="sum_devices",
        out_shape=(jax.ShapeDtypeStruct((rows, D), F32), jax.ShapeDtypeStruct((8, LANES), F32)),
        in_specs=[pl.BlockSpec(memory_space=pltpu.VMEM), ANY_SPEC],
        out_specs=(pl.BlockSpec(memory_space=pltpu.VMEM), pl.BlockSpec(memory_space=pltpu.VMEM)),
        compiler_params=_cparams(),
    )(g, after)


def _adamw_small(params):
    n = len(params)

    def body(*refs):
        ins, outs = refs[:4 * n], refs[4 * n:]
        for p in range(n):
            w_ref, g_ref, m_ref, v_ref = ins[4 * p:4 * p + 4]
            outs[3 * p][...], outs[3 * p + 1][...], outs[3 * p + 2][...] = _adamw(w_ref[...], g_ref[...], m_ref[...], v_ref[...])

    vm = pl.BlockSpec(memory_space=pltpu.VMEM)
    out = _call(
        body, name="adamw_small",
        out_shape=tuple(jax.ShapeDtypeStruct(w.shape, F32) for (w, _, _, _) in params for _ in range(3)),
        in_specs=[vm] * (4 * n), out_specs=tuple([vm] * (3 * n)),
        compiler_params=_cparams(),
    )(*[a for p in params for a in p])
    return [tuple(out[3 * p:3 * p + 3]) for p in range(n)]


def _adamw_shards(name, me, fulls, lands, w, m, v, transpose, axis=0):
    nl = w.shape[0]
    wshape = w.shape[1:]
    own_shape = lands[0].shape[1:]

    def body(me_ref, *refs):
        own_refs, land_refs = refs[:nl], refs[nl:2 * nl]
        w_ref, m_ref, v_ref, g_out, d_out, m_out, v_out = refs[2 * nl:]
        layer = pl.program_id(0)
        for l in range(nl):
            @pl.when(layer == l)
            def _(l=l):
                g = own_refs[l][...].astype(F32)
                for k in range(N_DEV - 1):
                    g = g + land_refs[l][k].astype(F32)
                if transpose:
                    g = g.T
                g_out[...] = g
                d_out[...], m_out[...], v_out[...] = _adamw(w_ref[...], g, m_ref[...], v_ref[...])

    def own_index(l_, me_ref):
        idx = [0] * len(own_shape)
        idx[axis] = me_ref[0]
        return tuple(idx)

    own_spec = pl.BlockSpec(tuple(own_shape), own_index)
    land_spec = pl.BlockSpec((N_DEV - 1,) + tuple(own_shape), lambda l_, me_ref: (0,) * (1 + len(own_shape)))
    wspec = pl.BlockSpec((None,) + tuple(wshape), lambda l_, me_ref: (l_,) + (0,) * len(wshape))
    return _call(
        body, name=name,
        grid_spec=pltpu.PrefetchScalarGridSpec(num_scalar_prefetch=1, grid=(nl,),
                                               in_specs=[own_spec] * nl + [land_spec] * nl + [wspec] * 3,
                                               out_specs=(wspec,) * 4),
        out_shape=tuple(jax.ShapeDtypeStruct(w.shape, F32) for _ in range(4)),
        compiler_params=_cparams(1),
    )(me.reshape(1), *fulls, *lands, w, m, v)


def _constants():
    lane = np.arange(LANES)
    bd = (lane[:, None] // HEAD_DIM == lane[None, :] // HEAD_DIM).astype(np.float32)
    half = ROT_DIM // 2
    inv_freq = ROPE_THETA ** (-jnp.arange(half, dtype=F32) * 2.0 / ROT_DIM)
    invf = jnp.tile(inv_freq, LANES // half).reshape(1, LANES)
    return jnp.asarray(bd, BF16), invf


def kernel(x, c, positions, ada_w, ada_b, norm_g, attn_w_in, attn_q_norm, attn_k_norm, attn_sinks, attn_w_out, pool_w_in, pool_w_group, pool_scale, pool_w_out, loss_target, m_ada_w, m_ada_b, m_norm_g, m_attn_w_in, m_attn_q_norm, m_attn_k_norm, m_attn_sinks, m_attn_w_out, m_pool_w_in, m_pool_w_group, m_pool_scale, m_pool_w_out, v_ada_w, v_ada_b, v_norm_g, v_attn_w_in, v_attn_q_norm, v_attn_k_norm, v_attn_sinks, v_attn_w_out, v_pool_w_in, v_pool_w_group, v_pool_scale, v_pool_w_out):
    seq = x.shape[1]
    me = 4 * lax.axis_index("x") + 2 * lax.axis_index("y") + lax.axis_index("c")
    bd, invf = _constants()
    t_mm = min(512, seq)
    rope = _rope_table(positions.reshape(seq, 1), invf, t_mm)
    t_bw = min(256, seq)
    shard = pool_scale.shape[1]
    cols = ada_w.shape[2]

    w_in_rows = jnp.swapaxes(attn_w_in, 1, 2)
    w_first, = _prep_weights(me, [(w_in_rows, 0, "N")], "prep_first")
    first_w, token = _gather_first_start(w_first, c)
    prepped = _prep_weights(me, [(attn_w_out, 0, "N"), (pool_w_in, 0, "T"), (pool_w_out, 0, "N"), (pool_w_group, 0, "G"),
                                 (w_in_rows, 1, "N"), (attn_w_out, 1, "N"), (pool_w_in, 1, "T"), (pool_w_out, 1, "N"),
                                 (pool_w_group, 1, "G")], "prep_rest")

    first = jnp.concatenate([c, jnp.pad(pool_scale, ((0, 0), (0, D - shard))), jnp.zeros((5, D), F32)], axis=0)
    first = _allgather_small(first + token[0, 0], "allgather_c", rope)
    c_all = first[:, 0, :]
    scale_full = jnp.transpose(first[:, 1:3, :shard], (1, 0, 2)).reshape(2, D)
    mod_part = _ada_forward(c_all, ada_w)
    mod_all = _allgather_small(mod_part.reshape(DEPTH * N_DEV, cols), "allgather_mod", prepped[0])
    mod_all = mod_all.reshape(N_DEV, DEPTH, N_DEV, cols)
    mine = lax.dynamic_index_in_dim(mod_all, me, axis=2, keepdims=False)
    mod = jnp.transpose(mine, (1, 0, 2)).reshape(DEPTH, 3 * D) + ada_b
    pool_rows = jnp.stack([jnp.zeros_like(scale_full[0]), scale_full[0], jnp.zeros_like(scale_full[0]), scale_full[1]])
    mod = jnp.concatenate([mod.reshape(DEPTH, 3, D), norm_g[:, None, :], pool_rows[:, None, :],
                           jnp.zeros((DEPTH, 3, D), F32)], axis=1)
    rows = mod.reshape(DEPTH, 8, 1, D)

    groups = [prepped[0:1], prepped[1:4], prepped[4:6], prepped[6:9]]
    gaxes = [(0,), (0,), (0, 0, 1), (0, 0), (0, 0, 1)]
    first_w, token = _gather_first_forward(first_w, mod)
    rest, token = _gather_start(groups, gaxes[1:], token, "gather_start_rest")
    started = [None] + rest

    saved, weights = [], []
    h = x[0]
    for i in range(DEPTH):
        j = i // 2
        s = dict(x=h)
        if i == 0:
            w_in_t = _gather_first_wait(first_w, token)
        else:
            wts = _gather_wait(started[i + 1], gaxes[i + 1], h, f"gather_wait_{i}")
        if i % 2 == 0:
            if i > 0:
                w_in_t, w_out = wts
            s["gain"] = jnp.concatenate([jnp.tile(attn_q_norm[j], N_HEADS), jnp.tile(attn_k_norm[j], N_KV)]).reshape(1, QK_W)
            s["qk_raw"], s["qs"], s["kd"], s["vd"], s["g"] = _attn_in_proj(
                h, rope, rows, mod, i, w_in_t, j, s["gain"], bd, t_bw)
            s["o"] = _attn_forward(attn_sinks, s["qs"], s["kd"], s["vd"], j)
            if i == 0:
                w_out, = _gather_wait(started[1], gaxes[1], s["o"], "gather_wait_0_out")
            h, s["br"] = _attn_out_proj(h, s["o"], s["g"], w_out, j, mod, i, t_mm)
            weights.append((w_in_t, w_out))
        else:
            p_in_t, p_out, p_grp = wts
            s["v"], s["g"] = _pool_in_proj(h, rows, mod, i, p_in_t, j, t_mm)
            if i < DEPTH - 1:
                h, s["br"] = _pool_mix_out(h, s["v"], s["g"], p_grp, p_out, j, rows, mod, i, t_mm)
            else:
                dx, s["br"], loss_part = _pool_mix_out(h, s["v"], s["g"], p_grp, p_out, j, rows, mod, i, t_mm,
                                                       loss_target[0])
            weights.append(wts)
        saved.append(s)

    vecs, gates, gains, dsinks, pool_vecs = [None] * DEPTH, [None] * DEPTH, [None] * 2, [None] * 2, [None] * 2
    sent = {}
    token = jnp.zeros((8, LANES), F32)
    for i in reversed(range(DEPTH)):
        j = i // 2
        s = saved[i]
        if i % 2 == 0:
            w_in_t, w_out = weights[i]
            dos, dg, d_w_out, gates[i] = _attn_out_proj_bwd(dx, s["br"], s["o"], s["g"], w_out, j, mod, i, t_mm, token)
            if i == 0:
                sent["0_out"], token = _scatter_start([d_w_out], (0,), "scatter_start_0_out", token)
            dq, dk, dv, dsinks[j] = _attn_backward(attn_sinks, s["qs"], dos, s["kd"], s["vd"], j, token)
            dx, d_in_t, vecs[i], gains[j] = _attn_in_proj_bwd(
                s["x"], dx, rope, s["qk_raw"], dq, dk, dv, dg, rows, mod, i, w_in_t, j, s["gain"], bd, t_bw)
            if i > 0:
                sent[i], token = _scatter_start([d_in_t, d_w_out], (0, 0), f"scatter_start_{i}", token)
        else:
            p_in_t, p_out, p_grp = weights[i]
            dpool, dg, d_p_out, d_p_grp, pool_vecs[j] = _pool_mix_out_bwd(
                dx, s["br"], s["v"], s["g"], p_grp, p_out, j, rows, mod, i, t_mm, token)
            dx, d_in_t, vecs[i] = _pool_in_proj_bwd(s["x"], dx, dpool, dg, rows, mod, i, p_in_t, j, t_bw, token)
            sent[i], token = _scatter_start([d_in_t, d_p_out, d_p_grp], (0, 0, 1), f"scatter_start_{i}", token)

    vec = _build_vec(vecs, gates, pool_vecs, gains, dsinks, loss_part)
    vec_rows = lax.dynamic_update_slice(jnp.zeros((N_DEV * VEC_ROWS, D), F32), vec, (me * VEC_ROWS, 0))
    vec_sent, token = _gather_start([[vec_rows]], [(0,)], loss_part, "vec_gather_start")
    sent["0_in"], token = _scatter_start([d_in_t], (0,), "scatter_start_0_in", token)

    got = {}
    for i in (3, 1):
        fulls, lands = _scatter_wait(sent[i], (0, 0, 1), token, f"scatter_wait_{i}")
        got[i] = dict(zip(("in", "out", "grp"), zip(fulls, lands)))
    pick = lambda ls, kind: ([got[i][kind][0] for i in ls], [got[i][kind][1] for i in ls])
    res = {}
    res["pool_w_in"] = _adamw_shards("adamw_pool_w_in", me, *pick((1, 3), "in"), pool_w_in, m_pool_w_in, v_pool_w_in, True)
    res["pool_w_out"] = _adamw_shards("adamw_pool_w_out", me, *pick((1, 3), "out"), pool_w_out, m_pool_w_out,
                                      v_pool_w_out, False)
    res["pool_w_group"] = _adamw_shards("adamw_pool_w_group", me, *pick((1, 3), "grp"), pool_w_group, m_pool_w_group,
                                        v_pool_w_group, False, axis=1)

    vec_all, = _gather_wait(vec_sent[0], (0,), res["pool_w_group"][0], "vec_gather_wait")
    vec_all = vec_all.reshape(N_DEV, VEC_ROWS, D)
    tot, folded = _sum_devices(vec_all, token)
    loss = tot[24, 0]
    small = dict(
        ada_b=(ada_b, tot[0:12].reshape(DEPTH, 3 * D), m_ada_b, v_ada_b),
        norm_g=(norm_g, tot[12:16], m_norm_g, v_norm_g),
        q_norm=(attn_q_norm, folded[0:2, :HEAD_DIM], m_attn_q_norm, v_attn_q_norm),
        k_norm=(attn_k_norm, folded[2:4, :HEAD_DIM], m_attn_k_norm, v_attn_k_norm),
        sinks=(attn_sinks, tot[20:22, :N_HEADS], m_attn_sinks, v_attn_sinks),
        pool_scale=(pool_scale, lax.dynamic_slice(tot, (22, me * shard), (2, shard)), m_pool_scale, v_pool_scale),
    )
    res.update({k: (a[1],) + upd for (k, a), upd in zip(small.items(), _adamw_small(list(small.values())))})

    dmod_all = vec_all[:, 0:12, :].reshape(N_DEV, DEPTH, 3 * D)
    dmod_mine = lax.dynamic_slice_in_dim(dmod_all, me * cols, cols, axis=2)
    dmod_mine = jnp.pad(jnp.transpose(dmod_mine, (1, 0, 2)), ((0, 0), (0, N_DEV), (0, 0))) + token[0, 0]
    res["ada_w"] = _ada_backward_adamw(jnp.pad(c_all, ((0, N_DEV), (0, 0))), dmod_mine, ada_w, m_ada_w, v_ada_w)

    fulls, lands = _scatter_wait(sent[2], (0, 0), res["ada_w"][0], "scatter_wait_2")
    got[2] = dict(zip(("in", "out"), zip(fulls, lands)))
    got[0] = {}
    for kind in ("out", "in"):
        fulls, lands = _scatter_wait(sent["0_" + kind], (0,), res["ada_w"][0], "scatter_wait_0_" + kind)
        got[0][kind] = (fulls[0], lands[0])
    res["attn_w_out"] = _adamw_shards("adamw_attn_w_out", me, *pick((0, 2), "out"), attn_w_out, m_attn_w_out,
                                      v_attn_w_out, False)
    res["attn_w_in"] = tuple(jnp.swapaxes(a, 1, 2) for a in _adamw_shards(
        "adamw_attn_w_in", me, *pick((0, 2), "in"), w_in_rows, jnp.swapaxes(m_attn_w_in, 1, 2),
        jnp.swapaxes(v_attn_w_in, 1, 2), False))

    order = ("ada_w", "ada_b", "norm_g", "attn_w_in", "q_norm", "k_norm", "sinks", "attn_w_out", "pool_w_in",
             "pool_w_group", "pool_scale", "pool_w_out")
    return (loss, dx[None], *[res[k][0] for k in order], *[res[k][1] for k in order], *[res[k][2] for k in order],
            *[res[k][3] for k in order])
```

```python
import functools

import numpy as np
import jax
import jax.numpy as jnp
from jax import lax
from jax.experimental import pallas as pl
from jax.experimental.pallas import tpu as pltpu

F32 = jnp.float32
BF16 = jnp.bfloat16
MESH = pl.DeviceIdType.MESH

N_DEV = 8
D = 1024
DEPTH = 4
HEAD_DIM = 64
N_HEADS = 16
N_KV = 4
QK_W = 1280
ATTN_IN = 2560
POOL_IN = 2048
QBLK = 128
KX_W = N_KV * 128
CHUNK = 256
POOL_WINDOWS = (2, 4, 8, 16)
HALO = 16
ROPE_THETA = 500000.0
ROT_DIM = 16
NORM_EPS = 1e-6
ADAM_LR = 0.001
ADAM_B1 = 0.9
ADAM_B2 = 0.999
ADAM_EPS = 1e-08
ADAM_WD = 0.01
ADAM_STEP = 10

LANES = 128
VMEM_LIMIT = 56 * 2**20
VEC_ROWS = 32


def _cparams(n_grid=0, **kw):
    if n_grid:
        kw["dimension_semantics"] = ("arbitrary",) * n_grid
    return pltpu.CompilerParams(vmem_limit_bytes=VMEM_LIMIT, **kw)


def _call(body, **kw):
    return pl.pallas_call(body, **kw)


def _mod_spec(layer):
    return pl.BlockSpec((None, 8, D), lambda *_: (layer, 0, 0), pipeline_mode=pl.Buffered(1))


def _mod_row_spec(layer, row):
    return pl.BlockSpec((None, None, 1, D), lambda *_: (layer, row, 0, 0), pipeline_mode=pl.Buffered(1))


NORM_ROW, POOL_SCALE_ROW = 3, 4


def _const_spec(shape):
    nd = len(shape)
    return pl.BlockSpec(shape, lambda *_: (0,) * nd, pipeline_mode=pl.Buffered(1))


def _dot(a, b):
    return jnp.dot(a, b, preferred_element_type=F32)


def _dot_nt(a, b):
    return lax.dot_general(a, b, (((1,), (1,)), ((), ())), preferred_element_type=F32)


def _dot_tn(a, b):
    return lax.dot_general(a, b, (((0,), (0,)), ((), ())), preferred_element_type=F32)


def _group_mean(x, m):
    return _dot(x.astype(BF16), m) * (1.0 / HEAD_DIM)


def _sigmoid(g):
    return 1.0 / (1.0 + jnp.exp(-g))


def _norm_mod(x, ng, sc, sh):
    r = lax.rsqrt(jnp.mean(x * x, axis=-1, keepdims=True) + NORM_EPS)
    xh = x * r
    h = (xh * ng) * (1.0 + sc) + sh
    return xh, r, h


def _rope_table(pos_col, invf_row, tile):
    seq = pos_col.shape[0]

    def body(pos_ref, invf_ref, out_ref):
        ang = pos_ref[...].astype(F32) * invf_ref[...]
        l64 = lax.broadcasted_iota(jnp.int32, (tile, LANES), 1) & (HEAD_DIM - 1)
        cs, sn = jnp.cos(ang), jnp.sin(ang)
        out_ref[:, 0:LANES] = jnp.where(l64 < ROT_DIM, cs, 1.0)
        out_ref[:, LANES:2 * LANES] = jnp.where(l64 < ROT_DIM // 2, -sn, 0.0)
        out_ref[:, 2 * LANES:3 * LANES] = jnp.where((l64 >= ROT_DIM // 2) & (l64 < ROT_DIM), sn, 0.0)

    return _call(
        body, name="rope_table", grid=(seq // tile,),
        out_shape=jax.ShapeDtypeStruct((seq, 3 * LANES), F32),
        in_specs=[pl.BlockSpec((tile, 1), lambda i: (i, 0)), _const_spec((1, LANES))],
        out_specs=pl.BlockSpec((tile, 3 * LANES), lambda i: (i, 0)),
        compiler_params=_cparams(1),
    )(pos_col, invf_row)


def _rope_tabs(rope_ref):
    return rope_ref[:, 0:LANES], rope_ref[:, LANES:2 * LANES], rope_ref[:, 2 * LANES:3 * LANES]


def _rope(y, tabs):
    cos_t, sin_a, sin_b = tabs
    return y * cos_t + pltpu.roll(y, LANES - ROT_DIM // 2, 1) * sin_a + pltpu.roll(y, ROT_DIM // 2, 1) * sin_b


def _rope_bwd(dy, tabs):
    cos_t, sin_a, sin_b = tabs
    return dy * cos_t + pltpu.roll(dy * sin_a, ROT_DIM // 2, 1) + pltpu.roll(dy * sin_b, LANES - ROT_DIM // 2, 1)


def _low_half(rows):
    return lax.broadcasted_iota(jnp.int32, (rows, LANES), 1) < HEAD_DIM


def _adamw(w, g, m, v):
    m = ADAM_B1 * m + (1.0 - ADAM_B1) * g
    v = ADAM_B2 * v + (1.0 - ADAM_B2) * (g * g)
    m_hat = m / (1.0 - ADAM_B1 ** ADAM_STEP)
    v_hat = v / (1.0 - ADAM_B2 ** ADAM_STEP)
    delta = -ADAM_LR * (m_hat / (jnp.sqrt(v_hat) + ADAM_EPS) + ADAM_WD * w)
    return delta, m, v


def _my_position():
    x, y, c = lax.axis_index("x"), lax.axis_index("y"), lax.axis_index("c")
    return x, y, c, 4 * x + 2 * y + c


def _peers(x, y, c):
    out = []
    for k in range(1, N_DEV):
        px = 1 - x if k & 4 else x
        py = 1 - y if k & 2 else y
        pc = 1 - c if k & 1 else c
        out.append(((px, py, pc), 4 * px + 2 * py + pc))
    return out


def _allgather_small(v, name, after):
    rows, cols = v.shape

    def body(v_ref, after_ref, out_ref, send_sems, recv_sems, local_sem):
        x, y, c, me = _my_position()
        local = pltpu.make_async_copy(v_ref, out_ref.at[me], local_sem)
        local.start()
        sends = []
        for k, (peer, _) in enumerate(_peers(x, y, c)):
            cp = pltpu.make_async_remote_copy(v_ref, out_ref.at[me], send_sems.at[k], recv_sems.at[k],
                                              device_id=peer, device_id_type=MESH)
            cp.start()
            sends.append(cp)
        for k, (peer, idx) in enumerate(_peers(x, y, c)):
            pltpu.make_async_remote_copy(v_ref, out_ref.at[idx], send_sems.at[k], recv_sems.at[k],
                                         device_id=peer, device_id_type=MESH).wait_recv()
        for cp in sends:
            cp.wait_send()
        local.wait()

    return _call(
        body, name=name,
        out_shape=jax.ShapeDtypeStruct((N_DEV, rows, cols), F32),
        in_specs=[pl.BlockSpec(memory_space=pltpu.VMEM), pl.BlockSpec(memory_space=pl.ANY)],
        out_specs=pl.BlockSpec(memory_space=pltpu.VMEM),
        scratch_shapes=[pltpu.SemaphoreType.DMA((N_DEV - 1,)), pltpu.SemaphoreType.DMA((N_DEV - 1,)),
                        pltpu.SemaphoreType.DMA(())],
        compiler_params=_cparams(),
    )(v, after)


def _shard_rows(ref, idx, rows, axis):
    sl = [slice(None)] * len(ref.shape)
    sl[axis] = pl.ds(idx * rows, rows)
    return ref.at[tuple(sl)]


def _own_and_peer_rows(ref, me, idx, axis):
    rows = ref.shape[axis] // N_DEV
    return _shard_rows(ref, me, rows, axis), _shard_rows(ref, idx, rows, axis)


HBM_SPEC = pl.BlockSpec(memory_space=pltpu.HBM)
SEM_SPEC = pl.BlockSpec(memory_space=pltpu.SEMAPHORE)
ANY_SPEC = pl.BlockSpec(memory_space=pl.ANY)
DATAFLOW = pltpu.SideEffectType.DATAFLOW_SIDE_EFFECTING


def _hbm(a):
    return pltpu.with_memory_space_constraint(a, pltpu.HBM)


def _gather_start(layers, axes, after, name):
    flat = [a for arrs in layers for a in arrs]
    flat_axes = [ax for axs in axes for ax in axs]
    n, nl = len(flat), len(layers)

    def body(*refs):
        ins, sems, token = refs[:n], refs[n + 1:n + 1 + 2 * nl], refs[-1]
        x, y, c, me = _my_position()
        a0 = 0
        for li, arrs in enumerate(layers):
            for k, (peer, _) in enumerate(_peers(x, y, c)):
                for a in range(len(arrs)):
                    rows, _ = _own_and_peer_rows(ins[a0 + a], me, me, flat_axes[a0 + a])
                    pltpu.make_async_remote_copy(rows, rows, sems[2 * li].at[k * len(arrs) + a],
                                                 sems[2 * li + 1].at[k * len(arrs) + a],
                                                 device_id=peer, device_id_type=MESH).start()
            a0 += len(arrs)
        token[...] = jnp.zeros_like(token)

    sem_shapes = []
    for arrs in layers:
        sem_shapes += [pltpu.SemaphoreType.DMA(((N_DEV - 1) * len(arrs),))] * 2
    out = _call(
        body, name=name,
        out_shape=(*sem_shapes, *[pltpu.HBM(a.shape, a.dtype) for a in flat], jax.ShapeDtypeStruct((8, LANES), F32)),
        in_specs=[HBM_SPEC] * n + [ANY_SPEC],
        out_specs=(*[SEM_SPEC] * (2 * nl), *[HBM_SPEC] * n, pl.BlockSpec(memory_space=pltpu.VMEM)),
        input_output_aliases={a: 2 * nl + a for a in range(n)},
        compiler_params=_cparams(has_side_effects=DATAFLOW),
    )(*[_hbm(a) for a in flat], after)
    per_layer, a0 = [], 0
    for li, arrs in enumerate(layers):
        per_layer.append((out[2 * li], out[2 * li + 1], list(out[2 * nl + a0:2 * nl + a0 + len(arrs)])))
        a0 += len(arrs)
    return per_layer, out[-1]


def _gather_wait(started, axes, after, name):
    send_sems, recv_sems, arrs = started
    n = len(arrs)

    def body(*refs):
        ins, send_ref, recv_ref = refs[:n], refs[n], refs[n + 1]
        x, y, c, me = _my_position()
        for k, (peer, idx) in enumerate(_peers(x, y, c)):
            for a in range(n):
                own, theirs = _own_and_peer_rows(ins[a], me, idx, axes[a])
                cp = pltpu.make_async_remote_copy(own, theirs, send_ref.at[k * n + a], recv_ref.at[k * n + a],
                                                  device_id=peer, device_id_type=MESH)
                cp.wait_send()
                cp.wait_recv()

    return _call(
        body, name=name,
        out_shape=tuple(pltpu.HBM(a.shape, a.dtype) for a in arrs),
        in_specs=[HBM_SPEC] * n + [SEM_SPEC, SEM_SPEC, ANY_SPEC],
        out_specs=tuple([HBM_SPEC] * n),
        input_output_aliases={a: a for a in range(n)},
        compiler_params=_cparams(has_side_effects=DATAFLOW),
    )(*arrs, send_sems, recv_sems, after)


def _first_relations(x, y, c):
    return [(x, y, 1 - c), (1 - x, y, c), (x, 1 - y, c), (1 - x, 1 - y, c)]


def _gather_first_start(arr, after):
    n_rel = 4

    def body(a_ref, after_ref, send_ref, recv_ref, thru, token):
        x, y, c, me = _my_position()
        rows, _ = _own_and_peer_rows(a_ref, me, me, 0)
        for k, peer in enumerate(_first_relations(x, y, c)):
            pltpu.make_async_remote_copy(rows, rows, send_ref.at[k], recv_ref.at[k], device_id=peer, device_id_type=MESH).start()
        token[...] = jnp.zeros_like(token)

    sem = pltpu.SemaphoreType.DMA((n_rel,))
    out = _call(
        body, name="gather_first_start",
        out_shape=(sem, sem, pltpu.HBM(arr.shape, arr.dtype), jax.ShapeDtypeStruct((8, LANES), F32)),
        in_specs=[HBM_SPEC, ANY_SPEC],
        out_specs=(SEM_SPEC, SEM_SPEC, HBM_SPEC, pl.BlockSpec(memory_space=pltpu.VMEM)),
        input_output_aliases={0: 2},
        compiler_params=_cparams(has_side_effects=DATAFLOW),
    )(_hbm(arr), after)
    return out[:3], out[3]


def _gather_first_forward(started, after):
    send_a, recv_a, arr = started

    def body(a_ref, send_a_ref, recv_a_ref, after_ref, send_b_ref, recv_b_ref, thru, token):
        x, y, c, me = _my_position()
        sibling = (x, y, 1 - c)
        for k, peer in enumerate(_first_relations(x, y, c)):
            own, theirs = _own_and_peer_rows(a_ref, me, 4 * peer[0] + 2 * peer[1] + peer[2], 0)
            cp = pltpu.make_async_remote_copy(own, theirs, send_a_ref.at[k], recv_a_ref.at[k], device_id=peer, device_id_type=MESH)
            cp.wait_send()
            cp.wait_recv()
            if k > 0:
                pltpu.make_async_remote_copy(theirs, theirs, send_b_ref.at[k - 1], recv_b_ref.at[k - 1],
                                             device_id=sibling, device_id_type=MESH).start()
        token[...] = jnp.zeros_like(token)

    sem = pltpu.SemaphoreType.DMA((3,))
    out = _call(
        body, name="gather_first_forward",
        out_shape=(sem, sem, pltpu.HBM(arr.shape, arr.dtype), jax.ShapeDtypeStruct((8, LANES), F32)),
        in_specs=[HBM_SPEC, SEM_SPEC, SEM_SPEC, ANY_SPEC],
        out_specs=(SEM_SPEC, SEM_SPEC, HBM_SPEC, pl.BlockSpec(memory_space=pltpu.VMEM)),
        input_output_aliases={0: 2},
        compiler_params=_cparams(has_side_effects=DATAFLOW),
    )(arr, send_a, recv_a, after)
    return out[:3], out[3]


def _gather_first_wait(forwarded, after):
    send_b, recv_b, arr = forwarded

    def body(a_ref, send_b_ref, recv_b_ref, after_ref, thru):
        x, y, c, me = _my_position()
        sibling = (x, y, 1 - c)
        for k, peer in enumerate(_first_relations(x, y, c)[1:]):
            _, sent = _own_and_peer_rows(a_ref, me, 4 * peer[0] + 2 * peer[1] + peer[2], 0)
            _, got = _own_and_peer_rows(a_ref, me, 4 * peer[0] + 2 * peer[1] + (1 - peer[2]), 0)
            cp = pltpu.make_async_remote_copy(sent, got, send_b_ref.at[k], recv_b_ref.at[k], device_id=sibling, device_id_type=MESH)
            cp.wait_send()
            cp.wait_recv()

    return _call(
        body, name="gather_first_wait",
        out_shape=pltpu.HBM(arr.shape, arr.dtype),
        in_specs=[HBM_SPEC, SEM_SPEC, SEM_SPEC, ANY_SPEC],
        out_specs=HBM_SPEC,
        input_output_aliases={0: 0},
        compiler_params=_cparams(has_side_effects=DATAFLOW),
    )(arr, send_b, recv_b, after)


def _scatter_start(fulls, axes, name, after):
    n = len(fulls)
    lands = []
    for f, ax in zip(fulls, axes):
        shp = list(f.shape)
        shp[ax] //= N_DEV
        lands.append(_hbm(lax.empty((N_DEV - 1,) + tuple(shp), f.dtype)))

    def body(*refs):
        srcs, dsts, send_ref, recv_ref, token = refs[:n], refs[n:2 * n], refs[2 * n + 1], refs[2 * n + 2], refs[-1]
        x, y, c, me = _my_position()
        for k, (peer, idx) in enumerate(_peers(x, y, c)):
            for a in range(n):
                _, theirs = _own_and_peer_rows(srcs[a], me, idx, axes[a])
                pltpu.make_async_remote_copy(theirs, dsts[a].at[k], send_ref.at[k * n + a], recv_ref.at[k * n + a],
                                             device_id=peer, device_id_type=MESH).start()
        token[...] = jnp.zeros_like(token)

    sem = pltpu.SemaphoreType.DMA(((N_DEV - 1) * n,))
    out = _call(
        body, name=name,
        out_shape=(sem, sem, *[pltpu.HBM(a.shape, a.dtype) for a in fulls], *[pltpu.HBM(a.shape, a.dtype) for a in lands],
                   jax.ShapeDtypeStruct((8, LANES), F32)),
        in_specs=[HBM_SPEC] * (2 * n) + [ANY_SPEC],
        out_specs=(SEM_SPEC, SEM_SPEC, *[HBM_SPEC] * (2 * n), pl.BlockSpec(memory_space=pltpu.VMEM)),
        input_output_aliases={a: 2 + a for a in range(2 * n)},
        compiler_params=_cparams(has_side_effects=DATAFLOW),
    )(*[_hbm(a) for a in fulls], *lands, after)
    return (out[0], out[1], list(out[2:2 + n]), list(out[2 + n:2 + 2 * n])), out[-1]


def _scatter_wait(started, axes, after, name):
    send_sems, recv_sems, fulls, lands = started
    n = len(fulls)

    def body(*refs):
        srcs, dsts, send_ref, recv_ref = refs[:n], refs[n:2 * n], refs[2 * n], refs[2 * n + 1]
        x, y, c, me = _my_position()
        for k, (peer, idx) in enumerate(_peers(x, y, c)):
            for a in range(n):
                _, theirs = _own_and_peer_rows(srcs[a], me, idx, axes[a])
                cp = pltpu.make_async_remote_copy(theirs, dsts[a].at[k], send_ref.at[k * n + a], recv_ref.at[k * n + a],
                                                  device_id=peer, device_id_type=MESH)
                cp.wait_send()
                cp.wait_recv()

    out = _call(
        body, name=name,
        out_shape=tuple(pltpu.HBM(a.shape, a.dtype) for a in (*fulls, *lands)),
        in_specs=[HBM_SPEC] * (2 * n) + [SEM_SPEC, SEM_SPEC, ANY_SPEC],
        out_specs=tuple([HBM_SPEC] * (2 * n)),
        input_output_aliases={a: a for a in range(2 * n)},
        compiler_params=_cparams(has_side_effects=DATAFLOW),
    )(*fulls, *lands, send_sems, recv_sems, after)
    return list(out[:n]), list(out[n:])


def _prep_weights(me, items, name):
    def body(me_ref, *refs):
        for (_, _, kind), src, dst in zip(items, refs[:len(items)], refs[len(items):]):
            dst[...] = (src[...].T if kind == "T" else src[...]).astype(BF16)

    ins, in_specs, out_shapes, out_specs = [], [], [], []
    for src, j, kind in items:
        shard = src.shape[1:]
        ins.append(src)
        in_specs.append(pl.BlockSpec((None,) + tuple(shard), lambda i, me_ref, j=j, nd=len(shard): (j,) + (0,) * nd))
        if kind == "G":
            out_shapes.append((shard[0], N_DEV * shard[1], shard[2]))
            out_specs.append(pl.BlockSpec(tuple(shard), lambda i, me_ref: (0, me_ref[0], 0)))
        else:
            rows = shard[1] if kind == "T" else shard[0]
            out_shapes.append((N_DEV * rows, D))
            out_specs.append(pl.BlockSpec((rows, D), lambda i, me_ref: (me_ref[0], 0)))
    out = _call(
        body, name=name,
        grid_spec=pltpu.PrefetchScalarGridSpec(num_scalar_prefetch=1, grid=(1,), in_specs=in_specs, out_specs=tuple(out_specs)),
        out_shape=tuple(jax.ShapeDtypeStruct(s, BF16) for s in out_shapes),
        compiler_params=_cparams(1),
    )(me.reshape(1), *ins)
    return list(out)


def _ada_forward(c_all, ada_w):
    cols = ada_w.shape[2]

    def body(c_ref, w_ref, o_ref):
        cv = c_ref[...]
        sc = (cv * _sigmoid(cv)).astype(BF16)
        o_ref[...] = _dot(sc, w_ref[...].astype(BF16))

    return _call(
        body, name="ada_forward", grid=(DEPTH,),
        out_shape=jax.ShapeDtypeStruct((DEPTH, N_DEV, cols), F32),
        in_specs=[pl.BlockSpec((N_DEV, D), lambda i: (0, 0)), pl.BlockSpec((None, D, cols), lambda i: (i, 0, 0))],
        out_specs=pl.BlockSpec((None, N_DEV, cols), lambda i: (i, 0, 0)),
        compiler_params=_cparams(1),
    )(c_all, ada_w)


def _ada_backward_adamw(c_pad, dmod_pad, w, m, v):
    cols = w.shape[2]

    def body(c_ref, dm_ref, w_ref, m_ref, v_ref, g_out, d_out, m_out, v_out):
        cv = c_ref[...]
        sc = (cv * _sigmoid(cv)).astype(BF16)
        g = _dot_tn(sc, dm_ref[...].astype(BF16))
        g_out[...] = g
        d_out[...], m_out[...], v_out[...] = _adamw(w_ref[...], g, m_ref[...], v_ref[...])

    wspec = pl.BlockSpec((None, D, cols), lambda i: (i, 0, 0))
    return _call(
        body, name="ada_backward_adamw", grid=(DEPTH,),
        out_shape=tuple(jax.ShapeDtypeStruct(w.shape, F32) for _ in range(4)),
        in_specs=[pl.BlockSpec((2 * N_DEV, D), lambda i: (0, 0)), pl.BlockSpec((None, 2 * N_DEV, cols), lambda i: (i, 0, 0)),
                  wspec, wspec, wspec],
        out_specs=(wspec, wspec, wspec, wspec),
        compiler_params=_cparams(1),
    )(c_pad, dmod_pad, w, m, v)


def _attn_in_proj(x, rope, rows, mod, layer, w_t, j, gain, bd, tile):
    seq = x.shape[0]

    def body(x_ref, rope_ref, ng_ref, mod_ref, w_ref, gain_ref, bd_ref, qk_ref, qs_ref, kd_ref, vd_ref, g_ref):
        _, _, h = _norm_mod(x_ref[...], ng_ref[...], mod_ref[1:2, :], mod_ref[0:1, :])
        hb = h.astype(BF16)
        tabs = _rope_tabs(rope_ref)
        low = _low_half(tile)
        bdm = bd_ref[...]

        def put_kv(ref, blk, first_kv):
            sw = pltpu.roll(blk, HEAD_DIM, 1)
            ref[:, LANES * first_kv:LANES * (first_kv + 1)] = jnp.where(low, blk, sw).astype(BF16)
            ref[:, LANES * (first_kv + 1):LANES * (first_kv + 2)] = jnp.where(low, sw, blk).astype(BF16)

        def project(c):
            return _dot_nt(hb, w_ref[CHUNK * c:CHUNK * (c + 1), :])

        n_chunks = ATTN_IN // CHUNK
        per = CHUNK // LANES
        nxt = project(0)
        for c in range(n_chunks):
            cur = nxt
            if c + 1 < n_chunks:
                nxt = project(c + 1)
            col = CHUNK * c
            if col >= QK_W + N_KV * HEAD_DIM:
                g_ref[:, col - QK_W - N_KV * HEAD_DIM:col - QK_W - N_KV * HEAD_DIM + CHUNK] = cur.astype(BF16)
            elif col >= QK_W:
                for t in range(per):
                    put_kv(vd_ref, cur[:, LANES * t:LANES * (t + 1)], (col - QK_W) // HEAD_DIM + 2 * t)
            else:
                qk_ref[:, col:col + CHUNK] = cur
                for t in range(per):
                    b = per * c + t
                    blk = cur[:, LANES * t:LANES * (t + 1)]
                    ms = _group_mean(blk * blk, bdm)
                    y = (blk * lax.rsqrt(ms + NORM_EPS)) * gain_ref[:, LANES * b:LANES * (b + 1)]
                    rp = _rope(y, tabs)
                    if b < D // LANES:
                        rp = rp * (HEAD_DIM ** -0.5)
                        qs_ref[:, 2 * LANES * b:2 * LANES * b + LANES] = jnp.where(low, rp, 0.0).astype(BF16)
                        qs_ref[:, 2 * LANES * b + LANES:2 * LANES * (b + 1)] = jnp.where(low, 0.0, rp).astype(BF16)
                    else:
                        put_kv(kd_ref, rp, 2 * (b - D // LANES))

    row = lambda w: pl.BlockSpec((tile, w), lambda i: (i, 0))
    return _call(
        body, name=f"attn_in_proj_{j}", grid=(seq // tile,),
        out_shape=(jax.ShapeDtypeStruct((seq, QK_W), F32), jax.ShapeDtypeStruct((seq, N_HEADS * LANES), BF16),
                   jax.ShapeDtypeStruct((seq, KX_W), BF16), jax.ShapeDtypeStruct((seq, KX_W), BF16),
                   jax.ShapeDtypeStruct((seq, D), BF16)),
        in_specs=[row(D), row(3 * LANES), _mod_row_spec(layer, NORM_ROW), _mod_spec(layer), _const_spec((ATTN_IN, D)),
                  _const_spec((1, QK_W)), _const_spec((LANES, LANES))],
        out_specs=(row(QK_W), row(N_HEADS * LANES), row(KX_W), row(KX_W), row(D)),
        compiler_params=_cparams(1),
    )(x, rope, rows, mod, w_t, gain, bd)


def _band_mask(n, rows, keys_on_rows):
    shape = (2 * QBLK, rows) if keys_on_rows else (rows, 2 * QBLK)
    qi = lax.broadcasted_iota(jnp.int32, shape, 1 if keys_on_rows else 0) & (QBLK - 1)
    kj = lax.broadcasted_iota(jnp.int32, shape, 0 if keys_on_rows else 1)
    diff = QBLK + qi - kj
    first_key = jnp.where(n > 0, 0, QBLK)
    return (diff >= 0) & (diff < QBLK) & (kj >= first_key)


def _stack_heads(ref, heads):
    return jnp.concatenate([ref[:, LANES * h:LANES * (h + 1)] for h in heads], axis=0)


def _kv_block(prev_ref, cur_ref, kv):
    cols = slice(LANES * kv, LANES * (kv + 1))
    return jnp.concatenate([prev_ref[:, cols], cur_ref[:, cols]], axis=0)


def _pair_up(st, low):
    return jnp.concatenate([jnp.where(low, st[0:QBLK], st[QBLK:2 * QBLK]),
                            jnp.where(low, st[2 * QBLK:3 * QBLK], st[3 * QBLK:4 * QBLK])], axis=1)


def _attn_forward(sinks, qs, kd, vd, j):
    seq = qs.shape[0]
    nb = seq // QBLK

    def body(sink_ref, q_ref, kp_ref, kc_ref, vp_ref, vc_ref, o_ref):
        n = pl.program_id(0)
        ok = _band_mask(n, 4 * QBLK, False)
        low = _low_half(QBLK)
        rowi = lax.broadcasted_iota(jnp.int32, (4 * QBLK, 1), 0)

        def scores(kv):
            return _dot_nt(_stack_heads(q_ref, range(4 * kv, 4 * kv + 4)), _kv_block(kp_ref, kc_ref, kv))

        nxt = scores(0)
        for kv in range(N_KV):
            s = jnp.where(ok, nxt, -1e30)
            if kv + 1 < N_KV:
                nxt = scores(kv + 1)
            sink = jnp.where(rowi < QBLK, sink_ref[j, 4 * kv],
                             jnp.where(rowi < 2 * QBLK, sink_ref[j, 4 * kv + 1],
                                       jnp.where(rowi < 3 * QBLK, sink_ref[j, 4 * kv + 2], sink_ref[j, 4 * kv + 3])))
            m = jnp.maximum(jnp.max(s, axis=1, keepdims=True), sink)
            p = jnp.exp(s - m)
            den = jnp.sum(p, axis=1, keepdims=True) + jnp.exp(sink - m)
            o_st = _dot((p / den).astype(BF16), _kv_block(vp_ref, vc_ref, kv))
            o_ref[:, 2 * LANES * kv:2 * LANES * (kv + 1)] = _pair_up(o_st, low).astype(BF16)

    blk = lambda w: pl.BlockSpec((QBLK, w), lambda n: (n, 0))
    prev = lambda w: pl.BlockSpec((QBLK, w), lambda n: (jnp.maximum(n - 1, 0), 0))
    return _call(
        body, name=f"attn_forward_{j}", grid=(nb,),
        out_shape=jax.ShapeDtypeStruct((seq, D), BF16),
        in_specs=[pl.BlockSpec(memory_space=pltpu.SMEM), blk(N_HEADS * LANES), prev(KX_W), blk(KX_W), prev(KX_W), blk(KX_W)],
        out_specs=blk(D),
        compiler_params=_cparams(1),
    )(sinks, qs, kd, kd, vd, vd)


def _attn_out_proj(x, o, g, w, j, mod, layer, tile):
    seq = x.shape[0]

    def body(x_ref, o_ref, g_ref, w_ref, mod_ref, xo_ref, br_ref):
        gv = g_ref[...].astype(F32)
        u = (o_ref[...].astype(F32) * (gv * _sigmoid(gv))).astype(BF16)
        br = _dot(u, w_ref[...])
        br_ref[...] = br.astype(BF16)
        xo_ref[...] = x_ref[...] + mod_ref[2:3, :] * br

    row = pl.BlockSpec((tile, D), lambda i: (i, 0))
    return _call(
        body, name=f"attn_out_proj_{j}", grid=(seq // tile,),
        out_shape=(jax.ShapeDtypeStruct((seq, D), F32), jax.ShapeDtypeStruct((seq, D), BF16)),
        in_specs=[row, row, row, _const_spec((D, D)), _mod_spec(layer)],
        out_specs=(row, row),
        compiler_params=_cparams(1),
    )(x, o, g, w, mod)


def _attn_out_proj_bwd(dxn, br, o, g, w, j, mod, layer, tile, after):
    seq = dxn.shape[0]
    steps = seq // tile

    def body(dxn_ref, br_ref, o_ref, g_ref, w_ref, mod_ref, after_ref, do_ref, dg_ref, dw_ref, dgate_ref, dw_acc):
        i = pl.program_id(0)

        @pl.when(i == 0)
        def _():
            dw_acc[...] = jnp.zeros_like(dw_acc)
            dgate_ref[...] = jnp.zeros_like(dgate_ref)

        dxn_v, ov, gv = dxn_ref[...], o_ref[...].astype(F32), g_ref[...].astype(F32)
        dgate_ref[...] += jnp.sum(dxn_v * br_ref[...].astype(F32), axis=0, keepdims=True)
        dbr = (dxn_v * mod_ref[2:3, :]).astype(BF16)
        du = _dot_nt(dbr, w_ref[...])
        sg = _sigmoid(gv)
        sl = gv * sg
        dw_acc[...] += _dot_tn((ov * sl).astype(BF16), dbr)
        do = du * sl
        dg_ref[...] = (du * ov * (sg * (1.0 + gv * (1.0 - sg)))).astype(BF16)
        low = _low_half(tile)
        for b in range(D // LANES):
            blk = do[:, LANES * b:LANES * (b + 1)]
            do_ref[:, 2 * LANES * b:2 * LANES * b + LANES] = jnp.where(low, blk, 0.0).astype(BF16)
            do_ref[:, 2 * LANES * b + LANES:2 * LANES * (b + 1)] = jnp.where(low, 0.0, blk).astype(BF16)

        @pl.when(i == steps - 1)
        def _():
            dw_ref[...] = dw_acc[...].astype(BF16)

    row = lambda w_: pl.BlockSpec((tile, w_), lambda i: (i, 0))
    return _call(
        body, name=f"attn_out_proj_bwd_{j}", grid=(steps,),
        out_shape=(jax.ShapeDtypeStruct((seq, N_HEADS * LANES), BF16), jax.ShapeDtypeStruct((seq, D), BF16),
                   jax.ShapeDtypeStruct((D, D), BF16), jax.ShapeDtypeStruct((1, D), F32)),
        in_specs=[row(D), row(D), row(D), row(D), _const_spec((D, D)), _mod_spec(layer), ANY_SPEC],
        out_specs=(row(N_HEADS * LANES), row(D), pl.BlockSpec((D, D), lambda i: (0, 0)),
                   pl.BlockSpec((1, D), lambda i: (0, 0))),
        scratch_shapes=[pltpu.VMEM((D, D), F32)],
        compiler_params=_cparams(1),
    )(dxn, br, o, g, w, mod, after)


def _attn_backward(sinks, qs, dos, kd, vd, j, after):
    seq = qs.shape[0]
    nb = seq // QBLK

    def body(sink_ref, q_ref, do_ref, kp_ref, kc_ref, vp_ref, vc_ref, after_ref, dq_ref, dk_ref, dv_ref, dsink_ref,
             carry_k, carry_v, sink_acc):
        n = pl.program_id(0)

        @pl.when(n == 0)
        def _():
            carry_k[...] = jnp.zeros_like(carry_k)
            carry_v[...] = jnp.zeros_like(carry_v)
            sink_acc[...] = jnp.zeros_like(sink_acc)

        @pl.when(n < nb)
        def _():
            ok = _band_mask(n, 2 * QBLK, True)
            low = _low_half(QBLK)
            lane_q = lax.broadcasted_iota(jnp.int32, (1, 2 * QBLK), 1)
            dk_parts, dv_parts = [], []

            def first_products(g):
                kv, half = divmod(g, 2)
                heads = (4 * kv + half, 4 * kv + 2 + half)
                q = _stack_heads(q_ref, heads)
                do = _stack_heads(do_ref, heads)
                kk = _kv_block(kp_ref, kc_ref, kv)
                return heads, q, do, kk, _dot_nt(kk, q), _dot_nt(_kv_block(vp_ref, vc_ref, kv), do)

            nxt = first_products(0)
            dq_h, dk_kv, dv_kv = [], None, None
            for g in range(2 * N_KV):
                heads, q, do, kk, s_raw, dp_raw = nxt
                if g + 1 < 2 * N_KV:
                    nxt = first_products(g + 1)
                st = jnp.where(ok, s_raw, -1e30)
                sink = jnp.where(lane_q < QBLK, sink_ref[j, heads[0]], sink_ref[j, heads[1]])
                m = jnp.maximum(jnp.max(st, axis=0, keepdims=True), sink)
                e = jnp.exp(st - m)
                e_sink = jnp.exp(sink - m)
                inv = 1.0 / (jnp.sum(e, axis=0, keepdims=True) + e_sink)
                p = e * inv
                pdp = p * dp_raw
                delta = jnp.sum(pdp, axis=0, keepdims=True)
                ds = (pdp - p * delta).astype(BF16)
                sink_acc[g:g + 1, :] -= e_sink * inv * delta
                dk_g, dv_g = _dot(ds, q), _dot(p.astype(BF16), do)
                dk_kv = dk_g if dk_kv is None else dk_kv + dk_g
                dv_kv = dv_g if dv_kv is None else dv_kv + dv_g
                dq_h.append(_dot_tn(ds, kk))
                if g % 2 == 1:
                    kv = g // 2
                    for t in range(2):
                        dq_ref[:, LANES * (2 * kv + t):LANES * (2 * kv + t + 1)] = jnp.where(
                            low, dq_h[0][QBLK * t:QBLK * (t + 1)], dq_h[1][QBLK * t:QBLK * (t + 1)])
                    dk_parts.append(dk_kv + pltpu.roll(dk_kv, HEAD_DIM, 1))
                    dv_parts.append(dv_kv + pltpu.roll(dv_kv, HEAD_DIM, 1))
                    dq_h, dk_kv, dv_kv = [], None, None

            def order(parts, lo, hi):
                return jnp.concatenate([jnp.where(low, parts[0][lo:hi], parts[1][lo:hi]),
                                        jnp.where(low, parts[2][lo:hi], parts[3][lo:hi])], axis=1)

            dk_ref[...] = carry_k[...] + order(dk_parts, 0, QBLK)
            dv_ref[...] = (carry_v[...] + order(dv_parts, 0, QBLK)).astype(BF16)
            carry_k[...] = order(dk_parts, QBLK, 2 * QBLK)
            carry_v[...] = order(dv_parts, QBLK, 2 * QBLK)

        @pl.when(n == nb)
        def _():
            dk_ref[...] = carry_k[...]
            dv_ref[...] = carry_v[...].astype(BF16)
            lane = lax.broadcasted_iota(jnp.int32, (1, LANES), 1)
            out = jnp.zeros((1, LANES), F32)
            for g in range(2 * N_KV):
                for t in range(2):
                    tot = jnp.sum(sink_acc[g:g + 1, QBLK * t:QBLK * (t + 1)], axis=1, keepdims=True)
                    out = jnp.where(lane == 4 * (g // 2) + 2 * t + g % 2, tot, out)
            dsink_ref[...] = out

    cur = lambda w: pl.BlockSpec((QBLK, w), lambda n: (jnp.minimum(n, nb - 1), 0))
    prev = lambda w: pl.BlockSpec((QBLK, w), lambda n: (jnp.maximum(n - 1, 0), 0))
    return _call(
        body, name=f"attn_backward_{j}", grid=(nb + 1,),
        out_shape=(jax.ShapeDtypeStruct((seq, D), F32), jax.ShapeDtypeStruct((seq, N_KV * HEAD_DIM), F32),
                   jax.ShapeDtypeStruct((seq, N_KV * HEAD_DIM), BF16), jax.ShapeDtypeStruct((1, LANES), F32)),
        in_specs=[pl.BlockSpec(memory_space=pltpu.SMEM), cur(N_HEADS * LANES), cur(N_HEADS * LANES), prev(KX_W), cur(KX_W),
                  prev(KX_W), cur(KX_W), ANY_SPEC],
        out_specs=(cur(D), prev(N_KV * HEAD_DIM), prev(N_KV * HEAD_DIM), pl.BlockSpec((1, LANES), lambda n: (0, 0))),
        scratch_shapes=[pltpu.VMEM((QBLK, N_KV * HEAD_DIM), F32), pltpu.VMEM((QBLK, N_KV * HEAD_DIM), F32),
                        pltpu.VMEM((2 * N_KV, 2 * QBLK), F32)],
        compiler_params=_cparams(1),
    )(sinks, qs, dos, kd, kd, vd, vd, after)


def _in_proj_tail(x_ref, dxn_ref, ng_ref, mod_ref, w_ref, dproj, dx_ref, dw_acc, vec_acc):
    ng, sc, sh = ng_ref[...], mod_ref[1:2, :], mod_ref[0:1, :]
    xh, r, h = _norm_mod(x_ref[...], ng, sc, sh)
    dh = _dot(dproj, w_ref[...])
    dw_acc[...] += _dot_tn(dproj, h.astype(BF16))
    vec_acc[0:1, :] += jnp.sum(dh, axis=0, keepdims=True)
    vec_acc[1:2, :] += jnp.sum(dh * xh, axis=0, keepdims=True)
    dxh = dh * (ng * (1.0 + sc))
    dx_ref[...] = dxn_ref[...] + r * (dxh - xh * jnp.mean(dxh * xh, axis=-1, keepdims=True))


def _tail_finish(ng_ref, mod_ref, dw_ref, vec_ref, dw_acc, vec_acc):
    dw_ref[...] = dw_acc[...].astype(BF16)
    a = vec_acc[1:2, :]
    vec_ref[...] = jnp.zeros_like(vec_ref)
    vec_ref[0:1, :] = vec_acc[0:1, :]
    vec_ref[1:2, :] = a * ng_ref[...]
    vec_ref[3:4, :] = a * (1.0 + mod_ref[1:2, :])


def _attn_in_proj_bwd(x, dxn, rope, qk_raw, dq, dk, dv, dg, rows, mod, layer, w_t, j, gain, bd, tile):
    seq = x.shape[0]
    steps = seq // tile

    def body(x_ref, dxn_ref, rope_ref, qk_ref, dq_ref, dk_ref, dv_ref, dg_ref, ng_ref, mod_ref, w_ref, gain_ref,
             bd_ref, dx_ref, dw_ref, vec_ref, dgain_ref, dproj, dw_acc, vec_acc):
        i = pl.program_id(0)

        @pl.when(i == 0)
        def _():
            dw_acc[...] = jnp.zeros_like(dw_acc)
            vec_acc[...] = jnp.zeros_like(vec_acc)
            dgain_ref[...] = jnp.zeros_like(dgain_ref)

        tabs = _rope_tabs(rope_ref)
        bdm = bd_ref[...]
        for b in range(QK_W // LANES):
            cols = slice(LANES * b, LANES * (b + 1))
            raw = qk_ref[:, cols]
            if b < D // LANES:
                dy = dq_ref[:, cols] * (HEAD_DIM ** -0.5)
            else:
                dy = dk_ref[:, LANES * (b - D // LANES):LANES * (b + 1 - D // LANES)]
            dy = _rope_bwd(dy, tabs)
            rr = lax.rsqrt(_group_mean(raw * raw, bdm) + NORM_EPS)
            xh = raw * rr
            dgain_ref[:, cols] += jnp.sum(dy * xh, axis=0, keepdims=True)
            dxh = dy * gain_ref[:, cols]
            dproj[:, cols] = (rr * (dxh - xh * _group_mean(dxh * xh, bdm))).astype(BF16)
        dproj[:, QK_W:QK_W + N_KV * HEAD_DIM] = dv_ref[...]
        dproj[:, QK_W + N_KV * HEAD_DIM:] = dg_ref[...]
        _in_proj_tail(x_ref, dxn_ref, ng_ref, mod_ref, w_ref, dproj[...], dx_ref, dw_acc, vec_acc)

        @pl.when(i == steps - 1)
        def _():
            _tail_finish(ng_ref, mod_ref, dw_ref, vec_ref, dw_acc, vec_acc)

    row = lambda w, dt=None: pl.BlockSpec((tile, w), lambda i: (i, 0))
    fixed = lambda shape: pl.BlockSpec(shape, lambda i: (0,) * len(shape))
    return _call(
        body, name=f"attn_in_proj_bwd_{j}", grid=(steps,),
        out_shape=(jax.ShapeDtypeStruct((seq, D), F32), jax.ShapeDtypeStruct((ATTN_IN, D), BF16),
                   jax.ShapeDtypeStruct((8, D), F32), jax.ShapeDtypeStruct((1, QK_W), F32)),
        in_specs=[row(D), row(D), row(3 * LANES), row(QK_W), row(D), row(N_KV * HEAD_DIM), row(N_KV * HEAD_DIM), row(D),
                  _mod_row_spec(layer, NORM_ROW), _mod_spec(layer), _const_spec((ATTN_IN, D)), _const_spec((1, QK_W)),
                  _const_spec((LANES, LANES))],
        out_specs=(row(D), fixed((ATTN_IN, D)), fixed((8, D)), fixed((1, QK_W))),
        scratch_shapes=[pltpu.VMEM((tile, ATTN_IN), BF16), pltpu.VMEM((ATTN_IN, D), F32), pltpu.VMEM((8, D), F32)],
        compiler_params=_cparams(1),
    )(x, dxn, rope, qk_raw, dq, dk, dv, dg, rows, mod, w_t, gain, bd)


def _pool_in_proj(x, rows, mod, layer, w_t, j, tile):
    seq = x.shape[0]

    def body(x_ref, ng_ref, mod_ref, w_ref, v_ref, g_ref):
        _, _, h = _norm_mod(x_ref[...], ng_ref[...], mod_ref[1:2, :], mod_ref[0:1, :])
        proj = _dot_nt(h.astype(BF16), w_ref[...])
        v_ref[...] = proj[:, :D].astype(BF16)
        g_ref[...] = proj[:, D:].astype(BF16)

    row = pl.BlockSpec((tile, D), lambda i: (i, 0))
    return _call(
        body, name=f"pool_in_proj_{j}", grid=(seq // tile,),
        out_shape=(jax.ShapeDtypeStruct((seq, D), BF16), jax.ShapeDtypeStruct((seq, D), BF16)),
        in_specs=[row, _mod_row_spec(layer, NORM_ROW), _mod_spec(layer), _const_spec((POOL_IN, D))],
        out_specs=(row, row),
        compiler_params=_cparams(1),
    )(x, rows, mod, w_t)


PAD = 8


def _window_sums(ext, lo, hi, forward):
    gw = D // len(POOL_WINDOWS)
    planes = []
    for gi, w in enumerate(POOL_WINDOWS):
        cols = slice(gw * gi, gw * (gi + 1))
        src, k = 0, 1
        while k < w:
            d = k if forward else -k
            ext[1 - src, lo:hi, cols] = ext[src, lo:hi, cols] + ext[src, lo + d:hi + d, cols]
            src, k = 1 - src, 2 * k
        planes.append(src)
    return planes


def _pooled(ext, v_ref, first, tile):
    t_abs = first + lax.broadcasted_iota(jnp.int32, (tile, 1), 0)
    top = PAD + HALO
    planes = _window_sums(ext, PAD, top + tile, False)
    outs = []
    gw = D // len(POOL_WINDOWS)
    for gi, w in enumerate(POOL_WINDOWS):
        cols = slice(gw * gi, gw * (gi + 1))
        cnt = jnp.minimum(t_abs + 1, w).astype(F32)
        outs.append(ext[planes[gi], top:top + tile, cols] / cnt - v_ref[:, cols].astype(F32))
    return jnp.concatenate(outs, axis=1)


def _fill_ext(ext, halo_ref, v_ref, i, tile):
    ext[0, 0:PAD, :] = jnp.zeros((PAD, D), F32)
    ext[1, 0:PAD, :] = jnp.zeros((PAD, D), F32)
    ext[0, PAD:PAD + HALO, :] = jnp.where(i == 0, 0.0, halo_ref[...].astype(F32))
    ext[0, PAD + HALO:PAD + HALO + tile, :] = v_ref[...].astype(F32)


def _group_mix(pb, wg_ref):
    gw = D // len(POOL_WINDOWS)
    return jnp.concatenate([_dot(pb[:, gw * gi:gw * (gi + 1)], wg_ref[gi]) for gi in range(len(POOL_WINDOWS))], axis=1)


def _pool_mix_out(x, v, g, wg, w_out, j, rows, mod, layer, tile, target=None):
    seq = x.shape[0]

    def body(*refs):
        if target is None:
            x_ref, v_ref, halo_ref, g_ref, wg_ref, w_ref, scale_ref, mod_ref, xo_ref, br_ref, ext = refs
        else:
            x_ref, v_ref, halo_ref, g_ref, wg_ref, w_ref, scale_ref, mod_ref, t_ref, xo_ref, br_ref, loss_ref, ext = refs
        i = pl.program_id(0)
        _fill_ext(ext, halo_ref, v_ref, i, tile)
        pb = _pooled(ext, v_ref, i * tile, tile).astype(BF16)
        ms = _group_mix(pb, wg_ref) * scale_ref[...]
        gv = g_ref[...].astype(F32)
        u = (ms * (gv * _sigmoid(gv))).astype(BF16)
        br = _dot(u, w_ref[...])
        br_ref[...] = br.astype(BF16)
        y = x_ref[...] + mod_ref[2:3, :] * br
        if target is None:
            xo_ref[...] = y
        else:
            @pl.when(i == 0)
            def _():
                loss_ref[...] = jnp.zeros_like(loss_ref)

            e = y - t_ref[...]
            xo_ref[...] = e * (1.0 / D)
            loss_ref[...] += 0.5 * jnp.sum(jnp.mean(e * e, axis=-1, keepdims=True), axis=0, keepdims=True)

    row = pl.BlockSpec((tile, D), lambda i: (i, 0))
    halo = pl.BlockSpec((HALO, D), lambda i: (jnp.maximum(i * (tile // HALO) - 1, 0), 0))
    extra_in, extra_out, extra_shape = ([], (), ()) if target is None else (
        [row], (pl.BlockSpec((1, LANES), lambda i: (0, 0)),), (jax.ShapeDtypeStruct((1, LANES), F32),))
    return _call(
        body, name=f"pool_mix_out_{j}", grid=(seq // tile,),
        out_shape=(jax.ShapeDtypeStruct((seq, D), F32), jax.ShapeDtypeStruct((seq, D), BF16)) + extra_shape,
        in_specs=[row, row, halo, row, _const_spec(wg.shape), _const_spec((D, D)), _mod_row_spec(layer, POOL_SCALE_ROW),
                  _mod_spec(layer)] + extra_in,
        out_specs=(row, row) + extra_out,
        scratch_shapes=[pltpu.VMEM((2, tile + HALO + PAD, D), F32)],
        compiler_params=_cparams(1),
    )(x, v, v, g, wg, w_out, rows, mod, *(() if target is None else (target,)))


def _pool_mix_out_bwd(dxn, br, v, g, wg, w_out, j, rows, mod, layer, tile, after):
    seq = dxn.shape[0]
    steps = seq // tile
    ng_ = len(POOL_WINDOWS)
    gw = D // ng_

    def body(dxn_ref, br_ref, v_ref, halo_ref, g_ref, wg_ref, w_ref, scale_ref, mod_ref, after_ref,
             dpool_ref, dg_ref, dw_ref, dwg_ref, vec_ref, ext, dw_acc, dwg_acc):
        i = pl.program_id(0)

        @pl.when(i == 0)
        def _():
            dw_acc[...] = jnp.zeros_like(dw_acc)
            dwg_acc[...] = jnp.zeros_like(dwg_acc)
            vec_ref[...] = jnp.zeros_like(vec_ref)

        _fill_ext(ext, halo_ref, v_ref, i, tile)
        pb = _pooled(ext, v_ref, i * tile, tile).astype(BF16)
        mixed = _group_mix(pb, wg_ref)
        scale = scale_ref[...]
        ms = mixed * scale
        gv, dxn_v = g_ref[...].astype(F32), dxn_ref[...]
        sg = _sigmoid(gv)
        sl = gv * sg
        vec_ref[0:1, :] += jnp.sum(dxn_v * br_ref[...].astype(F32), axis=0, keepdims=True)
        dbr = (dxn_v * mod_ref[2:3, :]).astype(BF16)
        du = _dot_nt(dbr, w_ref[...])
        dw_acc[...] += _dot_tn((ms * sl).astype(BF16), dbr)
        dms = du * sl
        dg_ref[...] = (du * ms * (sg * (1.0 + gv * (1.0 - sg)))).astype(BF16)
        vec_ref[1:2, :] += jnp.sum(dms * mixed, axis=0, keepdims=True)
        dmx = (dms * scale).astype(BF16)
        for gi in range(ng_):
            cols = slice(gw * gi, gw * (gi + 1))
            dpool_ref[:, cols] = _dot_nt(dmx[:, cols], wg_ref[gi])
            dwg_acc[gi] += _dot_tn(pb[:, cols], dmx[:, cols])

        @pl.when(i == steps - 1)
        def _():
            dw_ref[...] = dw_acc[...].astype(BF16)
            dwg_ref[...] = dwg_acc[...].astype(BF16)

    row = pl.BlockSpec((tile, D), lambda i: (i, 0))
    halo = pl.BlockSpec((HALO, D), lambda i: (jnp.maximum(i * (tile // HALO) - 1, 0), 0))
    fixed = lambda shape: pl.BlockSpec(shape, lambda i: (0,) * len(shape))
    return _call(
        body, name=f"pool_mix_out_bwd_{j}", grid=(steps,),
        out_shape=(jax.ShapeDtypeStruct((seq, D), F32), jax.ShapeDtypeStruct((seq, D), BF16),
                   jax.ShapeDtypeStruct((D, D), BF16), jax.ShapeDtypeStruct((ng_, gw, gw), BF16),
                   jax.ShapeDtypeStruct((8, D), F32)),
        in_specs=[row, row, row, halo, row, _const_spec(wg.shape), _const_spec((D, D)), _mod_row_spec(layer, POOL_SCALE_ROW),
                  _mod_spec(layer), ANY_SPEC],
        out_specs=(row, row, fixed((D, D)), fixed((ng_, gw, gw)), fixed((8, D))),
        scratch_shapes=[pltpu.VMEM((2, tile + HALO + PAD, D), F32), pltpu.VMEM((D, D), F32), pltpu.VMEM((ng_, gw, gw), F32)],
        compiler_params=_cparams(1),
    )(dxn, br, v, v, g, wg, w_out, rows, mod, after)


def _pool_in_proj_bwd(x, dxn, dpool, dg, rows, mod, layer, w_t, j, tile, after):
    seq = x.shape[0]
    steps = seq // tile
    gw = D // len(POOL_WINDOWS)

    def body(x_ref, dxn_ref, dp_ref, halo_ref, dg_ref, ng_ref, mod_ref, w_ref, after_ref, dx_ref, dw_ref, vec_ref,
             ext, dproj, dw_acc, vec_acc):
        i = pl.program_id(0)

        @pl.when(i == 0)
        def _():
            dw_acc[...] = jnp.zeros_like(dw_acc)
            vec_acc[...] = jnp.zeros_like(vec_acc)

        t_abs = i * tile + lax.broadcasted_iota(jnp.int32, (tile, 1), 0)
        last = i == steps - 1
        ext[0, tile + HALO:tile + HALO + PAD, :] = jnp.zeros((PAD, D), F32)
        ext[1, tile + HALO:tile + HALO + PAD, :] = jnp.zeros((PAD, D), F32)
        for gi, w in enumerate(POOL_WINDOWS):
            cols = slice(gw * gi, gw * (gi + 1))
            cnt = jnp.minimum(t_abs + 1, w).astype(F32)
            ext[0, 0:tile, cols] = dp_ref[:, cols] / cnt
            ext[0, tile:tile + HALO, cols] = jnp.where(last, 0.0, halo_ref[:, cols] * (1.0 / w))
        planes = _window_sums(ext, 0, tile + HALO, True)
        for gi, w in enumerate(POOL_WINDOWS):
            cols = slice(gw * gi, gw * (gi + 1))
            dproj[:, cols] = (ext[planes[gi], 0:tile, cols] - dp_ref[:, cols]).astype(BF16)
        dproj[:, D:] = dg_ref[...]
        _in_proj_tail(x_ref, dxn_ref, ng_ref, mod_ref, w_ref, dproj[...], dx_ref, dw_acc, vec_acc)

        @pl.when(last)
        def _():
            _tail_finish(ng_ref, mod_ref, dw_ref, vec_ref, dw_acc, vec_acc)

    row = pl.BlockSpec((tile, D), lambda i: (i, 0))
    halo = pl.BlockSpec((HALO, D), lambda i: (jnp.minimum((i + 1) * (tile // HALO), seq // HALO - 1), 0))
    fixed = lambda shape: pl.BlockSpec(shape, lambda i: (0,) * len(shape))
    return _call(
        body, name=f"pool_in_proj_bwd_{j}", grid=(steps,),
        out_shape=(jax.ShapeDtypeStruct((seq, D), F32), jax.ShapeDtypeStruct((POOL_IN, D), BF16),
                   jax.ShapeDtypeStruct((8, D), F32)),
        in_specs=[row, row, row, halo, row, _mod_row_spec(layer, NORM_ROW), _mod_spec(layer), _const_spec((POOL_IN, D)), ANY_SPEC],
        out_specs=(row, fixed((POOL_IN, D)), fixed((8, D))),
        scratch_shapes=[pltpu.VMEM((2, tile + HALO + PAD, D), F32), pltpu.VMEM((tile, POOL_IN), BF16), pltpu.VMEM((POOL_IN, D), F32),
                        pltpu.VMEM((8, D), F32)],
        compiler_params=_cparams(1),
    )(x, dxn, dpool, dpool, dg, rows, mod, w_t, after)


def _build_vec(vecs, gates, pool_vecs, gains, dsinks, loss_part):
    def body(v0, v1, v2, v3, g0, g2, p0, p1, n0, n1, s0, s1, loss_ref, out):
        out[...] = jnp.zeros_like(out)
        for i, v in enumerate((v0, v1, v2, v3)):
            out[3 * i:3 * i + 2, :] = v[0:2, :]
            out[12 + i:13 + i, :] = v[3:4, :]
        out[2:3, :] = g0[...]
        out[8:9, :] = g2[...]
        for j, (p, n, s) in enumerate(((p0, n0, s0), (p1, n1, s1))):
            out[3 * (2 * j + 1) + 2:3 * (2 * j + 1) + 3, :] = p[0:1, :]
            out[22 + j:23 + j, :] = p[1:2, :]
            out[16 + j:17 + j, :] = n[:, 0:D]
            out[18 + j:19 + j, 0:QK_W - D] = n[:, D:QK_W]
            out[20 + j:21 + j, 0:LANES] = s[...]
        out[24:25, 0:LANES] = loss_ref[...]

    vm = pl.BlockSpec(memory_space=pltpu.VMEM)
    args = (*vecs, gates[0], gates[2], *pool_vecs, *gains, *dsinks, loss_part)
    return _call(
        body, name="build_vec",
        out_shape=jax.ShapeDtypeStruct((VEC_ROWS, D), F32),
        in_specs=[vm] * len(args), out_specs=vm,
        compiler_params=_cparams(),
    )(*args)


def _sum_devices(g, after):
    rows = g.shape[1]

    def body(g_ref, after_ref, tot_ref, fold_ref):
        tot = g_ref[0]
        for p in range(1, N_DEV):
            tot = tot + g_ref[p]
        tot_ref[...] = tot
        f = tot[16:24, 0:LANES]
        for b in range(1, D // LANES):
            f = f + tot[16:24, LANES * b:LANES * (b + 1)]
        fold_ref[...] = f + pltpu.roll(f, HEAD_DIM, 1)

    return _call(
        body, name="sum_devices",
        out_shape=(jax.ShapeDtypeStruct((rows, D), F32), jax.ShapeDtypeStruct((8, LANES), F32)),
        in_specs=[pl.BlockSpec(memory_space=pltpu.VMEM), ANY_SPEC],
        out_specs=(pl.BlockSpec(memory_space=pltpu.VMEM), pl.BlockSpec(memory_space=pltpu.VMEM)),
        compiler_params=_cparams(),
    )(g, after)


def _adamw_small(params):
    n = len(params)

    def body(*refs):
        ins, outs = refs[:4 * n], refs[4 * n:]
        for p in range(n):
            w_ref, g_ref, m_ref, v_ref = ins[4 * p:4 * p + 4]
            outs[3 * p][...], outs[3 * p + 1][...], outs[3 * p + 2][...] = _adamw(w_ref[...], g_ref[...], m_ref[...], v_ref[...])

    vm = pl.BlockSpec(memory_space=pltpu.VMEM)
    out = _call(
        body, name="adamw_small",
        out_shape=tuple(jax.ShapeDtypeStruct(w.shape, F32) for (w, _, _, _) in params for _ in range(3)),
        in_specs=[vm] * (4 * n), out_specs=tuple([vm] * (3 * n)),
        compiler_params=_cparams(),
    )(*[a for p in params for a in p])
    return [tuple(out[3 * p:3 * p + 3]) for p in range(n)]


def _adamw_shards(name, me, fulls, lands, w, m, v, transpose, axis=0):
    nl = w.shape[0]
    wshape = w.shape[1:]
    own_shape = lands[0].shape[1:]

    def body(me_ref, *refs):
        own_refs, land_refs = refs[:nl], refs[nl:2 * nl]
        w_ref, m_ref, v_ref, g_out, d_out, m_out, v_out = refs[2 * nl:]
        layer = pl.program_id(0)
        for l in range(nl):
            @pl.when(layer == l)
            def _(l=l):
                g = own_refs[l][...].astype(F32)
                for k in range(N_DEV - 1):
                    g = g + land_refs[l][k].astype(F32)
                if transpose:
                    g = g.T
                g_out[...] = g
                d_out[...], m_out[...], v_out[...] = _adamw(w_ref[...], g, m_ref[...], v_ref[...])

    def own_index(l_, me_ref):
        idx = [0] * len(own_shape)
        idx[axis] = me_ref[0]
        return tuple(idx)

    own_spec = pl.BlockSpec(tuple(own_shape), own_index)
    land_spec = pl.BlockSpec((N_DEV - 1,) + tuple(own_shape), lambda l_, me_ref: (0,) * (1 + len(own_shape)))
    wspec = pl.BlockSpec((None,) + tuple(wshape), lambda l_, me_ref: (l_,) + (0,) * len(wshape))
    return _call(
        body, name=name,
        grid_spec=pltpu.PrefetchScalarGridSpec(num_scalar_prefetch=1, grid=(nl,),
                                               in_specs=[own_spec] * nl + [land_spec] * nl + [wspec] * 3,
                                               out_specs=(wspec,) * 4),
        out_shape=tuple(jax.ShapeDtypeStruct(w.shape, F32) for _ in range(4)),
        compiler_params=_cparams(1),
    )(me.reshape(1), *fulls, *lands, w, m, v)


def _constants():
    lane = np.arange(LANES)
    bd = (lane[:, None] // HEAD_DIM == lane[None, :] // HEAD_DIM).astype(np.float32)
    half = ROT_DIM // 2
    inv_freq = ROPE_THETA ** (-jnp.arange(half, dtype=F32) * 2.0 / ROT_DIM)
    invf = jnp.tile(inv_freq, LANES // half).reshape(1, LANES)
    return jnp.asarray(bd, BF16), invf


def kernel(x, c, positions, ada_w, ada_b, norm_g, attn_w_in, attn_q_norm, attn_k_norm, attn_sinks, attn_w_out, pool_w_in, pool_w_group, pool_scale, pool_w_out, loss_target, m_ada_w, m_ada_b, m_norm_g, m_attn_w_in, m_attn_q_norm, m_attn_k_norm, m_attn_sinks, m_attn_w_out, m_pool_w_in, m_pool_w_group, m_pool_scale, m_pool_w_out, v_ada_w, v_ada_b, v_norm_g, v_attn_w_in, v_attn_q_norm, v_attn_k_norm, v_attn_sinks, v_attn_w_out, v_pool_w_in, v_pool_w_group, v_pool_scale, v_pool_w_out):
    seq = x.shape[1]
    me = 4 * lax.axis_index("x") + 2 * lax.axis_index("y") + lax.axis_index("c")
    bd, invf = _constants()
    t_mm = min(512, seq)
    rope = _rope_table(positions.reshape(seq, 1), invf, t_mm)
    t_bw = min(256, seq)
    shard = pool_scale.shape[1]
    cols = ada_w.shape[2]

    w_in_rows = jnp.swapaxes(attn_w_in, 1, 2)
    w_first, = _prep_weights(me, [(w_in_rows, 0, "N")], "prep_first")
    first_w, token = _gather_first_start(w_first, c)
    prepped = _prep_weights(me, [(attn_w_out, 0, "N"), (pool_w_in, 0, "T"), (pool_w_out, 0, "N"), (pool_w_group, 0, "G"),
                                 (w_in_rows, 1, "N"), (attn_w_out, 1, "N"), (pool_w_in, 1, "T"), (pool_w_out, 1, "N"),
                                 (pool_w_group, 1, "G")], "prep_rest")

    first = jnp.concatenate([c, jnp.pad(pool_scale, ((0, 0), (0, D - shard))), jnp.zeros((5, D), F32)], axis=0)
    first = _allgather_small(first + token[0, 0], "allgather_c", rope)
    c_all = first[:, 0, :]
    scale_full = jnp.transpose(first[:, 1:3, :shard], (1, 0, 2)).reshape(2, D)
    mod_part = _ada_forward(c_all, ada_w)
    mod_all = _allgather_small(mod_part.reshape(DEPTH * N_DEV, cols), "allgather_mod", prepped[0])
    mod_all = mod_all.reshape(N_DEV, DEPTH, N_DEV, cols)
    mine = lax.dynamic_index_in_dim(mod_all, me, axis=2, keepdims=False)
    mod = jnp.transpose(mine, (1, 0, 2)).reshape(DEPTH, 3 * D) + ada_b
    pool_rows = jnp.stack([jnp.zeros_like(scale_full[0]), scale_full[0], jnp.zeros_like(scale_full[0]), scale_full[1]])
    mod = jnp.concatenate([mod.reshape(DEPTH, 3, D), norm_g[:, None, :], pool_rows[:, None, :],
                           jnp.zeros((DEPTH, 3, D), F32)], axis=1)
    rows = mod.reshape(DEPTH, 8, 1, D)

    groups = [prepped[0:1], prepped[1:4], prepped[4:6], prepped[6:9]]
    gaxes = [(0,), (0,), (0, 0, 1), (0, 0), (0, 0, 1)]
    first_w, token = _gather_first_forward(first_w, mod)
    rest, token = _gather_start(groups, gaxes[1:], token, "gather_start_rest")
    started = [None] + rest

    saved, weights = [], []
    h = x[0]
    for i in range(DEPTH):
        j = i // 2
        s = dict(x=h)
        if i == 0:
            w_in_t = _gather_first_wait(first_w, token)
        else:
            wts = _gather_wait(started[i + 1], gaxes[i + 1], h, f"gather_wait_{i}")
        if i % 2 == 0:
            if i > 0:
                w_in_t, w_out = wts
            s["gain"] = jnp.concatenate([jnp.tile(attn_q_norm[j], N_HEADS), jnp.tile(attn_k_norm[j], N_KV)]).reshape(1, QK_W)
            s["qk_raw"], s["qs"], s["kd"], s["vd"], s["g"] = _attn_in_proj(
                h, rope, rows, mod, i, w_in_t, j, s["gain"], bd, t_mm)
            s["o"] = _attn_forward(attn_sinks, s["qs"], s["kd"], s["vd"], j)
            if i == 0:
                w_out, = _gather_wait(started[1], gaxes[1], s["o"], "gather_wait_0_out")
            h, s["br"] = _attn_out_proj(h, s["o"], s["g"], w_out, j, mod, i, t_mm)
            weights.append((w_in_t, w_out))
        else:
            p_in_t, p_out, p_grp = wts
            s["v"], s["g"] = _pool_in_proj(h, rows, mod, i, p_in_t, j, t_mm)
            if i < DEPTH - 1:
                h, s["br"] = _pool_mix_out(h, s["v"], s["g"], p_grp, p_out, j, rows, mod, i, t_mm)
            else:
                dx, s["br"], loss_part = _pool_mix_out(h, s["v"], s["g"], p_grp, p_out, j, rows, mod, i, t_mm,
                                                       loss_target[0])
            weights.append(wts)
        saved.append(s)

    vecs, gates, gains, dsinks, pool_vecs = [None] * DEPTH, [None] * DEPTH, [None] * 2, [None] * 2, [None] * 2
    sent = {}
    token = jnp.zeros((8, LANES), F32)
    for i in reversed(range(DEPTH)):
        j = i // 2
        s = saved[i]
        if i % 2 == 0:
            w_in_t, w_out = weights[i]
            dos, dg, d_w_out, gates[i] = _attn_out_proj_bwd(dx, s["br"], s["o"], s["g"], w_out, j, mod, i, t_mm, token)
            if i == 0:
                sent["0_out"], token = _scatter_start([d_w_out], (0,), "scatter_start_0_out", token)
            dq, dk, dv, dsinks[j] = _attn_backward(attn_sinks, s["qs"], dos, s["kd"], s["vd"], j, token)
            dx, d_in_t, vecs[i], gains[j] = _attn_in_proj_bwd(
                s["x"], dx, rope, s["qk_raw"], dq, dk, dv, dg, rows, mod, i, w_in_t, j, s["gain"], bd, t_bw)
            if i > 0:
                sent[i], token = _scatter_start([d_in_t, d_w_out], (0, 0), f"scatter_start_{i}", token)
        else:
            p_in_t, p_out, p_grp = weights[i]
            dpool, dg, d_p_out, d_p_grp, pool_vecs[j] = _pool_mix_out_bwd(
                dx, s["br"], s["v"], s["g"], p_grp, p_out, j, rows, mod, i, t_mm, token)
            dx, d_in_t, vecs[i] = _pool_in_proj_bwd(s["x"], dx, dpool, dg, rows, mod, i, p_in_t, j, t_bw, token)
            sent[i], token = _scatter_start([d_in_t, d_p_out, d_p_grp], (0, 0, 1), f"scatter_start_{i}", token)

    vec = _build_vec(vecs, gates, pool_vecs, gains, dsinks, loss_part)
    vec_rows = lax.dynamic_update_slice(jnp.zeros((N_DEV * VEC_ROWS, D), F32), vec, (me * VEC_ROWS, 0))
    vec_sent, token = _gather_start([[vec_rows]], [(0,)], loss_part, "vec_gather_start")
    sent["0_in"], token = _scatter_start([d_in_t], (0,), "scatter_start_0_in", token)

    got = {}
    for i in (3, 1):
        fulls, lands = _scatter_wait(sent[i], (0, 0, 1), token, f"scatter_wait_{i}")
        got[i] = dict(zip(("in", "out", "grp"), zip(fulls, lands)))
    pick = lambda ls, kind: ([got[i][kind][0] for i in ls], [got[i][kind][1] for i in ls])
    res = {}
    res["pool_w_in"] = _adamw_shards("adamw_pool_w_in", me, *pick((1, 3), "in"), pool_w_in, m_pool_w_in, v_pool_w_in, True)
    res["pool_w_out"] = _adamw_shards("adamw_pool_w_out", me, *pick((1, 3), "out"), pool_w_out, m_pool_w_out,
                                      v_pool_w_out, False)
    res["pool_w_group"] = _adamw_shards("adamw_pool_w_group", me, *pick((1, 3), "grp"), pool_w_group, m_pool_w_group,
                                        v_pool_w_group, False, axis=1)

    vec_all, = _gather_wait(vec_sent[0], (0,), res["pool_w_group"][0], "vec_gather_wait")
    vec_all = vec_all.reshape(N_DEV, VEC_ROWS, D)
    tot, folded = _sum_devices(vec_all, token)
    loss = tot[24, 0]
    small = dict(
        ada_b=(ada_b, tot[0:12].reshape(DEPTH, 3 * D), m_ada_b, v_ada_b),
        norm_g=(norm_g, tot[12:16], m_norm_g, v_norm_g),
        q_norm=(attn_q_norm, folded[0:2, :HEAD_DIM], m_attn_q_norm, v_attn_q_norm),
        k_norm=(attn_k_norm, folded[2:4, :HEAD_DIM], m_attn_k_norm, v_attn_k_norm),
        sinks=(attn_sinks, tot[20:22, :N_HEADS], m_attn_sinks, v_attn_sinks),
        pool_scale=(pool_scale, lax.dynamic_slice(tot, (22, me * shard), (2, shard)), m_pool_scale, v_pool_scale),
    )
    res.update({k: (a[1],) + upd for (k, a), upd in zip(small.items(), _adamw_small(list(small.values())))})

    dmod_all = vec_all[:, 0:12, :].reshape(N_DEV, DEPTH, 3 * D)
    dmod_mine = lax.dynamic_slice_in_dim(dmod_all, me * cols, cols, axis=2)
    dmod_mine = jnp.pad(jnp.transpose(dmod_mine, (1, 0, 2)), ((0, 0), (0, N_DEV), (0, 0))) + token[0, 0]
    res["ada_w"] = _ada_backward_adamw(jnp.pad(c_all, ((0, N_DEV), (0, 0))), dmod_mine, ada_w, m_ada_w, v_ada_w)

    fulls, lands = _scatter_wait(sent[2], (0, 0), res["ada_w"][0], "scatter_wait_2")
    got[2] = dict(zip(("in", "out"), zip(fulls, lands)))
    got[0] = {}
    for kind in ("out", "in"):
        fulls, lands = _scatter_wait(sent["0_" + kind], (0,), res["ada_w"][0], "scatter_wait_0_" + kind)
        got[0][kind] = (fulls[0], lands[0])
    res["attn_w_out"] = _adamw_shards("adamw_attn_w_out", me, *pick((0, 2), "out"), attn_w_out, m_attn_w_out,
                                      v_attn_w_out, False)
    res["attn_w_in"] = tuple(jnp.swapaxes(a, 1, 2) for a in _adamw_shards(
        "adamw_attn_w_in", me, *pick((0, 2), "in"), w_in_rows, jnp.swapaxes(m_attn_w_in, 1, 2),
        jnp.swapaxes(v_attn_w_in, 1, 2), False))

    order = ("ada_w", "ada_b", "norm_g", "attn_w_in", "q_norm", "k_norm", "sinks", "attn_w_out", "pool_w_in",
             "pool_w_group", "pool_scale", "pool_w_out")
    return (loss, dx[None], *[res[k][0] for k in order], *[res[k][1] for k in order], *[res[k][2] for k in order],
            *[res[k][3] for k in order])
```

```python
import numpy as np
import jax
import jax.numpy as jnp
from jax import lax
from jax.experimental import pallas as pl
from jax.experimental.pallas import tpu as pltpu

F32 = jnp.float32
BF16 = jnp.bfloat16
MESH = pl.DeviceIdType.MESH

N_DEV = 8
D = 1024
DEPTH = 4
HEAD_DIM = 64
N_HEADS = 16
N_KV = 4
QK_W = 1280
ATTN_IN = 2560
POOL_IN = 2048
QBLK = 128
KX_W = N_KV * 128
CHUNK = 256
POOL_WINDOWS = (2, 4, 8, 16)
HALO = 16
ROPE_THETA = 500000.0
ROT_DIM = 16
NORM_EPS = 1e-6
ADAM_LR = 0.001
ADAM_B1 = 0.9
ADAM_B2 = 0.999
ADAM_EPS = 1e-08
ADAM_WD = 0.01
ADAM_STEP = 10

LANES = 128
VMEM_LIMIT = 56 * 2**20
VEC_ROWS = 32


def _cparams(n_grid=0, **kw):
    if n_grid:
        kw["dimension_semantics"] = ("arbitrary",) * n_grid
    return pltpu.CompilerParams(vmem_limit_bytes=VMEM_LIMIT, **kw)


def _call(body, **kw):
    return pl.pallas_call(body, **kw)


def _mod_spec(layer):
    return pl.BlockSpec((None, 8, D), lambda *_: (layer, 0, 0), pipeline_mode=pl.Buffered(1))


def _mod_row_spec(layer, row):
    return pl.BlockSpec((None, None, 1, D), lambda *_: (layer, row, 0, 0), pipeline_mode=pl.Buffered(1))


NORM_ROW, POOL_SCALE_ROW = 3, 4


def _const_spec(shape):
    nd = len(shape)
    return pl.BlockSpec(shape, lambda *_: (0,) * nd, pipeline_mode=pl.Buffered(1))


def _dot(a, b):
    return jnp.dot(a, b, preferred_element_type=F32)


def _dot_nt(a, b):
    return lax.dot_general(a, b, (((1,), (1,)), ((), ())), preferred_element_type=F32)


def _dot_tn(a, b):
    return lax.dot_general(a, b, (((0,), (0,)), ((), ())), preferred_element_type=F32)


def _group_mean(x, m):
    return _dot(x.astype(BF16), m) * (1.0 / HEAD_DIM)


def _sigmoid(g):
    return 1.0 / (1.0 + jnp.exp(-g))


def _norm_mod(x, ng, sc, sh):
    r = lax.rsqrt(jnp.mean(x * x, axis=-1, keepdims=True) + NORM_EPS)
    xh = x * r
    h = (xh * ng) * (1.0 + sc) + sh
    return xh, r, h


def _rope_table(pos_col, invf_row, tile):
    seq = pos_col.shape[0]

    def body(pos_ref, invf_ref, out_ref):
        ang = pos_ref[...].astype(F32) * invf_ref[...]
        l64 = lax.broadcasted_iota(jnp.int32, (tile, LANES), 1) & (HEAD_DIM - 1)
        cs, sn = jnp.cos(ang), jnp.sin(ang)
        out_ref[:, 0:LANES] = jnp.where(l64 < ROT_DIM, cs, 1.0)
        out_ref[:, LANES:2 * LANES] = jnp.where(l64 < ROT_DIM // 2, -sn, 0.0)
        out_ref[:, 2 * LANES:3 * LANES] = jnp.where((l64 >= ROT_DIM // 2) & (l64 < ROT_DIM), sn, 0.0)

    return _call(
        body, name="rope_table", grid=(seq // tile,),
        out_shape=jax.ShapeDtypeStruct((seq, 3 * LANES), F32),
        in_specs=[pl.BlockSpec((tile, 1), lambda i: (i, 0)), _const_spec((1, LANES))],
        out_specs=pl.BlockSpec((tile, 3 * LANES), lambda i: (i, 0)),
        compiler_params=_cparams(1),
    )(pos_col, invf_row)


def _rope_tabs(rope_ref):
    return rope_ref[:, 0:LANES], rope_ref[:, LANES:2 * LANES], rope_ref[:, 2 * LANES:3 * LANES]


def _rope(y, tabs):
    cos_t, sin_a, sin_b = tabs
    return y * cos_t + pltpu.roll(y, LANES - ROT_DIM // 2, 1) * sin_a + pltpu.roll(y, ROT_DIM // 2, 1) * sin_b


def _rope_bwd(dy, tabs):
    cos_t, sin_a, sin_b = tabs
    return dy * cos_t + pltpu.roll(dy * sin_a, ROT_DIM // 2, 1) + pltpu.roll(dy * sin_b, LANES - ROT_DIM // 2, 1)


def _low_half(rows):
    return lax.broadcasted_iota(jnp.int32, (rows, LANES), 1) < HEAD_DIM


def _adamw(w, g, m, v):
    m = ADAM_B1 * m + (1.0 - ADAM_B1) * g
    v = ADAM_B2 * v + (1.0 - ADAM_B2) * (g * g)
    m_hat = m / (1.0 - ADAM_B1 ** ADAM_STEP)
    v_hat = v / (1.0 - ADAM_B2 ** ADAM_STEP)
    delta = -ADAM_LR * (m_hat / (jnp.sqrt(v_hat) + ADAM_EPS) + ADAM_WD * w)
    return delta, m, v


def _my_position():
    x, y, c = lax.axis_index("x"), lax.axis_index("y"), lax.axis_index("c")
    return x, y, c, 4 * x + 2 * y + c


def _peers(x, y, c):
    out = []
    for k in range(1, N_DEV):
        px = 1 - x if k & 4 else x
        py = 1 - y if k & 2 else y
        pc = 1 - c if k & 1 else c
        out.append(((px, py, pc), 4 * px + 2 * py + pc))
    return out


def _allgather_small(v, name, after):
    rows, cols = v.shape

    def body(v_ref, after_ref, out_ref, send_sems, recv_sems, local_sem):
        x, y, c, me = _my_position()
        local = pltpu.make_async_copy(v_ref, out_ref.at[me], local_sem)
        local.start()
        sends = []
        for k, (peer, _) in enumerate(_peers(x, y, c)):
            cp = pltpu.make_async_remote_copy(v_ref, out_ref.at[me], send_sems.at[k], recv_sems.at[k],
                                              device_id=peer, device_id_type=MESH)
            cp.start()
            sends.append(cp)
        for k, (peer, idx) in enumerate(_peers(x, y, c)):
            pltpu.make_async_remote_copy(v_ref, out_ref.at[idx], send_sems.at[k], recv_sems.at[k],
                                         device_id=peer, device_id_type=MESH).wait_recv()
        for cp in sends:
            cp.wait_send()
        local.wait()

    return _call(
        body, name=name,
        out_shape=jax.ShapeDtypeStruct((N_DEV, rows, cols), F32),
        in_specs=[pl.BlockSpec(memory_space=pltpu.VMEM), pl.BlockSpec(memory_space=pl.ANY)],
        out_specs=pl.BlockSpec(memory_space=pltpu.VMEM),
        scratch_shapes=[pltpu.SemaphoreType.DMA((N_DEV - 1,)), pltpu.SemaphoreType.DMA((N_DEV - 1,)),
                        pltpu.SemaphoreType.DMA(())],
        compiler_params=_cparams(),
    )(v, after)


def _shard_rows(ref, idx, rows, axis):
    sl = [slice(None)] * len(ref.shape)
    sl[axis] = pl.ds(idx * rows, rows)
    return ref.at[tuple(sl)]


def _own_and_peer_rows(ref, me, idx, axis):
    rows = ref.shape[axis] // N_DEV
    return _shard_rows(ref, me, rows, axis), _shard_rows(ref, idx, rows, axis)


HBM_SPEC = pl.BlockSpec(memory_space=pltpu.HBM)
SEM_SPEC = pl.BlockSpec(memory_space=pltpu.SEMAPHORE)
ANY_SPEC = pl.BlockSpec(memory_space=pl.ANY)
DATAFLOW = pltpu.SideEffectType.DATAFLOW_SIDE_EFFECTING


def _hbm(a):
    return pltpu.with_memory_space_constraint(a, pltpu.HBM)


def _gather_start(layers, axes, after, name):
    flat = [a for arrs in layers for a in arrs]
    flat_axes = [ax for axs in axes for ax in axs]
    n, nl = len(flat), len(layers)

    def body(*refs):
        ins, sems, token = refs[:n], refs[n + 1:n + 1 + 2 * nl], refs[-1]
        x, y, c, me = _my_position()
        a0 = 0
        for li, arrs in enumerate(layers):
            for k, (peer, _) in enumerate(_peers(x, y, c)):
                for a in range(len(arrs)):
                    rows, _ = _own_and_peer_rows(ins[a0 + a], me, me, flat_axes[a0 + a])
                    pltpu.make_async_remote_copy(rows, rows, sems[2 * li].at[k * len(arrs) + a],
                                                 sems[2 * li + 1].at[k * len(arrs) + a],
                                                 device_id=peer, device_id_type=MESH).start()
            a0 += len(arrs)
        token[...] = jnp.zeros_like(token)

    sem_shapes = []
    for arrs in layers:
        sem_shapes += [pltpu.SemaphoreType.DMA(((N_DEV - 1) * len(arrs),))] * 2
    out = _call(
        body, name=name,
        out_shape=(*sem_shapes, *[pltpu.HBM(a.shape, a.dtype) for a in flat], jax.ShapeDtypeStruct((8, LANES), F32)),
        in_specs=[HBM_SPEC] * n + [ANY_SPEC],
        out_specs=(*[SEM_SPEC] * (2 * nl), *[HBM_SPEC] * n, pl.BlockSpec(memory_space=pltpu.VMEM)),
        input_output_aliases={a: 2 * nl + a for a in range(n)},
        compiler_params=_cparams(has_side_effects=DATAFLOW),
    )(*[_hbm(a) for a in flat], after)
    per_layer, a0 = [], 0
    for li, arrs in enumerate(layers):
        per_layer.append((out[2 * li], out[2 * li + 1], list(out[2 * nl + a0:2 * nl + a0 + len(arrs)])))
        a0 += len(arrs)
    return per_layer, out[-1]


def _gather_wait(started, axes, after, name):
    send_sems, recv_sems, arrs = started
    n = len(arrs)

    def body(*refs):
        ins, send_ref, recv_ref = refs[:n], refs[n], refs[n + 1]
        x, y, c, me = _my_position()
        for k, (peer, idx) in enumerate(_peers(x, y, c)):
            for a in range(n):
                own, theirs = _own_and_peer_rows(ins[a], me, idx, axes[a])
                cp = pltpu.make_async_remote_copy(own, theirs, send_ref.at[k * n + a], recv_ref.at[k * n + a],
                                                  device_id=peer, device_id_type=MESH)
                cp.wait_send()
                cp.wait_recv()

    return _call(
        body, name=name,
        out_shape=tuple(pltpu.HBM(a.shape, a.dtype) for a in arrs),
        in_specs=[HBM_SPEC] * n + [SEM_SPEC, SEM_SPEC, ANY_SPEC],
        out_specs=tuple([HBM_SPEC] * n),
        input_output_aliases={a: a for a in range(n)},
        compiler_params=_cparams(has_side_effects=DATAFLOW),
    )(*arrs, send_sems, recv_sems, after)


def _first_relations(x, y, c):
    return [(x, y, 1 - c), (1 - x, y, c), (x, 1 - y, c), (1 - x, 1 - y, c)]


def _gather_first_start(arr, after):
    n_rel = 4

    def body(a_ref, after_ref, send_ref, recv_ref, thru, token):
        x, y, c, me = _my_position()
        rows, _ = _own_and_peer_rows(a_ref, me, me, 0)
        for k, peer in enumerate(_first_relations(x, y, c)):
            pltpu.make_async_remote_copy(rows, rows, send_ref.at[k], recv_ref.at[k], device_id=peer, device_id_type=MESH).start()
        token[...] = jnp.zeros_like(token)

    sem = pltpu.SemaphoreType.DMA((n_rel,))
    out = _call(
        body, name="gather_first_start",
        out_shape=(sem, sem, pltpu.HBM(arr.shape, arr.dtype), jax.ShapeDtypeStruct((8, LANES), F32)),
        in_specs=[HBM_SPEC, ANY_SPEC],
        out_specs=(SEM_SPEC, SEM_SPEC, HBM_SPEC, pl.BlockSpec(memory_space=pltpu.VMEM)),
        input_output_aliases={0: 2},
        compiler_params=_cparams(has_side_effects=DATAFLOW),
    )(_hbm(arr), after)
    return out[:3], out[3]


def _gather_first_forward(started, after):
    send_a, recv_a, arr = started

    def body(a_ref, send_a_ref, recv_a_ref, after_ref, send_b_ref, recv_b_ref, thru, token):
        x, y, c, me = _my_position()
        sibling = (x, y, 1 - c)
        for k, peer in enumerate(_first_relations(x, y, c)):
            own, theirs = _own_and_peer_rows(a_ref, me, 4 * peer[0] + 2 * peer[1] + peer[2], 0)
            cp = pltpu.make_async_remote_copy(own, theirs, send_a_ref.at[k], recv_a_ref.at[k], device_id=peer, device_id_type=MESH)
            cp.wait_send()
            cp.wait_recv()
            if k > 0:
                pltpu.make_async_remote_copy(theirs, theirs, send_b_ref.at[k - 1], recv_b_ref.at[k - 1],
                                             device_id=sibling, device_id_type=MESH).start()
        token[...] = jnp.zeros_like(token)

    sem = pltpu.SemaphoreType.DMA((3,))
    out = _call(
        body, name="gather_first_forward",
        out_shape=(sem, sem, pltpu.HBM(arr.shape, arr.dtype), jax.ShapeDtypeStruct((8, LANES), F32)),
        in_specs=[HBM_SPEC, SEM_SPEC, SEM_SPEC, ANY_SPEC],
        out_specs=(SEM_SPEC, SEM_SPEC, HBM_SPEC, pl.BlockSpec(memory_space=pltpu.VMEM)),
        input_output_aliases={0: 2},
        compiler_params=_cparams(has_side_effects=DATAFLOW),
    )(arr, send_a, recv_a, after)
    return out[:3], out[3]


def _gather_first_wait(forwarded, after):
    send_b, recv_b, arr = forwarded

    def body(a_ref, send_b_ref, recv_b_ref, after_ref, thru):
        x, y, c, me = _my_position()
        sibling = (x, y, 1 - c)
        for k, peer in enumerate(_first_relations(x, y, c)[1:]):
            _, sent = _own_and_peer_rows(a_ref, me, 4 * peer[0] + 2 * peer[1] + peer[2], 0)
            _, got = _own_and_peer_rows(a_ref, me, 4 * peer[0] + 2 * peer[1] + (1 - peer[2]), 0)
            cp = pltpu.make_async_remote_copy(sent, got, send_b_ref.at[k], recv_b_ref.at[k], device_id=sibling, device_id_type=MESH)
            cp.wait_send()
            cp.wait_recv()

    return _call(
        body, name="gather_first_wait",
        out_shape=pltpu.HBM(arr.shape, arr.dtype),
        in_specs=[HBM_SPEC, SEM_SPEC, SEM_SPEC, ANY_SPEC],
        out_specs=HBM_SPEC,
        input_output_aliases={0: 0},
        compiler_params=_cparams(has_side_effects=DATAFLOW),
    )(arr, send_b, recv_b, after)


def _scatter_start(fulls, axes, name, after):
    n = len(fulls)
    lands = []
    for f, ax in zip(fulls, axes):
        shp = list(f.shape)
        shp[ax] //= N_DEV
        lands.append(_hbm(lax.empty((N_DEV - 1,) + tuple(shp), f.dtype)))

    def body(*refs):
        srcs, dsts, send_ref, recv_ref, token = refs[:n], refs[n:2 * n], refs[2 * n + 1], refs[2 * n + 2], refs[-1]
        x, y, c, me = _my_position()
        for k, (peer, idx) in enumerate(_peers(x, y, c)):
            for a in range(n):
                _, theirs = _own_and_peer_rows(srcs[a], me, idx, axes[a])
                pltpu.make_async_remote_copy(theirs, dsts[a].at[k], send_ref.at[k * n + a], recv_ref.at[k * n + a],
                                             device_id=peer, device_id_type=MESH).start()
        token[...] = jnp.zeros_like(token)

    sem = pltpu.SemaphoreType.DMA(((N_DEV - 1) * n,))
    out = _call(
        body, name=name,
        out_shape=(sem, sem, *[pltpu.HBM(a.shape, a.dtype) for a in fulls], *[pltpu.HBM(a.shape, a.dtype) for a in lands],
                   jax.ShapeDtypeStruct((8, LANES), F32)),
        in_specs=[HBM_SPEC] * (2 * n) + [ANY_SPEC],
        out_specs=(SEM_SPEC, SEM_SPEC, *[HBM_SPEC] * (2 * n), pl.BlockSpec(memory_space=pltpu.VMEM)),
        input_output_aliases={a: 2 + a for a in range(2 * n)},
        compiler_params=_cparams(has_side_effects=DATAFLOW),
    )(*[_hbm(a) for a in fulls], *lands, after)
    return (out[0], out[1], list(out[2:2 + n]), list(out[2 + n:2 + 2 * n])), out[-1]


def _scatter_wait(started, axes, after, name):
    send_sems, recv_sems, fulls, lands = started
    n = len(fulls)

    def body(*refs):
        srcs, dsts, send_ref, recv_ref = refs[:n], refs[n:2 * n], refs[2 * n], refs[2 * n + 1]
        x, y, c, me = _my_position()
        for k, (peer, idx) in enumerate(_peers(x, y, c)):
            for a in range(n):
                _, theirs = _own_and_peer_rows(srcs[a], me, idx, axes[a])
                cp = pltpu.make_async_remote_copy(theirs, dsts[a].at[k], send_ref.at[k * n + a], recv_ref.at[k * n + a],
                                                  device_id=peer, device_id_type=MESH)
                cp.wait_send()
                cp.wait_recv()

    out = _call(
        body, name=name,
        out_shape=tuple(pltpu.HBM(a.shape, a.dtype) for a in (*fulls, *lands)),
        in_specs=[HBM_SPEC] * (2 * n) + [SEM_SPEC, SEM_SPEC, ANY_SPEC],
        out_specs=tuple([HBM_SPEC] * (2 * n)),
        input_output_aliases={a: a for a in range(2 * n)},
        compiler_params=_cparams(has_side_effects=DATAFLOW),
    )(*fulls, *lands, send_sems, recv_sems, after)
    return list(out[:n]), list(out[n:])


def _prep_weights(me, items, name):
    def body(me_ref, *refs):
        for (_, _, kind), src, dst in zip(items, refs[:len(items)], refs[len(items):]):
            dst[...] = (src[...].T if kind == "T" else src[...]).astype(BF16)

    ins, in_specs, out_shapes, out_specs = [], [], [], []
    for src, j, kind in items:
        shard = src.shape[1:]
        ins.append(src)
        in_specs.append(pl.BlockSpec((None,) + tuple(shard), lambda i, me_ref, j=j, nd=len(shard): (j,) + (0,) * nd))
        if kind == "G":
            out_shapes.append((shard[0], N_DEV * shard[1], shard[2]))
            out_specs.append(pl.BlockSpec(tuple(shard), lambda i, me_ref: (0, me_ref[0], 0)))
        else:
            rows = shard[1] if kind == "T" else shard[0]
            out_shapes.append((N_DEV * rows, D))
            out_specs.append(pl.BlockSpec((rows, D), lambda i, me_ref: (me_ref[0], 0)))
    out = _call(
        body, name=name,
        grid_spec=pltpu.PrefetchScalarGridSpec(num_scalar_prefetch=1, grid=(1,), in_specs=in_specs, out_specs=tuple(out_specs)),
        out_shape=tuple(jax.ShapeDtypeStruct(s, BF16) for s in out_shapes),
        compiler_params=_cparams(1),
    )(me.reshape(1), *ins)
    return list(out)


def _ada_forward(c_all, ada_w):
    cols = ada_w.shape[2]

    def body(c_ref, w_ref, o_ref):
        cv = c_ref[...]
        sc = (cv * _sigmoid(cv)).astype(BF16)
        o_ref[...] = _dot(sc, w_ref[...].astype(BF16))

    return _call(
        body, name="ada_forward", grid=(DEPTH,),
        out_shape=jax.ShapeDtypeStruct((DEPTH, N_DEV, cols), F32),
        in_specs=[pl.BlockSpec((N_DEV, D), lambda i: (0, 0)), pl.BlockSpec((None, D, cols), lambda i: (i, 0, 0))],
        out_specs=pl.BlockSpec((None, N_DEV, cols), lambda i: (i, 0, 0)),
        compiler_params=_cparams(1),
    )(c_all, ada_w)


def _ada_backward_adamw(c_pad, dmod_pad, w, m, v):
    cols = w.shape[2]

    def body(c_ref, dm_ref, w_ref, m_ref, v_ref, g_out, d_out, m_out, v_out):
        cv = c_ref[...]
        sc = (cv * _sigmoid(cv)).astype(BF16)
        g = _dot_tn(sc, dm_ref[...].astype(BF16))
        g_out[...] = g
        d_out[...], m_out[...], v_out[...] = _adamw(w_ref[...], g, m_ref[...], v_ref[...])

    wspec = pl.BlockSpec((None, D, cols), lambda i: (i, 0, 0))
    return _call(
        body, name="ada_backward_adamw", grid=(DEPTH,),
        out_shape=tuple(jax.ShapeDtypeStruct(w.shape, F32) for _ in range(4)),
        in_specs=[pl.BlockSpec((2 * N_DEV, D), lambda i: (0, 0)), pl.BlockSpec((None, 2 * N_DEV, cols), lambda i: (i, 0, 0)),
                  wspec, wspec, wspec],
        out_specs=(wspec, wspec, wspec, wspec),
        compiler_params=_cparams(1),
    )(c_pad, dmod_pad, w, m, v)


def _attn_in_proj(x, rope, rows, mod, layer, w_t, j, gain, bd, tile):
    seq = x.shape[0]

    def body(x_ref, rope_ref, ng_ref, mod_ref, w_ref, gain_ref, bd_ref, qk_ref, qs_ref, kd_ref, vd_ref, g_ref):
        _, _, h = _norm_mod(x_ref[...], ng_ref[...], mod_ref[1:2, :], mod_ref[0:1, :])
        hb = h.astype(BF16)
        tabs = _rope_tabs(rope_ref)
        low = _low_half(tile)
        bdm = bd_ref[...]

        def put_kv(ref, blk, first_kv):
            sw = pltpu.roll(blk, HEAD_DIM, 1)
            ref[:, LANES * first_kv:LANES * (first_kv + 1)] = jnp.where(low, blk, sw).astype(BF16)
            ref[:, LANES * (first_kv + 1):LANES * (first_kv + 2)] = jnp.where(low, sw, blk).astype(BF16)

        def project(c):
            return _dot_nt(hb, w_ref[CHUNK * c:CHUNK * (c + 1), :])

        n_chunks = ATTN_IN // CHUNK
        per = CHUNK // LANES
        nxt = project(0)
        for c in range(n_chunks):
            cur = nxt
            if c + 1 < n_chunks:
                nxt = project(c + 1)
            col = CHUNK * c
            if col >= QK_W + N_KV * HEAD_DIM:
                g_ref[:, col - QK_W - N_KV * HEAD_DIM:col - QK_W - N_KV * HEAD_DIM + CHUNK] = cur.astype(BF16)
            elif col >= QK_W:
                for t in range(per):
                    put_kv(vd_ref, cur[:, LANES * t:LANES * (t + 1)], (col - QK_W) // HEAD_DIM + 2 * t)
            else:
                qk_ref[:, col:col + CHUNK] = cur
                for t in range(per):
                    b = per * c + t
                    blk = cur[:, LANES * t:LANES * (t + 1)]
                    ms = _group_mean(blk * blk, bdm)
                    y = (blk * lax.rsqrt(ms + NORM_EPS)) * gain_ref[:, LANES * b:LANES * (b + 1)]
                    rp = _rope(y, tabs)
                    if b < D // LANES:
                        rp = rp * (HEAD_DIM ** -0.5)
                        qs_ref[:, 2 * LANES * b:2 * LANES * b + LANES] = jnp.where(low, rp, 0.0).astype(BF16)
                        qs_ref[:, 2 * LANES * b + LANES:2 * LANES * (b + 1)] = jnp.where(low, 0.0, rp).astype(BF16)
                    else:
                        put_kv(kd_ref, rp, 2 * (b - D // LANES))

    row = lambda w: pl.BlockSpec((tile, w), lambda i: (i, 0))
    return _call(
        body, name=f"attn_in_proj_{j}", grid=(seq // tile,),
        out_shape=(jax.ShapeDtypeStruct((seq, QK_W), F32), jax.ShapeDtypeStruct((seq, N_HEADS * LANES), BF16),
                   jax.ShapeDtypeStruct((seq, KX_W), BF16), jax.ShapeDtypeStruct((seq, KX_W), BF16),
                   jax.ShapeDtypeStruct((seq, D), BF16)),
        in_specs=[row(D), row(3 * LANES), _mod_row_spec(layer, NORM_ROW), _mod_spec(layer), _const_spec((ATTN_IN, D)),
                  _const_spec((1, QK_W)), _const_spec((LANES, LANES))],
        out_specs=(row(QK_W), row(N_HEADS * LANES), row(KX_W), row(KX_W), row(D)),
        compiler_params=_cparams(1),
    )(x, rope, rows, mod, w_t, gain, bd)


def _band_mask(n, rows, keys_on_rows):
    shape = (2 * QBLK, rows) if keys_on_rows else (rows, 2 * QBLK)
    qi = lax.broadcasted_iota(jnp.int32, shape, 1 if keys_on_rows else 0) & (QBLK - 1)
    kj = lax.broadcasted_iota(jnp.int32, shape, 0 if keys_on_rows else 1)
    diff = QBLK + qi - kj
    first_key = jnp.where(n > 0, 0, QBLK)
    return (diff >= 0) & (diff < QBLK) & (kj >= first_key)


def _stack_heads(ref, heads):
    return jnp.concatenate([ref[:, LANES * h:LANES * (h + 1)] for h in heads], axis=0)


def _kv_block(prev_ref, cur_ref, kv):
    cols = slice(LANES * kv, LANES * (kv + 1))
    return jnp.concatenate([prev_ref[:, cols], cur_ref[:, cols]], axis=0)


def _pair_up(st, low):
    return jnp.concatenate([jnp.where(low, st[0:QBLK], st[QBLK:2 * QBLK]),
                            jnp.where(low, st[2 * QBLK:3 * QBLK], st[3 * QBLK:4 * QBLK])], axis=1)


def _attn_forward(sinks, qs, kd, vd, j):
    seq = qs.shape[0]
    per = 2
    nb = seq // (per * QBLK)

    def body(sink_ref, q_ref, kp_ref, kc_ref, vp_ref, vc_ref, o_ref):
        n = pl.program_id(0)
        low = _low_half(QBLK)
        rowi = lax.broadcasted_iota(jnp.int32, (4 * QBLK, 1), 0)
        groups = per * N_KV

        def keys(p_ref, c_ref, blk, kv):
            cols = slice(LANES * kv, LANES * (kv + 1))
            if blk == 0:
                return jnp.concatenate([p_ref[:, cols], c_ref[0:QBLK, cols]], axis=0)
            return c_ref[QBLK * (blk - 1):QBLK * (blk + 1), cols]

        def scores(g):
            blk, kv = divmod(g, N_KV)
            q = jnp.concatenate([q_ref[QBLK * blk:QBLK * (blk + 1), LANES * h:LANES * (h + 1)]
                                 for h in range(4 * kv, 4 * kv + 4)], axis=0)
            return _dot_nt(q, keys(kp_ref, kc_ref, blk, kv))

        nxt = scores(0)
        for g in range(groups):
            blk, kv = divmod(g, N_KV)
            ok = _band_mask(n if blk == 0 else 1, 4 * QBLK, False)
            s = jnp.where(ok, nxt, -1e30)
            if g + 1 < groups:
                nxt = scores(g + 1)
            sink = jnp.where(rowi < QBLK, sink_ref[j, 4 * kv],
                             jnp.where(rowi < 2 * QBLK, sink_ref[j, 4 * kv + 1],
                                       jnp.where(rowi < 3 * QBLK, sink_ref[j, 4 * kv + 2], sink_ref[j, 4 * kv + 3])))
            m = jnp.maximum(jnp.max(s, axis=1, keepdims=True), sink)
            p = jnp.exp(s - m)
            den = jnp.sum(p, axis=1, keepdims=True) + jnp.exp(sink - m)
            o_st = _dot((p / den).astype(BF16), keys(vp_ref, vc_ref, blk, kv))
            o_ref[QBLK * blk:QBLK * (blk + 1), 2 * LANES * kv:2 * LANES * (kv + 1)] = _pair_up(o_st, low).astype(BF16)

    cur = lambda w: pl.BlockSpec((per * QBLK, w), lambda n: (n, 0))
    prev = lambda w: pl.BlockSpec((QBLK, w), lambda n: (jnp.maximum(per * n - 1, 0), 0))
    return _call(
        body, name=f"attn_forward_{j}", grid=(nb,),
        out_shape=jax.ShapeDtypeStruct((seq, D), BF16),
        in_specs=[pl.BlockSpec(memory_space=pltpu.SMEM), cur(N_HEADS * LANES), prev(KX_W), cur(KX_W), prev(KX_W), cur(KX_W)],
        out_specs=cur(D),
        compiler_params=_cparams(1),
    )(sinks, qs, kd, kd, vd, vd)


def _attn_out_proj(x, o, g, w, j, mod, layer, tile):
    seq = x.shape[0]

    def body(x_ref, o_ref, g_ref, w_ref, mod_ref, xo_ref, br_ref):
        gv = g_ref[...].astype(F32)
        u = (o_ref[...].astype(F32) * (gv * _sigmoid(gv))).astype(BF16)
        br = _dot(u, w_ref[...])
        br_ref[...] = br.astype(BF16)
        xo_ref[...] = x_ref[...] + mod_ref[2:3, :] * br

    row = pl.BlockSpec((tile, D), lambda i: (i, 0))
    return _call(
        body, name=f"attn_out_proj_{j}", grid=(seq // tile,),
        out_shape=(jax.ShapeDtypeStruct((seq, D), F32), jax.ShapeDtypeStruct((seq, D), BF16)),
        in_specs=[row, row, row, _const_spec((D, D)), _mod_spec(layer)],
        out_specs=(row, row),
        compiler_params=_cparams(1),
    )(x, o, g, w, mod)


def _attn_out_proj_bwd(dxn, br, o, g, w, j, mod, layer, tile, after):
    seq = dxn.shape[0]
    steps = seq // tile

    def body(dxn_ref, br_ref, o_ref, g_ref, w_ref, mod_ref, after_ref, do_ref, dg_ref, dw_ref, dgate_ref, dw_acc):
        i = pl.program_id(0)

        @pl.when(i == 0)
        def _():
            dw_acc[...] = jnp.zeros_like(dw_acc)
            dgate_ref[...] = jnp.zeros_like(dgate_ref)

        dxn_v, ov, gv = dxn_ref[...], o_ref[...].astype(F32), g_ref[...].astype(F32)
        dgate_ref[...] += jnp.sum(dxn_v * br_ref[...].astype(F32), axis=0, keepdims=True)
        dbr = (dxn_v * mod_ref[2:3, :]).astype(BF16)
        du = _dot_nt(dbr, w_ref[...])
        sg = _sigmoid(gv)
        sl = gv * sg
        dw_acc[...] += _dot_tn((ov * sl).astype(BF16), dbr)
        do = du * sl
        dg_ref[...] = (du * ov * (sg * (1.0 + gv * (1.0 - sg)))).astype(BF16)
        low = _low_half(tile)
        for b in range(D // LANES):
            blk = do[:, LANES * b:LANES * (b + 1)]
            do_ref[:, 2 * LANES * b:2 * LANES * b + LANES] = jnp.where(low, blk, 0.0).astype(BF16)
            do_ref[:, 2 * LANES * b + LANES:2 * LANES * (b + 1)] = jnp.where(low, 0.0, blk).astype(BF16)

        @pl.when(i == steps - 1)
        def _():
            dw_ref[...] = dw_acc[...].astype(BF16)

    row = lambda w_: pl.BlockSpec((tile, w_), lambda i: (i, 0))
    return _call(
        body, name=f"attn_out_proj_bwd_{j}", grid=(steps,),
        out_shape=(jax.ShapeDtypeStruct((seq, N_HEADS * LANES), BF16), jax.ShapeDtypeStruct((seq, D), BF16),
                   jax.ShapeDtypeStruct((D, D), BF16), jax.ShapeDtypeStruct((1, D), F32)),
        in_specs=[row(D), row(D), row(D), row(D), _const_spec((D, D)), _mod_spec(layer), ANY_SPEC],
        out_specs=(row(N_HEADS * LANES), row(D), pl.BlockSpec((D, D), lambda i: (0, 0)),
                   pl.BlockSpec((1, D), lambda i: (0, 0))),
        scratch_shapes=[pltpu.VMEM((D, D), F32)],
        compiler_params=_cparams(1),
    )(dxn, br, o, g, w, mod, after)


def _attn_backward(sinks, qs, dos, kd, vd, j, after):
    seq = qs.shape[0]
    nb = seq // QBLK

    def body(sink_ref, q_ref, do_ref, kp_ref, kc_ref, vp_ref, vc_ref, after_ref, dq_ref, dk_ref, dv_ref, dsink_ref,
             carry_k, carry_v, sink_acc):
        n = pl.program_id(0)

        @pl.when(n == 0)
        def _():
            carry_k[...] = jnp.zeros_like(carry_k)
            carry_v[...] = jnp.zeros_like(carry_v)
            sink_acc[...] = jnp.zeros_like(sink_acc)

        @pl.when(n < nb)
        def _():
            ok = _band_mask(n, 2 * QBLK, True)
            low = _low_half(QBLK)
            lane_q = lax.broadcasted_iota(jnp.int32, (1, 2 * QBLK), 1)
            dk_parts, dv_parts = [], []

            def first_products(g):
                kv, half = divmod(g, 2)
                heads = (4 * kv + half, 4 * kv + 2 + half)
                q = _stack_heads(q_ref, heads)
                do = _stack_heads(do_ref, heads)
                kk = _kv_block(kp_ref, kc_ref, kv)
                return heads, q, do, kk, _dot_nt(kk, q), _dot_nt(_kv_block(vp_ref, vc_ref, kv), do)

            nxt = first_products(0)
            dq_h, dk_kv, dv_kv = [], None, None
            for g in range(2 * N_KV):
                heads, q, do, kk, s_raw, dp_raw = nxt
                if g + 1 < 2 * N_KV:
                    nxt = first_products(g + 1)
                st = jnp.where(ok, s_raw, -1e30)
                sink = jnp.where(lane_q < QBLK, sink_ref[j, heads[0]], sink_ref[j, heads[1]])
                m = jnp.maximum(jnp.max(st, axis=0, keepdims=True), sink)
                e = jnp.exp(st - m)
                e_sink = jnp.exp(sink - m)
                inv = 1.0 / (jnp.sum(e, axis=0, keepdims=True) + e_sink)
                p = e * inv
                pdp = p * dp_raw
                delta = jnp.sum(pdp, axis=0, keepdims=True)
                ds = (pdp - p * delta).astype(BF16)
                sink_acc[g:g + 1, :] -= e_sink * inv * delta
                dk_g, dv_g = _dot(ds, q), _dot(p.astype(BF16), do)
                dk_kv = dk_g if dk_kv is None else dk_kv + dk_g
                dv_kv = dv_g if dv_kv is None else dv_kv + dv_g
                dq_h.append(_dot_tn(ds, kk))
                if g % 2 == 1:
                    kv = g // 2
                    for t in range(2):
                        dq_ref[:, LANES * (2 * kv + t):LANES * (2 * kv + t + 1)] = jnp.where(
                            low, dq_h[0][QBLK * t:QBLK * (t + 1)], dq_h[1][QBLK * t:QBLK * (t + 1)])
                    dk_parts.append(dk_kv + pltpu.roll(dk_kv, HEAD_DIM, 1))
                    dv_parts.append(dv_kv + pltpu.roll(dv_kv, HEAD_DIM, 1))
                    dq_h, dk_kv, dv_kv = [], None, None

            def order(parts, lo, hi):
                return jnp.concatenate([jnp.where(low, parts[0][lo:hi], parts[1][lo:hi]),
                                        jnp.where(low, parts[2][lo:hi], parts[3][lo:hi])], axis=1)

            dk_ref[...] = carry_k[...] + order(dk_parts, 0, QBLK)
            dv_ref[...] = (carry_v[...] + order(dv_parts, 0, QBLK)).astype(BF16)
            carry_k[...] = order(dk_parts, QBLK, 2 * QBLK)
            carry_v[...] = order(dv_parts, QBLK, 2 * QBLK)

        @pl.when(n == nb)
        def _():
            dk_ref[...] = carry_k[...]
            dv_ref[...] = carry_v[...].astype(BF16)
            lane = lax.broadcasted_iota(jnp.int32, (1, LANES), 1)
            out = jnp.zeros((1, LANES), F32)
            for g in range(2 * N_KV):
                for t in range(2):
                    tot = jnp.sum(sink_acc[g:g + 1, QBLK * t:QBLK * (t + 1)], axis=1, keepdims=True)
                    out = jnp.where(lane == 4 * (g // 2) + 2 * t + g % 2, tot, out)
            dsink_ref[...] = out

    cur = lambda w: pl.BlockSpec((QBLK, w), lambda n: (jnp.minimum(n, nb - 1), 0))
    prev = lambda w: pl.BlockSpec((QBLK, w), lambda n: (jnp.maximum(n - 1, 0), 0))
    return _call(
        body, name=f"attn_backward_{j}", grid=(nb + 1,),
        out_shape=(jax.ShapeDtypeStruct((seq, D), F32), jax.ShapeDtypeStruct((seq, N_KV * HEAD_DIM), F32),
                   jax.ShapeDtypeStruct((seq, N_KV * HEAD_DIM), BF16), jax.ShapeDtypeStruct((1, LANES), F32)),
        in_specs=[pl.BlockSpec(memory_space=pltpu.SMEM), cur(N_HEADS * LANES), cur(N_HEADS * LANES), prev(KX_W), cur(KX_W),
                  prev(KX_W), cur(KX_W), ANY_SPEC],
        out_specs=(cur(D), prev(N_KV * HEAD_DIM), prev(N_KV * HEAD_DIM), pl.BlockSpec((1, LANES), lambda n: (0, 0))),
        scratch_shapes=[pltpu.VMEM((QBLK, N_KV * HEAD_DIM), F32), pltpu.VMEM((QBLK, N_KV * HEAD_DIM), F32),
                        pltpu.VMEM((2 * N_KV, 2 * QBLK), F32)],
        compiler_params=_cparams(1),
    )(sinks, qs, dos, kd, kd, vd, vd, after)


def _in_proj_tail(x_ref, dxn_ref, ng_ref, mod_ref, w_ref, dproj, dx_ref, dw_acc, vec_acc):
    ng, sc, sh = ng_ref[...], mod_ref[1:2, :], mod_ref[0:1, :]
    xh, r, h = _norm_mod(x_ref[...], ng, sc, sh)
    dh = _dot(dproj, w_ref[...])
    dw_acc[...] += _dot_tn(dproj, h.astype(BF16))
    vec_acc[0:1, :] += jnp.sum(dh, axis=0, keepdims=True)
    vec_acc[1:2, :] += jnp.sum(dh * xh, axis=0, keepdims=True)
    dxh = dh * (ng * (1.0 + sc))
    dx_ref[...] = dxn_ref[...] + r * (dxh - xh * jnp.mean(dxh * xh, axis=-1, keepdims=True))


def _tail_finish(ng_ref, mod_ref, dw_ref, vec_ref, dw_acc, vec_acc):
    dw_ref[...] = dw_acc[...].astype(BF16)
    a = vec_acc[1:2, :]
    vec_ref[...] = jnp.zeros_like(vec_ref)
    vec_ref[0:1, :] = vec_acc[0:1, :]
    vec_ref[1:2, :] = a * ng_ref[...]
    vec_ref[3:4, :] = a * (1.0 + mod_ref[1:2, :])


def _attn_in_proj_bwd(x, dxn, rope, qk_raw, dq, dk, dv, dg, rows, mod, layer, w_t, j, gain, bd, tile):
    seq = x.shape[0]
    steps = seq // tile

    def body(x_ref, dxn_ref, rope_ref, qk_ref, dq_ref, dk_ref, dv_ref, dg_ref, ng_ref, mod_ref, w_ref, gain_ref,
             bd_ref, dx_ref, dw_ref, vec_ref, dgain_ref, dproj, dw_acc, vec_acc):
        i = pl.program_id(0)

        @pl.when(i == 0)
        def _():
            dw_acc[...] = jnp.zeros_like(dw_acc)
            vec_acc[...] = jnp.zeros_like(vec_acc)
            dgain_ref[...] = jnp.zeros_like(dgain_ref)

        tabs = _rope_tabs(rope_ref)
        bdm = bd_ref[...]
        for b in range(QK_W // LANES):
            cols = slice(LANES * b, LANES * (b + 1))
            raw = qk_ref[:, cols]
            if b < D // LANES:
                dy = dq_ref[:, cols] * (HEAD_DIM ** -0.5)
            else:
                dy = dk_ref[:, LANES * (b - D // LANES):LANES * (b + 1 - D // LANES)]
            dy = _rope_bwd(dy, tabs)
            rr = lax.rsqrt(_group_mean(raw * raw, bdm) + NORM_EPS)
            xh = raw * rr
            dgain_ref[:, cols] += jnp.sum(dy * xh, axis=0, keepdims=True)
            dxh = dy * gain_ref[:, cols]
            dproj[:, cols] = (rr * (dxh - xh * _group_mean(dxh * xh, bdm))).astype(BF16)
        dproj[:, QK_W:QK_W + N_KV * HEAD_DIM] = dv_ref[...]
        dproj[:, QK_W + N_KV * HEAD_DIM:] = dg_ref[...]
        _in_proj_tail(x_ref, dxn_ref, ng_ref, mod_ref, w_ref, dproj[...], dx_ref, dw_acc, vec_acc)

        @pl.when(i == steps - 1)
        def _():
            _tail_finish(ng_ref, mod_ref, dw_ref, vec_ref, dw_acc, vec_acc)

    row = lambda w, dt=None: pl.BlockSpec((tile, w), lambda i: (i, 0))
    fixed = lambda shape: pl.BlockSpec(shape, lambda i: (0,) * len(shape))
    return _call(
        body, name=f"attn_in_proj_bwd_{j}", grid=(steps,),
        out_shape=(jax.ShapeDtypeStruct((seq, D), F32), jax.ShapeDtypeStruct((ATTN_IN, D), BF16),
                   jax.ShapeDtypeStruct((8, D), F32), jax.ShapeDtypeStruct((1, QK_W), F32)),
        in_specs=[row(D), row(D), row(3 * LANES), row(QK_W), row(D), row(N_KV * HEAD_DIM), row(N_KV * HEAD_DIM), row(D),
                  _mod_row_spec(layer, NORM_ROW), _mod_spec(layer), _const_spec((ATTN_IN, D)), _const_spec((1, QK_W)),
                  _const_spec((LANES, LANES))],
        out_specs=(row(D), fixed((ATTN_IN, D)), fixed((8, D)), fixed((1, QK_W))),
        scratch_shapes=[pltpu.VMEM((tile, ATTN_IN), BF16), pltpu.VMEM((ATTN_IN, D), F32), pltpu.VMEM((8, D), F32)],
        compiler_params=_cparams(1),
    )(x, dxn, rope, qk_raw, dq, dk, dv, dg, rows, mod, w_t, gain, bd)


def _pool_in_proj(x, rows, mod, layer, w_t, j, tile):
    seq = x.shape[0]

    def body(x_ref, ng_ref, mod_ref, w_ref, v_ref, g_ref):
        _, _, h = _norm_mod(x_ref[...], ng_ref[...], mod_ref[1:2, :], mod_ref[0:1, :])
        proj = _dot_nt(h.astype(BF16), w_ref[...])
        v_ref[...] = proj[:, :D].astype(BF16)
        g_ref[...] = proj[:, D:].astype(BF16)

    row = pl.BlockSpec((tile, D), lambda i: (i, 0))
    return _call(
        body, name=f"pool_in_proj_{j}", grid=(seq // tile,),
        out_shape=(jax.ShapeDtypeStruct((seq, D), BF16), jax.ShapeDtypeStruct((seq, D), BF16)),
        in_specs=[row, _mod_row_spec(layer, NORM_ROW), _mod_spec(layer), _const_spec((POOL_IN, D))],
        out_specs=(row, row),
        compiler_params=_cparams(1),
    )(x, rows, mod, w_t)


PAD = 8


def _window_sums(ext, lo, hi, forward):
    gw = D // len(POOL_WINDOWS)
    planes = []
    for gi, w in enumerate(POOL_WINDOWS):
        cols = slice(gw * gi, gw * (gi + 1))
        src, k = 0, 1
        while k < w:
            d = k if forward else -k
            ext[1 - src, lo:hi, cols] = ext[src, lo:hi, cols] + ext[src, lo + d:hi + d, cols]
            src, k = 1 - src, 2 * k
        planes.append(src)
    return planes


def _pooled(ext, v_ref, first, tile):
    t_abs = first + lax.broadcasted_iota(jnp.int32, (tile, 1), 0)
    top = PAD + HALO
    planes = _window_sums(ext, PAD, top + tile, False)
    outs = []
    gw = D // len(POOL_WINDOWS)
    for gi, w in enumerate(POOL_WINDOWS):
        cols = slice(gw * gi, gw * (gi + 1))
        cnt = jnp.minimum(t_abs + 1, w).astype(F32)
        outs.append(ext[planes[gi], top:top + tile, cols] / cnt - v_ref[:, cols].astype(F32))
    return jnp.concatenate(outs, axis=1)


def _fill_ext(ext, halo_ref, v_ref, i, tile):
    ext[0, 0:PAD, :] = jnp.zeros((PAD, D), F32)
    ext[1, 0:PAD, :] = jnp.zeros((PAD, D), F32)
    ext[0, PAD:PAD + HALO, :] = jnp.where(i == 0, 0.0, halo_ref[...].astype(F32))
    ext[0, PAD + HALO:PAD + HALO + tile, :] = v_ref[...].astype(F32)


def _group_mix(pb, wg_ref):
    gw = D // len(POOL_WINDOWS)
    return jnp.concatenate([_dot(pb[:, gw * gi:gw * (gi + 1)], wg_ref[gi]) for gi in range(len(POOL_WINDOWS))], axis=1)


def _pool_mix_out(x, v, g, wg, w_out, j, rows, mod, layer, tile, target=None):
    seq = x.shape[0]

    def body(*refs):
        if target is None:
            x_ref, v_ref, halo_ref, g_ref, wg_ref, w_ref, scale_ref, mod_ref, xo_ref, br_ref, ext = refs
        else:
            x_ref, v_ref, halo_ref, g_ref, wg_ref, w_ref, scale_ref, mod_ref, t_ref, xo_ref, br_ref, loss_ref, ext = refs
        i = pl.program_id(0)
        _fill_ext(ext, halo_ref, v_ref, i, tile)
        pb = _pooled(ext, v_ref, i * tile, tile).astype(BF16)
        ms = _group_mix(pb, wg_ref) * scale_ref[...]
        gv = g_ref[...].astype(F32)
        u = (ms * (gv * _sigmoid(gv))).astype(BF16)
        br = _dot(u, w_ref[...])
        br_ref[...] = br.astype(BF16)
        y = x_ref[...] + mod_ref[2:3, :] * br
        if target is None:
            xo_ref[...] = y
        else:
            @pl.when(i == 0)
            def _():
                loss_ref[...] = jnp.zeros_like(loss_ref)

            e = y - t_ref[...]
            xo_ref[...] = e * (1.0 / D)
            loss_ref[...] += 0.5 * jnp.sum(jnp.mean(e * e, axis=-1, keepdims=True), axis=0, keepdims=True)

    row = pl.BlockSpec((tile, D), lambda i: (i, 0))
    halo = pl.BlockSpec((HALO, D), lambda i: (jnp.maximum(i * (tile // HALO) - 1, 0), 0))
    extra_in, extra_out, extra_shape = ([], (), ()) if target is None else (
        [row], (pl.BlockSpec((1, LANES), lambda i: (0, 0)),), (jax.ShapeDtypeStruct((1, LANES), F32),))
    return _call(
        body, name=f"pool_mix_out_{j}", grid=(seq // tile,),
        out_shape=(jax.ShapeDtypeStruct((seq, D), F32), jax.ShapeDtypeStruct((seq, D), BF16)) + extra_shape,
        in_specs=[row, row, halo, row, _const_spec(wg.shape), _const_spec((D, D)), _mod_row_spec(layer, POOL_SCALE_ROW),
                  _mod_spec(layer)] + extra_in,
        out_specs=(row, row) + extra_out,
        scratch_shapes=[pltpu.VMEM((2, tile + HALO + PAD, D), F32)],
        compiler_params=_cparams(1),
    )(x, v, v, g, wg, w_out, rows, mod, *(() if target is None else (target,)))


def _pool_mix_out_bwd(dxn, br, v, g, wg, w_out, j, rows, mod, layer, tile, after):
    seq = dxn.shape[0]
    steps = seq // tile
    ng_ = len(POOL_WINDOWS)
    gw = D // ng_

    def body(dxn_ref, br_ref, v_ref, halo_ref, g_ref, wg_ref, w_ref, scale_ref, mod_ref, after_ref,
             dpool_ref, dg_ref, dw_ref, dwg_ref, vec_ref, ext, dw_acc, dwg_acc):
        i = pl.program_id(0)

        @pl.when(i == 0)
        def _():
            dw_acc[...] = jnp.zeros_like(dw_acc)
            dwg_acc[...] = jnp.zeros_like(dwg_acc)
            vec_ref[...] = jnp.zeros_like(vec_ref)

        _fill_ext(ext, halo_ref, v_ref, i, tile)
        pb = _pooled(ext, v_ref, i * tile, tile).astype(BF16)
        mixed = _group_mix(pb, wg_ref)
        scale = scale_ref[...]
        ms = mixed * scale
        gv, dxn_v = g_ref[...].astype(F32), dxn_ref[...]
        sg = _sigmoid(gv)
        sl = gv * sg
        vec_ref[0:1, :] += jnp.sum(dxn_v * br_ref[...].astype(F32), axis=0, keepdims=True)
        dbr = (dxn_v * mod_ref[2:3, :]).astype(BF16)
        du = _dot_nt(dbr, w_ref[...])
        dw_acc[...] += _dot_tn((ms * sl).astype(BF16), dbr)
        dms = du * sl
        dg_ref[...] = (du * ms * (sg * (1.0 + gv * (1.0 - sg)))).astype(BF16)
        vec_ref[1:2, :] += jnp.sum(dms * mixed, axis=0, keepdims=True)
        dmx = (dms * scale).astype(BF16)
        for gi in range(ng_):
            cols = slice(gw * gi, gw * (gi + 1))
            dpool_ref[:, cols] = _dot_nt(dmx[:, cols], wg_ref[gi])
            dwg_acc[gi] += _dot_tn(pb[:, cols], dmx[:, cols])

        @pl.when(i == steps - 1)
        def _():
            dw_ref[...] = dw_acc[...].astype(BF16)
            dwg_ref[...] = dwg_acc[...].astype(BF16)

    row = pl.BlockSpec((tile, D), lambda i: (i, 0))
    halo = pl.BlockSpec((HALO, D), lambda i: (jnp.maximum(i * (tile // HALO) - 1, 0), 0))
    fixed = lambda shape: pl.BlockSpec(shape, lambda i: (0,) * len(shape))
    return _call(
        body, name=f"pool_mix_out_bwd_{j}", grid=(steps,),
        out_shape=(jax.ShapeDtypeStruct((seq, D), F32), jax.ShapeDtypeStruct((seq, D), BF16),
                   jax.ShapeDtypeStruct((D, D), BF16), jax.ShapeDtypeStruct((ng_, gw, gw), BF16),
                   jax.ShapeDtypeStruct((8, D), F32)),
        in_specs=[row, row, row, halo, row, _const_spec(wg.shape), _const_spec((D, D)), _mod_row_spec(layer, POOL_SCALE_ROW),
                  _mod_spec(layer), ANY_SPEC],
        out_specs=(row, row, fixed((D, D)), fixed((ng_, gw, gw)), fixed((8, D))),
        scratch_shapes=[pltpu.VMEM((2, tile + HALO + PAD, D), F32), pltpu.VMEM((D, D), F32), pltpu.VMEM((ng_, gw, gw), F32)],
        compiler_params=_cparams(1),
    )(dxn, br, v, v, g, wg, w_out, rows, mod, after)


def _pool_in_proj_bwd(x, dxn, dpool, dg, rows, mod, layer, w_t, j, tile, after):
    seq = x.shape[0]
    steps = seq // tile
    gw = D // len(POOL_WINDOWS)

    def body(x_ref, dxn_ref, dp_ref, halo_ref, dg_ref, ng_ref, mod_ref, w_ref, after_ref, dx_ref, dw_ref, vec_ref,
             ext, dproj, dw_acc, vec_acc):
        i = pl.program_id(0)

        @pl.when(i == 0)
        def _():
            dw_acc[...] = jnp.zeros_like(dw_acc)
            vec_acc[...] = jnp.zeros_like(vec_acc)

        t_abs = i * tile + lax.broadcasted_iota(jnp.int32, (tile, 1), 0)
        last = i == steps - 1
        ext[0, tile + HALO:tile + HALO + PAD, :] = jnp.zeros((PAD, D), F32)
        ext[1, tile + HALO:tile + HALO + PAD, :] = jnp.zeros((PAD, D), F32)
        for gi, w in enumerate(POOL_WINDOWS):
            cols = slice(gw * gi, gw * (gi + 1))
            cnt = jnp.minimum(t_abs + 1, w).astype(F32)
            ext[0, 0:tile, cols] = dp_ref[:, cols] / cnt
            ext[0, tile:tile + HALO, cols] = jnp.where(last, 0.0, halo_ref[:, cols] * (1.0 / w))
        planes = _window_sums(ext, 0, tile + HALO, True)
        for gi, w in enumerate(POOL_WINDOWS):
            cols = slice(gw * gi, gw * (gi + 1))
            dproj[:, cols] = (ext[planes[gi], 0:tile, cols] - dp_ref[:, cols]).astype(BF16)
        dproj[:, D:] = dg_ref[...]
        _in_proj_tail(x_ref, dxn_ref, ng_ref, mod_ref, w_ref, dproj[...], dx_ref, dw_acc, vec_acc)

        @pl.when(last)
        def _():
            _tail_finish(ng_ref, mod_ref, dw_ref, vec_ref, dw_acc, vec_acc)

    row = pl.BlockSpec((tile, D), lambda i: (i, 0))
    halo = pl.BlockSpec((HALO, D), lambda i: (jnp.minimum((i + 1) * (tile // HALO), seq // HALO - 1), 0))
    fixed = lambda shape: pl.BlockSpec(shape, lambda i: (0,) * len(shape))
    return _call(
        body, name=f"pool_in_proj_bwd_{j}", grid=(steps,),
        out_shape=(jax.ShapeDtypeStruct((seq, D), F32), jax.ShapeDtypeStruct((POOL_IN, D), BF16),
                   jax.ShapeDtypeStruct((8, D), F32)),
        in_specs=[row, row, row, halo, row, _mod_row_spec(layer, NORM_ROW), _mod_spec(layer), _const_spec((POOL_IN, D)), ANY_SPEC],
        out_specs=(row, fixed((POOL_IN, D)), fixed((8, D))),
        scratch_shapes=[pltpu.VMEM((2, tile + HALO + PAD, D), F32), pltpu.VMEM((tile, POOL_IN), BF16), pltpu.VMEM((POOL_IN, D), F32),
                        pltpu.VMEM((8, D), F32)],
        compiler_params=_cparams(1),
    )(x, dxn, dpool, dpool, dg, rows, mod, w_t, after)


def _build_vec(vecs, gates, pool_vecs, gains, dsinks, loss_part):
    def body(v0, v1, v2, v3, g0, g2, p0, p1, n0, n1, s0, s1, loss_ref, out):
        out[...] = jnp.zeros_like(out)
        for i, v in enumerate((v0, v1, v2, v3)):
            out[3 * i:3 * i + 2, :] = v[0:2, :]
            out[12 + i:13 + i, :] = v[3:4, :]
        out[2:3, :] = g0[...]
        out[8:9, :] = g2[...]
        for j, (p, n, s) in enumerate(((p0, n0, s0), (p1, n1, s1))):
            out[3 * (2 * j + 1) + 2:3 * (2 * j + 1) + 3, :] = p[0:1, :]
            out[22 + j:23 + j, :] = p[1:2, :]
            out[16 + j:17 + j, :] = n[:, 0:D]
            out[18 + j:19 + j, 0:QK_W - D] = n[:, D:QK_W]
            out[20 + j:21 + j, 0:LANES] = s[...]
        out[24:25, 0:LANES] = loss_ref[...]

    vm = pl.BlockSpec(memory_space=pltpu.VMEM)
    args = (*vecs, gates[0], gates[2], *pool_vecs, *gains, *dsinks, loss_part)
    return _call(
        body, name="build_vec",
        out_shape=jax.ShapeDtypeStruct((VEC_ROWS, D), F32),
        in_specs=[vm] * len(args), out_specs=vm,
        compiler_params=_cparams(),
    )(*args)


def _sum_devices(g, after):
    rows = g.shape[1]

    def body(g_ref, after_ref, tot_ref, fold_ref):
        tot = g_ref[0]
        for p in range(1, N_DEV):
            tot = tot + g_ref[p]
        tot_ref[...] = tot
        f = tot[16:24, 0:LANES]
        for b in range(1, D // LANES):
            f = f + tot[16:24, LANES * b:LANES * (b + 1)]
        fold_ref[...] = f + pltpu.roll(f, HEAD_DIM, 1)

    return _call(
        body, name="sum_devices",
        out_shape=(jax.ShapeDtypeStruct((rows, D), F32), jax.ShapeDtypeStruct((8, LANES), F32)),
        in_specs=[pl.BlockSpec(memory_space=pltpu.VMEM), ANY_SPEC],
        out_specs=(pl.BlockSpec(memory_space=pltpu.VMEM), pl.BlockSpec(memory_space=pltpu.VMEM)),
        compiler_params=_cparams(),
    )(g, after)


def _adamw_small(params):
    n = len(params)

    def body(*refs):
        ins, outs = refs[:4 * n], refs[4 * n:]
        for p in range(n):
            w_ref, g_ref, m_ref, v_ref = ins[4 * p:4 * p + 4]
            outs[3 * p][...], outs[3 * p + 1][...], outs[3 * p + 2][...] = _adamw(w_ref[...], g_ref[...], m_ref[...], v_ref[...])

    vm = pl.BlockSpec(memory_space=pltpu.VMEM)
    out = _call(
        body, name="adamw_small",
        out_shape=tuple(jax.ShapeDtypeStruct(w.shape, F32) for (w, _, _, _) in params for _ in range(3)),
        in_specs=[vm] * (4 * n), out_specs=tuple([vm] * (3 * n)),
        compiler_params=_cparams(),
    )(*[a for p in params for a in p])
    return [tuple(out[3 * p:3 * p + 3]) for p in range(n)]


def _adamw_shards(name, me, fulls, lands, w, m, v, transpose, axis=0):
    nl = w.shape[0]
    wshape = w.shape[1:]
    own_shape = lands[0].shape[1:]

    def body(me_ref, *refs):
        own_refs, land_refs = refs[:nl], refs[nl:2 * nl]
        w_ref, m_ref, v_ref, g_out, d_out, m_out, v_out = refs[2 * nl:]
        layer = pl.program_id(0)
        for l in range(nl):
            @pl.when(layer == l)
            def _(l=l):
                g = own_refs[l][...].astype(F32)
                for k in range(N_DEV - 1):
                    g = g + land_refs[l][k].astype(F32)
                if transpose:
                    g = g.T
                g_out[...] = g
                d_out[...], m_out[...], v_out[...] = _adamw(w_ref[...], g, m_ref[...], v_ref[...])

    def own_index(l_, me_ref):
        idx = [0] * len(own_shape)
        idx[axis] = me_ref[0]
        return tuple(idx)

    own_spec = pl.BlockSpec(tuple(own_shape), own_index)
    land_spec = pl.BlockSpec((N_DEV - 1,) + tuple(own_shape), lambda l_, me_ref: (0,) * (1 + len(own_shape)))
    wspec = pl.BlockSpec((None,) + tuple(wshape), lambda l_, me_ref: (l_,) + (0,) * len(wshape))
    return _call(
        body, name=name,
        grid_spec=pltpu.PrefetchScalarGridSpec(num_scalar_prefetch=1, grid=(nl,),
                                               in_specs=[own_spec] * nl + [land_spec] * nl + [wspec] * 3,
                                               out_specs=(wspec,) * 4),
        out_shape=tuple(jax.ShapeDtypeStruct(w.shape, F32) for _ in range(4)),
        compiler_params=_cparams(1),
    )(me.reshape(1), *fulls, *lands, w, m, v)


def _constants():
    lane = np.arange(LANES)
    bd = (lane[:, None] // HEAD_DIM == lane[None, :] // HEAD_DIM).astype(np.float32)
    half = ROT_DIM // 2
    inv_freq = ROPE_THETA ** (-jnp.arange(half, dtype=F32) * 2.0 / ROT_DIM)
    invf = jnp.tile(inv_freq, LANES // half).reshape(1, LANES)
    return jnp.asarray(bd, BF16), invf


def kernel(x, c, positions, ada_w, ada_b, norm_g, attn_w_in, attn_q_norm, attn_k_norm, attn_sinks, attn_w_out, pool_w_in, pool_w_group, pool_scale, pool_w_out, loss_target, m_ada_w, m_ada_b, m_norm_g, m_attn_w_in, m_attn_q_norm, m_attn_k_norm, m_attn_sinks, m_attn_w_out, m_pool_w_in, m_pool_w_group, m_pool_scale, m_pool_w_out, v_ada_w, v_ada_b, v_norm_g, v_attn_w_in, v_attn_q_norm, v_attn_k_norm, v_attn_sinks, v_attn_w_out, v_pool_w_in, v_pool_w_group, v_pool_scale, v_pool_w_out):
    seq = x.shape[1]
    me = 4 * lax.axis_index("x") + 2 * lax.axis_index("y") + lax.axis_index("c")
    bd, invf = _constants()
    t_mm = min(512, seq)
    rope = _rope_table(positions.reshape(seq, 1), invf, t_mm)
    t_bw = min(256, seq)
    shard = pool_scale.shape[1]
    cols = ada_w.shape[2]

    w_in_rows = jnp.swapaxes(attn_w_in, 1, 2)
    w_first, = _prep_weights(me, [(w_in_rows, 0, "N")], "prep_first")
    first_w, token = _gather_first_start(w_first, c)
    prepped = _prep_weights(me, [(attn_w_out, 0, "N"), (pool_w_in, 0, "T"), (pool_w_out, 0, "N"), (pool_w_group, 0, "G"),
                                 (w_in_rows, 1, "N"), (attn_w_out, 1, "N"), (pool_w_in, 1, "T"), (pool_w_out, 1, "N"),
                                 (pool_w_group, 1, "G")], "prep_rest")

    first = jnp.concatenate([c, jnp.pad(pool_scale, ((0, 0), (0, D - shard))), jnp.zeros((5, D), F32)], axis=0)
    first = _allgather_small(first + token[0, 0], "allgather_c", rope)
    c_all = first[:, 0, :]
    scale_full = jnp.transpose(first[:, 1:3, :shard], (1, 0, 2)).reshape(2, D)
    mod_part = _ada_forward(c_all, ada_w)
    mod_all = _allgather_small(mod_part.reshape(DEPTH * N_DEV, cols), "allgather_mod", prepped[0])
    mod_all = mod_all.reshape(N_DEV, DEPTH, N_DEV, cols)
    mine = lax.dynamic_index_in_dim(mod_all, me, axis=2, keepdims=False)
    mod = jnp.transpose(mine, (1, 0, 2)).reshape(DEPTH, 3 * D) + ada_b
    pool_rows = jnp.stack([jnp.zeros_like(scale_full[0]), scale_full[0], jnp.zeros_like(scale_full[0]), scale_full[1]])
    mod = jnp.concatenate([mod.reshape(DEPTH, 3, D), norm_g[:, None, :], pool_rows[:, None, :],
                           jnp.zeros((DEPTH, 3, D), F32)], axis=1)
    rows = mod.reshape(DEPTH, 8, 1, D)

    groups = [prepped[0:1], prepped[1:4], prepped[4:6], prepped[6:9]]
    gaxes = [(0,), (0,), (0, 0, 1), (0, 0), (0, 0, 1)]
    first_w, token = _gather_first_forward(first_w, mod)
    rest, token = _gather_start(groups, gaxes[1:], token, "gather_start_rest")
    started = [None] + rest

    saved, weights = [], []
    h = x[0]
    for i in range(DEPTH):
        j = i // 2
        s = dict(x=h)
        if i == 0:
            w_in_t = _gather_first_wait(first_w, token)
        else:
            wts = _gather_wait(started[i + 1], gaxes[i + 1], h, f"gather_wait_{i}")
        if i % 2 == 0:
            if i > 0:
                w_in_t, w_out = wts
            s["gain"] = jnp.concatenate([jnp.tile(attn_q_norm[j], N_HEADS), jnp.tile(attn_k_norm[j], N_KV)]).reshape(1, QK_W)
            s["qk_raw"], s["qs"], s["kd"], s["vd"], s["g"] = _attn_in_proj(
                h, rope, rows, mod, i, w_in_t, j, s["gain"], bd, t_mm)
            s["o"] = _attn_forward(attn_sinks, s["qs"], s["kd"], s["vd"], j)
            if i == 0:
                w_out, = _gather_wait(started[1], gaxes[1], s["o"], "gather_wait_0_out")
            h, s["br"] = _attn_out_proj(h, s["o"], s["g"], w_out, j, mod, i, t_mm)
            weights.append((w_in_t, w_out))
        else:
            p_in_t, p_out, p_grp = wts
            s["v"], s["g"] = _pool_in_proj(h, rows, mod, i, p_in_t, j, t_mm)
            if i < DEPTH - 1:
                h, s["br"] = _pool_mix_out(h, s["v"], s["g"], p_grp, p_out, j, rows, mod, i, t_mm)
            else:
                dx, s["br"], loss_part = _pool_mix_out(h, s["v"], s["g"], p_grp, p_out, j, rows, mod, i, t_mm,
                                                       loss_target[0])
            weights.append(wts)
        saved.append(s)

    vecs, gates, gains, dsinks, pool_vecs = [None] * DEPTH, [None] * DEPTH, [None] * 2, [None] * 2, [None] * 2
    sent = {}
    token = jnp.zeros((8, LANES), F32)
    for i in reversed(range(DEPTH)):
        j = i // 2
        s = saved[i]
        if i % 2 == 0:
            w_in_t, w_out = weights[i]
            dos, dg, d_w_out, gates[i] = _attn_out_proj_bwd(dx, s["br"], s["o"], s["g"], w_out, j, mod, i, t_mm, token)
            if i == 0:
                sent["0_out"], token = _scatter_start([d_w_out], (0,), "scatter_start_0_out", token)
            dq, dk, dv, dsinks[j] = _attn_backward(attn_sinks, s["qs"], dos, s["kd"], s["vd"], j, token)
            dx, d_in_t, vecs[i], gains[j] = _attn_in_proj_bwd(
                s["x"], dx, rope, s["qk_raw"], dq, dk, dv, dg, rows, mod, i, w_in_t, j, s["gain"], bd, t_bw)
            if i > 0:
                sent[i], token = _scatter_start([d_in_t, d_w_out], (0, 0), f"scatter_start_{i}", token)
        else:
            p_in_t, p_out, p_grp = weights[i]
            dpool, dg, d_p_out, d_p_grp, pool_vecs[j] = _pool_mix_out_bwd(
                dx, s["br"], s["v"], s["g"], p_grp, p_out, j, rows, mod, i, t_mm, token)
            dx, d_in_t, vecs[i] = _pool_in_proj_bwd(s["x"], dx, dpool, dg, rows, mod, i, p_in_t, j, t_bw, token)
            sent[i], token = _scatter_start([d_in_t, d_p_out, d_p_grp], (0, 0, 1), f"scatter_start_{i}", token)

    vec = _build_vec(vecs, gates, pool_vecs, gains, dsinks, loss_part)
    vec_rows = lax.dynamic_update_slice(jnp.zeros((N_DEV * VEC_ROWS, D), F32), vec, (me * VEC_ROWS, 0))
    vec_sent, token = _gather_start([[vec_rows]], [(0,)], loss_part, "vec_gather_start")
    sent["0_in"], token = _scatter_start([d_in_t], (0,), "scatter_start_0_in", token)

    got = {}
    for i in (3, 1):
        fulls, lands = _scatter_wait(sent[i], (0, 0, 1), token, f"scatter_wait_{i}")
        got[i] = dict(zip(("in", "out", "grp"), zip(fulls, lands)))
    pick = lambda ls, kind: ([got[i][kind][0] for i in ls], [got[i][kind][1] for i in ls])
    res = {}
    res["pool_w_in"] = _adamw_shards("adamw_pool_w_in", me, *pick((1, 3), "in"), pool_w_in, m_pool_w_in, v_pool_w_in, True)
    res["pool_w_out"] = _adamw_shards("adamw_pool_w_out", me, *pick((1, 3), "out"), pool_w_out, m_pool_w_out,
                                      v_pool_w_out, False)
    res["pool_w_group"] = _adamw_shards("adamw_pool_w_group", me, *pick((1, 3), "grp"), pool_w_group, m_pool_w_group,
                                        v_pool_w_group, False, axis=1)

    vec_all, = _gather_wait(vec_sent[0], (0,), res["pool_w_group"][0], "vec_gather_wait")
    vec_all = vec_all.reshape(N_DEV, VEC_ROWS, D)
    tot, folded = _sum_devices(vec_all, token)
    loss = tot[24, 0]
    small = dict(
        ada_b=(ada_b, tot[0:12].reshape(DEPTH, 3 * D), m_ada_b, v_ada_b),
        norm_g=(norm_g, tot[12:16], m_norm_g, v_norm_g),
        q_norm=(attn_q_norm, folded[0:2, :HEAD_DIM], m_attn_q_norm, v_attn_q_norm),
        k_norm=(attn_k_norm, folded[2:4, :HEAD_DIM], m_attn_k_norm, v_attn_k_norm),
        sinks=(attn_sinks, tot[20:22, :N_HEADS], m_attn_sinks, v_attn_sinks),
        pool_scale=(pool_scale, lax.dynamic_slice(tot, (22, me * shard), (2, shard)), m_pool_scale, v_pool_scale),
    )
    res.update({k: (a[1],) + upd for (k, a), upd in zip(small.items(), _adamw_small(list(small.values())))})

    dmod_all = vec_all[:, 0:12, :].reshape(N_DEV, DEPTH, 3 * D)
    dmod_mine = lax.dynamic_slice_in_dim(dmod_all, me * cols, cols, axis=2)
    dmod_mine = jnp.pad(jnp.transpose(dmod_mine, (1, 0, 2)), ((0, 0), (0, N_DEV), (0, 0))) + token[0, 0]
    res["ada_w"] = _ada_backward_adamw(jnp.pad(c_all, ((0, N_DEV), (0, 0))), dmod_mine, ada_w, m_ada_w, v_ada_w)

    fulls, lands = _scatter_wait(sent[2], (0, 0), res["ada_w"][0], "scatter_wait_2")
    got[2] = dict(zip(("in", "out"), zip(fulls, lands)))
    got[0] = {}
    for kind in ("out", "in"):
        fulls, lands = _scatter_wait(sent["0_" + kind], (0,), res["ada_w"][0], "scatter_wait_0_" + kind)
        got[0][kind] = (fulls[0], lands[0])
    res["attn_w_out"] = _adamw_shards("adamw_attn_w_out", me, *pick((0, 2), "out"), attn_w_out, m_attn_w_out,
                                      v_attn_w_out, False)
    res["attn_w_in"] = tuple(jnp.swapaxes(a, 1, 2) for a in _adamw_shards(
        "adamw_attn_w_in", me, *pick((0, 2), "in"), w_in_rows, jnp.swapaxes(m_attn_w_in, 1, 2),
        jnp.swapaxes(v_attn_w_in, 1, 2), False))

    order = ("ada_w", "ada_b", "norm_g", "attn_w_in", "q_norm", "k_norm", "sinks", "attn_w_out", "pool_w_in",
             "pool_w_group", "pool_scale", "pool_w_out")
    return (loss, dx[None], *[res[k][0] for k in order], *[res[k][1] for k in order], *[res[k][2] for k in order],
            *[res[k][3] for k in order])
```

```python
import numpy as np
import jax
import jax.numpy as jnp
from jax import lax
from jax.experimental import pallas as pl
from jax.experimental.pallas import tpu as pltpu

F32 = jnp.float32
BF16 = jnp.bfloat16
MESH = pl.DeviceIdType.MESH

N_DEV = 8
D = 1024
DEPTH = 4
HEAD_DIM = 64
N_HEADS = 16
N_KV = 4
QK_W = 1280
ATTN_IN = 2560
POOL_IN = 2048
QBLK = 128
KX_W = N_KV * 128
CHUNK = 256
POOL_WINDOWS = (2, 4, 8, 16)
HALO = 16
ROPE_THETA = 500000.0
ROT_DIM = 16
NORM_EPS = 1e-6
ADAM_LR = 0.001
ADAM_B1 = 0.9
ADAM_B2 = 0.999
ADAM_EPS = 1e-08
ADAM_WD = 0.01
ADAM_STEP = 10

LANES = 128
VMEM_LIMIT = 56 * 2**20
VEC_ROWS = 32


def _cparams(n_grid=0, **kw):
    if n_grid:
        kw["dimension_semantics"] = ("arbitrary",) * n_grid
    return pltpu.CompilerParams(vmem_limit_bytes=VMEM_LIMIT, **kw)


def _call(body, **kw):
    return pl.pallas_call(body, **kw)


def _mod_spec(layer):
    return pl.BlockSpec((None, 8, D), lambda *_: (layer, 0, 0), pipeline_mode=pl.Buffered(1))


def _mod_row_spec(layer, row):
    return pl.BlockSpec((None, None, 1, D), lambda *_: (layer, row, 0, 0), pipeline_mode=pl.Buffered(1))


NORM_ROW, POOL_SCALE_ROW = 3, 4


def _const_spec(shape):
    nd = len(shape)
    return pl.BlockSpec(shape, lambda *_: (0,) * nd, pipeline_mode=pl.Buffered(1))


def _dot(a, b):
    return jnp.dot(a, b, preferred_element_type=F32)


def _dot_nt(a, b):
    return lax.dot_general(a, b, (((1,), (1,)), ((), ())), preferred_element_type=F32)


def _dot_tn(a, b):
    return lax.dot_general(a, b, (((0,), (0,)), ((), ())), preferred_element_type=F32)


def _group_mean(x, m):
    return _dot(x.astype(BF16), m) * (1.0 / HEAD_DIM)


def _sigmoid(g):
    return 1.0 / (1.0 + jnp.exp(-g))


def _norm_mod(x, ng, sc, sh):
    r = lax.rsqrt(jnp.mean(x * x, axis=-1, keepdims=True) + NORM_EPS)
    xh = x * r
    h = (xh * ng) * (1.0 + sc) + sh
    return xh, r, h


def _rope_table(pos_col, invf_row, tile):
    seq = pos_col.shape[0]

    def body(pos_ref, invf_ref, out_ref):
        ang = pos_ref[...].astype(F32) * invf_ref[...]
        l64 = lax.broadcasted_iota(jnp.int32, (tile, LANES), 1) & (HEAD_DIM - 1)
        cs, sn = jnp.cos(ang), jnp.sin(ang)
        out_ref[:, 0:LANES] = jnp.where(l64 < ROT_DIM, cs, 1.0)
        out_ref[:, LANES:2 * LANES] = jnp.where(l64 < ROT_DIM // 2, -sn, 0.0)
        out_ref[:, 2 * LANES:3 * LANES] = jnp.where((l64 >= ROT_DIM // 2) & (l64 < ROT_DIM), sn, 0.0)

    return _call(
        body, name="rope_table", grid=(seq // tile,),
        out_shape=jax.ShapeDtypeStruct((seq, 3 * LANES), F32),
        in_specs=[pl.BlockSpec((tile, 1), lambda i: (i, 0)), _const_spec((1, LANES))],
        out_specs=pl.BlockSpec((tile, 3 * LANES), lambda i: (i, 0)),
        compiler_params=_cparams(1),
    )(pos_col, invf_row)


def _rope_tabs(rope_ref):
    return rope_ref[:, 0:LANES], rope_ref[:, LANES:2 * LANES], rope_ref[:, 2 * LANES:3 * LANES]


def _rope(y, tabs):
    cos_t, sin_a, sin_b = tabs
    return y * cos_t + pltpu.roll(y, LANES - ROT_DIM // 2, 1) * sin_a + pltpu.roll(y, ROT_DIM // 2, 1) * sin_b


def _rope_bwd(dy, tabs):
    cos_t, sin_a, sin_b = tabs
    return dy * cos_t + pltpu.roll(dy * sin_a, ROT_DIM // 2, 1) + pltpu.roll(dy * sin_b, LANES - ROT_DIM // 2, 1)


def _low_half(rows):
    return lax.broadcasted_iota(jnp.int32, (rows, LANES), 1) < HEAD_DIM


def _adamw(w, g, m, v):
    m = ADAM_B1 * m + (1.0 - ADAM_B1) * g
    v = ADAM_B2 * v + (1.0 - ADAM_B2) * (g * g)
    m_hat = m / (1.0 - ADAM_B1 ** ADAM_STEP)
    v_hat = v / (1.0 - ADAM_B2 ** ADAM_STEP)
    delta = -ADAM_LR * (m_hat / (jnp.sqrt(v_hat) + ADAM_EPS) + ADAM_WD * w)
    return delta, m, v


def _my_position():
    x, y, c = lax.axis_index("x"), lax.axis_index("y"), lax.axis_index("c")
    return x, y, c, 4 * x + 2 * y + c


def _peers(x, y, c):
    out = []
    for k in range(1, N_DEV):
        px = 1 - x if k & 4 else x
        py = 1 - y if k & 2 else y
        pc = 1 - c if k & 1 else c
        out.append(((px, py, pc), 4 * px + 2 * py + pc))
    return out


def _allgather_small(v, name, after):
    rows, cols = v.shape

    def body(v_ref, after_ref, out_ref, send_sems, recv_sems, local_sem):
        x, y, c, me = _my_position()
        local = pltpu.make_async_copy(v_ref, out_ref.at[me], local_sem)
        local.start()
        sends = []
        for k, (peer, _) in enumerate(_peers(x, y, c)):
            cp = pltpu.make_async_remote_copy(v_ref, out_ref.at[me], send_sems.at[k], recv_sems.at[k],
                                              device_id=peer, device_id_type=MESH)
            cp.start()
            sends.append(cp)
        for k, (peer, idx) in enumerate(_peers(x, y, c)):
            pltpu.make_async_remote_copy(v_ref, out_ref.at[idx], send_sems.at[k], recv_sems.at[k],
                                         device_id=peer, device_id_type=MESH).wait_recv()
        for cp in sends:
            cp.wait_send()
        local.wait()

    return _call(
        body, name=name,
        out_shape=jax.ShapeDtypeStruct((N_DEV, rows, cols), F32),
        in_specs=[pl.BlockSpec(memory_space=pltpu.VMEM), pl.BlockSpec(memory_space=pl.ANY)],
        out_specs=pl.BlockSpec(memory_space=pltpu.VMEM),
        scratch_shapes=[pltpu.SemaphoreType.DMA((N_DEV - 1,)), pltpu.SemaphoreType.DMA((N_DEV - 1,)),
                        pltpu.SemaphoreType.DMA(())],
        compiler_params=_cparams(),
    )(v, after)


def _shard_rows(ref, idx, rows, axis):
    sl = [slice(None)] * len(ref.shape)
    sl[axis] = pl.ds(idx * rows, rows)
    return ref.at[tuple(sl)]


def _own_and_peer_rows(ref, me, idx, axis):
    rows = ref.shape[axis] // N_DEV
    return _shard_rows(ref, me, rows, axis), _shard_rows(ref, idx, rows, axis)


HBM_SPEC = pl.BlockSpec(memory_space=pltpu.HBM)
SEM_SPEC = pl.BlockSpec(memory_space=pltpu.SEMAPHORE)
ANY_SPEC = pl.BlockSpec(memory_space=pl.ANY)
DATAFLOW = pltpu.SideEffectType.DATAFLOW_SIDE_EFFECTING


def _hbm(a):
    return pltpu.with_memory_space_constraint(a, pltpu.HBM)


def _gather_start(layers, axes, after, name):
    flat = [a for arrs in layers for a in arrs]
    flat_axes = [ax for axs in axes for ax in axs]
    n, nl = len(flat), len(layers)

    def body(*refs):
        ins, sems, token = refs[:n], refs[n + 1:n + 1 + 2 * nl], refs[-1]
        x, y, c, me = _my_position()
        a0 = 0
        for li, arrs in enumerate(layers):
            for k, (peer, _) in enumerate(_peers(x, y, c)):
                for a in range(len(arrs)):
                    rows, _ = _own_and_peer_rows(ins[a0 + a], me, me, flat_axes[a0 + a])
                    pltpu.make_async_remote_copy(rows, rows, sems[2 * li].at[k * len(arrs) + a],
                                                 sems[2 * li + 1].at[k * len(arrs) + a],
                                                 device_id=peer, device_id_type=MESH).start()
            a0 += len(arrs)
        token[...] = jnp.zeros_like(token)

    sem_shapes = []
    for arrs in layers:
        sem_shapes += [pltpu.SemaphoreType.DMA(((N_DEV - 1) * len(arrs),))] * 2
    out = _call(
        body, name=name,
        out_shape=(*sem_shapes, *[pltpu.HBM(a.shape, a.dtype) for a in flat], jax.ShapeDtypeStruct((8, LANES), F32)),
        in_specs=[HBM_SPEC] * n + [ANY_SPEC],
        out_specs=(*[SEM_SPEC] * (2 * nl), *[HBM_SPEC] * n, pl.BlockSpec(memory_space=pltpu.VMEM)),
        input_output_aliases={a: 2 * nl + a for a in range(n)},
        compiler_params=_cparams(has_side_effects=DATAFLOW),
    )(*[_hbm(a) for a in flat], after)
    per_layer, a0 = [], 0
    for li, arrs in enumerate(layers):
        per_layer.append((out[2 * li], out[2 * li + 1], list(out[2 * nl + a0:2 * nl + a0 + len(arrs)])))
        a0 += len(arrs)
    return per_layer, out[-1]


def _gather_wait(started, axes, after, name):
    send_sems, recv_sems, arrs = started
    n = len(arrs)

    def body(*refs):
        ins, send_ref, recv_ref = refs[:n], refs[n], refs[n + 1]
        x, y, c, me = _my_position()
        for k, (peer, idx) in enumerate(_peers(x, y, c)):
            for a in range(n):
                own, theirs = _own_and_peer_rows(ins[a], me, idx, axes[a])
                cp = pltpu.make_async_remote_copy(own, theirs, send_ref.at[k * n + a], recv_ref.at[k * n + a],
                                                  device_id=peer, device_id_type=MESH)
                cp.wait_send()
                cp.wait_recv()

    return _call(
        body, name=name,
        out_shape=tuple(pltpu.HBM(a.shape, a.dtype) for a in arrs),
        in_specs=[HBM_SPEC] * n + [SEM_SPEC, SEM_SPEC, ANY_SPEC],
        out_specs=tuple([HBM_SPEC] * n),
        input_output_aliases={a: a for a in range(n)},
        compiler_params=_cparams(has_side_effects=DATAFLOW),
    )(*arrs, send_sems, recv_sems, after)


def _first_relations(x, y, c):
    return [(x, y, 1 - c), (1 - x, y, c), (x, 1 - y, c), (1 - x, 1 - y, c)]


def _gather_first_start(arr, after):
    n_rel = 4

    def body(a_ref, after_ref, send_ref, recv_ref, thru, token):
        x, y, c, me = _my_position()
        rows, _ = _own_and_peer_rows(a_ref, me, me, 0)
        for k, peer in enumerate(_first_relations(x, y, c)):
            pltpu.make_async_remote_copy(rows, rows, send_ref.at[k], recv_ref.at[k], device_id=peer, device_id_type=MESH).start()
        token[...] = jnp.zeros_like(token)

    sem = pltpu.SemaphoreType.DMA((n_rel,))
    out = _call(
        body, name="gather_first_start",
        out_shape=(sem, sem, pltpu.HBM(arr.shape, arr.dtype), jax.ShapeDtypeStruct((8, LANES), F32)),
        in_specs=[HBM_SPEC, ANY_SPEC],
        out_specs=(SEM_SPEC, SEM_SPEC, HBM_SPEC, pl.BlockSpec(memory_space=pltpu.VMEM)),
        input_output_aliases={0: 2},
        compiler_params=_cparams(has_side_effects=DATAFLOW),
    )(_hbm(arr), after)
    return out[:3], out[3]


def _gather_first_forward(started, after):
    send_a, recv_a, arr = started

    def body(a_ref, send_a_ref, recv_a_ref, after_ref, send_b_ref, recv_b_ref, thru, token):
        x, y, c, me = _my_position()
        sibling = (x, y, 1 - c)
        for k, peer in enumerate(_first_relations(x, y, c)):
            own, theirs = _own_and_peer_rows(a_ref, me, 4 * peer[0] + 2 * peer[1] + peer[2], 0)
            cp = pltpu.make_async_remote_copy(own, theirs, send_a_ref.at[k], recv_a_ref.at[k], device_id=peer, device_id_type=MESH)
            cp.wait_send()
            cp.wait_recv()
            if k > 0:
                pltpu.make_async_remote_copy(theirs, theirs, send_b_ref.at[k - 1], recv_b_ref.at[k - 1],
                                             device_id=sibling, device_id_type=MESH).start()
        token[...] = jnp.zeros_like(token)

    sem = pltpu.SemaphoreType.DMA((3,))
    out = _call(
        body, name="gather_first_forward",
        out_shape=(sem, sem, pltpu.HBM(arr.shape, arr.dtype), jax.ShapeDtypeStruct((8, LANES), F32)),
        in_specs=[HBM_SPEC, SEM_SPEC, SEM_SPEC, ANY_SPEC],
        out_specs=(SEM_SPEC, SEM_SPEC, HBM_SPEC, pl.BlockSpec(memory_space=pltpu.VMEM)),
        input_output_aliases={0: 2},
        compiler_params=_cparams(has_side_effects=DATAFLOW),
    )(arr, send_a, recv_a, after)
    return out[:3], out[3]


def _gather_first_wait(forwarded, after):
    send_b, recv_b, arr = forwarded

    def body(a_ref, send_b_ref, recv_b_ref, after_ref, thru):
        x, y, c, me = _my_position()
        sibling = (x, y, 1 - c)
        for k, peer in enumerate(_first_relations(x, y, c)[1:]):
            _, sent = _own_and_peer_rows(a_ref, me, 4 * peer[0] + 2 * peer[1] + peer[2], 0)
            _, got = _own_and_peer_rows(a_ref, me, 4 * peer[0] + 2 * peer[1] + (1 - peer[2]), 0)
            cp = pltpu.make_async_remote_copy(sent, got, send_b_ref.at[k], recv_b_ref.at[k], device_id=sibling, device_id_type=MESH)
            cp.wait_send()
            cp.wait_recv()

    return _call(
        body, name="gather_first_wait",
        out_shape=pltpu.HBM(arr.shape, arr.dtype),
        in_specs=[HBM_SPEC, SEM_SPEC, SEM_SPEC, ANY_SPEC],
        out_specs=HBM_SPEC,
        input_output_aliases={0: 0},
        compiler_params=_cparams(has_side_effects=DATAFLOW),
    )(arr, send_b, recv_b, after)


def _scatter_start(fulls, axes, name, after):
    n = len(fulls)
    lands = []
    for f, ax in zip(fulls, axes):
        shp = list(f.shape)
        shp[ax] //= N_DEV
        lands.append(_hbm(lax.empty((N_DEV - 1,) + tuple(shp), f.dtype)))

    def body(*refs):
        srcs, dsts, send_ref, recv_ref, token = refs[:n], refs[n:2 * n], refs[2 * n + 1], refs[2 * n + 2], refs[-1]
        x, y, c, me = _my_position()
        for k, (peer, idx) in enumerate(_peers(x, y, c)):
            for a in range(n):
                _, theirs = _own_and_peer_rows(srcs[a], me, idx, axes[a])
                pltpu.make_async_remote_copy(theirs, dsts[a].at[k], send_ref.at[k * n + a], recv_ref.at[k * n + a],
                                             device_id=peer, device_id_type=MESH).start()
        token[...] = jnp.zeros_like(token)

    sem = pltpu.SemaphoreType.DMA(((N_DEV - 1) * n,))
    out = _call(
        body, name=name,
        out_shape=(sem, sem, *[pltpu.HBM(a.shape, a.dtype) for a in fulls], *[pltpu.HBM(a.shape, a.dtype) for a in lands],
                   jax.ShapeDtypeStruct((8, LANES), F32)),
        in_specs=[HBM_SPEC] * (2 * n) + [ANY_SPEC],
        out_specs=(SEM_SPEC, SEM_SPEC, *[HBM_SPEC] * (2 * n), pl.BlockSpec(memory_space=pltpu.VMEM)),
        input_output_aliases={a: 2 + a for a in range(2 * n)},
        compiler_params=_cparams(has_side_effects=DATAFLOW),
    )(*[_hbm(a) for a in fulls], *lands, after)
    return (out[0], out[1], list(out[2:2 + n]), list(out[2 + n:2 + 2 * n])), out[-1]


def _scatter_wait(started, axes, after, name):
    send_sems, recv_sems, fulls, lands = started
    n = len(fulls)

    def body(*refs):
        srcs, dsts, send_ref, recv_ref = refs[:n], refs[n:2 * n], refs[2 * n], refs[2 * n + 1]
        x, y, c, me = _my_position()
        for k, (peer, idx) in enumerate(_peers(x, y, c)):
            for a in range(n):
                _, theirs = _own_and_peer_rows(srcs[a], me, idx, axes[a])
                cp = pltpu.make_async_remote_copy(theirs, dsts[a].at[k], send_ref.at[k * n + a], recv_ref.at[k * n + a],
                                                  device_id=peer, device_id_type=MESH)
                cp.wait_send()
                cp.wait_recv()

    out = _call(
        body, name=name,
        out_shape=tuple(pltpu.HBM(a.shape, a.dtype) for a in (*fulls, *lands)),
        in_specs=[HBM_SPEC] * (2 * n) + [SEM_SPEC, SEM_SPEC, ANY_SPEC],
        out_specs=tuple([HBM_SPEC] * (2 * n)),
        input_output_aliases={a: a for a in range(2 * n)},
        compiler_params=_cparams(has_side_effects=DATAFLOW),
    )(*fulls, *lands, send_sems, recv_sems, after)
    return list(out[:n]), list(out[n:])


def _prep_weights(me, items, name):
    def body(me_ref, *refs):
        for (_, _, kind), src, dst in zip(items, refs[:len(items)], refs[len(items):]):
            dst[...] = (src[...].T if kind == "T" else src[...]).astype(BF16)

    ins, in_specs, out_shapes, out_specs = [], [], [], []
    for src, j, kind in items:
        shard = src.shape[1:]
        ins.append(src)
        in_specs.append(pl.BlockSpec((None,) + tuple(shard), lambda i, me_ref, j=j, nd=len(shard): (j,) + (0,) * nd))
        if kind == "G":
            out_shapes.append((shard[0], N_DEV * shard[1], shard[2]))
            out_specs.append(pl.BlockSpec(tuple(shard), lambda i, me_ref: (0, me_ref[0], 0)))
        else:
            rows = shard[1] if kind == "T" else shard[0]
            out_shapes.append((N_DEV * rows, D))
            out_specs.append(pl.BlockSpec((rows, D), lambda i, me_ref: (me_ref[0], 0)))
    out = _call(
        body, name=name,
        grid_spec=pltpu.PrefetchScalarGridSpec(num_scalar_prefetch=1, grid=(1,), in_specs=in_specs, out_specs=tuple(out_specs)),
        out_shape=tuple(jax.ShapeDtypeStruct(s, BF16) for s in out_shapes),
        compiler_params=_cparams(1),
    )(me.reshape(1), *ins)
    return list(out)


def _ada_forward(c_all, ada_w):
    cols = ada_w.shape[2]

    def body(c_ref, w_ref, o_ref):
        cv = c_ref[...]
        sc = (cv * _sigmoid(cv)).astype(BF16)
        o_ref[...] = _dot(sc, w_ref[...].astype(BF16))

    return _call(
        body, name="ada_forward", grid=(DEPTH,),
        out_shape=jax.ShapeDtypeStruct((DEPTH, N_DEV, cols), F32),
        in_specs=[pl.BlockSpec((N_DEV, D), lambda i: (0, 0)), pl.BlockSpec((None, D, cols), lambda i: (i, 0, 0))],
        out_specs=pl.BlockSpec((None, N_DEV, cols), lambda i: (i, 0, 0)),
        compiler_params=_cparams(1),
    )(c_all, ada_w)


def _ada_backward_adamw(c_pad, dmod_pad, w, m, v):
    cols = w.shape[2]

    def body(c_ref, dm_ref, w_ref, m_ref, v_ref, g_out, d_out, m_out, v_out):
        cv = c_ref[...]
        sc = (cv * _sigmoid(cv)).astype(BF16)
        g = _dot_tn(sc, dm_ref[...].astype(BF16))
        g_out[...] = g
        d_out[...], m_out[...], v_out[...] = _adamw(w_ref[...], g, m_ref[...], v_ref[...])

    wspec = pl.BlockSpec((None, D, cols), lambda i: (i, 0, 0))
    return _call(
        body, name="ada_backward_adamw", grid=(DEPTH,),
        out_shape=tuple(jax.ShapeDtypeStruct(w.shape, F32) for _ in range(4)),
        in_specs=[pl.BlockSpec((2 * N_DEV, D), lambda i: (0, 0)), pl.BlockSpec((None, 2 * N_DEV, cols), lambda i: (i, 0, 0)),
                  wspec, wspec, wspec],
        out_specs=(wspec, wspec, wspec, wspec),
        compiler_params=_cparams(1),
    )(c_pad, dmod_pad, w, m, v)


def _attn_in_proj(x, rope, rows, mod, layer, w_t, j, gain, bd, tile):
    seq = x.shape[0]

    def body(x_ref, rope_ref, ng_ref, mod_ref, w_ref, gain_ref, bd_ref, qk_ref, qs_ref, kd_ref, vd_ref, g_ref):
        _, _, h = _norm_mod(x_ref[...], ng_ref[...], mod_ref[1:2, :], mod_ref[0:1, :])
        hb = h.astype(BF16)
        tabs = _rope_tabs(rope_ref)
        low = _low_half(tile)
        bdm = bd_ref[...]

        def put_kv(ref, blk, first_kv):
            sw = pltpu.roll(blk, HEAD_DIM, 1)
            ref[:, LANES * first_kv:LANES * (first_kv + 1)] = jnp.where(low, blk, sw).astype(BF16)
            ref[:, LANES * (first_kv + 1):LANES * (first_kv + 2)] = jnp.where(low, sw, blk).astype(BF16)

        def project(c):
            return _dot_nt(hb, w_ref[CHUNK * c:CHUNK * (c + 1), :])

        n_chunks = ATTN_IN // CHUNK
        per = CHUNK // LANES
        nxt = project(0)
        for c in range(n_chunks):
            cur = nxt
            if c + 1 < n_chunks:
                nxt = project(c + 1)
            col = CHUNK * c
            if col >= QK_W + N_KV * HEAD_DIM:
                g_ref[:, col - QK_W - N_KV * HEAD_DIM:col - QK_W - N_KV * HEAD_DIM + CHUNK] = cur.astype(BF16)
            elif col >= QK_W:
                for t in range(per):
                    put_kv(vd_ref, cur[:, LANES * t:LANES * (t + 1)], (col - QK_W) // HEAD_DIM + 2 * t)
            else:
                qk_ref[:, col:col + CHUNK] = cur
                for t in range(per):
                    b = per * c + t
                    blk = cur[:, LANES * t:LANES * (t + 1)]
                    ms = _group_mean(blk * blk, bdm)
                    y = (blk * lax.rsqrt(ms + NORM_EPS)) * gain_ref[:, LANES * b:LANES * (b + 1)]
                    rp = _rope(y, tabs)
                    if b < D // LANES:
                        rp = rp * (HEAD_DIM ** -0.5)
                        qs_ref[:, 2 * LANES * b:2 * LANES * b + LANES] = jnp.where(low, rp, 0.0).astype(BF16)
                        qs_ref[:, 2 * LANES * b + LANES:2 * LANES * (b + 1)] = jnp.where(low, 0.0, rp).astype(BF16)
                    else:
                        put_kv(kd_ref, rp, 2 * (b - D // LANES))

    row = lambda w: pl.BlockSpec((tile, w), lambda i: (i, 0))
    return _call(
        body, name=f"attn_in_proj_{j}", grid=(seq // tile,),
        out_shape=(jax.ShapeDtypeStruct((seq, QK_W), F32), jax.ShapeDtypeStruct((seq, N_HEADS * LANES), BF16),
                   jax.ShapeDtypeStruct((seq, KX_W), BF16), jax.ShapeDtypeStruct((seq, KX_W), BF16),
                   jax.ShapeDtypeStruct((seq, D), BF16)),
        in_specs=[row(D), row(3 * LANES), _mod_row_spec(layer, NORM_ROW), _mod_spec(layer), _const_spec((ATTN_IN, D)),
                  _const_spec((1, QK_W)), _const_spec((LANES, LANES))],
        out_specs=(row(QK_W), row(N_HEADS * LANES), row(KX_W), row(KX_W), row(D)),
        compiler_params=_cparams(1),
    )(x, rope, rows, mod, w_t, gain, bd)


def _band_mask(n, rows, keys_on_rows):
    shape = (2 * QBLK, rows) if keys_on_rows else (rows, 2 * QBLK)
    qi = lax.broadcasted_iota(jnp.int32, shape, 1 if keys_on_rows else 0) & (QBLK - 1)
    kj = lax.broadcasted_iota(jnp.int32, shape, 0 if keys_on_rows else 1)
    diff = QBLK + qi - kj
    first_key = jnp.where(n > 0, 0, QBLK)
    return (diff >= 0) & (diff < QBLK) & (kj >= first_key)


def _pair_up(st, low):
    return jnp.concatenate([jnp.where(low, st[0:QBLK], st[QBLK:2 * QBLK]),
                            jnp.where(low, st[2 * QBLK:3 * QBLK], st[3 * QBLK:4 * QBLK])], axis=1)


def _attn_forward(sinks, qs, kd, vd, j):
    seq = qs.shape[0]
    per = 2
    nb = seq // (per * QBLK)

    def body(sink_ref, q_ref, kp_ref, kc_ref, vp_ref, vc_ref, o_ref):
        n = pl.program_id(0)
        low = _low_half(QBLK)
        rowi = lax.broadcasted_iota(jnp.int32, (4 * QBLK, 1), 0)
        groups = per * N_KV

        def keys(p_ref, c_ref, blk, kv):
            cols = slice(LANES * kv, LANES * (kv + 1))
            if blk == 0:
                return jnp.concatenate([p_ref[:, cols], c_ref[0:QBLK, cols]], axis=0)
            return c_ref[QBLK * (blk - 1):QBLK * (blk + 1), cols]

        def scores(g):
            blk, kv = divmod(g, N_KV)
            q = jnp.concatenate([q_ref[QBLK * blk:QBLK * (blk + 1), LANES * h:LANES * (h + 1)]
                                 for h in range(4 * kv, 4 * kv + 4)], axis=0)
            return _dot_nt(q, keys(kp_ref, kc_ref, blk, kv))

        nxt = scores(0)
        for g in range(groups):
            blk, kv = divmod(g, N_KV)
            ok = _band_mask(n if blk == 0 else 1, 4 * QBLK, False)
            s = jnp.where(ok, nxt, -1e30)
            if g + 1 < groups:
                nxt = scores(g + 1)
            sink = jnp.where(rowi < QBLK, sink_ref[j, 4 * kv],
                             jnp.where(rowi < 2 * QBLK, sink_ref[j, 4 * kv + 1],
                                       jnp.where(rowi < 3 * QBLK, sink_ref[j, 4 * kv + 2], sink_ref[j, 4 * kv + 3])))
            m = jnp.maximum(jnp.max(s, axis=1, keepdims=True), sink)
            p = jnp.exp(s - m)
            den = jnp.sum(p, axis=1, keepdims=True) + jnp.exp(sink - m)
            o_st = _dot((p / den).astype(BF16), keys(vp_ref, vc_ref, blk, kv))
            o_ref[QBLK * blk:QBLK * (blk + 1), 2 * LANES * kv:2 * LANES * (kv + 1)] = _pair_up(o_st, low).astype(BF16)

    cur = lambda w: pl.BlockSpec((per * QBLK, w), lambda n: (n, 0))
    prev = lambda w: pl.BlockSpec((QBLK, w), lambda n: (jnp.maximum(per * n - 1, 0), 0))
    return _call(
        body, name=f"attn_forward_{j}", grid=(nb,),
        out_shape=jax.ShapeDtypeStruct((seq, D), BF16),
        in_specs=[pl.BlockSpec(memory_space=pltpu.SMEM), cur(N_HEADS * LANES), prev(KX_W), cur(KX_W), prev(KX_W), cur(KX_W)],
        out_specs=cur(D),
        compiler_params=_cparams(1),
    )(sinks, qs, kd, kd, vd, vd)


def _attn_out_proj(x, o, g, w, j, mod, layer, tile):
    seq = x.shape[0]

    def body(x_ref, o_ref, g_ref, w_ref, mod_ref, xo_ref, br_ref):
        gv = g_ref[...].astype(F32)
        u = (o_ref[...].astype(F32) * (gv * _sigmoid(gv))).astype(BF16)
        br = _dot(u, w_ref[...])
        br_ref[...] = br.astype(BF16)
        xo_ref[...] = x_ref[...] + mod_ref[2:3, :] * br

    row = pl.BlockSpec((tile, D), lambda i: (i, 0))
    return _call(
        body, name=f"attn_out_proj_{j}", grid=(seq // tile,),
        out_shape=(jax.ShapeDtypeStruct((seq, D), F32), jax.ShapeDtypeStruct((seq, D), BF16)),
        in_specs=[row, row, row, _const_spec((D, D)), _mod_spec(layer)],
        out_specs=(row, row),
        compiler_params=_cparams(1),
    )(x, o, g, w, mod)


def _attn_out_proj_bwd(dxn, br, o, g, w, j, mod, layer, tile, after):
    seq = dxn.shape[0]
    steps = seq // tile

    def body(dxn_ref, br_ref, o_ref, g_ref, w_ref, mod_ref, after_ref, do_ref, dg_ref, dw_ref, dgate_ref, dw_acc):
        i = pl.program_id(0)

        @pl.when(i == 0)
        def _():
            dw_acc[...] = jnp.zeros_like(dw_acc)
            dgate_ref[...] = jnp.zeros_like(dgate_ref)

        dxn_v, ov, gv = dxn_ref[...], o_ref[...].astype(F32), g_ref[...].astype(F32)
        dgate_ref[...] += jnp.sum(dxn_v * br_ref[...].astype(F32), axis=0, keepdims=True)
        dbr = (dxn_v * mod_ref[2:3, :]).astype(BF16)
        du = _dot_nt(dbr, w_ref[...])
        sg = _sigmoid(gv)
        sl = gv * sg
        dw_acc[...] += _dot_tn((ov * sl).astype(BF16), dbr)
        do = du * sl
        dg_ref[...] = (du * ov * (sg * (1.0 + gv * (1.0 - sg)))).astype(BF16)
        low = _low_half(tile)
        for b in range(D // LANES):
            blk = do[:, LANES * b:LANES * (b + 1)]
            do_ref[:, 2 * LANES * b:2 * LANES * b + LANES] = jnp.where(low, blk, 0.0).astype(BF16)
            do_ref[:, 2 * LANES * b + LANES:2 * LANES * (b + 1)] = jnp.where(low, 0.0, blk).astype(BF16)

        @pl.when(i == steps - 1)
        def _():
            dw_ref[...] = dw_acc[...].astype(BF16)

    row = lambda w_: pl.BlockSpec((tile, w_), lambda i: (i, 0))
    return _call(
        body, name=f"attn_out_proj_bwd_{j}", grid=(steps,),
        out_shape=(jax.ShapeDtypeStruct((seq, N_HEADS * LANES), BF16), jax.ShapeDtypeStruct((seq, D), BF16),
                   jax.ShapeDtypeStruct((D, D), BF16), jax.ShapeDtypeStruct((1, D), F32)),
        in_specs=[row(D), row(D), row(D), row(D), _const_spec((D, D)), _mod_spec(layer), ANY_SPEC],
        out_specs=(row(N_HEADS * LANES), row(D), pl.BlockSpec((D, D), lambda i: (0, 0)),
                   pl.BlockSpec((1, D), lambda i: (0, 0))),
        scratch_shapes=[pltpu.VMEM((D, D), F32)],
        compiler_params=_cparams(1),
    )(dxn, br, o, g, w, mod, after)


def _attn_backward(sinks, qs, dos, kd, vd, j, after):
    seq = qs.shape[0]
    per = 2
    nb = seq // (per * QBLK)
    kw = N_KV * HEAD_DIM

    def body(sink_ref, q_ref, do_ref, kp_ref, kc_ref, vp_ref, vc_ref, after_ref,
             dq_ref, dk_even, dk_odd, dv_even, dv_odd, dsink_ref, carry_k, carry_v, sink_acc):
        n = pl.program_id(0)

        @pl.when(n == 0)
        def _():
            carry_k[...] = jnp.zeros_like(carry_k)
            carry_v[...] = jnp.zeros_like(carry_v)
            sink_acc[...] = jnp.zeros_like(sink_acc)

        @pl.when(n < nb)
        def _():
            low = _low_half(QBLK)
            lane_q = lax.broadcasted_iota(jnp.int32, (1, 2 * QBLK), 1)
            groups = per * 2 * N_KV

            def keys(p_ref, c_ref, blk, kv):
                cols = slice(LANES * kv, LANES * (kv + 1))
                if blk == 0:
                    return jnp.concatenate([p_ref[:, cols], c_ref[0:QBLK, cols]], axis=0)
                return c_ref[QBLK * (blk - 1):QBLK * (blk + 1), cols]

            def first_products(g):
                blk, rest = divmod(g, 2 * N_KV)
                kv, half = divmod(rest, 2)
                heads = (4 * kv + half, 4 * kv + 2 + half)
                rows = slice(QBLK * blk, QBLK * (blk + 1))
                q = jnp.concatenate([q_ref[rows, LANES * h:LANES * (h + 1)] for h in heads], axis=0)
                do = jnp.concatenate([do_ref[rows, LANES * h:LANES * (h + 1)] for h in heads], axis=0)
                kk = keys(kp_ref, kc_ref, blk, kv)
                return heads, q, do, kk, _dot_nt(kk, q), _dot_nt(keys(vp_ref, vc_ref, blk, kv), do)

            nxt = first_products(0)
            dk_parts, dv_parts = [[], []], [[], []]
            dq_h, dk_kv, dv_kv = [], None, None
            for g in range(groups):
                blk, rest = divmod(g, 2 * N_KV)
                heads, q, do, kk, s_raw, dp_raw = nxt
                if g + 1 < groups:
                    nxt = first_products(g + 1)
                ok = _band_mask(n if blk == 0 else 1, 2 * QBLK, True)
                st = jnp.where(ok, s_raw, -1e30)
                sink = jnp.where(lane_q < QBLK, sink_ref[j, heads[0]], sink_ref[j, heads[1]])
                m = jnp.maximum(jnp.max(st, axis=0, keepdims=True), sink)
                e = jnp.exp(st - m)
                e_sink = jnp.exp(sink - m)
                inv = 1.0 / (jnp.sum(e, axis=0, keepdims=True) + e_sink)
                p = e * inv
                pdp = p * dp_raw
                delta = jnp.sum(pdp, axis=0, keepdims=True)
                ds = (pdp - p * delta).astype(BF16)
                sink_acc[rest:rest + 1, :] -= e_sink * inv * delta
                dk_g, dv_g = _dot(ds, q), _dot(p.astype(BF16), do)
                dk_kv = dk_g if dk_kv is None else dk_kv + dk_g
                dv_kv = dv_g if dv_kv is None else dv_kv + dv_g
                dq_h.append(_dot_tn(ds, kk))
                if g % 2 == 1:
                    kv = rest // 2
                    for t in range(2):
                        dq_ref[QBLK * blk:QBLK * (blk + 1), LANES * (2 * kv + t):LANES * (2 * kv + t + 1)] = jnp.where(
                            low, dq_h[0][QBLK * t:QBLK * (t + 1)], dq_h[1][QBLK * t:QBLK * (t + 1)])
                    dk_parts[blk].append(dk_kv + pltpu.roll(dk_kv, HEAD_DIM, 1))
                    dv_parts[blk].append(dv_kv + pltpu.roll(dv_kv, HEAD_DIM, 1))
                    dq_h, dk_kv, dv_kv = [], None, None

            def order(parts, lo, hi):
                return jnp.concatenate([jnp.where(low, parts[0][lo:hi], parts[1][lo:hi]),
                                        jnp.where(low, parts[2][lo:hi], parts[3][lo:hi])], axis=1)

            dk_odd[...] = carry_k[...] + order(dk_parts[0], 0, QBLK)
            dv_odd[...] = (carry_v[...] + order(dv_parts[0], 0, QBLK)).astype(BF16)
            dk_even[...] = order(dk_parts[0], QBLK, 2 * QBLK) + order(dk_parts[1], 0, QBLK)
            dv_even[...] = (order(dv_parts[0], QBLK, 2 * QBLK) + order(dv_parts[1], 0, QBLK)).astype(BF16)
            carry_k[...] = order(dk_parts[1], QBLK, 2 * QBLK)
            carry_v[...] = order(dv_parts[1], QBLK, 2 * QBLK)

        @pl.when(n == nb)
        def _():
            dk_odd[...] = carry_k[...]
            dv_odd[...] = carry_v[...].astype(BF16)
            lane = lax.broadcasted_iota(jnp.int32, (1, LANES), 1)
            out = jnp.zeros((1, LANES), F32)
            for g in range(2 * N_KV):
                for t in range(2):
                    tot = jnp.sum(sink_acc[g:g + 1, QBLK * t:QBLK * (t + 1)], axis=1, keepdims=True)
                    out = jnp.where(lane == 4 * (g // 2) + 2 * t + g % 2, tot, out)
            dsink_ref[...] = out

    cur = lambda w: pl.BlockSpec((per * QBLK, w), lambda n: (jnp.minimum(n, nb - 1), 0))
    prev = lambda w: pl.BlockSpec((QBLK, w), lambda n: (jnp.maximum(per * n - 1, 0), 0))
    even = pl.BlockSpec((None, QBLK, kw), lambda n: (jnp.minimum(n, nb - 1), 0, 0))
    odd = pl.BlockSpec((None, QBLK, kw), lambda n: (jnp.maximum(n - 1, 0), 0, 0))
    halves = lambda dt: jax.ShapeDtypeStruct((nb, QBLK, kw), dt)
    dq, dk_e, dk_o, dv_e, dv_o, dsink = _call(
        body, name=f"attn_backward_{j}", grid=(nb + 1,),
        out_shape=(jax.ShapeDtypeStruct((seq, D), F32), halves(F32), halves(F32), halves(BF16), halves(BF16),
                   jax.ShapeDtypeStruct((1, LANES), F32)),
        in_specs=[pl.BlockSpec(memory_space=pltpu.SMEM), cur(N_HEADS * LANES), cur(N_HEADS * LANES), prev(KX_W), cur(KX_W),
                  prev(KX_W), cur(KX_W), ANY_SPEC],
        out_specs=(cur(D), even, odd, even, odd, pl.BlockSpec((1, LANES), lambda n: (0, 0))),
        scratch_shapes=[pltpu.VMEM((QBLK, kw), F32), pltpu.VMEM((QBLK, kw), F32), pltpu.VMEM((2 * N_KV, 2 * QBLK), F32)],
        compiler_params=_cparams(1),
    )(sinks, qs, dos, kd, kd, vd, vd, after)
    weave = lambda ev, od: jnp.stack([ev, od], axis=1).reshape(seq, kw)
    return dq, weave(dk_e, dk_o), weave(dv_e, dv_o), dsink


def _in_proj_tail(x_ref, dxn_ref, ng_ref, mod_ref, w_ref, dproj, dx_ref, dw_acc, vec_acc):
    ng, sc, sh = ng_ref[...], mod_ref[1:2, :], mod_ref[0:1, :]
    xh, r, h = _norm_mod(x_ref[...], ng, sc, sh)
    dh = _dot(dproj, w_ref[...])
    dw_acc[...] += _dot_tn(dproj, h.astype(BF16))
    vec_acc[0:1, :] += jnp.sum(dh, axis=0, keepdims=True)
    vec_acc[1:2, :] += jnp.sum(dh * xh, axis=0, keepdims=True)
    dxh = dh * (ng * (1.0 + sc))
    dx_ref[...] = dxn_ref[...] + r * (dxh - xh * jnp.mean(dxh * xh, axis=-1, keepdims=True))


def _tail_finish(ng_ref, mod_ref, dw_ref, vec_ref, dw_acc, vec_acc):
    dw_ref[...] = dw_acc[...].astype(BF16)
    a = vec_acc[1:2, :]
    vec_ref[...] = jnp.zeros_like(vec_ref)
    vec_ref[0:1, :] = vec_acc[0:1, :]
    vec_ref[1:2, :] = a * ng_ref[...]
    vec_ref[3:4, :] = a * (1.0 + mod_ref[1:2, :])


def _attn_in_proj_bwd(x, dxn, rope, qk_raw, dq, dk, dv, dg, rows, mod, layer, w_t, j, gain, bd, tile):
    seq = x.shape[0]
    steps = seq // tile

    def body(x_ref, dxn_ref, rope_ref, qk_ref, dq_ref, dk_ref, dv_ref, dg_ref, ng_ref, mod_ref, w_ref, gain_ref,
             bd_ref, dx_ref, dw_ref, vec_ref, dgain_ref, dproj, dw_acc, vec_acc):
        i = pl.program_id(0)

        @pl.when(i == 0)
        def _():
            dw_acc[...] = jnp.zeros_like(dw_acc)
            vec_acc[...] = jnp.zeros_like(vec_acc)
            dgain_ref[...] = jnp.zeros_like(dgain_ref)

        tabs = _rope_tabs(rope_ref)
        bdm = bd_ref[...]
        for b in range(QK_W // LANES):
            cols = slice(LANES * b, LANES * (b + 1))
            raw = qk_ref[:, cols]
            if b < D // LANES:
                dy = dq_ref[:, cols] * (HEAD_DIM ** -0.5)
            else:
                dy = dk_ref[:, LANES * (b - D // LANES):LANES * (b + 1 - D // LANES)]
            dy = _rope_bwd(dy, tabs)
            rr = lax.rsqrt(_group_mean(raw * raw, bdm) + NORM_EPS)
            xh = raw * rr
            dgain_ref[:, cols] += jnp.sum(dy * xh, axis=0, keepdims=True)
            dxh = dy * gain_ref[:, cols]
            dproj[:, cols] = (rr * (dxh - xh * _group_mean(dxh * xh, bdm))).astype(BF16)
        dproj[:, QK_W:QK_W + N_KV * HEAD_DIM] = dv_ref[...]
        dproj[:, QK_W + N_KV * HEAD_DIM:] = dg_ref[...]
        _in_proj_tail(x_ref, dxn_ref, ng_ref, mod_ref, w_ref, dproj[...], dx_ref, dw_acc, vec_acc)

        @pl.when(i == steps - 1)
        def _():
            _tail_finish(ng_ref, mod_ref, dw_ref, vec_ref, dw_acc, vec_acc)

    row = lambda w, dt=None: pl.BlockSpec((tile, w), lambda i: (i, 0))
    fixed = lambda shape: pl.BlockSpec(shape, lambda i: (0,) * len(shape))
    return _call(
        body, name=f"attn_in_proj_bwd_{j}", grid=(steps,),
        out_shape=(jax.ShapeDtypeStruct((seq, D), F32), jax.ShapeDtypeStruct((ATTN_IN, D), BF16),
                   jax.ShapeDtypeStruct((8, D), F32), jax.ShapeDtypeStruct((1, QK_W), F32)),
        in_specs=[row(D), row(D), row(3 * LANES), row(QK_W), row(D), row(N_KV * HEAD_DIM), row(N_KV * HEAD_DIM), row(D),
                  _mod_row_spec(layer, NORM_ROW), _mod_spec(layer), _const_spec((ATTN_IN, D)), _const_spec((1, QK_W)),
                  _const_spec((LANES, LANES))],
        out_specs=(row(D), fixed((ATTN_IN, D)), fixed((8, D)), fixed((1, QK_W))),
        scratch_shapes=[pltpu.VMEM((tile, ATTN_IN), BF16), pltpu.VMEM((ATTN_IN, D), F32), pltpu.VMEM((8, D), F32)],
        compiler_params=_cparams(1),
    )(x, dxn, rope, qk_raw, dq, dk, dv, dg, rows, mod, w_t, gain, bd)


def _pool_in_proj(x, rows, mod, layer, w_t, j, tile):
    seq = x.shape[0]

    def body(x_ref, ng_ref, mod_ref, w_ref, v_ref, g_ref):
        _, _, h = _norm_mod(x_ref[...], ng_ref[...], mod_ref[1:2, :], mod_ref[0:1, :])
        proj = _dot_nt(h.astype(BF16), w_ref[...])
        v_ref[...] = proj[:, :D].astype(BF16)
        g_ref[...] = proj[:, D:].astype(BF16)

    row = pl.BlockSpec((tile, D), lambda i: (i, 0))
    return _call(
        body, name=f"pool_in_proj_{j}", grid=(seq // tile,),
        out_shape=(jax.ShapeDtypeStruct((seq, D), BF16), jax.ShapeDtypeStruct((seq, D), BF16)),
        in_specs=[row, _mod_row_spec(layer, NORM_ROW), _mod_spec(layer), _const_spec((POOL_IN, D))],
        out_specs=(row, row),
        compiler_params=_cparams(1),
    )(x, rows, mod, w_t)


PAD = 8


def _window_sums(ext, lo, hi, forward):
    gw = D // len(POOL_WINDOWS)
    planes = []
    for gi, w in enumerate(POOL_WINDOWS):
        cols = slice(gw * gi, gw * (gi + 1))
        src, k = 0, 1
        while k < w:
            d = k if forward else -k
            ext[1 - src, lo:hi, cols] = ext[src, lo:hi, cols] + ext[src, lo + d:hi + d, cols]
            src, k = 1 - src, 2 * k
        planes.append(src)
    return planes


def _pooled(ext, v_ref, first, tile):
    t_abs = first + lax.broadcasted_iota(jnp.int32, (tile, 1), 0)
    top = PAD + HALO
    planes = _window_sums(ext, PAD, top + tile, False)
    outs = []
    gw = D // len(POOL_WINDOWS)
    for gi, w in enumerate(POOL_WINDOWS):
        cols = slice(gw * gi, gw * (gi + 1))
        cnt = jnp.minimum(t_abs + 1, w).astype(F32)
        outs.append(ext[planes[gi], top:top + tile, cols] / cnt - v_ref[:, cols].astype(F32))
    return jnp.concatenate(outs, axis=1)


def _fill_ext(ext, halo_ref, v_ref, i, tile):
    ext[0, 0:PAD, :] = jnp.zeros((PAD, D), F32)
    ext[1, 0:PAD, :] = jnp.zeros((PAD, D), F32)
    ext[0, PAD:PAD + HALO, :] = jnp.where(i == 0, 0.0, halo_ref[...].astype(F32))
    ext[0, PAD + HALO:PAD + HALO + tile, :] = v_ref[...].astype(F32)


def _group_mix(pb, wg_ref):
    gw = D // len(POOL_WINDOWS)
    return jnp.concatenate([_dot(pb[:, gw * gi:gw * (gi + 1)], wg_ref[gi]) for gi in range(len(POOL_WINDOWS))], axis=1)


def _pool_mix_out(x, v, g, wg, w_out, j, rows, mod, layer, tile, target=None):
    seq = x.shape[0]

    def body(*refs):
        if target is None:
            x_ref, v_ref, halo_ref, g_ref, wg_ref, w_ref, scale_ref, mod_ref, xo_ref, br_ref, ext = refs
        else:
            x_ref, v_ref, halo_ref, g_ref, wg_ref, w_ref, scale_ref, mod_ref, t_ref, xo_ref, br_ref, loss_ref, ext = refs
        i = pl.program_id(0)
        _fill_ext(ext, halo_ref, v_ref, i, tile)
        pb = _pooled(ext, v_ref, i * tile, tile).astype(BF16)
        ms = _group_mix(pb, wg_ref) * scale_ref[...]
        gv = g_ref[...].astype(F32)
        u = (ms * (gv * _sigmoid(gv))).astype(BF16)
        br = _dot(u, w_ref[...])
        br_ref[...] = br.astype(BF16)
        y = x_ref[...] + mod_ref[2:3, :] * br
        if target is None:
            xo_ref[...] = y
        else:
            @pl.when(i == 0)
            def _():
                loss_ref[...] = jnp.zeros_like(loss_ref)

            e = y - t_ref[...]
            xo_ref[...] = e * (1.0 / D)
            loss_ref[...] += 0.5 * jnp.sum(jnp.mean(e * e, axis=-1, keepdims=True), axis=0, keepdims=True)

    row = pl.BlockSpec((tile, D), lambda i: (i, 0))
    halo = pl.BlockSpec((HALO, D), lambda i: (jnp.maximum(i * (tile // HALO) - 1, 0), 0))
    extra_in, extra_out, extra_shape = ([], (), ()) if target is None else (
        [row], (pl.BlockSpec((1, LANES), lambda i: (0, 0)),), (jax.ShapeDtypeStruct((1, LANES), F32),))
    return _call(
        body, name=f"pool_mix_out_{j}", grid=(seq // tile,),
        out_shape=(jax.ShapeDtypeStruct((seq, D), F32), jax.ShapeDtypeStruct((seq, D), BF16)) + extra_shape,
        in_specs=[row, row, halo, row, _const_spec(wg.shape), _const_spec((D, D)), _mod_row_spec(layer, POOL_SCALE_ROW),
                  _mod_spec(layer)] + extra_in,
        out_specs=(row, row) + extra_out,
        scratch_shapes=[pltpu.VMEM((2, tile + HALO + PAD, D), F32)],
        compiler_params=_cparams(1),
    )(x, v, v, g, wg, w_out, rows, mod, *(() if target is None else (target,)))


def _pool_mix_out_bwd(dxn, br, v, g, wg, w_out, j, rows, mod, layer, tile, after):
    seq = dxn.shape[0]
    steps = seq // tile
    ng_ = len(POOL_WINDOWS)
    gw = D // ng_

    def body(dxn_ref, br_ref, v_ref, halo_ref, g_ref, wg_ref, w_ref, scale_ref, mod_ref, after_ref,
             dpool_ref, dg_ref, dw_ref, dwg_ref, vec_ref, ext, dw_acc, dwg_acc):
        i = pl.program_id(0)

        @pl.when(i == 0)
        def _():
            dw_acc[...] = jnp.zeros_like(dw_acc)
            dwg_acc[...] = jnp.zeros_like(dwg_acc)
            vec_ref[...] = jnp.zeros_like(vec_ref)

        _fill_ext(ext, halo_ref, v_ref, i, tile)
        pb = _pooled(ext, v_ref, i * tile, tile).astype(BF16)
        mixed = _group_mix(pb, wg_ref)
        scale = scale_ref[...]
        ms = mixed * scale
        gv, dxn_v = g_ref[...].astype(F32), dxn_ref[...]
        sg = _sigmoid(gv)
        sl = gv * sg
        vec_ref[0:1, :] += jnp.sum(dxn_v * br_ref[...].astype(F32), axis=0, keepdims=True)
        dbr = (dxn_v * mod_ref[2:3, :]).astype(BF16)
        du = _dot_nt(dbr, w_ref[...])
        dw_acc[...] += _dot_tn((ms * sl).astype(BF16), dbr)
        dms = du * sl
        dg_ref[...] = (du * ms * (sg * (1.0 + gv * (1.0 - sg)))).astype(BF16)
        vec_ref[1:2, :] += jnp.sum(dms * mixed, axis=0, keepdims=True)
        dmx = (dms * scale).astype(BF16)
        for gi in range(ng_):
            cols = slice(gw * gi, gw * (gi + 1))
            dpool_ref[:, cols] = _dot_nt(dmx[:, cols], wg_ref[gi])
            dwg_acc[gi] += _dot_tn(pb[:, cols], dmx[:, cols])

        @pl.when(i == steps - 1)
        def _():
            dw_ref[...] = dw_acc[...].astype(BF16)
            dwg_ref[...] = dwg_acc[...].astype(BF16)

    row = pl.BlockSpec((tile, D), lambda i: (i, 0))
    halo = pl.BlockSpec((HALO, D), lambda i: (jnp.maximum(i * (tile // HALO) - 1, 0), 0))
    fixed = lambda shape: pl.BlockSpec(shape, lambda i: (0,) * len(shape))
    return _call(
        body, name=f"pool_mix_out_bwd_{j}", grid=(steps,),
        out_shape=(jax.ShapeDtypeStruct((seq, D), F32), jax.ShapeDtypeStruct((seq, D), BF16),
                   jax.ShapeDtypeStruct((D, D), BF16), jax.ShapeDtypeStruct((ng_, gw, gw), BF16),
                   jax.ShapeDtypeStruct((8, D), F32)),
        in_specs=[row, row, row, halo, row, _const_spec(wg.shape), _const_spec((D, D)), _mod_row_spec(layer, POOL_SCALE_ROW),
                  _mod_spec(layer), ANY_SPEC],
        out_specs=(row, row, fixed((D, D)), fixed((ng_, gw, gw)), fixed((8, D))),
        scratch_shapes=[pltpu.VMEM((2, tile + HALO + PAD, D), F32), pltpu.VMEM((D, D), F32), pltpu.VMEM((ng_, gw, gw), F32)],
        compiler_params=_cparams(1),
    )(dxn, br, v, v, g, wg, w_out, rows, mod, after)


def _pool_in_proj_bwd(x, dxn, dpool, dg, rows, mod, layer, w_t, j, tile, after):
    seq = x.shape[0]
    steps = seq // tile
    gw = D // len(POOL_WINDOWS)

    def body(x_ref, dxn_ref, dp_ref, halo_ref, dg_ref, ng_ref, mod_ref, w_ref, after_ref, dx_ref, dw_ref, vec_ref,
             ext, dproj, dw_acc, vec_acc):
        i = pl.program_id(0)

        @pl.when(i == 0)
        def _():
            dw_acc[...] = jnp.zeros_like(dw_acc)
            vec_acc[...] = jnp.zeros_like(vec_acc)

        t_abs = i * tile + lax.broadcasted_iota(jnp.int32, (tile, 1), 0)
        last = i == steps - 1
        ext[0, tile + HALO:tile + HALO + PAD, :] = jnp.zeros((PAD, D), F32)
        ext[1, tile + HALO:tile + HALO + PAD, :] = jnp.zeros((PAD, D), F32)
        for gi, w in enumerate(POOL_WINDOWS):
            cols = slice(gw * gi, gw * (gi + 1))
            cnt = jnp.minimum(t_abs + 1, w).astype(F32)
            ext[0, 0:tile, cols] = dp_ref[:, cols] / cnt
            ext[0, tile:tile + HALO, cols] = jnp.where(last, 0.0, halo_ref[:, cols] * (1.0 / w))
        planes = _window_sums(ext, 0, tile + HALO, True)
        for gi, w in enumerate(POOL_WINDOWS):
            cols = slice(gw * gi, gw * (gi + 1))
            dproj[:, cols] = (ext[planes[gi], 0:tile, cols] - dp_ref[:, cols]).astype(BF16)
        dproj[:, D:] = dg_ref[...]
        _in_proj_tail(x_ref, dxn_ref, ng_ref, mod_ref, w_ref, dproj[...], dx_ref, dw_acc, vec_acc)

        @pl.when(last)
        def _():
            _tail_finish(ng_ref, mod_ref, dw_ref, vec_ref, dw_acc, vec_acc)

    row = pl.BlockSpec((tile, D), lambda i: (i, 0))
    halo = pl.BlockSpec((HALO, D), lambda i: (jnp.minimum((i + 1) * (tile // HALO), seq // HALO - 1), 0))
    fixed = lambda shape: pl.BlockSpec(shape, lambda i: (0,) * len(shape))
    return _call(
        body, name=f"pool_in_proj_bwd_{j}", grid=(steps,),
        out_shape=(jax.ShapeDtypeStruct((seq, D), F32), jax.ShapeDtypeStruct((POOL_IN, D), BF16),
                   jax.ShapeDtypeStruct((8, D), F32)),
        in_specs=[row, row, row, halo, row, _mod_row_spec(layer, NORM_ROW), _mod_spec(layer), _const_spec((POOL_IN, D)), ANY_SPEC],
        out_specs=(row, fixed((POOL_IN, D)), fixed((8, D))),
        scratch_shapes=[pltpu.VMEM((2, tile + HALO + PAD, D), F32), pltpu.VMEM((tile, POOL_IN), BF16), pltpu.VMEM((POOL_IN, D), F32),
                        pltpu.VMEM((8, D), F32)],
        compiler_params=_cparams(1),
    )(x, dxn, dpool, dpool, dg, rows, mod, w_t, after)


def _build_vec(vecs, gates, pool_vecs, gains, dsinks, loss_part):
    def body(v0, v1, v2, v3, g0, g2, p0, p1, n0, n1, s0, s1, loss_ref, out):
        out[...] = jnp.zeros_like(out)
        for i, v in enumerate((v0, v1, v2, v3)):
            out[3 * i:3 * i + 2, :] = v[0:2, :]
            out[12 + i:13 + i, :] = v[3:4, :]
        out[2:3, :] = g0[...]
        out[8:9, :] = g2[...]
        for j, (p, n, s) in enumerate(((p0, n0, s0), (p1, n1, s1))):
            out[3 * (2 * j + 1) + 2:3 * (2 * j + 1) + 3, :] = p[0:1, :]
            out[22 + j:23 + j, :] = p[1:2, :]
            out[16 + j:17 + j, :] = n[:, 0:D]
            out[18 + j:19 + j, 0:QK_W - D] = n[:, D:QK_W]
            out[20 + j:21 + j, 0:LANES] = s[...]
        out[24:25, 0:LANES] = loss_ref[...]

    vm = pl.BlockSpec(memory_space=pltpu.VMEM)
    args = (*vecs, gates[0], gates[2], *pool_vecs, *gains, *dsinks, loss_part)
    return _call(
        body, name="build_vec",
        out_shape=jax.ShapeDtypeStruct((VEC_ROWS, D), F32),
        in_specs=[vm] * len(args), out_specs=vm,
        compiler_params=_cparams(),
    )(*args)


def _sum_devices(g, after):
    rows = g.shape[1]

    def body(g_ref, after_ref, tot_ref, fold_ref):
        tot = g_ref[0]
        for p in range(1, N_DEV):
            tot = tot + g_ref[p]
        tot_ref[...] = tot
        f = tot[16:24, 0:LANES]
        for b in range(1, D // LANES):
            f = f + tot[16:24, LANES * b:LANES * (b + 1)]
        fold_ref[...] = f + pltpu.roll(f, HEAD_DIM, 1)

    return _call(
        body, name="sum_devices",
        out_shape=(jax.ShapeDtypeStruct((rows, D), F32), jax.ShapeDtypeStruct((8, LANES), F32)),
        in_specs=[pl.BlockSpec(memory_space=pltpu.VMEM), ANY_SPEC],
        out_specs=(pl.BlockSpec(memory_space=pltpu.VMEM), pl.BlockSpec(memory_space=pltpu.VMEM)),
        compiler_params=_cparams(),
    )(g, after)


def _adamw_small(params):
    n = len(params)

    def body(*refs):
        ins, outs = refs[:4 * n], refs[4 * n:]
        for p in range(n):
            w_ref, g_ref, m_ref, v_ref = ins[4 * p:4 * p + 4]
            outs[3 * p][...], outs[3 * p + 1][...], outs[3 * p + 2][...] = _adamw(w_ref[...], g_ref[...], m_ref[...], v_ref[...])

    vm = pl.BlockSpec(memory_space=pltpu.VMEM)
    out = _call(
        body, name="adamw_small",
        out_shape=tuple(jax.ShapeDtypeStruct(w.shape, F32) for (w, _, _, _) in params for _ in range(3)),
        in_specs=[vm] * (4 * n), out_specs=tuple([vm] * (3 * n)),
        compiler_params=_cparams(),
    )(*[a for p in params for a in p])
    return [tuple(out[3 * p:3 * p + 3]) for p in range(n)]


def _adamw_shards(name, me, fulls, lands, w, m, v, transpose, axis=0):
    nl = w.shape[0]
    wshape = w.shape[1:]
    own_shape = lands[0].shape[1:]

    def body(me_ref, *refs):
        own_refs, land_refs = refs[:nl], refs[nl:2 * nl]
        w_ref, m_ref, v_ref, g_out, d_out, m_out, v_out = refs[2 * nl:]
        layer = pl.program_id(0)
        for l in range(nl):
            @pl.when(layer == l)
            def _(l=l):
                g = own_refs[l][...].astype(F32)
                for k in range(N_DEV - 1):
                    g = g + land_refs[l][k].astype(F32)
                if transpose:
                    g = g.T
                g_out[...] = g
                d_out[...], m_out[...], v_out[...] = _adamw(w_ref[...], g, m_ref[...], v_ref[...])

    def own_index(l_, me_ref):
        idx = [0] * len(own_shape)
        idx[axis] = me_ref[0]
        return tuple(idx)

    own_spec = pl.BlockSpec(tuple(own_shape), own_index)
    land_spec = pl.BlockSpec((N_DEV - 1,) + tuple(own_shape), lambda l_, me_ref: (0,) * (1 + len(own_shape)))
    wspec = pl.BlockSpec((None,) + tuple(wshape), lambda l_, me_ref: (l_,) + (0,) * len(wshape))
    return _call(
        body, name=name,
        grid_spec=pltpu.PrefetchScalarGridSpec(num_scalar_prefetch=1, grid=(nl,),
                                               in_specs=[own_spec] * nl + [land_spec] * nl + [wspec] * 3,
                                               out_specs=(wspec,) * 4),
        out_shape=tuple(jax.ShapeDtypeStruct(w.shape, F32) for _ in range(4)),
        compiler_params=_cparams(1),
    )(me.reshape(1), *fulls, *lands, w, m, v)


def _constants():
    lane = np.arange(LANES)
    bd = (lane[:, None] // HEAD_DIM == lane[None, :] // HEAD_DIM).astype(np.float32)
    half = ROT_DIM // 2
    inv_freq = ROPE_THETA ** (-jnp.arange(half, dtype=F32) * 2.0 / ROT_DIM)
    invf = jnp.tile(inv_freq, LANES // half).reshape(1, LANES)
    return jnp.asarray(bd, BF16), invf


def kernel(x, c, positions, ada_w, ada_b, norm_g, attn_w_in, attn_q_norm, attn_k_norm, attn_sinks, attn_w_out, pool_w_in, pool_w_group, pool_scale, pool_w_out, loss_target, m_ada_w, m_ada_b, m_norm_g, m_attn_w_in, m_attn_q_norm, m_attn_k_norm, m_attn_sinks, m_attn_w_out, m_pool_w_in, m_pool_w_group, m_pool_scale, m_pool_w_out, v_ada_w, v_ada_b, v_norm_g, v_attn_w_in, v_attn_q_norm, v_attn_k_norm, v_attn_sinks, v_attn_w_out, v_pool_w_in, v_pool_w_group, v_pool_scale, v_pool_w_out):
    seq = x.shape[1]
    me = 4 * lax.axis_index("x") + 2 * lax.axis_index("y") + lax.axis_index("c")
    bd, invf = _constants()
    t_mm = min(512, seq)
    rope = _rope_table(positions.reshape(seq, 1), invf, t_mm)
    t_bw = min(256, seq)
    shard = pool_scale.shape[1]
    cols = ada_w.shape[2]

    w_in_rows = jnp.swapaxes(attn_w_in, 1, 2)
    w_first, = _prep_weights(me, [(w_in_rows, 0, "N")], "prep_first")
    first_w, token = _gather_first_start(w_first, c)
    prepped = _prep_weights(me, [(attn_w_out, 0, "N"), (pool_w_in, 0, "T"), (pool_w_out, 0, "N"), (pool_w_group, 0, "G"),
                                 (w_in_rows, 1, "N"), (attn_w_out, 1, "N"), (pool_w_in, 1, "T"), (pool_w_out, 1, "N"),
                                 (pool_w_group, 1, "G")], "prep_rest")

    first = jnp.concatenate([c, jnp.pad(pool_scale, ((0, 0), (0, D - shard))), jnp.zeros((5, D), F32)], axis=0)
    first = _allgather_small(first + token[0, 0], "allgather_c", rope)
    c_all = first[:, 0, :]
    scale_full = jnp.transpose(first[:, 1:3, :shard], (1, 0, 2)).reshape(2, D)
    mod_part = _ada_forward(c_all, ada_w)
    mod_all = _allgather_small(mod_part.reshape(DEPTH * N_DEV, cols), "allgather_mod", prepped[0])
    mod_all = mod_all.reshape(N_DEV, DEPTH, N_DEV, cols)
    mine = lax.dynamic_index_in_dim(mod_all, me, axis=2, keepdims=False)
    mod = jnp.transpose(mine, (1, 0, 2)).reshape(DEPTH, 3 * D) + ada_b
    pool_rows = jnp.stack([jnp.zeros_like(scale_full[0]), scale_full[0], jnp.zeros_like(scale_full[0]), scale_full[1]])
    mod = jnp.concatenate([mod.reshape(DEPTH, 3, D), norm_g[:, None, :], pool_rows[:, None, :],
                           jnp.zeros((DEPTH, 3, D), F32)], axis=1)
    rows = mod.reshape(DEPTH, 8, 1, D)

    groups = [prepped[0:1], prepped[1:4], prepped[4:6], prepped[6:9]]
    gaxes = [(0,), (0,), (0, 0, 1), (0, 0), (0, 0, 1)]
    first_w, token = _gather_first_forward(first_w, mod)
    rest, token = _gather_start(groups, gaxes[1:], token, "gather_start_rest")
    started = [None] + rest

    saved, weights = [], []
    h = x[0]
    for i in range(DEPTH):
        j = i // 2
        s = dict(x=h)
        if i == 0:
            w_in_t = _gather_first_wait(first_w, token)
        else:
            wts = _gather_wait(started[i + 1], gaxes[i + 1], h, f"gather_wait_{i}")
        if i % 2 == 0:
            if i > 0:
                w_in_t, w_out = wts
            s["gain"] = jnp.concatenate([jnp.tile(attn_q_norm[j], N_HEADS), jnp.tile(attn_k_norm[j], N_KV)]).reshape(1, QK_W)
            s["qk_raw"], s["qs"], s["kd"], s["vd"], s["g"] = _attn_in_proj(
                h, rope, rows, mod, i, w_in_t, j, s["gain"], bd, t_mm)
            s["o"] = _attn_forward(attn_sinks, s["qs"], s["kd"], s["vd"], j)
            if i == 0:
                w_out, = _gather_wait(started[1], gaxes[1], s["o"], "gather_wait_0_out")
            h, s["br"] = _attn_out_proj(h, s["o"], s["g"], w_out, j, mod, i, t_mm)
            weights.append((w_in_t, w_out))
        else:
            p_in_t, p_out, p_grp = wts
            s["v"], s["g"] = _pool_in_proj(h, rows, mod, i, p_in_t, j, t_mm)
            if i < DEPTH - 1:
                h, s["br"] = _pool_mix_out(h, s["v"], s["g"], p_grp, p_out, j, rows, mod, i, t_mm)
            else:
                dx, s["br"], loss_part = _pool_mix_out(h, s["v"], s["g"], p_grp, p_out, j, rows, mod, i, t_mm,
                                                       loss_target[0])
            weights.append(wts)
        saved.append(s)

    vecs, gates, gains, dsinks, pool_vecs = [None] * DEPTH, [None] * DEPTH, [None] * 2, [None] * 2, [None] * 2
    sent = {}
    token = jnp.zeros((8, LANES), F32)
    for i in reversed(range(DEPTH)):
        j = i // 2
        s = saved[i]
        if i % 2 == 0:
            w_in_t, w_out = weights[i]
            dos, dg, d_w_out, gates[i] = _attn_out_proj_bwd(dx, s["br"], s["o"], s["g"], w_out, j, mod, i, t_mm, token)
            if i == 0:
                sent["0_out"], token = _scatter_start([d_w_out], (0,), "scatter_start_0_out", token)
            dq, dk, dv, dsinks[j] = _attn_backward(attn_sinks, s["qs"], dos, s["kd"], s["vd"], j, token)
            dx, d_in_t, vecs[i], gains[j] = _attn_in_proj_bwd(
                s["x"], dx, rope, s["qk_raw"], dq, dk, dv, dg, rows, mod, i, w_in_t, j, s["gain"], bd, t_bw)
            if i > 0:
                sent[i], token = _scatter_start([d_in_t, d_w_out], (0, 0), f"scatter_start_{i}", token)
        else:
            p_in_t, p_out, p_grp = weights[i]
            dpool, dg, d_p_out, d_p_grp, pool_vecs[j] = _pool_mix_out_bwd(
                dx, s["br"], s["v"], s["g"], p_grp, p_out, j, rows, mod, i, t_mm, token)
            dx, d_in_t, vecs[i] = _pool_in_proj_bwd(s["x"], dx, dpool, dg, rows, mod, i, p_in_t, j, t_bw, token)
            sent[i], token = _scatter_start([d_in_t, d_p_out, d_p_grp], (0, 0, 1), f"scatter_start_{i}", token)

    vec = _build_vec(vecs, gates, pool_vecs, gains, dsinks, loss_part)
    vec_rows = lax.dynamic_update_slice(jnp.zeros((N_DEV * VEC_ROWS, D), F32), vec, (me * VEC_ROWS, 0))
    vec_sent, token = _gather_start([[vec_rows]], [(0,)], loss_part, "vec_gather_start")
    sent["0_in"], token = _scatter_start([d_in_t], (0,), "scatter_start_0_in", token)

    got = {}
    for i in (3, 1):
        fulls, lands = _scatter_wait(sent[i], (0, 0, 1), token, f"scatter_wait_{i}")
        got[i] = dict(zip(("in", "out", "grp"), zip(fulls, lands)))
    pick = lambda ls, kind: ([got[i][kind][0] for i in ls], [got[i][kind][1] for i in ls])
    res = {}
    res["pool_w_in"] = _adamw_shards("adamw_pool_w_in", me, *pick((1, 3), "in"), pool_w_in, m_pool_w_in, v_pool_w_in, True)
    res["pool_w_out"] = _adamw_shards("adamw_pool_w_out", me, *pick((1, 3), "out"), pool_w_out, m_pool_w_out,
                                      v_pool_w_out, False)
    res["pool_w_group"] = _adamw_shards("adamw_pool_w_group", me, *pick((1, 3), "grp"), pool_w_group, m_pool_w_group,
                                        v_pool_w_group, False, axis=1)

    vec_all, = _gather_wait(vec_sent[0], (0,), res["pool_w_group"][0], "vec_gather_wait")
    vec_all = vec_all.reshape(N_DEV, VEC_ROWS, D)
    tot, folded = _sum_devices(vec_all, token)
    loss = tot[24, 0]
    small = dict(
        ada_b=(ada_b, tot[0:12].reshape(DEPTH, 3 * D), m_ada_b, v_ada_b),
        norm_g=(norm_g, tot[12:16], m_norm_g, v_norm_g),
        q_norm=(attn_q_norm, folded[0:2, :HEAD_DIM], m_attn_q_norm, v_attn_q_norm),
        k_norm=(attn_k_norm, folded[2:4, :HEAD_DIM], m_attn_k_norm, v_attn_k_norm),
        sinks=(attn_sinks, tot[20:22, :N_HEADS], m_attn_sinks, v_attn_sinks),
        pool_scale=(pool_scale, lax.dynamic_slice(tot, (22, me * shard), (2, shard)), m_pool_scale, v_pool_scale),
    )
    res.update({k: (a[1],) + upd for (k, a), upd in zip(small.items(), _adamw_small(list(small.values())))})

    dmod_all = vec_all[:, 0:12, :].reshape(N_DEV, DEPTH, 3 * D)
    dmod_mine = lax.dynamic_slice_in_dim(dmod_all, me * cols, cols, axis=2)
    dmod_mine = jnp.pad(jnp.transpose(dmod_mine, (1, 0, 2)), ((0, 0), (0, N_DEV), (0, 0))) + token[0, 0]
    res["ada_w"] = _ada_backward_adamw(jnp.pad(c_all, ((0, N_DEV), (0, 0))), dmod_mine, ada_w, m_ada_w, v_ada_w)

    fulls, lands = _scatter_wait(sent[2], (0, 0), res["ada_w"][0], "scatter_wait_2")
    got[2] = dict(zip(("in", "out"), zip(fulls, lands)))
    got[0] = {}
    for kind in ("out", "in"):
        fulls, lands = _scatter_wait(sent["0_" + kind], (0,), res["ada_w"][0], "scatter_wait_0_" + kind)
        got[0][kind] = (fulls[0], lands[0])
    res["attn_w_out"] = _adamw_shards("adamw_attn_w_out", me, *pick((0, 2), "out"), attn_w_out, m_attn_w_out,
                                      v_attn_w_out, False)
    res["attn_w_in"] = tuple(jnp.swapaxes(a, 1, 2) for a in _adamw_shards(
        "adamw_attn_w_in", me, *pick((0, 2), "in"), w_in_rows, jnp.swapaxes(m_attn_w_in, 1, 2),
        jnp.swapaxes(v_attn_w_in, 1, 2), False))

    order = ("ada_w", "ada_b", "norm_g", "attn_w_in", "q_norm", "k_norm", "sinks", "attn_w_out", "pool_w_in",
             "pool_w_group", "pool_scale", "pool_w_out")
    return (loss, dx[None], *[res[k][0] for k in order], *[res[k][1] for k in order], *[res[k][2] for k in order],
            *[res[k][3] for k in order])
```

```python
import numpy as np
import jax
import jax.numpy as jnp
from jax import lax
from jax.experimental import pallas as pl
from jax.experimental.pallas import tpu as pltpu

F32 = jnp.float32
BF16 = jnp.bfloat16
MESH = pl.DeviceIdType.MESH

N_DEV = 8
D = 1024
DEPTH = 4
HEAD_DIM = 64
N_HEADS = 16
N_KV = 4
QK_W = 1280
ATTN_IN = 2560
POOL_IN = 2048
QBLK = 128
KX_W = N_KV * 128
CHUNK = 256
POOL_WINDOWS = (2, 4, 8, 16)
HALO = 16
ROPE_THETA = 500000.0
ROT_DIM = 16
NORM_EPS = 1e-6
ADAM_LR = 0.001
ADAM_B1 = 0.9
ADAM_B2 = 0.999
ADAM_EPS = 1e-08
ADAM_WD = 0.01
ADAM_STEP = 10

LANES = 128
VMEM_LIMIT = 56 * 2**20
VEC_ROWS = 32


def _cparams(n_grid=0, **kw):
    if n_grid:
        kw["dimension_semantics"] = ("arbitrary",) * n_grid
    return pltpu.CompilerParams(vmem_limit_bytes=VMEM_LIMIT, **kw)


def _call(body, **kw):
    return pl.pallas_call(body, **kw)


def _mod_spec(layer):
    return pl.BlockSpec((None, 8, D), lambda *_: (layer, 0, 0), pipeline_mode=pl.Buffered(1))


def _mod_row_spec(layer, row):
    return pl.BlockSpec((None, None, 1, D), lambda *_: (layer, row, 0, 0), pipeline_mode=pl.Buffered(1))


NORM_ROW, POOL_SCALE_ROW = 3, 4


def _const_spec(shape):
    nd = len(shape)
    return pl.BlockSpec(shape, lambda *_: (0,) * nd, pipeline_mode=pl.Buffered(1))


def _dot(a, b):
    return jnp.dot(a, b, preferred_element_type=F32)


def _dot_nt(a, b):
    return lax.dot_general(a, b, (((1,), (1,)), ((), ())), preferred_element_type=F32)


def _dot_tn(a, b):
    return lax.dot_general(a, b, (((0,), (0,)), ((), ())), preferred_element_type=F32)


def _group_mean(x, m):
    return _dot(x.astype(BF16), m) * (1.0 / HEAD_DIM)


def _sigmoid(g):
    return 1.0 / (1.0 + jnp.exp(-g))


def _norm_mod(x, ng, sc, sh):
    r = lax.rsqrt(jnp.mean(x * x, axis=-1, keepdims=True) + NORM_EPS)
    xh = x * r
    h = (xh * ng) * (1.0 + sc) + sh
    return xh, r, h


def _rope_table(pos_col, invf_row, tile):
    seq = pos_col.shape[0]

    def body(pos_ref, invf_ref, out_ref):
        ang = pos_ref[...].astype(F32) * invf_ref[...]
        l64 = lax.broadcasted_iota(jnp.int32, (tile, LANES), 1) & (HEAD_DIM - 1)
        cs, sn = jnp.cos(ang), jnp.sin(ang)
        out_ref[:, 0:LANES] = jnp.where(l64 < ROT_DIM, cs, 1.0)
        out_ref[:, LANES:2 * LANES] = jnp.where(l64 < ROT_DIM // 2, -sn, 0.0)
        out_ref[:, 2 * LANES:3 * LANES] = jnp.where((l64 >= ROT_DIM // 2) & (l64 < ROT_DIM), sn, 0.0)

    return _call(
        body, name="rope_table", grid=(seq // tile,),
        out_shape=jax.ShapeDtypeStruct((seq, 3 * LANES), F32),
        in_specs=[pl.BlockSpec((tile, 1), lambda i: (i, 0)), _const_spec((1, LANES))],
        out_specs=pl.BlockSpec((tile, 3 * LANES), lambda i: (i, 0)),
        compiler_params=_cparams(1),
    )(pos_col, invf_row)


def _rope_tabs(rope_ref):
    return rope_ref[:, 0:LANES], rope_ref[:, LANES:2 * LANES], rope_ref[:, 2 * LANES:3 * LANES]


def _rope(y, tabs):
    cos_t, sin_a, sin_b = tabs
    return y * cos_t + pltpu.roll(y, LANES - ROT_DIM // 2, 1) * sin_a + pltpu.roll(y, ROT_DIM // 2, 1) * sin_b


def _rope_bwd(dy, tabs):
    cos_t, sin_a, sin_b = tabs
    return dy * cos_t + pltpu.roll(dy * sin_a, ROT_DIM // 2, 1) + pltpu.roll(dy * sin_b, LANES - ROT_DIM // 2, 1)


def _low_half(rows):
    return lax.broadcasted_iota(jnp.int32, (rows, LANES), 1) < HEAD_DIM


def _adamw(w, g, m, v):
    m = ADAM_B1 * m + (1.0 - ADAM_B1) * g
    v = ADAM_B2 * v + (1.0 - ADAM_B2) * (g * g)
    m_hat = m / (1.0 - ADAM_B1 ** ADAM_STEP)
    v_hat = v / (1.0 - ADAM_B2 ** ADAM_STEP)
    delta = -ADAM_LR * (m_hat / (jnp.sqrt(v_hat) + ADAM_EPS) + ADAM_WD * w)
    return delta, m, v


def _my_position():
    x, y, c = lax.axis_index("x"), lax.axis_index("y"), lax.axis_index("c")
    return x, y, c, 4 * x + 2 * y + c


def _peers(x, y, c):
    out = []
    for k in range(1, N_DEV):
        px = 1 - x if k & 4 else x
        py = 1 - y if k & 2 else y
        pc = 1 - c if k & 1 else c
        out.append(((px, py, pc), 4 * px + 2 * py + pc))
    return out


def _allgather_small(v, name, after):
    rows, cols = v.shape

    def body(v_ref, after_ref, out_ref, send_sems, recv_sems, local_sem):
        x, y, c, me = _my_position()
        local = pltpu.make_async_copy(v_ref, out_ref.at[me], local_sem)
        local.start()
        sends = []
        for k, (peer, _) in enumerate(_peers(x, y, c)):
            cp = pltpu.make_async_remote_copy(v_ref, out_ref.at[me], send_sems.at[k], recv_sems.at[k],
                                              device_id=peer, device_id_type=MESH)
            cp.start()
            sends.append(cp)
        for k, (peer, idx) in enumerate(_peers(x, y, c)):
            pltpu.make_async_remote_copy(v_ref, out_ref.at[idx], send_sems.at[k], recv_sems.at[k],
                                         device_id=peer, device_id_type=MESH).wait_recv()
        for cp in sends:
            cp.wait_send()
        local.wait()

    return _call(
        body, name=name,
        out_shape=jax.ShapeDtypeStruct((N_DEV, rows, cols), F32),
        in_specs=[pl.BlockSpec(memory_space=pltpu.VMEM), pl.BlockSpec(memory_space=pl.ANY)],
        out_specs=pl.BlockSpec(memory_space=pltpu.VMEM),
        scratch_shapes=[pltpu.SemaphoreType.DMA((N_DEV - 1,)), pltpu.SemaphoreType.DMA((N_DEV - 1,)),
                        pltpu.SemaphoreType.DMA(())],
        compiler_params=_cparams(),
    )(v, after)


def _shard_rows(ref, idx, rows, axis):
    sl = [slice(None)] * len(ref.shape)
    sl[axis] = pl.ds(idx * rows, rows)
    return ref.at[tuple(sl)]


def _own_and_peer_rows(ref, me, idx, axis):
    rows = ref.shape[axis] // N_DEV
    return _shard_rows(ref, me, rows, axis), _shard_rows(ref, idx, rows, axis)


HBM_SPEC = pl.BlockSpec(memory_space=pltpu.HBM)
SEM_SPEC = pl.BlockSpec(memory_space=pltpu.SEMAPHORE)
ANY_SPEC = pl.BlockSpec(memory_space=pl.ANY)
DATAFLOW = pltpu.SideEffectType.DATAFLOW_SIDE_EFFECTING


def _hbm(a):
    return pltpu.with_memory_space_constraint(a, pltpu.HBM)


def _gather_start(layers, axes, after, name):
    flat = [a for arrs in layers for a in arrs]
    flat_axes = [ax for axs in axes for ax in axs]
    n, nl = len(flat), len(layers)

    def body(*refs):
        ins, sems, token = refs[:n], refs[n + 1:n + 1 + 2 * nl], refs[-1]
        x, y, c, me = _my_position()
        a0 = 0
        for li, arrs in enumerate(layers):
            for k, (peer, _) in enumerate(_peers(x, y, c)):
                for a in range(len(arrs)):
                    rows, _ = _own_and_peer_rows(ins[a0 + a], me, me, flat_axes[a0 + a])
                    pltpu.make_async_remote_copy(rows, rows, sems[2 * li].at[k * len(arrs) + a],
                                                 sems[2 * li + 1].at[k * len(arrs) + a],
                                                 device_id=peer, device_id_type=MESH).start()
            a0 += len(arrs)
        token[...] = jnp.zeros_like(token)

    sem_shapes = []
    for arrs in layers:
        sem_shapes += [pltpu.SemaphoreType.DMA(((N_DEV - 1) * len(arrs),))] * 2
    out = _call(
        body, name=name,
        out_shape=(*sem_shapes, *[pltpu.HBM(a.shape, a.dtype) for a in flat], jax.ShapeDtypeStruct((8, LANES), F32)),
        in_specs=[HBM_SPEC] * n + [ANY_SPEC],
        out_specs=(*[SEM_SPEC] * (2 * nl), *[HBM_SPEC] * n, pl.BlockSpec(memory_space=pltpu.VMEM)),
        input_output_aliases={a: 2 * nl + a for a in range(n)},
        compiler_params=_cparams(has_side_effects=DATAFLOW),
    )(*[_hbm(a) for a in flat], after)
    per_layer, a0 = [], 0
    for li, arrs in enumerate(layers):
        per_layer.append((out[2 * li], out[2 * li + 1], list(out[2 * nl + a0:2 * nl + a0 + len(arrs)])))
        a0 += len(arrs)
    return per_layer, out[-1]


def _gather_wait(started, axes, after, name):
    send_sems, recv_sems, arrs = started
    n = len(arrs)

    def body(*refs):
        ins, send_ref, recv_ref = refs[:n], refs[n], refs[n + 1]
        x, y, c, me = _my_position()
        for k, (peer, idx) in enumerate(_peers(x, y, c)):
            for a in range(n):
                own, theirs = _own_and_peer_rows(ins[a], me, idx, axes[a])
                cp = pltpu.make_async_remote_copy(own, theirs, send_ref.at[k * n + a], recv_ref.at[k * n + a],
                                                  device_id=peer, device_id_type=MESH)
                cp.wait_send()
                cp.wait_recv()

    return _call(
        body, name=name,
        out_shape=tuple(pltpu.HBM(a.shape, a.dtype) for a in arrs),
        in_specs=[HBM_SPEC] * n + [SEM_SPEC, SEM_SPEC, ANY_SPEC],
        out_specs=tuple([HBM_SPEC] * n),
        input_output_aliases={a: a for a in range(n)},
        compiler_params=_cparams(has_side_effects=DATAFLOW),
    )(*arrs, send_sems, recv_sems, after)


def _first_relations(x, y, c):
    return [(x, y, 1 - c), (1 - x, y, c), (x, 1 - y, c), (1 - x, 1 - y, c)]


def _gather_first_start(arr, after):
    n_rel = 4

    def body(a_ref, after_ref, send_ref, recv_ref, thru, token):
        x, y, c, me = _my_position()
        rows, _ = _own_and_peer_rows(a_ref, me, me, 0)
        for k, peer in enumerate(_first_relations(x, y, c)):
            pltpu.make_async_remote_copy(rows, rows, send_ref.at[k], recv_ref.at[k], device_id=peer, device_id_type=MESH).start()
        token[...] = jnp.zeros_like(token)

    sem = pltpu.SemaphoreType.DMA((n_rel,))
    out = _call(
        body, name="gather_first_start",
        out_shape=(sem, sem, pltpu.HBM(arr.shape, arr.dtype), jax.ShapeDtypeStruct((8, LANES), F32)),
        in_specs=[HBM_SPEC, ANY_SPEC],
        out_specs=(SEM_SPEC, SEM_SPEC, HBM_SPEC, pl.BlockSpec(memory_space=pltpu.VMEM)),
        input_output_aliases={0: 2},
        compiler_params=_cparams(has_side_effects=DATAFLOW),
    )(_hbm(arr), after)
    return out[:3], out[3]


def _gather_first_forward(started, after):
    send_a, recv_a, arr = started

    def body(a_ref, send_a_ref, recv_a_ref, after_ref, send_b_ref, recv_b_ref, thru, token):
        x, y, c, me = _my_position()
        sibling = (x, y, 1 - c)
        for k, peer in enumerate(_first_relations(x, y, c)):
            own, theirs = _own_and_peer_rows(a_ref, me, 4 * peer[0] + 2 * peer[1] + peer[2], 0)
            cp = pltpu.make_async_remote_copy(own, theirs, send_a_ref.at[k], recv_a_ref.at[k], device_id=peer, device_id_type=MESH)
            cp.wait_send()
            cp.wait_recv()
            if k > 0:
                pltpu.make_async_remote_copy(theirs, theirs, send_b_ref.at[k - 1], recv_b_ref.at[k - 1],
                                             device_id=sibling, device_id_type=MESH).start()
        token[...] = jnp.zeros_like(token)

    sem = pltpu.SemaphoreType.DMA((3,))
    out = _call(
        body, name="gather_first_forward",
        out_shape=(sem, sem, pltpu.HBM(arr.shape, arr.dtype), jax.ShapeDtypeStruct((8, LANES), F32)),
        in_specs=[HBM_SPEC, SEM_SPEC, SEM_SPEC, ANY_SPEC],
        out_specs=(SEM_SPEC, SEM_SPEC, HBM_SPEC, pl.BlockSpec(memory_space=pltpu.VMEM)),
        input_output_aliases={0: 2},
        compiler_params=_cparams(has_side_effects=DATAFLOW),
    )(arr, send_a, recv_a, after)
    return out[:3], out[3]


def _gather_first_wait(forwarded, after):
    send_b, recv_b, arr = forwarded

    def body(a_ref, send_b_ref, recv_b_ref, after_ref, thru):
        x, y, c, me = _my_position()
        sibling = (x, y, 1 - c)
        for k, peer in enumerate(_first_relations(x, y, c)[1:]):
            _, sent = _own_and_peer_rows(a_ref, me, 4 * peer[0] + 2 * peer[1] + peer[2], 0)
            _, got = _own_and_peer_rows(a_ref, me, 4 * peer[0] + 2 * peer[1] + (1 - peer[2]), 0)
            cp = pltpu.make_async_remote_copy(sent, got, send_b_ref.at[k], recv_b_ref.at[k], device_id=sibling, device_id_type=MESH)
            cp.wait_send()
            cp.wait_recv()

    return _call(
        body, name="gather_first_wait",
        out_shape=pltpu.HBM(arr.shape, arr.dtype),
        in_specs=[HBM_SPEC, SEM_SPEC, SEM_SPEC, ANY_SPEC],
        out_specs=HBM_SPEC,
        input_output_aliases={0: 0},
        compiler_params=_cparams(has_side_effects=DATAFLOW),
    )(arr, send_b, recv_b, after)


def _scatter_start(fulls, axes, name, after):
    n = len(fulls)
    lands = []
    for f, ax in zip(fulls, axes):
        shp = list(f.shape)
        shp[ax] //= N_DEV
        lands.append(_hbm(lax.empty((N_DEV - 1,) + tuple(shp), f.dtype)))

    def body(*refs):
        srcs, dsts, send_ref, recv_ref, token = refs[:n], refs[n:2 * n], refs[2 * n + 1], refs[2 * n + 2], refs[-1]
        x, y, c, me = _my_position()
        for k, (peer, idx) in enumerate(_peers(x, y, c)):
            for a in range(n):
                _, theirs = _own_and_peer_rows(srcs[a], me, idx, axes[a])
                pltpu.make_async_remote_copy(theirs, dsts[a].at[k], send_ref.at[k * n + a], recv_ref.at[k * n + a],
                                             device_id=peer, device_id_type=MESH).start()
        token[...] = jnp.zeros_like(token)

    sem = pltpu.SemaphoreType.DMA(((N_DEV - 1) * n,))
    out = _call(
        body, name=name,
        out_shape=(sem, sem, *[pltpu.HBM(a.shape, a.dtype) for a in fulls], *[pltpu.HBM(a.shape, a.dtype) for a in lands],
                   jax.ShapeDtypeStruct((8, LANES), F32)),
        in_specs=[HBM_SPEC] * (2 * n) + [ANY_SPEC],
        out_specs=(SEM_SPEC, SEM_SPEC, *[HBM_SPEC] * (2 * n), pl.BlockSpec(memory_space=pltpu.VMEM)),
        input_output_aliases={a: 2 + a for a in range(2 * n)},
        compiler_params=_cparams(has_side_effects=DATAFLOW),
    )(*[_hbm(a) for a in fulls], *lands, after)
    return (out[0], out[1], list(out[2:2 + n]), list(out[2 + n:2 + 2 * n])), out[-1]


def _scatter_wait(started, axes, after, name):
    send_sems, recv_sems, fulls, lands = started
    n = len(fulls)

    def body(*refs):
        srcs, dsts, send_ref, recv_ref = refs[:n], refs[n:2 * n], refs[2 * n], refs[2 * n + 1]
        x, y, c, me = _my_position()
        for k, (peer, idx) in enumerate(_peers(x, y, c)):
            for a in range(n):
                _, theirs = _own_and_peer_rows(srcs[a], me, idx, axes[a])
                cp = pltpu.make_async_remote_copy(theirs, dsts[a].at[k], send_ref.at[k * n + a], recv_ref.at[k * n + a],
                                                  device_id=peer, device_id_type=MESH)
                cp.wait_send()
                cp.wait_recv()

    out = _call(
        body, name=name,
        out_shape=tuple(pltpu.HBM(a.shape, a.dtype) for a in (*fulls, *lands)),
        in_specs=[HBM_SPEC] * (2 * n) + [SEM_SPEC, SEM_SPEC, ANY_SPEC],
        out_specs=tuple([HBM_SPEC] * (2 * n)),
        input_output_aliases={a: a for a in range(2 * n)},
        compiler_params=_cparams(has_side_effects=DATAFLOW),
    )(*fulls, *lands, send_sems, recv_sems, after)
    return list(out[:n]), list(out[n:])


def _prep_weights(me, items, name):
    def body(me_ref, *refs):
        for (_, _, kind), src, dst in zip(items, refs[:len(items)], refs[len(items):]):
            dst[...] = (src[...].T if kind == "T" else src[...]).astype(BF16)

    ins, in_specs, out_shapes, out_specs = [], [], [], []
    for src, j, kind in items:
        shard = src.shape[1:]
        ins.append(src)
        in_specs.append(pl.BlockSpec((None,) + tuple(shard), lambda i, me_ref, j=j, nd=len(shard): (j,) + (0,) * nd))
        if kind == "G":
            out_shapes.append((shard[0], N_DEV * shard[1], shard[2]))
            out_specs.append(pl.BlockSpec(tuple(shard), lambda i, me_ref: (0, me_ref[0], 0)))
        else:
            rows = shard[1] if kind == "T" else shard[0]
            out_shapes.append((N_DEV * rows, D))
            out_specs.append(pl.BlockSpec((rows, D), lambda i, me_ref: (me_ref[0], 0)))
    out = _call(
        body, name=name,
        grid_spec=pltpu.PrefetchScalarGridSpec(num_scalar_prefetch=1, grid=(1,), in_specs=in_specs, out_specs=tuple(out_specs)),
        out_shape=tuple(jax.ShapeDtypeStruct(s, BF16) for s in out_shapes),
        compiler_params=_cparams(1),
    )(me.reshape(1), *ins)
    return list(out)


def _ada_forward(c_all, ada_w):
    cols = ada_w.shape[2]

    def body(c_ref, w_ref, o_ref):
        cv = c_ref[...]
        sc = (cv * _sigmoid(cv)).astype(BF16)
        o_ref[...] = _dot(sc, w_ref[...].astype(BF16))

    return _call(
        body, name="ada_forward", grid=(DEPTH,),
        out_shape=jax.ShapeDtypeStruct((DEPTH, N_DEV, cols), F32),
        in_specs=[pl.BlockSpec((N_DEV, D), lambda i: (0, 0)), pl.BlockSpec((None, D, cols), lambda i: (i, 0, 0))],
        out_specs=pl.BlockSpec((None, N_DEV, cols), lambda i: (i, 0, 0)),
        compiler_params=_cparams(1),
    )(c_all, ada_w)


def _ada_backward_adamw(c_pad, dmod_pad, w, m, v):
    cols = w.shape[2]

    def body(c_ref, dm_ref, w_ref, m_ref, v_ref, g_out, d_out, m_out, v_out):
        cv = c_ref[...]
        sc = (cv * _sigmoid(cv)).astype(BF16)
        g = _dot_tn(sc, dm_ref[...].astype(BF16))
        g_out[...] = g
        d_out[...], m_out[...], v_out[...] = _adamw(w_ref[...], g, m_ref[...], v_ref[...])

    wspec = pl.BlockSpec((None, D, cols), lambda i: (i, 0, 0))
    return _call(
        body, name="ada_backward_adamw", grid=(DEPTH,),
        out_shape=tuple(jax.ShapeDtypeStruct(w.shape, F32) for _ in range(4)),
        in_specs=[pl.BlockSpec((2 * N_DEV, D), lambda i: (0, 0)), pl.BlockSpec((None, 2 * N_DEV, cols), lambda i: (i, 0, 0)),
                  wspec, wspec, wspec],
        out_specs=(wspec, wspec, wspec, wspec),
        compiler_params=_cparams(1),
    )(c_pad, dmod_pad, w, m, v)


def _attn_in_proj(x, rope, rows, mod, layer, w_t, j, gain, bd, tile):
    seq = x.shape[0]

    def body(x_ref, rope_ref, ng_ref, mod_ref, w_ref, gain_ref, bd_ref, qk_ref, qs_ref, kd_ref, vd_ref, g_ref):
        _, _, h = _norm_mod(x_ref[...], ng_ref[...], mod_ref[1:2, :], mod_ref[0:1, :])
        hb = h.astype(BF16)
        tabs = _rope_tabs(rope_ref)
        low = _low_half(tile)
        bdm = bd_ref[...]

        def put_kv(ref, blk, first_kv):
            sw = pltpu.roll(blk, HEAD_DIM, 1)
            ref[:, LANES * first_kv:LANES * (first_kv + 1)] = jnp.where(low, blk, sw).astype(BF16)
            ref[:, LANES * (first_kv + 1):LANES * (first_kv + 2)] = jnp.where(low, sw, blk).astype(BF16)

        def project(c):
            return _dot_nt(hb, w_ref[CHUNK * c:CHUNK * (c + 1), :])

        n_chunks = ATTN_IN // CHUNK
        per = CHUNK // LANES
        nxt = project(0)
        for c in range(n_chunks):
            cur = nxt
            if c + 1 < n_chunks:
                nxt = project(c + 1)
            col = CHUNK * c
            if col >= QK_W + N_KV * HEAD_DIM:
                g_ref[:, col - QK_W - N_KV * HEAD_DIM:col - QK_W - N_KV * HEAD_DIM + CHUNK] = cur.astype(BF16)
            elif col >= QK_W:
                for t in range(per):
                    put_kv(vd_ref, cur[:, LANES * t:LANES * (t + 1)], (col - QK_W) // HEAD_DIM + 2 * t)
            else:
                qk_ref[:, col:col + CHUNK] = cur
                for t in range(per):
                    b = per * c + t
                    blk = cur[:, LANES * t:LANES * (t + 1)]
                    ms = _group_mean(blk * blk, bdm)
                    y = (blk * lax.rsqrt(ms + NORM_EPS)) * gain_ref[:, LANES * b:LANES * (b + 1)]
                    rp = _rope(y, tabs)
                    if b < D // LANES:
                        rp = rp * (HEAD_DIM ** -0.5)
                        qs_ref[:, 2 * LANES * b:2 * LANES * b + LANES] = jnp.where(low, rp, 0.0).astype(BF16)
                        qs_ref[:, 2 * LANES * b + LANES:2 * LANES * (b + 1)] = jnp.where(low, 0.0, rp).astype(BF16)
                    else:
                        put_kv(kd_ref, rp, 2 * (b - D // LANES))

    row = lambda w: pl.BlockSpec((tile, w), lambda i: (i, 0))
    return _call(
        body, name=f"attn_in_proj_{j}", grid=(seq // tile,),
        out_shape=(jax.ShapeDtypeStruct((seq, QK_W), F32), jax.ShapeDtypeStruct((seq, N_HEADS * LANES), BF16),
                   jax.ShapeDtypeStruct((seq, KX_W), BF16), jax.ShapeDtypeStruct((seq, KX_W), BF16),
                   jax.ShapeDtypeStruct((seq, D), BF16)),
        in_specs=[row(D), row(3 * LANES), _mod_row_spec(layer, NORM_ROW), _mod_spec(layer), _const_spec((ATTN_IN, D)),
                  _const_spec((1, QK_W)), _const_spec((LANES, LANES))],
        out_specs=(row(QK_W), row(N_HEADS * LANES), row(KX_W), row(KX_W), row(D)),
        compiler_params=_cparams(1),
    )(x, rope, rows, mod, w_t, gain, bd)


def _band_mask(n, rows, keys_on_rows):
    shape = (2 * QBLK, rows) if keys_on_rows else (rows, 2 * QBLK)
    qi = lax.broadcasted_iota(jnp.int32, shape, 1 if keys_on_rows else 0) & (QBLK - 1)
    kj = lax.broadcasted_iota(jnp.int32, shape, 0 if keys_on_rows else 1)
    diff = QBLK + qi - kj
    first_key = jnp.where(n > 0, 0, QBLK)
    return (diff >= 0) & (diff < QBLK) & (kj >= first_key)


def _pair_up(st, low):
    return jnp.concatenate([jnp.where(low, st[0:QBLK], st[QBLK:2 * QBLK]),
                            jnp.where(low, st[2 * QBLK:3 * QBLK], st[3 * QBLK:4 * QBLK])], axis=1)


def _attn_forward(sinks, qs, kd, vd, j):
    seq = qs.shape[0]
    per = 2
    nb = seq // (per * QBLK)

    def body(sink_ref, q_ref, kp_ref, kc_ref, vp_ref, vc_ref, o_ref):
        n = pl.program_id(0)
        low = _low_half(QBLK)
        rowi = lax.broadcasted_iota(jnp.int32, (4 * QBLK, 1), 0)
        groups = per * N_KV

        def keys(p_ref, c_ref, blk, kv):
            cols = slice(LANES * kv, LANES * (kv + 1))
            if blk == 0:
                return jnp.concatenate([p_ref[:, cols], c_ref[0:QBLK, cols]], axis=0)
            return c_ref[QBLK * (blk - 1):QBLK * (blk + 1), cols]

        def scores(g):
            blk, kv = divmod(g, N_KV)
            q = jnp.concatenate([q_ref[QBLK * blk:QBLK * (blk + 1), LANES * h:LANES * (h + 1)]
                                 for h in range(4 * kv, 4 * kv + 4)], axis=0)
            return _dot_nt(q, keys(kp_ref, kc_ref, blk, kv))

        nxt = scores(0)
        for g in range(groups):
            blk, kv = divmod(g, N_KV)
            ok = _band_mask(n if blk == 0 else 1, 4 * QBLK, False)
            s = jnp.where(ok, nxt, -1e30)
            if g + 1 < groups:
                nxt = scores(g + 1)
            sink = jnp.where(rowi < QBLK, sink_ref[j, 4 * kv],
                             jnp.where(rowi < 2 * QBLK, sink_ref[j, 4 * kv + 1],
                                       jnp.where(rowi < 3 * QBLK, sink_ref[j, 4 * kv + 2], sink_ref[j, 4 * kv + 3])))
            m = jnp.maximum(jnp.max(s, axis=1, keepdims=True), sink)
            p = jnp.exp(s - m)
            den = jnp.sum(p, axis=1, keepdims=True) + jnp.exp(sink - m)
            o_st = _dot((p / den).astype(BF16), keys(vp_ref, vc_ref, blk, kv))
            o_ref[QBLK * blk:QBLK * (blk + 1), 2 * LANES * kv:2 * LANES * (kv + 1)] = _pair_up(o_st, low).astype(BF16)

    cur = lambda w: pl.BlockSpec((per * QBLK, w), lambda n: (n, 0))
    prev = lambda w: pl.BlockSpec((QBLK, w), lambda n: (jnp.maximum(per * n - 1, 0), 0))
    return _call(
        body, name=f"attn_forward_{j}", grid=(nb,),
        out_shape=jax.ShapeDtypeStruct((seq, D), BF16),
        in_specs=[pl.BlockSpec(memory_space=pltpu.SMEM), cur(N_HEADS * LANES), prev(KX_W), cur(KX_W), prev(KX_W), cur(KX_W)],
        out_specs=cur(D),
        compiler_params=_cparams(1),
    )(sinks, qs, kd, kd, vd, vd)


def _attn_out_proj(x, o, g, w, j, mod, layer, tile):
    seq = x.shape[0]

    def body(x_ref, o_ref, g_ref, w_ref, mod_ref, xo_ref, br_ref):
        gv = g_ref[...].astype(F32)
        u = (o_ref[...].astype(F32) * (gv * _sigmoid(gv))).astype(BF16)
        br = _dot(u, w_ref[...])
        br_ref[...] = br.astype(BF16)
        xo_ref[...] = x_ref[...] + mod_ref[2:3, :] * br

    row = pl.BlockSpec((tile, D), lambda i: (i, 0))
    return _call(
        body, name=f"attn_out_proj_{j}", grid=(seq // tile,),
        out_shape=(jax.ShapeDtypeStruct((seq, D), F32), jax.ShapeDtypeStruct((seq, D), BF16)),
        in_specs=[row, row, row, _const_spec((D, D)), _mod_spec(layer)],
        out_specs=(row, row),
        compiler_params=_cparams(1),
    )(x, o, g, w, mod)


def _attn_out_proj_bwd(dxn, br, o, g, w, j, mod, layer, tile, after):
    seq = dxn.shape[0]
    steps = seq // tile

    def body(dxn_ref, br_ref, o_ref, g_ref, w_ref, mod_ref, after_ref, do_ref, dg_ref, dw_ref, dgate_ref, dw_acc):
        i = pl.program_id(0)

        @pl.when(i == 0)
        def _():
            dw_acc[...] = jnp.zeros_like(dw_acc)
            dgate_ref[...] = jnp.zeros_like(dgate_ref)

        dxn_v, ov, gv = dxn_ref[...], o_ref[...].astype(F32), g_ref[...].astype(F32)
        dgate_ref[...] += jnp.sum(dxn_v * br_ref[...].astype(F32), axis=0, keepdims=True)
        dbr = (dxn_v * mod_ref[2:3, :]).astype(BF16)
        du = _dot_nt(dbr, w_ref[...])
        sg = _sigmoid(gv)
        sl = gv * sg
        dw_acc[...] += _dot_tn((ov * sl).astype(BF16), dbr)
        do = du * sl
        dg_ref[...] = (du * ov * (sg * (1.0 + gv * (1.0 - sg)))).astype(BF16)
        low = _low_half(tile)
        for b in range(D // LANES):
            blk = do[:, LANES * b:LANES * (b + 1)]
            do_ref[:, 2 * LANES * b:2 * LANES * b + LANES] = jnp.where(low, blk, 0.0).astype(BF16)
            do_ref[:, 2 * LANES * b + LANES:2 * LANES * (b + 1)] = jnp.where(low, 0.0, blk).astype(BF16)

        @pl.when(i == steps - 1)
        def _():
            dw_ref[...] = dw_acc[...].astype(BF16)

    row = lambda w_: pl.BlockSpec((tile, w_), lambda i: (i, 0))
    return _call(
        body, name=f"attn_out_proj_bwd_{j}", grid=(steps,),
        out_shape=(jax.ShapeDtypeStruct((seq, N_HEADS * LANES), BF16), jax.ShapeDtypeStruct((seq, D), BF16),
                   jax.ShapeDtypeStruct((D, D), BF16), jax.ShapeDtypeStruct((1, D), F32)),
        in_specs=[row(D), row(D), row(D), row(D), _const_spec((D, D)), _mod_spec(layer), ANY_SPEC],
        out_specs=(row(N_HEADS * LANES), row(D), pl.BlockSpec((D, D), lambda i: (0, 0)),
                   pl.BlockSpec((1, D), lambda i: (0, 0))),
        scratch_shapes=[pltpu.VMEM((D, D), F32)],
        compiler_params=_cparams(1),
    )(dxn, br, o, g, w, mod, after)


def _attn_backward(sinks, qs, dos, kd, vd, j, after):
    seq = qs.shape[0]
    per = 2
    nb = seq // (per * QBLK)
    kw = N_KV * HEAD_DIM

    def body(sink_ref, q_ref, do_ref, kp_ref, kc_ref, vp_ref, vc_ref, after_ref,
             dq_ref, dk_even, dk_odd, dv_even, dv_odd, dsink_ref, carry_k, carry_v, sink_acc):
        n = pl.program_id(0)

        @pl.when(n == 0)
        def _():
            carry_k[...] = jnp.zeros_like(carry_k)
            carry_v[...] = jnp.zeros_like(carry_v)
            sink_acc[...] = jnp.zeros_like(sink_acc)

        @pl.when(n < nb)
        def _():
            low = _low_half(QBLK)
            lane_q = lax.broadcasted_iota(jnp.int32, (1, 2 * QBLK), 1)
            groups = per * 2 * N_KV

            def keys(p_ref, c_ref, blk, kv):
                cols = slice(LANES * kv, LANES * (kv + 1))
                if blk == 0:
                    return jnp.concatenate([p_ref[:, cols], c_ref[0:QBLK, cols]], axis=0)
                return c_ref[QBLK * (blk - 1):QBLK * (blk + 1), cols]

            def first_products(g):
                blk, rest = divmod(g, 2 * N_KV)
                kv, half = divmod(rest, 2)
                heads = (4 * kv + half, 4 * kv + 2 + half)
                rows = slice(QBLK * blk, QBLK * (blk + 1))
                q = jnp.concatenate([q_ref[rows, LANES * h:LANES * (h + 1)] for h in heads], axis=0)
                do = jnp.concatenate([do_ref[rows, LANES * h:LANES * (h + 1)] for h in heads], axis=0)
                kk = keys(kp_ref, kc_ref, blk, kv)
                return heads, q, do, kk, _dot_nt(kk, q), _dot_nt(keys(vp_ref, vc_ref, blk, kv), do)

            nxt = first_products(0)
            dk_parts, dv_parts = [[], []], [[], []]
            dq_h, dk_kv, dv_kv = [], None, None
            for g in range(groups):
                blk, rest = divmod(g, 2 * N_KV)
                heads, q, do, kk, s_raw, dp_raw = nxt
                if g + 1 < groups:
                    nxt = first_products(g + 1)
                ok = _band_mask(n if blk == 0 else 1, 2 * QBLK, True)
                st = jnp.where(ok, s_raw, -1e30)
                sink = jnp.where(lane_q < QBLK, sink_ref[j, heads[0]], sink_ref[j, heads[1]])
                m = jnp.maximum(jnp.max(st, axis=0, keepdims=True), sink)
                e = jnp.exp(st - m)
                e_sink = jnp.exp(sink - m)
                inv = 1.0 / (jnp.sum(e, axis=0, keepdims=True) + e_sink)
                p = e * inv
                pdp = p * dp_raw
                delta = jnp.sum(pdp, axis=0, keepdims=True)
                ds = (pdp - p * delta).astype(BF16)
                sink_acc[rest:rest + 1, :] -= e_sink * inv * delta
                dk_g, dv_g = _dot(ds, q), _dot(p.astype(BF16), do)
                dk_kv = dk_g if dk_kv is None else dk_kv + dk_g
                dv_kv = dv_g if dv_kv is None else dv_kv + dv_g
                dq_h.append(_dot_tn(ds, kk))
                if g % 2 == 1:
                    kv = rest // 2
                    for t in range(2):
                        dq_ref[QBLK * blk:QBLK * (blk + 1), LANES * (2 * kv + t):LANES * (2 * kv + t + 1)] = jnp.where(
                            low, dq_h[0][QBLK * t:QBLK * (t + 1)], dq_h[1][QBLK * t:QBLK * (t + 1)])
                    dk_parts[blk].append(dk_kv + pltpu.roll(dk_kv, HEAD_DIM, 1))
                    dv_parts[blk].append(dv_kv + pltpu.roll(dv_kv, HEAD_DIM, 1))
                    dq_h, dk_kv, dv_kv = [], None, None

            def order(parts, lo, hi):
                return jnp.concatenate([jnp.where(low, parts[0][lo:hi], parts[1][lo:hi]),
                                        jnp.where(low, parts[2][lo:hi], parts[3][lo:hi])], axis=1)

            dk_odd[...] = carry_k[...] + order(dk_parts[0], 0, QBLK)
            dv_odd[...] = (carry_v[...] + order(dv_parts[0], 0, QBLK)).astype(BF16)
            dk_even[...] = order(dk_parts[0], QBLK, 2 * QBLK) + order(dk_parts[1], 0, QBLK)
            dv_even[...] = (order(dv_parts[0], QBLK, 2 * QBLK) + order(dv_parts[1], 0, QBLK)).astype(BF16)
            carry_k[...] = order(dk_parts[1], QBLK, 2 * QBLK)
            carry_v[...] = order(dv_parts[1], QBLK, 2 * QBLK)

        @pl.when(n == nb)
        def _():
            dk_odd[...] = carry_k[...]
            dv_odd[...] = carry_v[...].astype(BF16)
            lane = lax.broadcasted_iota(jnp.int32, (1, LANES), 1)
            out = jnp.zeros((1, LANES), F32)
            for g in range(2 * N_KV):
                for t in range(2):
                    tot = jnp.sum(sink_acc[g:g + 1, QBLK * t:QBLK * (t + 1)], axis=1, keepdims=True)
                    out = jnp.where(lane == 4 * (g // 2) + 2 * t + g % 2, tot, out)
            dsink_ref[...] = out

    cur = lambda w: pl.BlockSpec((per * QBLK, w), lambda n: (jnp.minimum(n, nb - 1), 0))
    prev = lambda w: pl.BlockSpec((QBLK, w), lambda n: (jnp.maximum(per * n - 1, 0), 0))
    even = pl.BlockSpec((None, QBLK, kw), lambda n: (jnp.minimum(n, nb - 1), 0, 0))
    odd = pl.BlockSpec((None, QBLK, kw), lambda n: (jnp.maximum(n - 1, 0), 0, 0))
    halves = lambda dt: jax.ShapeDtypeStruct((nb, QBLK, kw), dt)
    dq, dk_e, dk_o, dv_e, dv_o, dsink = _call(
        body, name=f"attn_backward_{j}", grid=(nb + 1,),
        out_shape=(jax.ShapeDtypeStruct((seq, D), F32), halves(F32), halves(F32), halves(BF16), halves(BF16),
                   jax.ShapeDtypeStruct((1, LANES), F32)),
        in_specs=[pl.BlockSpec(memory_space=pltpu.SMEM), cur(N_HEADS * LANES), cur(N_HEADS * LANES), prev(KX_W), cur(KX_W),
                  prev(KX_W), cur(KX_W), ANY_SPEC],
        out_specs=(cur(D), even, odd, even, odd, pl.BlockSpec((1, LANES), lambda n: (0, 0))),
        scratch_shapes=[pltpu.VMEM((QBLK, kw), F32), pltpu.VMEM((QBLK, kw), F32), pltpu.VMEM((2 * N_KV, 2 * QBLK), F32)],
        compiler_params=_cparams(1),
    )(sinks, qs, dos, kd, kd, vd, vd, after)
    return dq, (dk_e, dk_o), (dv_e, dv_o), dsink


def _in_proj_tail(x_ref, dxn_ref, ng_ref, mod_ref, w_ref, dproj, dx_ref, dw_acc, vec_acc):
    ng, sc, sh = ng_ref[...], mod_ref[1:2, :], mod_ref[0:1, :]
    xh, r, h = _norm_mod(x_ref[...], ng, sc, sh)
    dh = _dot(dproj, w_ref[...])
    dw_acc[...] += _dot_tn(dproj, h.astype(BF16))
    vec_acc[0:1, :] += jnp.sum(dh, axis=0, keepdims=True)
    vec_acc[1:2, :] += jnp.sum(dh * xh, axis=0, keepdims=True)
    dxh = dh * (ng * (1.0 + sc))
    dx_ref[...] = dxn_ref[...] + r * (dxh - xh * jnp.mean(dxh * xh, axis=-1, keepdims=True))


def _tail_finish(ng_ref, mod_ref, dw_ref, vec_ref, dw_acc, vec_acc):
    dw_ref[...] = dw_acc[...].astype(BF16)
    a = vec_acc[1:2, :]
    vec_ref[...] = jnp.zeros_like(vec_ref)
    vec_ref[0:1, :] = vec_acc[0:1, :]
    vec_ref[1:2, :] = a * ng_ref[...]
    vec_ref[3:4, :] = a * (1.0 + mod_ref[1:2, :])


def _attn_in_proj_bwd(x, dxn, rope, qk_raw, dq, dk, dv, dg, rows, mod, layer, w_t, j, gain, bd, tile):
    seq = x.shape[0]
    steps = seq // tile

    assert tile == 2 * QBLK
    (dk_e, dk_o), (dv_e, dv_o) = dk, dv

    def body(x_ref, dxn_ref, rope_ref, qk_ref, dq_ref, dke_ref, dko_ref, dve_ref, dvo_ref, dg_ref, ng_ref, mod_ref, w_ref, gain_ref,
             bd_ref, dx_ref, dw_ref, vec_ref, dgain_ref, dproj, dw_acc, vec_acc):
        i = pl.program_id(0)

        @pl.when(i == 0)
        def _():
            dw_acc[...] = jnp.zeros_like(dw_acc)
            vec_acc[...] = jnp.zeros_like(vec_acc)
            dgain_ref[...] = jnp.zeros_like(dgain_ref)

        tabs = _rope_tabs(rope_ref)
        bdm = bd_ref[...]
        for b in range(QK_W // LANES):
            cols = slice(LANES * b, LANES * (b + 1))
            raw = qk_ref[:, cols]
            if b < D // LANES:
                dy = dq_ref[:, cols] * (HEAD_DIM ** -0.5)
            else:
                kcols = slice(LANES * (b - D // LANES), LANES * (b + 1 - D // LANES))
                dy = jnp.concatenate([dke_ref[:, kcols], dko_ref[:, kcols]], axis=0)
            dy = _rope_bwd(dy, tabs)
            rr = lax.rsqrt(_group_mean(raw * raw, bdm) + NORM_EPS)
            xh = raw * rr
            dgain_ref[:, cols] += jnp.sum(dy * xh, axis=0, keepdims=True)
            dxh = dy * gain_ref[:, cols]
            dproj[:, cols] = (rr * (dxh - xh * _group_mean(dxh * xh, bdm))).astype(BF16)
        dproj[0:QBLK, QK_W:QK_W + N_KV * HEAD_DIM] = dve_ref[...]
        dproj[QBLK:2 * QBLK, QK_W:QK_W + N_KV * HEAD_DIM] = dvo_ref[...]
        dproj[:, QK_W + N_KV * HEAD_DIM:] = dg_ref[...]
        _in_proj_tail(x_ref, dxn_ref, ng_ref, mod_ref, w_ref, dproj[...], dx_ref, dw_acc, vec_acc)

        @pl.when(i == steps - 1)
        def _():
            _tail_finish(ng_ref, mod_ref, dw_ref, vec_ref, dw_acc, vec_acc)

    row = lambda w: pl.BlockSpec((tile, w), lambda i: (i, 0))
    kblock = pl.BlockSpec((None, QBLK, N_KV * HEAD_DIM), lambda i: (i, 0, 0))
    fixed = lambda shape: pl.BlockSpec(shape, lambda i: (0,) * len(shape))
    return _call(
        body, name=f"attn_in_proj_bwd_{j}", grid=(steps,),
        out_shape=(jax.ShapeDtypeStruct((seq, D), F32), jax.ShapeDtypeStruct((ATTN_IN, D), BF16),
                   jax.ShapeDtypeStruct((8, D), F32), jax.ShapeDtypeStruct((1, QK_W), F32)),
        in_specs=[row(D), row(D), row(3 * LANES), row(QK_W), row(D), kblock, kblock, kblock, kblock, row(D),
                  _mod_row_spec(layer, NORM_ROW), _mod_spec(layer), _const_spec((ATTN_IN, D)), _const_spec((1, QK_W)),
                  _const_spec((LANES, LANES))],
        out_specs=(row(D), fixed((ATTN_IN, D)), fixed((8, D)), fixed((1, QK_W))),
        scratch_shapes=[pltpu.VMEM((tile, ATTN_IN), BF16), pltpu.VMEM((ATTN_IN, D), F32), pltpu.VMEM((8, D), F32)],
        compiler_params=_cparams(1),
    )(x, dxn, rope, qk_raw, dq, dk_e, dk_o, dv_e, dv_o, dg, rows, mod, w_t, gain, bd)


def _pool_in_proj(x, rows, mod, layer, w_t, j, tile):
    seq = x.shape[0]

    def body(x_ref, ng_ref, mod_ref, w_ref, v_ref, g_ref):
        _, _, h = _norm_mod(x_ref[...], ng_ref[...], mod_ref[1:2, :], mod_ref[0:1, :])
        proj = _dot_nt(h.astype(BF16), w_ref[...])
        v_ref[...] = proj[:, :D].astype(BF16)
        g_ref[...] = proj[:, D:].astype(BF16)

    row = pl.BlockSpec((tile, D), lambda i: (i, 0))
    return _call(
        body, name=f"pool_in_proj_{j}", grid=(seq // tile,),
        out_shape=(jax.ShapeDtypeStruct((seq, D), BF16), jax.ShapeDtypeStruct((seq, D), BF16)),
        in_specs=[row, _mod_row_spec(layer, NORM_ROW), _mod_spec(layer), _const_spec((POOL_IN, D))],
        out_specs=(row, row),
        compiler_params=_cparams(1),
    )(x, rows, mod, w_t)


PAD = 8


def _window_sums(ext, lo, hi, forward):
    gw = D // len(POOL_WINDOWS)
    planes = []
    for gi, w in enumerate(POOL_WINDOWS):
        cols = slice(gw * gi, gw * (gi + 1))
        src, k = 0, 1
        while k < w:
            d = k if forward else -k
            ext[1 - src, lo:hi, cols] = ext[src, lo:hi, cols] + ext[src, lo + d:hi + d, cols]
            src, k = 1 - src, 2 * k
        planes.append(src)
    return planes


def _pooled(ext, v_ref, first, tile):
    t_abs = first + lax.broadcasted_iota(jnp.int32, (tile, 1), 0)
    top = PAD + HALO
    planes = _window_sums(ext, PAD, top + tile, False)
    outs = []
    gw = D // len(POOL_WINDOWS)
    for gi, w in enumerate(POOL_WINDOWS):
        cols = slice(gw * gi, gw * (gi + 1))
        cnt = jnp.minimum(t_abs + 1, w).astype(F32)
        outs.append(ext[planes[gi], top:top + tile, cols] / cnt - v_ref[:, cols].astype(F32))
    return jnp.concatenate(outs, axis=1)


def _fill_ext(ext, halo_ref, v_ref, i, tile):
    ext[0, 0:PAD, :] = jnp.zeros((PAD, D), F32)
    ext[1, 0:PAD, :] = jnp.zeros((PAD, D), F32)
    ext[0, PAD:PAD + HALO, :] = jnp.where(i == 0, 0.0, halo_ref[...].astype(F32))
    ext[0, PAD + HALO:PAD + HALO + tile, :] = v_ref[...].astype(F32)


def _group_mix(pb, wg_ref):
    gw = D // len(POOL_WINDOWS)
    return jnp.concatenate([_dot(pb[:, gw * gi:gw * (gi + 1)], wg_ref[gi]) for gi in range(len(POOL_WINDOWS))], axis=1)


def _pool_mix_out(x, v, g, wg, w_out, j, rows, mod, layer, tile, target=None):
    seq = x.shape[0]

    def body(*refs):
        if target is None:
            x_ref, v_ref, halo_ref, g_ref, wg_ref, w_ref, scale_ref, mod_ref, xo_ref, br_ref, ext = refs
        else:
            x_ref, v_ref, halo_ref, g_ref, wg_ref, w_ref, scale_ref, mod_ref, t_ref, xo_ref, br_ref, loss_ref, ext = refs
        i = pl.program_id(0)
        _fill_ext(ext, halo_ref, v_ref, i, tile)
        pb = _pooled(ext, v_ref, i * tile, tile).astype(BF16)
        ms = _group_mix(pb, wg_ref) * scale_ref[...]
        gv = g_ref[...].astype(F32)
        u = (ms * (gv * _sigmoid(gv))).astype(BF16)
        br = _dot(u, w_ref[...])
        br_ref[...] = br.astype(BF16)
        y = x_ref[...] + mod_ref[2:3, :] * br
        if target is None:
            xo_ref[...] = y
        else:
            @pl.when(i == 0)
            def _():
                loss_ref[...] = jnp.zeros_like(loss_ref)

            e = y - t_ref[...]
            xo_ref[...] = e * (1.0 / D)
            loss_ref[...] += 0.5 * jnp.sum(jnp.mean(e * e, axis=-1, keepdims=True), axis=0, keepdims=True)

    row = pl.BlockSpec((tile, D), lambda i: (i, 0))
    halo = pl.BlockSpec((HALO, D), lambda i: (jnp.maximum(i * (tile // HALO) - 1, 0), 0))
    extra_in, extra_out, extra_shape = ([], (), ()) if target is None else (
        [row], (pl.BlockSpec((1, LANES), lambda i: (0, 0)),), (jax.ShapeDtypeStruct((1, LANES), F32),))
    return _call(
        body, name=f"pool_mix_out_{j}", grid=(seq // tile,),
        out_shape=(jax.ShapeDtypeStruct((seq, D), F32), jax.ShapeDtypeStruct((seq, D), BF16)) + extra_shape,
        in_specs=[row, row, halo, row, _const_spec(wg.shape), _const_spec((D, D)), _mod_row_spec(layer, POOL_SCALE_ROW),
                  _mod_spec(layer)] + extra_in,
        out_specs=(row, row) + extra_out,
        scratch_shapes=[pltpu.VMEM((2, tile + HALO + PAD, D), F32)],
        compiler_params=_cparams(1),
    )(x, v, v, g, wg, w_out, rows, mod, *(() if target is None else (target,)))


def _pool_mix_out_bwd(dxn, br, v, g, wg, w_out, j, rows, mod, layer, tile, after):
    seq = dxn.shape[0]
    steps = seq // tile
    ng_ = len(POOL_WINDOWS)
    gw = D // ng_

    def body(dxn_ref, br_ref, v_ref, halo_ref, g_ref, wg_ref, w_ref, scale_ref, mod_ref, after_ref,
             dpool_ref, dg_ref, dw_ref, dwg_ref, vec_ref, ext, dw_acc, dwg_acc):
        i = pl.program_id(0)

        @pl.when(i == 0)
        def _():
            dw_acc[...] = jnp.zeros_like(dw_acc)
            dwg_acc[...] = jnp.zeros_like(dwg_acc)
            vec_ref[...] = jnp.zeros_like(vec_ref)

        _fill_ext(ext, halo_ref, v_ref, i, tile)
        pb = _pooled(ext, v_ref, i * tile, tile).astype(BF16)
        mixed = _group_mix(pb, wg_ref)
        scale = scale_ref[...]
        ms = mixed * scale
        gv, dxn_v = g_ref[...].astype(F32), dxn_ref[...]
        sg = _sigmoid(gv)
        sl = gv * sg
        vec_ref[0:1, :] += jnp.sum(dxn_v * br_ref[...].astype(F32), axis=0, keepdims=True)
        dbr = (dxn_v * mod_ref[2:3, :]).astype(BF16)
        du = _dot_nt(dbr, w_ref[...])
        dw_acc[...] += _dot_tn((ms * sl).astype(BF16), dbr)
        dms = du * sl
        dg_ref[...] = (du * ms * (sg * (1.0 + gv * (1.0 - sg)))).astype(BF16)
        vec_ref[1:2, :] += jnp.sum(dms * mixed, axis=0, keepdims=True)
        dmx = (dms * scale).astype(BF16)
        for gi in range(ng_):
            cols = slice(gw * gi, gw * (gi + 1))
            dpool_ref[:, cols] = _dot_nt(dmx[:, cols], wg_ref[gi])
            dwg_acc[gi] += _dot_tn(pb[:, cols], dmx[:, cols])

        @pl.when(i == steps - 1)
        def _():
            dw_ref[...] = dw_acc[...].astype(BF16)
            dwg_ref[...] = dwg_acc[...].astype(BF16)

    row = pl.BlockSpec((tile, D), lambda i: (i, 0))
    halo = pl.BlockSpec((HALO, D), lambda i: (jnp.maximum(i * (tile // HALO) - 1, 0), 0))
    fixed = lambda shape: pl.BlockSpec(shape, lambda i: (0,) * len(shape))
    return _call(
        body, name=f"pool_mix_out_bwd_{j}", grid=(steps,),
        out_shape=(jax.ShapeDtypeStruct((seq, D), F32), jax.ShapeDtypeStruct((seq, D), BF16),
                   jax.ShapeDtypeStruct((D, D), BF16), jax.ShapeDtypeStruct((ng_, gw, gw), BF16),
                   jax.ShapeDtypeStruct((8, D), F32)),
        in_specs=[row, row, row, halo, row, _const_spec(wg.shape), _const_spec((D, D)), _mod_row_spec(layer, POOL_SCALE_ROW),
                  _mod_spec(layer), ANY_SPEC],
        out_specs=(row, row, fixed((D, D)), fixed((ng_, gw, gw)), fixed((8, D))),
        scratch_shapes=[pltpu.VMEM((2, tile + HALO + PAD, D), F32), pltpu.VMEM((D, D), F32), pltpu.VMEM((ng_, gw, gw), F32)],
        compiler_params=_cparams(1),
    )(dxn, br, v, v, g, wg, w_out, rows, mod, after)


def _pool_in_proj_bwd(x, dxn, dpool, dg, rows, mod, layer, w_t, j, tile, after):
    seq = x.shape[0]
    steps = seq // tile
    gw = D // len(POOL_WINDOWS)

    def body(x_ref, dxn_ref, dp_ref, halo_ref, dg_ref, ng_ref, mod_ref, w_ref, after_ref, dx_ref, dw_ref, vec_ref,
             ext, dproj, dw_acc, vec_acc):
        i = pl.program_id(0)

        @pl.when(i == 0)
        def _():
            dw_acc[...] = jnp.zeros_like(dw_acc)
            vec_acc[...] = jnp.zeros_like(vec_acc)

        t_abs = i * tile + lax.broadcasted_iota(jnp.int32, (tile, 1), 0)
        last = i == steps - 1
        ext[0, tile + HALO:tile + HALO + PAD, :] = jnp.zeros((PAD, D), F32)
        ext[1, tile + HALO:tile + HALO + PAD, :] = jnp.zeros((PAD, D), F32)
        for gi, w in enumerate(POOL_WINDOWS):
            cols = slice(gw * gi, gw * (gi + 1))
            cnt = jnp.minimum(t_abs + 1, w).astype(F32)
            ext[0, 0:tile, cols] = dp_ref[:, cols] / cnt
            ext[0, tile:tile + HALO, cols] = jnp.where(last, 0.0, halo_ref[:, cols] * (1.0 / w))
        planes = _window_sums(ext, 0, tile + HALO, True)
        for gi, w in enumerate(POOL_WINDOWS):
            cols = slice(gw * gi, gw * (gi + 1))
            dproj[:, cols] = (ext[planes[gi], 0:tile, cols] - dp_ref[:, cols]).astype(BF16)
        dproj[:, D:] = dg_ref[...]
        _in_proj_tail(x_ref, dxn_ref, ng_ref, mod_ref, w_ref, dproj[...], dx_ref, dw_acc, vec_acc)

        @pl.when(last)
        def _():
            _tail_finish(ng_ref, mod_ref, dw_ref, vec_ref, dw_acc, vec_acc)

    row = pl.BlockSpec((tile, D), lambda i: (i, 0))
    halo = pl.BlockSpec((HALO, D), lambda i: (jnp.minimum((i + 1) * (tile // HALO), seq // HALO - 1), 0))
    fixed = lambda shape: pl.BlockSpec(shape, lambda i: (0,) * len(shape))
    return _call(
        body, name=f"pool_in_proj_bwd_{j}", grid=(steps,),
        out_shape=(jax.ShapeDtypeStruct((seq, D), F32), jax.ShapeDtypeStruct((POOL_IN, D), BF16),
                   jax.ShapeDtypeStruct((8, D), F32)),
        in_specs=[row, row, row, halo, row, _mod_row_spec(layer, NORM_ROW), _mod_spec(layer), _const_spec((POOL_IN, D)), ANY_SPEC],
        out_specs=(row, fixed((POOL_IN, D)), fixed((8, D))),
        scratch_shapes=[pltpu.VMEM((2, tile + HALO + PAD, D), F32), pltpu.VMEM((tile, POOL_IN), BF16), pltpu.VMEM((POOL_IN, D), F32),
                        pltpu.VMEM((8, D), F32)],
        compiler_params=_cparams(1),
    )(x, dxn, dpool, dpool, dg, rows, mod, w_t, after)


def _build_vec(vecs, gates, pool_vecs, gains, dsinks, loss_part):
    def body(v0, v1, v2, v3, g0, g2, p0, p1, n0, n1, s0, s1, loss_ref, out):
        out[...] = jnp.zeros_like(out)
        for i, v in enumerate((v0, v1, v2, v3)):
            out[3 * i:3 * i + 2, :] = v[0:2, :]
            out[12 + i:13 + i, :] = v[3:4, :]
        out[2:3, :] = g0[...]
        out[8:9, :] = g2[...]
        for j, (p, n, s) in enumerate(((p0, n0, s0), (p1, n1, s1))):
            out[3 * (2 * j + 1) + 2:3 * (2 * j + 1) + 3, :] = p[0:1, :]
            out[22 + j:23 + j, :] = p[1:2, :]
            out[16 + j:17 + j, :] = n[:, 0:D]
            out[18 + j:19 + j, 0:QK_W - D] = n[:, D:QK_W]
            out[20 + j:21 + j, 0:LANES] = s[...]
        out[24:25, 0:LANES] = loss_ref[...]

    vm = pl.BlockSpec(memory_space=pltpu.VMEM)
    args = (*vecs, gates[0], gates[2], *pool_vecs, *gains, *dsinks, loss_part)
    return _call(
        body, name="build_vec",
        out_shape=jax.ShapeDtypeStruct((VEC_ROWS, D), F32),
        in_specs=[vm] * len(args), out_specs=vm,
        compiler_params=_cparams(),
    )(*args)


def _sum_devices(g, after):
    rows = g.shape[1]

    def body(g_ref, after_ref, tot_ref, fold_ref):
        tot = g_ref[0]
        for p in range(1, N_DEV):
            tot = tot + g_ref[p]
        tot_ref[...] = tot
        f = tot[16:24, 0:LANES]
        for b in range(1, D // LANES):
            f = f + tot[16:24, LANES * b:LANES * (b + 1)]
        fold_ref[...] = f + pltpu.roll(f, HEAD_DIM, 1)

    return _call(
        body, name="sum_devices",
        out_shape=(jax.ShapeDtypeStruct((rows, D), F32), jax.ShapeDtypeStruct((8, LANES), F32)),
        in_specs=[pl.BlockSpec(memory_space=pltpu.VMEM), ANY_SPEC],
        out_specs=(pl.BlockSpec(memory_space=pltpu.VMEM), pl.BlockSpec(memory_space=pltpu.VMEM)),
        compiler_params=_cparams(),
    )(g, after)


def _adamw_small(params):
    n = len(params)

    def body(*refs):
        ins, outs = refs[:4 * n], refs[4 * n:]
        for p in range(n):
            w_ref, g_ref, m_ref, v_ref = ins[4 * p:4 * p + 4]
            outs[3 * p][...], outs[3 * p + 1][...], outs[3 * p + 2][...] = _adamw(w_ref[...], g_ref[...], m_ref[...], v_ref[...])

    vm = pl.BlockSpec(memory_space=pltpu.VMEM)
    out = _call(
        body, name="adamw_small",
        out_shape=tuple(jax.ShapeDtypeStruct(w.shape, F32) for (w, _, _, _) in params for _ in range(3)),
        in_specs=[vm] * (4 * n), out_specs=tuple([vm] * (3 * n)),
        compiler_params=_cparams(),
    )(*[a for p in params for a in p])
    return [tuple(out[3 * p:3 * p + 3]) for p in range(n)]


def _adamw_shards(name, me, fulls, lands, w, m, v, transpose, axis=0):
    nl = w.shape[0]
    wshape = w.shape[1:]
    own_shape = lands[0].shape[1:]

    def body(me_ref, *refs):
        own_refs, land_refs = refs[:nl], refs[nl:2 * nl]
        w_ref, m_ref, v_ref, g_out, d_out, m_out, v_out = refs[2 * nl:]
        layer = pl.program_id(0)
        for l in range(nl):
            @pl.when(layer == l)
            def _(l=l):
                g = own_refs[l][...].astype(F32)
                for k in range(N_DEV - 1):
                    g = g + land_refs[l][k].astype(F32)
                if transpose:
                    g = g.T
                g_out[...] = g
                d_out[...], m_out[...], v_out[...] = _adamw(w_ref[...], g, m_ref[...], v_ref[...])

    def own_index(l_, me_ref):
        idx = [0] * len(own_shape)
        idx[axis] = me_ref[0]
        return tuple(idx)

    own_spec = pl.BlockSpec(tuple(own_shape), own_index)
    land_spec = pl.BlockSpec((N_DEV - 1,) + tuple(own_shape), lambda l_, me_ref: (0,) * (1 + len(own_shape)))
    wspec = pl.BlockSpec((None,) + tuple(wshape), lambda l_, me_ref: (l_,) + (0,) * len(wshape))
    return _call(
        body, name=name,
        grid_spec=pltpu.PrefetchScalarGridSpec(num_scalar_prefetch=1, grid=(nl,),
                                               in_specs=[own_spec] * nl + [land_spec] * nl + [wspec] * 3,
                                               out_specs=(wspec,) * 4),
        out_shape=tuple(jax.ShapeDtypeStruct(w.shape, F32) for _ in range(4)),
        compiler_params=_cparams(1),
    )(me.reshape(1), *fulls, *lands, w, m, v)


def _constants():
    lane = np.arange(LANES)
    bd = (lane[:, None] // HEAD_DIM == lane[None, :] // HEAD_DIM).astype(np.float32)
    half = ROT_DIM // 2
    inv_freq = ROPE_THETA ** (-jnp.arange(half, dtype=F32) * 2.0 / ROT_DIM)
    invf = jnp.tile(inv_freq, LANES // half).reshape(1, LANES)
    return jnp.asarray(bd, BF16), invf


def kernel(x, c, positions, ada_w, ada_b, norm_g, attn_w_in, attn_q_norm, attn_k_norm, attn_sinks, attn_w_out, pool_w_in, pool_w_group, pool_scale, pool_w_out, loss_target, m_ada_w, m_ada_b, m_norm_g, m_attn_w_in, m_attn_q_norm, m_attn_k_norm, m_attn_sinks, m_attn_w_out, m_pool_w_in, m_pool_w_group, m_pool_scale, m_pool_w_out, v_ada_w, v_ada_b, v_norm_g, v_attn_w_in, v_attn_q_norm, v_attn_k_norm, v_attn_sinks, v_attn_w_out, v_pool_w_in, v_pool_w_group, v_pool_scale, v_pool_w_out):
    seq = x.shape[1]
    me = 4 * lax.axis_index("x") + 2 * lax.axis_index("y") + lax.axis_index("c")
    bd, invf = _constants()
    t_mm = min(512, seq)
    rope = _rope_table(positions.reshape(seq, 1), invf, t_mm)
    t_bw = min(256, seq)
    shard = pool_scale.shape[1]
    cols = ada_w.shape[2]

    w_in_rows = jnp.swapaxes(attn_w_in, 1, 2)
    w_first, = _prep_weights(me, [(w_in_rows, 0, "N")], "prep_first")
    first_w, token = _gather_first_start(w_first, c)
    prepped = _prep_weights(me, [(attn_w_out, 0, "N"), (pool_w_in, 0, "T"), (pool_w_out, 0, "N"), (pool_w_group, 0, "G"),
                                 (w_in_rows, 1, "N"), (attn_w_out, 1, "N"), (pool_w_in, 1, "T"), (pool_w_out, 1, "N"),
                                 (pool_w_group, 1, "G")], "prep_rest")

    first = jnp.concatenate([c, jnp.pad(pool_scale, ((0, 0), (0, D - shard))), jnp.zeros((5, D), F32)], axis=0)
    first = _allgather_small(first + token[0, 0], "allgather_c", rope)
    c_all = first[:, 0, :]
    scale_full = jnp.transpose(first[:, 1:3, :shard], (1, 0, 2)).reshape(2, D)
    mod_part = _ada_forward(c_all, ada_w)
    mod_all = _allgather_small(mod_part.reshape(DEPTH * N_DEV, cols), "allgather_mod", prepped[0])
    mod_all = mod_all.reshape(N_DEV, DEPTH, N_DEV, cols)
    mine = lax.dynamic_index_in_dim(mod_all, me, axis=2, keepdims=False)
    mod = jnp.transpose(mine, (1, 0, 2)).reshape(DEPTH, 3 * D) + ada_b
    pool_rows = jnp.stack([jnp.zeros_like(scale_full[0]), scale_full[0], jnp.zeros_like(scale_full[0]), scale_full[1]])
    mod = jnp.concatenate([mod.reshape(DEPTH, 3, D), norm_g[:, None, :], pool_rows[:, None, :],
                           jnp.zeros((DEPTH, 3, D), F32)], axis=1)
    rows = mod.reshape(DEPTH, 8, 1, D)

    groups = [prepped[0:1], prepped[1:4], prepped[4:6], prepped[6:9]]
    gaxes = [(0,), (0,), (0, 0, 1), (0, 0), (0, 0, 1)]
    first_w, token = _gather_first_forward(first_w, mod)
    rest, token = _gather_start(groups, gaxes[1:], token, "gather_start_rest")
    started = [None] + rest

    saved, weights = [], []
    h = x[0]
    for i in range(DEPTH):
        j = i // 2
        s = dict(x=h)
        if i == 0:
            w_in_t = _gather_first_wait(first_w, token)
        else:
            wts = _gather_wait(started[i + 1], gaxes[i + 1], h, f"gather_wait_{i}")
        if i % 2 == 0:
            if i > 0:
                w_in_t, w_out = wts
            s["gain"] = jnp.concatenate([jnp.tile(attn_q_norm[j], N_HEADS), jnp.tile(attn_k_norm[j], N_KV)]).reshape(1, QK_W)
            s["qk_raw"], s["qs"], s["kd"], s["vd"], s["g"] = _attn_in_proj(
                h, rope, rows, mod, i, w_in_t, j, s["gain"], bd, t_mm)
            s["o"] = _attn_forward(attn_sinks, s["qs"], s["kd"], s["vd"], j)
            if i == 0:
                w_out, = _gather_wait(started[1], gaxes[1], s["o"], "gather_wait_0_out")
            h, s["br"] = _attn_out_proj(h, s["o"], s["g"], w_out, j, mod, i, t_mm)
            weights.append((w_in_t, w_out))
        else:
            p_in_t, p_out, p_grp = wts
            s["v"], s["g"] = _pool_in_proj(h, rows, mod, i, p_in_t, j, t_mm)
            if i < DEPTH - 1:
                h, s["br"] = _pool_mix_out(h, s["v"], s["g"], p_grp, p_out, j, rows, mod, i, t_mm)
            else:
                dx, s["br"], loss_part = _pool_mix_out(h, s["v"], s["g"], p_grp, p_out, j, rows, mod, i, t_mm,
                                                       loss_target[0])
            weights.append(wts)
        saved.append(s)

    vecs, gates, gains, dsinks, pool_vecs = [None] * DEPTH, [None] * DEPTH, [None] * 2, [None] * 2, [None] * 2
    sent = {}
    token = jnp.zeros((8, LANES), F32)
    for i in reversed(range(DEPTH)):
        j = i // 2
        s = saved[i]
        if i % 2 == 0:
            w_in_t, w_out = weights[i]
            dos, dg, d_w_out, gates[i] = _attn_out_proj_bwd(dx, s["br"], s["o"], s["g"], w_out, j, mod, i, t_mm, token)
            if i == 0:
                sent["0_out"], token = _scatter_start([d_w_out], (0,), "scatter_start_0_out", token)
            dq, dk, dv, dsinks[j] = _attn_backward(attn_sinks, s["qs"], dos, s["kd"], s["vd"], j, token)
            dx, d_in_t, vecs[i], gains[j] = _attn_in_proj_bwd(
                s["x"], dx, rope, s["qk_raw"], dq, dk, dv, dg, rows, mod, i, w_in_t, j, s["gain"], bd, t_bw)
            if i > 0:
                sent[i], token = _scatter_start([d_in_t, d_w_out], (0, 0), f"scatter_start_{i}", token)
        else:
            p_in_t, p_out, p_grp = weights[i]
            dpool, dg, d_p_out, d_p_grp, pool_vecs[j] = _pool_mix_out_bwd(
                dx, s["br"], s["v"], s["g"], p_grp, p_out, j, rows, mod, i, t_mm, token)
            dx, d_in_t, vecs[i] = _pool_in_proj_bwd(s["x"], dx, dpool, dg, rows, mod, i, p_in_t, j, t_bw, token)
            sent[i], token = _scatter_start([d_in_t, d_p_out, d_p_grp], (0, 0, 1), f"scatter_start_{i}", token)

    vec = _build_vec(vecs, gates, pool_vecs, gains, dsinks, loss_part)
    vec_rows = lax.dynamic_update_slice(jnp.zeros((N_DEV * VEC_ROWS, D), F32), vec, (me * VEC_ROWS, 0))
    vec_sent, token = _gather_start([[vec_rows]], [(0,)], loss_part, "vec_gather_start")
    sent["0_in"], token = _scatter_start([d_in_t], (0,), "scatter_start_0_in", token)

    got = {}
    for i in (3, 1):
        fulls, lands = _scatter_wait(sent[i], (0, 0, 1), token, f"scatter_wait_{i}")
        got[i] = dict(zip(("in", "out", "grp"), zip(fulls, lands)))
    pick = lambda ls, kind: ([got[i][kind][0] for i in ls], [got[i][kind][1] for i in ls])
    res = {}
    res["pool_w_in"] = _adamw_shards("adamw_pool_w_in", me, *pick((1, 3), "in"), pool_w_in, m_pool_w_in, v_pool_w_in, True)
    res["pool_w_out"] = _adamw_shards("adamw_pool_w_out", me, *pick((1, 3), "out"), pool_w_out, m_pool_w_out,
                                      v_pool_w_out, False)
    res["pool_w_group"] = _adamw_shards("adamw_pool_w_group", me, *pick((1, 3), "grp"), pool_w_group, m_pool_w_group,
                                        v_pool_w_group, False, axis=1)

    vec_all, = _gather_wait(vec_sent[0], (0,), res["pool_w_group"][0], "vec_gather_wait")
    vec_all = vec_all.reshape(N_DEV, VEC_ROWS, D)
    tot, folded = _sum_devices(vec_all, token)
    loss = tot[24, 0]
    small = dict(
        ada_b=(ada_b, tot[0:12].reshape(DEPTH, 3 * D), m_ada_b, v_ada_b),
        norm_g=(norm_g, tot[12:16], m_norm_g, v_norm_g),
        q_norm=(attn_q_norm, folded[0:2, :HEAD_DIM], m_attn_q_norm, v_attn_q_norm),
        k_norm=(attn_k_norm, folded[2:4, :HEAD_DIM], m_attn_k_norm, v_attn_k_norm),
        sinks=(attn_sinks, tot[20:22, :N_HEADS], m_attn_sinks, v_attn_sinks),
        pool_scale=(pool_scale, lax.dynamic_slice(tot, (22, me * shard), (2, shard)), m_pool_scale, v_pool_scale),
    )
    res.update({k: (a[1],) + upd for (k, a), upd in zip(small.items(), _adamw_small(list(small.values())))})

    dmod_all = vec_all[:, 0:12, :].reshape(N_DEV, DEPTH, 3 * D)
    dmod_mine = lax.dynamic_slice_in_dim(dmod_all, me * cols, cols, axis=2)
    dmod_mine = jnp.pad(jnp.transpose(dmod_mine, (1, 0, 2)), ((0, 0), (0, N_DEV), (0, 0))) + token[0, 0]
    res["ada_w"] = _ada_backward_adamw(jnp.pad(c_all, ((0, N_DEV), (0, 0))), dmod_mine, ada_w, m_ada_w, v_ada_w)

    fulls, lands = _scatter_wait(sent[2], (0, 0), res["ada_w"][0], "scatter_wait_2")
    got[2] = dict(zip(("in", "out"), zip(fulls, lands)))
    got[0] = {}
    for kind in ("out", "in"):
        fulls, lands = _scatter_wait(sent["0_" + kind], (0,), res["ada_w"][0], "scatter_wait_0_" + kind)
        got[0][kind] = (fulls[0], lands[0])
    res["attn_w_out"] = _adamw_shards("adamw_attn_w_out", me, *pick((0, 2), "out"), attn_w_out, m_attn_w_out,
                                      v_attn_w_out, False)
    res["attn_w_in"] = tuple(jnp.swapaxes(a, 1, 2) for a in _adamw_shards(
        "adamw_attn_w_in", me, *pick((0, 2), "in"), w_in_rows, jnp.swapaxes(m_attn_w_in, 1, 2),
        jnp.swapaxes(v_attn_w_in, 1, 2), False))

    order = ("ada_w", "ada_b", "norm_g", "attn_w_in", "q_norm", "k_norm", "sinks", "attn_w_out", "pool_w_in",
             "pool_w_group", "pool_scale", "pool_w_out")
    return (loss, dx[None], *[res[k][0] for k in order], *[res[k][1] for k in order], *[res[k][2] for k in order],
            *[res[k][3] for k in order])
```

```python
import numpy as np
import jax
import jax.numpy as jnp
from jax import lax
from jax.experimental import pallas as pl
from jax.experimental.pallas import tpu as pltpu

F32 = jnp.float32
BF16 = jnp.bfloat16
MESH = pl.DeviceIdType.MESH

N_DEV = 8
D = 1024
DEPTH = 4
HEAD_DIM = 64
N_HEADS = 16
N_KV = 4
QK_W = 1280
ATTN_IN = 2560
POOL_IN = 2048
QBLK = 128
KX_W = N_KV * 128
CHUNK = 256
POOL_WINDOWS = (2, 4, 8, 16)
HALO = 16
ROPE_THETA = 500000.0
ROT_DIM = 16
NORM_EPS = 1e-6
ADAM_LR = 0.001
ADAM_B1 = 0.9
ADAM_B2 = 0.999
ADAM_EPS = 1e-08
ADAM_WD = 0.01
ADAM_STEP = 10

LANES = 128
VMEM_LIMIT = 56 * 2**20
VEC_ROWS = 32


def _cparams(n_grid=0, **kw):
    if n_grid:
        kw["dimension_semantics"] = ("arbitrary",) * n_grid
    return pltpu.CompilerParams(vmem_limit_bytes=VMEM_LIMIT, **kw)


def _call(body, **kw):
    return pl.pallas_call(body, **kw)


def _mod_spec(layer):
    return pl.BlockSpec((None, 8, D), lambda *_: (layer, 0, 0), pipeline_mode=pl.Buffered(1))


def _mod_row_spec(layer, row):
    return pl.BlockSpec((None, None, 1, D), lambda *_: (layer, row, 0, 0), pipeline_mode=pl.Buffered(1))


NORM_ROW, POOL_SCALE_ROW = 3, 4


def _const_spec(shape):
    nd = len(shape)
    return pl.BlockSpec(shape, lambda *_: (0,) * nd, pipeline_mode=pl.Buffered(1))


def _dot(a, b):
    return jnp.dot(a, b, preferred_element_type=F32)


def _dot_nt(a, b):
    return lax.dot_general(a, b, (((1,), (1,)), ((), ())), preferred_element_type=F32)


def _dot_tn(a, b):
    return lax.dot_general(a, b, (((0,), (0,)), ((), ())), preferred_element_type=F32)


def _group_mean(x, m):
    return _dot(x.astype(BF16), m) * (1.0 / HEAD_DIM)


def _sigmoid(g):
    return 1.0 / (1.0 + jnp.exp(-g))


def _norm_mod(x, ng, sc, sh):
    r = lax.rsqrt(jnp.mean(x * x, axis=-1, keepdims=True) + NORM_EPS)
    xh = x * r
    h = (xh * ng) * (1.0 + sc) + sh
    return xh, r, h


def _rope_table(pos_col, invf_row, tile):
    seq = pos_col.shape[0]

    def body(pos_ref, invf_ref, out_ref):
        ang = pos_ref[...].astype(F32) * invf_ref[...]
        l64 = lax.broadcasted_iota(jnp.int32, (tile, LANES), 1) & (HEAD_DIM - 1)
        cs, sn = jnp.cos(ang), jnp.sin(ang)
        out_ref[:, 0:LANES] = jnp.where(l64 < ROT_DIM, cs, 1.0)
        out_ref[:, LANES:2 * LANES] = jnp.where(l64 < ROT_DIM // 2, -sn, 0.0)
        out_ref[:, 2 * LANES:3 * LANES] = jnp.where((l64 >= ROT_DIM // 2) & (l64 < ROT_DIM), sn, 0.0)

    return _call(
        body, name="rope_table", grid=(seq // tile,),
        out_shape=jax.ShapeDtypeStruct((seq, 3 * LANES), F32),
        in_specs=[pl.BlockSpec((tile, 1), lambda i: (i, 0)), _const_spec((1, LANES))],
        out_specs=pl.BlockSpec((tile, 3 * LANES), lambda i: (i, 0)),
        compiler_params=_cparams(1),
    )(pos_col, invf_row)


def _rope_tabs(rope_ref):
    return rope_ref[:, 0:LANES], rope_ref[:, LANES:2 * LANES], rope_ref[:, 2 * LANES:3 * LANES]


def _rope(y, tabs):
    cos_t, sin_a, sin_b = tabs
    return y * cos_t + pltpu.roll(y, LANES - ROT_DIM // 2, 1) * sin_a + pltpu.roll(y, ROT_DIM // 2, 1) * sin_b


def _rope_bwd(dy, tabs):
    cos_t, sin_a, sin_b = tabs
    return dy * cos_t + pltpu.roll(dy * sin_a, ROT_DIM // 2, 1) + pltpu.roll(dy * sin_b, LANES - ROT_DIM // 2, 1)


def _low_half(rows):
    return lax.broadcasted_iota(jnp.int32, (rows, LANES), 1) < HEAD_DIM


def _adamw(w, g, m, v):
    m = ADAM_B1 * m + (1.0 - ADAM_B1) * g
    v = ADAM_B2 * v + (1.0 - ADAM_B2) * (g * g)
    m_hat = m / (1.0 - ADAM_B1 ** ADAM_STEP)
    v_hat = v / (1.0 - ADAM_B2 ** ADAM_STEP)
    delta = -ADAM_LR * (m_hat / (jnp.sqrt(v_hat) + ADAM_EPS) + ADAM_WD * w)
    return delta, m, v


def _my_position():
    x, y, c = lax.axis_index("x"), lax.axis_index("y"), lax.axis_index("c")
    return x, y, c, 4 * x + 2 * y + c


def _peers(x, y, c):
    out = []
    for k in range(1, N_DEV):
        px = 1 - x if k & 4 else x
        py = 1 - y if k & 2 else y
        pc = 1 - c if k & 1 else c
        out.append(((px, py, pc), 4 * px + 2 * py + pc))
    return out


def _allgather_small(v, name, after):
    rows, cols = v.shape

    def body(v_ref, *refs):
        out_ref, send_sems, recv_sems, local_sem = refs[len(after):]
        x, y, c, me = _my_position()
        local = pltpu.make_async_copy(v_ref, out_ref.at[me], local_sem)
        local.start()
        sends = []
        for k, (peer, _) in enumerate(_peers(x, y, c)):
            cp = pltpu.make_async_remote_copy(v_ref, out_ref.at[me], send_sems.at[k], recv_sems.at[k],
                                              device_id=peer, device_id_type=MESH)
            cp.start()
            sends.append(cp)
        for k, (peer, idx) in enumerate(_peers(x, y, c)):
            pltpu.make_async_remote_copy(v_ref, out_ref.at[idx], send_sems.at[k], recv_sems.at[k],
                                         device_id=peer, device_id_type=MESH).wait_recv()
        for cp in sends:
            cp.wait_send()
        local.wait()

    return _call(
        body, name=name,
        out_shape=jax.ShapeDtypeStruct((N_DEV, rows, cols), F32),
        in_specs=[pl.BlockSpec(memory_space=pltpu.VMEM)] + [pl.BlockSpec(memory_space=pl.ANY)] * len(after),
        out_specs=pl.BlockSpec(memory_space=pltpu.VMEM),
        scratch_shapes=[pltpu.SemaphoreType.DMA((N_DEV - 1,)), pltpu.SemaphoreType.DMA((N_DEV - 1,)),
                        pltpu.SemaphoreType.DMA(())],
        compiler_params=_cparams(),
    )(v, *after)


def _shard_rows(ref, idx, rows, axis):
    sl = [slice(None)] * len(ref.shape)
    sl[axis] = pl.ds(idx * rows, rows)
    return ref.at[tuple(sl)]


def _own_and_peer_rows(ref, me, idx, axis):
    rows = ref.shape[axis] // N_DEV
    return _shard_rows(ref, me, rows, axis), _shard_rows(ref, idx, rows, axis)


HBM_SPEC = pl.BlockSpec(memory_space=pltpu.HBM)
SEM_SPEC = pl.BlockSpec(memory_space=pltpu.SEMAPHORE)
ANY_SPEC = pl.BlockSpec(memory_space=pl.ANY)
DATAFLOW = pltpu.SideEffectType.DATAFLOW_SIDE_EFFECTING


def _hbm(a):
    return pltpu.with_memory_space_constraint(a, pltpu.HBM)


def _gather_start(layers, axes, after, name):
    flat = [a for arrs in layers for a in arrs]
    flat_axes = [ax for axs in axes for ax in axs]
    n, nl = len(flat), len(layers)

    def body(*refs):
        ins, sems, token = refs[:n], refs[n + 1:n + 1 + 2 * nl], refs[-1]
        x, y, c, me = _my_position()
        a0 = 0
        for li, arrs in enumerate(layers):
            for k, (peer, _) in enumerate(_peers(x, y, c)):
                for a in range(len(arrs)):
                    rows, _ = _own_and_peer_rows(ins[a0 + a], me, me, flat_axes[a0 + a])
                    pltpu.make_async_remote_copy(rows, rows, sems[2 * li].at[k * len(arrs) + a],
                                                 sems[2 * li + 1].at[k * len(arrs) + a],
                                                 device_id=peer, device_id_type=MESH).start()
            a0 += len(arrs)
        token[...] = jnp.zeros_like(token)

    sem_shapes = []
    for arrs in layers:
        sem_shapes += [pltpu.SemaphoreType.DMA(((N_DEV - 1) * len(arrs),))] * 2
    out = _call(
        body, name=name,
        out_shape=(*sem_shapes, *[pltpu.HBM(a.shape, a.dtype) for a in flat], jax.ShapeDtypeStruct((8, LANES), F32)),
        in_specs=[HBM_SPEC] * n + [ANY_SPEC],
        out_specs=(*[SEM_SPEC] * (2 * nl), *[HBM_SPEC] * n, pl.BlockSpec(memory_space=pltpu.VMEM)),
        input_output_aliases={a: 2 * nl + a for a in range(n)},
        compiler_params=_cparams(has_side_effects=DATAFLOW),
    )(*[_hbm(a) for a in flat], after)
    per_layer, a0 = [], 0
    for li, arrs in enumerate(layers):
        per_layer.append((out[2 * li], out[2 * li + 1], list(out[2 * nl + a0:2 * nl + a0 + len(arrs)])))
        a0 += len(arrs)
    return per_layer, out[-1]


def _gather_wait(started, axes, after, name):
    send_sems, recv_sems, arrs = started
    n = len(arrs)

    def body(*refs):
        ins, send_ref, recv_ref = refs[:n], refs[n], refs[n + 1]
        x, y, c, me = _my_position()
        for k, (peer, idx) in enumerate(_peers(x, y, c)):
            for a in range(n):
                own, theirs = _own_and_peer_rows(ins[a], me, idx, axes[a])
                cp = pltpu.make_async_remote_copy(own, theirs, send_ref.at[k * n + a], recv_ref.at[k * n + a],
                                                  device_id=peer, device_id_type=MESH)
                cp.wait_send()
                cp.wait_recv()

    return _call(
        body, name=name,
        out_shape=tuple(pltpu.HBM(a.shape, a.dtype) for a in arrs),
        in_specs=[HBM_SPEC] * n + [SEM_SPEC, SEM_SPEC, ANY_SPEC],
        out_specs=tuple([HBM_SPEC] * n),
        input_output_aliases={a: a for a in range(n)},
        compiler_params=_cparams(has_side_effects=DATAFLOW),
    )(*arrs, send_sems, recv_sems, after)


def _first_relations(x, y, c):
    return [(x, y, 1 - c), (1 - x, y, c), (x, 1 - y, c), (1 - x, 1 - y, c)]


def _gather_first_start(arr, after):
    n_rel = 4

    def body(a_ref, after_ref, send_ref, recv_ref, thru, token):
        x, y, c, me = _my_position()
        rows, _ = _own_and_peer_rows(a_ref, me, me, 0)
        for k, peer in enumerate(_first_relations(x, y, c)):
            pltpu.make_async_remote_copy(rows, rows, send_ref.at[k], recv_ref.at[k], device_id=peer, device_id_type=MESH).start()
        token[...] = jnp.zeros_like(token)

    sem = pltpu.SemaphoreType.DMA((n_rel,))
    out = _call(
        body, name="gather_first_start",
        out_shape=(sem, sem, pltpu.HBM(arr.shape, arr.dtype), jax.ShapeDtypeStruct((8, LANES), F32)),
        in_specs=[HBM_SPEC, ANY_SPEC],
        out_specs=(SEM_SPEC, SEM_SPEC, HBM_SPEC, pl.BlockSpec(memory_space=pltpu.VMEM)),
        input_output_aliases={0: 2},
        compiler_params=_cparams(has_side_effects=DATAFLOW),
    )(_hbm(arr), after)
    return out[:3], out[3]


def _gather_first_forward(started, after):
    send_a, recv_a, arr = started

    def body(a_ref, send_a_ref, recv_a_ref, after_ref, send_b_ref, recv_b_ref, thru, token):
        x, y, c, me = _my_position()
        sibling = (x, y, 1 - c)
        for k, peer in enumerate(_first_relations(x, y, c)):
            own, theirs = _own_and_peer_rows(a_ref, me, 4 * peer[0] + 2 * peer[1] + peer[2], 0)
            cp = pltpu.make_async_remote_copy(own, theirs, send_a_ref.at[k], recv_a_ref.at[k], device_id=peer, device_id_type=MESH)
            cp.wait_send()
            cp.wait_recv()
            if k > 0:
                pltpu.make_async_remote_copy(theirs, theirs, send_b_ref.at[k - 1], recv_b_ref.at[k - 1],
                                             device_id=sibling, device_id_type=MESH).start()
        token[...] = jnp.zeros_like(token)

    sem = pltpu.SemaphoreType.DMA((3,))
    out = _call(
        body, name="gather_first_forward",
        out_shape=(sem, sem, pltpu.HBM(arr.shape, arr.dtype), jax.ShapeDtypeStruct((8, LANES), F32)),
        in_specs=[HBM_SPEC, SEM_SPEC, SEM_SPEC, ANY_SPEC],
        out_specs=(SEM_SPEC, SEM_SPEC, HBM_SPEC, pl.BlockSpec(memory_space=pltpu.VMEM)),
        input_output_aliases={0: 2},
        compiler_params=_cparams(has_side_effects=DATAFLOW),
    )(arr, send_a, recv_a, after)
    return out[:3], out[3]


def _gather_first_wait(forwarded, after):
    send_b, recv_b, arr = forwarded

    def body(a_ref, send_b_ref, recv_b_ref, after_ref, thru):
        x, y, c, me = _my_position()
        sibling = (x, y, 1 - c)
        for k, peer in enumerate(_first_relations(x, y, c)[1:]):
            _, sent = _own_and_peer_rows(a_ref, me, 4 * peer[0] + 2 * peer[1] + peer[2], 0)
            _, got = _own_and_peer_rows(a_ref, me, 4 * peer[0] + 2 * peer[1] + (1 - peer[2]), 0)
            cp = pltpu.make_async_remote_copy(sent, got, send_b_ref.at[k], recv_b_ref.at[k], device_id=sibling, device_id_type=MESH)
            cp.wait_send()
            cp.wait_recv()

    return _call(
        body, name="gather_first_wait",
        out_shape=pltpu.HBM(arr.shape, arr.dtype),
        in_specs=[HBM_SPEC, SEM_SPEC, SEM_SPEC, ANY_SPEC],
        out_specs=HBM_SPEC,
        input_output_aliases={0: 0},
        compiler_params=_cparams(has_side_effects=DATAFLOW),
    )(arr, send_b, recv_b, after)


def _scatter_start(fulls, axes, name, after):
    n = len(fulls)
    lands = []
    for f, ax in zip(fulls, axes):
        shp = list(f.shape)
        shp[ax] //= N_DEV
        lands.append(_hbm(lax.empty((N_DEV - 1,) + tuple(shp), f.dtype)))

    def body(*refs):
        srcs, dsts, send_ref, recv_ref, token = refs[:n], refs[n:2 * n], refs[2 * n + 1], refs[2 * n + 2], refs[-1]
        x, y, c, me = _my_position()
        for k, (peer, idx) in enumerate(_peers(x, y, c)):
            for a in range(n):
                _, theirs = _own_and_peer_rows(srcs[a], me, idx, axes[a])
                pltpu.make_async_remote_copy(theirs, dsts[a].at[k], send_ref.at[k * n + a], recv_ref.at[k * n + a],
                                             device_id=peer, device_id_type=MESH).start()
        token[...] = jnp.zeros_like(token)

    sem = pltpu.SemaphoreType.DMA(((N_DEV - 1) * n,))
    out = _call(
        body, name=name,
        out_shape=(sem, sem, *[pltpu.HBM(a.shape, a.dtype) for a in fulls], *[pltpu.HBM(a.shape, a.dtype) for a in lands],
                   jax.ShapeDtypeStruct((8, LANES), F32)),
        in_specs=[HBM_SPEC] * (2 * n) + [ANY_SPEC],
        out_specs=(SEM_SPEC, SEM_SPEC, *[HBM_SPEC] * (2 * n), pl.BlockSpec(memory_space=pltpu.VMEM)),
        input_output_aliases={a: 2 + a for a in range(2 * n)},
        compiler_params=_cparams(has_side_effects=DATAFLOW),
    )(*[_hbm(a) for a in fulls], *lands, after)
    return (out[0], out[1], list(out[2:2 + n]), list(out[2 + n:2 + 2 * n])), out[-1]


def _scatter_wait(started, axes, after, name):
    send_sems, recv_sems, fulls, lands = started
    n = len(fulls)

    def body(*refs):
        srcs, dsts, send_ref, recv_ref = refs[:n], refs[n:2 * n], refs[2 * n], refs[2 * n + 1]
        x, y, c, me = _my_position()
        for k, (peer, idx) in enumerate(_peers(x, y, c)):
            for a in range(n):
                _, theirs = _own_and_peer_rows(srcs[a], me, idx, axes[a])
                cp = pltpu.make_async_remote_copy(theirs, dsts[a].at[k], send_ref.at[k * n + a], recv_ref.at[k * n + a],
                                                  device_id=peer, device_id_type=MESH)
                cp.wait_send()
                cp.wait_recv()

    out = _call(
        body, name=name,
        out_shape=tuple(pltpu.HBM(a.shape, a.dtype) for a in (*fulls, *lands)),
        in_specs=[HBM_SPEC] * (2 * n) + [SEM_SPEC, SEM_SPEC, ANY_SPEC],
        out_specs=tuple([HBM_SPEC] * (2 * n)),
        input_output_aliases={a: a for a in range(2 * n)},
        compiler_params=_cparams(has_side_effects=DATAFLOW),
    )(*fulls, *lands, send_sems, recv_sems, after)
    return list(out[:n]), list(out[n:])


def _prep_weights(me, items, name):
    def body(me_ref, *refs):
        for (_, _, kind), src, dst in zip(items, refs[:len(items)], refs[len(items):]):
            dst[...] = (src[...].T if kind == "T" else src[...]).astype(BF16)

    ins, in_specs, out_shapes, out_specs = [], [], [], []
    for src, j, kind in items:
        shard = src.shape[1:]
        ins.append(src)
        in_specs.append(pl.BlockSpec((None,) + tuple(shard), lambda i, me_ref, j=j, nd=len(shard): (j,) + (0,) * nd))
        if kind == "G":
            out_shapes.append((shard[0], N_DEV * shard[1], shard[2]))
            out_specs.append(pl.BlockSpec(tuple(shard), lambda i, me_ref: (0, me_ref[0], 0)))
        else:
            rows = shard[1] if kind == "T" else shard[0]
            out_shapes.append((N_DEV * rows, D))
            out_specs.append(pl.BlockSpec((rows, D), lambda i, me_ref: (me_ref[0], 0)))
    out = _call(
        body, name=name,
        grid_spec=pltpu.PrefetchScalarGridSpec(num_scalar_prefetch=1, grid=(1,), in_specs=in_specs, out_specs=tuple(out_specs)),
        out_shape=tuple(jax.ShapeDtypeStruct(s, BF16) for s in out_shapes),
        compiler_params=_cparams(1),
    )(me.reshape(1), *ins)
    return list(out)


def _ada_forward(c_all, ada_w):
    cols = ada_w.shape[2]

    def body(c_ref, w_ref, o_ref):
        cv = c_ref[...]
        sc = (cv * _sigmoid(cv)).astype(BF16)
        o_ref[...] = _dot(sc, w_ref[...].astype(BF16))

    return _call(
        body, name="ada_forward", grid=(DEPTH,),
        out_shape=jax.ShapeDtypeStruct((DEPTH, N_DEV, cols), F32),
        in_specs=[pl.BlockSpec((N_DEV, D), lambda i: (0, 0)), pl.BlockSpec((None, D, cols), lambda i: (i, 0, 0))],
        out_specs=pl.BlockSpec((None, N_DEV, cols), lambda i: (i, 0, 0)),
        compiler_params=_cparams(1),
    )(c_all, ada_w)


def _ada_backward_adamw(c_pad, dmod_pad, w, m, v):
    cols = w.shape[2]

    def body(c_ref, dm_ref, w_ref, m_ref, v_ref, g_out, d_out, m_out, v_out):
        cv = c_ref[...]
        sc = (cv * _sigmoid(cv)).astype(BF16)
        g = _dot_tn(sc, dm_ref[...].astype(BF16))
        g_out[...] = g
        d_out[...], m_out[...], v_out[...] = _adamw(w_ref[...], g, m_ref[...], v_ref[...])

    wspec = pl.BlockSpec((None, D, cols), lambda i: (i, 0, 0))
    return _call(
        body, name="ada_backward_adamw", grid=(DEPTH,),
        out_shape=tuple(jax.ShapeDtypeStruct(w.shape, F32) for _ in range(4)),
        in_specs=[pl.BlockSpec((2 * N_DEV, D), lambda i: (0, 0)), pl.BlockSpec((None, 2 * N_DEV, cols), lambda i: (i, 0, 0)),
                  wspec, wspec, wspec],
        out_specs=(wspec, wspec, wspec, wspec),
        compiler_params=_cparams(1),
    )(c_pad, dmod_pad, w, m, v)


def _attn_in_proj(x, rope, rows, mod, layer, w_t, j, gain, bd, tile):
    seq = x.shape[0]

    def body(x_ref, rope_ref, ng_ref, mod_ref, w_ref, gain_ref, bd_ref, qk_ref, qs_ref, kd_ref, vd_ref, g_ref):
        _, _, h = _norm_mod(x_ref[...], ng_ref[...], mod_ref[1:2, :], mod_ref[0:1, :])
        hb = h.astype(BF16)
        tabs = _rope_tabs(rope_ref)
        low = _low_half(tile)
        bdm = bd_ref[...]

        def put_kv(ref, blk, first_kv):
            sw = pltpu.roll(blk, HEAD_DIM, 1)
            ref[:, LANES * first_kv:LANES * (first_kv + 1)] = jnp.where(low, blk, sw).astype(BF16)
            ref[:, LANES * (first_kv + 1):LANES * (first_kv + 2)] = jnp.where(low, sw, blk).astype(BF16)

        def project(c):
            return _dot_nt(hb, w_ref[CHUNK * c:CHUNK * (c + 1), :])

        n_chunks = ATTN_IN // CHUNK
        per = CHUNK // LANES
        nxt = project(0)
        for c in range(n_chunks):
            cur = nxt
            if c + 1 < n_chunks:
                nxt = project(c + 1)
            col = CHUNK * c
            if col >= QK_W + N_KV * HEAD_DIM:
                g_ref[:, col - QK_W - N_KV * HEAD_DIM:col - QK_W - N_KV * HEAD_DIM + CHUNK] = cur.astype(BF16)
            elif col >= QK_W:
                for t in range(per):
                    put_kv(vd_ref, cur[:, LANES * t:LANES * (t + 1)], (col - QK_W) // HEAD_DIM + 2 * t)
            else:
                qk_ref[:, col:col + CHUNK] = cur
                for t in range(per):
                    b = per * c + t
                    blk = cur[:, LANES * t:LANES * (t + 1)]
                    ms = _group_mean(blk * blk, bdm)
                    y = (blk * lax.rsqrt(ms + NORM_EPS)) * gain_ref[:, LANES * b:LANES * (b + 1)]
                    rp = _rope(y, tabs)
                    if b < D // LANES:
                        rp = rp * (HEAD_DIM ** -0.5)
                        qs_ref[:, 2 * LANES * b:2 * LANES * b + LANES] = jnp.where(low, rp, 0.0).astype(BF16)
                        qs_ref[:, 2 * LANES * b + LANES:2 * LANES * (b + 1)] = jnp.where(low, 0.0, rp).astype(BF16)
                    else:
                        put_kv(kd_ref, rp, 2 * (b - D // LANES))

    row = lambda w: pl.BlockSpec((tile, w), lambda i: (i, 0))
    return _call(
        body, name=f"attn_in_proj_{j}", grid=(seq // tile,),
        out_shape=(jax.ShapeDtypeStruct((seq, QK_W), F32), jax.ShapeDtypeStruct((seq, N_HEADS * LANES), BF16),
                   jax.ShapeDtypeStruct((seq, KX_W), BF16), jax.ShapeDtypeStruct((seq, KX_W), BF16),
                   jax.ShapeDtypeStruct((seq, D), BF16)),
        in_specs=[row(D), row(3 * LANES), _mod_row_spec(layer, NORM_ROW), _mod_spec(layer), _const_spec((ATTN_IN, D)),
                  _const_spec((1, QK_W)), _const_spec((LANES, LANES))],
        out_specs=(row(QK_W), row(N_HEADS * LANES), row(KX_W), row(KX_W), row(D)),
        compiler_params=_cparams(1),
    )(x, rope, rows, mod, w_t, gain, bd)


def _band_mask(n, rows, keys_on_rows):
    shape = (2 * QBLK, rows) if keys_on_rows else (rows, 2 * QBLK)
    qi = lax.broadcasted_iota(jnp.int32, shape, 1 if keys_on_rows else 0) & (QBLK - 1)
    kj = lax.broadcasted_iota(jnp.int32, shape, 0 if keys_on_rows else 1)
    diff = QBLK + qi - kj
    first_key = jnp.where(n > 0, 0, QBLK)
    return (diff >= 0) & (diff < QBLK) & (kj >= first_key)


def _pair_up(st, low):
    return jnp.concatenate([jnp.where(low, st[0:QBLK], st[QBLK:2 * QBLK]),
                            jnp.where(low, st[2 * QBLK:3 * QBLK], st[3 * QBLK:4 * QBLK])], axis=1)


def _attn_forward(sinks, qs, kd, vd, j):
    seq = qs.shape[0]
    per = 2
    nb = seq // (per * QBLK)

    def body(sink_ref, q_ref, kp_ref, kc_ref, vp_ref, vc_ref, o_ref):
        n = pl.program_id(0)
        low = _low_half(QBLK)
        rowi = lax.broadcasted_iota(jnp.int32, (4 * QBLK, 1), 0)
        groups = per * N_KV

        def keys(p_ref, c_ref, blk, kv):
            cols = slice(LANES * kv, LANES * (kv + 1))
            if blk == 0:
                return jnp.concatenate([p_ref[:, cols], c_ref[0:QBLK, cols]], axis=0)
            return c_ref[QBLK * (blk - 1):QBLK * (blk + 1), cols]

        def scores(g):
            blk, kv = divmod(g, N_KV)
            q = jnp.concatenate([q_ref[QBLK * blk:QBLK * (blk + 1), LANES * h:LANES * (h + 1)]
                                 for h in range(4 * kv, 4 * kv + 4)], axis=0)
            return _dot_nt(q, keys(kp_ref, kc_ref, blk, kv))

        nxt = scores(0)
        for g in range(groups):
            blk, kv = divmod(g, N_KV)
            ok = _band_mask(n if blk == 0 else 1, 4 * QBLK, False)
            s = jnp.where(ok, nxt, -1e30)
            if g + 1 < groups:
                nxt = scores(g + 1)
            sink = jnp.where(rowi < QBLK, sink_ref[j, 4 * kv],
                             jnp.where(rowi < 2 * QBLK, sink_ref[j, 4 * kv + 1],
                                       jnp.where(rowi < 3 * QBLK, sink_ref[j, 4 * kv + 2], sink_ref[j, 4 * kv + 3])))
            m = jnp.maximum(jnp.max(s, axis=1, keepdims=True), sink)
            p = jnp.exp(s - m)
            den = jnp.sum(p, axis=1, keepdims=True) + jnp.exp(sink - m)
            o_st = _dot((p / den).astype(BF16), keys(vp_ref, vc_ref, blk, kv))
            o_ref[QBLK * blk:QBLK * (blk + 1), 2 * LANES * kv:2 * LANES * (kv + 1)] = _pair_up(o_st, low).astype(BF16)

    cur = lambda w: pl.BlockSpec((per * QBLK, w), lambda n: (n, 0))
    prev = lambda w: pl.BlockSpec((QBLK, w), lambda n: (jnp.maximum(per * n - 1, 0), 0))
    return _call(
        body, name=f"attn_forward_{j}", grid=(nb,),
        out_shape=jax.ShapeDtypeStruct((seq, D), BF16),
        in_specs=[pl.BlockSpec(memory_space=pltpu.SMEM), cur(N_HEADS * LANES), prev(KX_W), cur(KX_W), prev(KX_W), cur(KX_W)],
        out_specs=cur(D),
        compiler_params=_cparams(1),
    )(sinks, qs, kd, kd, vd, vd)


def _attn_out_proj(x, o, g, w, j, mod, layer, tile):
    seq = x.shape[0]

    def body(x_ref, o_ref, g_ref, w_ref, mod_ref, xo_ref, br_ref):
        gv = g_ref[...].astype(F32)
        u = (o_ref[...].astype(F32) * (gv * _sigmoid(gv))).astype(BF16)
        br = _dot(u, w_ref[...])
        br_ref[...] = br.astype(BF16)
        xo_ref[...] = x_ref[...] + mod_ref[2:3, :] * br

    row = pl.BlockSpec((tile, D), lambda i: (i, 0))
    return _call(
        body, name=f"attn_out_proj_{j}", grid=(seq // tile,),
        out_shape=(jax.ShapeDtypeStruct((seq, D), F32), jax.ShapeDtypeStruct((seq, D), BF16)),
        in_specs=[row, row, row, _const_spec((D, D)), _mod_spec(layer)],
        out_specs=(row, row),
        compiler_params=_cparams(1),
    )(x, o, g, w, mod)


def _attn_out_proj_bwd(dxn, br, o, g, w, j, mod, layer, tile, after):
    seq = dxn.shape[0]
    steps = seq // tile

    def body(dxn_ref, br_ref, o_ref, g_ref, w_ref, mod_ref, after_ref, do_ref, dg_ref, dw_ref, dgate_ref, dw_acc):
        i = pl.program_id(0)

        @pl.when(i == 0)
        def _():
            dw_acc[...] = jnp.zeros_like(dw_acc)
            dgate_ref[...] = jnp.zeros_like(dgate_ref)

        dxn_v, ov, gv = dxn_ref[...], o_ref[...].astype(F32), g_ref[...].astype(F32)
        dgate_ref[...] += jnp.sum(dxn_v * br_ref[...].astype(F32), axis=0, keepdims=True)
        dbr = (dxn_v * mod_ref[2:3, :]).astype(BF16)
        du = _dot_nt(dbr, w_ref[...])
        sg = _sigmoid(gv)
        sl = gv * sg
        dw_acc[...] += _dot_tn((ov * sl).astype(BF16), dbr)
        do = du * sl
        dg_ref[...] = (du * ov * (sg * (1.0 + gv * (1.0 - sg)))).astype(BF16)
        low = _low_half(tile)
        for b in range(D // LANES):
            blk = do[:, LANES * b:LANES * (b + 1)]
            do_ref[:, 2 * LANES * b:2 * LANES * b + LANES] = jnp.where(low, blk, 0.0).astype(BF16)
            do_ref[:, 2 * LANES * b + LANES:2 * LANES * (b + 1)] = jnp.where(low, 0.0, blk).astype(BF16)

        @pl.when(i == steps - 1)
        def _():
            dw_ref[...] = dw_acc[...].astype(BF16)

    row = lambda w_: pl.BlockSpec((tile, w_), lambda i: (i, 0))
    return _call(
        body, name=f"attn_out_proj_bwd_{j}", grid=(steps,),
        out_shape=(jax.ShapeDtypeStruct((seq, N_HEADS * LANES), BF16), jax.ShapeDtypeStruct((seq, D), BF16),
                   jax.ShapeDtypeStruct((D, D), BF16), jax.ShapeDtypeStruct((1, D), F32)),
        in_specs=[row(D), row(D), row(D), row(D), _const_spec((D, D)), _mod_spec(layer), ANY_SPEC],
        out_specs=(row(N_HEADS * LANES), row(D), pl.BlockSpec((D, D), lambda i: (0, 0)),
                   pl.BlockSpec((1, D), lambda i: (0, 0))),
        scratch_shapes=[pltpu.VMEM((D, D), F32)],
        compiler_params=_cparams(1),
    )(dxn, br, o, g, w, mod, after)


def _attn_backward(sinks, qs, dos, kd, vd, j, after):
    seq = qs.shape[0]
    per = 2
    nb = seq // (per * QBLK)
    kw = N_KV * HEAD_DIM

    def body(sink_ref, q_ref, do_ref, kp_ref, kc_ref, vp_ref, vc_ref, after_ref,
             dq_ref, dk_even, dk_odd, dv_even, dv_odd, dsink_ref, carry_k, carry_v, sink_acc):
        n = pl.program_id(0)

        @pl.when(n == 0)
        def _():
            carry_k[...] = jnp.zeros_like(carry_k)
            carry_v[...] = jnp.zeros_like(carry_v)
            sink_acc[...] = jnp.zeros_like(sink_acc)

        @pl.when(n < nb)
        def _():
            low = _low_half(QBLK)
            lane_q = lax.broadcasted_iota(jnp.int32, (1, 2 * QBLK), 1)
            groups = per * 2 * N_KV

            def keys(p_ref, c_ref, blk, kv):
                cols = slice(LANES * kv, LANES * (kv + 1))
                if blk == 0:
                    return jnp.concatenate([p_ref[:, cols], c_ref[0:QBLK, cols]], axis=0)
                return c_ref[QBLK * (blk - 1):QBLK * (blk + 1), cols]

            def first_products(g):
                blk, rest = divmod(g, 2 * N_KV)
                kv, half = divmod(rest, 2)
                heads = (4 * kv + half, 4 * kv + 2 + half)
                rows = slice(QBLK * blk, QBLK * (blk + 1))
                q = jnp.concatenate([q_ref[rows, LANES * h:LANES * (h + 1)] for h in heads], axis=0)
                do = jnp.concatenate([do_ref[rows, LANES * h:LANES * (h + 1)] for h in heads], axis=0)
                kk = keys(kp_ref, kc_ref, blk, kv)
                return heads, q, do, kk, _dot_nt(kk, q), _dot_nt(keys(vp_ref, vc_ref, blk, kv), do)

            nxt = first_products(0)
            dk_parts, dv_parts = [[], []], [[], []]
            dq_h, dk_kv, dv_kv = [], None, None
            for g in range(groups):
                blk, rest = divmod(g, 2 * N_KV)
                heads, q, do, kk, s_raw, dp_raw = nxt
                if g + 1 < groups:
                    nxt = first_products(g + 1)
                ok = _band_mask(n if blk == 0 else 1, 2 * QBLK, True)
                st = jnp.where(ok, s_raw, -1e30)
                sink = jnp.where(lane_q < QBLK, sink_ref[j, heads[0]], sink_ref[j, heads[1]])
                m = jnp.maximum(jnp.max(st, axis=0, keepdims=True), sink)
                e = jnp.exp(st - m)
                e_sink = jnp.exp(sink - m)
                inv = 1.0 / (jnp.sum(e, axis=0, keepdims=True) + e_sink)
                p = e * inv
                pdp = p * dp_raw
                delta = jnp.sum(pdp, axis=0, keepdims=True)
                ds = (pdp - p * delta).astype(BF16)
                sink_acc[rest:rest + 1, :] -= e_sink * inv * delta
                dk_g, dv_g = _dot(ds, q), _dot(p.astype(BF16), do)
                dk_kv = dk_g if dk_kv is None else dk_kv + dk_g
                dv_kv = dv_g if dv_kv is None else dv_kv + dv_g
                dq_h.append(_dot_tn(ds, kk))
                if g % 2 == 1:
                    kv = rest // 2
                    for t in range(2):
                        dq_ref[QBLK * blk:QBLK * (blk + 1), LANES * (2 * kv + t):LANES * (2 * kv + t + 1)] = jnp.where(
                            low, dq_h[0][QBLK * t:QBLK * (t + 1)], dq_h[1][QBLK * t:QBLK * (t + 1)])
                    dk_parts[blk].append(dk_kv + pltpu.roll(dk_kv, HEAD_DIM, 1))
                    dv_parts[blk].append(dv_kv + pltpu.roll(dv_kv, HEAD_DIM, 1))
                    dq_h, dk_kv, dv_kv = [], None, None

            def order(parts, lo, hi):
                return jnp.concatenate([jnp.where(low, parts[0][lo:hi], parts[1][lo:hi]),
                                        jnp.where(low, parts[2][lo:hi], parts[3][lo:hi])], axis=1)

            dk_odd[...] = carry_k[...] + order(dk_parts[0], 0, QBLK)
            dv_odd[...] = (carry_v[...] + order(dv_parts[0], 0, QBLK)).astype(BF16)
            dk_even[...] = order(dk_parts[0], QBLK, 2 * QBLK) + order(dk_parts[1], 0, QBLK)
            dv_even[...] = (order(dv_parts[0], QBLK, 2 * QBLK) + order(dv_parts[1], 0, QBLK)).astype(BF16)
            carry_k[...] = order(dk_parts[1], QBLK, 2 * QBLK)
            carry_v[...] = order(dv_parts[1], QBLK, 2 * QBLK)

        @pl.when(n == nb)
        def _():
            dk_odd[...] = carry_k[...]
            dv_odd[...] = carry_v[...].astype(BF16)
            lane = lax.broadcasted_iota(jnp.int32, (1, LANES), 1)
            out = jnp.zeros((1, LANES), F32)
            for g in range(2 * N_KV):
                for t in range(2):
                    tot = jnp.sum(sink_acc[g:g + 1, QBLK * t:QBLK * (t + 1)], axis=1, keepdims=True)
                    out = jnp.where(lane == 4 * (g // 2) + 2 * t + g % 2, tot, out)
            dsink_ref[...] = out

    cur = lambda w: pl.BlockSpec((per * QBLK, w), lambda n: (jnp.minimum(n, nb - 1), 0))
    prev = lambda w: pl.BlockSpec((QBLK, w), lambda n: (jnp.maximum(per * n - 1, 0), 0))
    even = pl.BlockSpec((None, QBLK, kw), lambda n: (jnp.minimum(n, nb - 1), 0, 0))
    odd = pl.BlockSpec((None, QBLK, kw), lambda n: (jnp.maximum(n - 1, 0), 0, 0))
    halves = lambda dt: jax.ShapeDtypeStruct((nb, QBLK, kw), dt)
    dq, dk_e, dk_o, dv_e, dv_o, dsink = _call(
        body, name=f"attn_backward_{j}", grid=(nb + 1,),
        out_shape=(jax.ShapeDtypeStruct((seq, D), F32), halves(F32), halves(F32), halves(BF16), halves(BF16),
                   jax.ShapeDtypeStruct((1, LANES), F32)),
        in_specs=[pl.BlockSpec(memory_space=pltpu.SMEM), cur(N_HEADS * LANES), cur(N_HEADS * LANES), prev(KX_W), cur(KX_W),
                  prev(KX_W), cur(KX_W), ANY_SPEC],
        out_specs=(cur(D), even, odd, even, odd, pl.BlockSpec((1, LANES), lambda n: (0, 0))),
        scratch_shapes=[pltpu.VMEM((QBLK, kw), F32), pltpu.VMEM((QBLK, kw), F32), pltpu.VMEM((2 * N_KV, 2 * QBLK), F32)],
        compiler_params=_cparams(1),
    )(sinks, qs, dos, kd, kd, vd, vd, after)
    return dq, (dk_e, dk_o), (dv_e, dv_o), dsink


def _in_proj_tail(x_ref, dxn_ref, ng_ref, mod_ref, w_ref, dproj, dx_ref, dw_acc, vec_acc):
    ng, sc, sh = ng_ref[...], mod_ref[1:2, :], mod_ref[0:1, :]
    xh, r, h = _norm_mod(x_ref[...], ng, sc, sh)
    dh = _dot(dproj, w_ref[...])
    dw_acc[...] += _dot_tn(dproj, h.astype(BF16))
    vec_acc[0:1, :] += jnp.sum(dh, axis=0, keepdims=True)
    vec_acc[1:2, :] += jnp.sum(dh * xh, axis=0, keepdims=True)
    dxh = dh * (ng * (1.0 + sc))
    dx_ref[...] = dxn_ref[...] + r * (dxh - xh * jnp.mean(dxh * xh, axis=-1, keepdims=True))


def _tail_finish(ng_ref, mod_ref, dw_ref, vec_ref, dw_acc, vec_acc):
    dw_ref[...] = dw_acc[...].astype(BF16)
    a = vec_acc[1:2, :]
    vec_ref[...] = jnp.zeros_like(vec_ref)
    vec_ref[0:1, :] = vec_acc[0:1, :]
    vec_ref[1:2, :] = a * ng_ref[...]
    vec_ref[3:4, :] = a * (1.0 + mod_ref[1:2, :])


def _attn_in_proj_bwd(x, dxn, rope, qk_raw, dq, dk, dv, dg, rows, mod, layer, w_t, j, gain, bd, tile):
    seq = x.shape[0]
    steps = seq // tile

    assert tile == 2 * QBLK
    (dk_e, dk_o), (dv_e, dv_o) = dk, dv

    def body(x_ref, dxn_ref, rope_ref, qk_ref, dq_ref, dke_ref, dko_ref, dve_ref, dvo_ref, dg_ref, ng_ref, mod_ref, w_ref, gain_ref,
             bd_ref, dx_ref, dw_ref, vec_ref, dgain_ref, dproj, dw_acc, vec_acc):
        i = pl.program_id(0)

        @pl.when(i == 0)
        def _():
            dw_acc[...] = jnp.zeros_like(dw_acc)
            vec_acc[...] = jnp.zeros_like(vec_acc)
            dgain_ref[...] = jnp.zeros_like(dgain_ref)

        tabs = _rope_tabs(rope_ref)
        bdm = bd_ref[...]
        for b in range(QK_W // LANES):
            cols = slice(LANES * b, LANES * (b + 1))
            raw = qk_ref[:, cols]
            if b < D // LANES:
                dy = dq_ref[:, cols] * (HEAD_DIM ** -0.5)
            else:
                kcols = slice(LANES * (b - D // LANES), LANES * (b + 1 - D // LANES))
                dy = jnp.concatenate([dke_ref[:, kcols], dko_ref[:, kcols]], axis=0)
            dy = _rope_bwd(dy, tabs)
            rr = lax.rsqrt(_group_mean(raw * raw, bdm) + NORM_EPS)
            xh = raw * rr
            dgain_ref[:, cols] += jnp.sum(dy * xh, axis=0, keepdims=True)
            dxh = dy * gain_ref[:, cols]
            dproj[:, cols] = (rr * (dxh - xh * _group_mean(dxh * xh, bdm))).astype(BF16)
        dproj[0:QBLK, QK_W:QK_W + N_KV * HEAD_DIM] = dve_ref[...]
        dproj[QBLK:2 * QBLK, QK_W:QK_W + N_KV * HEAD_DIM] = dvo_ref[...]
        dproj[:, QK_W + N_KV * HEAD_DIM:] = dg_ref[...]
        _in_proj_tail(x_ref, dxn_ref, ng_ref, mod_ref, w_ref, dproj[...], dx_ref, dw_acc, vec_acc)

        @pl.when(i == steps - 1)
        def _():
            _tail_finish(ng_ref, mod_ref, dw_ref, vec_ref, dw_acc, vec_acc)

    row = lambda w: pl.BlockSpec((tile, w), lambda i: (i, 0))
    kblock = pl.BlockSpec((None, QBLK, N_KV * HEAD_DIM), lambda i: (i, 0, 0))
    fixed = lambda shape: pl.BlockSpec(shape, lambda i: (0,) * len(shape))
    return _call(
        body, name=f"attn_in_proj_bwd_{j}", grid=(steps,),
        out_shape=(jax.ShapeDtypeStruct((seq, D), F32), jax.ShapeDtypeStruct((ATTN_IN, D), BF16),
                   jax.ShapeDtypeStruct((8, D), F32), jax.ShapeDtypeStruct((1, QK_W), F32)),
        in_specs=[row(D), row(D), row(3 * LANES), row(QK_W), row(D), kblock, kblock, kblock, kblock, row(D),
                  _mod_row_spec(layer, NORM_ROW), _mod_spec(layer), _const_spec((ATTN_IN, D)), _const_spec((1, QK_W)),
                  _const_spec((LANES, LANES))],
        out_specs=(row(D), fixed((ATTN_IN, D)), fixed((8, D)), fixed((1, QK_W))),
        scratch_shapes=[pltpu.VMEM((tile, ATTN_IN), BF16), pltpu.VMEM((ATTN_IN, D), F32), pltpu.VMEM((8, D), F32)],
        compiler_params=_cparams(1),
    )(x, dxn, rope, qk_raw, dq, dk_e, dk_o, dv_e, dv_o, dg, rows, mod, w_t, gain, bd)


def _pool_in_proj(x, rows, mod, layer, w_t, j, tile):
    seq = x.shape[0]

    def body(x_ref, ng_ref, mod_ref, w_ref, v_ref, g_ref):
        _, _, h = _norm_mod(x_ref[...], ng_ref[...], mod_ref[1:2, :], mod_ref[0:1, :])
        proj = _dot_nt(h.astype(BF16), w_ref[...])
        v_ref[...] = proj[:, :D].astype(BF16)
        g_ref[...] = proj[:, D:].astype(BF16)

    row = pl.BlockSpec((tile, D), lambda i: (i, 0))
    return _call(
        body, name=f"pool_in_proj_{j}", grid=(seq // tile,),
        out_shape=(jax.ShapeDtypeStruct((seq, D), BF16), jax.ShapeDtypeStruct((seq, D), BF16)),
        in_specs=[row, _mod_row_spec(layer, NORM_ROW), _mod_spec(layer), _const_spec((POOL_IN, D))],
        out_specs=(row, row),
        compiler_params=_cparams(1),
    )(x, rows, mod, w_t)


PAD = 8


def _window_sums(ext, lo, hi, forward):
    gw = D // len(POOL_WINDOWS)
    planes = []
    for gi, w in enumerate(POOL_WINDOWS):
        cols = slice(gw * gi, gw * (gi + 1))
        src, k = 0, 1
        while k < w:
            d = k if forward else -k
            ext[1 - src, lo:hi, cols] = ext[src, lo:hi, cols] + ext[src, lo + d:hi + d, cols]
            src, k = 1 - src, 2 * k
        planes.append(src)
    return planes


def _pooled(ext, v_ref, first, tile):
    t_abs = first + lax.broadcasted_iota(jnp.int32, (tile, 1), 0)
    top = PAD + HALO
    planes = _window_sums(ext, PAD, top + tile, False)
    outs = []
    gw = D // len(POOL_WINDOWS)
    for gi, w in enumerate(POOL_WINDOWS):
        cols = slice(gw * gi, gw * (gi + 1))
        cnt = jnp.minimum(t_abs + 1, w).astype(F32)
        outs.append(ext[planes[gi], top:top + tile, cols] / cnt - v_ref[:, cols].astype(F32))
    return jnp.concatenate(outs, axis=1)


def _fill_ext(ext, halo_ref, v_ref, i, tile):
    ext[0, 0:PAD, :] = jnp.zeros((PAD, D), F32)
    ext[1, 0:PAD, :] = jnp.zeros((PAD, D), F32)
    ext[0, PAD:PAD + HALO, :] = jnp.where(i == 0, 0.0, halo_ref[...].astype(F32))
    ext[0, PAD + HALO:PAD + HALO + tile, :] = v_ref[...].astype(F32)


def _group_mix(pb, wg_ref):
    gw = D // len(POOL_WINDOWS)
    return jnp.concatenate([_dot(pb[:, gw * gi:gw * (gi + 1)], wg_ref[gi]) for gi in range(len(POOL_WINDOWS))], axis=1)


def _pool_mix_out(x, v, g, wg, w_out, j, rows, mod, layer, tile, target=None):
    seq = x.shape[0]

    def body(*refs):
        if target is None:
            x_ref, v_ref, halo_ref, g_ref, wg_ref, w_ref, scale_ref, mod_ref, xo_ref, br_ref, ext = refs
        else:
            x_ref, v_ref, halo_ref, g_ref, wg_ref, w_ref, scale_ref, mod_ref, t_ref, xo_ref, br_ref, loss_ref, ext = refs
        i = pl.program_id(0)
        _fill_ext(ext, halo_ref, v_ref, i, tile)
        pb = _pooled(ext, v_ref, i * tile, tile).astype(BF16)
        ms = _group_mix(pb, wg_ref) * scale_ref[...]
        gv = g_ref[...].astype(F32)
        u = (ms * (gv * _sigmoid(gv))).astype(BF16)
        br = _dot(u, w_ref[...])
        br_ref[...] = br.astype(BF16)
        y = x_ref[...] + mod_ref[2:3, :] * br
        if target is None:
            xo_ref[...] = y
        else:
            @pl.when(i == 0)
            def _():
                loss_ref[...] = jnp.zeros_like(loss_ref)

            e = y - t_ref[...]
            xo_ref[...] = e * (1.0 / D)
            loss_ref[...] += 0.5 * jnp.sum(jnp.mean(e * e, axis=-1, keepdims=True), axis=0, keepdims=True)

    row = pl.BlockSpec((tile, D), lambda i: (i, 0))
    halo = pl.BlockSpec((HALO, D), lambda i: (jnp.maximum(i * (tile // HALO) - 1, 0), 0))
    extra_in, extra_out, extra_shape = ([], (), ()) if target is None else (
        [row], (pl.BlockSpec((1, LANES), lambda i: (0, 0)),), (jax.ShapeDtypeStruct((1, LANES), F32),))
    return _call(
        body, name=f"pool_mix_out_{j}", grid=(seq // tile,),
        out_shape=(jax.ShapeDtypeStruct((seq, D), F32), jax.ShapeDtypeStruct((seq, D), BF16)) + extra_shape,
        in_specs=[row, row, halo, row, _const_spec(wg.shape), _const_spec((D, D)), _mod_row_spec(layer, POOL_SCALE_ROW),
                  _mod_spec(layer)] + extra_in,
        out_specs=(row, row) + extra_out,
        scratch_shapes=[pltpu.VMEM((2, tile + HALO + PAD, D), F32)],
        compiler_params=_cparams(1),
    )(x, v, v, g, wg, w_out, rows, mod, *(() if target is None else (target,)))


def _pool_mix_out_bwd(dxn, br, v, g, wg, w_out, j, rows, mod, layer, tile, after):
    seq = dxn.shape[0]
    steps = seq // tile
    ng_ = len(POOL_WINDOWS)
    gw = D // ng_

    def body(dxn_ref, br_ref, v_ref, halo_ref, g_ref, wg_ref, w_ref, scale_ref, mod_ref, after_ref,
             dpool_ref, dg_ref, dw_ref, dwg_ref, vec_ref, ext, dw_acc, dwg_acc):
        i = pl.program_id(0)

        @pl.when(i == 0)
        def _():
            dw_acc[...] = jnp.zeros_like(dw_acc)
            dwg_acc[...] = jnp.zeros_like(dwg_acc)
            vec_ref[...] = jnp.zeros_like(vec_ref)

        _fill_ext(ext, halo_ref, v_ref, i, tile)
        pb = _pooled(ext, v_ref, i * tile, tile).astype(BF16)
        mixed = _group_mix(pb, wg_ref)
        scale = scale_ref[...]
        ms = mixed * scale
        gv, dxn_v = g_ref[...].astype(F32), dxn_ref[...]
        sg = _sigmoid(gv)
        sl = gv * sg
        vec_ref[0:1, :] += jnp.sum(dxn_v * br_ref[...].astype(F32), axis=0, keepdims=True)
        dbr = (dxn_v * mod_ref[2:3, :]).astype(BF16)
        du = _dot_nt(dbr, w_ref[...])
        dw_acc[...] += _dot_tn((ms * sl).astype(BF16), dbr)
        dms = du * sl
        dg_ref[...] = (du * ms * (sg * (1.0 + gv * (1.0 - sg)))).astype(BF16)
        vec_ref[1:2, :] += jnp.sum(dms * mixed, axis=0, keepdims=True)
        dmx = (dms * scale).astype(BF16)
        for gi in range(ng_):
            cols = slice(gw * gi, gw * (gi + 1))
            dpool_ref[:, cols] = _dot_nt(dmx[:, cols], wg_ref[gi])
            dwg_acc[gi] += _dot_tn(pb[:, cols], dmx[:, cols])

        @pl.when(i == steps - 1)
        def _():
            dw_ref[...] = dw_acc[...].astype(BF16)
            dwg_ref[...] = dwg_acc[...].astype(BF16)

    row = pl.BlockSpec((tile, D), lambda i: (i, 0))
    halo = pl.BlockSpec((HALO, D), lambda i: (jnp.maximum(i * (tile // HALO) - 1, 0), 0))
    fixed = lambda shape: pl.BlockSpec(shape, lambda i: (0,) * len(shape))
    return _call(
        body, name=f"pool_mix_out_bwd_{j}", grid=(steps,),
        out_shape=(jax.ShapeDtypeStruct((seq, D), F32), jax.ShapeDtypeStruct((seq, D), BF16),
                   jax.ShapeDtypeStruct((D, D), BF16), jax.ShapeDtypeStruct((ng_, gw, gw), BF16),
                   jax.ShapeDtypeStruct((8, D), F32)),
        in_specs=[row, row, row, halo, row, _const_spec(wg.shape), _const_spec((D, D)), _mod_row_spec(layer, POOL_SCALE_ROW),
                  _mod_spec(layer), ANY_SPEC],
        out_specs=(row, row, fixed((D, D)), fixed((ng_, gw, gw)), fixed((8, D))),
        scratch_shapes=[pltpu.VMEM((2, tile + HALO + PAD, D), F32), pltpu.VMEM((D, D), F32), pltpu.VMEM((ng_, gw, gw), F32)],
        compiler_params=_cparams(1),
    )(dxn, br, v, v, g, wg, w_out, rows, mod, after)


def _pool_in_proj_bwd(x, dxn, dpool, dg, rows, mod, layer, w_t, j, tile, after):
    seq = x.shape[0]
    steps = seq // tile
    gw = D // len(POOL_WINDOWS)

    def body(x_ref, dxn_ref, dp_ref, halo_ref, dg_ref, ng_ref, mod_ref, w_ref, after_ref, dx_ref, dw_ref, vec_ref,
             ext, dproj, dw_acc, vec_acc):
        i = pl.program_id(0)

        @pl.when(i == 0)
        def _():
            dw_acc[...] = jnp.zeros_like(dw_acc)
            vec_acc[...] = jnp.zeros_like(vec_acc)

        t_abs = i * tile + lax.broadcasted_iota(jnp.int32, (tile, 1), 0)
        last = i == steps - 1
        ext[0, tile + HALO:tile + HALO + PAD, :] = jnp.zeros((PAD, D), F32)
        ext[1, tile + HALO:tile + HALO + PAD, :] = jnp.zeros((PAD, D), F32)
        for gi, w in enumerate(POOL_WINDOWS):
            cols = slice(gw * gi, gw * (gi + 1))
            cnt = jnp.minimum(t_abs + 1, w).astype(F32)
            ext[0, 0:tile, cols] = dp_ref[:, cols] / cnt
            ext[0, tile:tile + HALO, cols] = jnp.where(last, 0.0, halo_ref[:, cols] * (1.0 / w))
        planes = _window_sums(ext, 0, tile + HALO, True)
        for gi, w in enumerate(POOL_WINDOWS):
            cols = slice(gw * gi, gw * (gi + 1))
            dproj[:, cols] = (ext[planes[gi], 0:tile, cols] - dp_ref[:, cols]).astype(BF16)
        dproj[:, D:] = dg_ref[...]
        _in_proj_tail(x_ref, dxn_ref, ng_ref, mod_ref, w_ref, dproj[...], dx_ref, dw_acc, vec_acc)

        @pl.when(last)
        def _():
            _tail_finish(ng_ref, mod_ref, dw_ref, vec_ref, dw_acc, vec_acc)

    row = pl.BlockSpec((tile, D), lambda i: (i, 0))
    halo = pl.BlockSpec((HALO, D), lambda i: (jnp.minimum((i + 1) * (tile // HALO), seq // HALO - 1), 0))
    fixed = lambda shape: pl.BlockSpec(shape, lambda i: (0,) * len(shape))
    return _call(
        body, name=f"pool_in_proj_bwd_{j}", grid=(steps,),
        out_shape=(jax.ShapeDtypeStruct((seq, D), F32), jax.ShapeDtypeStruct((POOL_IN, D), BF16),
                   jax.ShapeDtypeStruct((8, D), F32)),
        in_specs=[row, row, row, halo, row, _mod_row_spec(layer, NORM_ROW), _mod_spec(layer), _const_spec((POOL_IN, D)), ANY_SPEC],
        out_specs=(row, fixed((POOL_IN, D)), fixed((8, D))),
        scratch_shapes=[pltpu.VMEM((2, tile + HALO + PAD, D), F32), pltpu.VMEM((tile, POOL_IN), BF16), pltpu.VMEM((POOL_IN, D), F32),
                        pltpu.VMEM((8, D), F32)],
        compiler_params=_cparams(1),
    )(x, dxn, dpool, dpool, dg, rows, mod, w_t, after)


def _build_vec(vecs, gates, pool_vecs, gains, dsinks, loss_part):
    def body(v0, v1, v2, v3, g0, g2, p0, p1, n0, n1, s0, s1, loss_ref, out):
        out[...] = jnp.zeros_like(out)
        for i, v in enumerate((v0, v1, v2, v3)):
            out[3 * i:3 * i + 2, :] = v[0:2, :]
            out[12 + i:13 + i, :] = v[3:4, :]
        out[2:3, :] = g0[...]
        out[8:9, :] = g2[...]
        for j, (p, n, s) in enumerate(((p0, n0, s0), (p1, n1, s1))):
            out[3 * (2 * j + 1) + 2:3 * (2 * j + 1) + 3, :] = p[0:1, :]
            out[22 + j:23 + j, :] = p[1:2, :]
            out[16 + j:17 + j, :] = n[:, 0:D]
            out[18 + j:19 + j, 0:QK_W - D] = n[:, D:QK_W]
            out[20 + j:21 + j, 0:LANES] = s[...]
        out[24:25, 0:LANES] = loss_ref[...]

    vm = pl.BlockSpec(memory_space=pltpu.VMEM)
    args = (*vecs, gates[0], gates[2], *pool_vecs, *gains, *dsinks, loss_part)
    return _call(
        body, name="build_vec",
        out_shape=jax.ShapeDtypeStruct((VEC_ROWS, D), F32),
        in_specs=[vm] * len(args), out_specs=vm,
        compiler_params=_cparams(),
    )(*args)


def _sum_devices(g, after):
    rows = g.shape[1]

    def body(g_ref, after_ref, tot_ref, fold_ref):
        tot = g_ref[0]
        for p in range(1, N_DEV):
            tot = tot + g_ref[p]
        tot_ref[...] = tot
        f = tot[16:24, 0:LANES]
        for b in range(1, D // LANES):
            f = f + tot[16:24, LANES * b:LANES * (b + 1)]
        fold_ref[...] = f + pltpu.roll(f, HEAD_DIM, 1)

    return _call(
        body, name="sum_devices",
        out_shape=(jax.ShapeDtypeStruct((rows, D), F32), jax.ShapeDtypeStruct((8, LANES), F32)),
        in_specs=[pl.BlockSpec(memory_space=pltpu.VMEM), ANY_SPEC],
        out_specs=(pl.BlockSpec(memory_space=pltpu.VMEM), pl.BlockSpec(memory_space=pltpu.VMEM)),
        compiler_params=_cparams(),
    )(g, after)


def _adamw_small(params):
    n = len(params)

    def body(*refs):
        ins, outs = refs[:4 * n], refs[4 * n:]
        for p in range(n):
            w_ref, g_ref, m_ref, v_ref = ins[4 * p:4 * p + 4]
            outs[3 * p][...], outs[3 * p + 1][...], outs[3 * p + 2][...] = _adamw(w_ref[...], g_ref[...], m_ref[...], v_ref[...])

    vm = pl.BlockSpec(memory_space=pltpu.VMEM)
    out = _call(
        body, name="adamw_small",
        out_shape=tuple(jax.ShapeDtypeStruct(w.shape, F32) for (w, _, _, _) in params for _ in range(3)),
        in_specs=[vm] * (4 * n), out_specs=tuple([vm] * (3 * n)),
        compiler_params=_cparams(),
    )(*[a for p in params for a in p])
    return [tuple(out[3 * p:3 * p + 3]) for p in range(n)]


def _adamw_shards(name, me, fulls, lands, w, m, v, transpose, axis=0):
    nl = w.shape[0]
    wshape = w.shape[1:]
    own_shape = lands[0].shape[1:]

    def body(me_ref, *refs):
        own_refs, land_refs = refs[:nl], refs[nl:2 * nl]
        w_ref, m_ref, v_ref, g_out, d_out, m_out, v_out = refs[2 * nl:]
        layer = pl.program_id(0)
        for l in range(nl):
            @pl.when(layer == l)
            def _(l=l):
                g = own_refs[l][...].astype(F32)
                for k in range(N_DEV - 1):
                    g = g + land_refs[l][k].astype(F32)
                if transpose:
                    g = g.T
                g_out[...] = g
                d_out[...], m_out[...], v_out[...] = _adamw(w_ref[...], g, m_ref[...], v_ref[...])

    def own_index(l_, me_ref):
        idx = [0] * len(own_shape)
        idx[axis] = me_ref[0]
        return tuple(idx)

    own_spec = pl.BlockSpec(tuple(own_shape), own_index)
    land_spec = pl.BlockSpec((N_DEV - 1,) + tuple(own_shape), lambda l_, me_ref: (0,) * (1 + len(own_shape)))
    wspec = pl.BlockSpec((None,) + tuple(wshape), lambda l_, me_ref: (l_,) + (0,) * len(wshape))
    return _call(
        body, name=name,
        grid_spec=pltpu.PrefetchScalarGridSpec(num_scalar_prefetch=1, grid=(nl,),
                                               in_specs=[own_spec] * nl + [land_spec] * nl + [wspec] * 3,
                                               out_specs=(wspec,) * 4),
        out_shape=tuple(jax.ShapeDtypeStruct(w.shape, F32) for _ in range(4)),
        compiler_params=_cparams(1),
    )(me.reshape(1), *fulls, *lands, w, m, v)


def _constants():
    lane = np.arange(LANES)
    bd = (lane[:, None] // HEAD_DIM == lane[None, :] // HEAD_DIM).astype(np.float32)
    half = ROT_DIM // 2
    inv_freq = ROPE_THETA ** (-jnp.arange(half, dtype=F32) * 2.0 / ROT_DIM)
    invf = jnp.tile(inv_freq, LANES // half).reshape(1, LANES)
    return jnp.asarray(bd, BF16), invf


def kernel(x, c, positions, ada_w, ada_b, norm_g, attn_w_in, attn_q_norm, attn_k_norm, attn_sinks, attn_w_out, pool_w_in, pool_w_group, pool_scale, pool_w_out, loss_target, m_ada_w, m_ada_b, m_norm_g, m_attn_w_in, m_attn_q_norm, m_attn_k_norm, m_attn_sinks, m_attn_w_out, m_pool_w_in, m_pool_w_group, m_pool_scale, m_pool_w_out, v_ada_w, v_ada_b, v_norm_g, v_attn_w_in, v_attn_q_norm, v_attn_k_norm, v_attn_sinks, v_attn_w_out, v_pool_w_in, v_pool_w_group, v_pool_scale, v_pool_w_out):
    seq = x.shape[1]
    me = 4 * lax.axis_index("x") + 2 * lax.axis_index("y") + lax.axis_index("c")
    bd, invf = _constants()
    t_mm = min(512, seq)
    rope = _rope_table(positions.reshape(seq, 1), invf, t_mm)
    t_bw = min(256, seq)
    shard = pool_scale.shape[1]
    cols = ada_w.shape[2]

    w_in_rows = jnp.swapaxes(attn_w_in, 1, 2)
    w_first, = _prep_weights(me, [(w_in_rows, 0, "N")], "prep_first")
    first = jnp.concatenate([c, jnp.pad(pool_scale, ((0, 0), (0, D - shard))), jnp.zeros((5, D), F32)], axis=0)
    first = _allgather_small(first, "allgather_c", (w_first,))
    first_w, token = _gather_first_start(w_first, first)
    prepped = _prep_weights(me, [(attn_w_out, 0, "N"), (pool_w_in, 0, "T"), (pool_w_out, 0, "N"), (pool_w_group, 0, "G"),
                                 (w_in_rows, 1, "N"), (attn_w_out, 1, "N"), (pool_w_in, 1, "T"), (pool_w_out, 1, "N"),
                                 (pool_w_group, 1, "G")], "prep_rest")
    c_all = first[:, 0, :]
    scale_full = jnp.transpose(first[:, 1:3, :shard], (1, 0, 2)).reshape(2, D)
    mod_part = _ada_forward(c_all, ada_w)
    mod_all = _allgather_small(mod_part.reshape(DEPTH * N_DEV, cols), "allgather_mod", (prepped[0], rope, token))
    mod_all = mod_all.reshape(N_DEV, DEPTH, N_DEV, cols)
    mine = lax.dynamic_index_in_dim(mod_all, me, axis=2, keepdims=False)
    mod = jnp.transpose(mine, (1, 0, 2)).reshape(DEPTH, 3 * D) + ada_b
    pool_rows = jnp.stack([jnp.zeros_like(scale_full[0]), scale_full[0], jnp.zeros_like(scale_full[0]), scale_full[1]])
    mod = jnp.concatenate([mod.reshape(DEPTH, 3, D), norm_g[:, None, :], pool_rows[:, None, :],
                           jnp.zeros((DEPTH, 3, D), F32)], axis=1)
    rows = mod.reshape(DEPTH, 8, 1, D)

    groups = [prepped[0:1], prepped[1:4], prepped[4:6], prepped[6:9]]
    gaxes = [(0,), (0,), (0, 0, 1), (0, 0), (0, 0, 1)]
    first_w, token = _gather_first_forward(first_w, mod)
    rest, token = _gather_start(groups, gaxes[1:], token, "gather_start_rest")
    started = [None] + rest

    saved, weights = [], []
    h = x[0]
    for i in range(DEPTH):
        j = i // 2
        s = dict(x=h)
        if i == 0:
            w_in_t = _gather_first_wait(first_w, token)
        else:
            wts = _gather_wait(started[i + 1], gaxes[i + 1], h, f"gather_wait_{i}")
        if i % 2 == 0:
            if i > 0:
                w_in_t, w_out = wts
            s["gain"] = jnp.concatenate([jnp.tile(attn_q_norm[j], N_HEADS), jnp.tile(attn_k_norm[j], N_KV)]).reshape(1, QK_W)
            s["qk_raw"], s["qs"], s["kd"], s["vd"], s["g"] = _attn_in_proj(
                h, rope, rows, mod, i, w_in_t, j, s["gain"], bd, t_mm)
            s["o"] = _attn_forward(attn_sinks, s["qs"], s["kd"], s["vd"], j)
            if i == 0:
                w_out, = _gather_wait(started[1], gaxes[1], s["o"], "gather_wait_0_out")
            h, s["br"] = _attn_out_proj(h, s["o"], s["g"], w_out, j, mod, i, t_mm)
            weights.append((w_in_t, w_out))
        else:
            p_in_t, p_out, p_grp = wts
            s["v"], s["g"] = _pool_in_proj(h, rows, mod, i, p_in_t, j, t_mm)
            if i < DEPTH - 1:
                h, s["br"] = _pool_mix_out(h, s["v"], s["g"], p_grp, p_out, j, rows, mod, i, t_mm)
            else:
                dx, s["br"], loss_part = _pool_mix_out(h, s["v"], s["g"], p_grp, p_out, j, rows, mod, i, t_mm,
                                                       loss_target[0])
            weights.append(wts)
        saved.append(s)

    vecs, gates, gains, dsinks, pool_vecs = [None] * DEPTH, [None] * DEPTH, [None] * 2, [None] * 2, [None] * 2
    sent = {}
    token = jnp.zeros((8, LANES), F32)
    for i in reversed(range(DEPTH)):
        j = i // 2
        s = saved[i]
        if i % 2 == 0:
            w_in_t, w_out = weights[i]
            dos, dg, d_w_out, gates[i] = _attn_out_proj_bwd(dx, s["br"], s["o"], s["g"], w_out, j, mod, i, t_mm, token)
            if i == 0:
                sent["0_out"], token = _scatter_start([d_w_out], (0,), "scatter_start_0_out", token)
            dq, dk, dv, dsinks[j] = _attn_backward(attn_sinks, s["qs"], dos, s["kd"], s["vd"], j, token)
            dx, d_in_t, vecs[i], gains[j] = _attn_in_proj_bwd(
                s["x"], dx, rope, s["qk_raw"], dq, dk, dv, dg, rows, mod, i, w_in_t, j, s["gain"], bd, t_bw)
            if i > 0:
                sent[i], token = _scatter_start([d_in_t, d_w_out], (0, 0), f"scatter_start_{i}", token)
        else:
            p_in_t, p_out, p_grp = weights[i]
            dpool, dg, d_p_out, d_p_grp, pool_vecs[j] = _pool_mix_out_bwd(
                dx, s["br"], s["v"], s["g"], p_grp, p_out, j, rows, mod, i, t_mm, token)
            dx, d_in_t, vecs[i] = _pool_in_proj_bwd(s["x"], dx, dpool, dg, rows, mod, i, p_in_t, j, t_bw, token)
            sent[i], token = _scatter_start([d_in_t, d_p_out, d_p_grp], (0, 0, 1), f"scatter_start_{i}", token)

    vec = _build_vec(vecs, gates, pool_vecs, gains, dsinks, loss_part)
    vec_rows = lax.dynamic_update_slice(jnp.zeros((N_DEV * VEC_ROWS, D), F32), vec, (me * VEC_ROWS, 0))
    vec_sent, token = _gather_start([[vec_rows]], [(0,)], loss_part, "vec_gather_start")
    sent["0_in"], token = _scatter_start([d_in_t], (0,), "scatter_start_0_in", token)

    got = {}
    for i in (3, 1):
        fulls, lands = _scatter_wait(sent[i], (0, 0, 1), token, f"scatter_wait_{i}")
        got[i] = dict(zip(("in", "out", "grp"), zip(fulls, lands)))
    pick = lambda ls, kind: ([got[i][kind][0] for i in ls], [got[i][kind][1] for i in ls])
    res = {}
    res["pool_w_in"] = _adamw_shards("adamw_pool_w_in", me, *pick((1, 3), "in"), pool_w_in, m_pool_w_in, v_pool_w_in, True)
    res["pool_w_out"] = _adamw_shards("adamw_pool_w_out", me, *pick((1, 3), "out"), pool_w_out, m_pool_w_out,
                                      v_pool_w_out, False)
    res["pool_w_group"] = _adamw_shards("adamw_pool_w_group", me, *pick((1, 3), "grp"), pool_w_group, m_pool_w_group,
                                        v_pool_w_group, False, axis=1)

    vec_all, = _gather_wait(vec_sent[0], (0,), res["pool_w_group"][0], "vec_gather_wait")
    vec_all = vec_all.reshape(N_DEV, VEC_ROWS, D)
    tot, folded = _sum_devices(vec_all, token)
    loss = tot[24, 0]
    small = dict(
        ada_b=(ada_b, tot[0:12].reshape(DEPTH, 3 * D), m_ada_b, v_ada_b),
        norm_g=(norm_g, tot[12:16], m_norm_g, v_norm_g),
        q_norm=(attn_q_norm, folded[0:2, :HEAD_DIM], m_attn_q_norm, v_attn_q_norm),
        k_norm=(attn_k_norm, folded[2:4, :HEAD_DIM], m_attn_k_norm, v_attn_k_norm),
        sinks=(attn_sinks, tot[20:22, :N_HEADS], m_attn_sinks, v_attn_sinks),
        pool_scale=(pool_scale, lax.dynamic_slice(tot, (22, me * shard), (2, shard)), m_pool_scale, v_pool_scale),
    )
    res.update({k: (a[1],) + upd for (k, a), upd in zip(small.items(), _adamw_small(list(small.values())))})

    dmod_all = vec_all[:, 0:12, :].reshape(N_DEV, DEPTH, 3 * D)
    dmod_mine = lax.dynamic_slice_in_dim(dmod_all, me * cols, cols, axis=2)
    dmod_mine = jnp.pad(jnp.transpose(dmod_mine, (1, 0, 2)), ((0, 0), (0, N_DEV), (0, 0))) + token[0, 0]
    res["ada_w"] = _ada_backward_adamw(jnp.pad(c_all, ((0, N_DEV), (0, 0))), dmod_mine, ada_w, m_ada_w, v_ada_w)

    fulls, lands = _scatter_wait(sent[2], (0, 0), res["ada_w"][0], "scatter_wait_2")
    got[2] = dict(zip(("in", "out"), zip(fulls, lands)))
    got[0] = {}
    for kind in ("out", "in"):
        fulls, lands = _scatter_wait(sent["0_" + kind], (0,), res["ada_w"][0], "scatter_wait_0_" + kind)
        got[0][kind] = (fulls[0], lands[0])
    res["attn_w_out"] = _adamw_shards("adamw_attn_w_out", me, *pick((0, 2), "out"), attn_w_out, m_attn_w_out,
                                      v_attn_w_out, False)
    res["attn_w_in"] = tuple(jnp.swapaxes(a, 1, 2) for a in _adamw_shards(
        "adamw_attn_w_in", me, *pick((0, 2), "in"), w_in_rows, jnp.swapaxes(m_attn_w_in, 1, 2),
        jnp.swapaxes(v_attn_w_in, 1, 2), False))

    order = ("ada_w", "ada_b", "norm_g", "attn_w_in", "q_norm", "k_norm", "sinks", "attn_w_out", "pool_w_in",
             "pool_w_group", "pool_scale", "pool_w_out")
    return (loss, dx[None], *[res[k][0] for k in order], *[res[k][1] for k in order], *[res[k][2] for k in order],
            *[res[k][3] for k in order])
```

```python
import numpy as np
import jax
import jax.numpy as jnp
from jax import lax
from jax.experimental import pallas as pl
from jax.experimental.pallas import tpu as pltpu

F32 = jnp.float32
BF16 = jnp.bfloat16
MESH = pl.DeviceIdType.MESH

N_DEV = 8
D = 1024
DEPTH = 4
HEAD_DIM = 64
N_HEADS = 16
N_KV = 4
QK_W = 1280
ATTN_IN = 2560
POOL_IN = 2048
QBLK = 128
KX_W = N_KV * 128
CHUNK = 256
POOL_WINDOWS = (2, 4, 8, 16)
HALO = 16
ROPE_THETA = 500000.0
ROT_DIM = 16
NORM_EPS = 1e-6
ADAM_LR = 0.001
ADAM_B1 = 0.9
ADAM_B2 = 0.999
ADAM_EPS = 1e-08
ADAM_WD = 0.01
ADAM_STEP = 10

LANES = 128
VMEM_LIMIT = 56 * 2**20
VEC_ROWS = 32


def _cparams(n_grid=0, **kw):
    if n_grid:
        kw["dimension_semantics"] = ("arbitrary",) * n_grid
    return pltpu.CompilerParams(vmem_limit_bytes=VMEM_LIMIT, **kw)


def _call(body, **kw):
    return pl.pallas_call(body, **kw)


def _mod_spec(layer):
    return pl.BlockSpec((None, 8, D), lambda *_: (layer, 0, 0), pipeline_mode=pl.Buffered(1))


def _mod_row_spec(layer, row):
    return pl.BlockSpec((None, None, 1, D), lambda *_: (layer, row, 0, 0), pipeline_mode=pl.Buffered(1))


NORM_ROW, POOL_SCALE_ROW = 3, 4


def _const_spec(shape):
    nd = len(shape)
    return pl.BlockSpec(shape, lambda *_: (0,) * nd, pipeline_mode=pl.Buffered(1))


def _dot(a, b):
    return jnp.dot(a, b, preferred_element_type=F32)


def _dot_nt(a, b):
    return lax.dot_general(a, b, (((1,), (1,)), ((), ())), preferred_element_type=F32)


def _dot_tn(a, b):
    return lax.dot_general(a, b, (((0,), (0,)), ((), ())), preferred_element_type=F32)


def _group_mean(x, m):
    return _dot(x.astype(BF16), m) * (1.0 / HEAD_DIM)


def _sigmoid(g):
    return 1.0 / (1.0 + jnp.exp(-g))


def _norm_mod(x, ng, sc, sh):
    r = lax.rsqrt(jnp.mean(x * x, axis=-1, keepdims=True) + NORM_EPS)
    xh = x * r
    h = (xh * ng) * (1.0 + sc) + sh
    return xh, r, h


def _rope_table(pos_col, invf_row, tile):
    seq = pos_col.shape[0]

    def body(pos_ref, invf_ref, out_ref):
        ang = pos_ref[...].astype(F32) * invf_ref[...]
        l64 = lax.broadcasted_iota(jnp.int32, (tile, LANES), 1) & (HEAD_DIM - 1)
        cs, sn = jnp.cos(ang), jnp.sin(ang)
        out_ref[:, 0:LANES] = jnp.where(l64 < ROT_DIM, cs, 1.0)
        out_ref[:, LANES:2 * LANES] = jnp.where(l64 < ROT_DIM // 2, -sn, 0.0)
        out_ref[:, 2 * LANES:3 * LANES] = jnp.where((l64 >= ROT_DIM // 2) & (l64 < ROT_DIM), sn, 0.0)

    return _call(
        body, name="rope_table", grid=(seq // tile,),
        out_shape=jax.ShapeDtypeStruct((seq, 3 * LANES), F32),
        in_specs=[pl.BlockSpec((tile, 1), lambda i: (i, 0)), _const_spec((1, LANES))],
        out_specs=pl.BlockSpec((tile, 3 * LANES), lambda i: (i, 0)),
        compiler_params=_cparams(1),
    )(pos_col, invf_row)


def _rope_tabs(rope_ref):
    return rope_ref[:, 0:LANES], rope_ref[:, LANES:2 * LANES], rope_ref[:, 2 * LANES:3 * LANES]


def _rope(y, tabs):
    cos_t, sin_a, sin_b = tabs
    return y * cos_t + pltpu.roll(y, LANES - ROT_DIM // 2, 1) * sin_a + pltpu.roll(y, ROT_DIM // 2, 1) * sin_b


def _rope_bwd(dy, tabs):
    cos_t, sin_a, sin_b = tabs
    return dy * cos_t + pltpu.roll(dy * sin_a, ROT_DIM // 2, 1) + pltpu.roll(dy * sin_b, LANES - ROT_DIM // 2, 1)


def _low_half(rows):
    return lax.broadcasted_iota(jnp.int32, (rows, LANES), 1) < HEAD_DIM


def _adamw(w, g, m, v):
    m = ADAM_B1 * m + (1.0 - ADAM_B1) * g
    v = ADAM_B2 * v + (1.0 - ADAM_B2) * (g * g)
    m_hat = m / (1.0 - ADAM_B1 ** ADAM_STEP)
    v_hat = v / (1.0 - ADAM_B2 ** ADAM_STEP)
    delta = -ADAM_LR * (m_hat / (jnp.sqrt(v_hat) + ADAM_EPS) + ADAM_WD * w)
    return delta, m, v


def _my_position():
    x, y, c = lax.axis_index("x"), lax.axis_index("y"), lax.axis_index("c")
    return x, y, c, 4 * x + 2 * y + c


def _peers(x, y, c):
    out = []
    for k in range(1, N_DEV):
        px = 1 - x if k & 4 else x
        py = 1 - y if k & 2 else y
        pc = 1 - c if k & 1 else c
        out.append(((px, py, pc), 4 * px + 2 * py + pc))
    return out


def _allgather_small(v, name, after):
    rows, cols = v.shape

    def body(v_ref, *refs):
        out_ref, send_sems, recv_sems, local_sem = refs[len(after):]
        x, y, c, me = _my_position()
        local = pltpu.make_async_copy(v_ref, out_ref.at[me], local_sem)
        local.start()
        sends = []
        for k, (peer, _) in enumerate(_peers(x, y, c)):
            cp = pltpu.make_async_remote_copy(v_ref, out_ref.at[me], send_sems.at[k], recv_sems.at[k],
                                              device_id=peer, device_id_type=MESH)
            cp.start()
            sends.append(cp)
        for k, (peer, idx) in enumerate(_peers(x, y, c)):
            pltpu.make_async_remote_copy(v_ref, out_ref.at[idx], send_sems.at[k], recv_sems.at[k],
                                         device_id=peer, device_id_type=MESH).wait_recv()
        for cp in sends:
            cp.wait_send()
        local.wait()

    return _call(
        body, name=name,
        out_shape=jax.ShapeDtypeStruct((N_DEV, rows, cols), F32),
        in_specs=[pl.BlockSpec(memory_space=pltpu.VMEM)] + [pl.BlockSpec(memory_space=pl.ANY)] * len(after),
        out_specs=pl.BlockSpec(memory_space=pltpu.VMEM),
        scratch_shapes=[pltpu.SemaphoreType.DMA((N_DEV - 1,)), pltpu.SemaphoreType.DMA((N_DEV - 1,)),
                        pltpu.SemaphoreType.DMA(())],
        compiler_params=_cparams(),
    )(v, *after)


def _shard_rows(ref, idx, rows, axis):
    sl = [slice(None)] * len(ref.shape)
    sl[axis] = pl.ds(idx * rows, rows)
    return ref.at[tuple(sl)]


def _own_and_peer_rows(ref, me, idx, axis):
    rows = ref.shape[axis] // N_DEV
    return _shard_rows(ref, me, rows, axis), _shard_rows(ref, idx, rows, axis)


HBM_SPEC = pl.BlockSpec(memory_space=pltpu.HBM)
SEM_SPEC = pl.BlockSpec(memory_space=pltpu.SEMAPHORE)
ANY_SPEC = pl.BlockSpec(memory_space=pl.ANY)
DATAFLOW = pltpu.SideEffectType.DATAFLOW_SIDE_EFFECTING


def _hbm(a):
    return pltpu.with_memory_space_constraint(a, pltpu.HBM)


def _gather_start(layers, axes, after, name):
    flat = [a for arrs in layers for a in arrs]
    flat_axes = [ax for axs in axes for ax in axs]
    n, nl = len(flat), len(layers)

    def body(*refs):
        ins, sems, token = refs[:n], refs[n + 1:n + 1 + 2 * nl], refs[-1]
        x, y, c, me = _my_position()
        a0 = 0
        for li, arrs in enumerate(layers):
            for k, (peer, _) in enumerate(_peers(x, y, c)):
                for a in range(len(arrs)):
                    rows, _ = _own_and_peer_rows(ins[a0 + a], me, me, flat_axes[a0 + a])
                    pltpu.make_async_remote_copy(rows, rows, sems[2 * li].at[k * len(arrs) + a],
                                                 sems[2 * li + 1].at[k * len(arrs) + a],
                                                 device_id=peer, device_id_type=MESH).start()
            a0 += len(arrs)
        token[...] = jnp.zeros_like(token)

    sem_shapes = []
    for arrs in layers:
        sem_shapes += [pltpu.SemaphoreType.DMA(((N_DEV - 1) * len(arrs),))] * 2
    out = _call(
        body, name=name,
        out_shape=(*sem_shapes, *[pltpu.HBM(a.shape, a.dtype) for a in flat], jax.ShapeDtypeStruct((8, LANES), F32)),
        in_specs=[HBM_SPEC] * n + [ANY_SPEC],
        out_specs=(*[SEM_SPEC] * (2 * nl), *[HBM_SPEC] * n, pl.BlockSpec(memory_space=pltpu.VMEM)),
        input_output_aliases={a: 2 * nl + a for a in range(n)},
        compiler_params=_cparams(has_side_effects=DATAFLOW),
    )(*[_hbm(a) for a in flat], after)
    per_layer, a0 = [], 0
    for li, arrs in enumerate(layers):
        per_layer.append((out[2 * li], out[2 * li + 1], list(out[2 * nl + a0:2 * nl + a0 + len(arrs)])))
        a0 += len(arrs)
    return per_layer, out[-1]


def _gather_wait(started, axes, after, name):
    send_sems, recv_sems, arrs = started
    n = len(arrs)

    def body(*refs):
        ins, send_ref, recv_ref = refs[:n], refs[n], refs[n + 1]
        x, y, c, me = _my_position()
        for k, (peer, idx) in enumerate(_peers(x, y, c)):
            for a in range(n):
                own, theirs = _own_and_peer_rows(ins[a], me, idx, axes[a])
                cp = pltpu.make_async_remote_copy(own, theirs, send_ref.at[k * n + a], recv_ref.at[k * n + a],
                                                  device_id=peer, device_id_type=MESH)
                cp.wait_send()
                cp.wait_recv()

    return _call(
        body, name=name,
        out_shape=tuple(pltpu.HBM(a.shape, a.dtype) for a in arrs),
        in_specs=[HBM_SPEC] * n + [SEM_SPEC, SEM_SPEC, ANY_SPEC],
        out_specs=tuple([HBM_SPEC] * n),
        input_output_aliases={a: a for a in range(n)},
        compiler_params=_cparams(has_side_effects=DATAFLOW),
    )(*arrs, send_sems, recv_sems, after)


def _first_relations(x, y, c):
    return [(x, y, 1 - c), (1 - x, y, c), (x, 1 - y, c), (1 - x, 1 - y, c)]


def _gather_first_start(arr, after):
    n_rel = 4

    def body(a_ref, after_ref, send_ref, recv_ref, thru, token):
        x, y, c, me = _my_position()
        rows, _ = _own_and_peer_rows(a_ref, me, me, 0)
        for k, peer in enumerate(_first_relations(x, y, c)):
            pltpu.make_async_remote_copy(rows, rows, send_ref.at[k], recv_ref.at[k], device_id=peer, device_id_type=MESH).start()
        token[...] = jnp.zeros_like(token)

    sem = pltpu.SemaphoreType.DMA((n_rel,))
    out = _call(
        body, name="gather_first_start",
        out_shape=(sem, sem, pltpu.HBM(arr.shape, arr.dtype), jax.ShapeDtypeStruct((8, LANES), F32)),
        in_specs=[HBM_SPEC, ANY_SPEC],
        out_specs=(SEM_SPEC, SEM_SPEC, HBM_SPEC, pl.BlockSpec(memory_space=pltpu.VMEM)),
        input_output_aliases={0: 2},
        compiler_params=_cparams(has_side_effects=DATAFLOW),
    )(_hbm(arr), after)
    return out[:3], out[3]


def _gather_first_forward(started, after):
    send_a, recv_a, arr = started

    def body(a_ref, send_a_ref, recv_a_ref, after_ref, send_b_ref, recv_b_ref, thru, token):
        x, y, c, me = _my_position()
        sibling = (x, y, 1 - c)
        for k, peer in enumerate(_first_relations(x, y, c)):
            own, theirs = _own_and_peer_rows(a_ref, me, 4 * peer[0] + 2 * peer[1] + peer[2], 0)
            cp = pltpu.make_async_remote_copy(own, theirs, send_a_ref.at[k], recv_a_ref.at[k], device_id=peer, device_id_type=MESH)
            cp.wait_send()
            cp.wait_recv()
            if k > 0:
                pltpu.make_async_remote_copy(theirs, theirs, send_b_ref.at[k - 1], recv_b_ref.at[k - 1],
                                             device_id=sibling, device_id_type=MESH).start()
        token[...] = jnp.zeros_like(token)

    sem = pltpu.SemaphoreType.DMA((3,))
    out = _call(
        body, name="gather_first_forward",
        out_shape=(sem, sem, pltpu.HBM(arr.shape, arr.dtype), jax.ShapeDtypeStruct((8, LANES), F32)),
        in_specs=[HBM_SPEC, SEM_SPEC, SEM_SPEC, ANY_SPEC],
        out_specs=(SEM_SPEC, SEM_SPEC, HBM_SPEC, pl.BlockSpec(memory_space=pltpu.VMEM)),
        input_output_aliases={0: 2},
        compiler_params=_cparams(has_side_effects=DATAFLOW),
    )(arr, send_a, recv_a, after)
    return out[:3], out[3]


def _gather_first_wait(forwarded, after):
    send_b, recv_b, arr = forwarded

    def body(a_ref, send_b_ref, recv_b_ref, after_ref, thru):
        x, y, c, me = _my_position()
        sibling = (x, y, 1 - c)
        for k, peer in enumerate(_first_relations(x, y, c)[1:]):
            _, sent = _own_and_peer_rows(a_ref, me, 4 * peer[0] + 2 * peer[1] + peer[2], 0)
            _, got = _own_and_peer_rows(a_ref, me, 4 * peer[0] + 2 * peer[1] + (1 - peer[2]), 0)
            cp = pltpu.make_async_remote_copy(sent, got, send_b_ref.at[k], recv_b_ref.at[k], device_id=sibling, device_id_type=MESH)
            cp.wait_send()
            cp.wait_recv()

    return _call(
        body, name="gather_first_wait",
        out_shape=pltpu.HBM(arr.shape, arr.dtype),
        in_specs=[HBM_SPEC, SEM_SPEC, SEM_SPEC, ANY_SPEC],
        out_specs=HBM_SPEC,
        input_output_aliases={0: 0},
        compiler_params=_cparams(has_side_effects=DATAFLOW),
    )(arr, send_b, recv_b, after)


def _scatter_start(fulls, axes, name, after):
    n = len(fulls)
    lands = []
    for f, ax in zip(fulls, axes):
        shp = list(f.shape)
        shp[ax] //= N_DEV
        lands.append(_hbm(lax.empty((N_DEV - 1,) + tuple(shp), f.dtype)))

    def body(*refs):
        srcs, dsts, send_ref, recv_ref, token = refs[:n], refs[n:2 * n], refs[2 * n + 1], refs[2 * n + 2], refs[-1]
        x, y, c, me = _my_position()
        for k, (peer, idx) in enumerate(_peers(x, y, c)):
            for a in range(n):
                _, theirs = _own_and_peer_rows(srcs[a], me, idx, axes[a])
                pltpu.make_async_remote_copy(theirs, dsts[a].at[k], send_ref.at[k * n + a], recv_ref.at[k * n + a],
                                             device_id=peer, device_id_type=MESH).start()
        token[...] = jnp.zeros_like(token)

    sem = pltpu.SemaphoreType.DMA(((N_DEV - 1) * n,))
    out = _call(
        body, name=name,
        out_shape=(sem, sem, *[pltpu.HBM(a.shape, a.dtype) for a in fulls], *[pltpu.HBM(a.shape, a.dtype) for a in lands],
                   jax.ShapeDtypeStruct((8, LANES), F32)),
        in_specs=[HBM_SPEC] * (2 * n) + [ANY_SPEC],
        out_specs=(SEM_SPEC, SEM_SPEC, *[HBM_SPEC] * (2 * n), pl.BlockSpec(memory_space=pltpu.VMEM)),
        input_output_aliases={a: 2 + a for a in range(2 * n)},
        compiler_params=_cparams(has_side_effects=DATAFLOW),
    )(*[_hbm(a) for a in fulls], *lands, after)
    return (out[0], out[1], list(out[2:2 + n]), list(out[2 + n:2 + 2 * n])), out[-1]


def _scatter_wait(started, axes, after, name):
    send_sems, recv_sems, fulls, lands = started
    n = len(fulls)

    def body(*refs):
        srcs, dsts, send_ref, recv_ref = refs[:n], refs[n:2 * n], refs[2 * n], refs[2 * n + 1]
        x, y, c, me = _my_position()
        for k, (peer, idx) in enumerate(_peers(x, y, c)):
            for a in range(n):
                _, theirs = _own_and_peer_rows(srcs[a], me, idx, axes[a])
                cp = pltpu.make_async_remote_copy(theirs, dsts[a].at[k], send_ref.at[k * n + a], recv_ref.at[k * n + a],
                                                  device_id=peer, device_id_type=MESH)
                cp.wait_send()
                cp.wait_recv()

    out = _call(
        body, name=name,
        out_shape=tuple(pltpu.HBM(a.shape, a.dtype) for a in (*fulls, *lands)),
        in_specs=[HBM_SPEC] * (2 * n) + [SEM_SPEC, SEM_SPEC, ANY_SPEC],
        out_specs=tuple([HBM_SPEC] * (2 * n)),
        input_output_aliases={a: a for a in range(2 * n)},
        compiler_params=_cparams(has_side_effects=DATAFLOW),
    )(*fulls, *lands, send_sems, recv_sems, after)
    return list(out[:n]), list(out[n:])


def _prep_weights(me, items, name, after=()):
    def body(me_ref, *refs):
        outs = refs[len(items) + len(after):]
        for (_, _, kind), src, dst in zip(items, refs[:len(items)], outs):
            dst[...] = (src[...].T if kind == "T" else src[...]).astype(BF16)

    ins, in_specs, out_shapes, out_specs = [], [], [], []
    for src, j, kind in items:
        shard = src.shape[1:]
        ins.append(src)
        in_specs.append(pl.BlockSpec((None,) + tuple(shard), lambda i, me_ref, j=j, nd=len(shard): (j,) + (0,) * nd))
        if kind == "G":
            out_shapes.append((shard[0], N_DEV * shard[1], shard[2]))
            out_specs.append(pl.BlockSpec(tuple(shard), lambda i, me_ref: (0, me_ref[0], 0)))
        else:
            rows = shard[1] if kind == "T" else shard[0]
            out_shapes.append((N_DEV * rows, D))
            out_specs.append(pl.BlockSpec((rows, D), lambda i, me_ref: (me_ref[0], 0)))
    out = _call(
        body, name=name,
        grid_spec=pltpu.PrefetchScalarGridSpec(num_scalar_prefetch=1, grid=(1,), in_specs=in_specs + [ANY_SPEC] * len(after),
                                               out_specs=tuple(out_specs)),
        out_shape=tuple(jax.ShapeDtypeStruct(s, BF16) for s in out_shapes),
        compiler_params=_cparams(1),
    )(me.reshape(1), *ins, *after)
    return list(out)


def _ada_forward(c_all, ada_w):
    cols = ada_w.shape[2]

    def body(c_ref, w_ref, o_ref):
        cv = c_ref[...]
        sc = (cv * _sigmoid(cv)).astype(BF16)
        o_ref[...] = _dot(sc, w_ref[...].astype(BF16))

    return _call(
        body, name="ada_forward", grid=(DEPTH,),
        out_shape=jax.ShapeDtypeStruct((DEPTH, N_DEV, cols), F32),
        in_specs=[pl.BlockSpec((N_DEV, D), lambda i: (0, 0)), pl.BlockSpec((None, D, cols), lambda i: (i, 0, 0))],
        out_specs=pl.BlockSpec((None, N_DEV, cols), lambda i: (i, 0, 0)),
        compiler_params=_cparams(1),
    )(c_all, ada_w)


def _ada_backward_adamw(c_pad, dmod_pad, w, m, v):
    cols = w.shape[2]

    def body(c_ref, dm_ref, w_ref, m_ref, v_ref, g_out, d_out, m_out, v_out):
        cv = c_ref[...]
        sc = (cv * _sigmoid(cv)).astype(BF16)
        g = _dot_tn(sc, dm_ref[...].astype(BF16))
        g_out[...] = g
        d_out[...], m_out[...], v_out[...] = _adamw(w_ref[...], g, m_ref[...], v_ref[...])

    wspec = pl.BlockSpec((None, D, cols), lambda i: (i, 0, 0))
    return _call(
        body, name="ada_backward_adamw", grid=(DEPTH,),
        out_shape=tuple(jax.ShapeDtypeStruct(w.shape, F32) for _ in range(4)),
        in_specs=[pl.BlockSpec((2 * N_DEV, D), lambda i: (0, 0)), pl.BlockSpec((None, 2 * N_DEV, cols), lambda i: (i, 0, 0)),
                  wspec, wspec, wspec],
        out_specs=(wspec, wspec, wspec, wspec),
        compiler_params=_cparams(1),
    )(c_pad, dmod_pad, w, m, v)


def _attn_in_proj(x, rope, rows, mod, layer, w_t, j, gain, bd, tile):
    seq = x.shape[0]

    def body(x_ref, rope_ref, ng_ref, mod_ref, w_ref, gain_ref, bd_ref, qk_ref, qs_ref, kd_ref, vd_ref, g_ref):
        _, _, h = _norm_mod(x_ref[...], ng_ref[...], mod_ref[1:2, :], mod_ref[0:1, :])
        hb = h.astype(BF16)
        tabs = _rope_tabs(rope_ref)
        low = _low_half(tile)
        bdm = bd_ref[...]

        def put_kv(ref, blk, first_kv):
            sw = pltpu.roll(blk, HEAD_DIM, 1)
            ref[:, LANES * first_kv:LANES * (first_kv + 1)] = jnp.where(low, blk, sw).astype(BF16)
            ref[:, LANES * (first_kv + 1):LANES * (first_kv + 2)] = jnp.where(low, sw, blk).astype(BF16)

        def project(c):
            return _dot_nt(hb, w_ref[CHUNK * c:CHUNK * (c + 1), :])

        n_chunks = ATTN_IN // CHUNK
        per = CHUNK // LANES
        nxt = project(0)
        for c in range(n_chunks):
            cur = nxt
            if c + 1 < n_chunks:
                nxt = project(c + 1)
            col = CHUNK * c
            if col >= QK_W + N_KV * HEAD_DIM:
                g_ref[:, col - QK_W - N_KV * HEAD_DIM:col - QK_W - N_KV * HEAD_DIM + CHUNK] = cur.astype(BF16)
            elif col >= QK_W:
                for t in range(per):
                    put_kv(vd_ref, cur[:, LANES * t:LANES * (t + 1)], (col - QK_W) // HEAD_DIM + 2 * t)
            else:
                qk_ref[:, col:col + CHUNK] = cur
                for t in range(per):
                    b = per * c + t
                    blk = cur[:, LANES * t:LANES * (t + 1)]
                    ms = _group_mean(blk * blk, bdm)
                    y = (blk * lax.rsqrt(ms + NORM_EPS)) * gain_ref[:, LANES * b:LANES * (b + 1)]
                    rp = _rope(y, tabs)
                    if b < D // LANES:
                        rp = rp * (HEAD_DIM ** -0.5)
                        qs_ref[:, 2 * LANES * b:2 * LANES * b + LANES] = jnp.where(low, rp, 0.0).astype(BF16)
                        qs_ref[:, 2 * LANES * b + LANES:2 * LANES * (b + 1)] = jnp.where(low, 0.0, rp).astype(BF16)
                    else:
                        put_kv(kd_ref, rp, 2 * (b - D // LANES))

    row = lambda w: pl.BlockSpec((tile, w), lambda i: (i, 0))
    return _call(
        body, name=f"attn_in_proj_{j}", grid=(seq // tile,),
        out_shape=(jax.ShapeDtypeStruct((seq, QK_W), F32), jax.ShapeDtypeStruct((seq, N_HEADS * LANES), BF16),
                   jax.ShapeDtypeStruct((seq, KX_W), BF16), jax.ShapeDtypeStruct((seq, KX_W), BF16),
                   jax.ShapeDtypeStruct((seq, D), BF16)),
        in_specs=[row(D), row(3 * LANES), _mod_row_spec(layer, NORM_ROW), _mod_spec(layer), _const_spec((ATTN_IN, D)),
                  _const_spec((1, QK_W)), _const_spec((LANES, LANES))],
        out_specs=(row(QK_W), row(N_HEADS * LANES), row(KX_W), row(KX_W), row(D)),
        compiler_params=_cparams(1),
    )(x, rope, rows, mod, w_t, gain, bd)


def _band_mask(n, rows, keys_on_rows):
    shape = (2 * QBLK, rows) if keys_on_rows else (rows, 2 * QBLK)
    qi = lax.broadcasted_iota(jnp.int32, shape, 1 if keys_on_rows else 0) & (QBLK - 1)
    kj = lax.broadcasted_iota(jnp.int32, shape, 0 if keys_on_rows else 1)
    diff = QBLK + qi - kj
    first_key = jnp.where(n > 0, 0, QBLK)
    return (diff >= 0) & (diff < QBLK) & (kj >= first_key)


def _pair_up(st, low):
    return jnp.concatenate([jnp.where(low, st[0:QBLK], st[QBLK:2 * QBLK]),
                            jnp.where(low, st[2 * QBLK:3 * QBLK], st[3 * QBLK:4 * QBLK])], axis=1)


def _attn_forward(sinks, qs, kd, vd, j):
    seq = qs.shape[0]
    per = 2
    nb = seq // (per * QBLK)

    def body(sink_ref, q_ref, kp_ref, kc_ref, vp_ref, vc_ref, o_ref):
        n = pl.program_id(0)
        low = _low_half(QBLK)
        rowi = lax.broadcasted_iota(jnp.int32, (4 * QBLK, 1), 0)
        groups = per * N_KV

        def keys(p_ref, c_ref, blk, kv):
            cols = slice(LANES * kv, LANES * (kv + 1))
            if blk == 0:
                return jnp.concatenate([p_ref[:, cols], c_ref[0:QBLK, cols]], axis=0)
            return c_ref[QBLK * (blk - 1):QBLK * (blk + 1), cols]

        def scores(g):
            blk, kv = divmod(g, N_KV)
            q = jnp.concatenate([q_ref[QBLK * blk:QBLK * (blk + 1), LANES * h:LANES * (h + 1)]
                                 for h in range(4 * kv, 4 * kv + 4)], axis=0)
            return _dot_nt(q, keys(kp_ref, kc_ref, blk, kv))

        nxt = scores(0)
        for g in range(groups):
            blk, kv = divmod(g, N_KV)
            ok = _band_mask(n if blk == 0 else 1, 4 * QBLK, False)
            s = jnp.where(ok, nxt, -1e30)
            if g + 1 < groups:
                nxt = scores(g + 1)
            sink = jnp.where(rowi < QBLK, sink_ref[j, 4 * kv],
                             jnp.where(rowi < 2 * QBLK, sink_ref[j, 4 * kv + 1],
                                       jnp.where(rowi < 3 * QBLK, sink_ref[j, 4 * kv + 2], sink_ref[j, 4 * kv + 3])))
            m = jnp.maximum(jnp.max(s, axis=1, keepdims=True), sink)
            p = jnp.exp(s - m)
            den = jnp.sum(p, axis=1, keepdims=True) + jnp.exp(sink - m)
            o_st = _dot((p / den).astype(BF16), keys(vp_ref, vc_ref, blk, kv))
            o_ref[QBLK * blk:QBLK * (blk + 1), 2 * LANES * kv:2 * LANES * (kv + 1)] = _pair_up(o_st, low).astype(BF16)

    cur = lambda w: pl.BlockSpec((per * QBLK, w), lambda n: (n, 0))
    prev = lambda w: pl.BlockSpec((QBLK, w), lambda n: (jnp.maximum(per * n - 1, 0), 0))
    return _call(
        body, name=f"attn_forward_{j}", grid=(nb,),
        out_shape=jax.ShapeDtypeStruct((seq, D), BF16),
        in_specs=[pl.BlockSpec(memory_space=pltpu.SMEM), cur(N_HEADS * LANES), prev(KX_W), cur(KX_W), prev(KX_W), cur(KX_W)],
        out_specs=cur(D),
        compiler_params=_cparams(1),
    )(sinks, qs, kd, kd, vd, vd)


def _attn_out_proj(x, o, g, w, j, mod, layer, tile):
    seq = x.shape[0]

    def body(x_ref, o_ref, g_ref, w_ref, mod_ref, xo_ref, br_ref):
        gv = g_ref[...].astype(F32)
        u = (o_ref[...].astype(F32) * (gv * _sigmoid(gv))).astype(BF16)
        br = _dot(u, w_ref[...])
        br_ref[...] = br.astype(BF16)
        xo_ref[...] = x_ref[...] + mod_ref[2:3, :] * br

    row = pl.BlockSpec((tile, D), lambda i: (i, 0))
    return _call(
        body, name=f"attn_out_proj_{j}", grid=(seq // tile,),
        out_shape=(jax.ShapeDtypeStruct((seq, D), F32), jax.ShapeDtypeStruct((seq, D), BF16)),
        in_specs=[row, row, row, _const_spec((D, D)), _mod_spec(layer)],
        out_specs=(row, row),
        compiler_params=_cparams(1),
    )(x, o, g, w, mod)


def _attn_out_proj_bwd(dxn, br, o, g, w, j, mod, layer, tile, after):
    seq = dxn.shape[0]
    steps = seq // tile

    def body(dxn_ref, br_ref, o_ref, g_ref, w_ref, mod_ref, after_ref, do_ref, dg_ref, dw_ref, dgate_ref, dw_acc):
        i = pl.program_id(0)

        @pl.when(i == 0)
        def _():
            dw_acc[...] = jnp.zeros_like(dw_acc)
            dgate_ref[...] = jnp.zeros_like(dgate_ref)

        dxn_v, ov, gv = dxn_ref[...], o_ref[...].astype(F32), g_ref[...].astype(F32)
        dgate_ref[...] += jnp.sum(dxn_v * br_ref[...].astype(F32), axis=0, keepdims=True)
        dbr = (dxn_v * mod_ref[2:3, :]).astype(BF16)
        du = _dot_nt(dbr, w_ref[...])
        sg = _sigmoid(gv)
        sl = gv * sg
        dw_acc[...] += _dot_tn((ov * sl).astype(BF16), dbr)
        do = du * sl
        dg_ref[...] = (du * ov * (sg * (1.0 + gv * (1.0 - sg)))).astype(BF16)
        low = _low_half(tile)
        for b in range(D // LANES):
            blk = do[:, LANES * b:LANES * (b + 1)]
            do_ref[:, 2 * LANES * b:2 * LANES * b + LANES] = jnp.where(low, blk, 0.0).astype(BF16)
            do_ref[:, 2 * LANES * b + LANES:2 * LANES * (b + 1)] = jnp.where(low, 0.0, blk).astype(BF16)

        @pl.when(i == steps - 1)
        def _():
            dw_ref[...] = dw_acc[...].astype(BF16)

    row = lambda w_: pl.BlockSpec((tile, w_), lambda i: (i, 0))
    return _call(
        body, name=f"attn_out_proj_bwd_{j}", grid=(steps,),
        out_shape=(jax.ShapeDtypeStruct((seq, N_HEADS * LANES), BF16), jax.ShapeDtypeStruct((seq, D), BF16),
                   jax.ShapeDtypeStruct((D, D), BF16), jax.ShapeDtypeStruct((1, D), F32)),
        in_specs=[row(D), row(D), row(D), row(D), _const_spec((D, D)), _mod_spec(layer), ANY_SPEC],
        out_specs=(row(N_HEADS * LANES), row(D), pl.BlockSpec((D, D), lambda i: (0, 0)),
                   pl.BlockSpec((1, D), lambda i: (0, 0))),
        scratch_shapes=[pltpu.VMEM((D, D), F32)],
        compiler_params=_cparams(1),
    )(dxn, br, o, g, w, mod, after)


def _attn_backward(sinks, qs, dos, kd, vd, j, after):
    seq = qs.shape[0]
    per = 2
    nb = seq // (per * QBLK)
    kw = N_KV * HEAD_DIM

    def body(sink_ref, q_ref, do_ref, kp_ref, kc_ref, vp_ref, vc_ref, after_ref,
             dq_ref, dk_even, dk_odd, dv_even, dv_odd, dsink_ref, carry_k, carry_v, sink_acc):
        n = pl.program_id(0)

        @pl.when(n == 0)
        def _():
            carry_k[...] = jnp.zeros_like(carry_k)
            carry_v[...] = jnp.zeros_like(carry_v)
            sink_acc[...] = jnp.zeros_like(sink_acc)

        @pl.when(n < nb)
        def _():
            low = _low_half(QBLK)
            lane_q = lax.broadcasted_iota(jnp.int32, (1, 2 * QBLK), 1)
            groups = per * 2 * N_KV

            def keys(p_ref, c_ref, blk, kv):
                cols = slice(LANES * kv, LANES * (kv + 1))
                if blk == 0:
                    return jnp.concatenate([p_ref[:, cols], c_ref[0:QBLK, cols]], axis=0)
                return c_ref[QBLK * (blk - 1):QBLK * (blk + 1), cols]

            def first_products(g):
                blk, rest = divmod(g, 2 * N_KV)
                kv, half = divmod(rest, 2)
                heads = (4 * kv + half, 4 * kv + 2 + half)
                rows = slice(QBLK * blk, QBLK * (blk + 1))
                q = jnp.concatenate([q_ref[rows, LANES * h:LANES * (h + 1)] for h in heads], axis=0)
                do = jnp.concatenate([do_ref[rows, LANES * h:LANES * (h + 1)] for h in heads], axis=0)
                kk = keys(kp_ref, kc_ref, blk, kv)
                return heads, q, do, kk, _dot_nt(kk, q), _dot_nt(keys(vp_ref, vc_ref, blk, kv), do)

            nxt = first_products(0)
            dk_parts, dv_parts = [[], []], [[], []]
            dq_h, dk_kv, dv_kv = [], None, None
            for g in range(groups):
                blk, rest = divmod(g, 2 * N_KV)
                heads, q, do, kk, s_raw, dp_raw = nxt
                if g + 1 < groups:
                    nxt = first_products(g + 1)
                ok = _band_mask(n if blk == 0 else 1, 2 * QBLK, True)
                st = jnp.where(ok, s_raw, -1e30)
                sink = jnp.where(lane_q < QBLK, sink_ref[j, heads[0]], sink_ref[j, heads[1]])
                m = jnp.maximum(jnp.max(st, axis=0, keepdims=True), sink)
                e = jnp.exp(st - m)
                e_sink = jnp.exp(sink - m)
                inv = 1.0 / (jnp.sum(e, axis=0, keepdims=True) + e_sink)
                p = e * inv
                pdp = p * dp_raw
                delta = jnp.sum(pdp, axis=0, keepdims=True)
                ds = (pdp - p * delta).astype(BF16)
                sink_acc[rest:rest + 1, :] -= e_sink * inv * delta
                dk_g, dv_g = _dot(ds, q), _dot(p.astype(BF16), do)
                dk_kv = dk_g if dk_kv is None else dk_kv + dk_g
                dv_kv = dv_g if dv_kv is None else dv_kv + dv_g
                dq_h.append(_dot_tn(ds, kk))
                if g % 2 == 1:
                    kv = rest // 2
                    for t in range(2):
                        dq_ref[QBLK * blk:QBLK * (blk + 1), LANES * (2 * kv + t):LANES * (2 * kv + t + 1)] = jnp.where(
                            low, dq_h[0][QBLK * t:QBLK * (t + 1)], dq_h[1][QBLK * t:QBLK * (t + 1)])
                    dk_parts[blk].append(dk_kv + pltpu.roll(dk_kv, HEAD_DIM, 1))
                    dv_parts[blk].append(dv_kv + pltpu.roll(dv_kv, HEAD_DIM, 1))
                    dq_h, dk_kv, dv_kv = [], None, None

            def order(parts, lo, hi):
                return jnp.concatenate([jnp.where(low, parts[0][lo:hi], parts[1][lo:hi]),
                                        jnp.where(low, parts[2][lo:hi], parts[3][lo:hi])], axis=1)

            dk_odd[...] = carry_k[...] + order(dk_parts[0], 0, QBLK)
            dv_odd[...] = (carry_v[...] + order(dv_parts[0], 0, QBLK)).astype(BF16)
            dk_even[...] = order(dk_parts[0], QBLK, 2 * QBLK) + order(dk_parts[1], 0, QBLK)
            dv_even[...] = (order(dv_parts[0], QBLK, 2 * QBLK) + order(dv_parts[1], 0, QBLK)).astype(BF16)
            carry_k[...] = order(dk_parts[1], QBLK, 2 * QBLK)
            carry_v[...] = order(dv_parts[1], QBLK, 2 * QBLK)

        @pl.when(n == nb)
        def _():
            dk_odd[...] = carry_k[...]
            dv_odd[...] = carry_v[...].astype(BF16)
            lane = lax.broadcasted_iota(jnp.int32, (1, LANES), 1)
            out = jnp.zeros((1, LANES), F32)
            for g in range(2 * N_KV):
                for t in range(2):
                    tot = jnp.sum(sink_acc[g:g + 1, QBLK * t:QBLK * (t + 1)], axis=1, keepdims=True)
                    out = jnp.where(lane == 4 * (g // 2) + 2 * t + g % 2, tot, out)
            dsink_ref[...] = out

    cur = lambda w: pl.BlockSpec((per * QBLK, w), lambda n: (jnp.minimum(n, nb - 1), 0))
    prev = lambda w: pl.BlockSpec((QBLK, w), lambda n: (jnp.maximum(per * n - 1, 0), 0))
    even = pl.BlockSpec((None, QBLK, kw), lambda n: (jnp.minimum(n, nb - 1), 0, 0))
    odd = pl.BlockSpec((None, QBLK, kw), lambda n: (jnp.maximum(n - 1, 0), 0, 0))
    halves = lambda dt: jax.ShapeDtypeStruct((nb, QBLK, kw), dt)
    dq, dk_e, dk_o, dv_e, dv_o, dsink = _call(
        body, name=f"attn_backward_{j}", grid=(nb + 1,),
        out_shape=(jax.ShapeDtypeStruct((seq, D), F32), halves(F32), halves(F32), halves(BF16), halves(BF16),
                   jax.ShapeDtypeStruct((1, LANES), F32)),
        in_specs=[pl.BlockSpec(memory_space=pltpu.SMEM), cur(N_HEADS * LANES), cur(N_HEADS * LANES), prev(KX_W), cur(KX_W),
                  prev(KX_W), cur(KX_W), ANY_SPEC],
        out_specs=(cur(D), even, odd, even, odd, pl.BlockSpec((1, LANES), lambda n: (0, 0))),
        scratch_shapes=[pltpu.VMEM((QBLK, kw), F32), pltpu.VMEM((QBLK, kw), F32), pltpu.VMEM((2 * N_KV, 2 * QBLK), F32)],
        compiler_params=_cparams(1),
    )(sinks, qs, dos, kd, kd, vd, vd, after)
    return dq, (dk_e, dk_o), (dv_e, dv_o), dsink


def _in_proj_tail(x_ref, dxn_ref, ng_ref, mod_ref, w_ref, dproj, dx_ref, dw_acc, vec_acc):
    ng, sc, sh = ng_ref[...], mod_ref[1:2, :], mod_ref[0:1, :]
    xh, r, h = _norm_mod(x_ref[...], ng, sc, sh)
    dh = _dot(dproj, w_ref[...])
    dw_acc[...] += _dot_tn(dproj, h.astype(BF16))
    vec_acc[0:1, :] += jnp.sum(dh, axis=0, keepdims=True)
    vec_acc[1:2, :] += jnp.sum(dh * xh, axis=0, keepdims=True)
    dxh = dh * (ng * (1.0 + sc))
    dx_ref[...] = dxn_ref[...] + r * (dxh - xh * jnp.mean(dxh * xh, axis=-1, keepdims=True))


def _tail_finish(ng_ref, mod_ref, dw_ref, vec_ref, dw_acc, vec_acc):
    dw_ref[...] = dw_acc[...].astype(BF16)
    a = vec_acc[1:2, :]
    vec_ref[...] = jnp.zeros_like(vec_ref)
    vec_ref[0:1, :] = vec_acc[0:1, :]
    vec_ref[1:2, :] = a * ng_ref[...]
    vec_ref[3:4, :] = a * (1.0 + mod_ref[1:2, :])


def _attn_in_proj_bwd(x, dxn, rope, qk_raw, dq, dk, dv, dg, rows, mod, layer, w_t, j, gain, bd, tile):
    seq = x.shape[0]
    steps = seq // tile

    assert tile == 2 * QBLK
    (dk_e, dk_o), (dv_e, dv_o) = dk, dv

    def body(x_ref, dxn_ref, rope_ref, qk_ref, dq_ref, dke_ref, dko_ref, dve_ref, dvo_ref, dg_ref, ng_ref, mod_ref, w_ref, gain_ref,
             bd_ref, dx_ref, dw_ref, vec_ref, dgain_ref, dproj, dw_acc, vec_acc):
        i = pl.program_id(0)

        @pl.when(i == 0)
        def _():
            dw_acc[...] = jnp.zeros_like(dw_acc)
            vec_acc[...] = jnp.zeros_like(vec_acc)
            dgain_ref[...] = jnp.zeros_like(dgain_ref)

        tabs = _rope_tabs(rope_ref)
        bdm = bd_ref[...]
        for b in range(QK_W // LANES):
            cols = slice(LANES * b, LANES * (b + 1))
            raw = qk_ref[:, cols]
            if b < D // LANES:
                dy = dq_ref[:, cols] * (HEAD_DIM ** -0.5)
            else:
                kcols = slice(LANES * (b - D // LANES), LANES * (b + 1 - D // LANES))
                dy = jnp.concatenate([dke_ref[:, kcols], dko_ref[:, kcols]], axis=0)
            dy = _rope_bwd(dy, tabs)
            rr = lax.rsqrt(_group_mean(raw * raw, bdm) + NORM_EPS)
            xh = raw * rr
            dgain_ref[:, cols] += jnp.sum(dy * xh, axis=0, keepdims=True)
            dxh = dy * gain_ref[:, cols]
            dproj[:, cols] = (rr * (dxh - xh * _group_mean(dxh * xh, bdm))).astype(BF16)
        dproj[0:QBLK, QK_W:QK_W + N_KV * HEAD_DIM] = dve_ref[...]
        dproj[QBLK:2 * QBLK, QK_W:QK_W + N_KV * HEAD_DIM] = dvo_ref[...]
        dproj[:, QK_W + N_KV * HEAD_DIM:] = dg_ref[...]
        _in_proj_tail(x_ref, dxn_ref, ng_ref, mod_ref, w_ref, dproj[...], dx_ref, dw_acc, vec_acc)

        @pl.when(i == steps - 1)
        def _():
            _tail_finish(ng_ref, mod_ref, dw_ref, vec_ref, dw_acc, vec_acc)

    row = lambda w: pl.BlockSpec((tile, w), lambda i: (i, 0))
    kblock = pl.BlockSpec((None, QBLK, N_KV * HEAD_DIM), lambda i: (i, 0, 0))
    fixed = lambda shape: pl.BlockSpec(shape, lambda i: (0,) * len(shape))
    return _call(
        body, name=f"attn_in_proj_bwd_{j}", grid=(steps,),
        out_shape=(jax.ShapeDtypeStruct((seq, D), F32), jax.ShapeDtypeStruct((ATTN_IN, D), BF16),
                   jax.ShapeDtypeStruct((8, D), F32), jax.ShapeDtypeStruct((1, QK_W), F32)),
        in_specs=[row(D), row(D), row(3 * LANES), row(QK_W), row(D), kblock, kblock, kblock, kblock, row(D),
                  _mod_row_spec(layer, NORM_ROW), _mod_spec(layer), _const_spec((ATTN_IN, D)), _const_spec((1, QK_W)),
                  _const_spec((LANES, LANES))],
        out_specs=(row(D), fixed((ATTN_IN, D)), fixed((8, D)), fixed((1, QK_W))),
        scratch_shapes=[pltpu.VMEM((tile, ATTN_IN), BF16), pltpu.VMEM((ATTN_IN, D), F32), pltpu.VMEM((8, D), F32)],
        compiler_params=_cparams(1),
    )(x, dxn, rope, qk_raw, dq, dk_e, dk_o, dv_e, dv_o, dg, rows, mod, w_t, gain, bd)


def _pool_in_proj(x, rows, mod, layer, w_t, j, tile):
    seq = x.shape[0]

    def body(x_ref, ng_ref, mod_ref, w_ref, v_ref, g_ref):
        _, _, h = _norm_mod(x_ref[...], ng_ref[...], mod_ref[1:2, :], mod_ref[0:1, :])
        proj = _dot_nt(h.astype(BF16), w_ref[...])
        v_ref[...] = proj[:, :D].astype(BF16)
        g_ref[...] = proj[:, D:].astype(BF16)

    row = pl.BlockSpec((tile, D), lambda i: (i, 0))
    return _call(
        body, name=f"pool_in_proj_{j}", grid=(seq // tile,),
        out_shape=(jax.ShapeDtypeStruct((seq, D), BF16), jax.ShapeDtypeStruct((seq, D), BF16)),
        in_specs=[row, _mod_row_spec(layer, NORM_ROW), _mod_spec(layer), _const_spec((POOL_IN, D))],
        out_specs=(row, row),
        compiler_params=_cparams(1),
    )(x, rows, mod, w_t)


PAD = 8


def _window_sums(ext, lo, hi, forward):
    gw = D // len(POOL_WINDOWS)
    planes = []
    for gi, w in enumerate(POOL_WINDOWS):
        cols = slice(gw * gi, gw * (gi + 1))
        src, k = 0, 1
        while k < w:
            d = k if forward else -k
            ext[1 - src, lo:hi, cols] = ext[src, lo:hi, cols] + ext[src, lo + d:hi + d, cols]
            src, k = 1 - src, 2 * k
        planes.append(src)
    return planes


def _pooled(ext, v_ref, first, tile):
    t_abs = first + lax.broadcasted_iota(jnp.int32, (tile, 1), 0)
    top = PAD + HALO
    planes = _window_sums(ext, PAD, top + tile, False)
    outs = []
    gw = D // len(POOL_WINDOWS)
    for gi, w in enumerate(POOL_WINDOWS):
        cols = slice(gw * gi, gw * (gi + 1))
        cnt = jnp.minimum(t_abs + 1, w).astype(F32)
        outs.append(ext[planes[gi], top:top + tile, cols] / cnt - v_ref[:, cols].astype(F32))
    return jnp.concatenate(outs, axis=1)


def _fill_ext(ext, halo_ref, v_ref, i, tile):
    ext[0, 0:PAD, :] = jnp.zeros((PAD, D), F32)
    ext[1, 0:PAD, :] = jnp.zeros((PAD, D), F32)
    ext[0, PAD:PAD + HALO, :] = jnp.where(i == 0, 0.0, halo_ref[...].astype(F32))
    ext[0, PAD + HALO:PAD + HALO + tile, :] = v_ref[...].astype(F32)


def _group_mix(pb, wg_ref):
    gw = D // len(POOL_WINDOWS)
    return jnp.concatenate([_dot(pb[:, gw * gi:gw * (gi + 1)], wg_ref[gi]) for gi in range(len(POOL_WINDOWS))], axis=1)


def _pool_mix_out(x, v, g, wg, w_out, j, rows, mod, layer, tile, target=None):
    seq = x.shape[0]

    def body(*refs):
        if target is None:
            x_ref, v_ref, halo_ref, g_ref, wg_ref, w_ref, scale_ref, mod_ref, xo_ref, br_ref, ext = refs
        else:
            x_ref, v_ref, halo_ref, g_ref, wg_ref, w_ref, scale_ref, mod_ref, t_ref, xo_ref, br_ref, loss_ref, ext = refs
        i = pl.program_id(0)
        _fill_ext(ext, halo_ref, v_ref, i, tile)
        pb = _pooled(ext, v_ref, i * tile, tile).astype(BF16)
        ms = _group_mix(pb, wg_ref) * scale_ref[...]
        gv = g_ref[...].astype(F32)
        u = (ms * (gv * _sigmoid(gv))).astype(BF16)
        br = _dot(u, w_ref[...])
        br_ref[...] = br.astype(BF16)
        y = x_ref[...] + mod_ref[2:3, :] * br
        if target is None:
            xo_ref[...] = y
        else:
            @pl.when(i == 0)
            def _():
                loss_ref[...] = jnp.zeros_like(loss_ref)

            e = y - t_ref[...]
            xo_ref[...] = e * (1.0 / D)
            loss_ref[...] += 0.5 * jnp.sum(jnp.mean(e * e, axis=-1, keepdims=True), axis=0, keepdims=True)

    row = pl.BlockSpec((tile, D), lambda i: (i, 0))
    halo = pl.BlockSpec((HALO, D), lambda i: (jnp.maximum(i * (tile // HALO) - 1, 0), 0))
    extra_in, extra_out, extra_shape = ([], (), ()) if target is None else (
        [row], (pl.BlockSpec((1, LANES), lambda i: (0, 0)),), (jax.ShapeDtypeStruct((1, LANES), F32),))
    return _call(
        body, name=f"pool_mix_out_{j}", grid=(seq // tile,),
        out_shape=(jax.ShapeDtypeStruct((seq, D), F32), jax.ShapeDtypeStruct((seq, D), BF16)) + extra_shape,
        in_specs=[row, row, halo, row, _const_spec(wg.shape), _const_spec((D, D)), _mod_row_spec(layer, POOL_SCALE_ROW),
                  _mod_spec(layer)] + extra_in,
        out_specs=(row, row) + extra_out,
        scratch_shapes=[pltpu.VMEM((2, tile + HALO + PAD, D), F32)],
        compiler_params=_cparams(1),
    )(x, v, v, g, wg, w_out, rows, mod, *(() if target is None else (target,)))


def _pool_mix_out_bwd(dxn, br, v, g, wg, w_out, j, rows, mod, layer, tile, after):
    seq = dxn.shape[0]
    steps = seq // tile
    ng_ = len(POOL_WINDOWS)
    gw = D // ng_

    def body(dxn_ref, br_ref, v_ref, halo_ref, g_ref, wg_ref, w_ref, scale_ref, mod_ref, after_ref,
             dpool_ref, dg_ref, dw_ref, dwg_ref, vec_ref, ext, dw_acc, dwg_acc):
        i = pl.program_id(0)

        @pl.when(i == 0)
        def _():
            dw_acc[...] = jnp.zeros_like(dw_acc)
            dwg_acc[...] = jnp.zeros_like(dwg_acc)
            vec_ref[...] = jnp.zeros_like(vec_ref)

        _fill_ext(ext, halo_ref, v_ref, i, tile)
        pb = _pooled(ext, v_ref, i * tile, tile).astype(BF16)
        mixed = _group_mix(pb, wg_ref)
        scale = scale_ref[...]
        ms = mixed * scale
        gv, dxn_v = g_ref[...].astype(F32), dxn_ref[...]
        sg = _sigmoid(gv)
        sl = gv * sg
        vec_ref[0:1, :] += jnp.sum(dxn_v * br_ref[...].astype(F32), axis=0, keepdims=True)
        dbr = (dxn_v * mod_ref[2:3, :]).astype(BF16)
        du = _dot_nt(dbr, w_ref[...])
        dw_acc[...] += _dot_tn((ms * sl).astype(BF16), dbr)
        dms = du * sl
        dg_ref[...] = (du * ms * (sg * (1.0 + gv * (1.0 - sg)))).astype(BF16)
        vec_ref[1:2, :] += jnp.sum(dms * mixed, axis=0, keepdims=True)
        dmx = (dms * scale).astype(BF16)
        for gi in range(ng_):
            cols = slice(gw * gi, gw * (gi + 1))
            dpool_ref[:, cols] = _dot_nt(dmx[:, cols], wg_ref[gi])
            dwg_acc[gi] += _dot_tn(pb[:, cols], dmx[:, cols])

        @pl.when(i == steps - 1)
        def _():
            dw_ref[...] = dw_acc[...].astype(BF16)
            dwg_ref[...] = dwg_acc[...].astype(BF16)

    row = pl.BlockSpec((tile, D), lambda i: (i, 0))
    halo = pl.BlockSpec((HALO, D), lambda i: (jnp.maximum(i * (tile // HALO) - 1, 0), 0))
    fixed = lambda shape: pl.BlockSpec(shape, lambda i: (0,) * len(shape))
    return _call(
        body, name=f"pool_mix_out_bwd_{j}", grid=(steps,),
        out_shape=(jax.ShapeDtypeStruct((seq, D), F32), jax.ShapeDtypeStruct((seq, D), BF16),
                   jax.ShapeDtypeStruct((D, D), BF16), jax.ShapeDtypeStruct((ng_, gw, gw), BF16),
                   jax.ShapeDtypeStruct((8, D), F32)),
        in_specs=[row, row, row, halo, row, _const_spec(wg.shape), _const_spec((D, D)), _mod_row_spec(layer, POOL_SCALE_ROW),
                  _mod_spec(layer), ANY_SPEC],
        out_specs=(row, row, fixed((D, D)), fixed((ng_, gw, gw)), fixed((8, D))),
        scratch_shapes=[pltpu.VMEM((2, tile + HALO + PAD, D), F32), pltpu.VMEM((D, D), F32), pltpu.VMEM((ng_, gw, gw), F32)],
        compiler_params=_cparams(1),
    )(dxn, br, v, v, g, wg, w_out, rows, mod, after)


def _pool_in_proj_bwd(x, dxn, dpool, dg, rows, mod, layer, w_t, j, tile, after):
    seq = x.shape[0]
    steps = seq // tile
    gw = D // len(POOL_WINDOWS)

    def body(x_ref, dxn_ref, dp_ref, halo_ref, dg_ref, ng_ref, mod_ref, w_ref, after_ref, dx_ref, dw_ref, vec_ref,
             ext, dproj, dw_acc, vec_acc):
        i = pl.program_id(0)

        @pl.when(i == 0)
        def _():
            dw_acc[...] = jnp.zeros_like(dw_acc)
            vec_acc[...] = jnp.zeros_like(vec_acc)

        t_abs = i * tile + lax.broadcasted_iota(jnp.int32, (tile, 1), 0)
        last = i == steps - 1
        ext[0, tile + HALO:tile + HALO + PAD, :] = jnp.zeros((PAD, D), F32)
        ext[1, tile + HALO:tile + HALO + PAD, :] = jnp.zeros((PAD, D), F32)
        for gi, w in enumerate(POOL_WINDOWS):
            cols = slice(gw * gi, gw * (gi + 1))
            cnt = jnp.minimum(t_abs + 1, w).astype(F32)
            ext[0, 0:tile, cols] = dp_ref[:, cols] / cnt
            ext[0, tile:tile + HALO, cols] = jnp.where(last, 0.0, halo_ref[:, cols] * (1.0 / w))
        planes = _window_sums(ext, 0, tile + HALO, True)
        for gi, w in enumerate(POOL_WINDOWS):
            cols = slice(gw * gi, gw * (gi + 1))
            dproj[:, cols] = (ext[planes[gi], 0:tile, cols] - dp_ref[:, cols]).astype(BF16)
        dproj[:, D:] = dg_ref[...]
        _in_proj_tail(x_ref, dxn_ref, ng_ref, mod_ref, w_ref, dproj[...], dx_ref, dw_acc, vec_acc)

        @pl.when(last)
        def _():
            _tail_finish(ng_ref, mod_ref, dw_ref, vec_ref, dw_acc, vec_acc)

    row = pl.BlockSpec((tile, D), lambda i: (i, 0))
    halo = pl.BlockSpec((HALO, D), lambda i: (jnp.minimum((i + 1) * (tile // HALO), seq // HALO - 1), 0))
    fixed = lambda shape: pl.BlockSpec(shape, lambda i: (0,) * len(shape))
    return _call(
        body, name=f"pool_in_proj_bwd_{j}", grid=(steps,),
        out_shape=(jax.ShapeDtypeStruct((seq, D), F32), jax.ShapeDtypeStruct((POOL_IN, D), BF16),
                   jax.ShapeDtypeStruct((8, D), F32)),
        in_specs=[row, row, row, halo, row, _mod_row_spec(layer, NORM_ROW), _mod_spec(layer), _const_spec((POOL_IN, D)), ANY_SPEC],
        out_specs=(row, fixed((POOL_IN, D)), fixed((8, D))),
        scratch_shapes=[pltpu.VMEM((2, tile + HALO + PAD, D), F32), pltpu.VMEM((tile, POOL_IN), BF16), pltpu.VMEM((POOL_IN, D), F32),
                        pltpu.VMEM((8, D), F32)],
        compiler_params=_cparams(1),
    )(x, dxn, dpool, dpool, dg, rows, mod, w_t, after)


def _build_vec(vecs, gates, pool_vecs, gains, dsinks, loss_part):
    def body(v0, v1, v2, v3, g0, g2, p0, p1, n0, n1, s0, s1, loss_ref, out):
        out[...] = jnp.zeros_like(out)
        for i, v in enumerate((v0, v1, v2, v3)):
            out[3 * i:3 * i + 2, :] = v[0:2, :]
            out[12 + i:13 + i, :] = v[3:4, :]
        out[2:3, :] = g0[...]
        out[8:9, :] = g2[...]
        for j, (p, n, s) in enumerate(((p0, n0, s0), (p1, n1, s1))):
            out[3 * (2 * j + 1) + 2:3 * (2 * j + 1) + 3, :] = p[0:1, :]
            out[22 + j:23 + j, :] = p[1:2, :]
            out[16 + j:17 + j, :] = n[:, 0:D]
            out[18 + j:19 + j, 0:QK_W - D] = n[:, D:QK_W]
            out[20 + j:21 + j, 0:LANES] = s[...]
        out[24:25, 0:LANES] = loss_ref[...]

    vm = pl.BlockSpec(memory_space=pltpu.VMEM)
    args = (*vecs, gates[0], gates[2], *pool_vecs, *gains, *dsinks, loss_part)
    return _call(
        body, name="build_vec",
        out_shape=jax.ShapeDtypeStruct((VEC_ROWS, D), F32),
        in_specs=[vm] * len(args), out_specs=vm,
        compiler_params=_cparams(),
    )(*args)


def _sum_devices(g, after):
    rows = g.shape[1]

    def body(g_ref, after_ref, tot_ref, fold_ref):
        tot = g_ref[0]
        for p in range(1, N_DEV):
            tot = tot + g_ref[p]
        tot_ref[...] = tot
        f = tot[16:24, 0:LANES]
        for b in range(1, D // LANES):
            f = f + tot[16:24, LANES * b:LANES * (b + 1)]
        fold_ref[...] = f + pltpu.roll(f, HEAD_DIM, 1)

    return _call(
        body, name="sum_devices",
        out_shape=(jax.ShapeDtypeStruct((rows, D), F32), jax.ShapeDtypeStruct((8, LANES), F32)),
        in_specs=[pl.BlockSpec(memory_space=pltpu.VMEM), ANY_SPEC],
        out_specs=(pl.BlockSpec(memory_space=pltpu.VMEM), pl.BlockSpec(memory_space=pltpu.VMEM)),
        compiler_params=_cparams(),
    )(g, after)


def _adamw_small(params):
    n = len(params)

    def body(*refs):
        ins, outs = refs[:4 * n], refs[4 * n:]
        for p in range(n):
            w_ref, g_ref, m_ref, v_ref = ins[4 * p:4 * p + 4]
            outs[3 * p][...], outs[3 * p + 1][...], outs[3 * p + 2][...] = _adamw(w_ref[...], g_ref[...], m_ref[...], v_ref[...])

    vm = pl.BlockSpec(memory_space=pltpu.VMEM)
    out = _call(
        body, name="adamw_small",
        out_shape=tuple(jax.ShapeDtypeStruct(w.shape, F32) for (w, _, _, _) in params for _ in range(3)),
        in_specs=[vm] * (4 * n), out_specs=tuple([vm] * (3 * n)),
        compiler_params=_cparams(),
    )(*[a for p in params for a in p])
    return [tuple(out[3 * p:3 * p + 3]) for p in range(n)]


def _adamw_shards(name, me, fulls, lands, w, m, v, transpose, axis=0):
    nl = w.shape[0]
    wshape = w.shape[1:]
    own_shape = lands[0].shape[1:]

    def body(me_ref, *refs):
        own_refs, land_refs = refs[:nl], refs[nl:2 * nl]
        w_ref, m_ref, v_ref, g_out, d_out, m_out, v_out = refs[2 * nl:]
        layer = pl.program_id(0)
        for l in range(nl):
            @pl.when(layer == l)
            def _(l=l):
                g = own_refs[l][...].astype(F32)
                for k in range(N_DEV - 1):
                    g = g + land_refs[l][k].astype(F32)
                if transpose:
                    g = g.T
                g_out[...] = g
                d_out[...], m_out[...], v_out[...] = _adamw(w_ref[...], g, m_ref[...], v_ref[...])

    def own_index(l_, me_ref):
        idx = [0] * len(own_shape)
        idx[axis] = me_ref[0]
        return tuple(idx)

    own_spec = pl.BlockSpec(tuple(own_shape), own_index)
    land_spec = pl.BlockSpec((N_DEV - 1,) + tuple(own_shape), lambda l_, me_ref: (0,) * (1 + len(own_shape)))
    wspec = pl.BlockSpec((None,) + tuple(wshape), lambda l_, me_ref: (l_,) + (0,) * len(wshape))
    return _call(
        body, name=name,
        grid_spec=pltpu.PrefetchScalarGridSpec(num_scalar_prefetch=1, grid=(nl,),
                                               in_specs=[own_spec] * nl + [land_spec] * nl + [wspec] * 3,
                                               out_specs=(wspec,) * 4),
        out_shape=tuple(jax.ShapeDtypeStruct(w.shape, F32) for _ in range(4)),
        compiler_params=_cparams(1),
    )(me.reshape(1), *fulls, *lands, w, m, v)


def _constants():
    lane = np.arange(LANES)
    bd = (lane[:, None] // HEAD_DIM == lane[None, :] // HEAD_DIM).astype(np.float32)
    half = ROT_DIM // 2
    inv_freq = ROPE_THETA ** (-jnp.arange(half, dtype=F32) * 2.0 / ROT_DIM)
    invf = jnp.tile(inv_freq, LANES // half).reshape(1, LANES)
    return jnp.asarray(bd, BF16), invf


def kernel(x, c, positions, ada_w, ada_b, norm_g, attn_w_in, attn_q_norm, attn_k_norm, attn_sinks, attn_w_out, pool_w_in, pool_w_group, pool_scale, pool_w_out, loss_target, m_ada_w, m_ada_b, m_norm_g, m_attn_w_in, m_attn_q_norm, m_attn_k_norm, m_attn_sinks, m_attn_w_out, m_pool_w_in, m_pool_w_group, m_pool_scale, m_pool_w_out, v_ada_w, v_ada_b, v_norm_g, v_attn_w_in, v_attn_q_norm, v_attn_k_norm, v_attn_sinks, v_attn_w_out, v_pool_w_in, v_pool_w_group, v_pool_scale, v_pool_w_out):
    seq = x.shape[1]
    me = 4 * lax.axis_index("x") + 2 * lax.axis_index("y") + lax.axis_index("c")
    bd, invf = _constants()
    t_mm = min(512, seq)
    t_bw = min(256, seq)
    shard = pool_scale.shape[1]
    cols = ada_w.shape[2]

    w_in_rows = jnp.swapaxes(attn_w_in, 1, 2)
    w_first, = _prep_weights(me, [(w_in_rows, 0, "N")], "prep_first")
    first = jnp.concatenate([c, jnp.pad(pool_scale, ((0, 0), (0, D - shard))), jnp.zeros((5, D), F32)], axis=0)
    first_rows = lax.dynamic_update_slice(jnp.zeros((N_DEV * 8, D), F32), first, (me * 8, 0))
    c_sent, token = _gather_start([[first_rows]], [(0,)], w_first, "c_gather_start")
    first_w, token = _gather_first_start(w_first, token)
    prepped = _prep_weights(me, [(attn_w_out, 0, "N"), (pool_w_in, 0, "T"), (pool_w_out, 0, "N"), (pool_w_group, 0, "G"),
                                 (w_in_rows, 1, "N"), (attn_w_out, 1, "N"), (pool_w_in, 1, "T"), (pool_w_out, 1, "N"),
                                 (pool_w_group, 1, "G")], "prep_rest", after=(token,))
    first, = _gather_wait(c_sent[0], (0,), prepped[0], "c_gather_wait")
    first = first.reshape(N_DEV, 8, D)
    c_all = first[:, 0, :]
    scale_full = jnp.transpose(first[:, 1:3, :shard], (1, 0, 2)).reshape(2, D)
    mod_part = _ada_forward(c_all, ada_w)
    mod_rows = lax.dynamic_update_slice(jnp.zeros((N_DEV * DEPTH * N_DEV, cols), F32),
                                        mod_part.reshape(DEPTH * N_DEV, cols), (me * DEPTH * N_DEV, 0))
    mod_sent, token = _gather_start([[mod_rows]], [(0,)], token, "mod_gather_start")
    rope = _rope_table(positions.reshape(seq, 1), invf + token[0:1, 0:1], t_mm)
    mod_all, = _gather_wait(mod_sent[0], (0,), rope, "mod_gather_wait")
    mod_all = mod_all.reshape(N_DEV, DEPTH, N_DEV, cols)
    mine = lax.dynamic_index_in_dim(mod_all, me, axis=2, keepdims=False)
    mod = jnp.transpose(mine, (1, 0, 2)).reshape(DEPTH, 3 * D) + ada_b
    pool_rows = jnp.stack([jnp.zeros_like(scale_full[0]), scale_full[0], jnp.zeros_like(scale_full[0]), scale_full[1]])
    mod = jnp.concatenate([mod.reshape(DEPTH, 3, D), norm_g[:, None, :], pool_rows[:, None, :],
                           jnp.zeros((DEPTH, 3, D), F32)], axis=1)
    rows = mod.reshape(DEPTH, 8, 1, D)

    groups = [prepped[0:1], prepped[1:4], prepped[4:6], prepped[6:9]]
    gaxes = [(0,), (0,), (0, 0, 1), (0, 0), (0, 0, 1)]
    first_w, token = _gather_first_forward(first_w, mod)
    rest, token = _gather_start(groups, gaxes[1:], token, "gather_start_rest")
    started = [None] + rest

    saved, weights = [], []
    h = x[0]
    for i in range(DEPTH):
        j = i // 2
        s = dict(x=h)
        if i == 0:
            w_in_t = _gather_first_wait(first_w, token)
        else:
            wts = _gather_wait(started[i + 1], gaxes[i + 1], h, f"gather_wait_{i}")
        if i % 2 == 0:
            if i > 0:
                w_in_t, w_out = wts
            s["gain"] = jnp.concatenate([jnp.tile(attn_q_norm[j], N_HEADS), jnp.tile(attn_k_norm[j], N_KV)]).reshape(1, QK_W)
            s["qk_raw"], s["qs"], s["kd"], s["vd"], s["g"] = _attn_in_proj(
                h, rope, rows, mod, i, w_in_t, j, s["gain"], bd, t_mm)
            s["o"] = _attn_forward(attn_sinks, s["qs"], s["kd"], s["vd"], j)
            if i == 0:
                w_out, = _gather_wait(started[1], gaxes[1], s["o"], "gather_wait_0_out")
            h, s["br"] = _attn_out_proj(h, s["o"], s["g"], w_out, j, mod, i, t_mm)
            weights.append((w_in_t, w_out))
        else:
            p_in_t, p_out, p_grp = wts
            s["v"], s["g"] = _pool_in_proj(h, rows, mod, i, p_in_t, j, t_mm)
            if i < DEPTH - 1:
                h, s["br"] = _pool_mix_out(h, s["v"], s["g"], p_grp, p_out, j, rows, mod, i, t_mm)
            else:
                dx, s["br"], loss_part = _pool_mix_out(h, s["v"], s["g"], p_grp, p_out, j, rows, mod, i, t_mm,
                                                       loss_target[0])
            weights.append(wts)
        saved.append(s)

    vecs, gates, gains, dsinks, pool_vecs = [None] * DEPTH, [None] * DEPTH, [None] * 2, [None] * 2, [None] * 2
    sent = {}
    token = jnp.zeros((8, LANES), F32)
    for i in reversed(range(DEPTH)):
        j = i // 2
        s = saved[i]
        if i % 2 == 0:
            w_in_t, w_out = weights[i]
            dos, dg, d_w_out, gates[i] = _attn_out_proj_bwd(dx, s["br"], s["o"], s["g"], w_out, j, mod, i, t_mm, token)
            if i == 0:
                sent["0_out"], token = _scatter_start([d_w_out], (0,), "scatter_start_0_out", token)
            dq, dk, dv, dsinks[j] = _attn_backward(attn_sinks, s["qs"], dos, s["kd"], s["vd"], j, token)
            dx, d_in_t, vecs[i], gains[j] = _attn_in_proj_bwd(
                s["x"], dx, rope, s["qk_raw"], dq, dk, dv, dg, rows, mod, i, w_in_t, j, s["gain"], bd, t_bw)
            if i > 0:
                sent[i], token = _scatter_start([d_in_t, d_w_out], (0, 0), f"scatter_start_{i}", token)
        else:
            p_in_t, p_out, p_grp = weights[i]
            dpool, dg, d_p_out, d_p_grp, pool_vecs[j] = _pool_mix_out_bwd(
                dx, s["br"], s["v"], s["g"], p_grp, p_out, j, rows, mod, i, t_mm, token)
            dx, d_in_t, vecs[i] = _pool_in_proj_bwd(s["x"], dx, dpool, dg, rows, mod, i, p_in_t, j, t_bw, token)
            sent[i], token = _scatter_start([d_in_t, d_p_out, d_p_grp], (0, 0, 1), f"scatter_start_{i}", token)

    vec = _build_vec(vecs, gates, pool_vecs, gains, dsinks, loss_part)
    vec_rows = lax.dynamic_update_slice(jnp.zeros((N_DEV * VEC_ROWS, D), F32), vec, (me * VEC_ROWS, 0))
    vec_sent, token = _gather_start([[vec_rows]], [(0,)], loss_part, "vec_gather_start")
    sent["0_in"], token = _scatter_start([d_in_t], (0,), "scatter_start_0_in", token)

    got = {}
    for i in (3, 1):
        fulls, lands = _scatter_wait(sent[i], (0, 0, 1), token, f"scatter_wait_{i}")
        got[i] = dict(zip(("in", "out", "grp"), zip(fulls, lands)))
    pick = lambda ls, kind: ([got[i][kind][0] for i in ls], [got[i][kind][1] for i in ls])
    res = {}
    res["pool_w_in"] = _adamw_shards("adamw_pool_w_in", me, *pick((1, 3), "in"), pool_w_in, m_pool_w_in, v_pool_w_in, True)
    res["pool_w_out"] = _adamw_shards("adamw_pool_w_out", me, *pick((1, 3), "out"), pool_w_out, m_pool_w_out,
                                      v_pool_w_out, False)
    res["pool_w_group"] = _adamw_shards("adamw_pool_w_group", me, *pick((1, 3), "grp"), pool_w_group, m_pool_w_group,
                                        v_pool_w_group, False, axis=1)

    vec_all, = _gather_wait(vec_sent[0], (0,), res["pool_w_group"][0], "vec_gather_wait")
    vec_all = vec_all.reshape(N_DEV, VEC_ROWS, D)
    tot, folded = _sum_devices(vec_all, token)
    loss = tot[24, 0]
    small = dict(
        ada_b=(ada_b, tot[0:12].reshape(DEPTH, 3 * D), m_ada_b, v_ada_b),
        norm_g=(norm_g, tot[12:16], m_norm_g, v_norm_g),
        q_norm=(attn_q_norm, folded[0:2, :HEAD_DIM], m_attn_q_norm, v_attn_q_norm),
        k_norm=(attn_k_norm, folded[2:4, :HEAD_DIM], m_attn_k_norm, v_attn_k_norm),
        sinks=(attn_sinks, tot[20:22, :N_HEADS], m_attn_sinks, v_attn_sinks),
        pool_scale=(pool_scale, lax.dynamic_slice(tot, (22, me * shard), (2, shard)), m_pool_scale, v_pool_scale),
    )
    res.update({k: (a[1],) + upd for (k, a), upd in zip(small.items(), _adamw_small(list(small.values())))})

    dmod_all = vec_all[:, 0:12, :].reshape(N_DEV, DEPTH, 3 * D)
    dmod_mine = lax.dynamic_slice_in_dim(dmod_all, me * cols, cols, axis=2)
    dmod_mine = jnp.pad(jnp.transpose(dmod_mine, (1, 0, 2)), ((0, 0), (0, N_DEV), (0, 0))) + token[0, 0]
    res["ada_w"] = _ada_backward_adamw(jnp.pad(c_all, ((0, N_DEV), (0, 0))), dmod_mine, ada_w, m_ada_w, v_ada_w)

    fulls, lands = _scatter_wait(sent[2], (0, 0), res["ada_w"][0], "scatter_wait_2")
    got[2] = dict(zip(("in", "out"), zip(fulls, lands)))
    got[0] = {}
    for kind in ("out", "in"):
        fulls, lands = _scatter_wait(sent["0_" + kind], (0,), res["ada_w"][0], "scatter_wait_0_" + kind)
        got[0][kind] = (fulls[0], lands[0])
    res["attn_w_out"] = _adamw_shards("adamw_attn_w_out", me, *pick((0, 2), "out"), attn_w_out, m_attn_w_out,
                                      v_attn_w_out, False)
    res["attn_w_in"] = tuple(jnp.swapaxes(a, 1, 2) for a in _adamw_shards(
        "adamw_attn_w_in", me, *pick((0, 2), "in"), w_in_rows, jnp.swapaxes(m_attn_w_in, 1, 2),
        jnp.swapaxes(v_attn_w_in, 1, 2), False))

    order = ("ada_w", "ada_b", "norm_g", "attn_w_in", "q_norm", "k_norm", "sinks", "attn_w_out", "pool_w_in",
             "pool_w_group", "pool_scale", "pool_w_out")
    return (loss, dx[None], *[res[k][0] for k in order], *[res[k][1] for k in order], *[res[k][2] for k in order],
            *[res[k][3] for k in order])
```

```python
import numpy as np
import jax
import jax.numpy as jnp
from jax import lax
from jax.experimental import pallas as pl
from jax.experimental.pallas import tpu as pltpu

F32 = jnp.float32
BF16 = jnp.bfloat16
MESH = pl.DeviceIdType.MESH

N_DEV = 8
D = 1024
DEPTH = 4
HEAD_DIM = 64
N_HEADS = 16
N_KV = 4
QK_W = 1280
ATTN_IN = 2560
POOL_IN = 2048
QBLK = 128
KX_W = N_KV * 128
CHUNK = 256
POOL_WINDOWS = (2, 4, 8, 16)
HALO = 16
ROPE_THETA = 500000.0
ROT_DIM = 16
NORM_EPS = 1e-6
ADAM_LR = 0.001
ADAM_B1 = 0.9
ADAM_B2 = 0.999
ADAM_EPS = 1e-08
ADAM_WD = 0.01
ADAM_STEP = 10

LANES = 128
VMEM_LIMIT = 56 * 2**20
VEC_ROWS = 32


def _cparams(n_grid=0, **kw):
    if n_grid:
        kw["dimension_semantics"] = ("arbitrary",) * n_grid
    return pltpu.CompilerParams(vmem_limit_bytes=VMEM_LIMIT, **kw)


def _call(body, **kw):
    return pl.pallas_call(body, **kw)


def _mod_spec(layer):
    return pl.BlockSpec((None, 8, D), lambda *_: (layer, 0, 0), pipeline_mode=pl.Buffered(1))


def _mod_row_spec(layer, row):
    return pl.BlockSpec((None, None, 1, D), lambda *_: (layer, row, 0, 0), pipeline_mode=pl.Buffered(1))


NORM_ROW, POOL_SCALE_ROW = 3, 4


def _const_spec(shape):
    nd = len(shape)
    return pl.BlockSpec(shape, lambda *_: (0,) * nd, pipeline_mode=pl.Buffered(1))


def _dot(a, b):
    return jnp.dot(a, b, preferred_element_type=F32)


def _dot_nt(a, b):
    return lax.dot_general(a, b, (((1,), (1,)), ((), ())), preferred_element_type=F32)


def _dot_tn(a, b):
    return lax.dot_general(a, b, (((0,), (0,)), ((), ())), preferred_element_type=F32)


def _group_mean(x, m):
    return _dot(x.astype(BF16), m) * (1.0 / HEAD_DIM)


def _sigmoid(g):
    return 1.0 / (1.0 + jnp.exp(-g))


def _norm_mod(x, ng, sc, sh):
    r = lax.rsqrt(jnp.mean(x * x, axis=-1, keepdims=True) + NORM_EPS)
    xh = x * r
    h = (xh * ng) * (1.0 + sc) + sh
    return xh, r, h


def _rope_table(pos_col, invf_row, tile):
    seq = pos_col.shape[0]

    def body(pos_ref, invf_ref, out_ref):
        ang = pos_ref[...].astype(F32) * invf_ref[...]
        l64 = lax.broadcasted_iota(jnp.int32, (tile, LANES), 1) & (HEAD_DIM - 1)
        cs, sn = jnp.cos(ang), jnp.sin(ang)
        out_ref[:, 0:LANES] = jnp.where(l64 < ROT_DIM, cs, 1.0)
        out_ref[:, LANES:2 * LANES] = jnp.where(l64 < ROT_DIM // 2, -sn, 0.0)
        out_ref[:, 2 * LANES:3 * LANES] = jnp.where((l64 >= ROT_DIM // 2) & (l64 < ROT_DIM), sn, 0.0)

    return _call(
        body, name="rope_table", grid=(seq // tile,),
        out_shape=jax.ShapeDtypeStruct((seq, 3 * LANES), F32),
        in_specs=[pl.BlockSpec((tile, 1), lambda i: (i, 0)), _const_spec((1, LANES))],
        out_specs=pl.BlockSpec((tile, 3 * LANES), lambda i: (i, 0)),
        compiler_params=_cparams(1),
    )(pos_col, invf_row)


def _rope_tabs(rope_ref):
    return rope_ref[:, 0:LANES], rope_ref[:, LANES:2 * LANES], rope_ref[:, 2 * LANES:3 * LANES]


def _rope(y, tabs):
    cos_t, sin_a, sin_b = tabs
    return y * cos_t + pltpu.roll(y, LANES - ROT_DIM // 2, 1) * sin_a + pltpu.roll(y, ROT_DIM // 2, 1) * sin_b


def _rope_bwd(dy, tabs):
    cos_t, sin_a, sin_b = tabs
    return dy * cos_t + pltpu.roll(dy * sin_a, ROT_DIM // 2, 1) + pltpu.roll(dy * sin_b, LANES - ROT_DIM // 2, 1)


def _low_half(rows):
    return lax.broadcasted_iota(jnp.int32, (rows, LANES), 1) < HEAD_DIM


def _adamw(w, g, m, v):
    m = ADAM_B1 * m + (1.0 - ADAM_B1) * g
    v = ADAM_B2 * v + (1.0 - ADAM_B2) * (g * g)
    m_hat = m / (1.0 - ADAM_B1 ** ADAM_STEP)
    v_hat = v / (1.0 - ADAM_B2 ** ADAM_STEP)
    delta = -ADAM_LR * (m_hat / (jnp.sqrt(v_hat) + ADAM_EPS) + ADAM_WD * w)
    return delta, m, v


def _my_position():
    x, y, c = lax.axis_index("x"), lax.axis_index("y"), lax.axis_index("c")
    return x, y, c, 4 * x + 2 * y + c


def _peers(x, y, c):
    out = []
    for k in range(1, N_DEV):
        px = 1 - x if k & 4 else x
        py = 1 - y if k & 2 else y
        pc = 1 - c if k & 1 else c
        out.append(((px, py, pc), 4 * px + 2 * py + pc))
    return out


def _shard_rows(ref, idx, rows, axis):
    sl = [slice(None)] * len(ref.shape)
    sl[axis] = pl.ds(idx * rows, rows)
    return ref.at[tuple(sl)]


def _own_and_peer_rows(ref, me, idx, axis):
    rows = ref.shape[axis] // N_DEV
    return _shard_rows(ref, me, rows, axis), _shard_rows(ref, idx, rows, axis)


HBM_SPEC = pl.BlockSpec(memory_space=pltpu.HBM)
SEM_SPEC = pl.BlockSpec(memory_space=pltpu.SEMAPHORE)
ANY_SPEC = pl.BlockSpec(memory_space=pl.ANY)
DATAFLOW = pltpu.SideEffectType.DATAFLOW_SIDE_EFFECTING


def _hbm(a):
    return pltpu.with_memory_space_constraint(a, pltpu.HBM)


def _gather_start(layers, axes, after, name):
    flat = [a for arrs in layers for a in arrs]
    flat_axes = [ax for axs in axes for ax in axs]
    n, nl = len(flat), len(layers)

    def body(*refs):
        ins, sems, token = refs[:n], refs[n + 1:n + 1 + 2 * nl], refs[-1]
        x, y, c, me = _my_position()
        a0 = 0
        for li, arrs in enumerate(layers):
            for k, (peer, _) in enumerate(_peers(x, y, c)):
                for a in range(len(arrs)):
                    rows, _ = _own_and_peer_rows(ins[a0 + a], me, me, flat_axes[a0 + a])
                    pltpu.make_async_remote_copy(rows, rows, sems[2 * li].at[k * len(arrs) + a],
                                                 sems[2 * li + 1].at[k * len(arrs) + a],
                                                 device_id=peer, device_id_type=MESH).start()
            a0 += len(arrs)
        token[...] = jnp.zeros_like(token)

    sem_shapes = []
    for arrs in layers:
        sem_shapes += [pltpu.SemaphoreType.DMA(((N_DEV - 1) * len(arrs),))] * 2
    out = _call(
        body, name=name,
        out_shape=(*sem_shapes, *[pltpu.HBM(a.shape, a.dtype) for a in flat], jax.ShapeDtypeStruct((8, LANES), F32)),
        in_specs=[HBM_SPEC] * n + [ANY_SPEC],
        out_specs=(*[SEM_SPEC] * (2 * nl), *[HBM_SPEC] * n, pl.BlockSpec(memory_space=pltpu.VMEM)),
        input_output_aliases={a: 2 * nl + a for a in range(n)},
        compiler_params=_cparams(has_side_effects=DATAFLOW),
    )(*[_hbm(a) for a in flat], after)
    per_layer, a0 = [], 0
    for li, arrs in enumerate(layers):
        per_layer.append((out[2 * li], out[2 * li + 1], list(out[2 * nl + a0:2 * nl + a0 + len(arrs)])))
        a0 += len(arrs)
    return per_layer, out[-1]


def _gather_wait(started, axes, after, name):
    send_sems, recv_sems, arrs = started
    n = len(arrs)

    def body(*refs):
        ins, send_ref, recv_ref = refs[:n], refs[n], refs[n + 1]
        x, y, c, me = _my_position()
        for k, (peer, idx) in enumerate(_peers(x, y, c)):
            for a in range(n):
                own, theirs = _own_and_peer_rows(ins[a], me, idx, axes[a])
                cp = pltpu.make_async_remote_copy(own, theirs, send_ref.at[k * n + a], recv_ref.at[k * n + a],
                                                  device_id=peer, device_id_type=MESH)
                cp.wait_send()
                cp.wait_recv()

    return _call(
        body, name=name,
        out_shape=tuple(pltpu.HBM(a.shape, a.dtype) for a in arrs),
        in_specs=[HBM_SPEC] * n + [SEM_SPEC, SEM_SPEC, ANY_SPEC],
        out_specs=tuple([HBM_SPEC] * n),
        input_output_aliases={a: a for a in range(n)},
        compiler_params=_cparams(has_side_effects=DATAFLOW),
    )(*arrs, send_sems, recv_sems, after)


def _first_relations(x, y, c):
    return [(x, y, 1 - c), (1 - x, y, c), (x, 1 - y, c), (1 - x, 1 - y, c)]


def _gather_first_start(arr, after):
    n_rel = 4

    def body(a_ref, after_ref, send_ref, recv_ref, thru, token):
        x, y, c, me = _my_position()
        rows, _ = _own_and_peer_rows(a_ref, me, me, 0)
        for k, peer in enumerate(_first_relations(x, y, c)):
            pltpu.make_async_remote_copy(rows, rows, send_ref.at[k], recv_ref.at[k], device_id=peer, device_id_type=MESH).start()
        token[...] = jnp.zeros_like(token)

    sem = pltpu.SemaphoreType.DMA((n_rel,))
    out = _call(
        body, name="gather_first_start",
        out_shape=(sem, sem, pltpu.HBM(arr.shape, arr.dtype), jax.ShapeDtypeStruct((8, LANES), F32)),
        in_specs=[HBM_SPEC, ANY_SPEC],
        out_specs=(SEM_SPEC, SEM_SPEC, HBM_SPEC, pl.BlockSpec(memory_space=pltpu.VMEM)),
        input_output_aliases={0: 2},
        compiler_params=_cparams(has_side_effects=DATAFLOW),
    )(_hbm(arr), after)
    return out[:3], out[3]


def _gather_first_forward(started, after):
    send_a, recv_a, arr = started

    def body(a_ref, send_a_ref, recv_a_ref, after_ref, send_b_ref, recv_b_ref, thru, token):
        x, y, c, me = _my_position()
        sibling = (x, y, 1 - c)
        for k, peer in enumerate(_first_relations(x, y, c)):
            own, theirs = _own_and_peer_rows(a_ref, me, 4 * peer[0] + 2 * peer[1] + peer[2], 0)
            cp = pltpu.make_async_remote_copy(own, theirs, send_a_ref.at[k], recv_a_ref.at[k], device_id=peer, device_id_type=MESH)
            cp.wait_send()
            cp.wait_recv()
            if k > 0:
                pltpu.make_async_remote_copy(theirs, theirs, send_b_ref.at[k - 1], recv_b_ref.at[k - 1],
                                             device_id=sibling, device_id_type=MESH).start()
        token[...] = jnp.zeros_like(token)

    sem = pltpu.SemaphoreType.DMA((3,))
    out = _call(
        body, name="gather_first_forward",
        out_shape=(sem, sem, pltpu.HBM(arr.shape, arr.dtype), jax.ShapeDtypeStruct((8, LANES), F32)),
        in_specs=[HBM_SPEC, SEM_SPEC, SEM_SPEC, ANY_SPEC],
        out_specs=(SEM_SPEC, SEM_SPEC, HBM_SPEC, pl.BlockSpec(memory_space=pltpu.VMEM)),
        input_output_aliases={0: 2},
        compiler_params=_cparams(has_side_effects=DATAFLOW),
    )(arr, send_a, recv_a, after)
    return out[:3], out[3]


def _gather_first_wait(forwarded, after):
    send_b, recv_b, arr = forwarded

    def body(a_ref, send_b_ref, recv_b_ref, after_ref, thru):
        x, y, c, me = _my_position()
        sibling = (x, y, 1 - c)
        for k, peer in enumerate(_first_relations(x, y, c)[1:]):
            _, sent = _own_and_peer_rows(a_ref, me, 4 * peer[0] + 2 * peer[1] + peer[2], 0)
            _, got = _own_and_peer_rows(a_ref, me, 4 * peer[0] + 2 * peer[1] + (1 - peer[2]), 0)
            cp = pltpu.make_async_remote_copy(sent, got, send_b_ref.at[k], recv_b_ref.at[k], device_id=sibling, device_id_type=MESH)
            cp.wait_send()
            cp.wait_recv()

    return _call(
        body, name="gather_first_wait",
        out_shape=pltpu.HBM(arr.shape, arr.dtype),
        in_specs=[HBM_SPEC, SEM_SPEC, SEM_SPEC, ANY_SPEC],
        out_specs=HBM_SPEC,
        input_output_aliases={0: 0},
        compiler_params=_cparams(has_side_effects=DATAFLOW),
    )(arr, send_b, recv_b, after)


def _scatter_start(fulls, axes, name, after):
    n = len(fulls)
    lands = []
    for f, ax in zip(fulls, axes):
        shp = list(f.shape)
        shp[ax] //= N_DEV
        lands.append(_hbm(lax.empty((N_DEV - 1,) + tuple(shp), f.dtype)))

    def body(*refs):
        srcs, dsts, send_ref, recv_ref, token = refs[:n], refs[n:2 * n], refs[2 * n + 1], refs[2 * n + 2], refs[-1]
        x, y, c, me = _my_position()
        for k, (peer, idx) in enumerate(_peers(x, y, c)):
            for a in range(n):
                _, theirs = _own_and_peer_rows(srcs[a], me, idx, axes[a])
                pltpu.make_async_remote_copy(theirs, dsts[a].at[k], send_ref.at[k * n + a], recv_ref.at[k * n + a],
                                             device_id=peer, device_id_type=MESH).start()
        token[...] = jnp.zeros_like(token)

    sem = pltpu.SemaphoreType.DMA(((N_DEV - 1) * n,))
    out = _call(
        body, name=name,
        out_shape=(sem, sem, *[pltpu.HBM(a.shape, a.dtype) for a in fulls], *[pltpu.HBM(a.shape, a.dtype) for a in lands],
                   jax.ShapeDtypeStruct((8, LANES), F32)),
        in_specs=[HBM_SPEC] * (2 * n) + [ANY_SPEC],
        out_specs=(SEM_SPEC, SEM_SPEC, *[HBM_SPEC] * (2 * n), pl.BlockSpec(memory_space=pltpu.VMEM)),
        input_output_aliases={a: 2 + a for a in range(2 * n)},
        compiler_params=_cparams(has_side_effects=DATAFLOW),
    )(*[_hbm(a) for a in fulls], *lands, after)
    return (out[0], out[1], list(out[2:2 + n]), list(out[2 + n:2 + 2 * n])), out[-1]


def _scatter_wait(started, axes, after, name):
    send_sems, recv_sems, fulls, lands = started
    n = len(fulls)

    def body(*refs):
        srcs, dsts, send_ref, recv_ref = refs[:n], refs[n:2 * n], refs[2 * n], refs[2 * n + 1]
        x, y, c, me = _my_position()
        for k, (peer, idx) in enumerate(_peers(x, y, c)):
            for a in range(n):
                _, theirs = _own_and_peer_rows(srcs[a], me, idx, axes[a])
                cp = pltpu.make_async_remote_copy(theirs, dsts[a].at[k], send_ref.at[k * n + a], recv_ref.at[k * n + a],
                                                  device_id=peer, device_id_type=MESH)
                cp.wait_send()
                cp.wait_recv()

    out = _call(
        body, name=name,
        out_shape=tuple(pltpu.HBM(a.shape, a.dtype) for a in (*fulls, *lands)),
        in_specs=[HBM_SPEC] * (2 * n) + [SEM_SPEC, SEM_SPEC, ANY_SPEC],
        out_specs=tuple([HBM_SPEC] * (2 * n)),
        input_output_aliases={a: a for a in range(2 * n)},
        compiler_params=_cparams(has_side_effects=DATAFLOW),
    )(*fulls, *lands, send_sems, recv_sems, after)
    return list(out[:n]), list(out[n:])


def _prep_weights(me, items, name, after=()):
    def body(me_ref, *refs):
        outs = refs[len(items) + len(after):]
        for (_, _, kind), src, dst in zip(items, refs[:len(items)], outs):
            dst[...] = (src[...].T if kind == "T" else src[...]).astype(BF16)

    ins, in_specs, out_shapes, out_specs = [], [], [], []
    for src, j, kind in items:
        shard = src.shape[1:]
        ins.append(src)
        in_specs.append(pl.BlockSpec((None,) + tuple(shard), lambda i, me_ref, j=j, nd=len(shard): (j,) + (0,) * nd))
        if kind == "G":
            out_shapes.append((shard[0], N_DEV * shard[1], shard[2]))
            out_specs.append(pl.BlockSpec(tuple(shard), lambda i, me_ref: (0, me_ref[0], 0)))
        else:
            rows = shard[1] if kind == "T" else shard[0]
            out_shapes.append((N_DEV * rows, D))
            out_specs.append(pl.BlockSpec((rows, D), lambda i, me_ref: (me_ref[0], 0)))
    out = _call(
        body, name=name,
        grid_spec=pltpu.PrefetchScalarGridSpec(num_scalar_prefetch=1, grid=(1,), in_specs=in_specs + [ANY_SPEC] * len(after),
                                               out_specs=tuple(out_specs)),
        out_shape=tuple(jax.ShapeDtypeStruct(s, BF16) for s in out_shapes),
        compiler_params=_cparams(1),
    )(me.reshape(1), *ins, *after)
    return list(out)


def _ada_forward(c_all, ada_w):
    cols = ada_w.shape[2]

    def body(c_ref, w_ref, o_ref):
        cv = c_ref[...]
        sc = (cv * _sigmoid(cv)).astype(BF16)
        o_ref[...] = _dot(sc, w_ref[...].astype(BF16))

    return _call(
        body, name="ada_forward", grid=(DEPTH,),
        out_shape=jax.ShapeDtypeStruct((DEPTH, N_DEV, cols), F32),
        in_specs=[pl.BlockSpec((N_DEV, D), lambda i: (0, 0)), pl.BlockSpec((None, D, cols), lambda i: (i, 0, 0))],
        out_specs=pl.BlockSpec((None, N_DEV, cols), lambda i: (i, 0, 0)),
        compiler_params=_cparams(1),
    )(c_all, ada_w)


def _ada_backward_adamw(c_pad, dmod_pad, w, m, v):
    cols = w.shape[2]

    def body(c_ref, dm_ref, w_ref, m_ref, v_ref, g_out, d_out, m_out, v_out):
        cv = c_ref[...]
        sc = (cv * _sigmoid(cv)).astype(BF16)
        g = _dot_tn(sc, dm_ref[...].astype(BF16))
        g_out[...] = g
        d_out[...], m_out[...], v_out[...] = _adamw(w_ref[...], g, m_ref[...], v_ref[...])

    wspec = pl.BlockSpec((None, D, cols), lambda i: (i, 0, 0))
    return _call(
        body, name="ada_backward_adamw", grid=(DEPTH,),
        out_shape=tuple(jax.ShapeDtypeStruct(w.shape, F32) for _ in range(4)),
        in_specs=[pl.BlockSpec((2 * N_DEV, D), lambda i: (0, 0)), pl.BlockSpec((None, 2 * N_DEV, cols), lambda i: (i, 0, 0)),
                  wspec, wspec, wspec],
        out_specs=(wspec, wspec, wspec, wspec),
        compiler_params=_cparams(1),
    )(c_pad, dmod_pad, w, m, v)


def _attn_in_proj(x, rope, rows, mod, layer, w_t, j, gain, bd, tile):
    seq = x.shape[0]

    def body(x_ref, rope_ref, ng_ref, mod_ref, w_ref, gain_ref, bd_ref, qk_ref, qs_ref, kd_ref, vd_ref, g_ref):
        _, _, h = _norm_mod(x_ref[...], ng_ref[...], mod_ref[1:2, :], mod_ref[0:1, :])
        hb = h.astype(BF16)
        tabs = _rope_tabs(rope_ref)
        low = _low_half(tile)
        bdm = bd_ref[...]

        def put_kv(ref, blk, first_kv):
            sw = pltpu.roll(blk, HEAD_DIM, 1)
            ref[:, LANES * first_kv:LANES * (first_kv + 1)] = jnp.where(low, blk, sw).astype(BF16)
            ref[:, LANES * (first_kv + 1):LANES * (first_kv + 2)] = jnp.where(low, sw, blk).astype(BF16)

        def project(c):
            return _dot_nt(hb, w_ref[CHUNK * c:CHUNK * (c + 1), :])

        n_chunks = ATTN_IN // CHUNK
        per = CHUNK // LANES
        nxt = project(0)
        for c in range(n_chunks):
            cur = nxt
            if c + 1 < n_chunks:
                nxt = project(c + 1)
            col = CHUNK * c
            if col >= QK_W + N_KV * HEAD_DIM:
                g_ref[:, col - QK_W - N_KV * HEAD_DIM:col - QK_W - N_KV * HEAD_DIM + CHUNK] = cur.astype(BF16)
            elif col >= QK_W:
                for t in range(per):
                    put_kv(vd_ref, cur[:, LANES * t:LANES * (t + 1)], (col - QK_W) // HEAD_DIM + 2 * t)
            else:
                qk_ref[:, col:col + CHUNK] = cur
                for t in range(per):
                    b = per * c + t
                    blk = cur[:, LANES * t:LANES * (t + 1)]
                    ms = _group_mean(blk * blk, bdm)
                    y = (blk * lax.rsqrt(ms + NORM_EPS)) * gain_ref[:, LANES * b:LANES * (b + 1)]
                    rp = _rope(y, tabs)
                    if b < D // LANES:
                        rp = rp * (HEAD_DIM ** -0.5)
                        qs_ref[:, 2 * LANES * b:2 * LANES * b + LANES] = jnp.where(low, rp, 0.0).astype(BF16)
                        qs_ref[:, 2 * LANES * b + LANES:2 * LANES * (b + 1)] = jnp.where(low, 0.0, rp).astype(BF16)
                    else:
                        put_kv(kd_ref, rp, 2 * (b - D // LANES))

    row = lambda w: pl.BlockSpec((tile, w), lambda i: (i, 0))
    return _call(
        body, name=f"attn_in_proj_{j}", grid=(seq // tile,),
        out_shape=(jax.ShapeDtypeStruct((seq, QK_W), F32), jax.ShapeDtypeStruct((seq, N_HEADS * LANES), BF16),
                   jax.ShapeDtypeStruct((seq, KX_W), BF16), jax.ShapeDtypeStruct((seq, KX_W), BF16),
                   jax.ShapeDtypeStruct((seq, D), BF16)),
        in_specs=[row(D), row(3 * LANES), _mod_row_spec(layer, NORM_ROW), _mod_spec(layer), _const_spec((ATTN_IN, D)),
                  _const_spec((1, QK_W)), _const_spec((LANES, LANES))],
        out_specs=(row(QK_W), row(N_HEADS * LANES), row(KX_W), row(KX_W), row(D)),
        compiler_params=_cparams(1),
    )(x, rope, rows, mod, w_t, gain, bd)


def _band_mask(n, rows, keys_on_rows):
    shape = (2 * QBLK, rows) if keys_on_rows else (rows, 2 * QBLK)
    qi = lax.broadcasted_iota(jnp.int32, shape, 1 if keys_on_rows else 0) & (QBLK - 1)
    kj = lax.broadcasted_iota(jnp.int32, shape, 0 if keys_on_rows else 1)
    diff = QBLK + qi - kj
    first_key = jnp.where(n > 0, 0, QBLK)
    return (diff >= 0) & (diff < QBLK) & (kj >= first_key)


def _pair_up(st, low):
    return jnp.concatenate([jnp.where(low, st[0:QBLK], st[QBLK:2 * QBLK]),
                            jnp.where(low, st[2 * QBLK:3 * QBLK], st[3 * QBLK:4 * QBLK])], axis=1)


def _attn_forward(sinks, qs, kd, vd, j):
    seq = qs.shape[0]
    per = 2
    nb = seq // (per * QBLK)

    def body(sink_ref, q_ref, kp_ref, kc_ref, vp_ref, vc_ref, o_ref):
        n = pl.program_id(0)
        low = _low_half(QBLK)
        rowi = lax.broadcasted_iota(jnp.int32, (4 * QBLK, 1), 0)
        groups = per * N_KV

        def keys(p_ref, c_ref, blk, kv):
            cols = slice(LANES * kv, LANES * (kv + 1))
            if blk == 0:
                return jnp.concatenate([p_ref[:, cols], c_ref[0:QBLK, cols]], axis=0)
            return c_ref[QBLK * (blk - 1):QBLK * (blk + 1), cols]

        def scores(g):
            blk, kv = divmod(g, N_KV)
            q = jnp.concatenate([q_ref[QBLK * blk:QBLK * (blk + 1), LANES * h:LANES * (h + 1)]
                                 for h in range(4 * kv, 4 * kv + 4)], axis=0)
            return _dot_nt(q, keys(kp_ref, kc_ref, blk, kv))

        nxt = scores(0)
        for g in range(groups):
            blk, kv = divmod(g, N_KV)
            ok = _band_mask(n if blk == 0 else 1, 4 * QBLK, False)
            s = jnp.where(ok, nxt, -1e30)
            if g + 1 < groups:
                nxt = scores(g + 1)
            sink = jnp.where(rowi < QBLK, sink_ref[j, 4 * kv],
                             jnp.where(rowi < 2 * QBLK, sink_ref[j, 4 * kv + 1],
                                       jnp.where(rowi < 3 * QBLK, sink_ref[j, 4 * kv + 2], sink_ref[j, 4 * kv + 3])))
            m = jnp.maximum(jnp.max(s, axis=1, keepdims=True), sink)
            p = jnp.exp(s - m)
            den = jnp.sum(p, axis=1, keepdims=True) + jnp.exp(sink - m)
            o_st = _dot((p / den).astype(BF16), keys(vp_ref, vc_ref, blk, kv))
            o_ref[QBLK * blk:QBLK * (blk + 1), 2 * LANES * kv:2 * LANES * (kv + 1)] = _pair_up(o_st, low).astype(BF16)

    cur = lambda w: pl.BlockSpec((per * QBLK, w), lambda n: (n, 0))
    prev = lambda w: pl.BlockSpec((QBLK, w), lambda n: (jnp.maximum(per * n - 1, 0), 0))
    return _call(
        body, name=f"attn_forward_{j}", grid=(nb,),
        out_shape=jax.ShapeDtypeStruct((seq, D), BF16),
        in_specs=[pl.BlockSpec(memory_space=pltpu.SMEM), cur(N_HEADS * LANES), prev(KX_W), cur(KX_W), prev(KX_W), cur(KX_W)],
        out_specs=cur(D),
        compiler_params=_cparams(1),
    )(sinks, qs, kd, kd, vd, vd)


def _attn_out_proj(x, o, g, w, j, mod, layer, tile):
    seq = x.shape[0]

    def body(x_ref, o_ref, g_ref, w_ref, mod_ref, xo_ref, br_ref):
        gv = g_ref[...].astype(F32)
        u = (o_ref[...].astype(F32) * (gv * _sigmoid(gv))).astype(BF16)
        br = _dot(u, w_ref[...])
        br_ref[...] = br.astype(BF16)
        xo_ref[...] = x_ref[...] + mod_ref[2:3, :] * br

    row = pl.BlockSpec((tile, D), lambda i: (i, 0))
    return _call(
        body, name=f"attn_out_proj_{j}", grid=(seq // tile,),
        out_shape=(jax.ShapeDtypeStruct((seq, D), F32), jax.ShapeDtypeStruct((seq, D), BF16)),
        in_specs=[row, row, row, _const_spec((D, D)), _mod_spec(layer)],
        out_specs=(row, row),
        compiler_params=_cparams(1),
    )(x, o, g, w, mod)


def _attn_out_proj_bwd(dxn, br, o, g, w, j, mod, layer, tile, after):
    seq = dxn.shape[0]
    steps = seq // tile

    def body(dxn_ref, br_ref, o_ref, g_ref, w_ref, mod_ref, after_ref, do_ref, dg_ref, dw_ref, dgate_ref, dw_acc):
        i = pl.program_id(0)

        @pl.when(i == 0)
        def _():
            dw_acc[...] = jnp.zeros_like(dw_acc)
            dgate_ref[...] = jnp.zeros_like(dgate_ref)

        dxn_v, ov, gv = dxn_ref[...], o_ref[...].astype(F32), g_ref[...].astype(F32)
        dgate_ref[...] += jnp.sum(dxn_v * br_ref[...].astype(F32), axis=0, keepdims=True)
        dbr = (dxn_v * mod_ref[2:3, :]).astype(BF16)
        du = _dot_nt(dbr, w_ref[...])
        sg = _sigmoid(gv)
        sl = gv * sg
        dw_acc[...] += _dot_tn((ov * sl).astype(BF16), dbr)
        do = du * sl
        dg_ref[...] = (du * ov * (sg * (1.0 + gv * (1.0 - sg)))).astype(BF16)
        low = _low_half(tile)
        for b in range(D // LANES):
            blk = do[:, LANES * b:LANES * (b + 1)]
            do_ref[:, 2 * LANES * b:2 * LANES * b + LANES] = jnp.where(low, blk, 0.0).astype(BF16)
            do_ref[:, 2 * LANES * b + LANES:2 * LANES * (b + 1)] = jnp.where(low, 0.0, blk).astype(BF16)

        @pl.when(i == steps - 1)
        def _():
            dw_ref[...] = dw_acc[...].astype(BF16)

    row = lambda w_: pl.BlockSpec((tile, w_), lambda i: (i, 0))
    return _call(
        body, name=f"attn_out_proj_bwd_{j}", grid=(steps,),
        out_shape=(jax.ShapeDtypeStruct((seq, N_HEADS * LANES), BF16), jax.ShapeDtypeStruct((seq, D), BF16),
                   jax.ShapeDtypeStruct((D, D), BF16), jax.ShapeDtypeStruct((1, D), F32)),
        in_specs=[row(D), row(D), row(D), row(D), _const_spec((D, D)), _mod_spec(layer), ANY_SPEC],
        out_specs=(row(N_HEADS * LANES), row(D), pl.BlockSpec((D, D), lambda i: (0, 0)),
                   pl.BlockSpec((1, D), lambda i: (0, 0))),
        scratch_shapes=[pltpu.VMEM((D, D), F32)],
        compiler_params=_cparams(1),
    )(dxn, br, o, g, w, mod, after)


def _attn_backward(sinks, qs, dos, kd, vd, j, after):
    seq = qs.shape[0]
    per = 2
    nb = seq // (per * QBLK)
    kw = N_KV * HEAD_DIM

    def body(sink_ref, q_ref, do_ref, kp_ref, kc_ref, vp_ref, vc_ref, after_ref,
             dq_ref, dk_even, dk_odd, dv_even, dv_odd, dsink_ref, carry_k, carry_v, sink_acc):
        n = pl.program_id(0)

        @pl.when(n == 0)
        def _():
            carry_k[...] = jnp.zeros_like(carry_k)
            carry_v[...] = jnp.zeros_like(carry_v)
            sink_acc[...] = jnp.zeros_like(sink_acc)

        @pl.when(n < nb)
        def _():
            low = _low_half(QBLK)
            lane_q = lax.broadcasted_iota(jnp.int32, (1, 2 * QBLK), 1)
            groups = per * 2 * N_KV

            def keys(p_ref, c_ref, blk, kv):
                cols = slice(LANES * kv, LANES * (kv + 1))
                if blk == 0:
                    return jnp.concatenate([p_ref[:, cols], c_ref[0:QBLK, cols]], axis=0)
                return c_ref[QBLK * (blk - 1):QBLK * (blk + 1), cols]

            def first_products(g):
                blk, rest = divmod(g, 2 * N_KV)
                kv, half = divmod(rest, 2)
                heads = (4 * kv + half, 4 * kv + 2 + half)
                rows = slice(QBLK * blk, QBLK * (blk + 1))
                q = jnp.concatenate([q_ref[rows, LANES * h:LANES * (h + 1)] for h in heads], axis=0)
                do = jnp.concatenate([do_ref[rows, LANES * h:LANES * (h + 1)] for h in heads], axis=0)
                kk = keys(kp_ref, kc_ref, blk, kv)
                return heads, q, do, kk, _dot_nt(kk, q), _dot_nt(keys(vp_ref, vc_ref, blk, kv), do)

            nxt = first_products(0)
            dk_parts, dv_parts = [[], []], [[], []]
            dq_h, dk_kv, dv_kv = [], None, None
            for g in range(groups):
                blk, rest = divmod(g, 2 * N_KV)
                heads, q, do, kk, s_raw, dp_raw = nxt
                if g + 1 < groups:
                    nxt = first_products(g + 1)
                ok = _band_mask(n if blk == 0 else 1, 2 * QBLK, True)
                st = jnp.where(ok, s_raw, -1e30)
                sink = jnp.where(lane_q < QBLK, sink_ref[j, heads[0]], sink_ref[j, heads[1]])
                m = jnp.maximum(jnp.max(st, axis=0, keepdims=True), sink)
                e = jnp.exp(st - m)
                e_sink = jnp.exp(sink - m)
                inv = 1.0 / (jnp.sum(e, axis=0, keepdims=True) + e_sink)
                p = e * inv
                pdp = p * dp_raw
                delta = jnp.sum(pdp, axis=0, keepdims=True)
                ds = (pdp - p * delta).astype(BF16)
                sink_acc[rest:rest + 1, :] -= e_sink * inv * delta
                dk_g, dv_g = _dot(ds, q), _dot(p.astype(BF16), do)
                dk_kv = dk_g if dk_kv is None else dk_kv + dk_g
                dv_kv = dv_g if dv_kv is None else dv_kv + dv_g
                dq_h.append(_dot_tn(ds, kk))
                if g % 2 == 1:
                    kv = rest // 2
                    for t in range(2):
                        dq_ref[QBLK * blk:QBLK * (blk + 1), LANES * (2 * kv + t):LANES * (2 * kv + t + 1)] = jnp.where(
                            low, dq_h[0][QBLK * t:QBLK * (t + 1)], dq_h[1][QBLK * t:QBLK * (t + 1)])
                    dk_parts[blk].append(dk_kv + pltpu.roll(dk_kv, HEAD_DIM, 1))
                    dv_parts[blk].append(dv_kv + pltpu.roll(dv_kv, HEAD_DIM, 1))
                    dq_h, dk_kv, dv_kv = [], None, None

            def order(parts, lo, hi):
                return jnp.concatenate([jnp.where(low, parts[0][lo:hi], parts[1][lo:hi]),
                                        jnp.where(low, parts[2][lo:hi], parts[3][lo:hi])], axis=1)

            dk_odd[...] = carry_k[...] + order(dk_parts[0], 0, QBLK)
            dv_odd[...] = (carry_v[...] + order(dv_parts[0], 0, QBLK)).astype(BF16)
            dk_even[...] = order(dk_parts[0], QBLK, 2 * QBLK) + order(dk_parts[1], 0, QBLK)
            dv_even[...] = (order(dv_parts[0], QBLK, 2 * QBLK) + order(dv_parts[1], 0, QBLK)).astype(BF16)
            carry_k[...] = order(dk_parts[1], QBLK, 2 * QBLK)
            carry_v[...] = order(dv_parts[1], QBLK, 2 * QBLK)

        @pl.when(n == nb)
        def _():
            dk_odd[...] = carry_k[...]
            dv_odd[...] = carry_v[...].astype(BF16)
            lane = lax.broadcasted_iota(jnp.int32, (1, LANES), 1)
            out = jnp.zeros((1, LANES), F32)
            for g in range(2 * N_KV):
                for t in range(2):
                    tot = jnp.sum(sink_acc[g:g + 1, QBLK * t:QBLK * (t + 1)], axis=1, keepdims=True)
                    out = jnp.where(lane == 4 * (g // 2) + 2 * t + g % 2, tot, out)
            dsink_ref[...] = out

    cur = lambda w: pl.BlockSpec((per * QBLK, w), lambda n: (jnp.minimum(n, nb - 1), 0))
    prev = lambda w: pl.BlockSpec((QBLK, w), lambda n: (jnp.maximum(per * n - 1, 0), 0))
    even = pl.BlockSpec((None, QBLK, kw), lambda n: (jnp.minimum(n, nb - 1), 0, 0))
    odd = pl.BlockSpec((None, QBLK, kw), lambda n: (jnp.maximum(n - 1, 0), 0, 0))
    halves = lambda dt: jax.ShapeDtypeStruct((nb, QBLK, kw), dt)
    dq, dk_e, dk_o, dv_e, dv_o, dsink = _call(
        body, name=f"attn_backward_{j}", grid=(nb + 1,),
        out_shape=(jax.ShapeDtypeStruct((seq, D), F32), halves(F32), halves(F32), halves(BF16), halves(BF16),
                   jax.ShapeDtypeStruct((1, LANES), F32)),
        in_specs=[pl.BlockSpec(memory_space=pltpu.SMEM), cur(N_HEADS * LANES), cur(N_HEADS * LANES), prev(KX_W), cur(KX_W),
                  prev(KX_W), cur(KX_W), ANY_SPEC],
        out_specs=(cur(D), even, odd, even, odd, pl.BlockSpec((1, LANES), lambda n: (0, 0))),
        scratch_shapes=[pltpu.VMEM((QBLK, kw), F32), pltpu.VMEM((QBLK, kw), F32), pltpu.VMEM((2 * N_KV, 2 * QBLK), F32)],
        compiler_params=_cparams(1),
    )(sinks, qs, dos, kd, kd, vd, vd, after)
    return dq, (dk_e, dk_o), (dv_e, dv_o), dsink


def _in_proj_tail(x_ref, dxn_ref, ng_ref, mod_ref, w_ref, dproj, dx_ref, dw_acc, vec_acc):
    ng, sc, sh = ng_ref[...], mod_ref[1:2, :], mod_ref[0:1, :]
    xh, r, h = _norm_mod(x_ref[...], ng, sc, sh)
    dh = _dot(dproj, w_ref[...])
    dw_acc[...] += _dot_tn(dproj, h.astype(BF16))
    vec_acc[0:1, :] += jnp.sum(dh, axis=0, keepdims=True)
    vec_acc[1:2, :] += jnp.sum(dh * xh, axis=0, keepdims=True)
    dxh = dh * (ng * (1.0 + sc))
    dx_ref[...] = dxn_ref[...] + r * (dxh - xh * jnp.mean(dxh * xh, axis=-1, keepdims=True))


def _tail_finish(ng_ref, mod_ref, dw_ref, vec_ref, dw_acc, vec_acc):
    dw_ref[...] = dw_acc[...].astype(BF16)
    a = vec_acc[1:2, :]
    vec_ref[...] = jnp.zeros_like(vec_ref)
    vec_ref[0:1, :] = vec_acc[0:1, :]
    vec_ref[1:2, :] = a * ng_ref[...]
    vec_ref[3:4, :] = a * (1.0 + mod_ref[1:2, :])


def _attn_in_proj_bwd(x, dxn, rope, qk_raw, dq, dk, dv, dg, rows, mod, layer, w_t, j, gain, bd, tile):
    seq = x.shape[0]
    steps = seq // tile

    assert tile == 2 * QBLK
    (dk_e, dk_o), (dv_e, dv_o) = dk, dv

    def body(x_ref, dxn_ref, rope_ref, qk_ref, dq_ref, dke_ref, dko_ref, dve_ref, dvo_ref, dg_ref, ng_ref, mod_ref, w_ref, gain_ref,
             bd_ref, dx_ref, dw_ref, vec_ref, dgain_ref, dproj, dw_acc, vec_acc):
        i = pl.program_id(0)

        @pl.when(i == 0)
        def _():
            dw_acc[...] = jnp.zeros_like(dw_acc)
            vec_acc[...] = jnp.zeros_like(vec_acc)
            dgain_ref[...] = jnp.zeros_like(dgain_ref)

        tabs = _rope_tabs(rope_ref)
        bdm = bd_ref[...]
        for b in range(QK_W // LANES):
            cols = slice(LANES * b, LANES * (b + 1))
            raw = qk_ref[:, cols]
            if b < D // LANES:
                dy = dq_ref[:, cols] * (HEAD_DIM ** -0.5)
            else:
                kcols = slice(LANES * (b - D // LANES), LANES * (b + 1 - D // LANES))
                dy = jnp.concatenate([dke_ref[:, kcols], dko_ref[:, kcols]], axis=0)
            dy = _rope_bwd(dy, tabs)
            rr = lax.rsqrt(_group_mean(raw * raw, bdm) + NORM_EPS)
            xh = raw * rr
            dgain_ref[:, cols] += jnp.sum(dy * xh, axis=0, keepdims=True)
            dxh = dy * gain_ref[:, cols]
            dproj[:, cols] = (rr * (dxh - xh * _group_mean(dxh * xh, bdm))).astype(BF16)
        dproj[0:QBLK, QK_W:QK_W + N_KV * HEAD_DIM] = dve_ref[...]
        dproj[QBLK:2 * QBLK, QK_W:QK_W + N_KV * HEAD_DIM] = dvo_ref[...]
        dproj[:, QK_W + N_KV * HEAD_DIM:] = dg_ref[...]
        _in_proj_tail(x_ref, dxn_ref, ng_ref, mod_ref, w_ref, dproj[...], dx_ref, dw_acc, vec_acc)

        @pl.when(i == steps - 1)
        def _():
            _tail_finish(ng_ref, mod_ref, dw_ref, vec_ref, dw_acc, vec_acc)

    row = lambda w: pl.BlockSpec((tile, w), lambda i: (i, 0))
    kblock = pl.BlockSpec((None, QBLK, N_KV * HEAD_DIM), lambda i: (i, 0, 0))
    fixed = lambda shape: pl.BlockSpec(shape, lambda i: (0,) * len(shape))
    return _call(
        body, name=f"attn_in_proj_bwd_{j}", grid=(steps,),
        out_shape=(jax.ShapeDtypeStruct((seq, D), F32), jax.ShapeDtypeStruct((ATTN_IN, D), BF16),
                   jax.ShapeDtypeStruct((8, D), F32), jax.ShapeDtypeStruct((1, QK_W), F32)),
        in_specs=[row(D), row(D), row(3 * LANES), row(QK_W), row(D), kblock, kblock, kblock, kblock, row(D),
                  _mod_row_spec(layer, NORM_ROW), _mod_spec(layer), _const_spec((ATTN_IN, D)), _const_spec((1, QK_W)),
                  _const_spec((LANES, LANES))],
        out_specs=(row(D), fixed((ATTN_IN, D)), fixed((8, D)), fixed((1, QK_W))),
        scratch_shapes=[pltpu.VMEM((tile, ATTN_IN), BF16), pltpu.VMEM((ATTN_IN, D), F32), pltpu.VMEM((8, D), F32)],
        compiler_params=_cparams(1),
    )(x, dxn, rope, qk_raw, dq, dk_e, dk_o, dv_e, dv_o, dg, rows, mod, w_t, gain, bd)


def _pool_in_proj(x, rows, mod, layer, w_t, j, tile):
    seq = x.shape[0]

    def body(x_ref, ng_ref, mod_ref, w_ref, v_ref, g_ref):
        _, _, h = _norm_mod(x_ref[...], ng_ref[...], mod_ref[1:2, :], mod_ref[0:1, :])
        proj = _dot_nt(h.astype(BF16), w_ref[...])
        v_ref[...] = proj[:, :D].astype(BF16)
        g_ref[...] = proj[:, D:].astype(BF16)

    row = pl.BlockSpec((tile, D), lambda i: (i, 0))
    return _call(
        body, name=f"pool_in_proj_{j}", grid=(seq // tile,),
        out_shape=(jax.ShapeDtypeStruct((seq, D), BF16), jax.ShapeDtypeStruct((seq, D), BF16)),
        in_specs=[row, _mod_row_spec(layer, NORM_ROW), _mod_spec(layer), _const_spec((POOL_IN, D))],
        out_specs=(row, row),
        compiler_params=_cparams(1),
    )(x, rows, mod, w_t)


PAD = 8


def _window_sums(ext, lo, hi, forward):
    gw = D // len(POOL_WINDOWS)
    planes = []
    for gi, w in enumerate(POOL_WINDOWS):
        cols = slice(gw * gi, gw * (gi + 1))
        src, k = 0, 1
        while k < w:
            d = k if forward else -k
            ext[1 - src, lo:hi, cols] = ext[src, lo:hi, cols] + ext[src, lo + d:hi + d, cols]
            src, k = 1 - src, 2 * k
        planes.append(src)
    return planes


def _pooled(ext, v_ref, first, tile):
    t_abs = first + lax.broadcasted_iota(jnp.int32, (tile, 1), 0)
    top = PAD + HALO
    planes = _window_sums(ext, PAD, top + tile, False)
    outs = []
    gw = D // len(POOL_WINDOWS)
    for gi, w in enumerate(POOL_WINDOWS):
        cols = slice(gw * gi, gw * (gi + 1))
        cnt = jnp.minimum(t_abs + 1, w).astype(F32)
        outs.append(ext[planes[gi], top:top + tile, cols] / cnt - v_ref[:, cols].astype(F32))
    return jnp.concatenate(outs, axis=1)


def _fill_ext(ext, halo_ref, v_ref, i, tile):
    ext[0, 0:PAD, :] = jnp.zeros((PAD, D), F32)
    ext[1, 0:PAD, :] = jnp.zeros((PAD, D), F32)
    ext[0, PAD:PAD + HALO, :] = jnp.where(i == 0, 0.0, halo_ref[...].astype(F32))
    ext[0, PAD + HALO:PAD + HALO + tile, :] = v_ref[...].astype(F32)


def _group_mix(pb, wg_ref):
    gw = D // len(POOL_WINDOWS)
    return jnp.concatenate([_dot(pb[:, gw * gi:gw * (gi + 1)], wg_ref[gi]) for gi in range(len(POOL_WINDOWS))], axis=1)


def _pool_mix_out(x, v, g, wg, w_out, j, rows, mod, layer, tile, target=None):
    seq = x.shape[0]

    def body(*refs):
        if target is None:
            x_ref, v_ref, halo_ref, g_ref, wg_ref, w_ref, scale_ref, mod_ref, xo_ref, br_ref, ext = refs
        else:
            x_ref, v_ref, halo_ref, g_ref, wg_ref, w_ref, scale_ref, mod_ref, t_ref, xo_ref, br_ref, loss_ref, ext = refs
        i = pl.program_id(0)
        _fill_ext(ext, halo_ref, v_ref, i, tile)
        pb = _pooled(ext, v_ref, i * tile, tile).astype(BF16)
        ms = _group_mix(pb, wg_ref) * scale_ref[...]
        gv = g_ref[...].astype(F32)
        u = (ms * (gv * _sigmoid(gv))).astype(BF16)
        br = _dot(u, w_ref[...])
        br_ref[...] = br.astype(BF16)
        y = x_ref[...] + mod_ref[2:3, :] * br
        if target is None:
            xo_ref[...] = y
        else:
            @pl.when(i == 0)
            def _():
                loss_ref[...] = jnp.zeros_like(loss_ref)

            e = y - t_ref[...]
            xo_ref[...] = e * (1.0 / D)
            loss_ref[...] += 0.5 * jnp.sum(jnp.mean(e * e, axis=-1, keepdims=True), axis=0, keepdims=True)

    row = pl.BlockSpec((tile, D), lambda i: (i, 0))
    halo = pl.BlockSpec((HALO, D), lambda i: (jnp.maximum(i * (tile // HALO) - 1, 0), 0))
    extra_in, extra_out, extra_shape = ([], (), ()) if target is None else (
        [row], (pl.BlockSpec((1, LANES), lambda i: (0, 0)),), (jax.ShapeDtypeStruct((1, LANES), F32),))
    return _call(
        body, name=f"pool_mix_out_{j}", grid=(seq // tile,),
        out_shape=(jax.ShapeDtypeStruct((seq, D), F32), jax.ShapeDtypeStruct((seq, D), BF16)) + extra_shape,
        in_specs=[row, row, halo, row, _const_spec(wg.shape), _const_spec((D, D)), _mod_row_spec(layer, POOL_SCALE_ROW),
                  _mod_spec(layer)] + extra_in,
        out_specs=(row, row) + extra_out,
        scratch_shapes=[pltpu.VMEM((2, tile + HALO + PAD, D), F32)],
        compiler_params=_cparams(1),
    )(x, v, v, g, wg, w_out, rows, mod, *(() if target is None else (target,)))


def _pool_mix_out_bwd(dxn, br, v, g, wg, w_out, j, rows, mod, layer, tile, after):
    seq = dxn.shape[0]
    steps = seq // tile
    ng_ = len(POOL_WINDOWS)
    gw = D // ng_

    def body(dxn_ref, br_ref, v_ref, halo_ref, g_ref, wg_ref, w_ref, scale_ref, mod_ref, after_ref,
             dpool_ref, dg_ref, dw_ref, dwg_ref, vec_ref, ext, dw_acc, dwg_acc):
        i = pl.program_id(0)

        @pl.when(i == 0)
        def _():
            dw_acc[...] = jnp.zeros_like(dw_acc)
            dwg_acc[...] = jnp.zeros_like(dwg_acc)
            vec_ref[...] = jnp.zeros_like(vec_ref)

        _fill_ext(ext, halo_ref, v_ref, i, tile)
        pb = _pooled(ext, v_ref, i * tile, tile).astype(BF16)
        mixed = _group_mix(pb, wg_ref)
        scale = scale_ref[...]
        ms = mixed * scale
        gv, dxn_v = g_ref[...].astype(F32), dxn_ref[...]
        sg = _sigmoid(gv)
        sl = gv * sg
        vec_ref[0:1, :] += jnp.sum(dxn_v * br_ref[...].astype(F32), axis=0, keepdims=True)
        dbr = (dxn_v * mod_ref[2:3, :]).astype(BF16)
        du = _dot_nt(dbr, w_ref[...])
        dw_acc[...] += _dot_tn((ms * sl).astype(BF16), dbr)
        dms = du * sl
        dg_ref[...] = (du * ms * (sg * (1.0 + gv * (1.0 - sg)))).astype(BF16)
        vec_ref[1:2, :] += jnp.sum(dms * mixed, axis=0, keepdims=True)
        dmx = (dms * scale).astype(BF16)
        for gi in range(ng_):
            cols = slice(gw * gi, gw * (gi + 1))
            dpool_ref[:, cols] = _dot_nt(dmx[:, cols], wg_ref[gi])
            dwg_acc[gi] += _dot_tn(pb[:, cols], dmx[:, cols])

        @pl.when(i == steps - 1)
        def _():
            dw_ref[...] = dw_acc[...].astype(BF16)
            dwg_ref[...] = dwg_acc[...].astype(BF16)

    row = pl.BlockSpec((tile, D), lambda i: (i, 0))
    halo = pl.BlockSpec((HALO, D), lambda i: (jnp.maximum(i * (tile // HALO) - 1, 0), 0))
    fixed = lambda shape: pl.BlockSpec(shape, lambda i: (0,) * len(shape))
    return _call(
        body, name=f"pool_mix_out_bwd_{j}", grid=(steps,),
        out_shape=(jax.ShapeDtypeStruct((seq, D), F32), jax.ShapeDtypeStruct((seq, D), BF16),
                   jax.ShapeDtypeStruct((D, D), BF16), jax.ShapeDtypeStruct((ng_, gw, gw), BF16),
                   jax.ShapeDtypeStruct((8, D), F32)),
        in_specs=[row, row, row, halo, row, _const_spec(wg.shape), _const_spec((D, D)), _mod_row_spec(layer, POOL_SCALE_ROW),
                  _mod_spec(layer), ANY_SPEC],
        out_specs=(row, row, fixed((D, D)), fixed((ng_, gw, gw)), fixed((8, D))),
        scratch_shapes=[pltpu.VMEM((2, tile + HALO + PAD, D), F32), pltpu.VMEM((D, D), F32), pltpu.VMEM((ng_, gw, gw), F32)],
        compiler_params=_cparams(1),
    )(dxn, br, v, v, g, wg, w_out, rows, mod, after)


def _pool_in_proj_bwd(x, dxn, dpool, dg, rows, mod, layer, w_t, j, tile, after):
    seq = x.shape[0]
    steps = seq // tile
    gw = D // len(POOL_WINDOWS)

    def body(x_ref, dxn_ref, dp_ref, halo_ref, dg_ref, ng_ref, mod_ref, w_ref, after_ref, dx_ref, dw_ref, vec_ref,
             ext, dproj, dw_acc, vec_acc):
        i = pl.program_id(0)

        @pl.when(i == 0)
        def _():
            dw_acc[...] = jnp.zeros_like(dw_acc)
            vec_acc[...] = jnp.zeros_like(vec_acc)

        t_abs = i * tile + lax.broadcasted_iota(jnp.int32, (tile, 1), 0)
        last = i == steps - 1
        ext[0, tile + HALO:tile + HALO + PAD, :] = jnp.zeros((PAD, D), F32)
        ext[1, tile + HALO:tile + HALO + PAD, :] = jnp.zeros((PAD, D), F32)
        for gi, w in enumerate(POOL_WINDOWS):
            cols = slice(gw * gi, gw * (gi + 1))
            cnt = jnp.minimum(t_abs + 1, w).astype(F32)
            ext[0, 0:tile, cols] = dp_ref[:, cols] / cnt
            ext[0, tile:tile + HALO, cols] = jnp.where(last, 0.0, halo_ref[:, cols] * (1.0 / w))
        planes = _window_sums(ext, 0, tile + HALO, True)
        for gi, w in enumerate(POOL_WINDOWS):
            cols = slice(gw * gi, gw * (gi + 1))
            dproj[:, cols] = (ext[planes[gi], 0:tile, cols] - dp_ref[:, cols]).astype(BF16)
        dproj[:, D:] = dg_ref[...]
        _in_proj_tail(x_ref, dxn_ref, ng_ref, mod_ref, w_ref, dproj[...], dx_ref, dw_acc, vec_acc)

        @pl.when(last)
        def _():
            _tail_finish(ng_ref, mod_ref, dw_ref, vec_ref, dw_acc, vec_acc)

    row = pl.BlockSpec((tile, D), lambda i: (i, 0))
    halo = pl.BlockSpec((HALO, D), lambda i: (jnp.minimum((i + 1) * (tile // HALO), seq // HALO - 1), 0))
    fixed = lambda shape: pl.BlockSpec(shape, lambda i: (0,) * len(shape))
    return _call(
        body, name=f"pool_in_proj_bwd_{j}", grid=(steps,),
        out_shape=(jax.ShapeDtypeStruct((seq, D), F32), jax.ShapeDtypeStruct((POOL_IN, D), BF16),
                   jax.ShapeDtypeStruct((8, D), F32)),
        in_specs=[row, row, row, halo, row, _mod_row_spec(layer, NORM_ROW), _mod_spec(layer), _const_spec((POOL_IN, D)), ANY_SPEC],
        out_specs=(row, fixed((POOL_IN, D)), fixed((8, D))),
        scratch_shapes=[pltpu.VMEM((2, tile + HALO + PAD, D), F32), pltpu.VMEM((tile, POOL_IN), BF16), pltpu.VMEM((POOL_IN, D), F32),
                        pltpu.VMEM((8, D), F32)],
        compiler_params=_cparams(1),
    )(x, dxn, dpool, dpool, dg, rows, mod, w_t, after)


def _build_vec(vecs, gates, pool_vecs, gains, dsinks, loss_part):
    def body(v0, v1, v2, v3, g0, g2, p0, p1, n0, n1, s0, s1, loss_ref, out):
        out[...] = jnp.zeros_like(out)
        for i, v in enumerate((v0, v1, v2, v3)):
            out[3 * i:3 * i + 2, :] = v[0:2, :]
            out[12 + i:13 + i, :] = v[3:4, :]
        out[2:3, :] = g0[...]
        out[8:9, :] = g2[...]
        for j, (p, n, s) in enumerate(((p0, n0, s0), (p1, n1, s1))):
            out[3 * (2 * j + 1) + 2:3 * (2 * j + 1) + 3, :] = p[0:1, :]
            out[22 + j:23 + j, :] = p[1:2, :]
            out[16 + j:17 + j, :] = n[:, 0:D]
            out[18 + j:19 + j, 0:QK_W - D] = n[:, D:QK_W]
            out[20 + j:21 + j, 0:LANES] = s[...]
        out[24:25, 0:LANES] = loss_ref[...]

    vm = pl.BlockSpec(memory_space=pltpu.VMEM)
    args = (*vecs, gates[0], gates[2], *pool_vecs, *gains, *dsinks, loss_part)
    return _call(
        body, name="build_vec",
        out_shape=jax.ShapeDtypeStruct((VEC_ROWS, D), F32),
        in_specs=[vm] * len(args), out_specs=vm,
        compiler_params=_cparams(),
    )(*args)


def _sum_devices(g, after):
    rows = g.shape[1]

    def body(g_ref, after_ref, tot_ref, fold_ref):
        tot = g_ref[0]
        for p in range(1, N_DEV):
            tot = tot + g_ref[p]
        tot_ref[...] = tot
        f = tot[16:24, 0:LANES]
        for b in range(1, D // LANES):
            f = f + tot[16:24, LANES * b:LANES * (b + 1)]
        fold_ref[...] = f + pltpu.roll(f, HEAD_DIM, 1)

    return _call(
        body, name="sum_devices",
        out_shape=(jax.ShapeDtypeStruct((rows, D), F32), jax.ShapeDtypeStruct((8, LANES), F32)),
        in_specs=[pl.BlockSpec(memory_space=pltpu.VMEM), ANY_SPEC],
        out_specs=(pl.BlockSpec(memory_space=pltpu.VMEM), pl.BlockSpec(memory_space=pltpu.VMEM)),
        compiler_params=_cparams(),
    )(g, after)


def _adamw_small(params):
    n = len(params)

    def body(*refs):
        ins, outs = refs[:4 * n], refs[4 * n:]
        for p in range(n):
            w_ref, g_ref, m_ref, v_ref = ins[4 * p:4 * p + 4]
            outs[3 * p][...], outs[3 * p + 1][...], outs[3 * p + 2][...] = _adamw(w_ref[...], g_ref[...], m_ref[...], v_ref[...])

    vm = pl.BlockSpec(memory_space=pltpu.VMEM)
    out = _call(
        body, name="adamw_small",
        out_shape=tuple(jax.ShapeDtypeStruct(w.shape, F32) for (w, _, _, _) in params for _ in range(3)),
        in_specs=[vm] * (4 * n), out_specs=tuple([vm] * (3 * n)),
        compiler_params=_cparams(),
    )(*[a for p in params for a in p])
    return [tuple(out[3 * p:3 * p + 3]) for p in range(n)]


def _adamw_shards(name, me, fulls, lands, w, m, v, transpose, axis=0):
    nl = w.shape[0]
    wshape = w.shape[1:]
    own_shape = lands[0].shape[1:]

    def body(me_ref, *refs):
        own_refs, land_refs = refs[:nl], refs[nl:2 * nl]
        w_ref, m_ref, v_ref, g_out, d_out, m_out, v_out = refs[2 * nl:]
        layer = pl.program_id(0)
        for l in range(nl):
            @pl.when(layer == l)
            def _(l=l):
                g = own_refs[l][...].astype(F32)
                for k in range(N_DEV - 1):
                    g = g + land_refs[l][k].astype(F32)
                if transpose:
                    g = g.T
                g_out[...] = g
                d_out[...], m_out[...], v_out[...] = _adamw(w_ref[...], g, m_ref[...], v_ref[...])

    def own_index(l_, me_ref):
        idx = [0] * len(own_shape)
        idx[axis] = me_ref[0]
        return tuple(idx)

    own_spec = pl.BlockSpec(tuple(own_shape), own_index)
    land_spec = pl.BlockSpec((N_DEV - 1,) + tuple(own_shape), lambda l_, me_ref: (0,) * (1 + len(own_shape)))
    wspec = pl.BlockSpec((None,) + tuple(wshape), lambda l_, me_ref: (l_,) + (0,) * len(wshape))
    return _call(
        body, name=name,
        grid_spec=pltpu.PrefetchScalarGridSpec(num_scalar_prefetch=1, grid=(nl,),
                                               in_specs=[own_spec] * nl + [land_spec] * nl + [wspec] * 3,
                                               out_specs=(wspec,) * 4),
        out_shape=tuple(jax.ShapeDtypeStruct(w.shape, F32) for _ in range(4)),
        compiler_params=_cparams(1),
    )(me.reshape(1), *fulls, *lands, w, m, v)


def _constants():
    lane = np.arange(LANES)
    bd = (lane[:, None] // HEAD_DIM == lane[None, :] // HEAD_DIM).astype(np.float32)
    half = ROT_DIM // 2
    inv_freq = ROPE_THETA ** (-jnp.arange(half, dtype=F32) * 2.0 / ROT_DIM)
    invf = jnp.tile(inv_freq, LANES // half).reshape(1, LANES)
    return jnp.asarray(bd, BF16), invf


def kernel(x, c, positions, ada_w, ada_b, norm_g, attn_w_in, attn_q_norm, attn_k_norm, attn_sinks, attn_w_out, pool_w_in, pool_w_group, pool_scale, pool_w_out, loss_target, m_ada_w, m_ada_b, m_norm_g, m_attn_w_in, m_attn_q_norm, m_attn_k_norm, m_attn_sinks, m_attn_w_out, m_pool_w_in, m_pool_w_group, m_pool_scale, m_pool_w_out, v_ada_w, v_ada_b, v_norm_g, v_attn_w_in, v_attn_q_norm, v_attn_k_norm, v_attn_sinks, v_attn_w_out, v_pool_w_in, v_pool_w_group, v_pool_scale, v_pool_w_out):
    seq = x.shape[1]
    me = 4 * lax.axis_index("x") + 2 * lax.axis_index("y") + lax.axis_index("c")
    bd, invf = _constants()
    t_mm = min(512, seq)
    t_bw = min(256, seq)
    shard = pool_scale.shape[1]
    cols = ada_w.shape[2]

    w_in_rows = jnp.swapaxes(attn_w_in, 1, 2)
    first = jnp.concatenate([c, jnp.pad(pool_scale, ((0, 0), (0, D - shard))), jnp.zeros((5, D), F32)], axis=0)
    first_rows = lax.dynamic_update_slice(jnp.zeros((N_DEV * 8, D), F32), first, (me * 8, 0))
    c_sent, token = _gather_start([[first_rows]], [(0,)], c, "c_gather_start")
    w_first, = _prep_weights(me, [(w_in_rows, 0, "N")], "prep_first", after=(token,))
    first_w, token = _gather_first_start(w_first, token)
    prepped = _prep_weights(me, [(attn_w_out, 0, "N"), (pool_w_in, 0, "T"), (pool_w_out, 0, "N"), (pool_w_group, 0, "G"),
                                 (w_in_rows, 1, "N"), (attn_w_out, 1, "N"), (pool_w_in, 1, "T"), (pool_w_out, 1, "N"),
                                 (pool_w_group, 1, "G")], "prep_rest", after=(token,))
    first, = _gather_wait(c_sent[0], (0,), prepped[0], "c_gather_wait")
    first = first.reshape(N_DEV, 8, D)
    c_all = first[:, 0, :]
    scale_full = jnp.transpose(first[:, 1:3, :shard], (1, 0, 2)).reshape(2, D)
    mod_part = _ada_forward(c_all, ada_w)
    mod_rows = lax.dynamic_update_slice(jnp.zeros((N_DEV * DEPTH * N_DEV, cols), F32),
                                        mod_part.reshape(DEPTH * N_DEV, cols), (me * DEPTH * N_DEV, 0))
    mod_sent, token = _gather_start([[mod_rows]], [(0,)], token, "mod_gather_start")
    rope = _rope_table(positions.reshape(seq, 1), invf + token[0:1, 0:1], t_mm)
    mod_all, = _gather_wait(mod_sent[0], (0,), rope, "mod_gather_wait")
    mod_all = mod_all.reshape(N_DEV, DEPTH, N_DEV, cols)
    mine = lax.dynamic_index_in_dim(mod_all, me, axis=2, keepdims=False)
    mod = jnp.transpose(mine, (1, 0, 2)).reshape(DEPTH, 3 * D) + ada_b
    pool_rows = jnp.stack([jnp.zeros_like(scale_full[0]), scale_full[0], jnp.zeros_like(scale_full[0]), scale_full[1]])
    mod = jnp.concatenate([mod.reshape(DEPTH, 3, D), norm_g[:, None, :], pool_rows[:, None, :],
                           jnp.zeros((DEPTH, 3, D), F32)], axis=1)
    rows = mod.reshape(DEPTH, 8, 1, D)

    groups = [prepped[0:1], prepped[1:4], prepped[4:6], prepped[6:9]]
    gaxes = [(0,), (0,), (0, 0, 1), (0, 0), (0, 0, 1)]
    first_w, token = _gather_first_forward(first_w, mod)
    rest, token = _gather_start(groups, gaxes[1:], token, "gather_start_rest")
    started = [None] + rest

    saved, weights = [], []
    h = x[0]
    for i in range(DEPTH):
        j = i // 2
        s = dict(x=h)
        if i == 0:
            w_in_t = _gather_first_wait(first_w, token)
        else:
            wts = _gather_wait(started[i + 1], gaxes[i + 1], h, f"gather_wait_{i}")
        if i % 2 == 0:
            if i > 0:
                w_in_t, w_out = wts
            s["gain"] = jnp.concatenate([jnp.tile(attn_q_norm[j], N_HEADS), jnp.tile(attn_k_norm[j], N_KV)]).reshape(1, QK_W)
            s["qk_raw"], s["qs"], s["kd"], s["vd"], s["g"] = _attn_in_proj(
                h, rope, rows, mod, i, w_in_t, j, s["gain"], bd, t_mm)
            s["o"] = _attn_forward(attn_sinks, s["qs"], s["kd"], s["vd"], j)
            if i == 0:
                w_out, = _gather_wait(started[1], gaxes[1], s["o"], "gather_wait_0_out")
            h, s["br"] = _attn_out_proj(h, s["o"], s["g"], w_out, j, mod, i, t_mm)
            weights.append((w_in_t, w_out))
        else:
            p_in_t, p_out, p_grp = wts
            s["v"], s["g"] = _pool_in_proj(h, rows, mod, i, p_in_t, j, t_mm)
            if i < DEPTH - 1:
                h, s["br"] = _pool_mix_out(h, s["v"], s["g"], p_grp, p_out, j, rows, mod, i, t_mm)
            else:
                dx, s["br"], loss_part = _pool_mix_out(h, s["v"], s["g"], p_grp, p_out, j, rows, mod, i, t_mm,
                                                       loss_target[0])
            weights.append(wts)
        saved.append(s)

    vecs, gates, gains, dsinks, pool_vecs = [None] * DEPTH, [None] * DEPTH, [None] * 2, [None] * 2, [None] * 2
    sent = {}
    token = jnp.zeros((8, LANES), F32)
    for i in reversed(range(DEPTH)):
        j = i // 2
        s = saved[i]
        if i % 2 == 0:
            w_in_t, w_out = weights[i]
            dos, dg, d_w_out, gates[i] = _attn_out_proj_bwd(dx, s["br"], s["o"], s["g"], w_out, j, mod, i, t_mm, token)
            if i == 0:
                sent["0_out"], token = _scatter_start([d_w_out], (0,), "scatter_start_0_out", token)
            dq, dk, dv, dsinks[j] = _attn_backward(attn_sinks, s["qs"], dos, s["kd"], s["vd"], j, token)
            dx, d_in_t, vecs[i], gains[j] = _attn_in_proj_bwd(
                s["x"], dx, rope, s["qk_raw"], dq, dk, dv, dg, rows, mod, i, w_in_t, j, s["gain"], bd, t_bw)
            if i > 0:
                sent[i], token = _scatter_start([d_in_t, d_w_out], (0, 0), f"scatter_start_{i}", token)
        else:
            p_in_t, p_out, p_grp = weights[i]
            dpool, dg, d_p_out, d_p_grp, pool_vecs[j] = _pool_mix_out_bwd(
                dx, s["br"], s["v"], s["g"], p_grp, p_out, j, rows, mod, i, t_mm, token)
            dx, d_in_t, vecs[i] = _pool_in_proj_bwd(s["x"], dx, dpool, dg, rows, mod, i, p_in_t, j, t_bw, token)
            sent[i], token = _scatter_start([d_in_t, d_p_out, d_p_grp], (0, 0, 1), f"scatter_start_{i}", token)

    vec = _build_vec(vecs, gates, pool_vecs, gains, dsinks, loss_part)
    vec_rows = lax.dynamic_update_slice(jnp.zeros((N_DEV * VEC_ROWS, D), F32), vec, (me * VEC_ROWS, 0))
    vec_sent, token = _gather_start([[vec_rows]], [(0,)], loss_part, "vec_gather_start")
    sent["0_in"], token = _scatter_start([d_in_t], (0,), "scatter_start_0_in", token)

    got = {}
    for i in (3, 1):
        fulls, lands = _scatter_wait(sent[i], (0, 0, 1), token, f"scatter_wait_{i}")
        got[i] = dict(zip(("in", "out", "grp"), zip(fulls, lands)))
    pick = lambda ls, kind: ([got[i][kind][0] for i in ls], [got[i][kind][1] for i in ls])
    res = {}
    res["pool_w_in"] = _adamw_shards("adamw_pool_w_in", me, *pick((1, 3), "in"), pool_w_in, m_pool_w_in, v_pool_w_in, True)
    res["pool_w_out"] = _adamw_shards("adamw_pool_w_out", me, *pick((1, 3), "out"), pool_w_out, m_pool_w_out,
                                      v_pool_w_out, False)
    res["pool_w_group"] = _adamw_shards("adamw_pool_w_group", me, *pick((1, 3), "grp"), pool_w_group, m_pool_w_group,
                                        v_pool_w_group, False, axis=1)

    vec_all, = _gather_wait(vec_sent[0], (0,), res["pool_w_group"][0], "vec_gather_wait")
    vec_all = vec_all.reshape(N_DEV, VEC_ROWS, D)
    tot, folded = _sum_devices(vec_all, token)
    loss = tot[24, 0]
    small = dict(
        ada_b=(ada_b, tot[0:12].reshape(DEPTH, 3 * D), m_ada_b, v_ada_b),
        norm_g=(norm_g, tot[12:16], m_norm_g, v_norm_g),
        q_norm=(attn_q_norm, folded[0:2, :HEAD_DIM], m_attn_q_norm, v_attn_q_norm),
        k_norm=(attn_k_norm, folded[2:4, :HEAD_DIM], m_attn_k_norm, v_attn_k_norm),
        sinks=(attn_sinks, tot[20:22, :N_HEADS], m_attn_sinks, v_attn_sinks),
        pool_scale=(pool_scale, lax.dynamic_slice(tot, (22, me * shard), (2, shard)), m_pool_scale, v_pool_scale),
    )
    res.update({k: (a[1],) + upd for (k, a), upd in zip(small.items(), _adamw_small(list(small.values())))})

    dmod_all = vec_all[:, 0:12, :].reshape(N_DEV, DEPTH, 3 * D)
    dmod_mine = lax.dynamic_slice_in_dim(dmod_all, me * cols, cols, axis=2)
    dmod_mine = jnp.pad(jnp.transpose(dmod_mine, (1, 0, 2)), ((0, 0), (0, N_DEV), (0, 0))) + token[0, 0]
    res["ada_w"] = _ada_backward_adamw(jnp.pad(c_all, ((0, N_DEV), (0, 0))), dmod_mine, ada_w, m_ada_w, v_ada_w)

    fulls, lands = _scatter_wait(sent[2], (0, 0), res["ada_w"][0], "scatter_wait_2")
    got[2] = dict(zip(("in", "out"), zip(fulls, lands)))
    got[0] = {}
    for kind in ("out", "in"):
        fulls, lands = _scatter_wait(sent["0_" + kind], (0,), res["ada_w"][0], "scatter_wait_0_" + kind)
        got[0][kind] = (fulls[0], lands[0])
    res["attn_w_out"] = _adamw_shards("adamw_attn_w_out", me, *pick((0, 2), "out"), attn_w_out, m_attn_w_out,
                                      v_attn_w_out, False)
    res["attn_w_in"] = tuple(jnp.swapaxes(a, 1, 2) for a in _adamw_shards(
        "adamw_attn_w_in", me, *pick((0, 2), "in"), w_in_rows, jnp.swapaxes(m_attn_w_in, 1, 2),
        jnp.swapaxes(v_attn_w_in, 1, 2), False))

    order = ("ada_w", "ada_b", "norm_g", "attn_w_in", "q_norm", "k_norm", "sinks", "attn_w_out", "pool_w_in",
             "pool_w_group", "pool_scale", "pool_w_out")
    return (loss, dx[None], *[res[k][0] for k in order], *[res[k][1] for k in order], *[res[k][2] for k in order],
            *[res[k][3] for k in order])
```
